```python
import math
import jax, jax.numpy as jnp
from jax import lax
import numpy as np

D_MODEL = 1024
BATCH = 2
SEQ = 8192
DEPTH = 1
DEC_BATCH = 16
DEC_SEQ = 16
PAST_LEN = 2048

CHUNK = 64
Q_BLOCK = 128
DA_HEADS = 4
DA_QK = 64
DA_V = 2 * DA_QK
ML_HEADS = 4
ML_DK = 128
ML_DV = 128
ROPE_THETA = 10000.0
N_EXPERTS = 32
TOP_K = 4
D_FF = D_MODEL
SWIGLU_LIMIT = 7.0
SWIGLU_ALPHA = 1.702
MOE_BLOCK_PROMPT = 256
MOE_BLOCK_SAMPLE = 16
EPS = 1e-6

DA_QW = DA_HEADS * 2 * DA_QK
DA_VW = DA_HEADS * DA_V
ML_QW = ML_HEADS * ML_DK
ML_VW = ML_HEADS * ML_DV
MIX_W = DA_VW + ML_VW
IN_SIZES = (DA_QW, DA_QW, DA_VW, ML_QW, ML_QW, ML_VW, ML_VW, ML_HEADS, ML_HEADS)
D_IN = sum(IN_SIZES)
SPLIT_POINTS = tuple(int(s) for s in np.cumsum(IN_SIZES)[:-1])

kernel_name = 'hybrid_diffattn_mlstm_moe_stream_step'


def rmsnorm(x, g):
    xf = x.astype(jnp.float32)
    y = xf * lax.rsqrt(jnp.mean(xf * xf, axis=-1, keepdims=True) + EPS)
    return (y * g.astype(jnp.float32)).astype(x.dtype)


def rope(x, pos):
    half = DA_QK // 2
    inv = ROPE_THETA ** (-jnp.arange(0, DA_QK, 2, dtype=jnp.float32) / DA_QK)
    ang = pos.astype(jnp.float32)[:, None] * inv[None, :]
    cos = jnp.cos(ang)[:, None, None, :]
    sin = jnp.sin(ang)[:, None, None, :]
    xf = x.astype(jnp.float32)
    x1, x2 = xf[..., :half], xf[..., half:]
    return jnp.concatenate([x1 * cos - x2 * sin, x2 * cos + x1 * sin], axis=-1).astype(x.dtype)


def project(xn, w_in, b_ig, b_fg):
    B, S, _ = xn.shape
    z = xn @ w_in
    dq, dk, dv, mq, mk, mv, mo, ig, fg = jnp.split(z, SPLIT_POINTS, axis=-1)
    dq = dq.reshape(B, S, DA_HEADS, 2, DA_QK)
    dk = dk.reshape(B, S, DA_HEADS, 2, DA_QK)
    dv = dv.reshape(B, S, DA_HEADS, DA_V)
    mq = mq.reshape(B, S, ML_HEADS, ML_DK)
    mk = mk.reshape(B, S, ML_HEADS, ML_DK) * (ML_DK ** -0.5)
    mv = mv.reshape(B, S, ML_HEADS, ML_DV)
    ig = ig.astype(jnp.float32) + b_ig.astype(jnp.float32)
    lf = jax.nn.log_sigmoid(fg.astype(jnp.float32) + b_fg.astype(jnp.float32))
    return dq, dk, dv, mq, mk, mv, mo, ig, lf


def diff_lambda(lq1, lk1, lq2, lk2, lam_init):
    f = lambda a: a.astype(jnp.float32)
    return jnp.exp(jnp.sum(f(lq1) * f(lk1))) - jnp.exp(jnp.sum(f(lq2) * f(lk2))) + lam_init


def diff_combine(s, v, lam):
    p = jax.nn.softmax(s, axis=-1)
    a = p[:, :, 0] - lam * p[:, :, 1]
    return jnp.einsum('bhqk,bkhd->bqhd', a.astype(v.dtype), v)


def diff_attn_prompt(q, k, v, lam):
    B, S = q.shape[:2]
    nb = S // Q_BLOCK
    qb = q.reshape(B, nb, Q_BLOCK, DA_HEADS, 2, DA_QK).swapaxes(0, 1)
    kpos = jnp.arange(S)
    scale = DA_QK ** -0.5

    def block(args):
        qi, i = args
        qpos = i * Q_BLOCK + jnp.arange(Q_BLOCK)
        limit = (qpos // CHUNK + 1) * CHUNK
        mask = kpos[None, :] < limit[:, None]
        s = jnp.einsum('bqhcd,bkhcd->bhcqk', qi, k).astype(jnp.float32) * scale
        s = jnp.where(mask, s, -jnp.inf)
        return diff_combine(s, v, lam)

    o = lax.map(block, (qb, jnp.arange(nb)))
    return o.swapaxes(0, 1).reshape(B, S, DA_HEADS, DA_V)


def diff_attn_sample(q, k_new, v_new, k_cache, v_cache, lam):
    kk = jnp.concatenate([k_cache.astype(k_new.dtype), k_new], axis=1)
    vv = jnp.concatenate([v_cache.astype(v_new.dtype), v_new], axis=1)
    s = jnp.einsum('bqhcd,bkhcd->bhcqk', q, kk).astype(jnp.float32) * (DA_QK ** -0.5)
    return diff_combine(s, vv, lam)


def mlstm_chunk(carry, inp):
    C, n, m = carry
    q, k, v, ig, lf = inp
    L = q.shape[2]
    F = jnp.cumsum(lf, axis=-1)
    tril = jnp.tril(jnp.ones((L, L), dtype=bool))
    a = jnp.where(tril, F[..., :, None] - F[..., None, :] + ig[..., None, :], -jnp.inf)
    b = F + m[..., None]
    m_t = jnp.maximum(b, jnp.max(a, axis=-1))
    w = jnp.exp(a - m_t[..., None])
    sc = jnp.exp(b - m_t)
    wqk = w * jnp.einsum('bhtd,bhsd->bhts', q, k)
    num = jnp.einsum('bhts,bhse->bhte', wqk, v) + sc[..., None] * jnp.einsum('bhtd,bhde->bhte', q, C)
    den = jnp.sum(wqk, axis=-1) + sc * jnp.einsum('bhtd,bhd->bht', q, n)
    h = num / jnp.maximum(jnp.abs(den), jnp.exp(-m_t))[..., None]
    FL = F[..., -1]
    g = FL[..., None] - F + ig
    m_new = jnp.maximum(FL + m, jnp.max(g, axis=-1))
    ws = jnp.exp(g - m_new[..., None])
    decay = jnp.exp(FL + m - m_new)
    C_new = decay[..., None, None] * C + jnp.einsum('bhs,bhsd,bhse->bhde', ws, k, v)
    n_new = decay[..., None] * n + jnp.einsum('bhs,bhsd->bhd', ws, k)
    return (C_new, n_new, m_new), h


def mlstm_run(q, k, v, ig, lf, C0, n0, m0, chunk):
    B, S, H, _ = q.shape
    nc = S // chunk
    f32 = jnp.float32
    seq5 = lambda t: t.astype(f32).reshape(B, nc, chunk, H, t.shape[-1]).transpose(1, 0, 3, 2, 4)
    seq4 = lambda t: t.astype(f32).reshape(B, nc, chunk, H).transpose(1, 0, 3, 2)
    carry0 = (C0.astype(f32), n0.astype(f32), m0.astype(f32))
    carry, h = lax.scan(mlstm_chunk, carry0, (seq5(q), seq5(k), seq5(v), seq4(ig), seq4(lf)))
    h = h.transpose(1, 0, 3, 2, 4).reshape(B, S, H, ML_DV).astype(q.dtype)
    return h, carry


def mix_out(da, ml, mo, subln, lam_init, w_out):
    B, S = da.shape[:2]
    da = rmsnorm(da, subln) * (1.0 - lam_init)
    ml = jax.nn.sigmoid(mo) * ml.reshape(B, S, ML_VW)
    return jnp.concatenate([da.reshape(B, S, DA_VW), ml], axis=-1) @ w_out


def moe_ffn(x2d, w_router, b_router, w_gate_up, b_gate_up, w_down, b_down, block):
    T = x2d.shape[0]
    logits = (x2d @ w_router).astype(jnp.float32) + b_router.astype(jnp.float32)
    top_val, top_idx = lax.top_k(logits, TOP_K)
    gates = jax.nn.softmax(top_val, axis=-1)
    flat_e = top_idx.reshape(-1)
    order = jnp.argsort(flat_e)
    sorted_e = flat_e[order]
    tok = order // TOP_K
    g_sorted = gates.reshape(-1)[order]
    counts = jnp.zeros((N_EXPERTS,), jnp.int32).at[flat_e].add(1)
    padded = (counts + block - 1) // block * block
    start = jnp.cumsum(counts) - counts
    pend = jnp.cumsum(padded)
    pstart = pend - padded
    rows = T * TOP_K
    dest = pstart[sorted_e] + jnp.arange(rows, dtype=jnp.int32) - start[sorted_e]
    n_blocks = -(-rows // block) + N_EXPERTS
    buf_tok = jnp.zeros((n_blocks * block,), jnp.int32).at[dest].set(tok)
    blk_e = jnp.clip(jnp.searchsorted(pend, jnp.arange(n_blocks, dtype=jnp.int32) * block, side='right'), 0, N_EXPERTS - 1)
    xb = x2d[buf_tok].reshape(n_blocks, block, x2d.shape[1])

    def expert_block(args):
        xe, e = args
        gu = xe @ w_gate_up[e] + b_gate_up[e]
        gate = jnp.minimum(gu[:, :D_FF], SWIGLU_LIMIT)
        up = jnp.clip(gu[:, D_FF:], -SWIGLU_LIMIT, SWIGLU_LIMIT)
        glu = gate * jax.nn.sigmoid(SWIGLU_ALPHA * gate)
        return ((up + 1.0) * glu) @ w_down[e] + b_down[e]

    yb = lax.map(expert_block, (xb, blk_e)).reshape(n_blocks * block, x2d.shape[1])
    return jnp.zeros_like(x2d).at[tok].add(yb[dest] * g_sorted[:, None].astype(x2d.dtype))


def setup_inputs(seed: int = 0) -> dict:
    key = jax.random.key(seed)
    ks = jax.random.split(key, 24)
    nrm = lambda k, s, sc: jax.random.normal(k, s, jnp.float32) * sc
    return {
        'x_prompt': nrm(ks[0], (BATCH, SEQ, D_MODEL), 1.0),
        'x_sample': nrm(ks[1], (DEC_BATCH, DEC_SEQ, D_MODEL), 1.0),
        'cache_k': nrm(ks[2], (DEPTH, DEC_BATCH, PAST_LEN, DA_HEADS, 2, DA_QK), 1.0),
        'cache_v': nrm(ks[3], (DEPTH, DEC_BATCH, PAST_LEN, DA_HEADS, DA_V), 1.0),
        'state_C': nrm(ks[4], (DEPTH, DEC_BATCH, ML_HEADS, ML_DK, ML_DV), 1.0),
        'state_n': nrm(ks[5], (DEPTH, DEC_BATCH, ML_HEADS, ML_DK), 1.0),
        'state_m': nrm(ks[6], (DEPTH, DEC_BATCH, ML_HEADS), 0.5),
        'norm1': 1.0 + nrm(ks[7], (DEPTH, D_MODEL), 0.01),
        'w_in': nrm(ks[8], (DEPTH, D_MODEL, D_IN), D_MODEL ** -0.5),
        'b_igate': nrm(ks[9], (DEPTH, ML_HEADS), 0.1),
        'b_fgate': 3.0 + nrm(ks[10], (DEPTH, ML_HEADS), 0.5),
        'lambda_q1': nrm(ks[11], (DEPTH, DA_QK), 0.1),
        'lambda_k1': nrm(ks[12], (DEPTH, DA_QK), 0.1),
        'lambda_q2': nrm(ks[13], (DEPTH, DA_QK), 0.1),
        'lambda_k2': nrm(ks[14], (DEPTH, DA_QK), 0.1),
        'subln': 1.0 + nrm(ks[15], (DEPTH, DA_V), 0.01),
        'w_out': nrm(ks[16], (DEPTH, MIX_W, D_MODEL), MIX_W ** -0.5),
        'norm2': 1.0 + nrm(ks[17], (DEPTH, D_MODEL), 0.01),
        'w_router': nrm(ks[18], (DEPTH, D_MODEL, N_EXPERTS), D_MODEL ** -0.5),
        'b_router': nrm(ks[19], (DEPTH, N_EXPERTS), 0.01),
        'w_gate_up': nrm(ks[20], (DEPTH, N_EXPERTS, D_MODEL, 2 * D_FF), D_MODEL ** -0.5),
        'b_gate_up': nrm(ks[21], (DEPTH, N_EXPERTS, 2 * D_FF), 0.01),
        'w_down': nrm(ks[22], (DEPTH, N_EXPERTS, D_FF, D_MODEL), D_FF ** -0.5),
        'b_down': nrm(ks[23], (DEPTH, N_EXPERTS, D_MODEL), 0.01),
        'norm_f': 1.0 + nrm(jax.random.fold_in(key, 99), (D_MODEL,), 0.01),
    }


def reference(x_prompt, x_sample, cache_k, cache_v, state_C, state_n, state_m, norm1, w_in, b_igate, b_fgate, lambda_q1, lambda_k1, lambda_q2, lambda_k2, subln, w_out, norm2, w_router, b_router, w_gate_up, b_gate_up, w_down, b_down, norm_f):
    B, S, _ = x_prompt.shape
    DB, L, _ = x_sample.shape
    P = cache_k.shape[2]
    pos_p = jnp.arange(S)
    pos_s = P + jnp.arange(L)
    xp, xs = x_prompt, x_sample
    kp_l, vp_l, Cp_l, np_l, mp_l = [], [], [], [], []
    ks_l, vs_l, Cs_l, ns_l, ms_l = [], [], [], [], []
    for l in range(DEPTH):
        lam_init = 0.8 - 0.6 * math.exp(-0.3 * l)
        lam = diff_lambda(lambda_q1[l], lambda_k1[l], lambda_q2[l], lambda_k2[l], lam_init)
        dq, dk, dv, mq, mk, mv, mo, ig, lf = project(rmsnorm(xp, norm1[l]), w_in[l], b_igate[l], b_fgate[l])
        dq, dk = rope(dq, pos_p), rope(dk, pos_p)
        da = diff_attn_prompt(dq, dk, dv, lam)
        zC = jnp.zeros((B, ML_HEADS, ML_DK, ML_DV), jnp.float32)
        zn = jnp.zeros((B, ML_HEADS, ML_DK), jnp.float32)
        zm = jnp.zeros((B, ML_HEADS), jnp.float32)
        ml, (Cp, n_p, mp) = mlstm_run(mq, mk, mv, ig, lf, zC, zn, zm, CHUNK)
        xp = xp + mix_out(da, ml, mo, subln[l], lam_init, w_out[l])
        xp = xp + moe_ffn(rmsnorm(xp, norm2[l]).reshape(B * S, D_MODEL), w_router[l], b_router[l], w_gate_up[l], b_gate_up[l], w_down[l], b_down[l], MOE_BLOCK_PROMPT).reshape(B, S, D_MODEL)
        kp_l.append(dk); vp_l.append(dv); Cp_l.append(Cp); np_l.append(n_p); mp_l.append(mp)
        sq, sk, sv, tq, tk, tv, to, tig, tlf = project(rmsnorm(xs, norm1[l]), w_in[l], b_igate[l], b_fgate[l])
        sq, sk = rope(sq, pos_s), rope(sk, pos_s)
        da_s = diff_attn_sample(sq, sk, sv, cache_k[l], cache_v[l], lam)
        ml_s, (Cs, n_s, ms) = mlstm_run(tq, tk, tv, tig, tlf, state_C[l], state_n[l], state_m[l], L)
        xs = xs + mix_out(da_s, ml_s, to, subln[l], lam_init, w_out[l])
        xs = xs + moe_ffn(rmsnorm(xs, norm2[l]).reshape(DB * L, D_MODEL), w_router[l], b_router[l], w_gate_up[l], b_gate_up[l], w_down[l], b_down[l], MOE_BLOCK_SAMPLE).reshape(DB, L, D_MODEL)
        ks_l.append(sk); vs_l.append(sv)
        Cs_l.append(Cs.astype(state_C.dtype)); ns_l.append(n_s.astype(state_n.dtype)); ms_l.append(ms.astype(state_m.dtype))
    y_prompt = rmsnorm(xp, norm_f)
    y_sample = rmsnorm(xs, norm_f)
    return (y_prompt, y_sample, jnp.stack(kp_l), jnp.stack(vp_l), jnp.stack(Cp_l), jnp.stack(np_l), jnp.stack(mp_l), jnp.stack(ks_l), jnp.stack(vs_l), jnp.stack(Cs_l), jnp.stack(ns_l), jnp.stack(ms_l))
```

```python
import functools
import math

import jax
import jax.numpy as jnp
import numpy as np
from jax import lax
from jax.experimental import pallas as pl
from jax.experimental.pallas import tpu as pltpu

D_MODEL = 1024
CHUNK = 64
DA_HEADS = 4
DA_QK = 64
DA_V = 128
ML_HEADS = 4
ML_DK = 128
ML_DV = 128
HEAD_W = 128
GROUP_W = 512
ROPE_THETA = 10000.0
N_EXPERTS = 32
TOP_K = 4
D_FF = 1024
SWIGLU_LIMIT = 7.0
SWIGLU_ALPHA = 1.702
EPS = 1e-6
NEG_BIG = -1e30

PROJ_TILE = 512
ATTN_TQ = 256
MOE_BLOCK = 256
FF_CHUNK = 512
VMEM_LIMIT = 56 * 1024 * 1024

_F32 = jnp.float32
_BF16 = jnp.bfloat16


def _cparams(sem):
    return pltpu.CompilerParams(dimension_semantics=sem, vmem_limit_bytes=VMEM_LIMIT)


def _dot(a, b, dims=(((1,), (0,)), ((), ())), precision=None):
    return lax.dot_general(a, b, dims, precision=precision, preferred_element_type=_F32)


_NT = (((1,), (1,)), ((), ()))
_TN = (((0,), (0,)), ((), ()))


def _proj_kernel(x_ref, g_ref, w_ref, bg_ref, inv_ref,
                 q_ref, kf_ref, kb_ref, vf_ref, vt_ref, mq_ref, mk_ref, mv_ref, mo_ref, gc_ref, gr_ref,
                 *, tm, pos_base, pos_mod):
    x = x_ref[0]
    xn = (x * lax.rsqrt(jnp.mean(x * x, axis=-1, keepdims=True) + EPS) * g_ref[...]).astype(_BF16)

    row = pl.program_id(1) * tm + lax.broadcasted_iota(jnp.int32, (tm, 1), 0)
    pos = (pos_base + row % pos_mod).astype(_F32)
    ang = pos * inv_ref[...]
    cos = jnp.cos(ang)
    sin = jnp.sin(ang)
    lane = lax.broadcasted_iota(jnp.int32, (1, HEAD_W), 1)
    first = (lane % DA_QK) < (DA_QK // 2)
    sin = jnp.where(first, -sin, sin)

    def rope(z):
        rot = jnp.where(first, pltpu.roll(z, HEAD_W - DA_QK // 2, 1), pltpu.roll(z, DA_QK // 2, 1))
        return z * cos + rot * sin

    def group(c, width=GROUP_W):
        return _dot(xn, w_ref[:, c * GROUP_W:c * GROUP_W + width])

    zq = group(0)
    for h in range(DA_HEADS):
        sl = slice(h * HEAD_W, (h + 1) * HEAD_W)
        q_ref[0, :, sl] = (rope(zq[:, sl]) * (DA_QK ** -0.5)).astype(_BF16)
    zk = group(1)
    for h in range(DA_HEADS):
        sl = slice(h * HEAD_W, (h + 1) * HEAD_W)
        rk = rope(zk[:, sl])
        kf_ref[0, :, sl] = rk
        kb_ref[0, :, sl] = rk.astype(_BF16)
    zv = group(2)
    vf_ref[0] = zv
    vt_ref[0, 0] = zv.T.astype(_BF16)
    mq_ref[0] = group(3).astype(_BF16)
    mk_ref[0] = (group(4) * (ML_DK ** -0.5)).astype(_BF16)
    mv_ref[0] = group(5).astype(_BF16)
    mo_ref[0] = group(6)
    zg = group(7, HEAD_W) + bg_ref[...]
    lane_g = lax.broadcasted_iota(jnp.int32, zg.shape, 1)
    logsig = jnp.minimum(zg, 0.0) - jnp.log(1.0 + jnp.exp(-jnp.abs(zg)))
    gates = jnp.where(lane_g < ML_HEADS, zg, logsig)
    gc_ref[0] = gates[:, :8]
    gr_ref[0] = gates.T[:8, :]


def _proj(x3, norm1, w_nat, b_gates, inv_full, *, tm, pos_base, pos_mod):
    nb, s, _ = x3.shape
    nt = s // tm
    wcols = w_nat.shape[1]
    tok = lambda w, dt: jax.ShapeDtypeStruct((nb, s, w), dt)
    tok_spec = lambda w: pl.BlockSpec((1, tm, w), lambda b, i: (b, i, 0))
    const = lambda shape: pl.BlockSpec(shape, lambda b, i: (0,) * len(shape))
    out_shape = [tok(GROUP_W, _BF16), tok(GROUP_W, _F32), tok(GROUP_W, _BF16), tok(GROUP_W, _F32),
                 jax.ShapeDtypeStruct((nb, nt, GROUP_W, tm), _BF16),
                 tok(GROUP_W, _BF16), tok(GROUP_W, _BF16), tok(GROUP_W, _BF16), tok(GROUP_W, _F32),
                 tok(8, _F32), jax.ShapeDtypeStruct((nb, 8, s), _F32)]
    out_specs = [tok_spec(GROUP_W)] * 4 + [pl.BlockSpec((1, 1, GROUP_W, tm), lambda b, i: (b, i, 0, 0))] \
        + [tok_spec(GROUP_W)] * 4 + [tok_spec(8), pl.BlockSpec((1, 8, tm), lambda b, i: (b, 0, i))]
    return pl.pallas_call(
        functools.partial(_proj_kernel, tm=tm, pos_base=pos_base, pos_mod=pos_mod),
        grid=(nb, nt),
        in_specs=[tok_spec(D_MODEL), const((1, D_MODEL)), const((D_MODEL, wcols)), const((1, HEAD_W)),
                  const((1, HEAD_W))],
        out_specs=out_specs, out_shape=out_shape,
        compiler_params=_cparams(("parallel", "parallel")), name="proj",
    )(x3, norm1, w_nat, b_gates, inv_full)


def _lambda(lam_ref, lam_init):
    lv = lam_ref[...]
    s1 = jnp.sum(lv[0:1] * lv[1:2], axis=-1, keepdims=True)
    s2 = jnp.sum(lv[2:3] * lv[3:4], axis=-1, keepdims=True)
    return jnp.exp(s1) - jnp.exp(s2) + lam_init


def _split_components(q):
    lane = lax.broadcasted_iota(jnp.int32, q.shape, 1)
    zero = jnp.zeros_like(q)
    return jnp.concatenate([jnp.where(lane < DA_QK, q, zero), jnp.where(lane >= DA_QK, q, zero)], axis=0)


def _attn_kernel(lam_ref, q_ref, k_ref, vt_ref, o_ref, acc_ref, *, tq, tk, lam_init):
    i = pl.program_id(2)
    qz = _split_components(q_ref[0])
    acc_ref[...] = jnp.zeros_like(acc_ref)

    def step(j, carry, masked):
        m, l = carry
        k_t = k_ref[0, pl.ds(pl.multiple_of(j * tk, tk), tk), :]
        s = _dot(k_t, qz, _NT)
        if masked:
            kpos = j * tk + lax.broadcasted_iota(jnp.int32, (tk, 1), 0)
            qpos = i * tq + lax.broadcasted_iota(jnp.int32, (1, 2 * tq), 1) % tq
            s = jnp.where(kpos < (qpos // CHUNK + 1) * CHUNK, s, NEG_BIG)
        m_new = jnp.maximum(m, jnp.max(s, axis=0, keepdims=True))
        alpha = jnp.exp(m - m_new)
        p = jnp.exp(s - m_new)
        l_new = alpha * l + jnp.sum(p, axis=0, keepdims=True)
        acc_ref[...] = alpha * acc_ref[...] + _dot(vt_ref[0, j], p.astype(_BF16))
        return m_new, l_new

    n_full = (i * tq) // tk
    init = (jnp.full((1, 2 * tq), NEG_BIG, _F32), jnp.zeros((1, 2 * tq), _F32))
    carry = lax.fori_loop(0, n_full, lambda j, c: step(j, c, False), init)
    _, l = step(n_full, carry, True)

    o = acc_ref[...] / l
    lam = _lambda(lam_ref, lam_init)
    o_ref[0] = (o[:, :tq] - lam * o[:, tq:]).T


def _attn_prompt(lamv, q, kb, vt, *, lam_init):
    nb, s, _ = q.shape
    nk, tk = vt.shape[1], vt.shape[3]
    tq = ATTN_TQ
    return pl.pallas_call(
        functools.partial(_attn_kernel, tq=tq, tk=tk, lam_init=lam_init),
        grid=(nb, DA_HEADS, s // tq),
        in_specs=[pl.BlockSpec((4, DA_QK), lambda b, h, i: (0, 0)),
                  pl.BlockSpec((1, tq, HEAD_W), lambda b, h, i: (b, i, h)),
                  pl.BlockSpec((1, s, HEAD_W), lambda b, h, i: (b, 0, h)),
                  pl.BlockSpec((1, nk, HEAD_W, tk), lambda b, h, i: (b, 0, h, 0))],
        out_specs=pl.BlockSpec((1, tq, HEAD_W), lambda b, h, i: (b, i, h)),
        out_shape=jax.ShapeDtypeStruct((nb, s, GROUP_W), _F32),
        scratch_shapes=[pltpu.VMEM((HEAD_W, 2 * tq), _F32)],
        compiler_params=_cparams(("parallel", "parallel", "arbitrary")), name="attn",
    )(lamv, q, kb, vt)


def _attn_sample_kernel(lam_ref, q_ref, kn_ref, vn_ref, kc_ref, vc_ref, o_ref, *, lq, lam_init):
    lam = _lambda(lam_ref, lam_init)
    for h in range(DA_HEADS):
        sl = slice(h * HEAD_W, (h + 1) * HEAD_W)
        qz = _split_components(q_ref[0, :, sl])
        kc = kc_ref[0, :, sl].astype(_BF16)
        vc = vc_ref[0, :, sl].astype(_BF16)
        s_c = _dot(qz, kc, _NT)
        s_n = _dot(qz, kn_ref[0, :, sl], _NT)
        m = jnp.maximum(jnp.max(s_c, axis=-1, keepdims=True), jnp.max(s_n, axis=-1, keepdims=True))
        p_c = jnp.exp(s_c - m)
        p_n = jnp.exp(s_n - m)
        l = jnp.sum(p_c, axis=-1, keepdims=True) + jnp.sum(p_n, axis=-1, keepdims=True)
        o = (_dot(p_c.astype(_BF16), vc) + _dot(p_n.astype(_BF16), vn_ref[0, :, sl])) / l
        o_ref[0, :, sl] = o[:lq] - lam * o[lq:]


def _attn_sample(lamv, q, kn, vn, kc, vc, *, lam_init):
    nb, lq, _ = q.shape
    past = kc.shape[1]
    new = pl.BlockSpec((1, lq, GROUP_W), lambda b: (b, 0, 0))
    old = pl.BlockSpec((1, past, GROUP_W), lambda b: (b, 0, 0))
    return pl.pallas_call(
        functools.partial(_attn_sample_kernel, lq=lq, lam_init=lam_init),
        grid=(nb,),
        in_specs=[pl.BlockSpec((4, DA_QK), lambda b: (0, 0)), new, new, new, old, old],
        out_specs=new, out_shape=jax.ShapeDtypeStruct((nb, lq, GROUP_W), _F32),
        compiler_params=_cparams(("parallel",)), name="attn_sample",
    )(lamv, q, kn, vn, kc, vc)


def _mlstm_kernel(q_ref, k_ref, v_ref, gc_ref, gr_ref, cn0_ref, m0_ref, h_ref, cn_ref, m_ref,
                  cn_sc, m_sc, *, chunk, nchunks):
    j = pl.program_id(1)

    @pl.when(j == 0)
    def _():
        cn_sc[...] = cn0_ref[0]
        m_sc[...] = m0_ref[0]

    L = chunk
    r_i = lax.broadcasted_iota(jnp.int32, (L, L), 0)
    c_i = lax.broadcasted_iota(jnp.int32, (L, L), 1)
    causal = r_i >= c_i
    tril = causal.astype(_F32)
    triu = (r_i <= c_i).astype(_F32)
    ones_col = (lax.broadcasted_iota(jnp.int32, (L, ML_DV), 1) == 0).astype(_BF16)
    hi = lax.Precision.HIGHEST

    for c in range(nchunks):
        rows = slice(c * L, (c + 1) * L)
        gcol = gc_ref[0, rows, :]
        grow = gr_ref[0, :, rows]
        fc_all = _dot(tril, gcol, precision=hi)
        fr_all = _dot(grow, triu, precision=hi)
        for hh in range(ML_HEADS):
            sl = slice(hh * HEAD_W, (hh + 1) * HEAD_W)
            m = m_sc[hh][0:1, 0:1]
            fc = fc_all[:, ML_HEADS + hh:ML_HEADS + hh + 1]
            fr = fr_all[ML_HEADS + hh:ML_HEADS + hh + 1, :]
            igc = gcol[:, hh:hh + 1]
            igr = grow[hh:hh + 1, :]
            a = jnp.where(causal, fc - fr + igr, -jnp.inf)
            b = fc + m
            m_t = jnp.maximum(b, jnp.max(a, axis=-1, keepdims=True))
            w = jnp.exp(a - m_t)
            sc = jnp.exp(b - m_t)
            q = q_ref[0, rows, sl]
            k = k_ref[0, rows, sl]
            v = v_ref[0, rows, sl]
            wqk = w * _dot(q, k, _NT)
            cn = cn_sc[hh]
            inter = _dot(q, cn.astype(_BF16))
            num = _dot(wqk.astype(_BF16), v) + sc * inter[:, :ML_DV]
            den = jnp.sum(wqk, axis=-1, keepdims=True) + sc * inter[:, ML_DV:ML_DV + 1]
            h_ref[0, rows, sl] = num / jnp.maximum(jnp.abs(den), jnp.exp(-m_t))

            fl = fc[L - 1:L, :]
            g = fl - fc + igc
            m_new = jnp.maximum(fl + m, jnp.max(g, axis=0, keepdims=True))
            ws = jnp.exp(g - m_new)
            decay = jnp.exp(fl + m - m_new)
            kw = (k.astype(_F32) * ws).astype(_BF16)
            v1 = jnp.concatenate([v, ones_col], axis=1)
            cn_sc[hh] = decay * cn + _dot(kw, v1, _TN)
            m_sc[hh] = jnp.broadcast_to(m_new, m_sc.shape[1:])

    @pl.when(j == pl.num_programs(1) - 1)
    def _():
        cn_ref[0] = cn_sc[...]
        m_ref[0] = m_sc[...]


def _mlstm(q, k, v, gc, gr, cn0, m0, *, chunk, block):
    nb, s, _ = q.shape
    tok = lambda w: pl.BlockSpec((1, block, w), lambda b, j: (b, j, 0))
    state = pl.BlockSpec((1, ML_HEADS, ML_DK, 2 * ML_DV), lambda b, j: (b, 0, 0, 0))
    mspec = pl.BlockSpec((1, ML_HEADS, 8, HEAD_W), lambda b, j: (b, 0, 0, 0))
    return pl.pallas_call(
        functools.partial(_mlstm_kernel, chunk=chunk, nchunks=block // chunk),
        grid=(nb, s // block),
        in_specs=[tok(GROUP_W), tok(GROUP_W), tok(GROUP_W), tok(8),
                  pl.BlockSpec((1, 8, block), lambda b, j: (b, 0, j)), state, mspec],
        out_specs=[tok(GROUP_W), state, mspec],
        out_shape=[jax.ShapeDtypeStruct((nb, s, GROUP_W), _F32),
                   jax.ShapeDtypeStruct(cn0.shape, _F32), jax.ShapeDtypeStruct(m0.shape, _F32)],
        scratch_shapes=[pltpu.VMEM((ML_HEADS, ML_DK, 2 * ML_DV), _F32), pltpu.VMEM((ML_HEADS, 8, HEAD_W), _F32)],
        compiler_params=_cparams(("parallel", "arbitrary")), name="mlstm",
    )(q, k, v, gc, gr, cn0, m0)


def _mix_kernel(da_ref, ml_ref, mo_ref, x_ref, subln_ref, wo_ref, g2_ref, wr_ref, br_ref, cnt0_ref,
                xmid_ref, xn_ref, idx_ref, gate_ref, rank_ref, cnt_ref, cnt_sc, *, tm, lam_init):
    step = pl.program_id(0)

    @pl.when(step == 0)
    def _():
        cnt_sc[...] = cnt0_ref[...]

    parts = []
    for h in range(DA_HEADS):
        d = da_ref[:, h * HEAD_W:(h + 1) * HEAD_W]
        d = d * lax.rsqrt(jnp.mean(d * d, axis=-1, keepdims=True) + EPS) * subln_ref[...]
        parts.append((d * (1.0 - lam_init)).astype(_BF16))
    parts.append((jax.nn.sigmoid(mo_ref[...]) * ml_ref[...]).astype(_BF16))
    xm = x_ref[...] + _dot(jnp.concatenate(parts, axis=1), wo_ref[...])
    xmid_ref[...] = xm
    xn = (xm * lax.rsqrt(jnp.mean(xm * xm, axis=-1, keepdims=True) + EPS) * g2_ref[...]).astype(_BF16)
    xn_ref[...] = xn

    logits = _dot(wr_ref[...], xn, _NT) + br_ref[...]
    e_i = lax.broadcasted_iota(jnp.int32, logits.shape, 0)
    member = jnp.zeros(logits.shape, jnp.bool_)
    vals, idxs = [], []
    for _ in range(TOP_K):
        mx = jnp.max(logits, axis=0, keepdims=True)
        sel = jnp.min(jnp.where(logits == mx, e_i, N_EXPERTS), axis=0, keepdims=True)
        hit = e_i == sel
        member = member | hit
        logits = jnp.where(hit, -jnp.inf, logits)
        vals.append(mx)
        idxs.append(sel)
    ex = [jnp.exp(v - vals[0]) for v in vals]
    tot = ex[0] + ex[1] + ex[2] + ex[3]
    idx_ref[...] = jnp.concatenate(idxs, axis=0)
    gate_ref[...] = jnp.concatenate([e / tot for e in ex], axis=0)

    upper = (lax.broadcasted_iota(jnp.int32, (tm, tm), 0) < lax.broadcasted_iota(jnp.int32, (tm, tm), 1))
    memf = member.astype(_F32)
    before = _dot(memf.astype(_BF16), upper.astype(_BF16)) + cnt_sc[...]
    rank_ref[...] = jnp.concatenate(
        [jnp.sum(jnp.where(e_i == s, before, 0.0), axis=0, keepdims=True) for s in idxs], axis=0).astype(jnp.int32)
    cnt_sc[...] += jnp.sum(memf, axis=1, keepdims=True)
    cnt_ref[...] = cnt_sc[...]


def _mix(da, ml, mo, x, subln, w_out, norm2, w_rt, b_r, cnt0, *, tm, lam_init):
    t = x.shape[0]
    tok = lambda w: pl.BlockSpec((tm, w), lambda i: (i, 0))
    const = lambda shape: pl.BlockSpec(shape, lambda i: (0,) * len(shape))
    lane_tok = pl.BlockSpec((TOP_K, tm), lambda i: (0, i))
    return pl.pallas_call(
        functools.partial(_mix_kernel, tm=tm, lam_init=lam_init),
        grid=(t // tm,),
        in_specs=[tok(GROUP_W), tok(GROUP_W), tok(GROUP_W), tok(D_MODEL), const((1, DA_V)),
                  const((D_MODEL, D_MODEL)), const((1, D_MODEL)), const((N_EXPERTS, D_MODEL)),
                  const((N_EXPERTS, 1)), const((N_EXPERTS, 1))],
        out_specs=[tok(D_MODEL), tok(D_MODEL), lane_tok, lane_tok, lane_tok, const((N_EXPERTS, 1))],
        out_shape=[jax.ShapeDtypeStruct((t, D_MODEL), _F32), jax.ShapeDtypeStruct((t, D_MODEL), _BF16),
                   jax.ShapeDtypeStruct((TOP_K, t), jnp.int32), jax.ShapeDtypeStruct((TOP_K, t), _F32),
                   jax.ShapeDtypeStruct((TOP_K, t), jnp.int32), jax.ShapeDtypeStruct((N_EXPERTS, 1), _F32)],
        scratch_shapes=[pltpu.VMEM((N_EXPERTS, 1), _F32)],
        compiler_params=_cparams(("arbitrary",)), name="mix",
    )(da, ml, mo, x, subln, w_out, norm2, w_rt, b_r, cnt0)


def _moe_kernel(be_ref, nu_ref, x_ref, wgu_ref, bgu_ref, wd_ref, bd_ref, y_ref, wgu_sc, wd_sc):
    j = pl.program_id(0)
    e_prev = be_ref[jnp.maximum(j - 1, 0)]
    fresh = (j == 0) | (be_ref[j] != e_prev)

    @pl.when(fresh & (j < nu_ref[0]))
    def _():
        rows = 128
        def cast_gu(r, _):
            sl = pl.ds(pl.multiple_of(r * rows, rows), rows)
            wgu_sc[sl, :] = wgu_ref[0, sl, :].astype(_BF16)
            return 0
        lax.fori_loop(0, D_MODEL // rows, cast_gu, 0)
        def cast_d(r, _):
            sl = pl.ds(pl.multiple_of(r * rows, rows), rows)
            wd_sc[sl, :] = wd_ref[0, sl, :].astype(_BF16)
            return 0
        lax.fori_loop(0, D_FF // rows, cast_d, 0)

    @pl.when(j < nu_ref[0])
    def _():
        x = x_ref[...]
        acc = jnp.zeros(y_ref.shape, _F32)
        for c in range(D_FF // FF_CHUNK):
            lo = c * FF_CHUNK
            gate = _dot(x, wgu_sc[:, lo:lo + FF_CHUNK]) + bgu_ref[0, :, lo:lo + FF_CHUNK]
            up = _dot(x, wgu_sc[:, D_FF + lo:D_FF + lo + FF_CHUNK]) + bgu_ref[0, :, D_FF + lo:D_FF + lo + FF_CHUNK]
            gate = jnp.minimum(gate, SWIGLU_LIMIT)
            up = jnp.clip(up, -SWIGLU_LIMIT, SWIGLU_LIMIT)
            act = (up + 1.0) * (gate * jax.nn.sigmoid(SWIGLU_ALPHA * gate))
            acc = acc + _dot(act.astype(_BF16), wd_sc[lo:lo + FF_CHUNK, :])
        y_ref[...] = acc + bd_ref[0]


def _moe(blk_e, n_used, xb, w_gu, b_gu, w_d, b_d):
    n_blocks = xb.shape[0] // MOE_BLOCK
    last = lambda j, nu: jnp.minimum(j, nu[0] - 1)
    row_spec = pl.BlockSpec((MOE_BLOCK, D_MODEL), lambda j, be, nu: (last(j, nu), 0))
    exp_spec = lambda r, c: pl.BlockSpec((1, r, c), lambda j, be, nu: (be[last(j, nu)], 0, 0))
    grid_spec = pltpu.PrefetchScalarGridSpec(
        num_scalar_prefetch=2, grid=(n_blocks,),
        in_specs=[row_spec, exp_spec(D_MODEL, 2 * D_FF), exp_spec(1, 2 * D_FF), exp_spec(D_FF, D_MODEL),
                  exp_spec(1, D_MODEL)],
        out_specs=row_spec,
        scratch_shapes=[pltpu.VMEM((D_MODEL, 2 * D_FF), _BF16), pltpu.VMEM((D_FF, D_MODEL), _BF16)])
    return pl.pallas_call(
        _moe_kernel, grid_spec=grid_spec, out_shape=jax.ShapeDtypeStruct(xb.shape, _F32),
        compiler_params=_cparams(("arbitrary",)), name="moe",
    )(blk_e, n_used, xb, w_gu, b_gu, w_d, b_d)


def _final_kernel(x_ref, y_ref, g_ref, nf_ref, o_ref):
    g = g_ref[...]
    x = x_ref[...]
    for k in range(TOP_K):
        x = x + g[:, k:k + 1] * y_ref[k]
    o_ref[...] = x * lax.rsqrt(jnp.mean(x * x, axis=-1, keepdims=True) + EPS) * nf_ref[...]


def _final(xmid, yg, gates_t, norm_f, *, tm):
    t = xmid.shape[0]
    return pl.pallas_call(
        _final_kernel, grid=(t // tm,),
        in_specs=[pl.BlockSpec((tm, D_MODEL), lambda i: (i, 0)),
                  pl.BlockSpec((TOP_K, tm, D_MODEL), lambda i: (0, i, 0)),
                  pl.BlockSpec((tm, TOP_K), lambda i: (i, 0)),
                  pl.BlockSpec((1, D_MODEL), lambda i: (0, 0))],
        out_specs=pl.BlockSpec((tm, D_MODEL), lambda i: (i, 0)),
        out_shape=jax.ShapeDtypeStruct((t, D_MODEL), _F32),
        compiler_params=_cparams(("parallel",)), name="final",
    )(xmid, yg, gates_t, norm_f)


def _pack_state(c, n, m):
    nb = c.shape[0]
    ncol = jnp.zeros((nb, ML_HEADS, ML_DK, ML_DV), _F32).at[..., 0].set(n.astype(_F32))
    cn = jnp.concatenate([c.astype(_F32), ncol], axis=-1)
    return cn, jnp.broadcast_to(m.astype(_F32)[:, :, None, None], (nb, ML_HEADS, 8, HEAD_W))


def _unpack_state(cn, m):
    return cn[..., :ML_DV], cn[..., ML_DV], m[:, :, 0, 0]


def kernel(x_prompt, x_sample, cache_k, cache_v, state_C, state_n, state_m, norm1, w_in, b_igate, b_fgate,
           lambda_q1, lambda_k1, lambda_q2, lambda_k2, subln, w_out, norm2, w_router, b_router, w_gate_up,
           b_gate_up, w_down, b_down, norm_f):
    nb, s, _ = x_prompt.shape
    db, ls, _ = x_sample.shape
    past = cache_k.shape[2]
    depth = w_in.shape[0]
    assert depth == 1 and s % PROJ_TILE == 0 and (db * ls) % 8 == 0
    l = 0
    lam_init = 0.8 - 0.6 * math.exp(-0.3 * l)
    t_p, t_s = nb * s, db * ls

    w = w_in[l]
    gate_cols = jnp.pad(w[:, 7 * GROUP_W:], ((0, 0), (0, HEAD_W - 2 * ML_HEADS)))
    w_nat = jnp.concatenate([w[:, :7 * GROUP_W], gate_cols], axis=1).astype(_BF16)
    b_gates = jnp.pad(jnp.concatenate([b_igate[l], b_fgate[l]]).astype(_F32), (0, HEAD_W - 2 * ML_HEADS))[None]
    inv = ROPE_THETA ** (-jnp.arange(0, DA_QK, 2, dtype=_F32) / DA_QK)
    inv_full = jnp.tile(inv, HEAD_W // (DA_QK // 2))[None]
    lamv = jnp.stack([lambda_q1[l], lambda_k1[l], lambda_q2[l], lambda_k2[l]]).astype(_F32)
    g1 = norm1[l][None]

    (q_p, kf_p, kb_p, vf_p, vt_p, mq_p, mk_p, mv_p, mo_p, gc_p, gr_p) = _proj(
        x_prompt, g1, w_nat, b_gates, inv_full, tm=PROJ_TILE, pos_base=0, pos_mod=s)
    (q_s, kf_s, kb_s, vf_s, _, mq_s, mk_s, mv_s, mo_s, gc_s, gr_s) = _proj(
        x_sample.reshape(1, t_s, D_MODEL), g1, w_nat, b_gates, inv_full, tm=t_s, pos_base=past, pos_mod=ls)

    da_p = _attn_prompt(lamv, q_p, kb_p, vt_p, lam_init=lam_init)
    per_b = lambda a: a.reshape(db, ls, a.shape[-1])
    da_s = _attn_sample(lamv, per_b(q_s), per_b(kb_s), per_b(vf_s.astype(_BF16)),
                        cache_k[l].reshape(db, past, GROUP_W), cache_v[l].reshape(db, past, GROUP_W),
                        lam_init=lam_init)

    zeros_state = _pack_state(jnp.zeros((nb, ML_HEADS, ML_DK, ML_DV), _F32), jnp.zeros((nb, ML_HEADS, ML_DK), _F32),
                              jnp.zeros((nb, ML_HEADS), _F32))
    h_p, cn_p, m_p = _mlstm(mq_p, mk_p, mv_p, gc_p, gr_p, *zeros_state, chunk=CHUNK, block=PROJ_TILE)
    gr_sb = gr_s.reshape(8, db, ls).transpose(1, 0, 2)
    h_s, cn_s, m_s = _mlstm(per_b(mq_s), per_b(mk_s), per_b(mv_s), per_b(gc_s), gr_sb,
                            *_pack_state(state_C[l], state_n[l], state_m[l]), chunk=ls, block=ls)
    c_p, n_p, mm_p = _unpack_state(cn_p, m_p)
    c_s, n_s, mm_s = _unpack_state(cn_s, m_s)

    wo = w_out[l].astype(_BF16)
    w_rt = w_router[l].T.astype(_BF16)
    b_r = b_router[l].astype(_F32)[:, None]
    mix = functools.partial(_mix, subln=subln[l][None], w_out=wo, norm2=norm2[l][None], w_rt=w_rt, b_r=b_r,
                            lam_init=lam_init)
    flat = lambda a: a.reshape(-1, a.shape[-1])
    xm_p, xn_p, idx_p, gate_p, rank_p, cnt_p = mix(
        flat(da_p), flat(h_p), flat(mo_p), flat(x_prompt), cnt0=jnp.zeros((N_EXPERTS, 1), _F32), tm=PROJ_TILE)
    xm_s, xn_s, idx_s, gate_s, rank_s, cnt = mix(
        flat(da_s), flat(h_s), flat(mo_s), flat(x_sample), cnt0=cnt_p, tm=t_s)

    t_all = t_p + t_s
    rows = t_all * TOP_K
    n_blocks = -(-rows // MOE_BLOCK) + N_EXPERTS
    counts = cnt[:, 0].astype(jnp.int32)
    padded = (counts + MOE_BLOCK - 1) // MOE_BLOCK * MOE_BLOCK
    pend = jnp.cumsum(padded)
    pstart = pend - padded
    dest_p = pstart[idx_p] + rank_p
    dest_s = pstart[idx_s] + rank_s
    dest = jnp.concatenate([dest_p, dest_s], axis=1)
    n_used = (pend[-1] // MOE_BLOCK).astype(jnp.int32)[None]
    blk_e = jnp.clip(jnp.searchsorted(pend, jnp.arange(n_blocks, dtype=jnp.int32) * MOE_BLOCK, side='right'),
                     0, N_EXPERTS - 1).astype(jnp.int32)

    xn_all = jnp.concatenate([xn_p, xn_s], axis=0)
    tok_ids = jnp.broadcast_to(jnp.arange(t_all, dtype=jnp.int32)[None], (TOP_K, t_all))
    buf_tok = jnp.zeros((n_blocks * MOE_BLOCK,), jnp.int32).at[dest.reshape(-1)].set(tok_ids.reshape(-1))
    xb = xn_all[buf_tok]
    yb = _moe(blk_e, n_used, xb, w_gate_up[l], b_gate_up[l][:, None, :], w_down[l], b_down[l][:, None, :])
    nf = norm_f[None]
    y_p = _final(xm_p, yb[dest_p], gate_p.T, nf, tm=PROJ_TILE).reshape(nb, s, D_MODEL)
    y_s = _final(xm_s, yb[dest_s], gate_s.T, nf, tm=t_s).reshape(db, ls, D_MODEL)

    st = lambda a: a[None]
    return (y_p, y_s,
            st(kf_p.reshape(nb, s, DA_HEADS, 2, DA_QK)), st(vf_p.reshape(nb, s, DA_HEADS, DA_V)),
            st(c_p), st(n_p), st(mm_p),
            st(kf_s.reshape(db, ls, DA_HEADS, 2, DA_QK)), st(vf_s.reshape(db, ls, DA_HEADS, DA_V)),
            st(c_s.astype(state_C.dtype)), st(n_s.astype(state_n.dtype)), st(mm_s.astype(state_m.dtype)))
```

```python
import functools
import math

import jax
import jax.numpy as jnp
import numpy as np
from jax import lax
from jax.experimental import pallas as pl
from jax.experimental.pallas import tpu as pltpu

D_MODEL = 1024
CHUNK = 64
DA_HEADS = 4
DA_QK = 64
DA_V = 128
ML_HEADS = 4
ML_DK = 128
ML_DV = 128
HEAD_W = 128
GROUP_W = 512
ROPE_THETA = 10000.0
N_EXPERTS = 32
TOP_K = 4
D_FF = 1024
SWIGLU_LIMIT = 7.0
SWIGLU_ALPHA = 1.702
EPS = 1e-6
NEG_BIG = -1e30

LOG2E = math.log2(math.e)
VT_ROWS = 144
PROJ_TILE = 512
ATTN_TQ = 256
MOE_BLOCK = 256
FF_CHUNK = 512
VMEM_LIMIT = 56 * 1024 * 1024

_F32 = jnp.float32
_BF16 = jnp.bfloat16


def _cparams(sem):
    return pltpu.CompilerParams(dimension_semantics=sem, vmem_limit_bytes=VMEM_LIMIT)


def _dot(a, b, dims=(((1,), (0,)), ((), ())), precision=None):
    return lax.dot_general(a, b, dims, precision=precision, preferred_element_type=_F32)


_NT = (((1,), (1,)), ((), ()))
_TN = (((0,), (0,)), ((), ()))


def _proj_kernel(x_ref, g_ref, w_ref, bg_ref, inv_ref,
                 q_ref, kf_ref, kb_ref, vf_ref, vt_ref, mq_ref, mk_ref, mv_ref, mo_ref, gc_ref, gr_ref,
                 *, tm, pos_base, pos_mod):
    x = x_ref[0]
    xn = (x * lax.rsqrt(jnp.mean(x * x, axis=-1, keepdims=True) + EPS) * g_ref[...]).astype(_BF16)

    row = pl.program_id(1) * tm + lax.broadcasted_iota(jnp.int32, (tm, 1), 0)
    pos = (pos_base + row % pos_mod).astype(_F32)
    ang = pos * inv_ref[...]
    cos = jnp.cos(ang)
    sin = jnp.sin(ang)
    lane = lax.broadcasted_iota(jnp.int32, (1, HEAD_W), 1)
    first = (lane % DA_QK) < (DA_QK // 2)
    sin = jnp.where(first, -sin, sin)

    def rope(z):
        rot = jnp.where(first, pltpu.roll(z, HEAD_W - DA_QK // 2, 1), pltpu.roll(z, DA_QK // 2, 1))
        return z * cos + rot * sin

    def group(c, width=GROUP_W):
        return _dot(xn, w_ref[:, c * GROUP_W:c * GROUP_W + width])

    zq = group(0)
    for h in range(DA_HEADS):
        sl = slice(h * HEAD_W, (h + 1) * HEAD_W)
        q_ref[0, :, sl] = (rope(zq[:, sl]) * (DA_QK ** -0.5 * LOG2E)).astype(_BF16)
    zk = group(1)
    for h in range(DA_HEADS):
        sl = slice(h * HEAD_W, (h + 1) * HEAD_W)
        rk = rope(zk[:, sl])
        kf_ref[0, :, sl] = rk
        kb_ref[0, :, sl] = rk.astype(_BF16)
    zv = group(2)
    vf_ref[0] = zv
    zvt = zv.T.astype(_BF16)
    ones = jnp.ones((VT_ROWS - DA_V, tm), _BF16)
    vt_ref[0, 0] = jnp.concatenate(
        [part for h in range(DA_HEADS) for part in (zvt[h * DA_V:(h + 1) * DA_V], ones)], axis=0)
    mq_ref[0] = group(3).astype(_BF16)
    mk_ref[0] = (group(4) * (ML_DK ** -0.5)).astype(_BF16)
    mv_ref[0] = group(5).astype(_BF16)
    mo_ref[0] = group(6)
    zg = group(7, HEAD_W) + bg_ref[...]
    lane_g = lax.broadcasted_iota(jnp.int32, zg.shape, 1)
    logsig = jnp.minimum(zg, 0.0) - jnp.log(1.0 + jnp.exp(-jnp.abs(zg)))
    gates = jnp.where(lane_g < ML_HEADS, zg, logsig)
    gc_ref[0] = gates[:, :8]
    gr_ref[0] = gates.T[:8, :]


def _proj(x3, norm1, w_nat, b_gates, inv_full, *, tm, pos_base, pos_mod):
    nb, s, _ = x3.shape
    nt = s // tm
    wcols = w_nat.shape[1]
    tok = lambda w, dt: jax.ShapeDtypeStruct((nb, s, w), dt)
    tok_spec = lambda w: pl.BlockSpec((1, tm, w), lambda b, i: (b, i, 0))
    const = lambda shape: pl.BlockSpec(shape, lambda b, i: (0,) * len(shape))
    out_shape = [tok(GROUP_W, _BF16), tok(GROUP_W, _F32), tok(GROUP_W, _BF16), tok(GROUP_W, _F32),
                 jax.ShapeDtypeStruct((nb, nt, DA_HEADS * VT_ROWS, tm), _BF16),
                 tok(GROUP_W, _BF16), tok(GROUP_W, _BF16), tok(GROUP_W, _BF16), tok(GROUP_W, _F32),
                 tok(8, _F32), jax.ShapeDtypeStruct((nb, 8, s), _F32)]
    out_specs = [tok_spec(GROUP_W)] * 4 + [pl.BlockSpec((1, 1, DA_HEADS * VT_ROWS, tm), lambda b, i: (b, i, 0, 0))] \
        + [tok_spec(GROUP_W)] * 4 + [tok_spec(8), pl.BlockSpec((1, 8, tm), lambda b, i: (b, 0, i))]
    return pl.pallas_call(
        functools.partial(_proj_kernel, tm=tm, pos_base=pos_base, pos_mod=pos_mod),
        grid=(nb, nt),
        in_specs=[tok_spec(D_MODEL), const((1, D_MODEL)), const((D_MODEL, wcols)), const((1, HEAD_W)),
                  const((1, HEAD_W))],
        out_specs=out_specs, out_shape=out_shape,
        compiler_params=_cparams(("parallel", "parallel")), name="proj",
    )(x3, norm1, w_nat, b_gates, inv_full)


def _lambda(lam_ref, lam_init):
    lv = lam_ref[...]
    s1 = jnp.sum(lv[0:1] * lv[1:2], axis=-1, keepdims=True)
    s2 = jnp.sum(lv[2:3] * lv[3:4], axis=-1, keepdims=True)
    return jnp.exp(s1) - jnp.exp(s2) + lam_init


def _split_components(q):
    lane = lax.broadcasted_iota(jnp.int32, q.shape, 1)
    zero = jnp.zeros_like(q)
    return jnp.concatenate([jnp.where(lane < DA_QK, q, zero), jnp.where(lane >= DA_QK, q, zero)], axis=0)


def _attn_kernel(lam_ref, q_ref, k_ref, vt_ref, o_ref, acc_ref, s_ref, *, tq, tk, lam_init):
    i = pl.program_id(1)
    heads = range(DA_HEADS)
    qz = [_split_components(q_ref[0, :, h * HEAD_W:(h + 1) * HEAD_W]) for h in heads]
    acc_ref[...] = jnp.zeros_like(acc_ref)

    def scores(h, j):
        k_t = k_ref[0, pl.ds(pl.multiple_of(j * tk, tk), tk), h * HEAD_W:(h + 1) * HEAD_W]
        return _dot(k_t, qz[h], _NT)

    def step(j, ms, last):
        if last:
            kpos = j * tk + lax.broadcasted_iota(jnp.int32, (tk, 1), 0)
            qpos = i * tq + lax.broadcasted_iota(jnp.int32, (1, 2 * tq), 1) % tq
            visible = kpos < (qpos // CHUNK + 1) * CHUNK
        out = []
        s = s_ref[...]
        for h in heads:
            s_next = None
            if h + 1 < DA_HEADS:
                s_next = scores(h + 1, j)
            elif not last:
                s_next = scores(0, j + 1)
            if last:
                s = jnp.where(visible, s, NEG_BIG)
            m_new = jnp.maximum(ms[h], jnp.max(s, axis=0, keepdims=True))
            alpha = jnp.exp2(ms[h] - m_new)
            p = jnp.exp2(s - m_new).astype(_BF16)
            acc_ref[h] = alpha * acc_ref[h] + _dot(vt_ref[0, j, h * VT_ROWS:(h + 1) * VT_ROWS, :], p)
            out.append(m_new)
            s = s_next
        if not last:
            s_ref[...] = s
        return tuple(out)

    n_full = (i * tq) // tk
    s_ref[...] = scores(0, 0)
    init = tuple(jnp.full((1, 2 * tq), NEG_BIG, _F32) for _ in heads)
    ms = lax.fori_loop(0, n_full, lambda j, c: step(j, c, False), init)
    step(n_full, ms, True)

    lam = _lambda(lam_ref, lam_init)
    for h in heads:
        acc = acc_ref[h]
        o = acc[:DA_V] / acc[DA_V:DA_V + 1]
        o_ref[0, :, h * HEAD_W:(h + 1) * HEAD_W] = (o[:, :tq] - lam * o[:, tq:]).T


def _attn_prompt(lamv, q, kb, vt, *, lam_init):
    nb, s, _ = q.shape
    nk, tk = vt.shape[1], vt.shape[3]
    tq = ATTN_TQ
    return pl.pallas_call(
        functools.partial(_attn_kernel, tq=tq, tk=tk, lam_init=lam_init),
        grid=(nb, s // tq),
        in_specs=[pl.BlockSpec((4, DA_QK), lambda b, i: (0, 0)),
                  pl.BlockSpec((1, tq, GROUP_W), lambda b, i: (b, i, 0)),
                  pl.BlockSpec((1, s, GROUP_W), lambda b, i: (b, 0, 0)),
                  pl.BlockSpec((1, nk, DA_HEADS * VT_ROWS, tk), lambda b, i: (b, 0, 0, 0))],
        out_specs=pl.BlockSpec((1, tq, GROUP_W), lambda b, i: (b, i, 0)),
        out_shape=jax.ShapeDtypeStruct((nb, s, GROUP_W), _F32),
        scratch_shapes=[pltpu.VMEM((DA_HEADS, VT_ROWS, 2 * tq), _F32), pltpu.VMEM((tk, 2 * tq), _F32)],
        compiler_params=_cparams(("parallel", "arbitrary")), name="attn",
    )(lamv, q, kb, vt)


def _attn_sample_kernel(lam_ref, q_ref, kn_ref, vn_ref, kc_ref, vc_ref, o_ref, *, lq, lam_init):
    lam = _lambda(lam_ref, lam_init)
    for h in range(DA_HEADS):
        sl = slice(h * HEAD_W, (h + 1) * HEAD_W)
        qz = _split_components(q_ref[0, :, sl])
        kc = kc_ref[0, :, sl].astype(_BF16)
        vc = vc_ref[0, :, sl].astype(_BF16)
        s_c = _dot(qz, kc, _NT)
        s_n = _dot(qz, kn_ref[0, :, sl], _NT)
        m = jnp.maximum(jnp.max(s_c, axis=-1, keepdims=True), jnp.max(s_n, axis=-1, keepdims=True))
        p_c = jnp.exp2(s_c - m)
        p_n = jnp.exp2(s_n - m)
        l = jnp.sum(p_c, axis=-1, keepdims=True) + jnp.sum(p_n, axis=-1, keepdims=True)
        o = (_dot(p_c.astype(_BF16), vc) + _dot(p_n.astype(_BF16), vn_ref[0, :, sl])) / l
        o_ref[0, :, sl] = o[:lq] - lam * o[lq:]


def _attn_sample(lamv, q, kn, vn, kc, vc, *, lam_init):
    nb, lq, _ = q.shape
    past = kc.shape[1]
    new = pl.BlockSpec((1, lq, GROUP_W), lambda b: (b, 0, 0))
    old = pl.BlockSpec((1, past, GROUP_W), lambda b: (b, 0, 0))
    return pl.pallas_call(
        functools.partial(_attn_sample_kernel, lq=lq, lam_init=lam_init),
        grid=(nb,),
        in_specs=[pl.BlockSpec((4, DA_QK), lambda b: (0, 0)), new, new, new, old, old],
        out_specs=new, out_shape=jax.ShapeDtypeStruct((nb, lq, GROUP_W), _F32),
        compiler_params=_cparams(("parallel",)), name="attn_sample",
    )(lamv, q, kn, vn, kc, vc)


def _mlstm_kernel(q_ref, k_ref, v_ref, gc_ref, gr_ref, cn0_ref, m0_ref, h_ref, cn_ref, m_ref,
                  cn_sc, m_sc, *, chunk, nchunks):
    j = pl.program_id(1)

    @pl.when(j == 0)
    def _():
        cn_sc[...] = cn0_ref[0]
        m_sc[...] = m0_ref[0]

    L = chunk
    r_i = lax.broadcasted_iota(jnp.int32, (L, L), 0)
    c_i = lax.broadcasted_iota(jnp.int32, (L, L), 1)
    causal = r_i >= c_i
    tril = causal.astype(_F32)
    triu = (r_i <= c_i).astype(_F32)
    ones_col = (lax.broadcasted_iota(jnp.int32, (L, ML_DV), 1) == 0).astype(_BF16)
    hi = lax.Precision.HIGHEST

    for c in range(nchunks):
        rows = slice(c * L, (c + 1) * L)
        gcol = gc_ref[0, rows, :]
        grow = gr_ref[0, :, rows]
        fc_all = _dot(tril, gcol, precision=hi)
        fr_all = _dot(grow, triu, precision=hi)
        for hh in range(ML_HEADS):
            sl = slice(hh * HEAD_W, (hh + 1) * HEAD_W)
            m = m_sc[hh][0:1, 0:1]
            fc = fc_all[:, ML_HEADS + hh:ML_HEADS + hh + 1]
            fr = fr_all[ML_HEADS + hh:ML_HEADS + hh + 1, :]
            igc = gcol[:, hh:hh + 1]
            igr = grow[hh:hh + 1, :]
            a = jnp.where(causal, fc - fr + igr, -jnp.inf)
            b = fc + m
            m_t = jnp.maximum(b, jnp.max(a, axis=-1, keepdims=True))
            w = jnp.exp(a - m_t)
            sc = jnp.exp(b - m_t)
            q = q_ref[0, rows, sl]
            k = k_ref[0, rows, sl]
            v = v_ref[0, rows, sl]
            wqk = w * _dot(q, k, _NT)
            cn = cn_sc[hh]
            inter = _dot(q, cn.astype(_BF16))
            num = _dot(wqk.astype(_BF16), v) + sc * inter[:, :ML_DV]
            den = jnp.sum(wqk, axis=-1, keepdims=True) + sc * inter[:, ML_DV:ML_DV + 1]
            h_ref[0, rows, sl] = num / jnp.maximum(jnp.abs(den), jnp.exp(-m_t))

            fl = fc[L - 1:L, :]
            g = fl - fc + igc
            m_new = jnp.maximum(fl + m, jnp.max(g, axis=0, keepdims=True))
            ws = jnp.exp(g - m_new)
            decay = jnp.exp(fl + m - m_new)
            kw = (k.astype(_F32) * ws).astype(_BF16)
            v1 = jnp.concatenate([v, ones_col], axis=1)
            cn_sc[hh] = decay * cn + _dot(kw, v1, _TN)
            m_sc[hh] = jnp.broadcast_to(m_new, m_sc.shape[1:])

    @pl.when(j == pl.num_programs(1) - 1)
    def _():
        cn_ref[0] = cn_sc[...]
        m_ref[0] = m_sc[...]


def _mlstm(q, k, v, gc, gr, cn0, m0, *, chunk, block):
    nb, s, _ = q.shape
    tok = lambda w: pl.BlockSpec((1, block, w), lambda b, j: (b, j, 0))
    state = pl.BlockSpec((1, ML_HEADS, ML_DK, 2 * ML_DV), lambda b, j: (b, 0, 0, 0))
    mspec = pl.BlockSpec((1, ML_HEADS, 8, HEAD_W), lambda b, j: (b, 0, 0, 0))
    return pl.pallas_call(
        functools.partial(_mlstm_kernel, chunk=chunk, nchunks=block // chunk),
        grid=(nb, s // block),
        in_specs=[tok(GROUP_W), tok(GROUP_W), tok(GROUP_W), tok(8),
                  pl.BlockSpec((1, 8, block), lambda b, j: (b, 0, j)), state, mspec],
        out_specs=[tok(GROUP_W), state, mspec],
        out_shape=[jax.ShapeDtypeStruct((nb, s, GROUP_W), _F32),
                   jax.ShapeDtypeStruct(cn0.shape, _F32), jax.ShapeDtypeStruct(m0.shape, _F32)],
        scratch_shapes=[pltpu.VMEM((ML_HEADS, ML_DK, 2 * ML_DV), _F32), pltpu.VMEM((ML_HEADS, 8, HEAD_W), _F32)],
        compiler_params=_cparams(("parallel", "arbitrary")), name="mlstm",
    )(q, k, v, gc, gr, cn0, m0)


def _mix_kernel(da_ref, ml_ref, mo_ref, x_ref, subln_ref, wo_ref, g2_ref, wr_ref, br_ref, cnt0_ref,
                xmid_ref, xn_ref, idx_ref, gate_ref, rank_ref, cnt_ref, cnt_sc, *, tm, lam_init):
    step = pl.program_id(0)

    @pl.when(step == 0)
    def _():
        cnt_sc[...] = cnt0_ref[...]

    parts = []
    for h in range(DA_HEADS):
        d = da_ref[:, h * HEAD_W:(h + 1) * HEAD_W]
        d = d * lax.rsqrt(jnp.mean(d * d, axis=-1, keepdims=True) + EPS) * subln_ref[...]
        parts.append((d * (1.0 - lam_init)).astype(_BF16))
    parts.append((jax.nn.sigmoid(mo_ref[...]) * ml_ref[...]).astype(_BF16))
    xm = x_ref[...] + _dot(jnp.concatenate(parts, axis=1), wo_ref[...])
    xmid_ref[...] = xm
    xn = (xm * lax.rsqrt(jnp.mean(xm * xm, axis=-1, keepdims=True) + EPS) * g2_ref[...]).astype(_BF16)
    xn_ref[...] = xn

    logits = _dot(wr_ref[...], xn, _NT) + br_ref[...]
    e_i = lax.broadcasted_iota(jnp.int32, logits.shape, 0)
    member = jnp.zeros(logits.shape, jnp.bool_)
    vals, idxs = [], []
    for _ in range(TOP_K):
        mx = jnp.max(logits, axis=0, keepdims=True)
        sel = jnp.min(jnp.where(logits == mx, e_i, N_EXPERTS), axis=0, keepdims=True)
        hit = e_i == sel
        member = member | hit
        logits = jnp.where(hit, -jnp.inf, logits)
        vals.append(mx)
        idxs.append(sel)
    ex = [jnp.exp(v - vals[0]) for v in vals]
    tot = ex[0] + ex[1] + ex[2] + ex[3]
    idx_ref[...] = jnp.concatenate(idxs, axis=0)
    gate_ref[...] = jnp.concatenate([e / tot for e in ex], axis=0)

    upper = (lax.broadcasted_iota(jnp.int32, (tm, tm), 0) < lax.broadcasted_iota(jnp.int32, (tm, tm), 1))
    memf = member.astype(_F32)
    before = _dot(memf.astype(_BF16), upper.astype(_BF16)) + cnt_sc[...]
    rank_ref[...] = jnp.concatenate(
        [jnp.sum(jnp.where(e_i == s, before, 0.0), axis=0, keepdims=True) for s in idxs], axis=0).astype(jnp.int32)
    cnt_sc[...] += jnp.sum(memf, axis=1, keepdims=True)
    cnt_ref[...] = cnt_sc[...]


def _mix(da, ml, mo, x, subln, w_out, norm2, w_rt, b_r, cnt0, *, tm, lam_init):
    t = x.shape[0]
    tok = lambda w: pl.BlockSpec((tm, w), lambda i: (i, 0))
    const = lambda shape: pl.BlockSpec(shape, lambda i: (0,) * len(shape))
    lane_tok = pl.BlockSpec((TOP_K, tm), lambda i: (0, i))
    return pl.pallas_call(
        functools.partial(_mix_kernel, tm=tm, lam_init=lam_init),
        grid=(t // tm,),
        in_specs=[tok(GROUP_W), tok(GROUP_W), tok(GROUP_W), tok(D_MODEL), const((1, DA_V)),
                  const((D_MODEL, D_MODEL)), const((1, D_MODEL)), const((N_EXPERTS, D_MODEL)),
                  const((N_EXPERTS, 1)), const((N_EXPERTS, 1))],
        out_specs=[tok(D_MODEL), tok(D_MODEL), lane_tok, lane_tok, lane_tok, const((N_EXPERTS, 1))],
        out_shape=[jax.ShapeDtypeStruct((t, D_MODEL), _F32), jax.ShapeDtypeStruct((t, D_MODEL), _BF16),
                   jax.ShapeDtypeStruct((TOP_K, t), jnp.int32), jax.ShapeDtypeStruct((TOP_K, t), _F32),
                   jax.ShapeDtypeStruct((TOP_K, t), jnp.int32), jax.ShapeDtypeStruct((N_EXPERTS, 1), _F32)],
        scratch_shapes=[pltpu.VMEM((N_EXPERTS, 1), _F32)],
        compiler_params=_cparams(("arbitrary",)), name="mix",
    )(da, ml, mo, x, subln, w_out, norm2, w_rt, b_r, cnt0)


def _moe_kernel(be_ref, nu_ref, x_ref, wgu_ref, bgu_ref, wd_ref, bd_ref, y_ref, wgu_sc, wd_sc):
    j = pl.program_id(0)
    e_prev = be_ref[jnp.maximum(j - 1, 0)]
    fresh = (j == 0) | (be_ref[j] != e_prev)

    @pl.when(fresh & (j < nu_ref[0]))
    def _():
        rows = 128
        def cast_gu(r, _):
            sl = pl.ds(pl.multiple_of(r * rows, rows), rows)
            wgu_sc[sl, :] = wgu_ref[0, sl, :].astype(_BF16)
            return 0
        lax.fori_loop(0, D_MODEL // rows, cast_gu, 0)
        def cast_d(r, _):
            sl = pl.ds(pl.multiple_of(r * rows, rows), rows)
            wd_sc[sl, :] = wd_ref[0, sl, :].astype(_BF16)
            return 0
        lax.fori_loop(0, D_FF // rows, cast_d, 0)

    @pl.when(j < nu_ref[0])
    def _():
        x = x_ref[...]
        acc = jnp.zeros(y_ref.shape, _F32)
        for c in range(D_FF // FF_CHUNK):
            lo = c * FF_CHUNK
            gate = _dot(x, wgu_sc[:, lo:lo + FF_CHUNK]) + bgu_ref[0, :, lo:lo + FF_CHUNK]
            up = _dot(x, wgu_sc[:, D_FF + lo:D_FF + lo + FF_CHUNK]) + bgu_ref[0, :, D_FF + lo:D_FF + lo + FF_CHUNK]
            gate = jnp.minimum(gate, SWIGLU_LIMIT)
            up = jnp.clip(up, -SWIGLU_LIMIT, SWIGLU_LIMIT)
            act = (up + 1.0) * (gate * jax.nn.sigmoid(SWIGLU_ALPHA * gate))
            acc = acc + _dot(act.astype(_BF16), wd_sc[lo:lo + FF_CHUNK, :])
        y_ref[...] = acc + bd_ref[0]


def _moe(blk_e, n_used, xb, w_gu, b_gu, w_d, b_d):
    n_blocks = xb.shape[0] // MOE_BLOCK
    last = lambda j, nu: jnp.minimum(j, nu[0] - 1)
    row_spec = pl.BlockSpec((MOE_BLOCK, D_MODEL), lambda j, be, nu: (last(j, nu), 0))
    exp_spec = lambda r, c: pl.BlockSpec((1, r, c), lambda j, be, nu: (be[last(j, nu)], 0, 0))
    grid_spec = pltpu.PrefetchScalarGridSpec(
        num_scalar_prefetch=2, grid=(n_blocks,),
        in_specs=[row_spec, exp_spec(D_MODEL, 2 * D_FF), exp_spec(1, 2 * D_FF), exp_spec(D_FF, D_MODEL),
                  exp_spec(1, D_MODEL)],
        out_specs=row_spec,
        scratch_shapes=[pltpu.VMEM((D_MODEL, 2 * D_FF), _BF16), pltpu.VMEM((D_FF, D_MODEL), _BF16)])
    return pl.pallas_call(
        _moe_kernel, grid_spec=grid_spec, out_shape=jax.ShapeDtypeStruct(xb.shape, _F32),
        compiler_params=_cparams(("arbitrary",)), name="moe",
    )(blk_e, n_used, xb, w_gu, b_gu, w_d, b_d)


def _final_kernel(x_ref, y_ref, g_ref, nf_ref, o_ref):
    g = g_ref[...]
    x = x_ref[...]
    for k in range(TOP_K):
        x = x + g[:, k:k + 1] * y_ref[k]
    o_ref[...] = x * lax.rsqrt(jnp.mean(x * x, axis=-1, keepdims=True) + EPS) * nf_ref[...]


def _final(xmid, yg, gates_t, norm_f, *, tm):
    t = xmid.shape[0]
    return pl.pallas_call(
        _final_kernel, grid=(t // tm,),
        in_specs=[pl.BlockSpec((tm, D_MODEL), lambda i: (i, 0)),
                  pl.BlockSpec((TOP_K, tm, D_MODEL), lambda i: (0, i, 0)),
                  pl.BlockSpec((tm, TOP_K), lambda i: (i, 0)),
                  pl.BlockSpec((1, D_MODEL), lambda i: (0, 0))],
        out_specs=pl.BlockSpec((tm, D_MODEL), lambda i: (i, 0)),
        out_shape=jax.ShapeDtypeStruct((t, D_MODEL), _F32),
        compiler_params=_cparams(("parallel",)), name="final",
    )(xmid, yg, gates_t, norm_f)


def _pack_state(c, n, m):
    nb = c.shape[0]
    ncol = jnp.zeros((nb, ML_HEADS, ML_DK, ML_DV), _F32).at[..., 0].set(n.astype(_F32))
    cn = jnp.concatenate([c.astype(_F32), ncol], axis=-1)
    return cn, jnp.broadcast_to(m.astype(_F32)[:, :, None, None], (nb, ML_HEADS, 8, HEAD_W))


def _unpack_state(cn, m):
    return cn[..., :ML_DV], cn[..., ML_DV], m[:, :, 0, 0]


def kernel(x_prompt, x_sample, cache_k, cache_v, state_C, state_n, state_m, norm1, w_in, b_igate, b_fgate,
           lambda_q1, lambda_k1, lambda_q2, lambda_k2, subln, w_out, norm2, w_router, b_router, w_gate_up,
           b_gate_up, w_down, b_down, norm_f):
    nb, s, _ = x_prompt.shape
    db, ls, _ = x_sample.shape
    past = cache_k.shape[2]
    depth = w_in.shape[0]
    assert depth == 1 and s % PROJ_TILE == 0 and (db * ls) % 8 == 0
    l = 0
    lam_init = 0.8 - 0.6 * math.exp(-0.3 * l)
    t_p, t_s = nb * s, db * ls

    w = w_in[l]
    gate_cols = jnp.pad(w[:, 7 * GROUP_W:], ((0, 0), (0, HEAD_W - 2 * ML_HEADS)))
    w_nat = jnp.concatenate([w[:, :7 * GROUP_W], gate_cols], axis=1).astype(_BF16)
    b_gates = jnp.pad(jnp.concatenate([b_igate[l], b_fgate[l]]).astype(_F32), (0, HEAD_W - 2 * ML_HEADS))[None]
    inv = ROPE_THETA ** (-jnp.arange(0, DA_QK, 2, dtype=_F32) / DA_QK)
    inv_full = jnp.tile(inv, HEAD_W // (DA_QK // 2))[None]
    lamv = jnp.stack([lambda_q1[l], lambda_k1[l], lambda_q2[l], lambda_k2[l]]).astype(_F32)
    g1 = norm1[l][None]

    (q_p, kf_p, kb_p, vf_p, vt_p, mq_p, mk_p, mv_p, mo_p, gc_p, gr_p) = _proj(
        x_prompt, g1, w_nat, b_gates, inv_full, tm=PROJ_TILE, pos_base=0, pos_mod=s)
    (q_s, kf_s, kb_s, vf_s, _, mq_s, mk_s, mv_s, mo_s, gc_s, gr_s) = _proj(
        x_sample.reshape(1, t_s, D_MODEL), g1, w_nat, b_gates, inv_full, tm=t_s, pos_base=past, pos_mod=ls)

    da_p = _attn_prompt(lamv, q_p, kb_p, vt_p, lam_init=lam_init)
    per_b = lambda a: a.reshape(db, ls, a.shape[-1])
    da_s = _attn_sample(lamv, per_b(q_s), per_b(kb_s), per_b(vf_s.astype(_BF16)),
                        cache_k[l].reshape(db, past, GROUP_W), cache_v[l].reshape(db, past, GROUP_W),
                        lam_init=lam_init)

    zeros_state = _pack_state(jnp.zeros((nb, ML_HEADS, ML_DK, ML_DV), _F32), jnp.zeros((nb, ML_HEADS, ML_DK), _F32),
                              jnp.zeros((nb, ML_HEADS), _F32))
    h_p, cn_p, m_p = _mlstm(mq_p, mk_p, mv_p, gc_p, gr_p, *zeros_state, chunk=CHUNK, block=PROJ_TILE)
    gr_sb = gr_s.reshape(8, db, ls).transpose(1, 0, 2)
    h_s, cn_s, m_s = _mlstm(per_b(mq_s), per_b(mk_s), per_b(mv_s), per_b(gc_s), gr_sb,
                            *_pack_state(state_C[l], state_n[l], state_m[l]), chunk=ls, block=ls)
    c_p, n_p, mm_p = _unpack_state(cn_p, m_p)
    c_s, n_s, mm_s = _unpack_state(cn_s, m_s)

    wo = w_out[l].astype(_BF16)
    w_rt = w_router[l].T.astype(_BF16)
    b_r = b_router[l].astype(_F32)[:, None]
    mix = functools.partial(_mix, subln=subln[l][None], w_out=wo, norm2=norm2[l][None], w_rt=w_rt, b_r=b_r,
                            lam_init=lam_init)
    flat = lambda a: a.reshape(-1, a.shape[-1])
    xm_p, xn_p, idx_p, gate_p, rank_p, cnt_p = mix(
        flat(da_p), flat(h_p), flat(mo_p), flat(x_prompt), cnt0=jnp.zeros((N_EXPERTS, 1), _F32), tm=PROJ_TILE)
    xm_s, xn_s, idx_s, gate_s, rank_s, cnt = mix(
        flat(da_s), flat(h_s), flat(mo_s), flat(x_sample), cnt0=cnt_p, tm=t_s)

    t_all = t_p + t_s
    rows = t_all * TOP_K
    n_blocks = -(-rows // MOE_BLOCK) + N_EXPERTS
    counts = cnt[:, 0].astype(jnp.int32)
    padded = (counts + MOE_BLOCK - 1) // MOE_BLOCK * MOE_BLOCK
    pend = jnp.cumsum(padded)
    pstart = pend - padded
    dest_p = pstart[idx_p] + rank_p
    dest_s = pstart[idx_s] + rank_s
    dest = jnp.concatenate([dest_p, dest_s], axis=1)
    n_used = (pend[-1] // MOE_BLOCK).astype(jnp.int32)[None]
    blk_lo = jnp.arange(n_blocks, dtype=jnp.int32) * MOE_BLOCK
    blk_e = jnp.minimum(jnp.sum((pend[None, :] <= blk_lo[:, None]).astype(jnp.int32), axis=1), N_EXPERTS - 1)

    xn_all = jnp.concatenate([xn_p, xn_s], axis=0)
    tok_ids = jnp.broadcast_to(jnp.arange(t_all, dtype=jnp.int32)[None], (TOP_K, t_all))
    buf_tok = jnp.zeros((n_blocks * MOE_BLOCK,), jnp.int32).at[dest.reshape(-1)].set(tok_ids.reshape(-1))
    xb = xn_all[buf_tok]
    yb = _moe(blk_e, n_used, xb, w_gate_up[l], b_gate_up[l][:, None, :], w_down[l], b_down[l][:, None, :])
    nf = norm_f[None]
    y_p = _final(xm_p, yb[dest_p], gate_p.T, nf, tm=PROJ_TILE).reshape(nb, s, D_MODEL)
    y_s = _final(xm_s, yb[dest_s], gate_s.T, nf, tm=t_s).reshape(db, ls, D_MODEL)

    st = lambda a: a[None]
    return (y_p, y_s,
            st(kf_p.reshape(nb, s, DA_HEADS, 2, DA_QK)), st(vf_p.reshape(nb, s, DA_HEADS, DA_V)),
            st(c_p), st(n_p), st(mm_p),
            st(kf_s.reshape(db, ls, DA_HEADS, 2, DA_QK)), st(vf_s.reshape(db, ls, DA_HEADS, DA_V)),
            st(c_s.astype(state_C.dtype)), st(n_s.astype(state_n.dtype)), st(mm_s.astype(state_m.dtype)))
```

```python
import functools
import math

import jax
import jax.numpy as jnp
import numpy as np
from jax import lax
from jax.experimental import pallas as pl
from jax.experimental.pallas import tpu as pltpu
from jax.experimental.pallas import tpu_sc as plsc

D_MODEL = 1024
CHUNK = 64
DA_HEADS = 4
DA_QK = 64
DA_V = 128
ML_HEADS = 4
ML_DK = 128
ML_DV = 128
HEAD_W = 128
GROUP_W = 512
ROPE_THETA = 10000.0
N_EXPERTS = 32
TOP_K = 4
D_FF = 1024
SWIGLU_LIMIT = 7.0
SWIGLU_ALPHA = 1.702
EPS = 1e-6
NEG_BIG = -1e30

LOG2E = math.log2(math.e)
VT_ROWS = 144
PROJ_TILE = 512
ATTN_TQ = 256
MOE_BLOCK = 256
FF_CHUNK = 512
PACK_W = D_MODEL // 2
SC_CORES = 2
SC_SUBCORES = 16
SC_WORKERS = SC_CORES * SC_SUBCORES
DISPATCH_CHUNK = 40
COMBINE_CHUNK = 80
VMEM_LIMIT = 56 * 1024 * 1024

_F32 = jnp.float32
_BF16 = jnp.bfloat16


def _cparams(sem):
    return pltpu.CompilerParams(dimension_semantics=sem, vmem_limit_bytes=VMEM_LIMIT)


def _dot(a, b, dims=(((1,), (0,)), ((), ())), precision=None):
    return lax.dot_general(a, b, dims, precision=precision, preferred_element_type=_F32)


_NT = (((1,), (1,)), ((), ()))
_TN = (((0,), (0,)), ((), ()))


def _pack_rows(x):
    half = x.shape[1] // 2
    bits = lambda v: lax.bitcast_convert_type(v.astype(_BF16).astype(_F32), jnp.int32)
    return bits(x[:, :half]) | lax.shift_right_logical(bits(x[:, half:]), 16)


def _unpack_rows(p):
    hi = lax.bitcast_convert_type(p & jnp.int32(-65536), _F32)
    lo = lax.bitcast_convert_type(lax.shift_left(p, 16), _F32)
    return hi, lo


def _proj_kernel(x_ref, g_ref, w_ref, bg_ref, inv_ref,
                 q_ref, kf_ref, kb_ref, vf_ref, vt_ref, mq_ref, mk_ref, mv_ref, mo_ref, gc_ref, gr_ref,
                 *, tm, pos_base, pos_mod):
    x = x_ref[0]
    xn = (x * lax.rsqrt(jnp.mean(x * x, axis=-1, keepdims=True) + EPS) * g_ref[...]).astype(_BF16)

    row = pl.program_id(1) * tm + lax.broadcasted_iota(jnp.int32, (tm, 1), 0)
    pos = (pos_base + row % pos_mod).astype(_F32)
    ang = pos * inv_ref[...]
    cos = jnp.cos(ang)
    sin = jnp.sin(ang)
    lane = lax.broadcasted_iota(jnp.int32, (1, HEAD_W), 1)
    first = (lane % DA_QK) < (DA_QK // 2)
    sin = jnp.where(first, -sin, sin)

    def rope(z):
        rot = jnp.where(first, pltpu.roll(z, HEAD_W - DA_QK // 2, 1), pltpu.roll(z, DA_QK // 2, 1))
        return z * cos + rot * sin

    def group(c, width=GROUP_W):
        return _dot(xn, w_ref[:, c * GROUP_W:c * GROUP_W + width])

    zq = group(0)
    for h in range(DA_HEADS):
        sl = slice(h * HEAD_W, (h + 1) * HEAD_W)
        q_ref[0, :, sl] = (rope(zq[:, sl]) * (DA_QK ** -0.5 * LOG2E)).astype(_BF16)
    zk = group(1)
    for h in range(DA_HEADS):
        sl = slice(h * HEAD_W, (h + 1) * HEAD_W)
        rk = rope(zk[:, sl])
        kf_ref[0, :, sl] = rk
        kb_ref[0, :, sl] = rk.astype(_BF16)
    zv = group(2)
    vf_ref[0] = zv
    zvt = zv.T.astype(_BF16)
    ones = jnp.ones((VT_ROWS - DA_V, tm), _BF16)
    vt_ref[0, 0] = jnp.concatenate(
        [part for h in range(DA_HEADS) for part in (zvt[h * DA_V:(h + 1) * DA_V], ones)], axis=0)
    mq_ref[0] = group(3).astype(_BF16)
    mk_ref[0] = (group(4) * (ML_DK ** -0.5)).astype(_BF16)
    mv_ref[0] = group(5).astype(_BF16)
    mo_ref[0] = group(6)
    zg = group(7, HEAD_W) + bg_ref[...]
    lane_g = lax.broadcasted_iota(jnp.int32, zg.shape, 1)
    logsig = jnp.minimum(zg, 0.0) - jnp.log(1.0 + jnp.exp(-jnp.abs(zg)))
    gates = jnp.where(lane_g < ML_HEADS, zg, logsig)
    gc_ref[0] = gates[:, :8]
    gr_ref[0] = gates.T[:8, :]


def _proj(x3, norm1, w_nat, b_gates, inv_full, *, tm, pos_base, pos_mod):
    nb, s, _ = x3.shape
    nt = s // tm
    wcols = w_nat.shape[1]
    tok = lambda w, dt: jax.ShapeDtypeStruct((nb, s, w), dt)
    tok_spec = lambda w: pl.BlockSpec((1, tm, w), lambda b, i: (b, i, 0))
    const = lambda shape: pl.BlockSpec(shape, lambda b, i: (0,) * len(shape))
    out_shape = [tok(GROUP_W, _BF16), tok(GROUP_W, _F32), tok(GROUP_W, _BF16), tok(GROUP_W, _F32),
                 jax.ShapeDtypeStruct((nb, nt, DA_HEADS * VT_ROWS, tm), _BF16),
                 tok(GROUP_W, _BF16), tok(GROUP_W, _BF16), tok(GROUP_W, _BF16), tok(GROUP_W, _F32),
                 tok(8, _F32), jax.ShapeDtypeStruct((nb, 8, s), _F32)]
    out_specs = [tok_spec(GROUP_W)] * 4 + [pl.BlockSpec((1, 1, DA_HEADS * VT_ROWS, tm), lambda b, i: (b, i, 0, 0))] \
        + [tok_spec(GROUP_W)] * 4 + [tok_spec(8), pl.BlockSpec((1, 8, tm), lambda b, i: (b, 0, i))]
    return pl.pallas_call(
        functools.partial(_proj_kernel, tm=tm, pos_base=pos_base, pos_mod=pos_mod),
        grid=(nb, nt),
        in_specs=[tok_spec(D_MODEL), const((1, D_MODEL)), const((D_MODEL, wcols)), const((1, HEAD_W)),
                  const((1, HEAD_W))],
        out_specs=out_specs, out_shape=out_shape,
        compiler_params=_cparams(("parallel", "parallel")), name="proj",
    )(x3, norm1, w_nat, b_gates, inv_full)


def _lambda(lam_ref, lam_init):
    lv = lam_ref[...]
    s1 = jnp.sum(lv[0:1] * lv[1:2], axis=-1, keepdims=True)
    s2 = jnp.sum(lv[2:3] * lv[3:4], axis=-1, keepdims=True)
    return jnp.exp(s1) - jnp.exp(s2) + lam_init


def _split_components(q):
    lane = lax.broadcasted_iota(jnp.int32, q.shape, 1)
    zero = jnp.zeros_like(q)
    return jnp.concatenate([jnp.where(lane < DA_QK, q, zero), jnp.where(lane >= DA_QK, q, zero)], axis=0)


def _attn_kernel(lam_ref, q_ref, k_ref, vt_ref, o_ref, acc_ref, s_ref, *, tq, tk, lam_init):
    i = pl.program_id(1)
    heads = range(DA_HEADS)
    qz = [_split_components(q_ref[0, :, h * HEAD_W:(h + 1) * HEAD_W]) for h in heads]
    acc_ref[...] = jnp.zeros_like(acc_ref)

    def scores(h, j):
        k_t = k_ref[0, pl.ds(pl.multiple_of(j * tk, tk), tk), h * HEAD_W:(h + 1) * HEAD_W]
        return _dot(k_t, qz[h], _NT)

    def step(j, ms, last):
        if last:
            kpos = j * tk + lax.broadcasted_iota(jnp.int32, (tk, 1), 0)
            qpos = i * tq + lax.broadcasted_iota(jnp.int32, (1, 2 * tq), 1) % tq
            visible = kpos < (qpos // CHUNK + 1) * CHUNK
        out = []
        s = s_ref[...]
        for h in heads:
            s_next = None
            if h + 1 < DA_HEADS:
                s_next = scores(h + 1, j)
            elif not last:
                s_next = scores(0, j + 1)
            if last:
                s = jnp.where(visible, s, NEG_BIG)
            m_new = jnp.maximum(ms[h], jnp.max(s, axis=0, keepdims=True))
            alpha = jnp.exp2(ms[h] - m_new)
            p = jnp.exp2(s - m_new).astype(_BF16)
            acc_ref[h] = alpha * acc_ref[h] + _dot(vt_ref[0, j, h * VT_ROWS:(h + 1) * VT_ROWS, :], p)
            out.append(m_new)
            s = s_next
        if not last:
            s_ref[...] = s
        return tuple(out)

    n_full = (i * tq) // tk
    s_ref[...] = scores(0, 0)
    init = tuple(jnp.full((1, 2 * tq), NEG_BIG, _F32) for _ in heads)
    ms = lax.fori_loop(0, n_full, lambda j, c: step(j, c, False), init)
    step(n_full, ms, True)

    lam = _lambda(lam_ref, lam_init)
    for h in heads:
        acc = acc_ref[h]
        o = acc[:DA_V] / acc[DA_V:DA_V + 1]
        o_ref[0, :, h * HEAD_W:(h + 1) * HEAD_W] = (o[:, :tq] - lam * o[:, tq:]).T


def _attn_prompt(lamv, q, kb, vt, *, lam_init):
    nb, s, _ = q.shape
    nk, tk = vt.shape[1], vt.shape[3]
    tq = ATTN_TQ
    return pl.pallas_call(
        functools.partial(_attn_kernel, tq=tq, tk=tk, lam_init=lam_init),
        grid=(nb, s // tq),
        in_specs=[pl.BlockSpec((4, DA_QK), lambda b, i: (0, 0)),
                  pl.BlockSpec((1, tq, GROUP_W), lambda b, i: (b, i, 0)),
                  pl.BlockSpec((1, s, GROUP_W), lambda b, i: (b, 0, 0)),
                  pl.BlockSpec((1, nk, DA_HEADS * VT_ROWS, tk), lambda b, i: (b, 0, 0, 0))],
        out_specs=pl.BlockSpec((1, tq, GROUP_W), lambda b, i: (b, i, 0)),
        out_shape=jax.ShapeDtypeStruct((nb, s, GROUP_W), _F32),
        scratch_shapes=[pltpu.VMEM((DA_HEADS, VT_ROWS, 2 * tq), _F32), pltpu.VMEM((tk, 2 * tq), _F32)],
        compiler_params=_cparams(("parallel", "arbitrary")), name="attn",
    )(lamv, q, kb, vt)


def _attn_sample_kernel(lam_ref, q_ref, kn_ref, vn_ref, kc_ref, vc_ref, o_ref, *, lq, lam_init):
    lam = _lambda(lam_ref, lam_init)
    for h in range(DA_HEADS):
        sl = slice(h * HEAD_W, (h + 1) * HEAD_W)
        qz = _split_components(q_ref[0, :, sl])
        kc = kc_ref[0, :, sl].astype(_BF16)
        vc = vc_ref[0, :, sl].astype(_BF16)
        s_c = _dot(qz, kc, _NT)
        s_n = _dot(qz, kn_ref[0, :, sl], _NT)
        m = jnp.maximum(jnp.max(s_c, axis=-1, keepdims=True), jnp.max(s_n, axis=-1, keepdims=True))
        p_c = jnp.exp2(s_c - m)
        p_n = jnp.exp2(s_n - m)
        l = jnp.sum(p_c, axis=-1, keepdims=True) + jnp.sum(p_n, axis=-1, keepdims=True)
        o = (_dot(p_c.astype(_BF16), vc) + _dot(p_n.astype(_BF16), vn_ref[0, :, sl])) / l
        o_ref[0, :, sl] = o[:lq] - lam * o[lq:]


def _attn_sample(lamv, q, kn, vn, kc, vc, *, lam_init):
    nb, lq, _ = q.shape
    past = kc.shape[1]
    new = pl.BlockSpec((1, lq, GROUP_W), lambda b: (b, 0, 0))
    old = pl.BlockSpec((1, past, GROUP_W), lambda b: (b, 0, 0))
    return pl.pallas_call(
        functools.partial(_attn_sample_kernel, lq=lq, lam_init=lam_init),
        grid=(nb,),
        in_specs=[pl.BlockSpec((4, DA_QK), lambda b: (0, 0)), new, new, new, old, old],
        out_specs=new, out_shape=jax.ShapeDtypeStruct((nb, lq, GROUP_W), _F32),
        compiler_params=_cparams(("parallel",)), name="attn_sample",
    )(lamv, q, kn, vn, kc, vc)


def _mlstm_kernel(q_ref, k_ref, v_ref, gc_ref, gr_ref, cn0_ref, m0_ref, h_ref, cn_ref, m_ref,
                  cn_sc, m_sc, *, chunk, nchunks):
    j = pl.program_id(1)

    @pl.when(j == 0)
    def _():
        cn_sc[...] = cn0_ref[0]
        m_sc[...] = m0_ref[0]

    L = chunk
    r_i = lax.broadcasted_iota(jnp.int32, (L, L), 0)
    c_i = lax.broadcasted_iota(jnp.int32, (L, L), 1)
    causal = r_i >= c_i
    tril = causal.astype(_F32)
    triu = (r_i <= c_i).astype(_F32)
    ones_col = (lax.broadcasted_iota(jnp.int32, (L, ML_DV), 1) == 0).astype(_BF16)
    hi = lax.Precision.HIGHEST

    for c in range(nchunks):
        rows = slice(c * L, (c + 1) * L)
        gcol = gc_ref[0, rows, :]
        grow = gr_ref[0, :, rows]
        fc_all = _dot(tril, gcol, precision=hi)
        fr_all = _dot(grow, triu, precision=hi)
        for hh in range(ML_HEADS):
            sl = slice(hh * HEAD_W, (hh + 1) * HEAD_W)
            m = m_sc[hh][0:1, 0:1]
            fc = fc_all[:, ML_HEADS + hh:ML_HEADS + hh + 1]
            fr = fr_all[ML_HEADS + hh:ML_HEADS + hh + 1, :]
            igc = gcol[:, hh:hh + 1]
            igr = grow[hh:hh + 1, :]
            a = jnp.where(causal, fc - fr + igr, -jnp.inf)
            b = fc + m
            m_t = jnp.maximum(b, jnp.max(a, axis=-1, keepdims=True))
            w = jnp.exp(a - m_t)
            sc = jnp.exp(b - m_t)
            q = q_ref[0, rows, sl]
            k = k_ref[0, rows, sl]
            v = v_ref[0, rows, sl]
            wqk = w * _dot(q, k, _NT)
            cn = cn_sc[hh]
            inter = _dot(q, cn.astype(_BF16))
            num = _dot(wqk.astype(_BF16), v) + sc * inter[:, :ML_DV]
            den = jnp.sum(wqk, axis=-1, keepdims=True) + sc * inter[:, ML_DV:ML_DV + 1]
            h_ref[0, rows, sl] = num / jnp.maximum(jnp.abs(den), jnp.exp(-m_t))

            fl = fc[L - 1:L, :]
            g = fl - fc + igc
            m_new = jnp.maximum(fl + m, jnp.max(g, axis=0, keepdims=True))
            ws = jnp.exp(g - m_new)
            decay = jnp.exp(fl + m - m_new)
            kw = (k.astype(_F32) * ws).astype(_BF16)
            v1 = jnp.concatenate([v, ones_col], axis=1)
            cn_sc[hh] = decay * cn + _dot(kw, v1, _TN)
            m_sc[hh] = jnp.broadcast_to(m_new, m_sc.shape[1:])

    @pl.when(j == pl.num_programs(1) - 1)
    def _():
        cn_ref[0] = cn_sc[...]
        m_ref[0] = m_sc[...]


def _mlstm(q, k, v, gc, gr, cn0, m0, *, chunk, block):
    nb, s, _ = q.shape
    tok = lambda w: pl.BlockSpec((1, block, w), lambda b, j: (b, j, 0))
    state = pl.BlockSpec((1, ML_HEADS, ML_DK, 2 * ML_DV), lambda b, j: (b, 0, 0, 0))
    mspec = pl.BlockSpec((1, ML_HEADS, 8, HEAD_W), lambda b, j: (b, 0, 0, 0))
    return pl.pallas_call(
        functools.partial(_mlstm_kernel, chunk=chunk, nchunks=block // chunk),
        grid=(nb, s // block),
        in_specs=[tok(GROUP_W), tok(GROUP_W), tok(GROUP_W), tok(8),
                  pl.BlockSpec((1, 8, block), lambda b, j: (b, 0, j)), state, mspec],
        out_specs=[tok(GROUP_W), state, mspec],
        out_shape=[jax.ShapeDtypeStruct((nb, s, GROUP_W), _F32),
                   jax.ShapeDtypeStruct(cn0.shape, _F32), jax.ShapeDtypeStruct(m0.shape, _F32)],
        scratch_shapes=[pltpu.VMEM((ML_HEADS, ML_DK, 2 * ML_DV), _F32), pltpu.VMEM((ML_HEADS, 8, HEAD_W), _F32)],
        compiler_params=_cparams(("parallel", "arbitrary")), name="mlstm",
    )(q, k, v, gc, gr, cn0, m0)


def _mix_kernel(da_ref, ml_ref, mo_ref, x_ref, subln_ref, wo_ref, g2_ref, wr_ref, br_ref, cnt0_ref,
                xmid_ref, xp_ref, idx_ref, gate_ref, rank_ref, cnt_ref, cnt_sc, *, tm, lam_init):
    step = pl.program_id(0)

    @pl.when(step == 0)
    def _():
        cnt_sc[...] = cnt0_ref[...]

    parts = []
    for h in range(DA_HEADS):
        d = da_ref[:, h * HEAD_W:(h + 1) * HEAD_W]
        d = d * lax.rsqrt(jnp.mean(d * d, axis=-1, keepdims=True) + EPS) * subln_ref[...]
        parts.append((d * (1.0 - lam_init)).astype(_BF16))
    parts.append((jax.nn.sigmoid(mo_ref[...]) * ml_ref[...]).astype(_BF16))
    xm = x_ref[...] + _dot(jnp.concatenate(parts, axis=1), wo_ref[...])
    xmid_ref[...] = xm
    xn_f = xm * lax.rsqrt(jnp.mean(xm * xm, axis=-1, keepdims=True) + EPS) * g2_ref[...]
    xp_ref[...] = _pack_rows(xn_f)
    xn = xn_f.astype(_BF16)

    logits = _dot(wr_ref[...], xn, _NT) + br_ref[...]
    e_i = lax.broadcasted_iota(jnp.int32, logits.shape, 0)
    member = jnp.zeros(logits.shape, jnp.bool_)
    vals, idxs = [], []
    for _ in range(TOP_K):
        mx = jnp.max(logits, axis=0, keepdims=True)
        sel = jnp.min(jnp.where(logits == mx, e_i, N_EXPERTS), axis=0, keepdims=True)
        hit = e_i == sel
        member = member | hit
        logits = jnp.where(hit, -jnp.inf, logits)
        vals.append(mx)
        idxs.append(sel)
    ex = [jnp.exp(v - vals[0]) for v in vals]
    tot = ex[0] + ex[1] + ex[2] + ex[3]
    idx_ref[...] = jnp.concatenate(idxs, axis=0)
    gate_ref[...] = jnp.concatenate([e / tot for e in ex], axis=0)

    upper = (lax.broadcasted_iota(jnp.int32, (tm, tm), 0) < lax.broadcasted_iota(jnp.int32, (tm, tm), 1))
    memf = member.astype(_F32)
    before = _dot(memf.astype(_BF16), upper.astype(_BF16)) + cnt_sc[...]
    rank_ref[...] = jnp.concatenate(
        [jnp.sum(jnp.where(e_i == s, before, 0.0), axis=0, keepdims=True) for s in idxs], axis=0).astype(jnp.int32)
    cnt_sc[...] += jnp.sum(memf, axis=1, keepdims=True)
    cnt_ref[...] = cnt_sc[...]


def _mix(da, ml, mo, x, subln, w_out, norm2, w_rt, b_r, cnt0, *, tm, lam_init):
    t = x.shape[0]
    tok = lambda w: pl.BlockSpec((tm, w), lambda i: (i, 0))
    const = lambda shape: pl.BlockSpec(shape, lambda i: (0,) * len(shape))
    lane_tok = pl.BlockSpec((TOP_K, tm), lambda i: (0, i))
    return pl.pallas_call(
        functools.partial(_mix_kernel, tm=tm, lam_init=lam_init),
        grid=(t // tm,),
        in_specs=[tok(GROUP_W), tok(GROUP_W), tok(GROUP_W), tok(D_MODEL), const((1, DA_V)),
                  const((D_MODEL, D_MODEL)), const((1, D_MODEL)), const((N_EXPERTS, D_MODEL)),
                  const((N_EXPERTS, 1)), const((N_EXPERTS, 1))],
        out_specs=[tok(D_MODEL), tok(PACK_W), lane_tok, lane_tok, lane_tok, const((N_EXPERTS, 1))],
        out_shape=[jax.ShapeDtypeStruct((t, D_MODEL), _F32), jax.ShapeDtypeStruct((t, PACK_W), jnp.int32),
                   jax.ShapeDtypeStruct((TOP_K, t), jnp.int32), jax.ShapeDtypeStruct((TOP_K, t), _F32),
                   jax.ShapeDtypeStruct((TOP_K, t), jnp.int32), jax.ShapeDtypeStruct((N_EXPERTS, 1), _F32)],
        scratch_shapes=[pltpu.VMEM((N_EXPERTS, 1), _F32)],
        compiler_params=_cparams(("arbitrary",)), name="mix",
    )(da, ml, mo, x, subln, w_out, norm2, w_rt, b_r, cnt0)


def _moe_kernel(be_ref, nu_ref, x_ref, wgu_ref, bgu_ref, wd_ref, bd_ref, y_ref, wgu_sc, wd_sc):
    j = pl.program_id(0)
    e_prev = be_ref[jnp.maximum(j - 1, 0)]
    fresh = (j == 0) | (be_ref[j] != e_prev)

    @pl.when(fresh & (j < nu_ref[0]))
    def _():
        rows = 128
        def cast_gu(r, _):
            sl = pl.ds(pl.multiple_of(r * rows, rows), rows)
            wgu_sc[sl, :] = wgu_ref[0, sl, :].astype(_BF16)
            return 0
        lax.fori_loop(0, D_MODEL // rows, cast_gu, 0)
        def cast_d(r, _):
            sl = pl.ds(pl.multiple_of(r * rows, rows), rows)
            wd_sc[sl, :] = wd_ref[0, sl, :].astype(_BF16)
            return 0
        lax.fori_loop(0, D_FF // rows, cast_d, 0)

    @pl.when(j >= nu_ref[0])
    def _():
        y_ref[...] = jnp.zeros_like(y_ref)

    @pl.when(j < nu_ref[0])
    def _():
        hi, lo = _unpack_rows(x_ref[...])
        x = jnp.concatenate([hi.astype(_BF16), lo.astype(_BF16)], axis=1)
        acc = jnp.zeros((MOE_BLOCK, D_MODEL), _F32)
        for c in range(D_FF // FF_CHUNK):
            lo = c * FF_CHUNK
            gate = _dot(x, wgu_sc[:, lo:lo + FF_CHUNK]) + bgu_ref[0, :, lo:lo + FF_CHUNK]
            up = _dot(x, wgu_sc[:, D_FF + lo:D_FF + lo + FF_CHUNK]) + bgu_ref[0, :, D_FF + lo:D_FF + lo + FF_CHUNK]
            gate = jnp.minimum(gate, SWIGLU_LIMIT)
            up = jnp.clip(up, -SWIGLU_LIMIT, SWIGLU_LIMIT)
            act = (up + 1.0) * (gate * jax.nn.sigmoid(SWIGLU_ALPHA * gate))
            acc = acc + _dot(act.astype(_BF16), wd_sc[lo:lo + FF_CHUNK, :])
        y_ref[...] = _pack_rows(acc + bd_ref[0])


def _moe(blk_e, n_used, xb, w_gu, b_gu, w_d, b_d):
    n_blocks = xb.shape[0] // MOE_BLOCK
    last = lambda j, nu: jnp.minimum(j, nu[0] - 1)
    row_spec = pl.BlockSpec((MOE_BLOCK, PACK_W), lambda j, be, nu: (last(j, nu), 0))
    exp_spec = lambda r, c: pl.BlockSpec((1, r, c), lambda j, be, nu: (be[last(j, nu)], 0, 0))
    grid_spec = pltpu.PrefetchScalarGridSpec(
        num_scalar_prefetch=2, grid=(n_blocks,),
        in_specs=[row_spec, exp_spec(D_MODEL, 2 * D_FF), exp_spec(1, 2 * D_FF), exp_spec(D_FF, D_MODEL),
                  exp_spec(1, D_MODEL)],
        out_specs=pl.BlockSpec((MOE_BLOCK, PACK_W), lambda j, be, nu: (j, 0)),
        scratch_shapes=[pltpu.VMEM((D_MODEL, 2 * D_FF), _BF16), pltpu.VMEM((D_FF, D_MODEL), _BF16)])
    return pl.pallas_call(
        _moe_kernel, grid_spec=grid_spec, out_shape=jax.ShapeDtypeStruct(xb.shape, jnp.int32),
        compiler_params=_cparams(("arbitrary",)), name="moe",
    )(blk_e, n_used, xb, w_gu, b_gu, w_d, b_d)


def _sc_mesh():
    return plsc.VectorSubcoreMesh(core_axis_name="c", subcore_axis_name="s")


def _sc_worker():
    return lax.axis_index("s") * SC_CORES + lax.axis_index("c")


def _sc_dispatch(xp, dest3, n_rows):
    n_chunks, _, chunk = dest3.shape
    per_worker = n_chunks // SC_WORKERS
    assert per_worker * SC_WORKERS == n_chunks and chunk % 8 == 0

    def body(x_hbm, d_hbm, o_hbm, idx_v, rows_v, sem):
        first = _sc_worker() * per_worker

        @pl.loop(0, per_worker)
        def _(ci):
            ch = first + ci
            pltpu.sync_copy(d_hbm.at[ch], idx_v)
            pltpu.sync_copy(x_hbm.at[pl.ds(pl.multiple_of(ch * chunk, 8), chunk)], rows_v)
            copies = [pltpu.async_copy(rows_v, o_hbm.at[idx_v.at[k]], sem) for k in range(TOP_K)]
            for cp in copies:
                cp.wait()

    return pl.kernel(
        body, out_type=jax.ShapeDtypeStruct((n_rows, PACK_W), jnp.int32), mesh=_sc_mesh(),
        scratch_types=[pltpu.VMEM((TOP_K, chunk), jnp.int32), pltpu.VMEM((chunk, PACK_W), jnp.int32),
                       pltpu.SemaphoreType.DMA],
        name="sc_dispatch")(xp, dest3)


def _sc_gather(table, idx2):
    n_chunks, chunk = idx2.shape
    per_worker = n_chunks // SC_WORKERS
    assert per_worker * SC_WORKERS == n_chunks and chunk % 8 == 0

    def body(t_hbm, i_hbm, o_hbm, idx_v, rows_v, sem):
        worker = _sc_worker()
        pltpu.sync_copy(i_hbm.at[worker], idx_v)

        @pl.loop(0, per_worker)
        def _(ci):
            pltpu.async_copy(t_hbm.at[idx_v.at[ci]], rows_v, sem).wait()
            row0 = (worker * per_worker + ci) * chunk
            pltpu.sync_copy(rows_v, o_hbm.at[pl.ds(pl.multiple_of(row0, 8), chunk)])

    return pl.kernel(
        body, out_type=jax.ShapeDtypeStruct((n_chunks * chunk, PACK_W), jnp.int32), mesh=_sc_mesh(),
        scratch_types=[pltpu.VMEM((per_worker, chunk), jnp.int32), pltpu.VMEM((chunk, PACK_W), jnp.int32),
                       pltpu.SemaphoreType.DMA],
        name="sc_gather")(table, idx2.reshape(SC_WORKERS, per_worker, chunk))


def _final_kernel(x_ref, y_ref, g_ref, nf_ref, o_ref):
    g = g_ref[...]
    hi = jnp.zeros((x_ref.shape[0], PACK_W), _F32)
    lo = jnp.zeros((x_ref.shape[0], PACK_W), _F32)
    for k in range(TOP_K):
        y_hi, y_lo = _unpack_rows(y_ref[k])
        hi = hi + g[:, k:k + 1] * y_hi
        lo = lo + g[:, k:k + 1] * y_lo
    x = x_ref[...] + jnp.concatenate([hi, lo], axis=1)
    o_ref[...] = x * lax.rsqrt(jnp.mean(x * x, axis=-1, keepdims=True) + EPS) * nf_ref[...]


def _final(xmid, yg, gates_t, norm_f, *, tm, first_block):
    t = xmid.shape[0]
    return pl.pallas_call(
        _final_kernel, grid=(t // tm,),
        in_specs=[pl.BlockSpec((tm, D_MODEL), lambda i: (i, 0)),
                  pl.BlockSpec((TOP_K, tm, PACK_W), lambda i: (0, i + first_block, 0)),
                  pl.BlockSpec((tm, TOP_K), lambda i: (i + first_block, 0)),
                  pl.BlockSpec((1, D_MODEL), lambda i: (0, 0))],
        out_specs=pl.BlockSpec((tm, D_MODEL), lambda i: (i, 0)),
        out_shape=jax.ShapeDtypeStruct((t, D_MODEL), _F32),
        compiler_params=_cparams(("parallel",)), name="final",
    )(xmid, yg, gates_t, norm_f)


def _pack_state(c, n, m):
    nb = c.shape[0]
    ncol = jnp.zeros((nb, ML_HEADS, ML_DK, ML_DV), _F32).at[..., 0].set(n.astype(_F32))
    cn = jnp.concatenate([c.astype(_F32), ncol], axis=-1)
    return cn, jnp.broadcast_to(m.astype(_F32)[:, :, None, None], (nb, ML_HEADS, 8, HEAD_W))


def _unpack_state(cn, m):
    return cn[..., :ML_DV], cn[..., ML_DV], m[:, :, 0, 0]


def kernel(x_prompt, x_sample, cache_k, cache_v, state_C, state_n, state_m, norm1, w_in, b_igate, b_fgate,
           lambda_q1, lambda_k1, lambda_q2, lambda_k2, subln, w_out, norm2, w_router, b_router, w_gate_up,
           b_gate_up, w_down, b_down, norm_f):
    nb, s, _ = x_prompt.shape
    db, ls, _ = x_sample.shape
    past = cache_k.shape[2]
    depth = w_in.shape[0]
    assert depth == 1 and s % PROJ_TILE == 0 and (db * ls) % 8 == 0
    l = 0
    lam_init = 0.8 - 0.6 * math.exp(-0.3 * l)
    t_p, t_s = nb * s, db * ls

    w = w_in[l]
    gate_cols = jnp.pad(w[:, 7 * GROUP_W:], ((0, 0), (0, HEAD_W - 2 * ML_HEADS)))
    w_nat = jnp.concatenate([w[:, :7 * GROUP_W], gate_cols], axis=1).astype(_BF16)
    b_gates = jnp.pad(jnp.concatenate([b_igate[l], b_fgate[l]]).astype(_F32), (0, HEAD_W - 2 * ML_HEADS))[None]
    inv = ROPE_THETA ** (-jnp.arange(0, DA_QK, 2, dtype=_F32) / DA_QK)
    inv_full = jnp.tile(inv, HEAD_W // (DA_QK // 2))[None]
    lamv = jnp.stack([lambda_q1[l], lambda_k1[l], lambda_q2[l], lambda_k2[l]]).astype(_F32)
    g1 = norm1[l][None]

    (q_p, kf_p, kb_p, vf_p, vt_p, mq_p, mk_p, mv_p, mo_p, gc_p, gr_p) = _proj(
        x_prompt, g1, w_nat, b_gates, inv_full, tm=PROJ_TILE, pos_base=0, pos_mod=s)
    (q_s, kf_s, kb_s, vf_s, _, mq_s, mk_s, mv_s, mo_s, gc_s, gr_s) = _proj(
        x_sample.reshape(1, t_s, D_MODEL), g1, w_nat, b_gates, inv_full, tm=t_s, pos_base=past, pos_mod=ls)

    da_p = _attn_prompt(lamv, q_p, kb_p, vt_p, lam_init=lam_init)
    per_b = lambda a: a.reshape(db, ls, a.shape[-1])
    da_s = _attn_sample(lamv, per_b(q_s), per_b(kb_s), per_b(vf_s.astype(_BF16)),
                        cache_k[l].reshape(db, past, GROUP_W), cache_v[l].reshape(db, past, GROUP_W),
                        lam_init=lam_init)

    zeros_state = _pack_state(jnp.zeros((nb, ML_HEADS, ML_DK, ML_DV), _F32), jnp.zeros((nb, ML_HEADS, ML_DK), _F32),
                              jnp.zeros((nb, ML_HEADS), _F32))
    h_p, cn_p, m_p = _mlstm(mq_p, mk_p, mv_p, gc_p, gr_p, *zeros_state, chunk=CHUNK, block=PROJ_TILE)
    gr_sb = gr_s.reshape(8, db, ls).transpose(1, 0, 2)
    h_s, cn_s, m_s = _mlstm(per_b(mq_s), per_b(mk_s), per_b(mv_s), per_b(gc_s), gr_sb,
                            *_pack_state(state_C[l], state_n[l], state_m[l]), chunk=ls, block=ls)
    c_p, n_p, mm_p = _unpack_state(cn_p, m_p)
    c_s, n_s, mm_s = _unpack_state(cn_s, m_s)

    wo = w_out[l].astype(_BF16)
    w_rt = w_router[l].T.astype(_BF16)
    b_r = b_router[l].astype(_F32)[:, None]
    mix = functools.partial(_mix, subln=subln[l][None], w_out=wo, norm2=norm2[l][None], w_rt=w_rt, b_r=b_r,
                            lam_init=lam_init)
    flat = lambda a: a.reshape(-1, a.shape[-1])
    xm_p, xp_p, idx_p, gate_p, rank_p, cnt_p = mix(
        flat(da_p), flat(h_p), flat(mo_p), flat(x_prompt), cnt0=jnp.zeros((N_EXPERTS, 1), _F32), tm=PROJ_TILE)
    xm_s, xp_s, idx_s, gate_s, rank_s, cnt = mix(
        flat(da_s), flat(h_s), flat(mo_s), flat(x_sample), cnt0=cnt_p, tm=t_s)

    t_all = t_p + t_s
    rows = t_all * TOP_K
    n_blocks = -(-rows // MOE_BLOCK) + N_EXPERTS
    counts = cnt[:, 0].astype(jnp.int32)
    padded = (counts + MOE_BLOCK - 1) // MOE_BLOCK * MOE_BLOCK
    pend = jnp.cumsum(padded)
    pstart = pend - padded
    idx = jnp.concatenate([idx_p, idx_s], axis=1)
    dest = pstart[idx] + jnp.concatenate([rank_p, rank_s], axis=1)
    n_used = (pend[-1] // MOE_BLOCK).astype(jnp.int32)[None]
    blk_lo = jnp.arange(n_blocks, dtype=jnp.int32) * MOE_BLOCK
    blk_e = jnp.minimum(jnp.sum((pend[None, :] <= blk_lo[:, None]).astype(jnp.int32), axis=1), N_EXPERTS - 1)

    assert t_all % (SC_WORKERS * DISPATCH_CHUNK) == 0 and rows % (SC_WORKERS * COMBINE_CHUNK) == 0
    dest3 = dest.reshape(TOP_K, t_all // DISPATCH_CHUNK, DISPATCH_CHUNK).transpose(1, 0, 2)
    xb = _sc_dispatch(jnp.concatenate([xp_p, xp_s], axis=0), dest3, n_blocks * MOE_BLOCK)
    yb = _moe(blk_e, n_used, xb, w_gate_up[l], b_gate_up[l][:, None, :], w_down[l], b_down[l][:, None, :])
    yg = _sc_gather(yb, dest.reshape(rows // COMBINE_CHUNK, COMBINE_CHUNK)).reshape(TOP_K, t_all, PACK_W)
    gates_t = jnp.concatenate([gate_p, gate_s], axis=1).T
    nf = norm_f[None]
    y_p = _final(xm_p, yg, gates_t, nf, tm=PROJ_TILE, first_block=0).reshape(nb, s, D_MODEL)
    y_s = _final(xm_s, yg, gates_t, nf, tm=t_s, first_block=t_p // t_s).reshape(db, ls, D_MODEL)

    st = lambda a: a[None]
    return (y_p, y_s,
            st(kf_p.reshape(nb, s, DA_HEADS, 2, DA_QK)), st(vf_p.reshape(nb, s, DA_HEADS, DA_V)),
            st(c_p), st(n_p), st(mm_p),
            st(kf_s.reshape(db, ls, DA_HEADS, 2, DA_QK)), st(vf_s.reshape(db, ls, DA_HEADS, DA_V)),
            st(c_s.astype(state_C.dtype)), st(n_s.astype(state_n.dtype)), st(mm_s.astype(state_m.dtype)))
```

```python
import functools
import math

import jax
import jax.numpy as jnp
import numpy as np
from jax import lax
from jax.experimental import pallas as pl
from jax.experimental.pallas import tpu as pltpu
from jax.experimental.pallas import tpu_sc as plsc

D_MODEL = 1024
CHUNK = 64
DA_HEADS = 4
DA_QK = 64
DA_V = 128
ML_HEADS = 4
ML_DK = 128
ML_DV = 128
HEAD_W = 128
GROUP_W = 512
ROPE_THETA = 10000.0
N_EXPERTS = 32
TOP_K = 4
D_FF = 1024
SWIGLU_LIMIT = 7.0
SWIGLU_ALPHA = 1.702
EPS = 1e-6
NEG_BIG = -1e30

LOG2E = math.log2(math.e)
VT_ROWS = 144
PROJ_TILE = 512
ATTN_TQ = 256
MOE_BLOCK = 256
FF_CHUNK = 512
PACK_W = D_MODEL // 2
SC_CORES = 2
SC_SUBCORES = 16
SC_WORKERS = SC_CORES * SC_SUBCORES
DISPATCH_CHUNK = 40
COMBINE_CHUNK = 80
VMEM_LIMIT = 56 * 1024 * 1024

_F32 = jnp.float32
_BF16 = jnp.bfloat16


def _cparams(sem):
    return pltpu.CompilerParams(dimension_semantics=sem, vmem_limit_bytes=VMEM_LIMIT)


def _dot(a, b, dims=(((1,), (0,)), ((), ())), precision=None):
    return lax.dot_general(a, b, dims, precision=precision, preferred_element_type=_F32)


_NT = (((1,), (1,)), ((), ()))
_TN = (((0,), (0,)), ((), ()))


def _pack_rows(x):
    half = x.shape[1] // 2
    bits = lambda v: lax.bitcast_convert_type(v.astype(_BF16).astype(_F32), jnp.int32)
    return bits(x[:, :half]) | lax.shift_right_logical(bits(x[:, half:]), 16)


def _unpack_rows(p):
    hi = lax.bitcast_convert_type(p & jnp.int32(-65536), _F32)
    lo = lax.bitcast_convert_type(lax.shift_left(p, 16), _F32)
    return hi, lo


def _proj_kernel(x_ref, g_ref, w_ref, bg_ref, inv_ref,
                 q_ref, kf_ref, kb_ref, vf_ref, vt_ref, mq_ref, mk_ref, mv_ref, mo_ref, gc_ref, gr_ref,
                 *, tm, pos_base, pos_mod):
    x = x_ref[0]
    xn = (x * lax.rsqrt(jnp.mean(x * x, axis=-1, keepdims=True) + EPS) * g_ref[...]).astype(_BF16)

    row = pl.program_id(1) * tm + lax.broadcasted_iota(jnp.int32, (tm, 1), 0)
    pos = (pos_base + row % pos_mod).astype(_F32)
    ang = pos * inv_ref[...]
    cos = jnp.cos(ang)
    sin = jnp.sin(ang)
    lane = lax.broadcasted_iota(jnp.int32, (1, HEAD_W), 1)
    first = (lane % DA_QK) < (DA_QK // 2)
    sin = jnp.where(first, -sin, sin)

    def rope(z):
        rot = jnp.where(first, pltpu.roll(z, HEAD_W - DA_QK // 2, 1), pltpu.roll(z, DA_QK // 2, 1))
        return z * cos + rot * sin

    def group(c, width=GROUP_W):
        return _dot(xn, w_ref[:, c * GROUP_W:c * GROUP_W + width])

    def put_q(zq):
        for h in range(DA_HEADS):
            sl = slice(h * HEAD_W, (h + 1) * HEAD_W)
            q_ref[0, :, sl] = (rope(zq[:, sl]) * (DA_QK ** -0.5 * LOG2E)).astype(_BF16)

    def put_k(zk):
        for h in range(DA_HEADS):
            sl = slice(h * HEAD_W, (h + 1) * HEAD_W)
            rk = rope(zk[:, sl])
            kf_ref[0, :, sl] = rk
            kb_ref[0, :, sl] = rk.astype(_BF16)

    def put_v(zv):
        vf_ref[0] = zv
        zvt = zv.T.astype(_BF16)
        ones = jnp.ones((VT_ROWS - DA_V, tm), _BF16)
        vt_ref[0, 0] = jnp.concatenate(
            [part for h in range(DA_HEADS) for part in (zvt[h * DA_V:(h + 1) * DA_V], ones)], axis=0)

    def put_mq(z):
        mq_ref[0] = z.astype(_BF16)

    def put_mk(z):
        mk_ref[0] = (z * (ML_DK ** -0.5)).astype(_BF16)

    def put_mv(z):
        mv_ref[0] = z.astype(_BF16)

    def put_mo(z):
        mo_ref[0] = z

    def put_gates(z):
        zg = z + bg_ref[...]
        lane_g = lax.broadcasted_iota(jnp.int32, zg.shape, 1)
        logsig = jnp.minimum(zg, 0.0) - jnp.log(1.0 + jnp.exp(-jnp.abs(zg)))
        gates = jnp.where(lane_g < ML_HEADS, zg, logsig)
        gc_ref[0] = gates[:, :8]
        gr_ref[0] = gates.T[:8, :]

    sinks = [(put_q, GROUP_W), (put_k, GROUP_W), (put_v, GROUP_W), (put_mq, GROUP_W), (put_mk, GROUP_W),
             (put_mv, GROUP_W), (put_mo, GROUP_W), (put_gates, HEAD_W)]
    nxt = group(0, sinks[0][1])
    for c, (sink, _) in enumerate(sinks):
        cur = nxt
        if c + 1 < len(sinks):
            nxt = group(c + 1, sinks[c + 1][1])
        sink(cur)


def _proj(x3, norm1, w_nat, b_gates, inv_full, *, tm, pos_base, pos_mod):
    nb, s, _ = x3.shape
    nt = s // tm
    wcols = w_nat.shape[1]
    tok = lambda w, dt: jax.ShapeDtypeStruct((nb, s, w), dt)
    tok_spec = lambda w: pl.BlockSpec((1, tm, w), lambda b, i: (b, i, 0))
    const = lambda shape: pl.BlockSpec(shape, lambda b, i: (0,) * len(shape))
    out_shape = [tok(GROUP_W, _BF16), tok(GROUP_W, _F32), tok(GROUP_W, _BF16), tok(GROUP_W, _F32),
                 jax.ShapeDtypeStruct((nb, nt, DA_HEADS * VT_ROWS, tm), _BF16),
                 tok(GROUP_W, _BF16), tok(GROUP_W, _BF16), tok(GROUP_W, _BF16), tok(GROUP_W, _F32),
                 tok(8, _F32), jax.ShapeDtypeStruct((nb, 8, s), _F32)]
    out_specs = [tok_spec(GROUP_W)] * 4 + [pl.BlockSpec((1, 1, DA_HEADS * VT_ROWS, tm), lambda b, i: (b, i, 0, 0))] \
        + [tok_spec(GROUP_W)] * 4 + [tok_spec(8), pl.BlockSpec((1, 8, tm), lambda b, i: (b, 0, i))]
    return pl.pallas_call(
        functools.partial(_proj_kernel, tm=tm, pos_base=pos_base, pos_mod=pos_mod),
        grid=(nb, nt),
        in_specs=[tok_spec(D_MODEL), const((1, D_MODEL)), const((D_MODEL, wcols)), const((1, HEAD_W)),
                  const((1, HEAD_W))],
        out_specs=out_specs, out_shape=out_shape,
        compiler_params=_cparams(("parallel", "parallel")), name="proj",
    )(x3, norm1, w_nat, b_gates, inv_full)


def _lambda(lam_ref, lam_init):
    lv = lam_ref[...]
    s1 = jnp.sum(lv[0:1] * lv[1:2], axis=-1, keepdims=True)
    s2 = jnp.sum(lv[2:3] * lv[3:4], axis=-1, keepdims=True)
    return jnp.exp(s1) - jnp.exp(s2) + lam_init


def _split_components(q):
    lane = lax.broadcasted_iota(jnp.int32, q.shape, 1)
    zero = jnp.zeros_like(q)
    return jnp.concatenate([jnp.where(lane < DA_QK, q, zero), jnp.where(lane >= DA_QK, q, zero)], axis=0)


def _attn_kernel(lam_ref, q_ref, k_ref, vt_ref, o_ref, acc_ref, s_ref, *, tq, tk, lam_init):
    i = pl.program_id(1)
    heads = range(DA_HEADS)
    qz = [_split_components(q_ref[0, :, h * HEAD_W:(h + 1) * HEAD_W]) for h in heads]
    acc_ref[...] = jnp.zeros_like(acc_ref)

    def scores(h, j):
        k_t = k_ref[0, pl.ds(pl.multiple_of(j * tk, tk), tk), h * HEAD_W:(h + 1) * HEAD_W]
        return _dot(k_t, qz[h], _NT)

    def step(j, ms, last):
        if last:
            kpos = j * tk + lax.broadcasted_iota(jnp.int32, (tk, 1), 0)
            qpos = i * tq + lax.broadcasted_iota(jnp.int32, (1, 2 * tq), 1) % tq
            visible = kpos < (qpos // CHUNK + 1) * CHUNK
        out = []
        s = s_ref[...]
        for h in heads:
            s_next = None
            if h + 1 < DA_HEADS:
                s_next = scores(h + 1, j)
            elif not last:
                s_next = scores(0, j + 1)
            if last:
                s = jnp.where(visible, s, NEG_BIG)
            m_new = jnp.maximum(ms[h], jnp.max(s, axis=0, keepdims=True))
            alpha = jnp.exp2(ms[h] - m_new)
            p = jnp.exp2(s - m_new).astype(_BF16)
            acc_ref[h] = alpha * acc_ref[h] + _dot(vt_ref[0, j, h * VT_ROWS:(h + 1) * VT_ROWS, :], p)
            out.append(m_new)
            s = s_next
        if not last:
            s_ref[...] = s
        return tuple(out)

    n_full = (i * tq) // tk
    s_ref[...] = scores(0, 0)
    init = tuple(jnp.full((1, 2 * tq), NEG_BIG, _F32) for _ in heads)
    ms = lax.fori_loop(0, n_full, lambda j, c: step(j, c, False), init)
    step(n_full, ms, True)

    lam = _lambda(lam_ref, lam_init)
    for h in heads:
        acc = acc_ref[h]
        o = acc[:DA_V] / acc[DA_V:DA_V + 1]
        o_ref[0, :, h * HEAD_W:(h + 1) * HEAD_W] = (o[:, :tq] - lam * o[:, tq:]).T


def _attn_prompt(lamv, q, kb, vt, *, lam_init):
    nb, s, _ = q.shape
    nk, tk = vt.shape[1], vt.shape[3]
    tq = ATTN_TQ
    return pl.pallas_call(
        functools.partial(_attn_kernel, tq=tq, tk=tk, lam_init=lam_init),
        grid=(nb, s // tq),
        in_specs=[pl.BlockSpec((4, DA_QK), lambda b, i: (0, 0)),
                  pl.BlockSpec((1, tq, GROUP_W), lambda b, i: (b, i, 0)),
                  pl.BlockSpec((1, s, GROUP_W), lambda b, i: (b, 0, 0)),
                  pl.BlockSpec((1, nk, DA_HEADS * VT_ROWS, tk), lambda b, i: (b, 0, 0, 0))],
        out_specs=pl.BlockSpec((1, tq, GROUP_W), lambda b, i: (b, i, 0)),
        out_shape=jax.ShapeDtypeStruct((nb, s, GROUP_W), _F32),
        scratch_shapes=[pltpu.VMEM((DA_HEADS, VT_ROWS, 2 * tq), _F32), pltpu.VMEM((tk, 2 * tq), _F32)],
        compiler_params=_cparams(("parallel", "arbitrary")), name="attn",
    )(lamv, q, kb, vt)


def _attn_sample_kernel(lam_ref, q_ref, kn_ref, vn_ref, kc_ref, vc_ref, o_ref, *, lq, lam_init):
    lam = _lambda(lam_ref, lam_init)
    for h in range(DA_HEADS):
        sl = slice(h * HEAD_W, (h + 1) * HEAD_W)
        qz = _split_components(q_ref[0, :, sl])
        kc = kc_ref[0, :, sl].astype(_BF16)
        vc = vc_ref[0, :, sl].astype(_BF16)
        s_c = _dot(qz, kc, _NT)
        s_n = _dot(qz, kn_ref[0, :, sl], _NT)
        m = jnp.maximum(jnp.max(s_c, axis=-1, keepdims=True), jnp.max(s_n, axis=-1, keepdims=True))
        p_c = jnp.exp2(s_c - m)
        p_n = jnp.exp2(s_n - m)
        l = jnp.sum(p_c, axis=-1, keepdims=True) + jnp.sum(p_n, axis=-1, keepdims=True)
        o = (_dot(p_c.astype(_BF16), vc) + _dot(p_n.astype(_BF16), vn_ref[0, :, sl])) / l
        o_ref[0, :, sl] = o[:lq] - lam * o[lq:]


def _attn_sample(lamv, q, kn, vn, kc, vc, *, lam_init):
    nb, lq, _ = q.shape
    past = kc.shape[1]
    new = pl.BlockSpec((1, lq, GROUP_W), lambda b: (b, 0, 0))
    old = pl.BlockSpec((1, past, GROUP_W), lambda b: (b, 0, 0))
    return pl.pallas_call(
        functools.partial(_attn_sample_kernel, lq=lq, lam_init=lam_init),
        grid=(nb,),
        in_specs=[pl.BlockSpec((4, DA_QK), lambda b: (0, 0)), new, new, new, old, old],
        out_specs=new, out_shape=jax.ShapeDtypeStruct((nb, lq, GROUP_W), _F32),
        compiler_params=_cparams(("parallel",)), name="attn_sample",
    )(lamv, q, kn, vn, kc, vc)


def _mlstm_kernel(q_ref, k_ref, v_ref, gc_ref, gr_ref, cn0_ref, m0_ref, h_ref, cn_ref, m_ref,
                  cn_sc, m_sc, *, chunk, nchunks):
    j = pl.program_id(1)

    @pl.when(j == 0)
    def _():
        cn_sc[...] = cn0_ref[0]
        m_sc[...] = m0_ref[0]

    L = chunk
    r_i = lax.broadcasted_iota(jnp.int32, (L, L), 0)
    c_i = lax.broadcasted_iota(jnp.int32, (L, L), 1)
    causal = r_i >= c_i
    tril = causal.astype(_F32)
    triu = (r_i <= c_i).astype(_F32)
    ones_col = (lax.broadcasted_iota(jnp.int32, (L, ML_DV), 1) == 0).astype(_BF16)
    hi = lax.Precision.HIGHEST

    for c in range(nchunks):
        rows = slice(c * L, (c + 1) * L)
        gcol = gc_ref[0, rows, :]
        grow = gr_ref[0, :, rows]
        fc_all = _dot(tril, gcol, precision=hi)
        fr_all = _dot(grow, triu, precision=hi)
        for hh in range(ML_HEADS):
            sl = slice(hh * HEAD_W, (hh + 1) * HEAD_W)
            m = m_sc[hh][0:1, 0:1]
            fc = fc_all[:, ML_HEADS + hh:ML_HEADS + hh + 1]
            fr = fr_all[ML_HEADS + hh:ML_HEADS + hh + 1, :]
            igc = gcol[:, hh:hh + 1]
            igr = grow[hh:hh + 1, :]
            a = jnp.where(causal, fc - fr + igr, -jnp.inf)
            b = fc + m
            m_t = jnp.maximum(b, jnp.max(a, axis=-1, keepdims=True))
            w = jnp.exp(a - m_t)
            sc = jnp.exp(b - m_t)
            q = q_ref[0, rows, sl]
            k = k_ref[0, rows, sl]
            v = v_ref[0, rows, sl]
            wqk = w * _dot(q, k, _NT)
            cn = cn_sc[hh]
            inter = _dot(q, cn.astype(_BF16))
            num = _dot(wqk.astype(_BF16), v) + sc * inter[:, :ML_DV]
            den = jnp.sum(wqk, axis=-1, keepdims=True) + sc * inter[:, ML_DV:ML_DV + 1]
            h_ref[0, rows, sl] = num / jnp.maximum(jnp.abs(den), jnp.exp(-m_t))

            fl = fc[L - 1:L, :]
            g = fl - fc + igc
            m_new = jnp.maximum(fl + m, jnp.max(g, axis=0, keepdims=True))
            ws = jnp.exp(g - m_new)
            decay = jnp.exp(fl + m - m_new)
            kw = (k.astype(_F32) * ws).astype(_BF16)
            v1 = jnp.concatenate([v, ones_col], axis=1)
            cn_sc[hh] = decay * cn + _dot(kw, v1, _TN)
            m_sc[hh] = jnp.broadcast_to(m_new, m_sc.shape[1:])

    @pl.when(j == pl.num_programs(1) - 1)
    def _():
        cn_ref[0] = cn_sc[...]
        m_ref[0] = m_sc[...]


def _mlstm(q, k, v, gc, gr, cn0, m0, *, chunk, block):
    nb, s, _ = q.shape
    tok = lambda w: pl.BlockSpec((1, block, w), lambda b, j: (b, j, 0))
    state = pl.BlockSpec((1, ML_HEADS, ML_DK, 2 * ML_DV), lambda b, j: (b, 0, 0, 0))
    mspec = pl.BlockSpec((1, ML_HEADS, 8, HEAD_W), lambda b, j: (b, 0, 0, 0))
    return pl.pallas_call(
        functools.partial(_mlstm_kernel, chunk=chunk, nchunks=block // chunk),
        grid=(nb, s // block),
        in_specs=[tok(GROUP_W), tok(GROUP_W), tok(GROUP_W), tok(8),
                  pl.BlockSpec((1, 8, block), lambda b, j: (b, 0, j)), state, mspec],
        out_specs=[tok(GROUP_W), state, mspec],
        out_shape=[jax.ShapeDtypeStruct((nb, s, GROUP_W), _F32),
                   jax.ShapeDtypeStruct(cn0.shape, _F32), jax.ShapeDtypeStruct(m0.shape, _F32)],
        scratch_shapes=[pltpu.VMEM((ML_HEADS, ML_DK, 2 * ML_DV), _F32), pltpu.VMEM((ML_HEADS, 8, HEAD_W), _F32)],
        compiler_params=_cparams(("parallel", "arbitrary")), name="mlstm",
    )(q, k, v, gc, gr, cn0, m0)


def _mix_kernel(da_ref, ml_ref, mo_ref, x_ref, subln_ref, wo_ref, g2_ref, wr_ref, br_ref, cnt0_ref,
                xmid_ref, xp_ref, idx_ref, gate_ref, rank_ref, cnt_ref, cnt_sc, *, tm, lam_init):
    step = pl.program_id(0)

    @pl.when(step == 0)
    def _():
        cnt_sc[...] = cnt0_ref[...]

    parts = []
    for h in range(DA_HEADS):
        d = da_ref[:, h * HEAD_W:(h + 1) * HEAD_W]
        d = d * lax.rsqrt(jnp.mean(d * d, axis=-1, keepdims=True) + EPS) * subln_ref[...]
        parts.append((d * (1.0 - lam_init)).astype(_BF16))
    parts.append((jax.nn.sigmoid(mo_ref[...]) * ml_ref[...]).astype(_BF16))
    xm = x_ref[...] + _dot(jnp.concatenate(parts, axis=1), wo_ref[...])
    xmid_ref[...] = xm
    xn_f = xm * lax.rsqrt(jnp.mean(xm * xm, axis=-1, keepdims=True) + EPS) * g2_ref[...]
    xp_ref[...] = _pack_rows(xn_f)
    xn = xn_f.astype(_BF16)

    logits = _dot(wr_ref[...], xn, _NT) + br_ref[...]
    e_i = lax.broadcasted_iota(jnp.int32, logits.shape, 0)
    member = jnp.zeros(logits.shape, jnp.bool_)
    vals, idxs = [], []
    for _ in range(TOP_K):
        mx = jnp.max(logits, axis=0, keepdims=True)
        sel = jnp.min(jnp.where(logits == mx, e_i, N_EXPERTS), axis=0, keepdims=True)
        hit = e_i == sel
        member = member | hit
        logits = jnp.where(hit, -jnp.inf, logits)
        vals.append(mx)
        idxs.append(sel)
    ex = [jnp.exp(v - vals[0]) for v in vals]
    tot = ex[0] + ex[1] + ex[2] + ex[3]
    idx_ref[...] = jnp.concatenate(idxs, axis=0)
    gate_ref[...] = jnp.concatenate([e / tot for e in ex], axis=0)

    upper = (lax.broadcasted_iota(jnp.int32, (tm, tm), 0) < lax.broadcasted_iota(jnp.int32, (tm, tm), 1))
    memf = member.astype(_F32)
    before = _dot(memf.astype(_BF16), upper.astype(_BF16)) + cnt_sc[...]
    rank_ref[...] = jnp.concatenate(
        [jnp.sum(jnp.where(e_i == s, before, 0.0), axis=0, keepdims=True) for s in idxs], axis=0).astype(jnp.int32)
    cnt_sc[...] += jnp.sum(memf, axis=1, keepdims=True)
    cnt_ref[...] = cnt_sc[...]


def _mix(da, ml, mo, x, subln, w_out, norm2, w_rt, b_r, cnt0, *, tm, lam_init):
    t = x.shape[0]
    tok = lambda w: pl.BlockSpec((tm, w), lambda i: (i, 0))
    const = lambda shape: pl.BlockSpec(shape, lambda i: (0,) * len(shape))
    lane_tok = pl.BlockSpec((TOP_K, tm), lambda i: (0, i))
    return pl.pallas_call(
        functools.partial(_mix_kernel, tm=tm, lam_init=lam_init),
        grid=(t // tm,),
        in_specs=[tok(GROUP_W), tok(GROUP_W), tok(GROUP_W), tok(D_MODEL), const((1, DA_V)),
                  const((D_MODEL, D_MODEL)), const((1, D_MODEL)), const((N_EXPERTS, D_MODEL)),
                  const((N_EXPERTS, 1)), const((N_EXPERTS, 1))],
        out_specs=[tok(D_MODEL), tok(PACK_W), lane_tok, lane_tok, lane_tok, const((N_EXPERTS, 1))],
        out_shape=[jax.ShapeDtypeStruct((t, D_MODEL), _F32), jax.ShapeDtypeStruct((t, PACK_W), jnp.int32),
                   jax.ShapeDtypeStruct((TOP_K, t), jnp.int32), jax.ShapeDtypeStruct((TOP_K, t), _F32),
                   jax.ShapeDtypeStruct((TOP_K, t), jnp.int32), jax.ShapeDtypeStruct((N_EXPERTS, 1), _F32)],
        scratch_shapes=[pltpu.VMEM((N_EXPERTS, 1), _F32)],
        compiler_params=_cparams(("arbitrary",)), name="mix",
    )(da, ml, mo, x, subln, w_out, norm2, w_rt, b_r, cnt0)


def _moe_kernel(be_ref, nu_ref, x_ref, wgu_ref, bgu_ref, wd_ref, bd_ref, y_ref, wgu_sc, wd_sc):
    j = pl.program_id(0)
    e_prev = be_ref[jnp.maximum(j - 1, 0)]
    fresh = (j == 0) | (be_ref[j] != e_prev)

    @pl.when(fresh & (j < nu_ref[0]))
    def _():
        rows = 128
        def cast_gu(r, _):
            sl = pl.ds(pl.multiple_of(r * rows, rows), rows)
            wgu_sc[sl, :] = wgu_ref[0, sl, :].astype(_BF16)
            return 0
        lax.fori_loop(0, D_MODEL // rows, cast_gu, 0)
        def cast_d(r, _):
            sl = pl.ds(pl.multiple_of(r * rows, rows), rows)
            wd_sc[sl, :] = wd_ref[0, sl, :].astype(_BF16)
            return 0
        lax.fori_loop(0, D_FF // rows, cast_d, 0)

    @pl.when(j >= nu_ref[0])
    def _():
        y_ref[...] = jnp.zeros_like(y_ref)

    @pl.when(j < nu_ref[0])
    def _():
        hi, lo = _unpack_rows(x_ref[...])
        x = jnp.concatenate([hi.astype(_BF16), lo.astype(_BF16)], axis=1)
        n_chunks = D_FF // FF_CHUNK

        def gate_up(c):
            lo = c * FF_CHUNK
            return (_dot(x, wgu_sc[:, lo:lo + FF_CHUNK]), _dot(x, wgu_sc[:, D_FF + lo:D_FF + lo + FF_CHUNK]))

        acc = None
        nxt = gate_up(0)
        for c in range(n_chunks):
            lo = c * FF_CHUNK
            gate, up = nxt
            if c + 1 < n_chunks:
                nxt = gate_up(c + 1)
            gate = jnp.minimum(gate + bgu_ref[0, :, lo:lo + FF_CHUNK], SWIGLU_LIMIT)
            up = jnp.clip(up + bgu_ref[0, :, D_FF + lo:D_FF + lo + FF_CHUNK], -SWIGLU_LIMIT, SWIGLU_LIMIT)
            act = (up + 1.0) * (gate * jax.nn.sigmoid(SWIGLU_ALPHA * gate))
            down = _dot(act.astype(_BF16), wd_sc[lo:lo + FF_CHUNK, :])
            acc = down if acc is None else acc + down
        y_ref[...] = _pack_rows(acc + bd_ref[0])


def _moe(blk_e, n_used, xb, w_gu, b_gu, w_d, b_d):
    n_blocks = xb.shape[0] // MOE_BLOCK
    last = lambda j, nu: jnp.minimum(j, nu[0] - 1)
    row_spec = pl.BlockSpec((MOE_BLOCK, PACK_W), lambda j, be, nu: (last(j, nu), 0))
    exp_spec = lambda r, c: pl.BlockSpec((1, r, c), lambda j, be, nu: (be[last(j, nu)], 0, 0))
    grid_spec = pltpu.PrefetchScalarGridSpec(
        num_scalar_prefetch=2, grid=(n_blocks,),
        in_specs=[row_spec, exp_spec(D_MODEL, 2 * D_FF), exp_spec(1, 2 * D_FF), exp_spec(D_FF, D_MODEL),
                  exp_spec(1, D_MODEL)],
        out_specs=pl.BlockSpec((MOE_BLOCK, PACK_W), lambda j, be, nu: (j, 0)),
        scratch_shapes=[pltpu.VMEM((D_MODEL, 2 * D_FF), _BF16), pltpu.VMEM((D_FF, D_MODEL), _BF16)])
    return pl.pallas_call(
        _moe_kernel, grid_spec=grid_spec, out_shape=jax.ShapeDtypeStruct(xb.shape, jnp.int32),
        compiler_params=_cparams(("arbitrary",)), name="moe",
    )(blk_e, n_used, xb, w_gu, b_gu, w_d, b_d)


def _sc_mesh():
    return plsc.VectorSubcoreMesh(core_axis_name="c", subcore_axis_name="s")


def _sc_worker():
    return lax.axis_index("s") * SC_CORES + lax.axis_index("c")


def _sc_dispatch(xp, dest3, n_rows):
    n_chunks, _, chunk = dest3.shape
    per_worker = n_chunks // SC_WORKERS
    assert per_worker * SC_WORKERS == n_chunks and chunk % 8 == 0

    def body(x_hbm, d_hbm, o_hbm, idx_v, rows_v, sem):
        first = _sc_worker() * per_worker

        @pl.loop(0, per_worker)
        def _(ci):
            ch = first + ci
            pltpu.sync_copy(d_hbm.at[ch], idx_v)
            pltpu.sync_copy(x_hbm.at[pl.ds(pl.multiple_of(ch * chunk, 8), chunk)], rows_v)
            copies = [pltpu.async_copy(rows_v, o_hbm.at[idx_v.at[k]], sem) for k in range(TOP_K)]
            for cp in copies:
                cp.wait()

    return pl.kernel(
        body, out_type=jax.ShapeDtypeStruct((n_rows, PACK_W), jnp.int32), mesh=_sc_mesh(),
        scratch_types=[pltpu.VMEM((TOP_K, chunk), jnp.int32), pltpu.VMEM((chunk, PACK_W), jnp.int32),
                       pltpu.SemaphoreType.DMA],
        name="sc_dispatch")(xp, dest3)


def _sc_gather(table, idx2):
    n_chunks, chunk = idx2.shape
    per_worker = n_chunks // SC_WORKERS
    assert per_worker * SC_WORKERS == n_chunks and chunk % 8 == 0

    def body(t_hbm, i_hbm, o_hbm, idx_v, rows_v, sem):
        worker = _sc_worker()
        pltpu.sync_copy(i_hbm.at[worker], idx_v)

        @pl.loop(0, per_worker)
        def _(ci):
            pltpu.async_copy(t_hbm.at[idx_v.at[ci]], rows_v, sem).wait()
            row0 = (worker * per_worker + ci) * chunk
            pltpu.sync_copy(rows_v, o_hbm.at[pl.ds(pl.multiple_of(row0, 8), chunk)])

    return pl.kernel(
        body, out_type=jax.ShapeDtypeStruct((n_chunks * chunk, PACK_W), jnp.int32), mesh=_sc_mesh(),
        scratch_types=[pltpu.VMEM((per_worker, chunk), jnp.int32), pltpu.VMEM((chunk, PACK_W), jnp.int32),
                       pltpu.SemaphoreType.DMA],
        name="sc_gather")(table, idx2.reshape(SC_WORKERS, per_worker, chunk))


def _final_kernel(x_ref, y_ref, g_ref, nf_ref, o_ref):
    g = g_ref[...]
    hi = jnp.zeros((x_ref.shape[0], PACK_W), _F32)
    lo = jnp.zeros((x_ref.shape[0], PACK_W), _F32)
    for k in range(TOP_K):
        y_hi, y_lo = _unpack_rows(y_ref[k])
        hi = hi + g[:, k:k + 1] * y_hi
        lo = lo + g[:, k:k + 1] * y_lo
    x = x_ref[...] + jnp.concatenate([hi, lo], axis=1)
    o_ref[...] = x * lax.rsqrt(jnp.mean(x * x, axis=-1, keepdims=True) + EPS) * nf_ref[...]


def _final(xmid, yg, gates_t, norm_f, *, tm, first_block):
    t = xmid.shape[0]
    return pl.pallas_call(
        _final_kernel, grid=(t // tm,),
        in_specs=[pl.BlockSpec((tm, D_MODEL), lambda i: (i, 0)),
                  pl.BlockSpec((TOP_K, tm, PACK_W), lambda i: (0, i + first_block, 0)),
                  pl.BlockSpec((tm, TOP_K), lambda i: (i + first_block, 0)),
                  pl.BlockSpec((1, D_MODEL), lambda i: (0, 0))],
        out_specs=pl.BlockSpec((tm, D_MODEL), lambda i: (i, 0)),
        out_shape=jax.ShapeDtypeStruct((t, D_MODEL), _F32),
        compiler_params=_cparams(("parallel",)), name="final",
    )(xmid, yg, gates_t, norm_f)


def _pack_state(c, n, m):
    nb = c.shape[0]
    ncol = jnp.zeros((nb, ML_HEADS, ML_DK, ML_DV), _F32).at[..., 0].set(n.astype(_F32))
    cn = jnp.concatenate([c.astype(_F32), ncol], axis=-1)
    return cn, jnp.broadcast_to(m.astype(_F32)[:, :, None, None], (nb, ML_HEADS, 8, HEAD_W))


def _unpack_state(cn, m):
    return cn[..., :ML_DV], cn[..., ML_DV], m[:, :, 0, 0]


def kernel(x_prompt, x_sample, cache_k, cache_v, state_C, state_n, state_m, norm1, w_in, b_igate, b_fgate,
           lambda_q1, lambda_k1, lambda_q2, lambda_k2, subln, w_out, norm2, w_router, b_router, w_gate_up,
           b_gate_up, w_down, b_down, norm_f):
    nb, s, _ = x_prompt.shape
    db, ls, _ = x_sample.shape
    past = cache_k.shape[2]
    depth = w_in.shape[0]
    assert depth == 1 and s % PROJ_TILE == 0 and (db * ls) % 8 == 0
    l = 0
    lam_init = 0.8 - 0.6 * math.exp(-0.3 * l)
    t_p, t_s = nb * s, db * ls

    w = w_in[l]
    gate_cols = jnp.pad(w[:, 7 * GROUP_W:], ((0, 0), (0, HEAD_W - 2 * ML_HEADS)))
    w_nat = jnp.concatenate([w[:, :7 * GROUP_W], gate_cols], axis=1).astype(_BF16)
    b_gates = jnp.pad(jnp.concatenate([b_igate[l], b_fgate[l]]).astype(_F32), (0, HEAD_W - 2 * ML_HEADS))[None]
    inv = ROPE_THETA ** (-jnp.arange(0, DA_QK, 2, dtype=_F32) / DA_QK)
    inv_full = jnp.tile(inv, HEAD_W // (DA_QK // 2))[None]
    lamv = jnp.stack([lambda_q1[l], lambda_k1[l], lambda_q2[l], lambda_k2[l]]).astype(_F32)
    g1 = norm1[l][None]

    (q_p, kf_p, kb_p, vf_p, vt_p, mq_p, mk_p, mv_p, mo_p, gc_p, gr_p) = _proj(
        x_prompt, g1, w_nat, b_gates, inv_full, tm=PROJ_TILE, pos_base=0, pos_mod=s)
    (q_s, kf_s, kb_s, vf_s, _, mq_s, mk_s, mv_s, mo_s, gc_s, gr_s) = _proj(
        x_sample.reshape(1, t_s, D_MODEL), g1, w_nat, b_gates, inv_full, tm=t_s, pos_base=past, pos_mod=ls)

    da_p = _attn_prompt(lamv, q_p, kb_p, vt_p, lam_init=lam_init)
    per_b = lambda a: a.reshape(db, ls, a.shape[-1])
    da_s = _attn_sample(lamv, per_b(q_s), per_b(kb_s), per_b(vf_s.astype(_BF16)),
                        cache_k[l].reshape(db, past, GROUP_W), cache_v[l].reshape(db, past, GROUP_W),
                        lam_init=lam_init)

    zeros_state = _pack_state(jnp.zeros((nb, ML_HEADS, ML_DK, ML_DV), _F32), jnp.zeros((nb, ML_HEADS, ML_DK), _F32),
                              jnp.zeros((nb, ML_HEADS), _F32))
    h_p, cn_p, m_p = _mlstm(mq_p, mk_p, mv_p, gc_p, gr_p, *zeros_state, chunk=CHUNK, block=PROJ_TILE)
    gr_sb = gr_s.reshape(8, db, ls).transpose(1, 0, 2)
    h_s, cn_s, m_s = _mlstm(per_b(mq_s), per_b(mk_s), per_b(mv_s), per_b(gc_s), gr_sb,
                            *_pack_state(state_C[l], state_n[l], state_m[l]), chunk=ls, block=ls)
    c_p, n_p, mm_p = _unpack_state(cn_p, m_p)
    c_s, n_s, mm_s = _unpack_state(cn_s, m_s)

    wo = w_out[l].astype(_BF16)
    w_rt = w_router[l].T.astype(_BF16)
    b_r = b_router[l].astype(_F32)[:, None]
    mix = functools.partial(_mix, subln=subln[l][None], w_out=wo, norm2=norm2[l][None], w_rt=w_rt, b_r=b_r,
                            lam_init=lam_init)
    flat = lambda a: a.reshape(-1, a.shape[-1])
    xm_p, xp_p, idx_p, gate_p, rank_p, cnt_p = mix(
        flat(da_p), flat(h_p), flat(mo_p), flat(x_prompt), cnt0=jnp.zeros((N_EXPERTS, 1), _F32), tm=PROJ_TILE)
    xm_s, xp_s, idx_s, gate_s, rank_s, cnt = mix(
        flat(da_s), flat(h_s), flat(mo_s), flat(x_sample), cnt0=cnt_p, tm=t_s)

    t_all = t_p + t_s
    rows = t_all * TOP_K
    n_blocks = -(-rows // MOE_BLOCK) + N_EXPERTS
    counts = cnt[:, 0].astype(jnp.int32)
    padded = (counts + MOE_BLOCK - 1) // MOE_BLOCK * MOE_BLOCK
    pend = jnp.cumsum(padded)
    pstart = pend - padded
    idx = jnp.concatenate([idx_p, idx_s], axis=1)
    experts = jnp.arange(N_EXPERTS, dtype=jnp.int32)[:, None, None]
    first_row = jnp.sum(jnp.where(idx[None] == experts, pstart[:, None, None], 0), axis=0)
    dest = first_row + jnp.concatenate([rank_p, rank_s], axis=1)
    n_used = (pend[-1] // MOE_BLOCK).astype(jnp.int32)[None]
    blk_lo = jnp.arange(n_blocks, dtype=jnp.int32) * MOE_BLOCK
    blk_e = jnp.minimum(jnp.sum((pend[None, :] <= blk_lo[:, None]).astype(jnp.int32), axis=1), N_EXPERTS - 1)

    assert t_all % (SC_WORKERS * DISPATCH_CHUNK) == 0 and rows % (SC_WORKERS * COMBINE_CHUNK) == 0
    dest3 = dest.reshape(TOP_K, t_all // DISPATCH_CHUNK, DISPATCH_CHUNK).transpose(1, 0, 2)
    xb = _sc_dispatch(jnp.concatenate([xp_p, xp_s], axis=0), dest3, n_blocks * MOE_BLOCK)
    yb = _moe(blk_e, n_used, xb, w_gate_up[l], b_gate_up[l][:, None, :], w_down[l], b_down[l][:, None, :])
    yg = _sc_gather(yb, dest.reshape(rows // COMBINE_CHUNK, COMBINE_CHUNK)).reshape(TOP_K, t_all, PACK_W)
    gates_t = jnp.concatenate([gate_p, gate_s], axis=1).T
    nf = norm_f[None]
    y_p = _final(xm_p, yg, gates_t, nf, tm=PROJ_TILE, first_block=0).reshape(nb, s, D_MODEL)
    y_s = _final(xm_s, yg, gates_t, nf, tm=t_s, first_block=t_p // t_s).reshape(db, ls, D_MODEL)

    st = lambda a: a[None]
    return (y_p, y_s,
            st(kf_p.reshape(nb, s, DA_HEADS, 2, DA_QK)), st(vf_p.reshape(nb, s, DA_HEADS, DA_V)),
            st(c_p), st(n_p), st(mm_p),
            st(kf_s.reshape(db, ls, DA_HEADS, 2, DA_QK)), st(vf_s.reshape(db, ls, DA_HEADS, DA_V)),
            st(c_s.astype(state_C.dtype)), st(n_s.astype(state_n.dtype)), st(mm_s.astype(state_m.dtype)))
```

```python
import functools
import math

import jax
import jax.numpy as jnp
import numpy as np
from jax import lax
from jax.experimental import pallas as pl
from jax.experimental.pallas import tpu as pltpu
from jax.experimental.pallas import tpu_sc as plsc

D_MODEL = 1024
CHUNK = 64
DA_HEADS = 4
DA_QK = 64
DA_V = 128
ML_HEADS = 4
ML_DK = 128
ML_DV = 128
HEAD_W = 128
GROUP_W = 512
ROPE_THETA = 10000.0
N_EXPERTS = 32
TOP_K = 4
D_FF = 1024
SWIGLU_LIMIT = 7.0
SWIGLU_ALPHA = 1.702
EPS = 1e-6
NEG_BIG = -1e30

LOG2E = math.log2(math.e)
VT_ROWS = 144
PROJ_TILE = 512
ATTN_TQ = 256
MOE_BLOCK = 256
FF_CHUNK = 512
PACK_W = D_MODEL // 2
SC_CORES = 2
SC_SUBCORES = 16
SC_WORKERS = SC_CORES * SC_SUBCORES
DISPATCH_CHUNK = 40
COMBINE_CHUNK = 80
VMEM_LIMIT = 56 * 1024 * 1024

_F32 = jnp.float32
_BF16 = jnp.bfloat16


def _cparams(sem):
    return pltpu.CompilerParams(dimension_semantics=sem, vmem_limit_bytes=VMEM_LIMIT)


def _dot(a, b, dims=(((1,), (0,)), ((), ())), precision=None):
    return lax.dot_general(a, b, dims, precision=precision, preferred_element_type=_F32)


_NT = (((1,), (1,)), ((), ()))
_TN = (((0,), (0,)), ((), ()))


def _pack_rows(x):
    half = x.shape[1] // 2
    bits = lambda v: lax.bitcast_convert_type(v.astype(_BF16).astype(_F32), jnp.int32)
    return bits(x[:, :half]) | lax.shift_right_logical(bits(x[:, half:]), 16)


def _unpack_rows(p):
    hi = lax.bitcast_convert_type(p & jnp.int32(-65536), _F32)
    lo = lax.bitcast_convert_type(lax.shift_left(p, 16), _F32)
    return hi, lo


def _proj_kernel(x_ref, g_ref, w_ref, bg_ref, inv_ref,
                 q_ref, kf_ref, kb_ref, vf_ref, vt_ref, mq_ref, mk_ref, mv_ref, mo_ref, gc_ref, gr_ref,
                 *, tm, pos_base, pos_mod):
    x = x_ref[0]
    xn = (x * lax.rsqrt(jnp.mean(x * x, axis=-1, keepdims=True) + EPS) * g_ref[...]).astype(_BF16)

    row = pl.program_id(1) * tm + lax.broadcasted_iota(jnp.int32, (tm, 1), 0)
    pos = (pos_base + row % pos_mod).astype(_F32)
    ang = pos * inv_ref[...]
    cos = jnp.cos(ang)
    sin = jnp.sin(ang)
    lane = lax.broadcasted_iota(jnp.int32, (1, HEAD_W), 1)
    first = (lane % DA_QK) < (DA_QK // 2)
    sin = jnp.where(first, -sin, sin)

    def rope(z):
        rot = jnp.where(first, pltpu.roll(z, HEAD_W - DA_QK // 2, 1), pltpu.roll(z, DA_QK // 2, 1))
        return z * cos + rot * sin

    def group(c, width=GROUP_W):
        return _dot(xn, w_ref[:, c * GROUP_W:c * GROUP_W + width])

    def put_q(zq):
        for h in range(DA_HEADS):
            sl = slice(h * HEAD_W, (h + 1) * HEAD_W)
            q_ref[0, :, sl] = (rope(zq[:, sl]) * (DA_QK ** -0.5 * LOG2E)).astype(_BF16)

    def put_k(zk):
        for h in range(DA_HEADS):
            sl = slice(h * HEAD_W, (h + 1) * HEAD_W)
            rk = rope(zk[:, sl])
            kf_ref[0, :, sl] = rk
            kb_ref[0, :, sl] = rk.astype(_BF16)

    def put_v(zv):
        for h in range(DA_HEADS):
            vf_ref[0, pl.ds(h, tm, stride=DA_HEADS), :] = zv[:, h * DA_V:(h + 1) * DA_V]
        zvt = zv.T.astype(_BF16)
        ones = jnp.ones((VT_ROWS - DA_V, tm), _BF16)
        vt_ref[0, 0] = jnp.concatenate(
            [part for h in range(DA_HEADS) for part in (zvt[h * DA_V:(h + 1) * DA_V], ones)], axis=0)

    def put_mq(z):
        mq_ref[0] = z.astype(_BF16)

    def put_mk(z):
        mk_ref[0] = (z * (ML_DK ** -0.5)).astype(_BF16)

    def put_mv(z):
        mv_ref[0] = z.astype(_BF16)

    def put_mo(z):
        mo_ref[0] = z

    def put_gates(z):
        zg = z + bg_ref[...]
        lane_g = lax.broadcasted_iota(jnp.int32, zg.shape, 1)
        logsig = jnp.minimum(zg, 0.0) - jnp.log(1.0 + jnp.exp(-jnp.abs(zg)))
        gates = jnp.where(lane_g < ML_HEADS, zg, logsig)
        gc_ref[0] = gates[:, :8]
        gr_ref[0] = gates.T[:8, :]

    sinks = [(put_q, GROUP_W), (put_k, GROUP_W), (put_v, GROUP_W), (put_mq, GROUP_W), (put_mk, GROUP_W),
             (put_mv, GROUP_W), (put_mo, GROUP_W), (put_gates, HEAD_W)]
    nxt = group(0, sinks[0][1])
    for c, (sink, _) in enumerate(sinks):
        cur = nxt
        if c + 1 < len(sinks):
            nxt = group(c + 1, sinks[c + 1][1])
        sink(cur)


def _proj(x3, norm1, w_nat, b_gates, inv_full, *, tm, pos_base, pos_mod):
    nb, s, _ = x3.shape
    nt = s // tm
    wcols = w_nat.shape[1]
    tok = lambda w, dt: jax.ShapeDtypeStruct((nb, s, w), dt)
    tok_spec = lambda w: pl.BlockSpec((1, tm, w), lambda b, i: (b, i, 0))
    const = lambda shape: pl.BlockSpec(shape, lambda b, i: (0,) * len(shape))
    out_shape = [tok(GROUP_W, _BF16), tok(GROUP_W, _F32), tok(GROUP_W, _BF16),
                 jax.ShapeDtypeStruct((nb, s * DA_HEADS, DA_V), _F32),
                 jax.ShapeDtypeStruct((nb, nt, DA_HEADS * VT_ROWS, tm), _BF16),
                 tok(GROUP_W, _BF16), tok(GROUP_W, _BF16), tok(GROUP_W, _BF16), tok(GROUP_W, _F32),
                 tok(8, _F32), jax.ShapeDtypeStruct((nb, 8, s), _F32)]
    out_specs = [tok_spec(GROUP_W)] * 3 + [pl.BlockSpec((1, tm * DA_HEADS, DA_V), lambda b, i: (b, i, 0)),
                                           pl.BlockSpec((1, 1, DA_HEADS * VT_ROWS, tm), lambda b, i: (b, i, 0, 0))] \
        + [tok_spec(GROUP_W)] * 4 + [tok_spec(8), pl.BlockSpec((1, 8, tm), lambda b, i: (b, 0, i))]
    return pl.pallas_call(
        functools.partial(_proj_kernel, tm=tm, pos_base=pos_base, pos_mod=pos_mod),
        grid=(nb, nt),
        in_specs=[tok_spec(D_MODEL), const((1, D_MODEL)), const((D_MODEL, wcols)), const((1, HEAD_W)),
                  const((1, HEAD_W))],
        out_specs=out_specs, out_shape=out_shape,
        compiler_params=_cparams(("parallel", "parallel")), name="proj",
    )(x3, norm1, w_nat, b_gates, inv_full)


def _lambda(lam_ref, lam_init):
    lv = lam_ref[...]
    s1 = jnp.sum(lv[0:1] * lv[1:2], axis=-1, keepdims=True)
    s2 = jnp.sum(lv[2:3] * lv[3:4], axis=-1, keepdims=True)
    return jnp.exp(s1) - jnp.exp(s2) + lam_init


def _split_components(q):
    lane = lax.broadcasted_iota(jnp.int32, q.shape, 1)
    zero = jnp.zeros_like(q)
    return jnp.concatenate([jnp.where(lane < DA_QK, q, zero), jnp.where(lane >= DA_QK, q, zero)], axis=0)


def _attn_kernel(lam_ref, q_ref, k_ref, vt_ref, o_ref, acc_ref, s_ref, *, tq, tk, lam_init):
    i = pl.program_id(1)
    heads = range(DA_HEADS)
    qz = [_split_components(q_ref[0, :, h * HEAD_W:(h + 1) * HEAD_W]) for h in heads]
    acc_ref[...] = jnp.zeros_like(acc_ref)

    def scores(h, j):
        k_t = k_ref[0, pl.ds(pl.multiple_of(j * tk, tk), tk), h * HEAD_W:(h + 1) * HEAD_W]
        return _dot(k_t, qz[h], _NT)

    def step(j, ms, last):
        if last:
            kpos = j * tk + lax.broadcasted_iota(jnp.int32, (tk, 1), 0)
            qpos = i * tq + lax.broadcasted_iota(jnp.int32, (1, 2 * tq), 1) % tq
            visible = kpos < (qpos // CHUNK + 1) * CHUNK
        out = []
        s = s_ref[...]
        for h in heads:
            s_next = None
            if h + 1 < DA_HEADS:
                s_next = scores(h + 1, j)
            elif not last:
                s_next = scores(0, j + 1)
            if last:
                s = jnp.where(visible, s, NEG_BIG)
            m_new = jnp.maximum(ms[h], jnp.max(s, axis=0, keepdims=True))
            alpha = jnp.exp2(ms[h] - m_new)
            p = jnp.exp2(s - m_new).astype(_BF16)
            acc_ref[h] = alpha * acc_ref[h] + _dot(vt_ref[0, j, h * VT_ROWS:(h + 1) * VT_ROWS, :], p)
            out.append(m_new)
            s = s_next
        if not last:
            s_ref[...] = s
        return tuple(out)

    n_full = (i * tq) // tk
    s_ref[...] = scores(0, 0)
    init = tuple(jnp.full((1, 2 * tq), NEG_BIG, _F32) for _ in heads)
    ms = lax.fori_loop(0, n_full, lambda j, c: step(j, c, False), init)
    step(n_full, ms, True)

    lam = _lambda(lam_ref, lam_init)
    for h in heads:
        acc = acc_ref[h]
        o = acc[:DA_V] / acc[DA_V:DA_V + 1]
        o_ref[0, :, h * HEAD_W:(h + 1) * HEAD_W] = (o[:, :tq] - lam * o[:, tq:]).T


def _attn_prompt(lamv, q, kb, vt, *, lam_init):
    nb, s, _ = q.shape
    nk, tk = vt.shape[1], vt.shape[3]
    tq = ATTN_TQ
    return pl.pallas_call(
        functools.partial(_attn_kernel, tq=tq, tk=tk, lam_init=lam_init),
        grid=(nb, s // tq),
        in_specs=[pl.BlockSpec((4, DA_QK), lambda b, i: (0, 0)),
                  pl.BlockSpec((1, tq, GROUP_W), lambda b, i: (b, i, 0)),
                  pl.BlockSpec((1, s, GROUP_W), lambda b, i: (b, 0, 0)),
                  pl.BlockSpec((1, nk, DA_HEADS * VT_ROWS, tk), lambda b, i: (b, 0, 0, 0))],
        out_specs=pl.BlockSpec((1, tq, GROUP_W), lambda b, i: (b, i, 0)),
        out_shape=jax.ShapeDtypeStruct((nb, s, GROUP_W), _F32),
        scratch_shapes=[pltpu.VMEM((DA_HEADS, VT_ROWS, 2 * tq), _F32), pltpu.VMEM((tk, 2 * tq), _F32)],
        compiler_params=_cparams(("parallel", "arbitrary")), name="attn",
    )(lamv, q, kb, vt)


def _attn_sample_kernel(lam_ref, q_ref, kn_ref, vn_ref, kc_ref, vc_ref, o_ref, *, lq, past, lam_init):
    lam = _lambda(lam_ref, lam_init)
    for h in range(DA_HEADS):
        sl = slice(h * HEAD_W, (h + 1) * HEAD_W)
        qz = _split_components(q_ref[0, :, sl])
        kct = kc_ref[0, sl, :].astype(_BF16)
        vc = vc_ref[0, pl.ds(h, past, stride=DA_HEADS), :].astype(_BF16)
        s_c = _dot(qz, kct)
        s_n = _dot(qz, kn_ref[0, :, sl], _NT)
        m = jnp.maximum(jnp.max(s_c, axis=-1, keepdims=True), jnp.max(s_n, axis=-1, keepdims=True))
        p_c = jnp.exp2(s_c - m)
        p_n = jnp.exp2(s_n - m)
        l = jnp.sum(p_c, axis=-1, keepdims=True) + jnp.sum(p_n, axis=-1, keepdims=True)
        o = (_dot(p_c.astype(_BF16), vc) + _dot(p_n.astype(_BF16), vn_ref[0, :, sl])) / l
        o_ref[0, :, sl] = o[:lq] - lam * o[lq:]


def _attn_sample(lamv, q, kn, vn, kct, vc, *, lam_init):
    nb, lq, _ = q.shape
    past = kct.shape[2]
    new = pl.BlockSpec((1, lq, GROUP_W), lambda b: (b, 0, 0))
    return pl.pallas_call(
        functools.partial(_attn_sample_kernel, lq=lq, past=past, lam_init=lam_init),
        grid=(nb,),
        in_specs=[pl.BlockSpec((4, DA_QK), lambda b: (0, 0)), new, new, new,
                  pl.BlockSpec((1, GROUP_W, past), lambda b: (b, 0, 0)),
                  pl.BlockSpec((1, past * DA_HEADS, DA_V), lambda b: (b, 0, 0))],
        out_specs=new, out_shape=jax.ShapeDtypeStruct((nb, lq, GROUP_W), _F32),
        compiler_params=_cparams(("parallel",)), name="attn_sample",
    )(lamv, q, kn, vn, kct, vc)


def _mlstm_kernel(q_ref, k_ref, v_ref, gc_ref, gr_ref, cn0_ref, m0_ref, h_ref, cn_ref, m_ref,
                  cn_sc, m_sc, *, chunk, nchunks):
    j = pl.program_id(1)

    @pl.when(j == 0)
    def _():
        cn_sc[...] = cn0_ref[0]
        m_sc[...] = m0_ref[0]

    L = chunk
    r_i = lax.broadcasted_iota(jnp.int32, (L, L), 0)
    c_i = lax.broadcasted_iota(jnp.int32, (L, L), 1)
    causal = r_i >= c_i
    tril = causal.astype(_F32)
    triu = (r_i <= c_i).astype(_F32)
    ones_col = (lax.broadcasted_iota(jnp.int32, (L, ML_DV), 1) == 0).astype(_BF16)
    hi = lax.Precision.HIGHEST

    for c in range(nchunks):
        rows = slice(c * L, (c + 1) * L)
        gcol = gc_ref[0, rows, :]
        grow = gr_ref[0, :, rows]
        fc_all = _dot(tril, gcol, precision=hi)
        fr_all = _dot(grow, triu, precision=hi)
        for hh in range(ML_HEADS):
            sl = slice(hh * HEAD_W, (hh + 1) * HEAD_W)
            m = m_sc[hh][0:1, 0:1]
            fc = fc_all[:, ML_HEADS + hh:ML_HEADS + hh + 1]
            fr = fr_all[ML_HEADS + hh:ML_HEADS + hh + 1, :]
            igc = gcol[:, hh:hh + 1]
            igr = grow[hh:hh + 1, :]
            a = jnp.where(causal, fc - fr + igr, -jnp.inf)
            b = fc + m
            m_t = jnp.maximum(b, jnp.max(a, axis=-1, keepdims=True))
            w = jnp.exp(a - m_t)
            sc = jnp.exp(b - m_t)
            q = q_ref[0, rows, sl]
            k = k_ref[0, rows, sl]
            v = v_ref[0, rows, sl]
            wqk = w * _dot(q, k, _NT)
            cn = cn_sc[hh]
            inter = _dot(q, cn.astype(_BF16))
            num = _dot(wqk.astype(_BF16), v) + sc * inter[:, :ML_DV]
            den = jnp.sum(wqk, axis=-1, keepdims=True) + sc * inter[:, ML_DV:ML_DV + 1]
            h_ref[0, rows, sl] = num / jnp.maximum(jnp.abs(den), jnp.exp(-m_t))

            fl = fc[L - 1:L, :]
            g = fl - fc + igc
            m_new = jnp.maximum(fl + m, jnp.max(g, axis=0, keepdims=True))
            ws = jnp.exp(g - m_new)
            decay = jnp.exp(fl + m - m_new)
            kw = (k.astype(_F32) * ws).astype(_BF16)
            v1 = jnp.concatenate([v, ones_col], axis=1)
            cn_sc[hh] = decay * cn + _dot(kw, v1, _TN)
            m_sc[hh] = jnp.broadcast_to(m_new, m_sc.shape[1:])


    @pl.when(j == pl.num_programs(1) - 1)
    def _():
        cn_ref[0] = cn_sc[...]
        m_ref[0] = m_sc[...]


def _mlstm(q, k, v, gc, gr, cn0, m0, *, chunk, block):
    nb, s, _ = q.shape
    tok = lambda w: pl.BlockSpec((1, block, w), lambda b, j: (b, j, 0))
    state = pl.BlockSpec((1, ML_HEADS, ML_DK, 2 * ML_DV), lambda b, j: (b, 0, 0, 0))
    mspec = pl.BlockSpec((1, ML_HEADS, 8, HEAD_W), lambda b, j: (b, 0, 0, 0))
    return pl.pallas_call(
        functools.partial(_mlstm_kernel, chunk=chunk, nchunks=block // chunk),
        grid=(nb, s // block),
        in_specs=[tok(GROUP_W), tok(GROUP_W), tok(GROUP_W), tok(8),
                  pl.BlockSpec((1, 8, block), lambda b, j: (b, 0, j)), state, mspec],
        out_specs=[tok(GROUP_W), state, mspec],
        out_shape=[jax.ShapeDtypeStruct((nb, s, GROUP_W), _F32),
                   jax.ShapeDtypeStruct(cn0.shape, _F32), jax.ShapeDtypeStruct(m0.shape, _F32)],
        scratch_shapes=[pltpu.VMEM((ML_HEADS, ML_DK, 2 * ML_DV), _F32), pltpu.VMEM((ML_HEADS, 8, HEAD_W), _F32)],
        compiler_params=_cparams(("parallel", "arbitrary")), name="mlstm",
    )(q, k, v, gc, gr, cn0, m0)


def _mix_kernel(da_ref, ml_ref, mo_ref, x_ref, subln_ref, wo_ref, g2_ref, wr_ref, br_ref, cnt0_ref,
                xmid_ref, xp_ref, idx_ref, gate_ref, rank_ref, cnt_ref, cnt_sc, *, tm, lam_init):
    step = pl.program_id(0)

    @pl.when(step == 0)
    def _():
        cnt_sc[...] = cnt0_ref[...]

    parts = []
    for h in range(DA_HEADS):
        d = da_ref[:, h * HEAD_W:(h + 1) * HEAD_W]
        d = d * lax.rsqrt(jnp.mean(d * d, axis=-1, keepdims=True) + EPS) * subln_ref[...]
        parts.append((d * (1.0 - lam_init)).astype(_BF16))
    parts.append((jax.nn.sigmoid(mo_ref[...]) * ml_ref[...]).astype(_BF16))
    xm = x_ref[...] + _dot(jnp.concatenate(parts, axis=1), wo_ref[...])
    xmid_ref[...] = xm
    xn_f = xm * lax.rsqrt(jnp.mean(xm * xm, axis=-1, keepdims=True) + EPS) * g2_ref[...]
    xp_ref[...] = _pack_rows(xn_f)
    xn = xn_f.astype(_BF16)

    logits = _dot(wr_ref[...], xn, _NT) + br_ref[...]
    e_i = lax.broadcasted_iota(jnp.int32, logits.shape, 0)
    member = jnp.zeros(logits.shape, jnp.bool_)
    vals, idxs = [], []
    for _ in range(TOP_K):
        mx = jnp.max(logits, axis=0, keepdims=True)
        sel = jnp.min(jnp.where(logits == mx, e_i, N_EXPERTS), axis=0, keepdims=True)
        hit = e_i == sel
        member = member | hit
        logits = jnp.where(hit, -jnp.inf, logits)
        vals.append(mx)
        idxs.append(sel)
    ex = [jnp.exp(v - vals[0]) for v in vals]
    tot = ex[0] + ex[1] + ex[2] + ex[3]
    idx_ref[...] = jnp.concatenate(idxs, axis=0)
    gate_ref[...] = jnp.concatenate([e / tot for e in ex], axis=0)

    upper = (lax.broadcasted_iota(jnp.int32, (tm, tm), 0) < lax.broadcasted_iota(jnp.int32, (tm, tm), 1))
    memf = member.astype(_F32)
    before = _dot(memf.astype(_BF16), upper.astype(_BF16)) + cnt_sc[...]
    rank_ref[...] = jnp.concatenate(
        [jnp.sum(jnp.where(e_i == s, before, 0.0), axis=0, keepdims=True) for s in idxs], axis=0).astype(jnp.int32)
    cnt_sc[...] += jnp.sum(memf, axis=1, keepdims=True)
    cnt_ref[...] = cnt_sc[...]


def _mix(da, ml, mo, x, subln, w_out, norm2, w_rt, b_r, cnt0, *, tm, lam_init):
    t = x.shape[0]
    tok = lambda w: pl.BlockSpec((tm, w), lambda i: (i, 0))
    const = lambda shape: pl.BlockSpec(shape, lambda i: (0,) * len(shape))
    lane_tok = pl.BlockSpec((TOP_K, tm), lambda i: (0, i))
    return pl.pallas_call(
        functools.partial(_mix_kernel, tm=tm, lam_init=lam_init),
        grid=(t // tm,),
        in_specs=[tok(GROUP_W), tok(GROUP_W), tok(GROUP_W), tok(D_MODEL), const((1, DA_V)),
                  const((D_MODEL, D_MODEL)), const((1, D_MODEL)), const((N_EXPERTS, D_MODEL)),
                  const((N_EXPERTS, 1)), const((N_EXPERTS, 1))],
        out_specs=[tok(D_MODEL), tok(PACK_W), lane_tok, lane_tok, lane_tok, const((N_EXPERTS, 1))],
        out_shape=[jax.ShapeDtypeStruct((t, D_MODEL), _F32), jax.ShapeDtypeStruct((t, PACK_W), jnp.int32),
                   jax.ShapeDtypeStruct((TOP_K, t), jnp.int32), jax.ShapeDtypeStruct((TOP_K, t), _F32),
                   jax.ShapeDtypeStruct((TOP_K, t), jnp.int32), jax.ShapeDtypeStruct((N_EXPERTS, 1), _F32)],
        scratch_shapes=[pltpu.VMEM((N_EXPERTS, 1), _F32)],
        compiler_params=_cparams(("arbitrary",)), name="mix",
    )(da, ml, mo, x, subln, w_out, norm2, w_rt, b_r, cnt0)


def _moe_kernel(be_ref, nu_ref, x_ref, wgu_ref, bgu_ref, wd_ref, bd_ref, y_ref, wgu_sc, wd_sc):
    j = pl.program_id(0)
    e_prev = be_ref[jnp.maximum(j - 1, 0)]
    fresh = (j == 0) | (be_ref[j] != e_prev)

    @pl.when(fresh & (j < nu_ref[0]))
    def _():
        rows = 128
        def cast_gu(r, _):
            sl = pl.ds(pl.multiple_of(r * rows, rows), rows)
            wgu_sc[sl, :] = wgu_ref[0, sl, :].astype(_BF16)
            return 0
        lax.fori_loop(0, D_MODEL // rows, cast_gu, 0)
        def cast_d(r, _):
            sl = pl.ds(pl.multiple_of(r * rows, rows), rows)
            wd_sc[sl, :] = wd_ref[0, sl, :].astype(_BF16)
            return 0
        lax.fori_loop(0, D_FF // rows, cast_d, 0)

    @pl.when(j >= nu_ref[0])
    def _():
        y_ref[...] = jnp.zeros_like(y_ref)

    @pl.when(j < nu_ref[0])
    def _():
        hi, lo = _unpack_rows(x_ref[...])
        x = jnp.concatenate([hi.astype(_BF16), lo.astype(_BF16)], axis=1)
        n_chunks = D_FF // FF_CHUNK

        def gate_up(c):
            lo = c * FF_CHUNK
            return (_dot(x, wgu_sc[:, lo:lo + FF_CHUNK]), _dot(x, wgu_sc[:, D_FF + lo:D_FF + lo + FF_CHUNK]))

        acc = None
        nxt = gate_up(0)
        for c in range(n_chunks):
            lo = c * FF_CHUNK
            gate, up = nxt
            if c + 1 < n_chunks:
                nxt = gate_up(c + 1)
            gate = jnp.minimum(gate + bgu_ref[0, :, lo:lo + FF_CHUNK], SWIGLU_LIMIT)
            up = jnp.clip(up + bgu_ref[0, :, D_FF + lo:D_FF + lo + FF_CHUNK], -SWIGLU_LIMIT, SWIGLU_LIMIT)
            act = (up + 1.0) * (gate * jax.nn.sigmoid(SWIGLU_ALPHA * gate))
            down = _dot(act.astype(_BF16), wd_sc[lo:lo + FF_CHUNK, :])
            acc = down if acc is None else acc + down
        y_ref[...] = _pack_rows(acc + bd_ref[0])


def _moe(blk_e, n_used, xb, w_gu, b_gu, w_d, b_d):
    n_blocks = xb.shape[0] // MOE_BLOCK
    last = lambda j, nu: jnp.minimum(j, nu[0] - 1)
    row_spec = pl.BlockSpec((MOE_BLOCK, PACK_W), lambda j, be, nu: (last(j, nu), 0))
    exp_spec = lambda r, c: pl.BlockSpec((1, r, c), lambda j, be, nu: (be[last(j, nu)], 0, 0))
    grid_spec = pltpu.PrefetchScalarGridSpec(
        num_scalar_prefetch=2, grid=(n_blocks,),
        in_specs=[row_spec, exp_spec(D_MODEL, 2 * D_FF), exp_spec(1, 2 * D_FF), exp_spec(D_FF, D_MODEL),
                  exp_spec(1, D_MODEL)],
        out_specs=pl.BlockSpec((MOE_BLOCK, PACK_W), lambda j, be, nu: (j, 0)),
        scratch_shapes=[pltpu.VMEM((D_MODEL, 2 * D_FF), _BF16), pltpu.VMEM((D_FF, D_MODEL), _BF16)])
    return pl.pallas_call(
        _moe_kernel, grid_spec=grid_spec, out_shape=jax.ShapeDtypeStruct(xb.shape, jnp.int32),
        compiler_params=_cparams(("arbitrary",)), name="moe",
    )(blk_e, n_used, xb, w_gu, b_gu, w_d, b_d)


def _sc_mesh():
    return plsc.VectorSubcoreMesh(core_axis_name="c", subcore_axis_name="s")


def _sc_worker():
    return lax.axis_index("s") * SC_CORES + lax.axis_index("c")


def _sc_dispatch(xp, dest3, n_rows):
    n_chunks, _, chunk = dest3.shape
    per_worker = n_chunks // SC_WORKERS
    assert per_worker * SC_WORKERS == n_chunks and chunk % 8 == 0

    def body(x_hbm, d_hbm, o_hbm, idx_v, rows_v, sem):
        first = _sc_worker() * per_worker

        @pl.loop(0, per_worker)
        def _(ci):
            ch = first + ci
            pltpu.sync_copy(d_hbm.at[ch], idx_v)
            pltpu.sync_copy(x_hbm.at[pl.ds(pl.multiple_of(ch * chunk, 8), chunk)], rows_v)
            copies = [pltpu.async_copy(rows_v, o_hbm.at[idx_v.at[k]], sem) for k in range(TOP_K)]
            for cp in copies:
                cp.wait()

    return pl.kernel(
        body, out_type=jax.ShapeDtypeStruct((n_rows, PACK_W), jnp.int32), mesh=_sc_mesh(),
        scratch_types=[pltpu.VMEM((TOP_K, chunk), jnp.int32), pltpu.VMEM((chunk, PACK_W), jnp.int32),
                       pltpu.SemaphoreType.DMA],
        name="sc_dispatch")(xp, dest3)


def _sc_gather(table, idx2):
    n_chunks, chunk = idx2.shape
    per_worker = n_chunks // SC_WORKERS
    assert per_worker * SC_WORKERS == n_chunks and chunk % 8 == 0

    def body(t_hbm, i_hbm, o_hbm, idx_v, rows_v, sem):
        worker = _sc_worker()
        pltpu.sync_copy(i_hbm.at[worker], idx_v)

        @pl.loop(0, per_worker)
        def _(ci):
            pltpu.async_copy(t_hbm.at[idx_v.at[ci]], rows_v, sem).wait()
            row0 = (worker * per_worker + ci) * chunk
            pltpu.sync_copy(rows_v, o_hbm.at[pl.ds(pl.multiple_of(row0, 8), chunk)])

    return pl.kernel(
        body, out_type=jax.ShapeDtypeStruct((n_chunks * chunk, PACK_W), jnp.int32), mesh=_sc_mesh(),
        scratch_types=[pltpu.VMEM((per_worker, chunk), jnp.int32), pltpu.VMEM((chunk, PACK_W), jnp.int32),
                       pltpu.SemaphoreType.DMA],
        name="sc_gather")(table, idx2.reshape(SC_WORKERS, per_worker, chunk))


def _final_kernel(x_ref, y_ref, g_ref, nf_ref, o_ref):
    g = g_ref[...]
    hi = jnp.zeros((x_ref.shape[0], PACK_W), _F32)
    lo = jnp.zeros((x_ref.shape[0], PACK_W), _F32)
    for k in range(TOP_K):
        y_hi, y_lo = _unpack_rows(y_ref[k])
        hi = hi + g[:, k:k + 1] * y_hi
        lo = lo + g[:, k:k + 1] * y_lo
    x = x_ref[...] + jnp.concatenate([hi, lo], axis=1)
    o_ref[...] = x * lax.rsqrt(jnp.mean(x * x, axis=-1, keepdims=True) + EPS) * nf_ref[...]


def _final(xmid, yg, gates_t, norm_f, *, tm, first_block):
    t = xmid.shape[0]
    return pl.pallas_call(
        _final_kernel, grid=(t // tm,),
        in_specs=[pl.BlockSpec((tm, D_MODEL), lambda i: (i, 0)),
                  pl.BlockSpec((TOP_K, tm, PACK_W), lambda i: (0, i + first_block, 0)),
                  pl.BlockSpec((tm, TOP_K), lambda i: (i + first_block, 0)),
                  pl.BlockSpec((1, D_MODEL), lambda i: (0, 0))],
        out_specs=pl.BlockSpec((tm, D_MODEL), lambda i: (i, 0)),
        out_shape=jax.ShapeDtypeStruct((t, D_MODEL), _F32),
        compiler_params=_cparams(("parallel",)), name="final",
    )(xmid, yg, gates_t, norm_f)


def _pack_state(c, n, m):
    nb = c.shape[0]
    ncol = jnp.zeros((nb, ML_HEADS, ML_DK, ML_DV), _F32).at[..., 0].set(n.astype(_F32))
    cn = jnp.concatenate([c.astype(_F32), ncol], axis=-1)
    return cn, jnp.broadcast_to(m.astype(_F32)[:, :, None, None], (nb, ML_HEADS, 8, HEAD_W))


def _unpack_state(cn, m):
    return cn[..., :ML_DV], cn[..., ML_DV], m[:, :, 0, 0]


def kernel(x_prompt, x_sample, cache_k, cache_v, state_C, state_n, state_m, norm1, w_in, b_igate, b_fgate,
           lambda_q1, lambda_k1, lambda_q2, lambda_k2, subln, w_out, norm2, w_router, b_router, w_gate_up,
           b_gate_up, w_down, b_down, norm_f):
    nb, s, _ = x_prompt.shape
    db, ls, _ = x_sample.shape
    past = cache_k.shape[2]
    depth = w_in.shape[0]
    assert depth == 1 and s % PROJ_TILE == 0 and (db * ls) % 8 == 0
    l = 0
    lam_init = 0.8 - 0.6 * math.exp(-0.3 * l)
    t_p, t_s = nb * s, db * ls

    w = w_in[l]
    gate_cols = jnp.pad(w[:, 7 * GROUP_W:], ((0, 0), (0, HEAD_W - 2 * ML_HEADS)))
    w_nat = jnp.concatenate([w[:, :7 * GROUP_W], gate_cols], axis=1).astype(_BF16)
    b_gates = jnp.pad(jnp.concatenate([b_igate[l], b_fgate[l]]).astype(_F32), (0, HEAD_W - 2 * ML_HEADS))[None]
    inv = ROPE_THETA ** (-jnp.arange(0, DA_QK, 2, dtype=_F32) / DA_QK)
    inv_full = jnp.tile(inv, HEAD_W // (DA_QK // 2))[None]
    lamv = jnp.stack([lambda_q1[l], lambda_k1[l], lambda_q2[l], lambda_k2[l]]).astype(_F32)
    g1 = norm1[l][None]

    (q_p, kf_p, kb_p, vf_p, vt_p, mq_p, mk_p, mv_p, mo_p, gc_p, gr_p) = _proj(
        x_prompt, g1, w_nat, b_gates, inv_full, tm=PROJ_TILE, pos_base=0, pos_mod=s)
    (q_s, kf_s, kb_s, vf_s, _, mq_s, mk_s, mv_s, mo_s, gc_s, gr_s) = _proj(
        x_sample.reshape(1, t_s, D_MODEL), g1, w_nat, b_gates, inv_full, tm=t_s, pos_base=past, pos_mod=ls)

    da_p = _attn_prompt(lamv, q_p, kb_p, vt_p, lam_init=lam_init)
    per_b = lambda a: a.reshape(db, ls, a.shape[-1])
    kct = jnp.transpose(cache_k[l], (0, 2, 3, 4, 1)).reshape(db, GROUP_W, past)
    vc = cache_v[l].reshape(db, past * DA_HEADS, DA_V)
    vn = vf_s.reshape(db, ls, GROUP_W).astype(_BF16)
    da_s = _attn_sample(lamv, per_b(q_s), per_b(kb_s), vn, kct, vc, lam_init=lam_init)

    zeros_state = _pack_state(jnp.zeros((nb, ML_HEADS, ML_DK, ML_DV), _F32), jnp.zeros((nb, ML_HEADS, ML_DK), _F32),
                              jnp.zeros((nb, ML_HEADS), _F32))
    h_p, cn_p, m_p = _mlstm(mq_p, mk_p, mv_p, gc_p, gr_p, *zeros_state, chunk=CHUNK, block=PROJ_TILE)
    gr_sb = gr_s.reshape(8, db, ls).transpose(1, 0, 2)
    h_s, cn_s, m_s = _mlstm(per_b(mq_s), per_b(mk_s), per_b(mv_s), per_b(gc_s), gr_sb,
                            *_pack_state(state_C[l], state_n[l], state_m[l]), chunk=ls, block=ls)
    c_p, n_p, mm_p = _unpack_state(cn_p, m_p)
    c_s, n_s, mm_s = _unpack_state(cn_s, m_s)

    wo = w_out[l].astype(_BF16)
    w_rt = w_router[l].T.astype(_BF16)
    b_r = b_router[l].astype(_F32)[:, None]
    mix = functools.partial(_mix, subln=subln[l][None], w_out=wo, norm2=norm2[l][None], w_rt=w_rt, b_r=b_r,
                            lam_init=lam_init)
    flat = lambda a: a.reshape(-1, a.shape[-1])
    xm_p, xp_p, idx_p, gate_p, rank_p, cnt_p = mix(
        flat(da_p), flat(h_p), flat(mo_p), flat(x_prompt), cnt0=jnp.zeros((N_EXPERTS, 1), _F32), tm=PROJ_TILE)
    xm_s, xp_s, idx_s, gate_s, rank_s, cnt = mix(
        flat(da_s), flat(h_s), flat(mo_s), flat(x_sample), cnt0=cnt_p, tm=t_s)

    t_all = t_p + t_s
    rows = t_all * TOP_K
    n_blocks = -(-rows // MOE_BLOCK) + N_EXPERTS
    counts = cnt[:, 0].astype(jnp.int32)
    padded = (counts + MOE_BLOCK - 1) // MOE_BLOCK * MOE_BLOCK
    pend = jnp.cumsum(padded)
    pstart = pend - padded
    idx = jnp.concatenate([idx_p, idx_s], axis=1)
    experts = jnp.arange(N_EXPERTS, dtype=jnp.int32)[:, None, None]
    first_row = jnp.sum(jnp.where(idx[None] == experts, pstart[:, None, None], 0), axis=0)
    dest = first_row + jnp.concatenate([rank_p, rank_s], axis=1)
    n_used = (pend[-1] // MOE_BLOCK).astype(jnp.int32)[None]
    blk_lo = jnp.arange(n_blocks, dtype=jnp.int32) * MOE_BLOCK
    blk_e = jnp.minimum(jnp.sum((pend[None, :] <= blk_lo[:, None]).astype(jnp.int32), axis=1), N_EXPERTS - 1)

    assert t_all % (SC_WORKERS * DISPATCH_CHUNK) == 0 and rows % (SC_WORKERS * COMBINE_CHUNK) == 0
    dest3 = dest.reshape(TOP_K, t_all // DISPATCH_CHUNK, DISPATCH_CHUNK).transpose(1, 0, 2)
    xb = _sc_dispatch(jnp.concatenate([xp_p, xp_s], axis=0), dest3, n_blocks * MOE_BLOCK)
    yb = _moe(blk_e, n_used, xb, w_gate_up[l], b_gate_up[l][:, None, :], w_down[l], b_down[l][:, None, :])
    yg = _sc_gather(yb, dest.reshape(rows // COMBINE_CHUNK, COMBINE_CHUNK)).reshape(TOP_K, t_all, PACK_W)
    gates_t = jnp.concatenate([gate_p, gate_s], axis=1).T
    nf = norm_f[None]
    y_p = _final(xm_p, yg, gates_t, nf, tm=PROJ_TILE, first_block=0).reshape(nb, s, D_MODEL)
    y_s = _final(xm_s, yg, gates_t, nf, tm=t_s, first_block=t_p // t_s).reshape(db, ls, D_MODEL)

    st = lambda a: a[None]
    return (y_p, y_s,
            st(kf_p.reshape(nb, s, DA_HEADS, 2, DA_QK)), st(vf_p.reshape(nb, s, DA_HEADS, DA_V)),
            st(c_p), st(n_p), st(mm_p),
            st(kf_s.reshape(db, ls, DA_HEADS, 2, DA_QK)), st(vf_s.reshape(db, ls, DA_HEADS, DA_V)),
            st(c_s.astype(state_C.dtype)), st(n_s.astype(state_n.dtype)), st(mm_s.astype(state_m.dtype)))
```

```python
import functools
import math

import jax
import jax.numpy as jnp
import numpy as np
from jax import lax
from jax.experimental import pallas as pl
from jax.experimental.pallas import tpu as pltpu
from jax.experimental.pallas import tpu_sc as plsc

D_MODEL = 1024
CHUNK = 64
DA_HEADS = 4
DA_QK = 64
DA_V = 128
ML_HEADS = 4
ML_DK = 128
ML_DV = 128
HEAD_W = 128
GROUP_W = 512
ROPE_THETA = 10000.0
N_EXPERTS = 32
TOP_K = 4
D_FF = 1024
SWIGLU_LIMIT = 7.0
SWIGLU_ALPHA = 1.702
EPS = 1e-6
NEG_BIG = -1e30

LOG2E = math.log2(math.e)
VT_ROWS = 144
PROJ_TILE = 512
ATTN_TQ = 256
MOE_BLOCK = 512
FF_CHUNK = 512
PACK_W = D_MODEL // 2
SC_CORES = 2
SC_SUBCORES = 16
SC_WORKERS = SC_CORES * SC_SUBCORES
DISPATCH_CHUNK = 40
COMBINE_CHUNK = 80
VMEM_LIMIT = 56 * 1024 * 1024

_F32 = jnp.float32
_BF16 = jnp.bfloat16


def _cparams(sem):
    return pltpu.CompilerParams(dimension_semantics=sem, vmem_limit_bytes=VMEM_LIMIT)


def _dot(a, b, dims=(((1,), (0,)), ((), ())), precision=None):
    return lax.dot_general(a, b, dims, precision=precision, preferred_element_type=_F32)


_NT = (((1,), (1,)), ((), ()))
_TN = (((0,), (0,)), ((), ()))


def _pack_rows(x):
    half = x.shape[1] // 2
    bits = lambda v: lax.bitcast_convert_type(v.astype(_BF16).astype(_F32), jnp.int32)
    return bits(x[:, :half]) | lax.shift_right_logical(bits(x[:, half:]), 16)


def _unpack_rows(p):
    hi = lax.bitcast_convert_type(p & jnp.int32(-65536), _F32)
    lo = lax.bitcast_convert_type(lax.shift_left(p, 16), _F32)
    return hi, lo


def _proj_kernel(x_ref, g_ref, w_ref, bg_ref, inv_ref,
                 q_ref, kf_ref, kb_ref, vf_ref, vt_ref, mq_ref, mk_ref, mv_ref, mo_ref, gc_ref, gr_ref,
                 *, tm, pos_base, pos_mod):
    x = x_ref[0]
    xn = (x * lax.rsqrt(jnp.mean(x * x, axis=-1, keepdims=True) + EPS) * g_ref[...]).astype(_BF16)

    row = pl.program_id(1) * tm + lax.broadcasted_iota(jnp.int32, (tm, 1), 0)
    pos = (pos_base + row % pos_mod).astype(_F32)
    ang = pos * inv_ref[...]
    cos = jnp.cos(ang)
    sin = jnp.sin(ang)
    lane = lax.broadcasted_iota(jnp.int32, (1, HEAD_W), 1)
    first = (lane % DA_QK) < (DA_QK // 2)
    sin = jnp.where(first, -sin, sin)

    def rope(z):
        rot = jnp.where(first, pltpu.roll(z, HEAD_W - DA_QK // 2, 1), pltpu.roll(z, DA_QK // 2, 1))
        return z * cos + rot * sin

    def group(c, width=GROUP_W):
        return _dot(xn, w_ref[:, c * GROUP_W:c * GROUP_W + width])

    def put_q(zq):
        for h in range(DA_HEADS):
            sl = slice(h * HEAD_W, (h + 1) * HEAD_W)
            q_ref[0, :, sl] = (rope(zq[:, sl]) * (DA_QK ** -0.5 * LOG2E)).astype(_BF16)

    def put_k(zk):
        for h in range(DA_HEADS):
            sl = slice(h * HEAD_W, (h + 1) * HEAD_W)
            rk = rope(zk[:, sl])
            kf_ref[0, :, sl] = rk
            kb_ref[0, :, sl] = rk.astype(_BF16)

    def put_v(zv):
        for h in range(DA_HEADS):
            vf_ref[0, pl.ds(h, tm, stride=DA_HEADS), :] = zv[:, h * DA_V:(h + 1) * DA_V]
        zvt = zv.T.astype(_BF16)
        ones = jnp.ones((VT_ROWS - DA_V, tm), _BF16)
        vt_ref[0, 0] = jnp.concatenate(
            [part for h in range(DA_HEADS) for part in (zvt[h * DA_V:(h + 1) * DA_V], ones)], axis=0)

    def put_mq(z):
        mq_ref[0] = z.astype(_BF16)

    def put_mk(z):
        mk_ref[0] = (z * (ML_DK ** -0.5)).astype(_BF16)

    def put_mv(z):
        mv_ref[0] = z.astype(_BF16)

    def put_mo(z):
        mo_ref[0] = z

    def put_gates(z):
        zg = z + bg_ref[...]
        lane_g = lax.broadcasted_iota(jnp.int32, zg.shape, 1)
        logsig = jnp.minimum(zg, 0.0) - jnp.log(1.0 + jnp.exp(-jnp.abs(zg)))
        gates = jnp.where(lane_g < ML_HEADS, zg, logsig)
        gc_ref[0] = gates[:, :8]
        gr_ref[0] = gates.T[:8, :]

    sinks = [(put_q, GROUP_W), (put_k, GROUP_W), (put_v, GROUP_W), (put_mq, GROUP_W), (put_mk, GROUP_W),
             (put_mv, GROUP_W), (put_mo, GROUP_W), (put_gates, HEAD_W)]
    nxt = group(0, sinks[0][1])
    for c, (sink, _) in enumerate(sinks):
        cur = nxt
        if c + 1 < len(sinks):
            nxt = group(c + 1, sinks[c + 1][1])
        sink(cur)


def _proj(x3, norm1, w_nat, b_gates, inv_full, *, tm, pos_base, pos_mod):
    nb, s, _ = x3.shape
    nt = s // tm
    wcols = w_nat.shape[1]
    tok = lambda w, dt: jax.ShapeDtypeStruct((nb, s, w), dt)
    tok_spec = lambda w: pl.BlockSpec((1, tm, w), lambda b, i: (b, i, 0))
    const = lambda shape: pl.BlockSpec(shape, lambda b, i: (0,) * len(shape))
    out_shape = [tok(GROUP_W, _BF16), tok(GROUP_W, _F32), tok(GROUP_W, _BF16),
                 jax.ShapeDtypeStruct((nb, s * DA_HEADS, DA_V), _F32),
                 jax.ShapeDtypeStruct((nb, nt, DA_HEADS * VT_ROWS, tm), _BF16),
                 tok(GROUP_W, _BF16), tok(GROUP_W, _BF16), tok(GROUP_W, _BF16), tok(GROUP_W, _F32),
                 tok(8, _F32), jax.ShapeDtypeStruct((nb, 8, s), _F32)]
    out_specs = [tok_spec(GROUP_W)] * 3 + [pl.BlockSpec((1, tm * DA_HEADS, DA_V), lambda b, i: (b, i, 0)),
                                           pl.BlockSpec((1, 1, DA_HEADS * VT_ROWS, tm), lambda b, i: (b, i, 0, 0))] \
        + [tok_spec(GROUP_W)] * 4 + [tok_spec(8), pl.BlockSpec((1, 8, tm), lambda b, i: (b, 0, i))]
    return pl.pallas_call(
        functools.partial(_proj_kernel, tm=tm, pos_base=pos_base, pos_mod=pos_mod),
        grid=(nb, nt),
        in_specs=[tok_spec(D_MODEL), const((1, D_MODEL)), const((D_MODEL, wcols)), const((1, HEAD_W)),
                  const((1, HEAD_W))],
        out_specs=out_specs, out_shape=out_shape,
        compiler_params=_cparams(("parallel", "parallel")), name="proj",
    )(x3, norm1, w_nat, b_gates, inv_full)


def _lambda(lam_ref, lam_init):
    lv = lam_ref[...]
    s1 = jnp.sum(lv[0:1] * lv[1:2], axis=-1, keepdims=True)
    s2 = jnp.sum(lv[2:3] * lv[3:4], axis=-1, keepdims=True)
    return jnp.exp(s1) - jnp.exp(s2) + lam_init


def _split_components(q):
    lane = lax.broadcasted_iota(jnp.int32, q.shape, 1)
    zero = jnp.zeros_like(q)
    return jnp.concatenate([jnp.where(lane < DA_QK, q, zero), jnp.where(lane >= DA_QK, q, zero)], axis=0)


def _attn_kernel(lam_ref, q_ref, k_ref, vt_ref, o_ref, acc_ref, s_ref, *, tq, tk, lam_init):
    i = pl.program_id(1)
    heads = range(DA_HEADS)
    qz = [_split_components(q_ref[0, :, h * HEAD_W:(h + 1) * HEAD_W]) for h in heads]
    acc_ref[...] = jnp.zeros_like(acc_ref)

    def scores(h, j):
        k_t = k_ref[0, pl.ds(pl.multiple_of(j * tk, tk), tk), h * HEAD_W:(h + 1) * HEAD_W]
        return _dot(k_t, qz[h], _NT)

    def step(j, ms, last):
        if last:
            kpos = j * tk + lax.broadcasted_iota(jnp.int32, (tk, 1), 0)
            qpos = i * tq + lax.broadcasted_iota(jnp.int32, (1, 2 * tq), 1) % tq
            visible = kpos < (qpos // CHUNK + 1) * CHUNK
        out = []
        s = s_ref[...]
        for h in heads:
            s_next = None
            if h + 1 < DA_HEADS:
                s_next = scores(h + 1, j)
            elif not last:
                s_next = scores(0, j + 1)
            if last:
                s = jnp.where(visible, s, NEG_BIG)
            m_new = jnp.maximum(ms[h], jnp.max(s, axis=0, keepdims=True))
            alpha = jnp.exp2(ms[h] - m_new)
            p = jnp.exp2(s - m_new).astype(_BF16)
            acc_ref[h] = alpha * acc_ref[h] + _dot(vt_ref[0, j, h * VT_ROWS:(h + 1) * VT_ROWS, :], p)
            out.append(m_new)
            s = s_next
        if not last:
            s_ref[...] = s
        return tuple(out)

    n_full = (i * tq) // tk
    s_ref[...] = scores(0, 0)
    init = tuple(jnp.full((1, 2 * tq), NEG_BIG, _F32) for _ in heads)
    ms = lax.fori_loop(0, n_full, lambda j, c: step(j, c, False), init)
    step(n_full, ms, True)

    lam = _lambda(lam_ref, lam_init)
    for h in heads:
        acc = acc_ref[h]
        o = acc[:DA_V] / acc[DA_V:DA_V + 1]
        o_ref[0, :, h * HEAD_W:(h + 1) * HEAD_W] = (o[:, :tq] - lam * o[:, tq:]).T


def _attn_prompt(lamv, q, kb, vt, *, lam_init):
    nb, s, _ = q.shape
    nk, tk = vt.shape[1], vt.shape[3]
    tq = ATTN_TQ
    return pl.pallas_call(
        functools.partial(_attn_kernel, tq=tq, tk=tk, lam_init=lam_init),
        grid=(nb, s // tq),
        in_specs=[pl.BlockSpec((4, DA_QK), lambda b, i: (0, 0)),
                  pl.BlockSpec((1, tq, GROUP_W), lambda b, i: (b, i, 0)),
                  pl.BlockSpec((1, s, GROUP_W), lambda b, i: (b, 0, 0)),
                  pl.BlockSpec((1, nk, DA_HEADS * VT_ROWS, tk), lambda b, i: (b, 0, 0, 0))],
        out_specs=pl.BlockSpec((1, tq, GROUP_W), lambda b, i: (b, i, 0)),
        out_shape=jax.ShapeDtypeStruct((nb, s, GROUP_W), _F32),
        scratch_shapes=[pltpu.VMEM((DA_HEADS, VT_ROWS, 2 * tq), _F32), pltpu.VMEM((tk, 2 * tq), _F32)],
        compiler_params=_cparams(("parallel", "arbitrary")), name="attn",
    )(lamv, q, kb, vt)


def _attn_sample_kernel(lam_ref, q_ref, kn_ref, vn_ref, kc_ref, vc_ref, o_ref, *, lq, past, lam_init):
    lam = _lambda(lam_ref, lam_init)
    for h in range(DA_HEADS):
        sl = slice(h * HEAD_W, (h + 1) * HEAD_W)
        qz = _split_components(q_ref[0, :, sl])
        kct = kc_ref[0, sl, :].astype(_BF16)
        vc = vc_ref[0, pl.ds(h, past, stride=DA_HEADS), :].astype(_BF16)
        s_c = _dot(qz, kct)
        s_n = _dot(qz, kn_ref[0, :, sl], _NT)
        m = jnp.maximum(jnp.max(s_c, axis=-1, keepdims=True), jnp.max(s_n, axis=-1, keepdims=True))
        p_c = jnp.exp2(s_c - m)
        p_n = jnp.exp2(s_n - m)
        l = jnp.sum(p_c, axis=-1, keepdims=True) + jnp.sum(p_n, axis=-1, keepdims=True)
        o = (_dot(p_c.astype(_BF16), vc) + _dot(p_n.astype(_BF16), vn_ref[0, :, sl])) / l
        o_ref[0, :, sl] = o[:lq] - lam * o[lq:]


def _attn_sample(lamv, q, kn, vn, kct, vc, *, lam_init):
    nb, lq, _ = q.shape
    past = kct.shape[2]
    new = pl.BlockSpec((1, lq, GROUP_W), lambda b: (b, 0, 0))
    return pl.pallas_call(
        functools.partial(_attn_sample_kernel, lq=lq, past=past, lam_init=lam_init),
        grid=(nb,),
        in_specs=[pl.BlockSpec((4, DA_QK), lambda b: (0, 0)), new, new, new,
                  pl.BlockSpec((1, GROUP_W, past), lambda b: (b, 0, 0)),
                  pl.BlockSpec((1, past * DA_HEADS, DA_V), lambda b: (b, 0, 0))],
        out_specs=new, out_shape=jax.ShapeDtypeStruct((nb, lq, GROUP_W), _F32),
        compiler_params=_cparams(("parallel",)), name="attn_sample",
    )(lamv, q, kn, vn, kct, vc)


def _mlstm_kernel(q_ref, k_ref, v_ref, gc_ref, gr_ref, cn0_ref, m0_ref, h_ref, cn_ref, m_ref,
                  cn_sc, m_sc, *, chunk, nchunks):
    j = pl.program_id(1)

    @pl.when(j == 0)
    def _():
        cn_sc[...] = cn0_ref[0]
        m_sc[...] = m0_ref[0]

    L = chunk
    r_i = lax.broadcasted_iota(jnp.int32, (L, L), 0)
    c_i = lax.broadcasted_iota(jnp.int32, (L, L), 1)
    causal = r_i >= c_i
    tril = causal.astype(_F32)
    triu = (r_i <= c_i).astype(_F32)
    ones_col = (lax.broadcasted_iota(jnp.int32, (L, ML_DV), 1) == 0).astype(_BF16)
    hi = lax.Precision.HIGHEST

    for c in range(nchunks):
        rows = slice(c * L, (c + 1) * L)
        gcol = gc_ref[0, rows, :]
        grow = gr_ref[0, :, rows]
        fc_all = _dot(tril, gcol, precision=hi)
        fr_all = _dot(grow, triu, precision=hi)
        for hh in range(ML_HEADS):
            sl = slice(hh * HEAD_W, (hh + 1) * HEAD_W)
            m = m_sc[hh][0:1, 0:1]
            fc = fc_all[:, ML_HEADS + hh:ML_HEADS + hh + 1]
            fr = fr_all[ML_HEADS + hh:ML_HEADS + hh + 1, :]
            igc = gcol[:, hh:hh + 1]
            igr = grow[hh:hh + 1, :]
            a = jnp.where(causal, fc - fr + igr, -jnp.inf)
            b = fc + m
            m_t = jnp.maximum(b, jnp.max(a, axis=-1, keepdims=True))
            w = jnp.exp(a - m_t)
            sc = jnp.exp(b - m_t)
            q = q_ref[0, rows, sl]
            k = k_ref[0, rows, sl]
            v = v_ref[0, rows, sl]
            wqk = w * _dot(q, k, _NT)
            cn = cn_sc[hh]
            inter = _dot(q, cn.astype(_BF16))
            num = _dot(wqk.astype(_BF16), v) + sc * inter[:, :ML_DV]
            den = jnp.sum(wqk, axis=-1, keepdims=True) + sc * inter[:, ML_DV:ML_DV + 1]
            h_ref[0, rows, sl] = num / jnp.maximum(jnp.abs(den), jnp.exp(-m_t))

            fl = fc[L - 1:L, :]
            g = fl - fc + igc
            m_new = jnp.maximum(fl + m, jnp.max(g, axis=0, keepdims=True))
            ws = jnp.exp(g - m_new)
            decay = jnp.exp(fl + m - m_new)
            kw = (k.astype(_F32) * ws).astype(_BF16)
            v1 = jnp.concatenate([v, ones_col], axis=1)
            cn_sc[hh] = decay * cn + _dot(kw, v1, _TN)
            m_sc[hh] = jnp.broadcast_to(m_new, m_sc.shape[1:])


    @pl.when(j == pl.num_programs(1) - 1)
    def _():
        cn_ref[0] = cn_sc[...]
        m_ref[0] = m_sc[...]


def _mlstm(q, k, v, gc, gr, cn0, m0, *, chunk, block):
    nb, s, _ = q.shape
    tok = lambda w: pl.BlockSpec((1, block, w), lambda b, j: (b, j, 0))
    state = pl.BlockSpec((1, ML_HEADS, ML_DK, 2 * ML_DV), lambda b, j: (b, 0, 0, 0))
    mspec = pl.BlockSpec((1, ML_HEADS, 8, HEAD_W), lambda b, j: (b, 0, 0, 0))
    return pl.pallas_call(
        functools.partial(_mlstm_kernel, chunk=chunk, nchunks=block // chunk),
        grid=(nb, s // block),
        in_specs=[tok(GROUP_W), tok(GROUP_W), tok(GROUP_W), tok(8),
                  pl.BlockSpec((1, 8, block), lambda b, j: (b, 0, j)), state, mspec],
        out_specs=[tok(GROUP_W), state, mspec],
        out_shape=[jax.ShapeDtypeStruct((nb, s, GROUP_W), _F32),
                   jax.ShapeDtypeStruct(cn0.shape, _F32), jax.ShapeDtypeStruct(m0.shape, _F32)],
        scratch_shapes=[pltpu.VMEM((ML_HEADS, ML_DK, 2 * ML_DV), _F32), pltpu.VMEM((ML_HEADS, 8, HEAD_W), _F32)],
        compiler_params=_cparams(("parallel", "arbitrary")), name="mlstm",
    )(q, k, v, gc, gr, cn0, m0)


def _mix_kernel(da_ref, ml_ref, mo_ref, x_ref, subln_ref, wo_ref, g2_ref, wr_ref, br_ref, cnt0_ref,
                xmid_ref, xp_ref, idx_ref, gate_ref, rank_ref, cnt_ref, cnt_sc, *, tm, lam_init):
    step = pl.program_id(0)

    @pl.when(step == 0)
    def _():
        cnt_sc[...] = cnt0_ref[...]

    parts = []
    for h in range(DA_HEADS):
        d = da_ref[:, h * HEAD_W:(h + 1) * HEAD_W]
        d = d * lax.rsqrt(jnp.mean(d * d, axis=-1, keepdims=True) + EPS) * subln_ref[...]
        parts.append((d * (1.0 - lam_init)).astype(_BF16))
    parts.append((jax.nn.sigmoid(mo_ref[...]) * ml_ref[...]).astype(_BF16))
    xm = x_ref[...] + _dot(jnp.concatenate(parts, axis=1), wo_ref[...])
    xmid_ref[...] = xm
    xn_f = xm * lax.rsqrt(jnp.mean(xm * xm, axis=-1, keepdims=True) + EPS) * g2_ref[...]
    xp_ref[...] = _pack_rows(xn_f)
    xn = xn_f.astype(_BF16)

    logits = _dot(wr_ref[...], xn, _NT) + br_ref[...]
    e_i = lax.broadcasted_iota(jnp.int32, logits.shape, 0)
    member = jnp.zeros(logits.shape, jnp.bool_)
    vals, idxs = [], []
    for _ in range(TOP_K):
        mx = jnp.max(logits, axis=0, keepdims=True)
        sel = jnp.min(jnp.where(logits == mx, e_i, N_EXPERTS), axis=0, keepdims=True)
        hit = e_i == sel
        member = member | hit
        logits = jnp.where(hit, -jnp.inf, logits)
        vals.append(mx)
        idxs.append(sel)
    ex = [jnp.exp(v - vals[0]) for v in vals]
    tot = ex[0] + ex[1] + ex[2] + ex[3]
    idx_ref[...] = jnp.concatenate(idxs, axis=0)
    gate_ref[...] = jnp.concatenate([e / tot for e in ex], axis=0)

    upper = (lax.broadcasted_iota(jnp.int32, (tm, tm), 0) < lax.broadcasted_iota(jnp.int32, (tm, tm), 1))
    memf = member.astype(_F32)
    before = _dot(memf.astype(_BF16), upper.astype(_BF16)) + cnt_sc[...]
    rank_ref[...] = jnp.concatenate(
        [jnp.sum(jnp.where(e_i == s, before, 0.0), axis=0, keepdims=True) for s in idxs], axis=0).astype(jnp.int32)
    cnt_sc[...] += jnp.sum(memf, axis=1, keepdims=True)
    cnt_ref[...] = cnt_sc[...]


def _mix(da, ml, mo, x, subln, w_out, norm2, w_rt, b_r, cnt0, *, tm, lam_init):
    t = x.shape[0]
    tok = lambda w: pl.BlockSpec((tm, w), lambda i: (i, 0))
    const = lambda shape: pl.BlockSpec(shape, lambda i: (0,) * len(shape))
    lane_tok = pl.BlockSpec((TOP_K, tm), lambda i: (0, i))
    return pl.pallas_call(
        functools.partial(_mix_kernel, tm=tm, lam_init=lam_init),
        grid=(t // tm,),
        in_specs=[tok(GROUP_W), tok(GROUP_W), tok(GROUP_W), tok(D_MODEL), const((1, DA_V)),
                  const((D_MODEL, D_MODEL)), const((1, D_MODEL)), const((N_EXPERTS, D_MODEL)),
                  const((N_EXPERTS, 1)), const((N_EXPERTS, 1))],
        out_specs=[tok(D_MODEL), tok(PACK_W), lane_tok, lane_tok, lane_tok, const((N_EXPERTS, 1))],
        out_shape=[jax.ShapeDtypeStruct((t, D_MODEL), _F32), jax.ShapeDtypeStruct((t, PACK_W), jnp.int32),
                   jax.ShapeDtypeStruct((TOP_K, t), jnp.int32), jax.ShapeDtypeStruct((TOP_K, t), _F32),
                   jax.ShapeDtypeStruct((TOP_K, t), jnp.int32), jax.ShapeDtypeStruct((N_EXPERTS, 1), _F32)],
        scratch_shapes=[pltpu.VMEM((N_EXPERTS, 1), _F32)],
        compiler_params=_cparams(("arbitrary",)), name="mix",
    )(da, ml, mo, x, subln, w_out, norm2, w_rt, b_r, cnt0)


def _moe_kernel(be_ref, nu_ref, x_ref, wgu_ref, bgu_ref, wd_ref, bd_ref, y_ref, wgu_sc, wd_sc):
    j = pl.program_id(0)
    e_prev = be_ref[jnp.maximum(j - 1, 0)]
    fresh = (j == 0) | (be_ref[j] != e_prev)

    @pl.when(fresh & (j < nu_ref[0]))
    def _():
        rows = 128
        def cast_gu(r, _):
            sl = pl.ds(pl.multiple_of(r * rows, rows), rows)
            wgu_sc[sl, :] = wgu_ref[0, sl, :].astype(_BF16)
            return 0
        lax.fori_loop(0, D_MODEL // rows, cast_gu, 0)
        def cast_d(r, _):
            sl = pl.ds(pl.multiple_of(r * rows, rows), rows)
            wd_sc[sl, :] = wd_ref[0, sl, :].astype(_BF16)
            return 0
        lax.fori_loop(0, D_FF // rows, cast_d, 0)

    @pl.when(j >= nu_ref[0])
    def _():
        y_ref[...] = jnp.zeros_like(y_ref)

    @pl.when(j < nu_ref[0])
    def _():
        hi, lo = _unpack_rows(x_ref[...])
        x = jnp.concatenate([hi.astype(_BF16), lo.astype(_BF16)], axis=1)
        n_chunks = D_FF // FF_CHUNK

        def gate_up(c):
            lo = c * FF_CHUNK
            return (_dot(x, wgu_sc[:, lo:lo + FF_CHUNK]), _dot(x, wgu_sc[:, D_FF + lo:D_FF + lo + FF_CHUNK]))

        acc = None
        nxt = gate_up(0)
        for c in range(n_chunks):
            lo = c * FF_CHUNK
            gate, up = nxt
            if c + 1 < n_chunks:
                nxt = gate_up(c + 1)
            gate = jnp.minimum(gate + bgu_ref[0, :, lo:lo + FF_CHUNK], SWIGLU_LIMIT)
            up = jnp.clip(up + bgu_ref[0, :, D_FF + lo:D_FF + lo + FF_CHUNK], -SWIGLU_LIMIT, SWIGLU_LIMIT)
            act = (up + 1.0) * (gate * jax.nn.sigmoid(SWIGLU_ALPHA * gate))
            down = _dot(act.astype(_BF16), wd_sc[lo:lo + FF_CHUNK, :])
            acc = down if acc is None else acc + down
        y_ref[...] = _pack_rows(acc + bd_ref[0])


def _moe(blk_e, n_used, xb, w_gu, b_gu, w_d, b_d):
    n_blocks = xb.shape[0] // MOE_BLOCK
    last = lambda j, nu: jnp.minimum(j, nu[0] - 1)
    row_spec = pl.BlockSpec((MOE_BLOCK, PACK_W), lambda j, be, nu: (last(j, nu), 0))
    exp_spec = lambda r, c: pl.BlockSpec((1, r, c), lambda j, be, nu: (be[last(j, nu)], 0, 0))
    grid_spec = pltpu.PrefetchScalarGridSpec(
        num_scalar_prefetch=2, grid=(n_blocks,),
        in_specs=[row_spec, exp_spec(D_MODEL, 2 * D_FF), exp_spec(1, 2 * D_FF), exp_spec(D_FF, D_MODEL),
                  exp_spec(1, D_MODEL)],
        out_specs=pl.BlockSpec((MOE_BLOCK, PACK_W), lambda j, be, nu: (j, 0)),
        scratch_shapes=[pltpu.VMEM((D_MODEL, 2 * D_FF), _BF16), pltpu.VMEM((D_FF, D_MODEL), _BF16)])
    return pl.pallas_call(
        _moe_kernel, grid_spec=grid_spec, out_shape=jax.ShapeDtypeStruct(xb.shape, jnp.int32),
        compiler_params=_cparams(("arbitrary",)), name="moe",
    )(blk_e, n_used, xb, w_gu, b_gu, w_d, b_d)


def _sc_mesh():
    return plsc.VectorSubcoreMesh(core_axis_name="c", subcore_axis_name="s")


def _sc_worker():
    return lax.axis_index("s") * SC_CORES + lax.axis_index("c")


def _sc_dispatch(xp, dest3, n_rows):
    n_chunks, _, chunk = dest3.shape
    per_worker = n_chunks // SC_WORKERS
    assert per_worker * SC_WORKERS == n_chunks and chunk % 8 == 0

    def body(x_hbm, d_hbm, o_hbm, idx_v, rows_a, rows_b, load_a, load_b, scat_a, scat_b):
        worker = _sc_worker()
        pltpu.sync_copy(d_hbm.at[worker], idx_v)

        def load(ci, rows, sem):
            row0 = (worker * per_worker + ci) * chunk
            return pltpu.make_async_copy(x_hbm.at[pl.ds(pl.multiple_of(row0, 8), chunk)], rows, sem)

        def scatters(ci, rows, sem):
            return [pltpu.make_async_copy(rows, o_hbm.at[idx_v.at[ci * TOP_K + k]], sem) for k in range(TOP_K)]

        def start(copies):
            for cp in copies:
                cp.start()

        def wait(copies):
            for cp in copies:
                cp.wait()

        load(0, rows_a, load_a).start()

        @pl.loop(0, per_worker // 2)
        def _(p):
            ca, cb = 2 * p, 2 * p + 1
            load(ca, rows_a, load_a).wait()
            load(cb, rows_b, load_b).start()
            start(scatters(ca, rows_a, scat_a))
            load(cb, rows_b, load_b).wait()
            wait(scatters(ca, rows_a, scat_a))
            load(jnp.minimum(ca + 2, per_worker - 1), rows_a, load_a).start()
            start(scatters(cb, rows_b, scat_b))
            wait(scatters(cb, rows_b, scat_b))

        last = per_worker - 1
        load(last, rows_a, load_a).wait()
        if per_worker % 2:
            start(scatters(last, rows_a, scat_a))
            wait(scatters(last, rows_a, scat_a))

    dma = pltpu.SemaphoreType.DMA
    return pl.kernel(
        body, out_type=jax.ShapeDtypeStruct((n_rows, PACK_W), jnp.int32), mesh=_sc_mesh(),
        scratch_types=[pltpu.VMEM((per_worker * TOP_K, chunk), jnp.int32), pltpu.VMEM((chunk, PACK_W), jnp.int32),
                       pltpu.VMEM((chunk, PACK_W), jnp.int32), dma, dma, dma, dma],
        name="sc_dispatch")(xp, dest3.reshape(SC_WORKERS, per_worker * TOP_K, chunk))


def _sc_gather(table, idx2):
    n_chunks, chunk = idx2.shape
    per_worker = n_chunks // SC_WORKERS
    assert per_worker * SC_WORKERS == n_chunks and chunk % 8 == 0

    def body(t_hbm, i_hbm, o_hbm, idx_v, rows_a, rows_b, gath_a, gath_b, put_a, put_b):
        worker = _sc_worker()
        pltpu.sync_copy(i_hbm.at[worker], idx_v)

        def gather(ci, rows, sem):
            return pltpu.make_async_copy(t_hbm.at[idx_v.at[ci]], rows, sem)

        def put(ci, rows, sem):
            row0 = (worker * per_worker + ci) * chunk
            return pltpu.make_async_copy(rows, o_hbm.at[pl.ds(pl.multiple_of(row0, 8), chunk)], sem)

        gather(0, rows_a, gath_a).start()

        @pl.loop(0, per_worker // 2)
        def _(p):
            ca, cb = 2 * p, 2 * p + 1
            gather(ca, rows_a, gath_a).wait()
            gather(cb, rows_b, gath_b).start()
            put(ca, rows_a, put_a).start()
            gather(cb, rows_b, gath_b).wait()
            put(ca, rows_a, put_a).wait()
            gather(jnp.minimum(ca + 2, per_worker - 1), rows_a, gath_a).start()
            put(cb, rows_b, put_b).start()
            put(cb, rows_b, put_b).wait()

        last = per_worker - 1
        gather(last, rows_a, gath_a).wait()
        if per_worker % 2:
            put(last, rows_a, put_a).start()
            put(last, rows_a, put_a).wait()

    dma = pltpu.SemaphoreType.DMA
    return pl.kernel(
        body, out_type=jax.ShapeDtypeStruct((n_chunks * chunk, PACK_W), jnp.int32), mesh=_sc_mesh(),
        scratch_types=[pltpu.VMEM((per_worker, chunk), jnp.int32), pltpu.VMEM((chunk, PACK_W), jnp.int32),
                       pltpu.VMEM((chunk, PACK_W), jnp.int32), dma, dma, dma, dma],
        name="sc_gather")(table, idx2.reshape(SC_WORKERS, per_worker, chunk))


def _final_kernel(x_ref, y_ref, g_ref, nf_ref, o_ref):
    g = g_ref[...]
    hi = jnp.zeros((x_ref.shape[0], PACK_W), _F32)
    lo = jnp.zeros((x_ref.shape[0], PACK_W), _F32)
    for k in range(TOP_K):
        y_hi, y_lo = _unpack_rows(y_ref[k])
        hi = hi + g[:, k:k + 1] * y_hi
        lo = lo + g[:, k:k + 1] * y_lo
    x = x_ref[...] + jnp.concatenate([hi, lo], axis=1)
    o_ref[...] = x * lax.rsqrt(jnp.mean(x * x, axis=-1, keepdims=True) + EPS) * nf_ref[...]


def _final(xmid, yg, gates_t, norm_f, *, tm, first_block):
    t = xmid.shape[0]
    return pl.pallas_call(
        _final_kernel, grid=(t // tm,),
        in_specs=[pl.BlockSpec((tm, D_MODEL), lambda i: (i, 0)),
                  pl.BlockSpec((TOP_K, tm, PACK_W), lambda i: (0, i + first_block, 0)),
                  pl.BlockSpec((tm, TOP_K), lambda i: (i + first_block, 0)),
                  pl.BlockSpec((1, D_MODEL), lambda i: (0, 0))],
        out_specs=pl.BlockSpec((tm, D_MODEL), lambda i: (i, 0)),
        out_shape=jax.ShapeDtypeStruct((t, D_MODEL), _F32),
        compiler_params=_cparams(("parallel",)), name="final",
    )(xmid, yg, gates_t, norm_f)


def _pack_state(c, n, m):
    nb = c.shape[0]
    ncol = jnp.zeros((nb, ML_HEADS, ML_DK, ML_DV), _F32).at[..., 0].set(n.astype(_F32))
    cn = jnp.concatenate([c.astype(_F32), ncol], axis=-1)
    return cn, jnp.broadcast_to(m.astype(_F32)[:, :, None, None], (nb, ML_HEADS, 8, HEAD_W))


def _unpack_state(cn, m):
    return cn[..., :ML_DV], cn[..., ML_DV], m[:, :, 0, 0]


def kernel(x_prompt, x_sample, cache_k, cache_v, state_C, state_n, state_m, norm1, w_in, b_igate, b_fgate,
           lambda_q1, lambda_k1, lambda_q2, lambda_k2, subln, w_out, norm2, w_router, b_router, w_gate_up,
           b_gate_up, w_down, b_down, norm_f):
    nb, s, _ = x_prompt.shape
    db, ls, _ = x_sample.shape
    past = cache_k.shape[2]
    depth = w_in.shape[0]
    assert depth == 1 and s % PROJ_TILE == 0 and (db * ls) % 8 == 0
    l = 0
    lam_init = 0.8 - 0.6 * math.exp(-0.3 * l)
    t_p, t_s = nb * s, db * ls

    w = w_in[l]
    gate_cols = jnp.pad(w[:, 7 * GROUP_W:], ((0, 0), (0, HEAD_W - 2 * ML_HEADS)))
    w_nat = jnp.concatenate([w[:, :7 * GROUP_W], gate_cols], axis=1).astype(_BF16)
    b_gates = jnp.pad(jnp.concatenate([b_igate[l], b_fgate[l]]).astype(_F32), (0, HEAD_W - 2 * ML_HEADS))[None]
    inv = ROPE_THETA ** (-jnp.arange(0, DA_QK, 2, dtype=_F32) / DA_QK)
    inv_full = jnp.tile(inv, HEAD_W // (DA_QK // 2))[None]
    lamv = jnp.stack([lambda_q1[l], lambda_k1[l], lambda_q2[l], lambda_k2[l]]).astype(_F32)
    g1 = norm1[l][None]

    (q_p, kf_p, kb_p, vf_p, vt_p, mq_p, mk_p, mv_p, mo_p, gc_p, gr_p) = _proj(
        x_prompt, g1, w_nat, b_gates, inv_full, tm=PROJ_TILE, pos_base=0, pos_mod=s)
    (q_s, kf_s, kb_s, vf_s, _, mq_s, mk_s, mv_s, mo_s, gc_s, gr_s) = _proj(
        x_sample.reshape(1, t_s, D_MODEL), g1, w_nat, b_gates, inv_full, tm=t_s, pos_base=past, pos_mod=ls)

    da_p = _attn_prompt(lamv, q_p, kb_p, vt_p, lam_init=lam_init)
    per_b = lambda a: a.reshape(db, ls, a.shape[-1])
    kct = jnp.transpose(cache_k[l], (0, 2, 3, 4, 1)).reshape(db, GROUP_W, past)
    vc = cache_v[l].reshape(db, past * DA_HEADS, DA_V)
    vn = vf_s.reshape(db, ls, GROUP_W).astype(_BF16)
    da_s = _attn_sample(lamv, per_b(q_s), per_b(kb_s), vn, kct, vc, lam_init=lam_init)

    zeros_state = _pack_state(jnp.zeros((nb, ML_HEADS, ML_DK, ML_DV), _F32), jnp.zeros((nb, ML_HEADS, ML_DK), _F32),
                              jnp.zeros((nb, ML_HEADS), _F32))
    h_p, cn_p, m_p = _mlstm(mq_p, mk_p, mv_p, gc_p, gr_p, *zeros_state, chunk=CHUNK, block=PROJ_TILE)
    gr_sb = gr_s.reshape(8, db, ls).transpose(1, 0, 2)
    h_s, cn_s, m_s = _mlstm(per_b(mq_s), per_b(mk_s), per_b(mv_s), per_b(gc_s), gr_sb,
                            *_pack_state(state_C[l], state_n[l], state_m[l]), chunk=ls, block=ls)
    c_p, n_p, mm_p = _unpack_state(cn_p, m_p)
    c_s, n_s, mm_s = _unpack_state(cn_s, m_s)

    wo = w_out[l].astype(_BF16)
    w_rt = w_router[l].T.astype(_BF16)
    b_r = b_router[l].astype(_F32)[:, None]
    mix = functools.partial(_mix, subln=subln[l][None], w_out=wo, norm2=norm2[l][None], w_rt=w_rt, b_r=b_r,
                            lam_init=lam_init)
    flat = lambda a: a.reshape(-1, a.shape[-1])
    xm_p, xp_p, idx_p, gate_p, rank_p, cnt_p = mix(
        flat(da_p), flat(h_p), flat(mo_p), flat(x_prompt), cnt0=jnp.zeros((N_EXPERTS, 1), _F32), tm=PROJ_TILE)
    xm_s, xp_s, idx_s, gate_s, rank_s, cnt = mix(
        flat(da_s), flat(h_s), flat(mo_s), flat(x_sample), cnt0=cnt_p, tm=t_s)

    t_all = t_p + t_s
    rows = t_all * TOP_K
    n_blocks = -(-rows // MOE_BLOCK) + N_EXPERTS
    counts = cnt[:, 0].astype(jnp.int32)
    padded = (counts + MOE_BLOCK - 1) // MOE_BLOCK * MOE_BLOCK
    pend = jnp.cumsum(padded)
    pstart = pend - padded
    idx = jnp.concatenate([idx_p, idx_s], axis=1)
    experts = jnp.arange(N_EXPERTS, dtype=jnp.int32)[:, None, None]
    first_row = jnp.sum(jnp.where(idx[None] == experts, pstart[:, None, None], 0), axis=0)
    dest = first_row + jnp.concatenate([rank_p, rank_s], axis=1)
    n_used = (pend[-1] // MOE_BLOCK).astype(jnp.int32)[None]
    blk_lo = jnp.arange(n_blocks, dtype=jnp.int32) * MOE_BLOCK
    blk_e = jnp.minimum(jnp.sum((pend[None, :] <= blk_lo[:, None]).astype(jnp.int32), axis=1), N_EXPERTS - 1)

    assert t_all % (SC_WORKERS * DISPATCH_CHUNK) == 0 and rows % (SC_WORKERS * COMBINE_CHUNK) == 0
    dest3 = dest.reshape(TOP_K, t_all // DISPATCH_CHUNK, DISPATCH_CHUNK).transpose(1, 0, 2)
    xb = _sc_dispatch(jnp.concatenate([xp_p, xp_s], axis=0), dest3, n_blocks * MOE_BLOCK)
    yb = _moe(blk_e, n_used, xb, w_gate_up[l], b_gate_up[l][:, None, :], w_down[l], b_down[l][:, None, :])
    yg = _sc_gather(yb, dest.reshape(rows // COMBINE_CHUNK, COMBINE_CHUNK)).reshape(TOP_K, t_all, PACK_W)
    gates_t = jnp.concatenate([gate_p, gate_s], axis=1).T
    nf = norm_f[None]
    y_p = _final(xm_p, yg, gates_t, nf, tm=PROJ_TILE, first_block=0).reshape(nb, s, D_MODEL)
    y_s = _final(xm_s, yg, gates_t, nf, tm=t_s, first_block=t_p // t_s).reshape(db, ls, D_MODEL)

    st = lambda a: a[None]
    return (y_p, y_s,
            st(kf_p.reshape(nb, s, DA_HEADS, 2, DA_QK)), st(vf_p.reshape(nb, s, DA_HEADS, DA_V)),
            st(c_p), st(n_p), st(mm_p),
            st(kf_s.reshape(db, ls, DA_HEADS, 2, DA_QK)), st(vf_s.reshape(db, ls, DA_HEADS, DA_V)),
            st(c_s.astype(state_C.dtype)), st(n_s.astype(state_n.dtype)), st(mm_s.astype(state_m.dtype)))
```

```python
import functools
import math

import jax
import jax.numpy as jnp
import numpy as np
from jax import lax
from jax.experimental import pallas as pl
from jax.experimental.pallas import tpu as pltpu
from jax.experimental.pallas import tpu_sc as plsc

D_MODEL = 1024
CHUNK = 64
DA_HEADS = 4
DA_QK = 64
DA_V = 128
ML_HEADS = 4
ML_DK = 128
ML_DV = 128
HEAD_W = 128
GROUP_W = 512
ROPE_THETA = 10000.0
N_EXPERTS = 32
TOP_K = 4
D_FF = 1024
SWIGLU_LIMIT = 7.0
SWIGLU_ALPHA = 1.702
EPS = 1e-6
NEG_BIG = -1e30

LOG2E = math.log2(math.e)
VT_ROWS = 144
PROJ_TILE = 512
ATTN_TQ = 512
MOE_BLOCK = 512
FF_CHUNK = 512
PACK_W = D_MODEL // 2
SC_CORES = 2
SC_SUBCORES = 16
SC_WORKERS = SC_CORES * SC_SUBCORES
DISPATCH_CHUNK = 40
COMBINE_CHUNK = 80
VMEM_LIMIT = 56 * 1024 * 1024

_F32 = jnp.float32
_BF16 = jnp.bfloat16


def _cparams(sem):
    return pltpu.CompilerParams(dimension_semantics=sem, vmem_limit_bytes=VMEM_LIMIT)


def _dot(a, b, dims=(((1,), (0,)), ((), ())), precision=None):
    return lax.dot_general(a, b, dims, precision=precision, preferred_element_type=_F32)


_NT = (((1,), (1,)), ((), ()))
_TN = (((0,), (0,)), ((), ()))


def _pack_rows(x):
    half = x.shape[1] // 2
    bits = lambda v: lax.bitcast_convert_type(v.astype(_BF16).astype(_F32), jnp.int32)
    return bits(x[:, :half]) | lax.shift_right_logical(bits(x[:, half:]), 16)


def _unpack_rows(p):
    hi = lax.bitcast_convert_type(p & jnp.int32(-65536), _F32)
    lo = lax.bitcast_convert_type(lax.shift_left(p, 16), _F32)
    return hi, lo


def _proj_kernel(x_ref, g_ref, w_ref, bg_ref, inv_ref,
                 q_ref, kf_ref, kb_ref, vf_ref, vt_ref, mq_ref, mk_ref, mv_ref, mo_ref, gc_ref, gr_ref,
                 mqt_ref, mvt_ref, *, tm, pos_base, pos_mod):
    x = x_ref[0]
    xn = (x * lax.rsqrt(jnp.mean(x * x, axis=-1, keepdims=True) + EPS) * g_ref[...]).astype(_BF16)

    row = pl.program_id(1) * tm + lax.broadcasted_iota(jnp.int32, (tm, 1), 0)
    pos = (pos_base + row % pos_mod).astype(_F32)
    ang = pos * inv_ref[...]
    cos = jnp.cos(ang)
    sin = jnp.sin(ang)
    lane = lax.broadcasted_iota(jnp.int32, (1, HEAD_W), 1)
    first = (lane % DA_QK) < (DA_QK // 2)
    sin = jnp.where(first, -sin, sin)

    def rope(z):
        rot = jnp.where(first, pltpu.roll(z, HEAD_W - DA_QK // 2, 1), pltpu.roll(z, DA_QK // 2, 1))
        return z * cos + rot * sin

    def group(c, width=GROUP_W):
        return _dot(xn, w_ref[:, c * GROUP_W:c * GROUP_W + width])

    def put_q(zq):
        for h in range(DA_HEADS):
            sl = slice(h * HEAD_W, (h + 1) * HEAD_W)
            q_ref[0, :, sl] = (rope(zq[:, sl]) * (DA_QK ** -0.5 * LOG2E)).astype(_BF16)

    def put_k(zk):
        for h in range(DA_HEADS):
            sl = slice(h * HEAD_W, (h + 1) * HEAD_W)
            rk = rope(zk[:, sl])
            kf_ref[0, :, sl] = rk
            kb_ref[0, :, sl] = rk.astype(_BF16)

    def with_ones_rows(zt):
        ones = jnp.ones((VT_ROWS - DA_V, tm), _BF16)
        return jnp.concatenate(
            [part for h in range(DA_HEADS) for part in (zt[h * DA_V:(h + 1) * DA_V], ones)], axis=0)

    def put_v(zv):
        for h in range(DA_HEADS):
            vf_ref[0, pl.ds(h, tm, stride=DA_HEADS), :] = zv[:, h * DA_V:(h + 1) * DA_V]
        vt_ref[0, 0] = with_ones_rows(zv.T.astype(_BF16))

    def put_mq(z):
        mq_ref[0] = z.astype(_BF16)
        mqt_ref[0, 0] = z.T.astype(_BF16)

    def put_mk(z):
        mk_ref[0] = (z * (ML_DK ** -0.5)).astype(_BF16)

    def put_mv(z):
        mv_ref[0] = z.astype(_BF16)
        mvt_ref[0, 0] = with_ones_rows(z.T.astype(_BF16))

    def put_mo(z):
        mo_ref[0] = z

    def put_gates(z):
        zg = z + bg_ref[...]
        lane_g = lax.broadcasted_iota(jnp.int32, zg.shape, 1)
        logsig = jnp.minimum(zg, 0.0) - jnp.log(1.0 + jnp.exp(-jnp.abs(zg)))
        gates = jnp.where(lane_g < ML_HEADS, zg, logsig)
        gc_ref[0] = gates[:, :8]
        gr_ref[0] = gates.T[:8, :]

    sinks = [(put_q, GROUP_W), (put_k, GROUP_W), (put_v, GROUP_W), (put_mq, GROUP_W), (put_mk, GROUP_W),
             (put_mv, GROUP_W), (put_mo, GROUP_W), (put_gates, HEAD_W)]
    nxt = group(0, sinks[0][1])
    for c, (sink, _) in enumerate(sinks):
        cur = nxt
        if c + 1 < len(sinks):
            nxt = group(c + 1, sinks[c + 1][1])
        sink(cur)


def _proj(x3, norm1, w_nat, b_gates, inv_full, *, tm, pos_base, pos_mod):
    nb, s, _ = x3.shape
    nt = s // tm
    wcols = w_nat.shape[1]
    tok = lambda w, dt: jax.ShapeDtypeStruct((nb, s, w), dt)
    tok_spec = lambda w: pl.BlockSpec((1, tm, w), lambda b, i: (b, i, 0))
    const = lambda shape: pl.BlockSpec(shape, lambda b, i: (0,) * len(shape))
    out_shape = [tok(GROUP_W, _BF16), tok(GROUP_W, _F32), tok(GROUP_W, _BF16),
                 jax.ShapeDtypeStruct((nb, s * DA_HEADS, DA_V), _F32),
                 jax.ShapeDtypeStruct((nb, nt, DA_HEADS * VT_ROWS, tm), _BF16),
                 tok(GROUP_W, _BF16), tok(GROUP_W, _BF16), tok(GROUP_W, _BF16), tok(GROUP_W, _F32),
                 tok(8, _F32), jax.ShapeDtypeStruct((nb, 8, s), _F32),
                 jax.ShapeDtypeStruct((nb, nt, GROUP_W, tm), _BF16),
                 jax.ShapeDtypeStruct((nb, nt, DA_HEADS * VT_ROWS, tm), _BF16)]
    out_specs = [tok_spec(GROUP_W)] * 3 + [pl.BlockSpec((1, tm * DA_HEADS, DA_V), lambda b, i: (b, i, 0)),
                                           pl.BlockSpec((1, 1, DA_HEADS * VT_ROWS, tm), lambda b, i: (b, i, 0, 0))] \
        + [tok_spec(GROUP_W)] * 4 + [tok_spec(8), pl.BlockSpec((1, 8, tm), lambda b, i: (b, 0, i)),
                                     pl.BlockSpec((1, 1, GROUP_W, tm), lambda b, i: (b, i, 0, 0)),
                                     pl.BlockSpec((1, 1, DA_HEADS * VT_ROWS, tm), lambda b, i: (b, i, 0, 0))]
    return pl.pallas_call(
        functools.partial(_proj_kernel, tm=tm, pos_base=pos_base, pos_mod=pos_mod),
        grid=(nb, nt),
        in_specs=[tok_spec(D_MODEL), const((1, D_MODEL)), const((D_MODEL, wcols)), const((1, HEAD_W)),
                  const((1, HEAD_W))],
        out_specs=out_specs, out_shape=out_shape,
        compiler_params=_cparams(("parallel", "parallel")), name="proj",
    )(x3, norm1, w_nat, b_gates, inv_full)


def _lambda(lam_ref, lam_init):
    lv = lam_ref[...]
    s1 = jnp.sum(lv[0:1] * lv[1:2], axis=-1, keepdims=True)
    s2 = jnp.sum(lv[2:3] * lv[3:4], axis=-1, keepdims=True)
    return jnp.exp(s1) - jnp.exp(s2) + lam_init


def _split_components(q):
    lane = lax.broadcasted_iota(jnp.int32, q.shape, 1)
    zero = jnp.zeros_like(q)
    return jnp.concatenate([jnp.where(lane < DA_QK, q, zero), jnp.where(lane >= DA_QK, q, zero)], axis=0)


def _attn_kernel(lam_ref, q_ref, k_ref, vt_ref, o_ref, acc_ref, s_ref, *, tq, tk, lam_init):
    i = pl.program_id(1)
    heads = range(DA_HEADS)
    qz = [_split_components(q_ref[0, :, h * HEAD_W:(h + 1) * HEAD_W]) for h in heads]
    acc_ref[...] = jnp.zeros_like(acc_ref)

    def scores(h, j):
        k_t = k_ref[0, pl.ds(pl.multiple_of(j * tk, tk), tk), h * HEAD_W:(h + 1) * HEAD_W]
        return _dot(k_t, qz[h], _NT)

    def step(j, ms, last):
        if last:
            kpos = j * tk + lax.broadcasted_iota(jnp.int32, (tk, 1), 0)
            qpos = i * tq + lax.broadcasted_iota(jnp.int32, (1, 2 * tq), 1) % tq
            visible = kpos < (qpos // CHUNK + 1) * CHUNK
        out = []
        s = s_ref[...]
        for h in heads:
            s_next = None
            if h + 1 < DA_HEADS:
                s_next = scores(h + 1, j)
            elif not last:
                s_next = scores(0, j + 1)
            if last:
                s = jnp.where(visible, s, NEG_BIG)
            m_new = jnp.maximum(ms[h], jnp.max(s, axis=0, keepdims=True))
            alpha = jnp.exp2(ms[h] - m_new)
            p = jnp.exp2(s - m_new).astype(_BF16)
            acc_ref[h] = alpha * acc_ref[h] + _dot(vt_ref[0, j, h * VT_ROWS:(h + 1) * VT_ROWS, :], p)
            out.append(m_new)
            s = s_next
        if not last:
            s_ref[...] = s
        return tuple(out)

    n_full = (i * tq) // tk
    s_ref[...] = scores(0, 0)
    init = tuple(jnp.full((1, 2 * tq), NEG_BIG, _F32) for _ in heads)
    ms = lax.fori_loop(0, n_full, lambda j, c: step(j, c, False), init)
    step(n_full, ms, True)

    lam = _lambda(lam_ref, lam_init)
    for h in heads:
        acc = acc_ref[h]
        o = acc[:DA_V] / acc[DA_V:DA_V + 1]
        o_ref[0, :, h * HEAD_W:(h + 1) * HEAD_W] = (o[:, :tq] - lam * o[:, tq:]).T


def _attn_prompt(lamv, q, kb, vt, *, lam_init):
    nb, s, _ = q.shape
    nk, tk = vt.shape[1], vt.shape[3]
    tq = ATTN_TQ
    return pl.pallas_call(
        functools.partial(_attn_kernel, tq=tq, tk=tk, lam_init=lam_init),
        grid=(nb, s // tq),
        in_specs=[pl.BlockSpec((4, DA_QK), lambda b, i: (0, 0)),
                  pl.BlockSpec((1, tq, GROUP_W), lambda b, i: (b, i, 0)),
                  pl.BlockSpec((1, s, GROUP_W), lambda b, i: (b, 0, 0)),
                  pl.BlockSpec((1, nk, DA_HEADS * VT_ROWS, tk), lambda b, i: (b, 0, 0, 0))],
        out_specs=pl.BlockSpec((1, tq, GROUP_W), lambda b, i: (b, i, 0)),
        out_shape=jax.ShapeDtypeStruct((nb, s, GROUP_W), _F32),
        scratch_shapes=[pltpu.VMEM((DA_HEADS, VT_ROWS, 2 * tq), _F32), pltpu.VMEM((tk, 2 * tq), _F32)],
        compiler_params=_cparams(("parallel", "arbitrary")), name="attn",
    )(lamv, q, kb, vt)


def _attn_sample_kernel(lam_ref, q_ref, kn_ref, vn_ref, kc_ref, vc_ref, o_ref, *, lq, past, lam_init):
    lam = _lambda(lam_ref, lam_init)
    for h in range(DA_HEADS):
        sl = slice(h * HEAD_W, (h + 1) * HEAD_W)
        qz = _split_components(q_ref[0, :, sl])
        kct = kc_ref[0, sl, :].astype(_BF16)
        vc = vc_ref[0, pl.ds(h, past, stride=DA_HEADS), :].astype(_BF16)
        s_c = _dot(qz, kct)
        s_n = _dot(qz, kn_ref[0, :, sl], _NT)
        m = jnp.maximum(jnp.max(s_c, axis=-1, keepdims=True), jnp.max(s_n, axis=-1, keepdims=True))
        p_c = jnp.exp2(s_c - m)
        p_n = jnp.exp2(s_n - m)
        l = jnp.sum(p_c, axis=-1, keepdims=True) + jnp.sum(p_n, axis=-1, keepdims=True)
        o = (_dot(p_c.astype(_BF16), vc) + _dot(p_n.astype(_BF16), vn_ref[0, :, sl])) / l
        o_ref[0, :, sl] = o[:lq] - lam * o[lq:]


def _attn_sample(lamv, q, kn, vn, kct, vc, *, lam_init):
    nb, lq, _ = q.shape
    past = kct.shape[2]
    new = pl.BlockSpec((1, lq, GROUP_W), lambda b: (b, 0, 0))
    return pl.pallas_call(
        functools.partial(_attn_sample_kernel, lq=lq, past=past, lam_init=lam_init),
        grid=(nb,),
        in_specs=[pl.BlockSpec((4, DA_QK), lambda b: (0, 0)), new, new, new,
                  pl.BlockSpec((1, GROUP_W, past), lambda b: (b, 0, 0)),
                  pl.BlockSpec((1, past * DA_HEADS, DA_V), lambda b: (b, 0, 0))],
        out_specs=new, out_shape=jax.ShapeDtypeStruct((nb, lq, GROUP_W), _F32),
        compiler_params=_cparams(("parallel",)), name="attn_sample",
    )(lamv, q, kn, vn, kct, vc)


def _mlstm_kernel(q_ref, k_ref, v_ref, gc_ref, gr_ref, cn0_ref, m0_ref, h_ref, cn_ref, m_ref,
                  cn_sc, m_sc, *, chunk, nchunks):
    j = pl.program_id(1)

    @pl.when(j == 0)
    def _():
        cn_sc[...] = cn0_ref[0]
        m_sc[...] = m0_ref[0]

    L = chunk
    r_i = lax.broadcasted_iota(jnp.int32, (L, L), 0)
    c_i = lax.broadcasted_iota(jnp.int32, (L, L), 1)
    causal = r_i >= c_i
    tril = causal.astype(_F32)
    triu = (r_i <= c_i).astype(_F32)
    ones_col = (lax.broadcasted_iota(jnp.int32, (L, ML_DV), 1) == 0).astype(_BF16)
    hi = lax.Precision.HIGHEST

    for c in range(nchunks):
        rows = slice(c * L, (c + 1) * L)
        gcol = gc_ref[0, rows, :]
        grow = gr_ref[0, :, rows]
        fc_all = _dot(tril, gcol, precision=hi)
        fr_all = _dot(grow, triu, precision=hi)
        for hh in range(ML_HEADS):
            sl = slice(hh * HEAD_W, (hh + 1) * HEAD_W)
            m = m_sc[hh][0:1, 0:1]
            fc = fc_all[:, ML_HEADS + hh:ML_HEADS + hh + 1]
            fr = fr_all[ML_HEADS + hh:ML_HEADS + hh + 1, :]
            igc = gcol[:, hh:hh + 1]
            igr = grow[hh:hh + 1, :]
            a = jnp.where(causal, fc - fr + igr, -jnp.inf)
            b = fc + m
            m_t = jnp.maximum(b, jnp.max(a, axis=-1, keepdims=True))
            w = jnp.exp(a - m_t)
            sc = jnp.exp(b - m_t)
            q = q_ref[0, rows, sl]
            k = k_ref[0, rows, sl]
            v = v_ref[0, rows, sl]
            wqk = w * _dot(q, k, _NT)
            cn = cn_sc[hh]
            inter = _dot(q, cn.astype(_BF16))
            num = _dot(wqk.astype(_BF16), v) + sc * inter[:, :ML_DV]
            den = jnp.sum(wqk, axis=-1, keepdims=True) + sc * inter[:, ML_DV:ML_DV + 1]
            h_ref[0, rows, sl] = num / jnp.maximum(jnp.abs(den), jnp.exp(-m_t))

            fl = fc[L - 1:L, :]
            g = fl - fc + igc
            m_new = jnp.maximum(fl + m, jnp.max(g, axis=0, keepdims=True))
            ws = jnp.exp(g - m_new)
            decay = jnp.exp(fl + m - m_new)
            kw = (k.astype(_F32) * ws).astype(_BF16)
            v1 = jnp.concatenate([v, ones_col], axis=1)
            cn_sc[hh] = decay * cn + _dot(kw, v1, _TN)
            m_sc[hh] = jnp.broadcast_to(m_new, m_sc.shape[1:])


    @pl.when(j == pl.num_programs(1) - 1)
    def _():
        cn_ref[0] = cn_sc[...]
        m_ref[0] = m_sc[...]


def _mlstm(q, k, v, gc, gr, cn0, m0, *, chunk, block):
    nb, s, _ = q.shape
    tok = lambda w: pl.BlockSpec((1, block, w), lambda b, j: (b, j, 0))
    state = pl.BlockSpec((1, ML_HEADS, ML_DK, 2 * ML_DV), lambda b, j: (b, 0, 0, 0))
    mspec = pl.BlockSpec((1, ML_HEADS, 8, HEAD_W), lambda b, j: (b, 0, 0, 0))
    return pl.pallas_call(
        functools.partial(_mlstm_kernel, chunk=chunk, nchunks=block // chunk),
        grid=(nb, s // block),
        in_specs=[tok(GROUP_W), tok(GROUP_W), tok(GROUP_W), tok(8),
                  pl.BlockSpec((1, 8, block), lambda b, j: (b, 0, j)), state, mspec],
        out_specs=[tok(GROUP_W), state, mspec],
        out_shape=[jax.ShapeDtypeStruct((nb, s, GROUP_W), _F32),
                   jax.ShapeDtypeStruct(cn0.shape, _F32), jax.ShapeDtypeStruct(m0.shape, _F32)],
        scratch_shapes=[pltpu.VMEM((ML_HEADS, ML_DK, 2 * ML_DV), _F32), pltpu.VMEM((ML_HEADS, 8, HEAD_W), _F32)],
        compiler_params=_cparams(("parallel", "arbitrary")), name="mlstm",
    )(q, k, v, gc, gr, cn0, m0)


def _mlstm_pairs_kernel(k_ref, qt_ref, vt_ref, gr_ref, st0_ref, m0_ref, h_ref, st_ref, m_ref, st_sc, m_sc,
                        *, block):
    j = pl.program_id(1)

    @pl.when(j == 0)
    def _():
        st_sc[...] = st0_ref[0]
        m_sc[...] = m0_ref[0]

    L, W = CHUNK, 2 * CHUNK
    g8 = gr_ref[0]
    pos = lax.broadcasted_iota(jnp.int32, (1, block), 1) % L
    roll = lambda x, sh: pltpu.roll(x, sh % block, 1)
    steps = [1 << b for b in range(L.bit_length() - 1)]

    def scan(x, combine, fill, reverse=False):
        for sh in steps:
            ok = (pos < L - sh) if reverse else (pos >= sh)
            x = combine(x, jnp.where(ok, roll(x, -sh if reverse else sh), fill))
        return x

    to_head_rows = lambda x: pltpu.roll(x, ML_HEADS, 0)
    f8 = to_head_rows(scan(g8, jnp.add, 0.0))
    d8 = g8 - f8
    cm8 = scan(d8, jnp.maximum, -jnp.inf)
    big_g = to_head_rows(scan(g8, jnp.add, 0.0, reverse=True) - g8) + g8
    mg8 = jnp.maximum(scan(big_g, jnp.maximum, -jnp.inf), scan(big_g, jnp.maximum, -jnp.inf, reverse=True))
    ws8 = jnp.exp(big_g - mg8)
    d_cols = d8.T

    s_i = lax.broadcasted_iota(jnp.int32, (W, W), 0)
    t_i = lax.broadcasted_iota(jnp.int32, (W, W), 1)
    allowed = (s_i // L == t_i // L) & (s_i <= t_i)
    lane_w = lax.broadcasted_iota(jnp.int32, (1, W), 1)
    first = lane_w < L

    sts = [st_sc[hh] for hh in range(ML_HEADS)]
    ms = [m_sc[hh][0:1, 0:1] for hh in range(ML_HEADS)]

    def independent(p, hh):
        lanes = slice(p * W, (p + 1) * W)
        feat = slice(hh * HEAD_W, (hh + 1) * HEAD_W)
        kp = k_ref[0, lanes, feat]
        qt = qt_ref[0, 0, feat, lanes]
        v1t = vt_ref[0, 0, hh * VT_ROWS:(hh + 1) * VT_ROWS, lanes]
        ws = ws8[hh:hh + 1, lanes]
        v1f = v1t.astype(_F32)
        return dict(
            p=p, hh=hh, qt=qt, v1t=v1t, qk_t=_dot(kp, qt),
            upd_a=_dot((v1f * jnp.where(first, ws, 0.0)).astype(_BF16), kp),
            upd_b=_dot((v1f * jnp.where(first, 0.0, ws)).astype(_BF16), kp))

    def dependent(u):
        p, hh, qt = u["p"], u["hh"], u["qt"]
        lanes = slice(p * W, (p + 1) * W)
        row = lambda x: x[hh:hh + 1, lanes]
        cm, f_row = row(cm8), row(f8)
        w_t = jnp.exp(jnp.where(allowed, d_cols[lanes, hh:hh + 1] - cm, -jnp.inf))
        intra = _dot(u["v1t"], (w_t * u["qk_t"]).astype(_BF16))
        st_a, m_a = sts[hh], ms[hh]
        fl_a, mg_a = f_row[:, L - 1:L], row(mg8)[:, 0:1]
        fl_b, mg_b = f_row[:, W - 1:W], row(mg8)[:, L:L + 1]
        m_b = jnp.maximum(fl_a + m_a, mg_a)
        st_b = jnp.exp(fl_a + m_a - m_b) * st_a + jnp.exp(mg_a - m_b) * u["upd_a"]
        m_c = jnp.maximum(fl_b + m_b, mg_b)
        sts[hh] = jnp.exp(fl_b + m_b - m_c) * st_b + jnp.exp(mg_b - m_c) * u["upd_b"]
        ms[hh] = m_c
        inter = jnp.where(first, _dot(st_a.astype(_BF16), qt), _dot(st_b.astype(_BF16), qt))
        m_prev = jnp.where(first, m_a, m_b)
        mt = jnp.maximum(m_prev, cm)
        tot = jnp.exp(cm - mt) * intra + jnp.exp(m_prev - mt) * inter
        den = jnp.maximum(jnp.abs(tot[ML_DV:ML_DV + 1]), jnp.exp(-(f_row + mt)))
        h_ref[0, lanes, hh * HEAD_W:(hh + 1) * HEAD_W] = (tot[:ML_DV] / den).T

    units = [(p, hh) for p in range(block // W) for hh in range(ML_HEADS)]
    nxt = independent(*units[0])
    for n in range(len(units)):
        cur = nxt
        if n + 1 < len(units):
            nxt = independent(*units[n + 1])
        dependent(cur)
    for hh in range(ML_HEADS):
        st_sc[hh] = sts[hh]
        m_sc[hh] = jnp.broadcast_to(ms[hh], m_sc.shape[1:])

    @pl.when(j == pl.num_programs(1) - 1)
    def _():
        st_ref[0] = st_sc[...]
        m_ref[0] = m_sc[...]


def _mlstm_pairs(k, qt, vt, gr, st0, m0):
    nb, s, _ = k.shape
    block = qt.shape[3]
    nt = s // block
    state = pl.BlockSpec((1, ML_HEADS, VT_ROWS, ML_DK), lambda b, j: (b, 0, 0, 0))
    mspec = pl.BlockSpec((1, ML_HEADS, 8, HEAD_W), lambda b, j: (b, 0, 0, 0))
    tok = pl.BlockSpec((1, block, GROUP_W), lambda b, j: (b, j, 0))
    return pl.pallas_call(
        functools.partial(_mlstm_pairs_kernel, block=block),
        grid=(nb, nt),
        in_specs=[tok, pl.BlockSpec((1, 1, GROUP_W, block), lambda b, j: (b, j, 0, 0)),
                  pl.BlockSpec((1, 1, ML_HEADS * VT_ROWS, block), lambda b, j: (b, j, 0, 0)),
                  pl.BlockSpec((1, 8, block), lambda b, j: (b, 0, j)), state, mspec],
        out_specs=[tok, state, mspec],
        out_shape=[jax.ShapeDtypeStruct((nb, s, GROUP_W), _F32), jax.ShapeDtypeStruct(st0.shape, _F32),
                   jax.ShapeDtypeStruct(m0.shape, _F32)],
        scratch_shapes=[pltpu.VMEM((ML_HEADS, VT_ROWS, ML_DK), _F32), pltpu.VMEM((ML_HEADS, 8, HEAD_W), _F32)],
        compiler_params=_cparams(("parallel", "arbitrary")), name="mlstm_pairs",
    )(k, qt, vt, gr, st0, m0)


def _mix_kernel(da_ref, ml_ref, mo_ref, x_ref, subln_ref, wo_ref, g2_ref, wr_ref, br_ref, cnt0_ref,
                xmid_ref, xp_ref, idx_ref, gate_ref, rank_ref, cnt_ref, cnt_sc, *, tm, lam_init):
    step = pl.program_id(0)

    @pl.when(step == 0)
    def _():
        cnt_sc[...] = cnt0_ref[...]

    parts = []
    for h in range(DA_HEADS):
        d = da_ref[:, h * HEAD_W:(h + 1) * HEAD_W]
        d = d * lax.rsqrt(jnp.mean(d * d, axis=-1, keepdims=True) + EPS) * subln_ref[...]
        parts.append((d * (1.0 - lam_init)).astype(_BF16))
    parts.append((jax.nn.sigmoid(mo_ref[...]) * ml_ref[...]).astype(_BF16))
    xm = x_ref[...] + _dot(jnp.concatenate(parts, axis=1), wo_ref[...])
    xmid_ref[...] = xm
    xn_f = xm * lax.rsqrt(jnp.mean(xm * xm, axis=-1, keepdims=True) + EPS) * g2_ref[...]
    xp_ref[...] = _pack_rows(xn_f)
    xn = xn_f.astype(_BF16)

    logits = _dot(wr_ref[...], xn, _NT) + br_ref[...]
    e_i = lax.broadcasted_iota(jnp.int32, logits.shape, 0)
    member = jnp.zeros(logits.shape, jnp.bool_)
    vals, idxs = [], []
    for _ in range(TOP_K):
        mx = jnp.max(logits, axis=0, keepdims=True)
        sel = jnp.min(jnp.where(logits == mx, e_i, N_EXPERTS), axis=0, keepdims=True)
        hit = e_i == sel
        member = member | hit
        logits = jnp.where(hit, -jnp.inf, logits)
        vals.append(mx)
        idxs.append(sel)
    ex = [jnp.exp(v - vals[0]) for v in vals]
    tot = ex[0] + ex[1] + ex[2] + ex[3]
    idx_ref[...] = jnp.concatenate(idxs, axis=0)
    gate_ref[...] = jnp.concatenate([e / tot for e in ex], axis=0)

    upper = (lax.broadcasted_iota(jnp.int32, (tm, tm), 0) < lax.broadcasted_iota(jnp.int32, (tm, tm), 1))
    memf = member.astype(_F32)
    before = _dot(memf.astype(_BF16), upper.astype(_BF16)) + cnt_sc[...]
    rank_ref[...] = jnp.concatenate(
        [jnp.sum(jnp.where(e_i == s, before, 0.0), axis=0, keepdims=True) for s in idxs], axis=0).astype(jnp.int32)
    cnt_sc[...] += jnp.sum(memf, axis=1, keepdims=True)
    cnt_ref[...] = cnt_sc[...]


def _mix(da, ml, mo, x, subln, w_out, norm2, w_rt, b_r, cnt0, *, tm, lam_init):
    t = x.shape[0]
    tok = lambda w: pl.BlockSpec((tm, w), lambda i: (i, 0))
    const = lambda shape: pl.BlockSpec(shape, lambda i: (0,) * len(shape))
    lane_tok = pl.BlockSpec((TOP_K, tm), lambda i: (0, i))
    return pl.pallas_call(
        functools.partial(_mix_kernel, tm=tm, lam_init=lam_init),
        grid=(t // tm,),
        in_specs=[tok(GROUP_W), tok(GROUP_W), tok(GROUP_W), tok(D_MODEL), const((1, DA_V)),
                  const((D_MODEL, D_MODEL)), const((1, D_MODEL)), const((N_EXPERTS, D_MODEL)),
                  const((N_EXPERTS, 1)), const((N_EXPERTS, 1))],
        out_specs=[tok(D_MODEL), tok(PACK_W), lane_tok, lane_tok, lane_tok, const((N_EXPERTS, 1))],
        out_shape=[jax.ShapeDtypeStruct((t, D_MODEL), _F32), jax.ShapeDtypeStruct((t, PACK_W), jnp.int32),
                   jax.ShapeDtypeStruct((TOP_K, t), jnp.int32), jax.ShapeDtypeStruct((TOP_K, t), _F32),
                   jax.ShapeDtypeStruct((TOP_K, t), jnp.int32), jax.ShapeDtypeStruct((N_EXPERTS, 1), _F32)],
        scratch_shapes=[pltpu.VMEM((N_EXPERTS, 1), _F32)],
        compiler_params=_cparams(("arbitrary",)), name="mix",
    )(da, ml, mo, x, subln, w_out, norm2, w_rt, b_r, cnt0)


def _moe_kernel(be_ref, nu_ref, x_ref, wgu_ref, bgu_ref, wd_ref, bd_ref, y_ref, wgu_sc, wd_sc):
    j = pl.program_id(0)
    e_prev = be_ref[jnp.maximum(j - 1, 0)]
    fresh = (j == 0) | (be_ref[j] != e_prev)

    @pl.when(fresh & (j < nu_ref[0]))
    def _():
        rows = 128
        def cast_gu(r, _):
            sl = pl.ds(pl.multiple_of(r * rows, rows), rows)
            wgu_sc[sl, :] = wgu_ref[0, sl, :].astype(_BF16)
            return 0
        lax.fori_loop(0, D_MODEL // rows, cast_gu, 0)
        def cast_d(r, _):
            sl = pl.ds(pl.multiple_of(r * rows, rows), rows)
            wd_sc[sl, :] = wd_ref[0, sl, :].astype(_BF16)
            return 0
        lax.fori_loop(0, D_FF // rows, cast_d, 0)

    @pl.when(j >= nu_ref[0])
    def _():
        y_ref[...] = jnp.zeros_like(y_ref)

    @pl.when(j < nu_ref[0])
    def _():
        hi, lo = _unpack_rows(x_ref[...])
        x = jnp.concatenate([hi.astype(_BF16), lo.astype(_BF16)], axis=1)
        n_chunks = D_FF // FF_CHUNK

        def gate_up(c):
            lo = c * FF_CHUNK
            return (_dot(x, wgu_sc[:, lo:lo + FF_CHUNK]), _dot(x, wgu_sc[:, D_FF + lo:D_FF + lo + FF_CHUNK]))

        acc = None
        nxt = gate_up(0)
        for c in range(n_chunks):
            lo = c * FF_CHUNK
            gate, up = nxt
            if c + 1 < n_chunks:
                nxt = gate_up(c + 1)
            gate = jnp.minimum(gate + bgu_ref[0, :, lo:lo + FF_CHUNK], SWIGLU_LIMIT)
            up = jnp.clip(up + bgu_ref[0, :, D_FF + lo:D_FF + lo + FF_CHUNK], -SWIGLU_LIMIT, SWIGLU_LIMIT)
            act = (up + 1.0) * (gate * jax.nn.sigmoid(SWIGLU_ALPHA * gate))
            down = _dot(act.astype(_BF16), wd_sc[lo:lo + FF_CHUNK, :])
            acc = down if acc is None else acc + down
        y_ref[...] = _pack_rows(acc + bd_ref[0])


def _moe(blk_e, n_used, xb, w_gu, b_gu, w_d, b_d):
    n_blocks = xb.shape[0] // MOE_BLOCK
    last = lambda j, nu: jnp.minimum(j, nu[0] - 1)
    row_spec = pl.BlockSpec((MOE_BLOCK, PACK_W), lambda j, be, nu: (last(j, nu), 0))
    exp_spec = lambda r, c: pl.BlockSpec((1, r, c), lambda j, be, nu: (be[last(j, nu)], 0, 0))
    grid_spec = pltpu.PrefetchScalarGridSpec(
        num_scalar_prefetch=2, grid=(n_blocks,),
        in_specs=[row_spec, exp_spec(D_MODEL, 2 * D_FF), exp_spec(1, 2 * D_FF), exp_spec(D_FF, D_MODEL),
                  exp_spec(1, D_MODEL)],
        out_specs=pl.BlockSpec((MOE_BLOCK, PACK_W), lambda j, be, nu: (j, 0)),
        scratch_shapes=[pltpu.VMEM((D_MODEL, 2 * D_FF), _BF16), pltpu.VMEM((D_FF, D_MODEL), _BF16)])
    return pl.pallas_call(
        _moe_kernel, grid_spec=grid_spec, out_shape=jax.ShapeDtypeStruct(xb.shape, jnp.int32),
        compiler_params=_cparams(("arbitrary",)), name="moe",
    )(blk_e, n_used, xb, w_gu, b_gu, w_d, b_d)


def _sc_mesh():
    return plsc.VectorSubcoreMesh(core_axis_name="c", subcore_axis_name="s")


def _sc_worker():
    return lax.axis_index("s") * SC_CORES + lax.axis_index("c")


def _sc_dispatch(xp, dest3, n_rows):
    n_chunks, _, chunk = dest3.shape
    per_worker = n_chunks // SC_WORKERS
    assert per_worker * SC_WORKERS == n_chunks and chunk % 8 == 0

    def body(x_hbm, d_hbm, o_hbm, idx_v, rows_a, rows_b, load_a, load_b, scat_a, scat_b):
        worker = _sc_worker()
        pltpu.sync_copy(d_hbm.at[worker], idx_v)

        def load(ci, rows, sem):
            row0 = (worker * per_worker + ci) * chunk
            return pltpu.make_async_copy(x_hbm.at[pl.ds(pl.multiple_of(row0, 8), chunk)], rows, sem)

        def scatters(ci, rows, sem):
            return [pltpu.make_async_copy(rows, o_hbm.at[idx_v.at[ci * TOP_K + k]], sem) for k in range(TOP_K)]

        def start(copies):
            for cp in copies:
                cp.start()

        def wait(copies):
            for cp in copies:
                cp.wait()

        load(0, rows_a, load_a).start()

        @pl.loop(0, per_worker // 2)
        def _(p):
            ca, cb = 2 * p, 2 * p + 1
            load(ca, rows_a, load_a).wait()
            load(cb, rows_b, load_b).start()
            start(scatters(ca, rows_a, scat_a))
            load(cb, rows_b, load_b).wait()
            wait(scatters(ca, rows_a, scat_a))
            load(jnp.minimum(ca + 2, per_worker - 1), rows_a, load_a).start()
            start(scatters(cb, rows_b, scat_b))
            wait(scatters(cb, rows_b, scat_b))

        last = per_worker - 1
        load(last, rows_a, load_a).wait()
        if per_worker % 2:
            start(scatters(last, rows_a, scat_a))
            wait(scatters(last, rows_a, scat_a))

    dma = pltpu.SemaphoreType.DMA
    return pl.kernel(
        body, out_type=jax.ShapeDtypeStruct((n_rows, PACK_W), jnp.int32), mesh=_sc_mesh(),
        scratch_types=[pltpu.VMEM((per_worker * TOP_K, chunk), jnp.int32), pltpu.VMEM((chunk, PACK_W), jnp.int32),
                       pltpu.VMEM((chunk, PACK_W), jnp.int32), dma, dma, dma, dma],
        name="sc_dispatch")(xp, dest3.reshape(SC_WORKERS, per_worker * TOP_K, chunk))


def _sc_gather(table, idx2):
    n_chunks, chunk = idx2.shape
    per_worker = n_chunks // SC_WORKERS
    assert per_worker * SC_WORKERS == n_chunks and chunk % 8 == 0

    def body(t_hbm, i_hbm, o_hbm, idx_v, rows_a, rows_b, gath_a, gath_b, put_a, put_b):
        worker = _sc_worker()
        pltpu.sync_copy(i_hbm.at[worker], idx_v)

        def gather(ci, rows, sem):
            return pltpu.make_async_copy(t_hbm.at[idx_v.at[ci]], rows, sem)

        def put(ci, rows, sem):
            row0 = (worker * per_worker + ci) * chunk
            return pltpu.make_async_copy(rows, o_hbm.at[pl.ds(pl.multiple_of(row0, 8), chunk)], sem)

        gather(0, rows_a, gath_a).start()

        @pl.loop(0, per_worker // 2)
        def _(p):
            ca, cb = 2 * p, 2 * p + 1
            gather(ca, rows_a, gath_a).wait()
            gather(cb, rows_b, gath_b).start()
            put(ca, rows_a, put_a).start()
            gather(cb, rows_b, gath_b).wait()
            put(ca, rows_a, put_a).wait()
            gather(jnp.minimum(ca + 2, per_worker - 1), rows_a, gath_a).start()
            put(cb, rows_b, put_b).start()
            put(cb, rows_b, put_b).wait()

        last = per_worker - 1
        gather(last, rows_a, gath_a).wait()
        if per_worker % 2:
            put(last, rows_a, put_a).start()
            put(last, rows_a, put_a).wait()

    dma = pltpu.SemaphoreType.DMA
    return pl.kernel(
        body, out_type=jax.ShapeDtypeStruct((n_chunks * chunk, PACK_W), jnp.int32), mesh=_sc_mesh(),
        scratch_types=[pltpu.VMEM((per_worker, chunk), jnp.int32), pltpu.VMEM((chunk, PACK_W), jnp.int32),
                       pltpu.VMEM((chunk, PACK_W), jnp.int32), dma, dma, dma, dma],
        name="sc_gather")(table, idx2.reshape(SC_WORKERS, per_worker, chunk))


def _final_kernel(x_ref, y_ref, g_ref, nf_ref, o_ref):
    g = g_ref[...]
    hi = jnp.zeros((x_ref.shape[0], PACK_W), _F32)
    lo = jnp.zeros((x_ref.shape[0], PACK_W), _F32)
    for k in range(TOP_K):
        y_hi, y_lo = _unpack_rows(y_ref[k])
        hi = hi + g[:, k:k + 1] * y_hi
        lo = lo + g[:, k:k + 1] * y_lo
    x = x_ref[...] + jnp.concatenate([hi, lo], axis=1)
    o_ref[...] = x * lax.rsqrt(jnp.mean(x * x, axis=-1, keepdims=True) + EPS) * nf_ref[...]


def _final(xmid, yg, gates_t, norm_f, *, tm, first_block):
    t = xmid.shape[0]
    return pl.pallas_call(
        _final_kernel, grid=(t // tm,),
        in_specs=[pl.BlockSpec((tm, D_MODEL), lambda i: (i, 0)),
                  pl.BlockSpec((TOP_K, tm, PACK_W), lambda i: (0, i + first_block, 0)),
                  pl.BlockSpec((tm, TOP_K), lambda i: (i + first_block, 0)),
                  pl.BlockSpec((1, D_MODEL), lambda i: (0, 0))],
        out_specs=pl.BlockSpec((tm, D_MODEL), lambda i: (i, 0)),
        out_shape=jax.ShapeDtypeStruct((t, D_MODEL), _F32),
        compiler_params=_cparams(("parallel",)), name="final",
    )(xmid, yg, gates_t, norm_f)


def _pack_state(c, n, m):
    nb = c.shape[0]
    ncol = jnp.zeros((nb, ML_HEADS, ML_DK, ML_DV), _F32).at[..., 0].set(n.astype(_F32))
    cn = jnp.concatenate([c.astype(_F32), ncol], axis=-1)
    return cn, jnp.broadcast_to(m.astype(_F32)[:, :, None, None], (nb, ML_HEADS, 8, HEAD_W))


def _unpack_state(cn, m):
    return cn[..., :ML_DV], cn[..., ML_DV], m[:, :, 0, 0]


def kernel(x_prompt, x_sample, cache_k, cache_v, state_C, state_n, state_m, norm1, w_in, b_igate, b_fgate,
           lambda_q1, lambda_k1, lambda_q2, lambda_k2, subln, w_out, norm2, w_router, b_router, w_gate_up,
           b_gate_up, w_down, b_down, norm_f):
    nb, s, _ = x_prompt.shape
    db, ls, _ = x_sample.shape
    past = cache_k.shape[2]
    depth = w_in.shape[0]
    assert depth == 1 and s % PROJ_TILE == 0 and (db * ls) % 8 == 0
    l = 0
    lam_init = 0.8 - 0.6 * math.exp(-0.3 * l)
    t_p, t_s = nb * s, db * ls

    w = w_in[l]
    gate_cols = jnp.pad(w[:, 7 * GROUP_W:], ((0, 0), (0, HEAD_W - 2 * ML_HEADS)))
    w_nat = jnp.concatenate([w[:, :7 * GROUP_W], gate_cols], axis=1).astype(_BF16)
    b_gates = jnp.pad(jnp.concatenate([b_igate[l], b_fgate[l]]).astype(_F32), (0, HEAD_W - 2 * ML_HEADS))[None]
    inv = ROPE_THETA ** (-jnp.arange(0, DA_QK, 2, dtype=_F32) / DA_QK)
    inv_full = jnp.tile(inv, HEAD_W // (DA_QK // 2))[None]
    lamv = jnp.stack([lambda_q1[l], lambda_k1[l], lambda_q2[l], lambda_k2[l]]).astype(_F32)
    g1 = norm1[l][None]

    (q_p, kf_p, kb_p, vf_p, vt_p, _, mk_p, _, mo_p, _, gr_p, mqt_p, mvt_p) = _proj(
        x_prompt, g1, w_nat, b_gates, inv_full, tm=PROJ_TILE, pos_base=0, pos_mod=s)
    (q_s, kf_s, kb_s, vf_s, _, mq_s, mk_s, mv_s, mo_s, gc_s, gr_s, _, _) = _proj(
        x_sample.reshape(1, t_s, D_MODEL), g1, w_nat, b_gates, inv_full, tm=t_s, pos_base=past, pos_mod=ls)

    da_p = _attn_prompt(lamv, q_p, kb_p, vt_p, lam_init=lam_init)
    per_b = lambda a: a.reshape(db, ls, a.shape[-1])
    kct = jnp.transpose(cache_k[l], (0, 2, 3, 4, 1)).reshape(db, GROUP_W, past)
    vc = cache_v[l].reshape(db, past * DA_HEADS, DA_V)
    vn = vf_s.reshape(db, ls, GROUP_W).astype(_BF16)
    da_s = _attn_sample(lamv, per_b(q_s), per_b(kb_s), vn, kct, vc, lam_init=lam_init)

    assert PROJ_TILE % (2 * CHUNK) == 0
    h_p, st_p, m_p = _mlstm_pairs(mk_p, mqt_p, mvt_p, gr_p, jnp.zeros((nb, ML_HEADS, VT_ROWS, ML_DK), _F32),
                                  jnp.zeros((nb, ML_HEADS, 8, HEAD_W), _F32))
    gr_sb = gr_s.reshape(8, db, ls).transpose(1, 0, 2)
    h_s, cn_s, m_s = _mlstm(per_b(mq_s), per_b(mk_s), per_b(mv_s), per_b(gc_s), gr_sb,
                            *_pack_state(state_C[l], state_n[l], state_m[l]), chunk=ls, block=ls)
    c_p, n_p, mm_p = jnp.swapaxes(st_p[:, :, :ML_DV, :], -1, -2), st_p[:, :, ML_DV, :], m_p[:, :, 0, 0]
    c_s, n_s, mm_s = _unpack_state(cn_s, m_s)

    wo = w_out[l].astype(_BF16)
    w_rt = w_router[l].T.astype(_BF16)
    b_r = b_router[l].astype(_F32)[:, None]
    mix = functools.partial(_mix, subln=subln[l][None], w_out=wo, norm2=norm2[l][None], w_rt=w_rt, b_r=b_r,
                            lam_init=lam_init)
    flat = lambda a: a.reshape(-1, a.shape[-1])
    xm_p, xp_p, idx_p, gate_p, rank_p, cnt_p = mix(
        flat(da_p), flat(h_p), flat(mo_p), flat(x_prompt), cnt0=jnp.zeros((N_EXPERTS, 1), _F32), tm=PROJ_TILE)
    xm_s, xp_s, idx_s, gate_s, rank_s, cnt = mix(
        flat(da_s), flat(h_s), flat(mo_s), flat(x_sample), cnt0=cnt_p, tm=t_s)

    t_all = t_p + t_s
    rows = t_all * TOP_K
    n_blocks = -(-rows // MOE_BLOCK) + N_EXPERTS
    counts = cnt[:, 0].astype(jnp.int32)
    padded = (counts + MOE_BLOCK - 1) // MOE_BLOCK * MOE_BLOCK
    pend = jnp.cumsum(padded)
    pstart = pend - padded
    idx = jnp.concatenate([idx_p, idx_s], axis=1)
    experts = jnp.arange(N_EXPERTS, dtype=jnp.int32)[:, None, None]
    first_row = jnp.sum(jnp.where(idx[None] == experts, pstart[:, None, None], 0), axis=0)
    dest = first_row + jnp.concatenate([rank_p, rank_s], axis=1)
    n_used = (pend[-1] // MOE_BLOCK).astype(jnp.int32)[None]
    blk_lo = jnp.arange(n_blocks, dtype=jnp.int32) * MOE_BLOCK
    blk_e = jnp.minimum(jnp.sum((pend[None, :] <= blk_lo[:, None]).astype(jnp.int32), axis=1), N_EXPERTS - 1)

    assert t_all % (SC_WORKERS * DISPATCH_CHUNK) == 0 and rows % (SC_WORKERS * COMBINE_CHUNK) == 0
    dest3 = dest.reshape(TOP_K, t_all // DISPATCH_CHUNK, DISPATCH_CHUNK).transpose(1, 0, 2)
    xb = _sc_dispatch(jnp.concatenate([xp_p, xp_s], axis=0), dest3, n_blocks * MOE_BLOCK)
    yb = _moe(blk_e, n_used, xb, w_gate_up[l], b_gate_up[l][:, None, :], w_down[l], b_down[l][:, None, :])
    yg = _sc_gather(yb, dest.reshape(rows // COMBINE_CHUNK, COMBINE_CHUNK)).reshape(TOP_K, t_all, PACK_W)
    gates_t = jnp.concatenate([gate_p, gate_s], axis=1).T
    nf = norm_f[None]
    y_p = _final(xm_p, yg, gates_t, nf, tm=PROJ_TILE, first_block=0).reshape(nb, s, D_MODEL)
    y_s = _final(xm_s, yg, gates_t, nf, tm=t_s, first_block=t_p // t_s).reshape(db, ls, D_MODEL)

    st = lambda a: a[None]
    return (y_p, y_s,
            st(kf_p.reshape(nb, s, DA_HEADS, 2, DA_QK)), st(vf_p.reshape(nb, s, DA_HEADS, DA_V)),
            st(c_p), st(n_p), st(mm_p),
            st(kf_s.reshape(db, ls, DA_HEADS, 2, DA_QK)), st(vf_s.reshape(db, ls, DA_HEADS, DA_V)),
            st(c_s.astype(state_C.dtype)), st(n_s.astype(state_n.dtype)), st(mm_s.astype(state_m.dtype)))
```

```python
import functools
import math

import jax
import jax.numpy as jnp
from jax import lax
from jax.experimental import pallas as pl
from jax.experimental.pallas import tpu as pltpu
from jax.experimental.pallas import tpu_sc as plsc

D_MODEL = 1024
CHUNK = 64
DA_HEADS = 4
DA_QK = 64
DA_V = 128
ML_HEADS = 4
ML_DK = 128
ML_DV = 128
HEAD_W = 128
GROUP_W = 512
ROPE_THETA = 10000.0
N_EXPERTS = 32
TOP_K = 4
D_FF = 1024
SWIGLU_LIMIT = 7.0
SWIGLU_ALPHA = 1.702
EPS = 1e-6
NEG_BIG = -1e30

LOG2E = math.log2(math.e)
VT_ROWS = 144
PROJ_TILE = 512
FINAL_TILE = 1024
ATTN_TQ = 512
MOE_BLOCK = 512
FF_CHUNK = 512
PACK_W = D_MODEL // 2
SC_CORES = 2
SC_SUBCORES = 16
SC_WORKERS = SC_CORES * SC_SUBCORES
DISPATCH_CHUNK = 64
COMBINE_CHUNK = 80
VMEM_LIMIT = 56 * 1024 * 1024

_F32 = jnp.float32
_BF16 = jnp.bfloat16


def _cparams(sem):
    return pltpu.CompilerParams(dimension_semantics=sem, vmem_limit_bytes=VMEM_LIMIT)


def _dot(a, b, dims=(((1,), (0,)), ((), ())), precision=None):
    return lax.dot_general(a, b, dims, precision=precision, preferred_element_type=_F32)


_NT = (((1,), (1,)), ((), ()))
_TN = (((0,), (0,)), ((), ()))


def _pack_rows(x):
    half = x.shape[1] // 2
    bits = lambda v: lax.bitcast_convert_type(v.astype(_BF16).astype(_F32), jnp.int32)
    return bits(x[:, :half]) | lax.shift_right_logical(bits(x[:, half:]), 16)


def _unpack_rows(p):
    hi = lax.bitcast_convert_type(p & jnp.int32(-65536), _F32)
    lo = lax.bitcast_convert_type(lax.shift_left(p, 16), _F32)
    return hi, lo


def _proj_kernel(x_ref, g_ref, w_ref, bg_ref, inv_ref, *out_refs, names, tm, pos_base, pos_mod):
    out = dict(zip(names, out_refs))
    x = x_ref[0]
    xn = (x * lax.rsqrt(jnp.mean(x * x, axis=-1, keepdims=True) + EPS) * g_ref[...]).astype(_BF16)

    row = pl.program_id(1) * tm + lax.broadcasted_iota(jnp.int32, (tm, 1), 0)
    pos = (pos_base + row % pos_mod).astype(_F32)
    ang = pos * inv_ref[...]
    cos = jnp.cos(ang)
    sin = jnp.sin(ang)
    lane = lax.broadcasted_iota(jnp.int32, (1, HEAD_W), 1)
    first = (lane % DA_QK) < (DA_QK // 2)
    sin = jnp.where(first, -sin, sin)

    def rope(z):
        rot = jnp.where(first, pltpu.roll(z, HEAD_W - DA_QK // 2, 1), pltpu.roll(z, DA_QK // 2, 1))
        return z * cos + rot * sin

    def group(c, width=GROUP_W):
        return _dot(xn, w_ref[:, c * GROUP_W:c * GROUP_W + width])

    def put_q(zq):
        for h in range(DA_HEADS):
            sl = slice(h * HEAD_W, (h + 1) * HEAD_W)
            out["q"][0, :, sl] = (rope(zq[:, sl]) * (DA_QK ** -0.5 * LOG2E)).astype(_BF16)

    def put_k(zk):
        for h in range(DA_HEADS):
            sl = slice(h * HEAD_W, (h + 1) * HEAD_W)
            rk = rope(zk[:, sl])
            out["kf"][0, :, sl] = rk
            out["kb"][0, :, sl] = rk.astype(_BF16)

    def with_ones_rows(zt):
        ones = jnp.ones((VT_ROWS - DA_V, tm), _BF16)
        return jnp.concatenate(
            [part for h in range(DA_HEADS) for part in (zt[h * DA_V:(h + 1) * DA_V], ones)], axis=0)

    def put_v(zv):
        for h in range(DA_HEADS):
            out["vf"][0, pl.ds(h, tm, stride=DA_HEADS), :] = zv[:, h * DA_V:(h + 1) * DA_V]
        if "vt" in out:
            out["vt"][0, 0] = with_ones_rows(zv.T.astype(_BF16))

    def put_mq(z):
        if "mq" in out:
            out["mq"][0] = z.astype(_BF16)
        if "mqt" in out:
            out["mqt"][0, 0] = z.T.astype(_BF16)

    def put_mk(z):
        out["mk"][0] = (z * (ML_DK ** -0.5)).astype(_BF16)

    def put_mv(z):
        if "mv" in out:
            out["mv"][0] = z.astype(_BF16)
        if "mvt" in out:
            out["mvt"][0, 0] = with_ones_rows(z.T.astype(_BF16))

    def put_mo(z):
        out["mo"][0] = z

    def put_gates(z):
        zg = z + bg_ref[...]
        lane_g = lax.broadcasted_iota(jnp.int32, zg.shape, 1)
        logsig = jnp.minimum(zg, 0.0) - jnp.log(1.0 + jnp.exp(-jnp.abs(zg)))
        gates = jnp.where(lane_g < ML_HEADS, zg, logsig)
        if "gc" in out:
            out["gc"][0] = gates[:, :8]
        out["gr"][0] = gates.T[:8, :]

    sinks = [(put_q, GROUP_W), (put_k, GROUP_W), (put_v, GROUP_W), (put_mq, GROUP_W), (put_mk, GROUP_W),
             (put_mv, GROUP_W), (put_mo, GROUP_W), (put_gates, HEAD_W)]
    nxt = group(0, sinks[0][1])
    for c, (sink, _) in enumerate(sinks):
        cur = nxt
        if c + 1 < len(sinks):
            nxt = group(c + 1, sinks[c + 1][1])
        sink(cur)


def _proj(x3, norm1, w_nat, b_gates, inv_full, *, names, tm, pos_base, pos_mod):
    nb, s, _ = x3.shape
    nt = s // tm
    wcols = w_nat.shape[1]
    tok = lambda w, dt: (jax.ShapeDtypeStruct((nb, s, w), dt), pl.BlockSpec((1, tm, w), lambda b, i: (b, i, 0)))
    slab = lambda rows: (jax.ShapeDtypeStruct((nb, nt, rows, tm), _BF16),
                         pl.BlockSpec((1, 1, rows, tm), lambda b, i: (b, i, 0, 0)))
    const = lambda shape: pl.BlockSpec(shape, lambda b, i: (0,) * len(shape))
    kinds = {
        "q": tok(GROUP_W, _BF16), "kf": tok(GROUP_W, _F32), "kb": tok(GROUP_W, _BF16),
        "vf": (jax.ShapeDtypeStruct((nb, s * DA_HEADS, DA_V), _F32),
               pl.BlockSpec((1, tm * DA_HEADS, DA_V), lambda b, i: (b, i, 0))),
        "vt": slab(DA_HEADS * VT_ROWS),
        "mq": tok(GROUP_W, _BF16), "mk": tok(GROUP_W, _BF16), "mv": tok(GROUP_W, _BF16), "mo": tok(GROUP_W, _F32),
        "gc": tok(8, _F32),
        "gr": (jax.ShapeDtypeStruct((nb, 8, s), _F32), pl.BlockSpec((1, 8, tm), lambda b, i: (b, 0, i))),
        "mqt": slab(GROUP_W), "mvt": slab(ML_HEADS * VT_ROWS),
    }
    outs = pl.pallas_call(
        functools.partial(_proj_kernel, names=names, tm=tm, pos_base=pos_base, pos_mod=pos_mod),
        grid=(nb, nt),
        in_specs=[pl.BlockSpec((1, tm, D_MODEL), lambda b, i: (b, i, 0)), const((1, D_MODEL)),
                  const((D_MODEL, wcols)), const((1, HEAD_W)), const((1, HEAD_W))],
        out_specs=[kinds[n][1] for n in names], out_shape=[kinds[n][0] for n in names],
        compiler_params=_cparams(("parallel", "parallel")), name="proj",
    )(x3, norm1, w_nat, b_gates, inv_full)
    return dict(zip(names, outs))


def _lambda(lam_ref, lam_init):
    lv = lam_ref[...]
    s1 = jnp.sum(lv[0:1] * lv[1:2], axis=-1, keepdims=True)
    s2 = jnp.sum(lv[2:3] * lv[3:4], axis=-1, keepdims=True)
    return jnp.exp(s1) - jnp.exp(s2) + lam_init


def _split_components(q):
    lane = lax.broadcasted_iota(jnp.int32, q.shape, 1)
    zero = jnp.zeros_like(q)
    return jnp.concatenate([jnp.where(lane < DA_QK, q, zero), jnp.where(lane >= DA_QK, q, zero)], axis=0)


def _attn_kernel(lam_ref, q_ref, k_ref, vt_ref, o_ref, acc_ref, s_ref, *, tq, tk, lam_init):
    i = pl.program_id(1)
    heads = range(DA_HEADS)
    qz = [_split_components(q_ref[0, :, h * HEAD_W:(h + 1) * HEAD_W]) for h in heads]
    acc_ref[...] = jnp.zeros_like(acc_ref)

    def scores(h, j):
        k_t = k_ref[0, pl.ds(pl.multiple_of(j * tk, tk), tk), h * HEAD_W:(h + 1) * HEAD_W]
        return _dot(k_t, qz[h], _NT)

    def step(j, ms, last):
        if last:
            kpos = j * tk + lax.broadcasted_iota(jnp.int32, (tk, 1), 0)
            qpos = i * tq + lax.broadcasted_iota(jnp.int32, (1, 2 * tq), 1) % tq
            visible = kpos < (qpos // CHUNK + 1) * CHUNK
        out = []
        s = s_ref[...]
        for h in heads:
            s_next = None
            if h + 1 < DA_HEADS:
                s_next = scores(h + 1, j)
            elif not last:
                s_next = scores(0, j + 1)
            if last:
                s = jnp.where(visible, s, NEG_BIG)
            m_new = jnp.maximum(ms[h], jnp.max(s, axis=0, keepdims=True))
            alpha = jnp.exp2(ms[h] - m_new)
            p = jnp.exp2(s - m_new).astype(_BF16)
            acc_ref[h] = alpha * acc_ref[h] + _dot(vt_ref[0, j, h * VT_ROWS:(h + 1) * VT_ROWS, :], p)
            out.append(m_new)
            s = s_next
        if not last:
            s_ref[...] = s
        return tuple(out)

    n_full = (i * tq) // tk
    s_ref[...] = scores(0, 0)
    init = tuple(jnp.full((1, 2 * tq), NEG_BIG, _F32) for _ in heads)
    ms = lax.fori_loop(0, n_full, lambda j, c: step(j, c, False), init)
    step(n_full, ms, True)

    lam = _lambda(lam_ref, lam_init)
    for h in heads:
        acc = acc_ref[h]
        o = acc[:DA_V] / acc[DA_V:DA_V + 1]
        o_ref[0, :, h * HEAD_W:(h + 1) * HEAD_W] = (o[:, :tq] - lam * o[:, tq:]).T


def _attn_prompt(lamv, q, kb, vt, *, lam_init):
    nb, s, _ = q.shape
    nk, tk = vt.shape[1], vt.shape[3]
    tq = ATTN_TQ
    assert tk % tq == 0
    return pl.pallas_call(
        functools.partial(_attn_kernel, tq=tq, tk=tk, lam_init=lam_init),
        grid=(nb, s // tq),
        in_specs=[pl.BlockSpec((4, DA_QK), lambda b, i: (0, 0)),
                  pl.BlockSpec((1, tq, GROUP_W), lambda b, i: (b, i, 0)),
                  pl.BlockSpec((1, s, GROUP_W), lambda b, i: (b, 0, 0)),
                  pl.BlockSpec((1, nk, DA_HEADS * VT_ROWS, tk), lambda b, i: (b, 0, 0, 0))],
        out_specs=pl.BlockSpec((1, tq, GROUP_W), lambda b, i: (b, i, 0)),
        out_shape=jax.ShapeDtypeStruct((nb, s, GROUP_W), _F32),
        scratch_shapes=[pltpu.VMEM((DA_HEADS, VT_ROWS, 2 * tq), _F32), pltpu.VMEM((tk, 2 * tq), _F32)],
        compiler_params=_cparams(("parallel", "arbitrary")), name="attn",
    )(lamv, q, kb, vt)


def _attn_sample_kernel(lam_ref, q_ref, kn_ref, vn_ref, kc_ref, vc_ref, o_ref, *, lq, past, lam_init):
    lam = _lambda(lam_ref, lam_init)
    for h in range(DA_HEADS):
        sl = slice(h * HEAD_W, (h + 1) * HEAD_W)
        qz = _split_components(q_ref[0, :, sl])
        kct = kc_ref[0, sl, :].astype(_BF16)
        vc = vc_ref[0, pl.ds(h, past, stride=DA_HEADS), :].astype(_BF16)
        s_c = _dot(qz, kct)
        s_n = _dot(qz, kn_ref[0, :, sl], _NT)
        m = jnp.maximum(jnp.max(s_c, axis=-1, keepdims=True), jnp.max(s_n, axis=-1, keepdims=True))
        p_c = jnp.exp2(s_c - m)
        p_n = jnp.exp2(s_n - m)
        l = jnp.sum(p_c, axis=-1, keepdims=True) + jnp.sum(p_n, axis=-1, keepdims=True)
        o = (_dot(p_c.astype(_BF16), vc) + _dot(p_n.astype(_BF16), vn_ref[0, :, sl])) / l
        o_ref[0, :, sl] = o[:lq] - lam * o[lq:]


def _attn_sample(lamv, q, kn, vn, kct, vc, *, lam_init):
    nb, lq, _ = q.shape
    past = kct.shape[2]
    new = pl.BlockSpec((1, lq, GROUP_W), lambda b: (b, 0, 0))
    return pl.pallas_call(
        functools.partial(_attn_sample_kernel, lq=lq, past=past, lam_init=lam_init),
        grid=(nb,),
        in_specs=[pl.BlockSpec((4, DA_QK), lambda b: (0, 0)), new, new, new,
                  pl.BlockSpec((1, GROUP_W, past), lambda b: (b, 0, 0)),
                  pl.BlockSpec((1, past * DA_HEADS, DA_V), lambda b: (b, 0, 0))],
        out_specs=new, out_shape=jax.ShapeDtypeStruct((nb, lq, GROUP_W), _F32),
        compiler_params=_cparams(("parallel",)), name="attn_sample",
    )(lamv, q, kn, vn, kct, vc)


def _mlstm_kernel(q_ref, k_ref, v_ref, gc_ref, gr_ref, c0_ref, n0_ref, m0_ref, h_ref, c_ref, n_ref, m_ref,
                  c_sc, n_sc, m_sc, *, chunk, nchunks):
    j = pl.program_id(1)

    @pl.when(j == 0)
    def _():
        c_sc[...] = c0_ref[0]
        n_sc[...] = n0_ref[0]
        m_sc[...] = m0_ref[0]

    L = chunk
    r_i = lax.broadcasted_iota(jnp.int32, (L, L), 0)
    c_i = lax.broadcasted_iota(jnp.int32, (L, L), 1)
    causal = r_i >= c_i
    tril = causal.astype(_F32)
    triu = (r_i <= c_i).astype(_F32)
    hi = lax.Precision.HIGHEST

    for c in range(nchunks):
        rows = slice(c * L, (c + 1) * L)
        gcol = gc_ref[0, rows, :]
        grow = gr_ref[0, :, rows]
        fc_all = _dot(tril, gcol, precision=hi)
        fr_all = _dot(grow, triu, precision=hi)
        for hh in range(ML_HEADS):
            sl = slice(hh * HEAD_W, (hh + 1) * HEAD_W)
            m = m_sc[hh][0:1, 0:1]
            fc = fc_all[:, ML_HEADS + hh:ML_HEADS + hh + 1]
            fr = fr_all[ML_HEADS + hh:ML_HEADS + hh + 1, :]
            igc = gcol[:, hh:hh + 1]
            igr = grow[hh:hh + 1, :]
            a = jnp.where(causal, fc - fr + igr, -jnp.inf)
            b = fc + m
            m_t = jnp.maximum(b, jnp.max(a, axis=-1, keepdims=True))
            w = jnp.exp(a - m_t)
            sc = jnp.exp(b - m_t)
            q = q_ref[0, rows, sl]
            k = k_ref[0, rows, sl]
            v = v_ref[0, rows, sl]
            wqk = w * _dot(q, k, _NT)
            cmat = c_sc[hh]
            nrow = n_sc[hh:hh + 1, :]
            qn = jnp.sum(q.astype(_F32) * nrow, axis=-1, keepdims=True)
            num = _dot(wqk.astype(_BF16), v) + sc * _dot(q, cmat.astype(_BF16))
            den = jnp.sum(wqk, axis=-1, keepdims=True) + sc * qn
            h_ref[0, rows, sl] = num / jnp.maximum(jnp.abs(den), jnp.exp(-m_t))

            fl = fc[L - 1:L, :]
            g = fl - fc + igc
            m_new = jnp.maximum(fl + m, jnp.max(g, axis=0, keepdims=True))
            decay = jnp.exp(fl + m - m_new)
            kw = k.astype(_F32) * jnp.exp(g - m_new)
            c_sc[hh] = decay * cmat + _dot(kw.astype(_BF16), v, _TN)
            n_sc[hh:hh + 1, :] = decay * nrow + jnp.sum(kw, axis=0, keepdims=True)
            m_sc[hh] = jnp.broadcast_to(m_new, m_sc.shape[1:])

    @pl.when(j == pl.num_programs(1) - 1)
    def _():
        c_ref[0] = c_sc[...]
        n_ref[0] = n_sc[...]
        m_ref[0] = m_sc[...]


def _mlstm(q, k, v, gc, gr, c0, n0, m0, *, chunk, block):
    nb, s, _ = q.shape
    tok = lambda w: pl.BlockSpec((1, block, w), lambda b, j: (b, j, 0))
    cspec = pl.BlockSpec((1, ML_HEADS, ML_DK, ML_DV), lambda b, j: (b, 0, 0, 0))
    nspec = pl.BlockSpec((1, ML_HEADS, ML_DK), lambda b, j: (b, 0, 0))
    mspec = pl.BlockSpec((1, ML_HEADS, 8, HEAD_W), lambda b, j: (b, 0, 0, 0))
    return pl.pallas_call(
        functools.partial(_mlstm_kernel, chunk=chunk, nchunks=block // chunk),
        grid=(nb, s // block),
        in_specs=[tok(GROUP_W), tok(GROUP_W), tok(GROUP_W), tok(8),
                  pl.BlockSpec((1, 8, block), lambda b, j: (b, 0, j)), cspec, nspec, mspec],
        out_specs=[tok(GROUP_W), cspec, nspec, mspec],
        out_shape=[jax.ShapeDtypeStruct((nb, s, GROUP_W), _F32), jax.ShapeDtypeStruct(c0.shape, _F32),
                   jax.ShapeDtypeStruct(n0.shape, _F32), jax.ShapeDtypeStruct(m0.shape, _F32)],
        scratch_shapes=[pltpu.VMEM((ML_HEADS, ML_DK, ML_DV), _F32), pltpu.VMEM((ML_HEADS, ML_DK), _F32),
                        pltpu.VMEM((ML_HEADS, 8, HEAD_W), _F32)],
        compiler_params=_cparams(("parallel", "arbitrary")), name="mlstm",
    )(q, k, v, gc, gr, c0, n0, m0)


def _mlstm_pairs_kernel(k_ref, qt_ref, vt_ref, gr_ref, st0_ref, m0_ref, h_ref, st_ref, m_ref, st_sc, m_sc,
                        *, block):
    j = pl.program_id(1)

    @pl.when(j == 0)
    def _():
        st_sc[...] = st0_ref[0]
        m_sc[...] = m0_ref[0]

    L, W = CHUNK, 2 * CHUNK
    g8 = gr_ref[0]
    pos = lax.broadcasted_iota(jnp.int32, (1, block), 1) % L
    roll = lambda x, sh: pltpu.roll(x, sh % block, 1)
    steps = [1 << b for b in range(L.bit_length() - 1)]

    def scan(x, combine, fill, reverse=False):
        for sh in steps:
            ok = (pos < L - sh) if reverse else (pos >= sh)
            x = combine(x, jnp.where(ok, roll(x, -sh if reverse else sh), fill))
        return x

    to_head_rows = lambda x: pltpu.roll(x, ML_HEADS, 0)
    f8 = to_head_rows(scan(g8, jnp.add, 0.0))
    d8 = g8 - f8
    cm8 = scan(d8, jnp.maximum, -jnp.inf)
    big_g = to_head_rows(scan(g8, jnp.add, 0.0, reverse=True) - g8) + g8
    mg8 = jnp.maximum(scan(big_g, jnp.maximum, -jnp.inf), scan(big_g, jnp.maximum, -jnp.inf, reverse=True))
    ws8 = jnp.exp(big_g - mg8)
    d_cols = d8.T

    s_i = lax.broadcasted_iota(jnp.int32, (W, W), 0)
    t_i = lax.broadcasted_iota(jnp.int32, (W, W), 1)
    allowed = (s_i // L == t_i // L) & (s_i <= t_i)
    lane_w = lax.broadcasted_iota(jnp.int32, (1, W), 1)
    first = lane_w < L

    sts = [st_sc[hh] for hh in range(ML_HEADS)]
    ms = [m_sc[hh][0:1, 0:1] for hh in range(ML_HEADS)]

    def independent(p, hh):
        lanes = slice(p * W, (p + 1) * W)
        feat = slice(hh * HEAD_W, (hh + 1) * HEAD_W)
        kp = k_ref[0, lanes, feat]
        qt = qt_ref[0, 0, feat, lanes]
        v1t = vt_ref[0, 0, hh * VT_ROWS:(hh + 1) * VT_ROWS, lanes]
        ws = ws8[hh:hh + 1, lanes]
        v1f = v1t.astype(_F32)
        return dict(
            p=p, hh=hh, qt=qt, v1t=v1t, qk_t=_dot(kp, qt),
            upd_a=_dot((v1f * jnp.where(first, ws, 0.0)).astype(_BF16), kp),
            upd_b=_dot((v1f * jnp.where(first, 0.0, ws)).astype(_BF16), kp))

    def dependent(u):
        p, hh, qt = u["p"], u["hh"], u["qt"]
        lanes = slice(p * W, (p + 1) * W)
        row = lambda x: x[hh:hh + 1, lanes]
        cm, f_row = row(cm8), row(f8)
        w_t = jnp.exp(jnp.where(allowed, d_cols[lanes, hh:hh + 1] - cm, -jnp.inf))
        intra = _dot(u["v1t"], (w_t * u["qk_t"]).astype(_BF16))
        st_a, m_a = sts[hh], ms[hh]
        fl_a, mg_a = f_row[:, L - 1:L], row(mg8)[:, 0:1]
        fl_b, mg_b = f_row[:, W - 1:W], row(mg8)[:, L:L + 1]
        m_b = jnp.maximum(fl_a + m_a, mg_a)
        st_b = jnp.exp(fl_a + m_a - m_b) * st_a + jnp.exp(mg_a - m_b) * u["upd_a"]
        m_c = jnp.maximum(fl_b + m_b, mg_b)
        sts[hh] = jnp.exp(fl_b + m_b - m_c) * st_b + jnp.exp(mg_b - m_c) * u["upd_b"]
        ms[hh] = m_c
        inter = jnp.where(first, _dot(st_a.astype(_BF16), qt), _dot(st_b.astype(_BF16), qt))
        m_prev = jnp.where(first, m_a, m_b)
        mt = jnp.maximum(m_prev, cm)
        tot = jnp.exp(cm - mt) * intra + jnp.exp(m_prev - mt) * inter
        den = jnp.maximum(jnp.abs(tot[ML_DV:ML_DV + 1]), jnp.exp(-(f_row + mt)))
        h_ref[0, lanes, hh * HEAD_W:(hh + 1) * HEAD_W] = (tot[:ML_DV] / den).T

    units = [(p, hh) for p in range(block // W) for hh in range(ML_HEADS)]
    nxt = independent(*units[0])
    for n in range(len(units)):
        cur = nxt
        if n + 1 < len(units):
            nxt = independent(*units[n + 1])
        dependent(cur)
    for hh in range(ML_HEADS):
        st_sc[hh] = sts[hh]
        m_sc[hh] = jnp.broadcast_to(ms[hh], m_sc.shape[1:])

    @pl.when(j == pl.num_programs(1) - 1)
    def _():
        st_ref[0] = st_sc[...]
        m_ref[0] = m_sc[...]


def _mlstm_pairs(k, qt, vt, gr, st0, m0):
    nb, s, _ = k.shape
    block = qt.shape[3]
    nt = s // block
    state = pl.BlockSpec((1, ML_HEADS, VT_ROWS, ML_DK), lambda b, j: (b, 0, 0, 0))
    mspec = pl.BlockSpec((1, ML_HEADS, 8, HEAD_W), lambda b, j: (b, 0, 0, 0))
    tok = pl.BlockSpec((1, block, GROUP_W), lambda b, j: (b, j, 0))
    return pl.pallas_call(
        functools.partial(_mlstm_pairs_kernel, block=block),
        grid=(nb, nt),
        in_specs=[tok, pl.BlockSpec((1, 1, GROUP_W, block), lambda b, j: (b, j, 0, 0)),
                  pl.BlockSpec((1, 1, ML_HEADS * VT_ROWS, block), lambda b, j: (b, j, 0, 0)),
                  pl.BlockSpec((1, 8, block), lambda b, j: (b, 0, j)), state, mspec],
        out_specs=[tok, state, mspec],
        out_shape=[jax.ShapeDtypeStruct((nb, s, GROUP_W), _F32), jax.ShapeDtypeStruct(st0.shape, _F32),
                   jax.ShapeDtypeStruct(m0.shape, _F32)],
        scratch_shapes=[pltpu.VMEM((ML_HEADS, VT_ROWS, ML_DK), _F32), pltpu.VMEM((ML_HEADS, 8, HEAD_W), _F32)],
        compiler_params=_cparams(("parallel", "arbitrary")), name="mlstm_pairs",
    )(k, qt, vt, gr, st0, m0)


def _mix_kernel(da_ref, ml_ref, mo_ref, x_ref, subln_ref, wo_ref, g2_ref, wr_ref, br_ref, cnt0_ref,
                xmid_ref, xp_ref, idx_ref, gate_ref, rank_ref, cnt_ref, cnt_sc, *, tm, lam_init):
    step = pl.program_id(0)

    @pl.when(step == 0)
    def _():
        cnt_sc[...] = cnt0_ref[...]

    parts = []
    for h in range(DA_HEADS):
        d = da_ref[:, h * HEAD_W:(h + 1) * HEAD_W]
        d = d * lax.rsqrt(jnp.mean(d * d, axis=-1, keepdims=True) + EPS) * subln_ref[...]
        parts.append((d * (1.0 - lam_init)).astype(_BF16))
    parts.append((jax.nn.sigmoid(mo_ref[...]) * ml_ref[...]).astype(_BF16))
    xm = x_ref[...] + _dot(jnp.concatenate(parts, axis=1), wo_ref[...])
    xmid_ref[...] = xm
    xn_f = xm * lax.rsqrt(jnp.mean(xm * xm, axis=-1, keepdims=True) + EPS) * g2_ref[...]
    xp_ref[...] = _pack_rows(xn_f)
    xn = xn_f.astype(_BF16)

    logits = _dot(wr_ref[...], xn, _NT) + br_ref[...]
    e_i = lax.broadcasted_iota(jnp.int32, logits.shape, 0)
    member = jnp.zeros(logits.shape, jnp.bool_)
    vals, idxs = [], []
    for _ in range(TOP_K):
        mx = jnp.max(logits, axis=0, keepdims=True)
        sel = jnp.min(jnp.where(logits == mx, e_i, N_EXPERTS), axis=0, keepdims=True)
        hit = e_i == sel
        member = member | hit
        logits = jnp.where(hit, -jnp.inf, logits)
        vals.append(mx)
        idxs.append(sel)
    ex = [jnp.exp(v - vals[0]) for v in vals]
    tot = ex[0] + ex[1] + ex[2] + ex[3]
    idx_ref[...] = jnp.concatenate(idxs, axis=0)
    gate_ref[...] = jnp.concatenate([e / tot for e in ex], axis=0)

    upper = (lax.broadcasted_iota(jnp.int32, (tm, tm), 0) < lax.broadcasted_iota(jnp.int32, (tm, tm), 1))
    memf = member.astype(_F32)
    before = _dot(memf.astype(_BF16), upper.astype(_BF16)) + cnt_sc[...]
    rank_ref[...] = jnp.concatenate(
        [jnp.sum(jnp.where(e_i == s, before, 0.0), axis=0, keepdims=True) for s in idxs], axis=0).astype(jnp.int32)
    cnt_sc[...] += jnp.sum(memf, axis=1, keepdims=True)
    cnt_ref[...] = cnt_sc[...]


def _mix(da, ml, mo, x, subln, w_out, norm2, w_rt, b_r, cnt0, *, tm, lam_init):
    t = x.shape[0]
    tok = lambda w: pl.BlockSpec((tm, w), lambda i: (i, 0))
    const = lambda shape: pl.BlockSpec(shape, lambda i: (0,) * len(shape))
    lane_tok = pl.BlockSpec((TOP_K, tm), lambda i: (0, i))
    return pl.pallas_call(
        functools.partial(_mix_kernel, tm=tm, lam_init=lam_init),
        grid=(t // tm,),
        in_specs=[tok(GROUP_W), tok(GROUP_W), tok(GROUP_W), tok(D_MODEL), const((1, DA_V)),
                  const((D_MODEL, D_MODEL)), const((1, D_MODEL)), const((N_EXPERTS, D_MODEL)),
                  const((N_EXPERTS, 1)), const((N_EXPERTS, 1))],
        out_specs=[tok(D_MODEL), tok(PACK_W), lane_tok, lane_tok, lane_tok, const((N_EXPERTS, 1))],
        out_shape=[jax.ShapeDtypeStruct((t, D_MODEL), _F32), jax.ShapeDtypeStruct((t, PACK_W), jnp.int32),
                   jax.ShapeDtypeStruct((TOP_K, t), jnp.int32), jax.ShapeDtypeStruct((TOP_K, t), _F32),
                   jax.ShapeDtypeStruct((TOP_K, t), jnp.int32), jax.ShapeDtypeStruct((N_EXPERTS, 1), _F32)],
        scratch_shapes=[pltpu.VMEM((N_EXPERTS, 1), _F32)],
        compiler_params=_cparams(("arbitrary",)), name="mix",
    )(da, ml, mo, x, subln, w_out, norm2, w_rt, b_r, cnt0)


def _moe_kernel(be_ref, nu_ref, nv_ref, x_ref, wgu_ref, bgu_ref, wd_ref, bd_ref, y_ref, wgu_sc, wd_sc):
    j = pl.program_id(0)
    e_prev = be_ref[jnp.maximum(j - 1, 0)]
    fresh = (j == 0) | (be_ref[j] != e_prev)

    @pl.when(fresh & (j < nu_ref[0]))
    def _():
        rows = 128
        def cast_gu(r, _):
            sl = pl.ds(pl.multiple_of(r * rows, rows), rows)
            wgu_sc[sl, :] = wgu_ref[0, sl, :].astype(_BF16)
            return 0
        lax.fori_loop(0, D_MODEL // rows, cast_gu, 0)
        def cast_d(r, _):
            sl = pl.ds(pl.multiple_of(r * rows, rows), rows)
            wd_sc[sl, :] = wd_ref[0, sl, :].astype(_BF16)
            return 0
        lax.fori_loop(0, D_FF // rows, cast_d, 0)

    @pl.when(j >= nu_ref[0])
    def _():
        y_ref[...] = jnp.zeros_like(y_ref)

    def ffn(row0):
        hi, lo = _unpack_rows(x_ref[row0:, :])
        x = jnp.concatenate([hi.astype(_BF16), lo.astype(_BF16)], axis=1)
        n_chunks = D_FF // FF_CHUNK

        def gate_up(c):
            lo = c * FF_CHUNK
            return (_dot(x, wgu_sc[:, lo:lo + FF_CHUNK]), _dot(x, wgu_sc[:, D_FF + lo:D_FF + lo + FF_CHUNK]))

        acc = None
        nxt = gate_up(0)
        for c in range(n_chunks):
            lo = c * FF_CHUNK
            gate, up = nxt
            if c + 1 < n_chunks:
                nxt = gate_up(c + 1)
            gate = jnp.minimum(gate + bgu_ref[0, :, lo:lo + FF_CHUNK], SWIGLU_LIMIT)
            up = jnp.clip(up + bgu_ref[0, :, D_FF + lo:D_FF + lo + FF_CHUNK], -SWIGLU_LIMIT, SWIGLU_LIMIT)
            act = (up + 1.0) * (gate * jax.nn.sigmoid(SWIGLU_ALPHA * gate))
            down = _dot(act.astype(_BF16), wd_sc[lo:lo + FF_CHUNK, :])
            acc = down if acc is None else acc + down
        y_ref[row0:, :] = _pack_rows(acc + bd_ref[0])

    half = MOE_BLOCK // 2
    used = j < nu_ref[0]

    @pl.when(used & (nv_ref[j] > half))
    def _():
        ffn(0)

    @pl.when(used & (nv_ref[j] <= half))
    def _():
        y_ref[:half, :] = jnp.zeros((half, PACK_W), jnp.int32)
        ffn(half)


def _moe(blk_e, n_used, blk_rows, xb, w_gu, b_gu, w_d, b_d):
    n_blocks = xb.shape[0] // MOE_BLOCK
    last = lambda j, nu: jnp.minimum(j, nu[0] - 1)
    row_spec = pl.BlockSpec((MOE_BLOCK, PACK_W), lambda j, be, nu, nv: (last(j, nu), 0))
    exp_spec = lambda r, c: pl.BlockSpec((1, r, c), lambda j, be, nu, nv: (be[last(j, nu)], 0, 0))
    grid_spec = pltpu.PrefetchScalarGridSpec(
        num_scalar_prefetch=3, grid=(n_blocks,),
        in_specs=[row_spec, exp_spec(D_MODEL, 2 * D_FF), exp_spec(1, 2 * D_FF), exp_spec(D_FF, D_MODEL),
                  exp_spec(1, D_MODEL)],
        out_specs=pl.BlockSpec((MOE_BLOCK, PACK_W), lambda j, be, nu, nv: (j, 0)),
        scratch_shapes=[pltpu.VMEM((D_MODEL, 2 * D_FF), _BF16), pltpu.VMEM((D_FF, D_MODEL), _BF16)])
    return pl.pallas_call(
        _moe_kernel, grid_spec=grid_spec, out_shape=jax.ShapeDtypeStruct(xb.shape, jnp.int32),
        compiler_params=_cparams(("arbitrary",)), name="moe",
    )(blk_e, n_used, blk_rows, xb, w_gu, b_gu, w_d, b_d)


def _sc_mesh():
    return plsc.VectorSubcoreMesh(core_axis_name="c", subcore_axis_name="s")


def _sc_worker():
    return lax.axis_index("s") * SC_CORES + lax.axis_index("c")


def _sc_dispatch(sources, n_rows):
    n_src = len(sources)
    shapes = []
    for _, dest3 in sources:
        n_chunks, _, chunk = dest3.shape
        per_worker = n_chunks // SC_WORKERS
        assert per_worker * SC_WORKERS == n_chunks and chunk % 8 == 0
        shapes.append((per_worker, chunk))

    def body(*refs):
        x_hbms, d_hbms, o_hbm = refs[:n_src], refs[n_src:2 * n_src], refs[2 * n_src]
        scratch = refs[2 * n_src + 1:]
        load_a, load_b, scat_a, scat_b = scratch[3 * n_src:]
        worker = _sc_worker()
        for i, (per_worker, chunk) in enumerate(shapes):
            x_hbm, idx_v, rows_a, rows_b = x_hbms[i], scratch[3 * i], scratch[3 * i + 1], scratch[3 * i + 2]
            pltpu.sync_copy(d_hbms[i].at[worker], idx_v)

            def load(ci, rows, sem):
                row0 = (worker * per_worker + ci) * chunk
                return pltpu.make_async_copy(x_hbm.at[pl.ds(pl.multiple_of(row0, 8), chunk)], rows, sem)

            def scatters(ci, rows, sem):
                return [pltpu.make_async_copy(rows, o_hbm.at[idx_v.at[ci * TOP_K + k]], sem) for k in range(TOP_K)]

            def start(copies):
                for cp in copies:
                    cp.start()

            def wait(copies):
                for cp in copies:
                    cp.wait()

            load(0, rows_a, load_a).start()

            @pl.loop(0, per_worker // 2)
            def _(p):
                ca, cb = 2 * p, 2 * p + 1
                load(ca, rows_a, load_a).wait()
                load(cb, rows_b, load_b).start()
                start(scatters(ca, rows_a, scat_a))
                load(cb, rows_b, load_b).wait()
                wait(scatters(ca, rows_a, scat_a))
                load(jnp.minimum(ca + 2, per_worker - 1), rows_a, load_a).start()
                start(scatters(cb, rows_b, scat_b))
                wait(scatters(cb, rows_b, scat_b))

            last = per_worker - 1
            load(last, rows_a, load_a).wait()
            if per_worker % 2:
                start(scatters(last, rows_a, scat_a))
                wait(scatters(last, rows_a, scat_a))

    dma = pltpu.SemaphoreType.DMA
    scratch_types = []
    for per_worker, chunk in shapes:
        scratch_types += [pltpu.VMEM((per_worker * TOP_K, chunk), jnp.int32), pltpu.VMEM((chunk, PACK_W), jnp.int32),
                          pltpu.VMEM((chunk, PACK_W), jnp.int32)]
    tables = [dest3.reshape(SC_WORKERS, pw * TOP_K, chunk) for (_, dest3), (pw, chunk) in zip(sources, shapes)]
    return pl.kernel(
        body, out_type=jax.ShapeDtypeStruct((n_rows, PACK_W), jnp.int32), mesh=_sc_mesh(),
        scratch_types=scratch_types + [dma, dma, dma, dma],
        name="sc_dispatch")(*[xp for xp, _ in sources], *tables)


def _sc_gather(table, idx2):
    n_chunks, chunk = idx2.shape
    per_worker = n_chunks // SC_WORKERS
    assert per_worker * SC_WORKERS == n_chunks and chunk % 8 == 0

    def body(t_hbm, i_hbm, o_hbm, idx_v, rows_a, rows_b, gath_a, gath_b, put_a, put_b):
        worker = _sc_worker()
        pltpu.sync_copy(i_hbm.at[worker], idx_v)

        def gather(ci, rows, sem):
            return pltpu.make_async_copy(t_hbm.at[idx_v.at[ci]], rows, sem)

        def put(ci, rows, sem):
            row0 = (worker * per_worker + ci) * chunk
            return pltpu.make_async_copy(rows, o_hbm.at[pl.ds(pl.multiple_of(row0, 8), chunk)], sem)

        gather(0, rows_a, gath_a).start()

        @pl.loop(0, per_worker // 2)
        def _(p):
            ca, cb = 2 * p, 2 * p + 1
            gather(ca, rows_a, gath_a).wait()
            gather(cb, rows_b, gath_b).start()
            put(ca, rows_a, put_a).start()
            gather(cb, rows_b, gath_b).wait()
            put(ca, rows_a, put_a).wait()
            gather(jnp.minimum(ca + 2, per_worker - 1), rows_a, gath_a).start()
            put(cb, rows_b, put_b).start()
            put(cb, rows_b, put_b).wait()

        last = per_worker - 1
        gather(last, rows_a, gath_a).wait()
        if per_worker % 2:
            put(last, rows_a, put_a).start()
            put(last, rows_a, put_a).wait()

    dma = pltpu.SemaphoreType.DMA
    return pl.kernel(
        body, out_type=jax.ShapeDtypeStruct((n_chunks * chunk, PACK_W), jnp.int32), mesh=_sc_mesh(),
        scratch_types=[pltpu.VMEM((per_worker, chunk), jnp.int32), pltpu.VMEM((chunk, PACK_W), jnp.int32),
                       pltpu.VMEM((chunk, PACK_W), jnp.int32), dma, dma, dma, dma],
        name="sc_gather")(table, idx2.reshape(SC_WORKERS, per_worker, chunk))


def _final_kernel(x_ref, y_ref, g_ref, nf_ref, o_ref):
    g = g_ref[...]
    hi = jnp.zeros((x_ref.shape[0], PACK_W), _F32)
    lo = jnp.zeros((x_ref.shape[0], PACK_W), _F32)
    for k in range(TOP_K):
        y_hi, y_lo = _unpack_rows(y_ref[k])
        hi = hi + g[:, k:k + 1] * y_hi
        lo = lo + g[:, k:k + 1] * y_lo
    x = x_ref[...] + jnp.concatenate([hi, lo], axis=1)
    o_ref[...] = x * lax.rsqrt(jnp.mean(x * x, axis=-1, keepdims=True) + EPS) * nf_ref[...]


def _final(xmid, yg, gates_t, norm_f, *, tm, first_block):
    t = xmid.shape[0]
    return pl.pallas_call(
        _final_kernel, grid=(t // tm,),
        in_specs=[pl.BlockSpec((tm, D_MODEL), lambda i: (i, 0)),
                  pl.BlockSpec((TOP_K, tm, PACK_W), lambda i: (0, i + first_block, 0)),
                  pl.BlockSpec((tm, TOP_K), lambda i: (i + first_block, 0)),
                  pl.BlockSpec((1, D_MODEL), lambda i: (0, 0))],
        out_specs=pl.BlockSpec((tm, D_MODEL), lambda i: (i, 0)),
        out_shape=jax.ShapeDtypeStruct((t, D_MODEL), _F32),
        compiler_params=_cparams(("parallel",)), name="final",
    )(xmid, yg, gates_t, norm_f)


def kernel(x_prompt, x_sample, cache_k, cache_v, state_C, state_n, state_m, norm1, w_in, b_igate, b_fgate,
           lambda_q1, lambda_k1, lambda_q2, lambda_k2, subln, w_out, norm2, w_router, b_router, w_gate_up,
           b_gate_up, w_down, b_down, norm_f):
    nb, s, _ = x_prompt.shape
    db, ls, _ = x_sample.shape
    past = cache_k.shape[2]
    depth = w_in.shape[0]
    assert depth == 1 and s % PROJ_TILE == 0 and (db * ls) % 8 == 0
    l = 0
    lam_init = 0.8 - 0.6 * math.exp(-0.3 * l)
    t_p, t_s = nb * s, db * ls

    w = w_in[l]
    gate_cols = jnp.pad(w[:, 7 * GROUP_W:], ((0, 0), (0, HEAD_W - 2 * ML_HEADS)))
    w_nat = jnp.concatenate([w[:, :7 * GROUP_W], gate_cols], axis=1).astype(_BF16)
    b_gates = jnp.pad(jnp.concatenate([b_igate[l], b_fgate[l]]).astype(_F32), (0, HEAD_W - 2 * ML_HEADS))[None]
    inv = ROPE_THETA ** (-jnp.arange(0, DA_QK, 2, dtype=_F32) / DA_QK)
    inv_full = jnp.tile(inv, HEAD_W // (DA_QK // 2))[None]
    lamv = jnp.stack([lambda_q1[l], lambda_k1[l], lambda_q2[l], lambda_k2[l]]).astype(_F32)
    g1 = norm1[l][None]

    pp = _proj(x_prompt, g1, w_nat, b_gates, inv_full, tm=PROJ_TILE, pos_base=0, pos_mod=s,
               names=("q", "kf", "kb", "vf", "vt", "mk", "mo", "gr", "mqt", "mvt"))
    ps = _proj(x_sample.reshape(1, t_s, D_MODEL), g1, w_nat, b_gates, inv_full, tm=t_s, pos_base=past, pos_mod=ls,
               names=("q", "kf", "kb", "vf", "mq", "mk", "mv", "mo", "gc", "gr"))
    kf_p, vf_p, kf_s, vf_s = pp["kf"], pp["vf"], ps["kf"], ps["vf"]

    da_p = _attn_prompt(lamv, pp["q"], pp["kb"], pp["vt"], lam_init=lam_init)
    per_b = lambda a: a.reshape(db, ls, a.shape[-1])
    kct = jnp.transpose(cache_k[l], (0, 2, 3, 4, 1)).reshape(db, GROUP_W, past)
    vc = cache_v[l].reshape(db, past * DA_HEADS, DA_V)
    vn = vf_s.reshape(db, ls, GROUP_W).astype(_BF16)
    da_s = _attn_sample(lamv, per_b(ps["q"]), per_b(ps["kb"]), vn, kct, vc, lam_init=lam_init)

    assert PROJ_TILE % (2 * CHUNK) == 0
    h_p, st_p, m_p = _mlstm_pairs(pp["mk"], pp["mqt"], pp["mvt"], pp["gr"],
                                  jnp.zeros((nb, ML_HEADS, VT_ROWS, ML_DK), _F32),
                                  jnp.zeros((nb, ML_HEADS, 8, HEAD_W), _F32))
    gr_sb = ps["gr"].reshape(8, db, ls).transpose(1, 0, 2)
    m0_s = jnp.broadcast_to(state_m[l].astype(_F32)[:, :, None, None], (db, ML_HEADS, 8, HEAD_W))
    h_s, c_s, n_s, m_s = _mlstm(per_b(ps["mq"]), per_b(ps["mk"]), per_b(ps["mv"]), per_b(ps["gc"]), gr_sb,
                                state_C[l].astype(_F32), state_n[l].astype(_F32), m0_s, chunk=ls, block=ls)
    c_p, n_p, mm_p = jnp.swapaxes(st_p[:, :, :ML_DV, :], -1, -2), st_p[:, :, ML_DV, :], m_p[:, :, 0, 0]
    mm_s = m_s[:, :, 0, 0]

    wo = w_out[l].astype(_BF16)
    w_rt = w_router[l].T.astype(_BF16)
    b_r = b_router[l].astype(_F32)[:, None]
    mix = functools.partial(_mix, subln=subln[l][None], w_out=wo, norm2=norm2[l][None], w_rt=w_rt, b_r=b_r,
                            lam_init=lam_init)
    flat = lambda a: a.reshape(-1, a.shape[-1])
    t_all = t_p + t_s
    assert t_p % t_s == 0
    xm_p, xp_p, idx_p, gate_p, rank_p, cnt_p = mix(
        flat(da_p), flat(h_p), flat(pp["mo"]), flat(x_prompt), cnt0=jnp.zeros((N_EXPERTS, 1), _F32), tm=PROJ_TILE)
    xm_s, xp_s, idx_s, gate_s, rank_s, cnt = mix(
        flat(da_s), flat(h_s), flat(ps["mo"]), flat(x_sample), cnt0=cnt_p, tm=t_s)

    rows = t_all * TOP_K
    n_blocks = -(-rows // MOE_BLOCK) + N_EXPERTS
    counts = cnt[:, 0].astype(jnp.int32)
    padded = (counts + MOE_BLOCK - 1) // MOE_BLOCK * MOE_BLOCK
    pend = jnp.cumsum(padded)
    pstart = pend - counts
    experts = jnp.arange(N_EXPERTS, dtype=jnp.int32)[:, None, None]
    first_row = lambda idx: jnp.sum(jnp.where(idx[None] == experts, pstart[:, None, None], 0), axis=0)
    dest_p, dest_s = first_row(idx_p) + rank_p, first_row(idx_s) + rank_s
    n_used = (pend[-1] // MOE_BLOCK).astype(jnp.int32)[None]
    blk_lo = jnp.arange(n_blocks, dtype=jnp.int32) * MOE_BLOCK
    blk_e = jnp.minimum(jnp.sum((pend[None, :] <= blk_lo[:, None]).astype(jnp.int32), axis=1), N_EXPERTS - 1)
    row_lo = jnp.sum(jnp.where(blk_e[:, None] == experts[:, 0].T, pstart[None, :], 0), axis=1)
    blk_rows = jnp.where(blk_lo < pend[-1], jnp.clip(blk_lo + MOE_BLOCK - row_lo, 0, MOE_BLOCK), 0).astype(jnp.int32)

    def chunked(dest):
        chunk = min(DISPATCH_CHUNK, dest.shape[1] // SC_WORKERS)
        return dest.reshape(TOP_K, dest.shape[1] // chunk, chunk).transpose(1, 0, 2)

    assert rows % (SC_WORKERS * COMBINE_CHUNK) == 0
    xb = _sc_dispatch([(xp_p, chunked(dest_p)), (xp_s, chunked(dest_s))], n_blocks * MOE_BLOCK)
    dest = jnp.concatenate([dest_p, dest_s], axis=1)
    yb = _moe(blk_e, n_used, blk_rows, xb, w_gate_up[l], b_gate_up[l][:, None, :], w_down[l], b_down[l][:, None, :])
    yg = _sc_gather(yb, dest.reshape(rows // COMBINE_CHUNK, COMBINE_CHUNK)).reshape(TOP_K, t_all, PACK_W)
    gates_t = jnp.concatenate([gate_p, gate_s], axis=1).T
    nf = norm_f[None]
    y_p = _final(xm_p, yg, gates_t, nf, tm=FINAL_TILE, first_block=0).reshape(nb, s, D_MODEL)
    y_s = _final(xm_s, yg, gates_t, nf, tm=t_s, first_block=t_p // t_s).reshape(db, ls, D_MODEL)

    st = lambda a: a[None]
    return (y_p, y_s,
            st(kf_p.reshape(nb, s, DA_HEADS, 2, DA_QK)), st(vf_p.reshape(nb, s, DA_HEADS, DA_V)),
            st(c_p), st(n_p), st(mm_p),
            st(kf_s.reshape(db, ls, DA_HEADS, 2, DA_QK)), st(vf_s.reshape(db, ls, DA_HEADS, DA_V)),
            st(c_s.astype(state_C.dtype)), st(n_s.astype(state_n.dtype)), st(mm_s.astype(state_m.dtype)))
```

```python
import functools
import math

import jax
import jax.numpy as jnp
from jax import lax
from jax.experimental import pallas as pl
from jax.experimental.pallas import tpu as pltpu
from jax.experimental.pallas import tpu_sc as plsc

D_MODEL = 1024
CHUNK = 64
DA_HEADS = 4
DA_QK = 64
DA_V = 128
ML_HEADS = 4
ML_DK = 128
ML_DV = 128
HEAD_W = 128
GROUP_W = 512
ROPE_THETA = 10000.0
N_EXPERTS = 32
TOP_K = 4
D_FF = 1024
SWIGLU_LIMIT = 7.0
SWIGLU_ALPHA = 1.702
EPS = 1e-6
NEG_BIG = -1e30

LOG2E = math.log2(math.e)
VT_ROWS = 144
PROJ_TILE = 512
FINAL_TILE = 1024
ATTN_TQ = 512
MOE_BLOCK = 512
FF_CHUNK = 512
PACK_W = D_MODEL // 2
SC_CORES = 2
SC_SUBCORES = 16
SC_WORKERS = SC_CORES * SC_SUBCORES
DISPATCH_CHUNK = 64
COMBINE_CHUNK = 80
VMEM_LIMIT = 56 * 1024 * 1024

_F32 = jnp.float32
_BF16 = jnp.bfloat16


def _cparams(sem):
    return pltpu.CompilerParams(dimension_semantics=sem, vmem_limit_bytes=VMEM_LIMIT)


def _dot(a, b, dims=(((1,), (0,)), ((), ())), precision=None):
    return lax.dot_general(a, b, dims, precision=precision, preferred_element_type=_F32)


_NT = (((1,), (1,)), ((), ()))
_TN = (((0,), (0,)), ((), ()))


def _pack_rows(x):
    half = x.shape[1] // 2
    bits = lambda v: lax.bitcast_convert_type(v.astype(_BF16).astype(_F32), jnp.int32)
    return bits(x[:, :half]) | lax.shift_right_logical(bits(x[:, half:]), 16)


def _unpack_rows(p):
    hi = lax.bitcast_convert_type(p & jnp.int32(-65536), _F32)
    lo = lax.bitcast_convert_type(lax.shift_left(p, 16), _F32)
    return hi, lo


def _proj_kernel(x_ref, g_ref, wt_ref, bg_ref, inv_ref, *out_refs, names, tm, pos_base, pos_mod):
    out = dict(zip(names, out_refs))
    x = x_ref[0]
    xn = (x * lax.rsqrt(jnp.mean(x * x, axis=-1, keepdims=True) + EPS) * g_ref[...]).astype(_BF16)

    row = pl.program_id(1) * tm + lax.broadcasted_iota(jnp.int32, (tm, 1), 0)
    pos = (pos_base + row % pos_mod).astype(_F32)
    ang = pos * inv_ref[...]
    cos = jnp.cos(ang)
    sin = jnp.sin(ang)
    lane = lax.broadcasted_iota(jnp.int32, (1, HEAD_W), 1)
    first = (lane % DA_QK) < (DA_QK // 2)
    sin = jnp.where(first, -sin, sin)

    def rope(z):
        rot = jnp.where(first, pltpu.roll(z, HEAD_W - DA_QK // 2, 1), pltpu.roll(z, DA_QK // 2, 1))
        return z * cos + rot * sin

    def group(c, width=GROUP_W):
        return _dot(xn, wt_ref[c * GROUP_W:c * GROUP_W + width, :], _NT)

    def put_q(zq):
        for h in range(DA_HEADS):
            sl = slice(h * HEAD_W, (h + 1) * HEAD_W)
            out["q"][0, :, sl] = (rope(zq[:, sl]) * (DA_QK ** -0.5 * LOG2E)).astype(_BF16)

    def put_k(zk):
        for h in range(DA_HEADS):
            sl = slice(h * HEAD_W, (h + 1) * HEAD_W)
            rk = rope(zk[:, sl])
            out["kf"][0, :, sl] = rk
            out["kb"][0, :, sl] = rk.astype(_BF16)

    def with_ones_rows(zt):
        ones = jnp.ones((VT_ROWS - DA_V, tm), _BF16)
        return jnp.concatenate(
            [part for h in range(DA_HEADS) for part in (zt[h * DA_V:(h + 1) * DA_V], ones)], axis=0)

    def put_v(zv):
        for h in range(DA_HEADS):
            out["vf"][0, pl.ds(h, tm, stride=DA_HEADS), :] = zv[:, h * DA_V:(h + 1) * DA_V]
        if "vt" in out:
            out["vt"][0, 0] = with_ones_rows(zv.T.astype(_BF16))

    def put_mq(z):
        if "mq" in out:
            out["mq"][0] = z.astype(_BF16)
        if "mqt" in out:
            out["mqt"][0, 0] = z.T.astype(_BF16)

    def put_mk(z):
        out["mk"][0] = (z * (ML_DK ** -0.5)).astype(_BF16)

    def put_mv(z):
        if "mv" in out:
            out["mv"][0] = z.astype(_BF16)
        if "mvt" in out:
            out["mvt"][0, 0] = with_ones_rows(z.T.astype(_BF16))

    def put_mo(z):
        out["mo"][0] = z

    def put_gates(z):
        zg = z + bg_ref[...]
        lane_g = lax.broadcasted_iota(jnp.int32, zg.shape, 1)
        logsig = jnp.minimum(zg, 0.0) - jnp.log(1.0 + jnp.exp(-jnp.abs(zg)))
        gates = jnp.where(lane_g < ML_HEADS, zg, logsig)
        if "gc" in out:
            out["gc"][0] = gates[:, :8]
        out["gr"][0] = gates.T[:8, :]

    sinks = [(put_q, GROUP_W), (put_k, GROUP_W), (put_v, GROUP_W), (put_mq, GROUP_W), (put_mk, GROUP_W),
             (put_mv, GROUP_W), (put_mo, GROUP_W), (put_gates, HEAD_W)]
    nxt = group(0, sinks[0][1])
    for c, (sink, _) in enumerate(sinks):
        cur = nxt
        if c + 1 < len(sinks):
            nxt = group(c + 1, sinks[c + 1][1])
        sink(cur)


def _proj(x3, norm1, wt, b_gates, inv_full, *, names, tm, pos_base, pos_mod):
    nb, s, _ = x3.shape
    nt = s // tm
    tok = lambda w, dt: (jax.ShapeDtypeStruct((nb, s, w), dt), pl.BlockSpec((1, tm, w), lambda b, i: (b, i, 0)))
    slab = lambda rows: (jax.ShapeDtypeStruct((nb, nt, rows, tm), _BF16),
                         pl.BlockSpec((1, 1, rows, tm), lambda b, i: (b, i, 0, 0)))
    const = lambda shape: pl.BlockSpec(shape, lambda b, i: (0,) * len(shape))
    kinds = {
        "q": tok(GROUP_W, _BF16), "kf": tok(GROUP_W, _F32), "kb": tok(GROUP_W, _BF16),
        "vf": (jax.ShapeDtypeStruct((nb, s * DA_HEADS, DA_V), _F32),
               pl.BlockSpec((1, tm * DA_HEADS, DA_V), lambda b, i: (b, i, 0))),
        "vt": slab(DA_HEADS * VT_ROWS),
        "mq": tok(GROUP_W, _BF16), "mk": tok(GROUP_W, _BF16), "mv": tok(GROUP_W, _BF16), "mo": tok(GROUP_W, _F32),
        "gc": tok(8, _F32),
        "gr": (jax.ShapeDtypeStruct((nb, 8, s), _F32), pl.BlockSpec((1, 8, tm), lambda b, i: (b, 0, i))),
        "mqt": slab(GROUP_W), "mvt": slab(ML_HEADS * VT_ROWS),
    }
    outs = pl.pallas_call(
        functools.partial(_proj_kernel, names=names, tm=tm, pos_base=pos_base, pos_mod=pos_mod),
        grid=(nb, nt),
        in_specs=[pl.BlockSpec((1, tm, D_MODEL), lambda b, i: (b, i, 0)), const((1, D_MODEL)),
                  const(wt.shape), const((1, HEAD_W)), const((1, HEAD_W))],
        out_specs=[kinds[n][1] for n in names], out_shape=[kinds[n][0] for n in names],
        compiler_params=_cparams(("parallel", "parallel")), name="proj",
    )(x3, norm1, wt, b_gates, inv_full)
    return dict(zip(names, outs))


def _lambda(lam_ref, lam_init):
    lv = lam_ref[...]
    s1 = jnp.sum(lv[0:1] * lv[1:2], axis=-1, keepdims=True)
    s2 = jnp.sum(lv[2:3] * lv[3:4], axis=-1, keepdims=True)
    return jnp.exp(s1) - jnp.exp(s2) + lam_init


def _split_components(q):
    lane = lax.broadcasted_iota(jnp.int32, q.shape, 1)
    zero = jnp.zeros_like(q)
    return jnp.concatenate([jnp.where(lane < DA_QK, q, zero), jnp.where(lane >= DA_QK, q, zero)], axis=0)


def _attn_kernel(lam_ref, q_ref, k_ref, vt_ref, o_ref, acc_ref, s_ref, *, tq, tk, lam_init):
    i = pl.program_id(1)
    heads = range(DA_HEADS)
    qz = [_split_components(q_ref[0, :, h * HEAD_W:(h + 1) * HEAD_W]) for h in heads]
    acc_ref[...] = jnp.zeros_like(acc_ref)

    def scores(h, j):
        k_t = k_ref[0, pl.ds(pl.multiple_of(j * tk, tk), tk), h * HEAD_W:(h + 1) * HEAD_W]
        return _dot(k_t, qz[h], _NT)

    def step(j, ms, last):
        if last:
            kpos = j * tk + lax.broadcasted_iota(jnp.int32, (tk, 1), 0)
            qpos = i * tq + lax.broadcasted_iota(jnp.int32, (1, 2 * tq), 1) % tq
            visible = kpos < (qpos // CHUNK + 1) * CHUNK
        out = []
        s = s_ref[...]
        for h in heads:
            s_next = None
            if h + 1 < DA_HEADS:
                s_next = scores(h + 1, j)
            elif not last:
                s_next = scores(0, j + 1)
            if last:
                s = jnp.where(visible, s, NEG_BIG)
            m_new = jnp.maximum(ms[h], jnp.max(s, axis=0, keepdims=True))
            alpha = jnp.exp2(ms[h] - m_new)
            p = jnp.exp2(s - m_new).astype(_BF16)
            acc_ref[h] = alpha * acc_ref[h] + _dot(vt_ref[0, j, h * VT_ROWS:(h + 1) * VT_ROWS, :], p)
            out.append(m_new)
            s = s_next
        if not last:
            s_ref[...] = s
        return tuple(out)

    n_full = (i * tq) // tk
    s_ref[...] = scores(0, 0)
    init = tuple(jnp.full((1, 2 * tq), NEG_BIG, _F32) for _ in heads)
    ms = lax.fori_loop(0, n_full, lambda j, c: step(j, c, False), init)
    step(n_full, ms, True)

    lam = _lambda(lam_ref, lam_init)
    for h in heads:
        acc = acc_ref[h]
        o = acc[:DA_V] / acc[DA_V:DA_V + 1]
        o_ref[0, :, h * HEAD_W:(h + 1) * HEAD_W] = (o[:, :tq] - lam * o[:, tq:]).T


def _attn_prompt(lamv, q, kb, vt, *, lam_init):
    nb, s, _ = q.shape
    nk, tk = vt.shape[1], vt.shape[3]
    tq = ATTN_TQ
    assert tk % tq == 0
    return pl.pallas_call(
        functools.partial(_attn_kernel, tq=tq, tk=tk, lam_init=lam_init),
        grid=(nb, s // tq),
        in_specs=[pl.BlockSpec((4, DA_QK), lambda b, i: (0, 0)),
                  pl.BlockSpec((1, tq, GROUP_W), lambda b, i: (b, i, 0)),
                  pl.BlockSpec((1, s, GROUP_W), lambda b, i: (b, 0, 0)),
                  pl.BlockSpec((1, nk, DA_HEADS * VT_ROWS, tk), lambda b, i: (b, 0, 0, 0))],
        out_specs=pl.BlockSpec((1, tq, GROUP_W), lambda b, i: (b, i, 0)),
        out_shape=jax.ShapeDtypeStruct((nb, s, GROUP_W), _F32),
        scratch_shapes=[pltpu.VMEM((DA_HEADS, VT_ROWS, 2 * tq), _F32), pltpu.VMEM((tk, 2 * tq), _F32)],
        compiler_params=_cparams(("parallel", "arbitrary")), name="attn",
    )(lamv, q, kb, vt)


def _attn_sample_kernel(lam_ref, q_ref, kn_ref, vn_ref, kc_ref, vc_ref, o_ref, *, lq, past, lam_init):
    lam = _lambda(lam_ref, lam_init)
    for h in range(DA_HEADS):
        sl = slice(h * HEAD_W, (h + 1) * HEAD_W)
        qz = _split_components(q_ref[0, :, sl])
        kct = kc_ref[0, sl, :].astype(_BF16)
        vc = vc_ref[0, pl.ds(h, past, stride=DA_HEADS), :].astype(_BF16)
        s_c = _dot(qz, kct)
        s_n = _dot(qz, kn_ref[0, :, sl], _NT)
        m = jnp.maximum(jnp.max(s_c, axis=-1, keepdims=True), jnp.max(s_n, axis=-1, keepdims=True))
        p_c = jnp.exp2(s_c - m)
        p_n = jnp.exp2(s_n - m)
        l = jnp.sum(p_c, axis=-1, keepdims=True) + jnp.sum(p_n, axis=-1, keepdims=True)
        o = (_dot(p_c.astype(_BF16), vc) + _dot(p_n.astype(_BF16), vn_ref[0, :, sl])) / l
        o_ref[0, :, sl] = o[:lq] - lam * o[lq:]


def _attn_sample(lamv, q, kn, vn, kct, vc, *, lam_init):
    nb, lq, _ = q.shape
    past = kct.shape[2]
    new = pl.BlockSpec((1, lq, GROUP_W), lambda b: (b, 0, 0))
    return pl.pallas_call(
        functools.partial(_attn_sample_kernel, lq=lq, past=past, lam_init=lam_init),
        grid=(nb,),
        in_specs=[pl.BlockSpec((4, DA_QK), lambda b: (0, 0)), new, new, new,
                  pl.BlockSpec((1, GROUP_W, past), lambda b: (b, 0, 0)),
                  pl.BlockSpec((1, past * DA_HEADS, DA_V), lambda b: (b, 0, 0))],
        out_specs=new, out_shape=jax.ShapeDtypeStruct((nb, lq, GROUP_W), _F32),
        compiler_params=_cparams(("parallel",)), name="attn_sample",
    )(lamv, q, kn, vn, kct, vc)


def _mlstm_kernel(q_ref, k_ref, v_ref, gc_ref, gr_ref, c0_ref, n0_ref, m0_ref, h_ref, c_ref, n_ref, m_ref,
                  c_sc, n_sc, m_sc, *, chunk, nchunks):
    j = pl.program_id(1)

    @pl.when(j == 0)
    def _():
        c_sc[...] = c0_ref[0]
        n_sc[...] = n0_ref[0]
        m_sc[...] = m0_ref[0]

    L = chunk
    r_i = lax.broadcasted_iota(jnp.int32, (L, L), 0)
    c_i = lax.broadcasted_iota(jnp.int32, (L, L), 1)
    causal = r_i >= c_i
    tril = causal.astype(_F32)
    triu = (r_i <= c_i).astype(_F32)
    hi = lax.Precision.HIGHEST

    for c in range(nchunks):
        rows = slice(c * L, (c + 1) * L)
        gcol = gc_ref[0, rows, :]
        grow = gr_ref[0, :, rows]
        fc_all = _dot(tril, gcol, precision=hi)
        fr_all = _dot(grow, triu, precision=hi)
        for hh in range(ML_HEADS):
            sl = slice(hh * HEAD_W, (hh + 1) * HEAD_W)
            m = m_sc[hh][0:1, 0:1]
            fc = fc_all[:, ML_HEADS + hh:ML_HEADS + hh + 1]
            fr = fr_all[ML_HEADS + hh:ML_HEADS + hh + 1, :]
            igc = gcol[:, hh:hh + 1]
            igr = grow[hh:hh + 1, :]
            a = jnp.where(causal, fc - fr + igr, -jnp.inf)
            b = fc + m
            m_t = jnp.maximum(b, jnp.max(a, axis=-1, keepdims=True))
            w = jnp.exp(a - m_t)
            sc = jnp.exp(b - m_t)
            q = q_ref[0, rows, sl]
            k = k_ref[0, rows, sl]
            v = v_ref[0, rows, sl]
            wqk = w * _dot(q, k, _NT)
            cmat = c_sc[hh]
            nrow = n_sc[hh:hh + 1, :]
            qn = jnp.sum(q.astype(_F32) * nrow, axis=-1, keepdims=True)
            num = _dot(wqk.astype(_BF16), v) + sc * _dot(q, cmat.astype(_BF16))
            den = jnp.sum(wqk, axis=-1, keepdims=True) + sc * qn
            h_ref[0, rows, sl] = num / jnp.maximum(jnp.abs(den), jnp.exp(-m_t))

            fl = fc[L - 1:L, :]
            g = fl - fc + igc
            m_new = jnp.maximum(fl + m, jnp.max(g, axis=0, keepdims=True))
            decay = jnp.exp(fl + m - m_new)
            kw = k.astype(_F32) * jnp.exp(g - m_new)
            c_sc[hh] = decay * cmat + _dot(kw.astype(_BF16), v, _TN)
            n_sc[hh:hh + 1, :] = decay * nrow + jnp.sum(kw, axis=0, keepdims=True)
            m_sc[hh] = jnp.broadcast_to(m_new, m_sc.shape[1:])

    @pl.when(j == pl.num_programs(1) - 1)
    def _():
        c_ref[0] = c_sc[...]
        n_ref[0] = n_sc[...]
        m_ref[0] = m_sc[...]


def _mlstm(q, k, v, gc, gr, c0, n0, m0, *, chunk, block):
    nb, s, _ = q.shape
    tok = lambda w: pl.BlockSpec((1, block, w), lambda b, j: (b, j, 0))
    cspec = pl.BlockSpec((1, ML_HEADS, ML_DK, ML_DV), lambda b, j: (b, 0, 0, 0))
    nspec = pl.BlockSpec((1, ML_HEADS, ML_DK), lambda b, j: (b, 0, 0))
    mspec = pl.BlockSpec((1, ML_HEADS, 8, HEAD_W), lambda b, j: (b, 0, 0, 0))
    return pl.pallas_call(
        functools.partial(_mlstm_kernel, chunk=chunk, nchunks=block // chunk),
        grid=(nb, s // block),
        in_specs=[tok(GROUP_W), tok(GROUP_W), tok(GROUP_W), tok(8),
                  pl.BlockSpec((1, 8, block), lambda b, j: (b, 0, j)), cspec, nspec, mspec],
        out_specs=[tok(GROUP_W), cspec, nspec, mspec],
        out_shape=[jax.ShapeDtypeStruct((nb, s, GROUP_W), _F32), jax.ShapeDtypeStruct(c0.shape, _F32),
                   jax.ShapeDtypeStruct(n0.shape, _F32), jax.ShapeDtypeStruct(m0.shape, _F32)],
        scratch_shapes=[pltpu.VMEM((ML_HEADS, ML_DK, ML_DV), _F32), pltpu.VMEM((ML_HEADS, ML_DK), _F32),
                        pltpu.VMEM((ML_HEADS, 8, HEAD_W), _F32)],
        compiler_params=_cparams(("parallel", "arbitrary")), name="mlstm",
    )(q, k, v, gc, gr, c0, n0, m0)


def _mlstm_pairs_kernel(k_ref, qt_ref, vt_ref, gr_ref, st0_ref, m0_ref, h_ref, st_ref, m_ref, st_sc, m_sc,
                        *, block):
    j = pl.program_id(1)

    @pl.when(j == 0)
    def _():
        st_sc[...] = st0_ref[0]
        m_sc[...] = m0_ref[0]

    L, W = CHUNK, 2 * CHUNK
    g8 = gr_ref[0]
    pos = lax.broadcasted_iota(jnp.int32, (1, block), 1) % L
    roll = lambda x, sh: pltpu.roll(x, sh % block, 1)
    steps = [1 << b for b in range(L.bit_length() - 1)]

    def scan(x, combine, fill, reverse=False):
        for sh in steps:
            ok = (pos < L - sh) if reverse else (pos >= sh)
            x = combine(x, jnp.where(ok, roll(x, -sh if reverse else sh), fill))
        return x

    to_head_rows = lambda x: pltpu.roll(x, ML_HEADS, 0)
    f8 = to_head_rows(scan(g8, jnp.add, 0.0))
    d8 = g8 - f8
    cm8 = scan(d8, jnp.maximum, -jnp.inf)
    big_g = to_head_rows(scan(g8, jnp.add, 0.0, reverse=True) - g8) + g8
    mg8 = jnp.maximum(scan(big_g, jnp.maximum, -jnp.inf), scan(big_g, jnp.maximum, -jnp.inf, reverse=True))
    ws8 = jnp.exp(big_g - mg8)
    d_cols = d8.T

    s_i = lax.broadcasted_iota(jnp.int32, (W, W), 0)
    t_i = lax.broadcasted_iota(jnp.int32, (W, W), 1)
    allowed = (s_i // L == t_i // L) & (s_i <= t_i)
    lane_w = lax.broadcasted_iota(jnp.int32, (1, W), 1)
    first = lane_w < L

    sts = [st_sc[hh] for hh in range(ML_HEADS)]
    ms = [m_sc[hh][0:1, 0:1] for hh in range(ML_HEADS)]

    def independent(p, hh):
        lanes = slice(p * W, (p + 1) * W)
        feat = slice(hh * HEAD_W, (hh + 1) * HEAD_W)
        kp = k_ref[0, lanes, feat]
        qt = qt_ref[0, 0, feat, lanes]
        v1t = vt_ref[0, 0, hh * VT_ROWS:(hh + 1) * VT_ROWS, lanes]
        ws = ws8[hh:hh + 1, lanes]
        v1f = v1t.astype(_F32)
        return dict(
            p=p, hh=hh, qt=qt, v1t=v1t, qk_t=_dot(kp, qt),
            upd_a=_dot((v1f * jnp.where(first, ws, 0.0)).astype(_BF16), kp),
            upd_b=_dot((v1f * jnp.where(first, 0.0, ws)).astype(_BF16), kp))

    def dependent(u):
        p, hh, qt = u["p"], u["hh"], u["qt"]
        lanes = slice(p * W, (p + 1) * W)
        row = lambda x: x[hh:hh + 1, lanes]
        cm, f_row = row(cm8), row(f8)
        w_t = jnp.exp(jnp.where(allowed, d_cols[lanes, hh:hh + 1] - cm, -jnp.inf))
        intra = _dot(u["v1t"], (w_t * u["qk_t"]).astype(_BF16))
        st_a, m_a = sts[hh], ms[hh]
        fl_a, mg_a = f_row[:, L - 1:L], row(mg8)[:, 0:1]
        fl_b, mg_b = f_row[:, W - 1:W], row(mg8)[:, L:L + 1]
        m_b = jnp.maximum(fl_a + m_a, mg_a)
        st_b = jnp.exp(fl_a + m_a - m_b) * st_a + jnp.exp(mg_a - m_b) * u["upd_a"]
        m_c = jnp.maximum(fl_b + m_b, mg_b)
        sts[hh] = jnp.exp(fl_b + m_b - m_c) * st_b + jnp.exp(mg_b - m_c) * u["upd_b"]
        ms[hh] = m_c
        inter = jnp.where(first, _dot(st_a.astype(_BF16), qt), _dot(st_b.astype(_BF16), qt))
        m_prev = jnp.where(first, m_a, m_b)
        mt = jnp.maximum(m_prev, cm)
        tot = jnp.exp(cm - mt) * intra + jnp.exp(m_prev - mt) * inter
        den = jnp.maximum(jnp.abs(tot[ML_DV:ML_DV + 1]), jnp.exp(-(f_row + mt)))
        h_ref[0, lanes, hh * HEAD_W:(hh + 1) * HEAD_W] = (tot[:ML_DV] / den).T

    units = [(p, hh) for p in range(block // W) for hh in range(ML_HEADS)]
    nxt = independent(*units[0])
    for n in range(len(units)):
        cur = nxt
        if n + 1 < len(units):
            nxt = independent(*units[n + 1])
        dependent(cur)
    for hh in range(ML_HEADS):
        st_sc[hh] = sts[hh]
        m_sc[hh] = jnp.broadcast_to(ms[hh], m_sc.shape[1:])

    @pl.when(j == pl.num_programs(1) - 1)
    def _():
        st_ref[0] = st_sc[...]
        m_ref[0] = m_sc[...]


def _mlstm_pairs(k, qt, vt, gr, st0, m0):
    nb, s, _ = k.shape
    block = qt.shape[3]
    nt = s // block
    state = pl.BlockSpec((1, ML_HEADS, VT_ROWS, ML_DK), lambda b, j: (b, 0, 0, 0))
    mspec = pl.BlockSpec((1, ML_HEADS, 8, HEAD_W), lambda b, j: (b, 0, 0, 0))
    tok = pl.BlockSpec((1, block, GROUP_W), lambda b, j: (b, j, 0))
    return pl.pallas_call(
        functools.partial(_mlstm_pairs_kernel, block=block),
        grid=(nb, nt),
        in_specs=[tok, pl.BlockSpec((1, 1, GROUP_W, block), lambda b, j: (b, j, 0, 0)),
                  pl.BlockSpec((1, 1, ML_HEADS * VT_ROWS, block), lambda b, j: (b, j, 0, 0)),
                  pl.BlockSpec((1, 8, block), lambda b, j: (b, 0, j)), state, mspec],
        out_specs=[tok, state, mspec],
        out_shape=[jax.ShapeDtypeStruct((nb, s, GROUP_W), _F32), jax.ShapeDtypeStruct(st0.shape, _F32),
                   jax.ShapeDtypeStruct(m0.shape, _F32)],
        scratch_shapes=[pltpu.VMEM((ML_HEADS, VT_ROWS, ML_DK), _F32), pltpu.VMEM((ML_HEADS, 8, HEAD_W), _F32)],
        compiler_params=_cparams(("parallel", "arbitrary")), name="mlstm_pairs",
    )(k, qt, vt, gr, st0, m0)


def _mix_kernel(da_ref, ml_ref, mo_ref, x_ref, subln_ref, wo_ref, g2_ref, wr_ref, br_ref, cnt0_ref,
                xmid_ref, xp_ref, idx_ref, gate_ref, rank_ref, cnt_ref, cnt_sc, *, tm, lam_init):
    step = pl.program_id(0)

    @pl.when(step == 0)
    def _():
        cnt_sc[...] = cnt0_ref[...]

    parts = []
    for h in range(DA_HEADS):
        d = da_ref[:, h * HEAD_W:(h + 1) * HEAD_W]
        d = d * lax.rsqrt(jnp.mean(d * d, axis=-1, keepdims=True) + EPS) * subln_ref[...]
        parts.append((d * (1.0 - lam_init)).astype(_BF16))
    parts.append((jax.nn.sigmoid(mo_ref[...]) * ml_ref[...]).astype(_BF16))
    xm = x_ref[...] + _dot(jnp.concatenate(parts, axis=1), wo_ref[...])
    xmid_ref[...] = xm
    xn_f = xm * lax.rsqrt(jnp.mean(xm * xm, axis=-1, keepdims=True) + EPS) * g2_ref[...]
    xp_ref[...] = _pack_rows(xn_f)
    xn = xn_f.astype(_BF16)

    logits = _dot(wr_ref[...], xn, _NT) + br_ref[...]
    e_i = lax.broadcasted_iota(jnp.int32, logits.shape, 0)
    member = jnp.zeros(logits.shape, jnp.bool_)
    vals, idxs = [], []
    for _ in range(TOP_K):
        mx = jnp.max(logits, axis=0, keepdims=True)
        sel = jnp.min(jnp.where(logits == mx, e_i, N_EXPERTS), axis=0, keepdims=True)
        hit = e_i == sel
        member = member | hit
        logits = jnp.where(hit, -jnp.inf, logits)
        vals.append(mx)
        idxs.append(sel)
    ex = [jnp.exp(v - vals[0]) for v in vals]
    tot = ex[0] + ex[1] + ex[2] + ex[3]
    idx_ref[...] = jnp.concatenate(idxs, axis=0)
    gate_ref[...] = jnp.concatenate([e / tot for e in ex], axis=0)

    upper = (lax.broadcasted_iota(jnp.int32, (tm, tm), 0) < lax.broadcasted_iota(jnp.int32, (tm, tm), 1))
    memf = member.astype(_F32)
    before = _dot(memf.astype(_BF16), upper.astype(_BF16)) + cnt_sc[...]
    rank_ref[...] = jnp.concatenate(
        [jnp.sum(jnp.where(e_i == s, before, 0.0), axis=0, keepdims=True) for s in idxs], axis=0).astype(jnp.int32)
    cnt_sc[...] += jnp.sum(memf, axis=1, keepdims=True)
    cnt_ref[...] = cnt_sc[...]


def _mix(da, ml, mo, x, subln, w_out, norm2, w_rt, b_r, cnt0, *, tm, lam_init):
    t = x.shape[0]
    tok = lambda w: pl.BlockSpec((tm, w), lambda i: (i, 0))
    const = lambda shape: pl.BlockSpec(shape, lambda i: (0,) * len(shape))
    lane_tok = pl.BlockSpec((TOP_K, tm), lambda i: (0, i))
    return pl.pallas_call(
        functools.partial(_mix_kernel, tm=tm, lam_init=lam_init),
        grid=(t // tm,),
        in_specs=[tok(GROUP_W), tok(GROUP_W), tok(GROUP_W), tok(D_MODEL), const((1, DA_V)),
                  const((D_MODEL, D_MODEL)), const((1, D_MODEL)), const((N_EXPERTS, D_MODEL)),
                  const((N_EXPERTS, 1)), const((N_EXPERTS, 1))],
        out_specs=[tok(D_MODEL), tok(PACK_W), lane_tok, lane_tok, lane_tok, const((N_EXPERTS, 1))],
        out_shape=[jax.ShapeDtypeStruct((t, D_MODEL), _F32), jax.ShapeDtypeStruct((t, PACK_W), jnp.int32),
                   jax.ShapeDtypeStruct((TOP_K, t), jnp.int32), jax.ShapeDtypeStruct((TOP_K, t), _F32),
                   jax.ShapeDtypeStruct((TOP_K, t), jnp.int32), jax.ShapeDtypeStruct((N_EXPERTS, 1), _F32)],
        scratch_shapes=[pltpu.VMEM((N_EXPERTS, 1), _F32)],
        compiler_params=_cparams(("arbitrary",)), name="mix",
    )(da, ml, mo, x, subln, w_out, norm2, w_rt, b_r, cnt0)


def _moe_kernel(be_ref, nu_ref, nv_ref, x_ref, wgu_ref, bgu_ref, wd_ref, bd_ref, y_ref, wgu_sc, wd_sc):
    j = pl.program_id(0)
    e_prev = be_ref[jnp.maximum(j - 1, 0)]
    fresh = (j == 0) | (be_ref[j] != e_prev)

    @pl.when(fresh & (j < nu_ref[0]))
    def _():
        rows = 128
        def cast_gu(r, _):
            sl = pl.ds(pl.multiple_of(r * rows, rows), rows)
            wgu_sc[sl, :] = wgu_ref[0, sl, :].astype(_BF16)
            return 0
        lax.fori_loop(0, D_MODEL // rows, cast_gu, 0)
        def cast_d(r, _):
            sl = pl.ds(pl.multiple_of(r * rows, rows), rows)
            wd_sc[sl, :] = wd_ref[0, sl, :].astype(_BF16)
            return 0
        lax.fori_loop(0, D_FF // rows, cast_d, 0)

    @pl.when(j >= nu_ref[0])
    def _():
        y_ref[...] = jnp.zeros_like(y_ref)

    def ffn(row0):
        hi, lo = _unpack_rows(x_ref[row0:, :])
        x = jnp.concatenate([hi.astype(_BF16), lo.astype(_BF16)], axis=1)
        n_chunks = D_FF // FF_CHUNK

        def gate_up(c):
            lo = c * FF_CHUNK
            return (_dot(x, wgu_sc[:, lo:lo + FF_CHUNK]), _dot(x, wgu_sc[:, D_FF + lo:D_FF + lo + FF_CHUNK]))

        acc = None
        nxt = gate_up(0)
        for c in range(n_chunks):
            lo = c * FF_CHUNK
            gate, up = nxt
            if c + 1 < n_chunks:
                nxt = gate_up(c + 1)
            gate = jnp.minimum(gate + bgu_ref[0, :, lo:lo + FF_CHUNK], SWIGLU_LIMIT)
            up = jnp.clip(up + bgu_ref[0, :, D_FF + lo:D_FF + lo + FF_CHUNK], -SWIGLU_LIMIT, SWIGLU_LIMIT)
            act = (up + 1.0) * (gate * jax.nn.sigmoid(SWIGLU_ALPHA * gate))
            down = _dot(act.astype(_BF16), wd_sc[lo:lo + FF_CHUNK, :])
            acc = down if acc is None else acc + down
        y_ref[row0:, :] = _pack_rows(acc + bd_ref[0])

    half = MOE_BLOCK // 2
    used = j < nu_ref[0]

    @pl.when(used & (nv_ref[j] > half))
    def _():
        ffn(0)

    @pl.when(used & (nv_ref[j] <= half))
    def _():
        y_ref[:half, :] = jnp.zeros((half, PACK_W), jnp.int32)
        ffn(half)


def _moe(blk_e, n_used, blk_rows, xb, w_gu, b_gu, w_d, b_d):
    n_blocks = xb.shape[0] // MOE_BLOCK
    last = lambda j, nu: jnp.minimum(j, nu[0] - 1)
    row_spec = pl.BlockSpec((MOE_BLOCK, PACK_W), lambda j, be, nu, nv: (last(j, nu), 0))
    exp_spec = lambda r, c: pl.BlockSpec((1, r, c), lambda j, be, nu, nv: (be[last(j, nu)], 0, 0))
    grid_spec = pltpu.PrefetchScalarGridSpec(
        num_scalar_prefetch=3, grid=(n_blocks,),
        in_specs=[row_spec, exp_spec(D_MODEL, 2 * D_FF), exp_spec(1, 2 * D_FF), exp_spec(D_FF, D_MODEL),
                  exp_spec(1, D_MODEL)],
        out_specs=pl.BlockSpec((MOE_BLOCK, PACK_W), lambda j, be, nu, nv: (j, 0)),
        scratch_shapes=[pltpu.VMEM((D_MODEL, 2 * D_FF), _BF16), pltpu.VMEM((D_FF, D_MODEL), _BF16)])
    return pl.pallas_call(
        _moe_kernel, grid_spec=grid_spec, out_shape=jax.ShapeDtypeStruct(xb.shape, jnp.int32),
        compiler_params=_cparams(("arbitrary",)), name="moe",
    )(blk_e, n_used, blk_rows, xb, w_gu, b_gu, w_d, b_d)


def _sc_mesh():
    return plsc.VectorSubcoreMesh(core_axis_name="c", subcore_axis_name="s")


def _sc_worker():
    return lax.axis_index("s") * SC_CORES + lax.axis_index("c")


def _sc_dispatch(sources, n_rows):
    n_src = len(sources)
    shapes = []
    for _, dest3 in sources:
        n_chunks, _, chunk = dest3.shape
        per_worker = n_chunks // SC_WORKERS
        assert per_worker * SC_WORKERS == n_chunks and chunk % 8 == 0
        shapes.append((per_worker, chunk))

    def body(*refs):
        x_hbms, d_hbms, o_hbm = refs[:n_src], refs[n_src:2 * n_src], refs[2 * n_src]
        scratch = refs[2 * n_src + 1:]
        load_a, load_b, scat_a, scat_b = scratch[3 * n_src:]
        worker = _sc_worker()
        for i, (per_worker, chunk) in enumerate(shapes):
            x_hbm, idx_v, rows_a, rows_b = x_hbms[i], scratch[3 * i], scratch[3 * i + 1], scratch[3 * i + 2]
            pltpu.sync_copy(d_hbms[i].at[worker], idx_v)

            def load(ci, rows, sem):
                row0 = (worker * per_worker + ci) * chunk
                return pltpu.make_async_copy(x_hbm.at[pl.ds(pl.multiple_of(row0, 8), chunk)], rows, sem)

            def scatters(ci, rows, sem):
                return [pltpu.make_async_copy(rows, o_hbm.at[idx_v.at[ci * TOP_K + k]], sem) for k in range(TOP_K)]

            def start(copies):
                for cp in copies:
                    cp.start()

            def wait(copies):
                for cp in copies:
                    cp.wait()

            load(0, rows_a, load_a).start()

            @pl.loop(0, per_worker // 2)
            def _(p):
                ca, cb = 2 * p, 2 * p + 1
                load(ca, rows_a, load_a).wait()
                load(cb, rows_b, load_b).start()
                start(scatters(ca, rows_a, scat_a))
                load(cb, rows_b, load_b).wait()
                wait(scatters(ca, rows_a, scat_a))
                load(jnp.minimum(ca + 2, per_worker - 1), rows_a, load_a).start()
                start(scatters(cb, rows_b, scat_b))
                wait(scatters(cb, rows_b, scat_b))

            last = per_worker - 1
            load(last, rows_a, load_a).wait()
            if per_worker % 2:
                start(scatters(last, rows_a, scat_a))
                wait(scatters(last, rows_a, scat_a))

    dma = pltpu.SemaphoreType.DMA
    scratch_types = []
    for per_worker, chunk in shapes:
        scratch_types += [pltpu.VMEM((per_worker * TOP_K, chunk), jnp.int32), pltpu.VMEM((chunk, PACK_W), jnp.int32),
                          pltpu.VMEM((chunk, PACK_W), jnp.int32)]
    tables = [dest3.reshape(SC_WORKERS, pw * TOP_K, chunk) for (_, dest3), (pw, chunk) in zip(sources, shapes)]
    return pl.kernel(
        body, out_type=jax.ShapeDtypeStruct((n_rows, PACK_W), jnp.int32), mesh=_sc_mesh(),
        scratch_types=scratch_types + [dma, dma, dma, dma],
        name="sc_dispatch")(*[xp for xp, _ in sources], *tables)


def _sc_gather(table, idx2):
    n_chunks, chunk = idx2.shape
    per_worker = n_chunks // SC_WORKERS
    assert per_worker * SC_WORKERS == n_chunks and chunk % 8 == 0

    def body(t_hbm, i_hbm, o_hbm, idx_v, rows_a, rows_b, gath_a, gath_b, put_a, put_b):
        worker = _sc_worker()
        pltpu.sync_copy(i_hbm.at[worker], idx_v)

        def gather(ci, rows, sem):
            return pltpu.make_async_copy(t_hbm.at[idx_v.at[ci]], rows, sem)

        def put(ci, rows, sem):
            row0 = (worker * per_worker + ci) * chunk
            return pltpu.make_async_copy(rows, o_hbm.at[pl.ds(pl.multiple_of(row0, 8), chunk)], sem)

        gather(0, rows_a, gath_a).start()

        @pl.loop(0, per_worker // 2)
        def _(p):
            ca, cb = 2 * p, 2 * p + 1
            gather(ca, rows_a, gath_a).wait()
            gather(cb, rows_b, gath_b).start()
            put(ca, rows_a, put_a).start()
            gather(cb, rows_b, gath_b).wait()
            put(ca, rows_a, put_a).wait()
            gather(jnp.minimum(ca + 2, per_worker - 1), rows_a, gath_a).start()
            put(cb, rows_b, put_b).start()
            put(cb, rows_b, put_b).wait()

        last = per_worker - 1
        gather(last, rows_a, gath_a).wait()
        if per_worker % 2:
            put(last, rows_a, put_a).start()
            put(last, rows_a, put_a).wait()

    dma = pltpu.SemaphoreType.DMA
    return pl.kernel(
        body, out_type=jax.ShapeDtypeStruct((n_chunks * chunk, PACK_W), jnp.int32), mesh=_sc_mesh(),
        scratch_types=[pltpu.VMEM((per_worker, chunk), jnp.int32), pltpu.VMEM((chunk, PACK_W), jnp.int32),
                       pltpu.VMEM((chunk, PACK_W), jnp.int32), dma, dma, dma, dma],
        name="sc_gather")(table, idx2.reshape(SC_WORKERS, per_worker, chunk))


def _final_kernel(x_ref, y_ref, g_ref, nf_ref, o_ref):
    g = g_ref[...]
    hi = jnp.zeros((x_ref.shape[0], PACK_W), _F32)
    lo = jnp.zeros((x_ref.shape[0], PACK_W), _F32)
    for k in range(TOP_K):
        y_hi, y_lo = _unpack_rows(y_ref[k])
        hi = hi + g[:, k:k + 1] * y_hi
        lo = lo + g[:, k:k + 1] * y_lo
    x = x_ref[...] + jnp.concatenate([hi, lo], axis=1)
    o_ref[...] = x * lax.rsqrt(jnp.mean(x * x, axis=-1, keepdims=True) + EPS) * nf_ref[...]


def _final(xmid, yg, gates_t, norm_f, *, tm, first_block):
    t = xmid.shape[0]
    return pl.pallas_call(
        _final_kernel, grid=(t // tm,),
        in_specs=[pl.BlockSpec((tm, D_MODEL), lambda i: (i, 0)),
                  pl.BlockSpec((TOP_K, tm, PACK_W), lambda i: (0, i + first_block, 0)),
                  pl.BlockSpec((tm, TOP_K), lambda i: (i + first_block, 0)),
                  pl.BlockSpec((1, D_MODEL), lambda i: (0, 0))],
        out_specs=pl.BlockSpec((tm, D_MODEL), lambda i: (i, 0)),
        out_shape=jax.ShapeDtypeStruct((t, D_MODEL), _F32),
        compiler_params=_cparams(("parallel",)), name="final",
    )(xmid, yg, gates_t, norm_f)


def kernel(x_prompt, x_sample, cache_k, cache_v, state_C, state_n, state_m, norm1, w_in, b_igate, b_fgate,
           lambda_q1, lambda_k1, lambda_q2, lambda_k2, subln, w_out, norm2, w_router, b_router, w_gate_up,
           b_gate_up, w_down, b_down, norm_f):
    nb, s, _ = x_prompt.shape
    db, ls, _ = x_sample.shape
    past = cache_k.shape[2]
    depth = w_in.shape[0]
    assert depth == 1 and s % PROJ_TILE == 0 and (db * ls) % 8 == 0
    l = 0
    lam_init = 0.8 - 0.6 * math.exp(-0.3 * l)
    t_p, t_s = nb * s, db * ls

    w_nat = jnp.pad(w_in[l].T, ((0, HEAD_W - 2 * ML_HEADS), (0, 0))).astype(_BF16)
    b_gates = jnp.pad(jnp.concatenate([b_igate[l], b_fgate[l]]).astype(_F32), (0, HEAD_W - 2 * ML_HEADS))[None]
    inv = ROPE_THETA ** (-jnp.arange(0, DA_QK, 2, dtype=_F32) / DA_QK)
    inv_full = jnp.tile(inv, HEAD_W // (DA_QK // 2))[None]
    lamv = jnp.stack([lambda_q1[l], lambda_k1[l], lambda_q2[l], lambda_k2[l]]).astype(_F32)
    g1 = norm1[l][None]

    pp = _proj(x_prompt, g1, w_nat, b_gates, inv_full, tm=PROJ_TILE, pos_base=0, pos_mod=s,
               names=("q", "kf", "kb", "vf", "vt", "mk", "mo", "gr", "mqt", "mvt"))
    ps = _proj(x_sample.reshape(1, t_s, D_MODEL), g1, w_nat, b_gates, inv_full, tm=t_s, pos_base=past, pos_mod=ls,
               names=("q", "kf", "kb", "vf", "mq", "mk", "mv", "mo", "gc", "gr"))
    kf_p, vf_p, kf_s, vf_s = pp["kf"], pp["vf"], ps["kf"], ps["vf"]

    da_p = _attn_prompt(lamv, pp["q"], pp["kb"], pp["vt"], lam_init=lam_init)
    per_b = lambda a: a.reshape(db, ls, a.shape[-1])
    kct = jnp.transpose(cache_k[l], (0, 2, 3, 4, 1)).reshape(db, GROUP_W, past)
    vc = cache_v[l].reshape(db, past * DA_HEADS, DA_V)
    vn = vf_s.reshape(db, ls, GROUP_W).astype(_BF16)
    da_s = _attn_sample(lamv, per_b(ps["q"]), per_b(ps["kb"]), vn, kct, vc, lam_init=lam_init)

    assert PROJ_TILE % (2 * CHUNK) == 0
    h_p, st_p, m_p = _mlstm_pairs(pp["mk"], pp["mqt"], pp["mvt"], pp["gr"],
                                  jnp.zeros((nb, ML_HEADS, VT_ROWS, ML_DK), _F32),
                                  jnp.zeros((nb, ML_HEADS, 8, HEAD_W), _F32))
    gr_sb = ps["gr"].reshape(8, db, ls).transpose(1, 0, 2)
    m0_s = jnp.broadcast_to(state_m[l].astype(_F32)[:, :, None, None], (db, ML_HEADS, 8, HEAD_W))
    h_s, c_s, n_s, m_s = _mlstm(per_b(ps["mq"]), per_b(ps["mk"]), per_b(ps["mv"]), per_b(ps["gc"]), gr_sb,
                                state_C[l].astype(_F32), state_n[l].astype(_F32), m0_s, chunk=ls, block=ls)
    c_p, n_p, mm_p = jnp.swapaxes(st_p[:, :, :ML_DV, :], -1, -2), st_p[:, :, ML_DV, :], m_p[:, :, 0, 0]
    mm_s = m_s[:, :, 0, 0]

    wo = w_out[l].astype(_BF16)
    w_rt = w_router[l].T.astype(_BF16)
    b_r = b_router[l].astype(_F32)[:, None]
    mix = functools.partial(_mix, subln=subln[l][None], w_out=wo, norm2=norm2[l][None], w_rt=w_rt, b_r=b_r,
                            lam_init=lam_init)
    flat = lambda a: a.reshape(-1, a.shape[-1])
    t_all = t_p + t_s
    assert t_p % t_s == 0
    xm_p, xp_p, idx_p, gate_p, rank_p, cnt_p = mix(
        flat(da_p), flat(h_p), flat(pp["mo"]), flat(x_prompt), cnt0=jnp.zeros((N_EXPERTS, 1), _F32), tm=PROJ_TILE)
    xm_s, xp_s, idx_s, gate_s, rank_s, cnt = mix(
        flat(da_s), flat(h_s), flat(ps["mo"]), flat(x_sample), cnt0=cnt_p, tm=t_s)

    rows = t_all * TOP_K
    n_blocks = -(-rows // MOE_BLOCK) + N_EXPERTS
    counts = cnt[:, 0].astype(jnp.int32)
    padded = (counts + MOE_BLOCK - 1) // MOE_BLOCK * MOE_BLOCK
    pend = jnp.cumsum(padded)
    pstart = pend - counts
    experts = jnp.arange(N_EXPERTS, dtype=jnp.int32)[:, None, None]
    first_row = lambda idx: jnp.sum(jnp.where(idx[None] == experts, pstart[:, None, None], 0), axis=0)
    dest_p, dest_s = first_row(idx_p) + rank_p, first_row(idx_s) + rank_s
    n_used = (pend[-1] // MOE_BLOCK).astype(jnp.int32)[None]
    blk_lo = jnp.arange(n_blocks, dtype=jnp.int32) * MOE_BLOCK
    blk_e = jnp.minimum(jnp.sum((pend[None, :] <= blk_lo[:, None]).astype(jnp.int32), axis=1), N_EXPERTS - 1)
    row_lo = jnp.sum(jnp.where(blk_e[:, None] == experts[:, 0].T, pstart[None, :], 0), axis=1)
    blk_rows = jnp.where(blk_lo < pend[-1], jnp.clip(blk_lo + MOE_BLOCK - row_lo, 0, MOE_BLOCK), 0).astype(jnp.int32)

    def chunked(dest):
        chunk = min(DISPATCH_CHUNK, dest.shape[1] // SC_WORKERS)
        return dest.reshape(TOP_K, dest.shape[1] // chunk, chunk).transpose(1, 0, 2)

    assert rows % (SC_WORKERS * COMBINE_CHUNK) == 0
    xb = _sc_dispatch([(xp_p, chunked(dest_p)), (xp_s, chunked(dest_s))], n_blocks * MOE_BLOCK)
    dest = jnp.concatenate([dest_p, dest_s], axis=1)
    yb = _moe(blk_e, n_used, blk_rows, xb, w_gate_up[l], b_gate_up[l][:, None, :], w_down[l], b_down[l][:, None, :])
    yg = _sc_gather(yb, dest.reshape(rows // COMBINE_CHUNK, COMBINE_CHUNK)).reshape(TOP_K, t_all, PACK_W)
    gates_t = jnp.concatenate([gate_p, gate_s], axis=1).T
    nf = norm_f[None]
    y_p = _final(xm_p, yg, gates_t, nf, tm=FINAL_TILE, first_block=0).reshape(nb, s, D_MODEL)
    y_s = _final(xm_s, yg, gates_t, nf, tm=t_s, first_block=t_p // t_s).reshape(db, ls, D_MODEL)

    st = lambda a: a[None]
    return (y_p, y_s,
            st(kf_p.reshape(nb, s, DA_HEADS, 2, DA_QK)), st(vf_p.reshape(nb, s, DA_HEADS, DA_V)),
            st(c_p), st(n_p), st(mm_p),
            st(kf_s.reshape(db, ls, DA_HEADS, 2, DA_QK)), st(vf_s.reshape(db, ls, DA_HEADS, DA_V)),
            st(c_s.astype(state_C.dtype)), st(n_s.astype(state_n.dtype)), st(mm_s.astype(state_m.dtype)))
```

```python
import functools
import math

import jax
import jax.numpy as jnp
from jax import lax
from jax.experimental import pallas as pl
from jax.experimental.pallas import tpu as pltpu
from jax.experimental.pallas import tpu_sc as plsc

D_MODEL = 1024
CHUNK = 64
DA_HEADS = 4
DA_QK = 64
DA_V = 128
ML_HEADS = 4
ML_DK = 128
ML_DV = 128
HEAD_W = 128
GROUP_W = 512
ROPE_THETA = 10000.0
N_EXPERTS = 32
TOP_K = 4
D_FF = 1024
SWIGLU_LIMIT = 7.0
SWIGLU_ALPHA = 1.702
EPS = 1e-6
NEG_BIG = -1e30

LOG2E = math.log2(math.e)
VT_ROWS = 144
PROJ_TILE = 512
FINAL_TILE = 1024
ATTN_TQ = 512
MOE_BLOCK = 512
FF_CHUNK = 512
PACK_W = D_MODEL // 2
SC_CORES = 2
SC_SUBCORES = 16
SC_WORKERS = SC_CORES * SC_SUBCORES
DISPATCH_CHUNK = 64
COMBINE_CHUNK = 80
VMEM_LIMIT = 56 * 1024 * 1024

_F32 = jnp.float32
_BF16 = jnp.bfloat16


def _cparams(sem):
    return pltpu.CompilerParams(dimension_semantics=sem, vmem_limit_bytes=VMEM_LIMIT)


def _dot(a, b, dims=(((1,), (0,)), ((), ())), precision=None):
    return lax.dot_general(a, b, dims, precision=precision, preferred_element_type=_F32)


_NT = (((1,), (1,)), ((), ()))
_TN = (((0,), (0,)), ((), ()))


def _pack_rows(x):
    half = x.shape[1] // 2
    bits = lambda v: lax.bitcast_convert_type(v.astype(_BF16).astype(_F32), jnp.int32)
    return bits(x[:, :half]) | lax.shift_right_logical(bits(x[:, half:]), 16)


def _unpack_rows(p):
    hi = lax.bitcast_convert_type(p & jnp.int32(-65536), _F32)
    lo = lax.bitcast_convert_type(lax.shift_left(p, 16), _F32)
    return hi, lo


def _proj_kernel(x_ref, g_ref, wt_ref, bg_ref, inv_ref, *out_refs, names, tm, pos_base, pos_mod):
    out = dict(zip(names, out_refs))
    x = x_ref[0]
    xn = (x * lax.rsqrt(jnp.mean(x * x, axis=-1, keepdims=True) + EPS) * g_ref[...]).astype(_BF16)

    row = pl.program_id(1) * tm + lax.broadcasted_iota(jnp.int32, (tm, 1), 0)
    pos = (pos_base + row % pos_mod).astype(_F32)
    ang = pos * inv_ref[...]
    cos = jnp.cos(ang)
    sin = jnp.sin(ang)
    lane = lax.broadcasted_iota(jnp.int32, (1, HEAD_W), 1)
    first = (lane % DA_QK) < (DA_QK // 2)
    sin = jnp.where(first, -sin, sin)

    def rope(z):
        rot = jnp.where(first, pltpu.roll(z, HEAD_W - DA_QK // 2, 1), pltpu.roll(z, DA_QK // 2, 1))
        return z * cos + rot * sin

    def group(c, width=GROUP_W):
        return _dot(xn, wt_ref[c * GROUP_W:c * GROUP_W + width, :], _NT)

    def put_q(zq):
        for h in range(DA_HEADS):
            sl = slice(h * HEAD_W, (h + 1) * HEAD_W)
            out["q"][0, :, sl] = (rope(zq[:, sl]) * (DA_QK ** -0.5 * LOG2E)).astype(_BF16)

    def put_k(zk):
        for h in range(DA_HEADS):
            sl = slice(h * HEAD_W, (h + 1) * HEAD_W)
            rk = rope(zk[:, sl])
            out["kf"][0, :, sl] = rk
            out["kb"][0, :, sl] = rk.astype(_BF16)

    def with_ones_rows(zt):
        ones = jnp.ones((VT_ROWS - DA_V, tm), _BF16)
        return jnp.concatenate(
            [part for h in range(DA_HEADS) for part in (zt[h * DA_V:(h + 1) * DA_V], ones)], axis=0)

    def put_v(zv):
        for h in range(DA_HEADS):
            out["vf"][0, pl.ds(h, tm, stride=DA_HEADS), :] = zv[:, h * DA_V:(h + 1) * DA_V]
        if "vt" in out:
            out["vt"][0, 0] = with_ones_rows(zv.T.astype(_BF16))

    def put_mq(z):
        if "mq" in out:
            out["mq"][0] = z.astype(_BF16)
        if "mqt" in out:
            out["mqt"][0, 0] = z.T.astype(_BF16)

    def put_mk(z):
        out["mk"][0] = (z * (ML_DK ** -0.5)).astype(_BF16)

    def put_mv(z):
        if "mv" in out:
            out["mv"][0] = z.astype(_BF16)
        if "mvt" in out:
            out["mvt"][0, 0] = with_ones_rows(z.T.astype(_BF16))

    def put_mo(z):
        out["og"][0] = jax.nn.sigmoid(z).astype(_BF16)

    def put_gates(z):
        zg = z + bg_ref[...]
        lane_g = lax.broadcasted_iota(jnp.int32, zg.shape, 1)
        logsig = jnp.minimum(zg, 0.0) - jnp.log(1.0 + jnp.exp(-jnp.abs(zg)))
        gates = jnp.where(lane_g < ML_HEADS, zg, logsig)
        if "gc" in out:
            out["gc"][0] = gates[:, :8]
        out["gr"][0] = gates.T[:8, :]

    sinks = [(put_q, GROUP_W), (put_k, GROUP_W), (put_v, GROUP_W), (put_mq, GROUP_W), (put_mk, GROUP_W),
             (put_mv, GROUP_W), (put_mo, GROUP_W), (put_gates, HEAD_W)]
    nxt = group(0, sinks[0][1])
    for c, (sink, _) in enumerate(sinks):
        cur = nxt
        if c + 1 < len(sinks):
            nxt = group(c + 1, sinks[c + 1][1])
        sink(cur)


def _proj(x3, norm1, wt, b_gates, inv_full, *, names, tm, pos_base, pos_mod):
    nb, s, _ = x3.shape
    nt = s // tm
    tok = lambda w, dt: (jax.ShapeDtypeStruct((nb, s, w), dt), pl.BlockSpec((1, tm, w), lambda b, i: (b, i, 0)))
    slab = lambda rows: (jax.ShapeDtypeStruct((nb, nt, rows, tm), _BF16),
                         pl.BlockSpec((1, 1, rows, tm), lambda b, i: (b, i, 0, 0)))
    const = lambda shape: pl.BlockSpec(shape, lambda b, i: (0,) * len(shape))
    kinds = {
        "q": tok(GROUP_W, _BF16), "kf": tok(GROUP_W, _F32), "kb": tok(GROUP_W, _BF16),
        "vf": (jax.ShapeDtypeStruct((nb, s * DA_HEADS, DA_V), _F32),
               pl.BlockSpec((1, tm * DA_HEADS, DA_V), lambda b, i: (b, i, 0))),
        "vt": slab(DA_HEADS * VT_ROWS),
        "mq": tok(GROUP_W, _BF16), "mk": tok(GROUP_W, _BF16), "mv": tok(GROUP_W, _BF16), "og": tok(GROUP_W, _BF16),
        "gc": tok(8, _F32),
        "gr": (jax.ShapeDtypeStruct((nb, 8, s), _F32), pl.BlockSpec((1, 8, tm), lambda b, i: (b, 0, i))),
        "mqt": slab(GROUP_W), "mvt": slab(ML_HEADS * VT_ROWS),
    }
    outs = pl.pallas_call(
        functools.partial(_proj_kernel, names=names, tm=tm, pos_base=pos_base, pos_mod=pos_mod),
        grid=(nb, nt),
        in_specs=[pl.BlockSpec((1, tm, D_MODEL), lambda b, i: (b, i, 0)), const((1, D_MODEL)),
                  const(wt.shape), const((1, HEAD_W)), const((1, HEAD_W))],
        out_specs=[kinds[n][1] for n in names], out_shape=[kinds[n][0] for n in names],
        compiler_params=_cparams(("parallel", "parallel")), name="proj",
    )(x3, norm1, wt, b_gates, inv_full)
    return dict(zip(names, outs))


def _lambda(lam_ref, lam_init):
    lv = lam_ref[...]
    s1 = jnp.sum(lv[0:1] * lv[1:2], axis=-1, keepdims=True)
    s2 = jnp.sum(lv[2:3] * lv[3:4], axis=-1, keepdims=True)
    return jnp.exp(s1) - jnp.exp(s2) + lam_init


def _split_components(q):
    lane = lax.broadcasted_iota(jnp.int32, q.shape, 1)
    zero = jnp.zeros_like(q)
    return jnp.concatenate([jnp.where(lane < DA_QK, q, zero), jnp.where(lane >= DA_QK, q, zero)], axis=0)


def _attn_kernel(lam_ref, q_ref, k_ref, vt_ref, o_ref, acc_ref, s_ref, *, tq, tk, lam_init):
    i = pl.program_id(1)
    heads = range(DA_HEADS)
    qz = [_split_components(q_ref[0, :, h * HEAD_W:(h + 1) * HEAD_W]) for h in heads]
    acc_ref[...] = jnp.zeros_like(acc_ref)

    def scores(h, j):
        k_t = k_ref[0, pl.ds(pl.multiple_of(j * tk, tk), tk), h * HEAD_W:(h + 1) * HEAD_W]
        return _dot(k_t, qz[h], _NT)

    def step(j, ms, last):
        if last:
            kpos = j * tk + lax.broadcasted_iota(jnp.int32, (tk, 1), 0)
            qpos = i * tq + lax.broadcasted_iota(jnp.int32, (1, 2 * tq), 1) % tq
            visible = kpos < (qpos // CHUNK + 1) * CHUNK
        out = []
        s = s_ref[...]
        for h in heads:
            s_next = None
            if h + 1 < DA_HEADS:
                s_next = scores(h + 1, j)
            elif not last:
                s_next = scores(0, j + 1)
            if last:
                s = jnp.where(visible, s, NEG_BIG)
            m_new = jnp.maximum(ms[h], jnp.max(s, axis=0, keepdims=True))
            alpha = jnp.exp2(ms[h] - m_new)
            p = jnp.exp2(s - m_new).astype(_BF16)
            acc_ref[h] = alpha * acc_ref[h] + _dot(vt_ref[0, j, h * VT_ROWS:(h + 1) * VT_ROWS, :], p)
            out.append(m_new)
            s = s_next
        if not last:
            s_ref[...] = s
        return tuple(out)

    n_full = (i * tq) // tk
    s_ref[...] = scores(0, 0)
    init = tuple(jnp.full((1, 2 * tq), NEG_BIG, _F32) for _ in heads)
    ms = lax.fori_loop(0, n_full, lambda j, c: step(j, c, False), init)
    step(n_full, ms, True)

    lam = _lambda(lam_ref, lam_init)
    for h in heads:
        acc = acc_ref[h]
        o = acc[:DA_V] / acc[DA_V:DA_V + 1]
        o_ref[0, :, h * HEAD_W:(h + 1) * HEAD_W] = (o[:, :tq] - lam * o[:, tq:]).T.astype(_BF16)


def _attn_prompt(lamv, q, kb, vt, *, lam_init):
    nb, s, _ = q.shape
    nk, tk = vt.shape[1], vt.shape[3]
    tq = ATTN_TQ
    assert tk % tq == 0
    return pl.pallas_call(
        functools.partial(_attn_kernel, tq=tq, tk=tk, lam_init=lam_init),
        grid=(nb, s // tq),
        in_specs=[pl.BlockSpec((4, DA_QK), lambda b, i: (0, 0)),
                  pl.BlockSpec((1, tq, GROUP_W), lambda b, i: (b, i, 0)),
                  pl.BlockSpec((1, s, GROUP_W), lambda b, i: (b, 0, 0)),
                  pl.BlockSpec((1, nk, DA_HEADS * VT_ROWS, tk), lambda b, i: (b, 0, 0, 0))],
        out_specs=pl.BlockSpec((1, tq, GROUP_W), lambda b, i: (b, i, 0)),
        out_shape=jax.ShapeDtypeStruct((nb, s, GROUP_W), _BF16),
        scratch_shapes=[pltpu.VMEM((DA_HEADS, VT_ROWS, 2 * tq), _F32), pltpu.VMEM((tk, 2 * tq), _F32)],
        compiler_params=_cparams(("parallel", "arbitrary")), name="attn",
    )(lamv, q, kb, vt)


def _attn_sample_kernel(lam_ref, q_ref, kn_ref, vn_ref, kc_ref, vc_ref, o_ref, *, lq, past, lam_init):
    lam = _lambda(lam_ref, lam_init)
    for h in range(DA_HEADS):
        sl = slice(h * HEAD_W, (h + 1) * HEAD_W)
        qz = _split_components(q_ref[0, :, sl])
        kct = kc_ref[0, sl, :].astype(_BF16)
        vc = vc_ref[0, pl.ds(h, past, stride=DA_HEADS), :].astype(_BF16)
        s_c = _dot(qz, kct)
        s_n = _dot(qz, kn_ref[0, :, sl], _NT)
        m = jnp.maximum(jnp.max(s_c, axis=-1, keepdims=True), jnp.max(s_n, axis=-1, keepdims=True))
        p_c = jnp.exp2(s_c - m)
        p_n = jnp.exp2(s_n - m)
        l = jnp.sum(p_c, axis=-1, keepdims=True) + jnp.sum(p_n, axis=-1, keepdims=True)
        o = (_dot(p_c.astype(_BF16), vc) + _dot(p_n.astype(_BF16), vn_ref[0, :, sl])) / l
        o_ref[0, :, sl] = (o[:lq] - lam * o[lq:]).astype(_BF16)


def _attn_sample(lamv, q, kn, vn, kct, vc, *, lam_init):
    nb, lq, _ = q.shape
    past = kct.shape[2]
    new = pl.BlockSpec((1, lq, GROUP_W), lambda b: (b, 0, 0))
    return pl.pallas_call(
        functools.partial(_attn_sample_kernel, lq=lq, past=past, lam_init=lam_init),
        grid=(nb,),
        in_specs=[pl.BlockSpec((4, DA_QK), lambda b: (0, 0)), new, new, new,
                  pl.BlockSpec((1, GROUP_W, past), lambda b: (b, 0, 0)),
                  pl.BlockSpec((1, past * DA_HEADS, DA_V), lambda b: (b, 0, 0))],
        out_specs=new, out_shape=jax.ShapeDtypeStruct((nb, lq, GROUP_W), _BF16),
        compiler_params=_cparams(("parallel",)), name="attn_sample",
    )(lamv, q, kn, vn, kct, vc)


def _mlstm_kernel(q_ref, k_ref, v_ref, gc_ref, gr_ref, c0_ref, n0_ref, m0_ref, h_ref, c_ref, n_ref, m_ref,
                  c_sc, n_sc, m_sc, *, chunk, nchunks):
    j = pl.program_id(1)

    @pl.when(j == 0)
    def _():
        c_sc[...] = c0_ref[0]
        n_sc[...] = n0_ref[0]
        m_sc[...] = m0_ref[0]

    L = chunk
    r_i = lax.broadcasted_iota(jnp.int32, (L, L), 0)
    c_i = lax.broadcasted_iota(jnp.int32, (L, L), 1)
    causal = r_i >= c_i
    tril = causal.astype(_F32)
    triu = (r_i <= c_i).astype(_F32)
    hi = lax.Precision.HIGHEST

    for c in range(nchunks):
        rows = slice(c * L, (c + 1) * L)
        gcol = gc_ref[0, rows, :]
        grow = gr_ref[0, :, rows]
        fc_all = _dot(tril, gcol, precision=hi)
        fr_all = _dot(grow, triu, precision=hi)
        for hh in range(ML_HEADS):
            sl = slice(hh * HEAD_W, (hh + 1) * HEAD_W)
            m = m_sc[hh][0:1, 0:1]
            fc = fc_all[:, ML_HEADS + hh:ML_HEADS + hh + 1]
            fr = fr_all[ML_HEADS + hh:ML_HEADS + hh + 1, :]
            igc = gcol[:, hh:hh + 1]
            igr = grow[hh:hh + 1, :]
            a = jnp.where(causal, fc - fr + igr, -jnp.inf)
            b = fc + m
            m_t = jnp.maximum(b, jnp.max(a, axis=-1, keepdims=True))
            w = jnp.exp(a - m_t)
            sc = jnp.exp(b - m_t)
            q = q_ref[0, rows, sl]
            k = k_ref[0, rows, sl]
            v = v_ref[0, rows, sl]
            wqk = w * _dot(q, k, _NT)
            cmat = c_sc[hh]
            nrow = n_sc[hh:hh + 1, :]
            qn = jnp.sum(q.astype(_F32) * nrow, axis=-1, keepdims=True)
            num = _dot(wqk.astype(_BF16), v) + sc * _dot(q, cmat.astype(_BF16))
            den = jnp.sum(wqk, axis=-1, keepdims=True) + sc * qn
            h_ref[0, rows, sl] = (num / jnp.maximum(jnp.abs(den), jnp.exp(-m_t))).astype(_BF16)

            fl = fc[L - 1:L, :]
            g = fl - fc + igc
            m_new = jnp.maximum(fl + m, jnp.max(g, axis=0, keepdims=True))
            decay = jnp.exp(fl + m - m_new)
            kw = k.astype(_F32) * jnp.exp(g - m_new)
            c_sc[hh] = decay * cmat + _dot(kw.astype(_BF16), v, _TN)
            n_sc[hh:hh + 1, :] = decay * nrow + jnp.sum(kw, axis=0, keepdims=True)
            m_sc[hh] = jnp.broadcast_to(m_new, m_sc.shape[1:])

    @pl.when(j == pl.num_programs(1) - 1)
    def _():
        c_ref[0] = c_sc[...]
        n_ref[0] = n_sc[...]
        m_ref[0] = m_sc[...]


def _mlstm(q, k, v, gc, gr, c0, n0, m0, *, chunk, block):
    nb, s, _ = q.shape
    tok = lambda w: pl.BlockSpec((1, block, w), lambda b, j: (b, j, 0))
    cspec = pl.BlockSpec((1, ML_HEADS, ML_DK, ML_DV), lambda b, j: (b, 0, 0, 0))
    nspec = pl.BlockSpec((1, ML_HEADS, ML_DK), lambda b, j: (b, 0, 0))
    mspec = pl.BlockSpec((1, ML_HEADS, 8, HEAD_W), lambda b, j: (b, 0, 0, 0))
    return pl.pallas_call(
        functools.partial(_mlstm_kernel, chunk=chunk, nchunks=block // chunk),
        grid=(nb, s // block),
        in_specs=[tok(GROUP_W), tok(GROUP_W), tok(GROUP_W), tok(8),
                  pl.BlockSpec((1, 8, block), lambda b, j: (b, 0, j)), cspec, nspec, mspec],
        out_specs=[tok(GROUP_W), cspec, nspec, mspec],
        out_shape=[jax.ShapeDtypeStruct((nb, s, GROUP_W), _BF16), jax.ShapeDtypeStruct(c0.shape, _F32),
                   jax.ShapeDtypeStruct(n0.shape, _F32), jax.ShapeDtypeStruct(m0.shape, _F32)],
        scratch_shapes=[pltpu.VMEM((ML_HEADS, ML_DK, ML_DV), _F32), pltpu.VMEM((ML_HEADS, ML_DK), _F32),
                        pltpu.VMEM((ML_HEADS, 8, HEAD_W), _F32)],
        compiler_params=_cparams(("parallel", "arbitrary")), name="mlstm",
    )(q, k, v, gc, gr, c0, n0, m0)


def _mlstm_pairs_kernel(k_ref, qt_ref, vt_ref, gr_ref, st0_ref, m0_ref, h_ref, st_ref, m_ref, st_sc, m_sc,
                        *, block):
    j = pl.program_id(1)

    @pl.when(j == 0)
    def _():
        st_sc[...] = st0_ref[0]
        m_sc[...] = m0_ref[0]

    L, W = CHUNK, 2 * CHUNK
    g8 = gr_ref[0]
    pos = lax.broadcasted_iota(jnp.int32, (1, block), 1) % L
    roll = lambda x, sh: pltpu.roll(x, sh % block, 1)
    steps = [1 << b for b in range(L.bit_length() - 1)]

    def scan(x, combine, fill, reverse=False):
        for sh in steps:
            ok = (pos < L - sh) if reverse else (pos >= sh)
            x = combine(x, jnp.where(ok, roll(x, -sh if reverse else sh), fill))
        return x

    to_head_rows = lambda x: pltpu.roll(x, ML_HEADS, 0)
    f8 = to_head_rows(scan(g8, jnp.add, 0.0))
    d8 = g8 - f8
    cm8 = scan(d8, jnp.maximum, -jnp.inf)
    big_g = to_head_rows(scan(g8, jnp.add, 0.0, reverse=True) - g8) + g8
    mg8 = jnp.maximum(scan(big_g, jnp.maximum, -jnp.inf), scan(big_g, jnp.maximum, -jnp.inf, reverse=True))
    ws8 = jnp.exp(big_g - mg8)
    d_cols = d8.T

    s_i = lax.broadcasted_iota(jnp.int32, (W, W), 0)
    t_i = lax.broadcasted_iota(jnp.int32, (W, W), 1)
    allowed = (s_i // L == t_i // L) & (s_i <= t_i)
    lane_w = lax.broadcasted_iota(jnp.int32, (1, W), 1)
    first = lane_w < L

    sts = [st_sc[hh] for hh in range(ML_HEADS)]
    ms = [m_sc[hh][0:1, 0:1] for hh in range(ML_HEADS)]

    def independent(p, hh):
        lanes = slice(p * W, (p + 1) * W)
        feat = slice(hh * HEAD_W, (hh + 1) * HEAD_W)
        kp = k_ref[0, lanes, feat]
        qt = qt_ref[0, 0, feat, lanes]
        v1t = vt_ref[0, 0, hh * VT_ROWS:(hh + 1) * VT_ROWS, lanes]
        ws = ws8[hh:hh + 1, lanes]
        v1f = v1t.astype(_F32)
        return dict(
            p=p, hh=hh, qt=qt, v1t=v1t, qk_t=_dot(kp, qt),
            upd_a=_dot((v1f * jnp.where(first, ws, 0.0)).astype(_BF16), kp),
            upd_b=_dot((v1f * jnp.where(first, 0.0, ws)).astype(_BF16), kp))

    def dependent(u):
        p, hh, qt = u["p"], u["hh"], u["qt"]
        lanes = slice(p * W, (p + 1) * W)
        row = lambda x: x[hh:hh + 1, lanes]
        cm, f_row = row(cm8), row(f8)
        w_t = jnp.exp(jnp.where(allowed, d_cols[lanes, hh:hh + 1] - cm, -jnp.inf))
        intra = _dot(u["v1t"], (w_t * u["qk_t"]).astype(_BF16))
        st_a, m_a = sts[hh], ms[hh]
        fl_a, mg_a = f_row[:, L - 1:L], row(mg8)[:, 0:1]
        fl_b, mg_b = f_row[:, W - 1:W], row(mg8)[:, L:L + 1]
        m_b = jnp.maximum(fl_a + m_a, mg_a)
        st_b = jnp.exp(fl_a + m_a - m_b) * st_a + jnp.exp(mg_a - m_b) * u["upd_a"]
        m_c = jnp.maximum(fl_b + m_b, mg_b)
        sts[hh] = jnp.exp(fl_b + m_b - m_c) * st_b + jnp.exp(mg_b - m_c) * u["upd_b"]
        ms[hh] = m_c
        inter = jnp.where(first, _dot(st_a.astype(_BF16), qt), _dot(st_b.astype(_BF16), qt))
        m_prev = jnp.where(first, m_a, m_b)
        mt = jnp.maximum(m_prev, cm)
        tot = jnp.exp(cm - mt) * intra + jnp.exp(m_prev - mt) * inter
        den = jnp.maximum(jnp.abs(tot[ML_DV:ML_DV + 1]), jnp.exp(-(f_row + mt)))
        h_ref[0, lanes, hh * HEAD_W:(hh + 1) * HEAD_W] = (tot[:ML_DV] / den).T.astype(_BF16)

    units = [(p, hh) for p in range(block // W) for hh in range(ML_HEADS)]
    nxt = independent(*units[0])
    for n in range(len(units)):
        cur = nxt
        if n + 1 < len(units):
            nxt = independent(*units[n + 1])
        dependent(cur)
    for hh in range(ML_HEADS):
        st_sc[hh] = sts[hh]
        m_sc[hh] = jnp.broadcast_to(ms[hh], m_sc.shape[1:])

    @pl.when(j == pl.num_programs(1) - 1)
    def _():
        st_ref[0] = st_sc[...]
        m_ref[0] = m_sc[...]


def _mlstm_pairs(k, qt, vt, gr, st0, m0):
    nb, s, _ = k.shape
    block = qt.shape[3]
    nt = s // block
    state = pl.BlockSpec((1, ML_HEADS, VT_ROWS, ML_DK), lambda b, j: (b, 0, 0, 0))
    mspec = pl.BlockSpec((1, ML_HEADS, 8, HEAD_W), lambda b, j: (b, 0, 0, 0))
    tok = pl.BlockSpec((1, block, GROUP_W), lambda b, j: (b, j, 0))
    return pl.pallas_call(
        functools.partial(_mlstm_pairs_kernel, block=block),
        grid=(nb, nt),
        in_specs=[tok, pl.BlockSpec((1, 1, GROUP_W, block), lambda b, j: (b, j, 0, 0)),
                  pl.BlockSpec((1, 1, ML_HEADS * VT_ROWS, block), lambda b, j: (b, j, 0, 0)),
                  pl.BlockSpec((1, 8, block), lambda b, j: (b, 0, j)), state, mspec],
        out_specs=[tok, state, mspec],
        out_shape=[jax.ShapeDtypeStruct((nb, s, GROUP_W), _BF16), jax.ShapeDtypeStruct(st0.shape, _F32),
                   jax.ShapeDtypeStruct(m0.shape, _F32)],
        scratch_shapes=[pltpu.VMEM((ML_HEADS, VT_ROWS, ML_DK), _F32), pltpu.VMEM((ML_HEADS, 8, HEAD_W), _F32)],
        compiler_params=_cparams(("parallel", "arbitrary")), name="mlstm_pairs",
    )(k, qt, vt, gr, st0, m0)


def _mix_kernel(da_ref, ml_ref, og_ref, x_ref, subln_ref, wo_ref, g2_ref, wr_ref, br_ref, cnt0_ref,
                xmid_ref, xp_ref, idx_ref, gate_ref, rank_ref, cnt_ref, cnt_sc, *, tm, lam_init):
    step = pl.program_id(0)

    @pl.when(step == 0)
    def _():
        cnt_sc[...] = cnt0_ref[...]

    parts = []
    for h in range(DA_HEADS):
        d = da_ref[:, h * HEAD_W:(h + 1) * HEAD_W].astype(_F32)
        d = d * lax.rsqrt(jnp.mean(d * d, axis=-1, keepdims=True) + EPS) * subln_ref[...]
        parts.append((d * (1.0 - lam_init)).astype(_BF16))
    parts.append((og_ref[...].astype(_F32) * ml_ref[...].astype(_F32)).astype(_BF16))
    xm = x_ref[...] + _dot(jnp.concatenate(parts, axis=1), wo_ref[...])
    xmid_ref[...] = xm
    xn_f = xm * lax.rsqrt(jnp.mean(xm * xm, axis=-1, keepdims=True) + EPS) * g2_ref[...]
    xp_ref[...] = _pack_rows(xn_f)
    xn = xn_f.astype(_BF16)

    logits = _dot(wr_ref[...], xn, _NT) + br_ref[...]
    e_i = lax.broadcasted_iota(jnp.int32, logits.shape, 0)
    member = jnp.zeros(logits.shape, jnp.bool_)
    vals, idxs = [], []
    for _ in range(TOP_K):
        mx = jnp.max(logits, axis=0, keepdims=True)
        sel = jnp.min(jnp.where(logits == mx, e_i, N_EXPERTS), axis=0, keepdims=True)
        hit = e_i == sel
        member = member | hit
        logits = jnp.where(hit, -jnp.inf, logits)
        vals.append(mx)
        idxs.append(sel)
    ex = [jnp.exp(v - vals[0]) for v in vals]
    tot = ex[0] + ex[1] + ex[2] + ex[3]
    idx_ref[...] = jnp.concatenate(idxs, axis=0)
    gate_ref[...] = jnp.concatenate([e / tot for e in ex], axis=0)

    upper = (lax.broadcasted_iota(jnp.int32, (tm, tm), 0) < lax.broadcasted_iota(jnp.int32, (tm, tm), 1))
    memf = member.astype(_F32)
    before = _dot(memf.astype(_BF16), upper.astype(_BF16)) + cnt_sc[...]
    rank_ref[...] = jnp.concatenate(
        [jnp.sum(jnp.where(e_i == s, before, 0.0), axis=0, keepdims=True) for s in idxs], axis=0).astype(jnp.int32)
    cnt_sc[...] += jnp.sum(memf, axis=1, keepdims=True)
    cnt_ref[...] = cnt_sc[...]


def _mix(da, ml, og, x, subln, w_out, norm2, w_rt, b_r, cnt0, *, tm, lam_init):
    t = x.shape[0]
    tok = lambda w: pl.BlockSpec((tm, w), lambda i: (i, 0))
    const = lambda shape: pl.BlockSpec(shape, lambda i: (0,) * len(shape))
    lane_tok = pl.BlockSpec((TOP_K, tm), lambda i: (0, i))
    return pl.pallas_call(
        functools.partial(_mix_kernel, tm=tm, lam_init=lam_init),
        grid=(t // tm,),
        in_specs=[tok(GROUP_W), tok(GROUP_W), tok(GROUP_W), tok(D_MODEL), const((1, DA_V)),
                  const((D_MODEL, D_MODEL)), const((1, D_MODEL)), const((N_EXPERTS, D_MODEL)),
                  const((N_EXPERTS, 1)), const((N_EXPERTS, 1))],
        out_specs=[tok(D_MODEL), tok(PACK_W), lane_tok, lane_tok, lane_tok, const((N_EXPERTS, 1))],
        out_shape=[jax.ShapeDtypeStruct((t, D_MODEL), _F32), jax.ShapeDtypeStruct((t, PACK_W), jnp.int32),
                   jax.ShapeDtypeStruct((TOP_K, t), jnp.int32), jax.ShapeDtypeStruct((TOP_K, t), _F32),
                   jax.ShapeDtypeStruct((TOP_K, t), jnp.int32), jax.ShapeDtypeStruct((N_EXPERTS, 1), _F32)],
        scratch_shapes=[pltpu.VMEM((N_EXPERTS, 1), _F32)],
        compiler_params=_cparams(("arbitrary",)), name="mix",
    )(da, ml, og, x, subln, w_out, norm2, w_rt, b_r, cnt0)


def _moe_kernel(be_ref, nu_ref, nv_ref, x_ref, wgu_ref, bgu_ref, wd_ref, bd_ref, y_ref, wgu_sc, wd_sc):
    j = pl.program_id(0)
    e_prev = be_ref[jnp.maximum(j - 1, 0)]
    fresh = (j == 0) | (be_ref[j] != e_prev)

    @pl.when(fresh & (j < nu_ref[0]))
    def _():
        rows = 128
        def cast_gu(r, _):
            sl = pl.ds(pl.multiple_of(r * rows, rows), rows)
            wgu_sc[sl, :] = wgu_ref[0, sl, :].astype(_BF16)
            return 0
        lax.fori_loop(0, D_MODEL // rows, cast_gu, 0)
        def cast_d(r, _):
            sl = pl.ds(pl.multiple_of(r * rows, rows), rows)
            wd_sc[sl, :] = wd_ref[0, sl, :].astype(_BF16)
            return 0
        lax.fori_loop(0, D_FF // rows, cast_d, 0)

    @pl.when(j >= nu_ref[0])
    def _():
        y_ref[...] = jnp.zeros_like(y_ref)

    def ffn(row0):
        hi, lo = _unpack_rows(x_ref[row0:, :])
        x = jnp.concatenate([hi.astype(_BF16), lo.astype(_BF16)], axis=1)
        n_chunks = D_FF // FF_CHUNK

        def gate_up(c):
            lo = c * FF_CHUNK
            return (_dot(x, wgu_sc[:, lo:lo + FF_CHUNK]), _dot(x, wgu_sc[:, D_FF + lo:D_FF + lo + FF_CHUNK]))

        acc = None
        nxt = gate_up(0)
        for c in range(n_chunks):
            lo = c * FF_CHUNK
            gate, up = nxt
            if c + 1 < n_chunks:
                nxt = gate_up(c + 1)
            gate = jnp.minimum(gate + bgu_ref[0, :, lo:lo + FF_CHUNK], SWIGLU_LIMIT)
            up = jnp.clip(up + bgu_ref[0, :, D_FF + lo:D_FF + lo + FF_CHUNK], -SWIGLU_LIMIT, SWIGLU_LIMIT)
            act = (up + 1.0) * (gate * jax.nn.sigmoid(SWIGLU_ALPHA * gate))
            down = _dot(act.astype(_BF16), wd_sc[lo:lo + FF_CHUNK, :])
            acc = down if acc is None else acc + down
        y_ref[row0:, :] = _pack_rows(acc + bd_ref[0])

    half = MOE_BLOCK // 2
    used = j < nu_ref[0]

    @pl.when(used & (nv_ref[j] > half))
    def _():
        ffn(0)

    @pl.when(used & (nv_ref[j] <= half))
    def _():
        y_ref[:half, :] = jnp.zeros((half, PACK_W), jnp.int32)
        ffn(half)


def _moe(blk_e, n_used, blk_rows, xb, w_gu, b_gu, w_d, b_d):
    n_blocks = xb.shape[0] // MOE_BLOCK
    last = lambda j, nu: jnp.minimum(j, nu[0] - 1)
    row_spec = pl.BlockSpec((MOE_BLOCK, PACK_W), lambda j, be, nu, nv: (last(j, nu), 0))
    exp_spec = lambda r, c: pl.BlockSpec((1, r, c), lambda j, be, nu, nv: (be[last(j, nu)], 0, 0))
    grid_spec = pltpu.PrefetchScalarGridSpec(
        num_scalar_prefetch=3, grid=(n_blocks,),
        in_specs=[row_spec, exp_spec(D_MODEL, 2 * D_FF), exp_spec(1, 2 * D_FF), exp_spec(D_FF, D_MODEL),
                  exp_spec(1, D_MODEL)],
        out_specs=pl.BlockSpec((MOE_BLOCK, PACK_W), lambda j, be, nu, nv: (j, 0)),
        scratch_shapes=[pltpu.VMEM((D_MODEL, 2 * D_FF), _BF16), pltpu.VMEM((D_FF, D_MODEL), _BF16)])
    return pl.pallas_call(
        _moe_kernel, grid_spec=grid_spec, out_shape=jax.ShapeDtypeStruct(xb.shape, jnp.int32),
        compiler_params=_cparams(("arbitrary",)), name="moe",
    )(blk_e, n_used, blk_rows, xb, w_gu, b_gu, w_d, b_d)


def _sc_mesh():
    return plsc.VectorSubcoreMesh(core_axis_name="c", subcore_axis_name="s")


def _sc_worker():
    return lax.axis_index("s") * SC_CORES + lax.axis_index("c")


def _sc_dispatch(sources, n_rows):
    n_src = len(sources)
    shapes = []
    for _, dest3 in sources:
        n_chunks, _, chunk = dest3.shape
        per_worker = n_chunks // SC_WORKERS
        assert per_worker * SC_WORKERS == n_chunks and chunk % 8 == 0
        shapes.append((per_worker, chunk))

    def body(*refs):
        x_hbms, d_hbms, o_hbm = refs[:n_src], refs[n_src:2 * n_src], refs[2 * n_src]
        scratch = refs[2 * n_src + 1:]
        load_a, load_b, scat_a, scat_b = scratch[3 * n_src:]
        worker = _sc_worker()
        for i, (per_worker, chunk) in enumerate(shapes):
            x_hbm, idx_v, rows_a, rows_b = x_hbms[i], scratch[3 * i], scratch[3 * i + 1], scratch[3 * i + 2]
            pltpu.sync_copy(d_hbms[i].at[worker], idx_v)

            def load(ci, rows, sem):
                row0 = (worker * per_worker + ci) * chunk
                return pltpu.make_async_copy(x_hbm.at[pl.ds(pl.multiple_of(row0, 8), chunk)], rows, sem)

            def scatters(ci, rows, sem):
                return [pltpu.make_async_copy(rows, o_hbm.at[idx_v.at[ci * TOP_K + k]], sem) for k in range(TOP_K)]

            def start(copies):
                for cp in copies:
                    cp.start()

            def wait(copies):
                for cp in copies:
                    cp.wait()

            load(0, rows_a, load_a).start()

            @pl.loop(0, per_worker // 2)
            def _(p):
                ca, cb = 2 * p, 2 * p + 1
                load(ca, rows_a, load_a).wait()
                load(cb, rows_b, load_b).start()
                start(scatters(ca, rows_a, scat_a))
                load(cb, rows_b, load_b).wait()
                wait(scatters(ca, rows_a, scat_a))
                load(jnp.minimum(ca + 2, per_worker - 1), rows_a, load_a).start()
                start(scatters(cb, rows_b, scat_b))
                wait(scatters(cb, rows_b, scat_b))

            last = per_worker - 1
            load(last, rows_a, load_a).wait()
            if per_worker % 2:
                start(scatters(last, rows_a, scat_a))
                wait(scatters(last, rows_a, scat_a))

    dma = pltpu.SemaphoreType.DMA
    scratch_types = []
    for per_worker, chunk in shapes:
        scratch_types += [pltpu.VMEM((per_worker * TOP_K, chunk), jnp.int32), pltpu.VMEM((chunk, PACK_W), jnp.int32),
                          pltpu.VMEM((chunk, PACK_W), jnp.int32)]
    tables = [dest3.reshape(SC_WORKERS, pw * TOP_K, chunk) for (_, dest3), (pw, chunk) in zip(sources, shapes)]
    return pl.kernel(
        body, out_type=jax.ShapeDtypeStruct((n_rows, PACK_W), jnp.int32), mesh=_sc_mesh(),
        scratch_types=scratch_types + [dma, dma, dma, dma],
        name="sc_dispatch")(*[xp for xp, _ in sources], *tables)


def _sc_gather(table, idx2):
    n_chunks, chunk = idx2.shape
    per_worker = n_chunks // SC_WORKERS
    assert per_worker * SC_WORKERS == n_chunks and chunk % 8 == 0

    def body(t_hbm, i_hbm, o_hbm, idx_v, rows_a, rows_b, gath_a, gath_b, put_a, put_b):
        worker = _sc_worker()
        pltpu.sync_copy(i_hbm.at[worker], idx_v)

        def gather(ci, rows, sem):
            return pltpu.make_async_copy(t_hbm.at[idx_v.at[ci]], rows, sem)

        def put(ci, rows, sem):
            row0 = (worker * per_worker + ci) * chunk
            return pltpu.make_async_copy(rows, o_hbm.at[pl.ds(pl.multiple_of(row0, 8), chunk)], sem)

        gather(0, rows_a, gath_a).start()

        @pl.loop(0, per_worker // 2)
        def _(p):
            ca, cb = 2 * p, 2 * p + 1
            gather(ca, rows_a, gath_a).wait()
            gather(cb, rows_b, gath_b).start()
            put(ca, rows_a, put_a).start()
            gather(cb, rows_b, gath_b).wait()
            put(ca, rows_a, put_a).wait()
            gather(jnp.minimum(ca + 2, per_worker - 1), rows_a, gath_a).start()
            put(cb, rows_b, put_b).start()
            put(cb, rows_b, put_b).wait()

        last = per_worker - 1
        gather(last, rows_a, gath_a).wait()
        if per_worker % 2:
            put(last, rows_a, put_a).start()
            put(last, rows_a, put_a).wait()

    dma = pltpu.SemaphoreType.DMA
    return pl.kernel(
        body, out_type=jax.ShapeDtypeStruct((n_chunks * chunk, PACK_W), jnp.int32), mesh=_sc_mesh(),
        scratch_types=[pltpu.VMEM((per_worker, chunk), jnp.int32), pltpu.VMEM((chunk, PACK_W), jnp.int32),
                       pltpu.VMEM((chunk, PACK_W), jnp.int32), dma, dma, dma, dma],
        name="sc_gather")(table, idx2.reshape(SC_WORKERS, per_worker, chunk))


def _final_kernel(x_ref, y_ref, g_ref, nf_ref, o_ref):
    g_rows = g_ref[...]
    g = jnp.concatenate([g_rows, jnp.zeros((8 - TOP_K, g_rows.shape[1]), _F32)], axis=0).T
    hi = jnp.zeros((x_ref.shape[0], PACK_W), _F32)
    lo = jnp.zeros((x_ref.shape[0], PACK_W), _F32)
    for k in range(TOP_K):
        y_hi, y_lo = _unpack_rows(y_ref[k])
        hi = hi + g[:, k:k + 1] * y_hi
        lo = lo + g[:, k:k + 1] * y_lo
    x = x_ref[...] + jnp.concatenate([hi, lo], axis=1)
    o_ref[...] = x * lax.rsqrt(jnp.mean(x * x, axis=-1, keepdims=True) + EPS) * nf_ref[...]


def _final(xmid, yg, gates, norm_f, *, tm, first_block):
    t = xmid.shape[0]
    return pl.pallas_call(
        _final_kernel, grid=(t // tm,),
        in_specs=[pl.BlockSpec((tm, D_MODEL), lambda i: (i, 0)),
                  pl.BlockSpec((TOP_K, tm, PACK_W), lambda i: (0, i + first_block, 0)),
                  pl.BlockSpec((TOP_K, tm), lambda i: (0, i)),
                  pl.BlockSpec((1, D_MODEL), lambda i: (0, 0))],
        out_specs=pl.BlockSpec((tm, D_MODEL), lambda i: (i, 0)),
        out_shape=jax.ShapeDtypeStruct((t, D_MODEL), _F32),
        compiler_params=_cparams(("parallel",)), name="final",
    )(xmid, yg, gates, norm_f)


def kernel(x_prompt, x_sample, cache_k, cache_v, state_C, state_n, state_m, norm1, w_in, b_igate, b_fgate,
           lambda_q1, lambda_k1, lambda_q2, lambda_k2, subln, w_out, norm2, w_router, b_router, w_gate_up,
           b_gate_up, w_down, b_down, norm_f):
    nb, s, _ = x_prompt.shape
    db, ls, _ = x_sample.shape
    past = cache_k.shape[2]
    depth = w_in.shape[0]
    assert depth == 1 and s % PROJ_TILE == 0 and (db * ls) % 8 == 0
    l = 0
    lam_init = 0.8 - 0.6 * math.exp(-0.3 * l)
    t_p, t_s = nb * s, db * ls

    w_nat = jnp.pad(w_in[l].T, ((0, HEAD_W - 2 * ML_HEADS), (0, 0))).astype(_BF16)
    b_gates = jnp.pad(jnp.concatenate([b_igate[l], b_fgate[l]]).astype(_F32), (0, HEAD_W - 2 * ML_HEADS))[None]
    inv = ROPE_THETA ** (-jnp.arange(0, DA_QK, 2, dtype=_F32) / DA_QK)
    inv_full = jnp.tile(inv, HEAD_W // (DA_QK // 2))[None]
    lamv = jnp.stack([lambda_q1[l], lambda_k1[l], lambda_q2[l], lambda_k2[l]]).astype(_F32)
    g1 = norm1[l][None]

    pp = _proj(x_prompt, g1, w_nat, b_gates, inv_full, tm=PROJ_TILE, pos_base=0, pos_mod=s,
               names=("q", "kf", "kb", "vf", "vt", "mk", "og", "gr", "mqt", "mvt"))
    ps = _proj(x_sample.reshape(1, t_s, D_MODEL), g1, w_nat, b_gates, inv_full, tm=t_s, pos_base=past, pos_mod=ls,
               names=("q", "kf", "kb", "vf", "mq", "mk", "mv", "og", "gc", "gr"))
    kf_p, vf_p, kf_s, vf_s = pp["kf"], pp["vf"], ps["kf"], ps["vf"]

    da_p = _attn_prompt(lamv, pp["q"], pp["kb"], pp["vt"], lam_init=lam_init)
    per_b = lambda a: a.reshape(db, ls, a.shape[-1])
    kct = jnp.transpose(cache_k[l], (0, 2, 3, 4, 1)).reshape(db, GROUP_W, past)
    vc = cache_v[l].reshape(db, past * DA_HEADS, DA_V)
    vn = vf_s.reshape(db, ls, GROUP_W).astype(_BF16)
    da_s = _attn_sample(lamv, per_b(ps["q"]), per_b(ps["kb"]), vn, kct, vc, lam_init=lam_init)

    assert PROJ_TILE % (2 * CHUNK) == 0
    h_p, st_p, m_p = _mlstm_pairs(pp["mk"], pp["mqt"], pp["mvt"], pp["gr"],
                                  jnp.zeros((nb, ML_HEADS, VT_ROWS, ML_DK), _F32),
                                  jnp.zeros((nb, ML_HEADS, 8, HEAD_W), _F32))
    gr_sb = ps["gr"].reshape(8, db, ls).transpose(1, 0, 2)
    m0_s = jnp.broadcast_to(state_m[l].astype(_F32)[:, :, None, None], (db, ML_HEADS, 8, HEAD_W))
    h_s, c_s, n_s, m_s = _mlstm(per_b(ps["mq"]), per_b(ps["mk"]), per_b(ps["mv"]), per_b(ps["gc"]), gr_sb,
                                state_C[l].astype(_F32), state_n[l].astype(_F32), m0_s, chunk=ls, block=ls)
    c_p, n_p, mm_p = jnp.swapaxes(st_p[:, :, :ML_DV, :], -1, -2), st_p[:, :, ML_DV, :], m_p[:, :, 0, 0]
    mm_s = m_s[:, :, 0, 0]

    wo = w_out[l].astype(_BF16)
    w_rt = w_router[l].T.astype(_BF16)
    b_r = b_router[l].astype(_F32)[:, None]
    mix = functools.partial(_mix, subln=subln[l][None], w_out=wo, norm2=norm2[l][None], w_rt=w_rt, b_r=b_r,
                            lam_init=lam_init)
    flat = lambda a: a.reshape(-1, a.shape[-1])
    t_all = t_p + t_s
    assert t_p % t_s == 0
    xm_p, xp_p, idx_p, gate_p, rank_p, cnt_p = mix(
        flat(da_p), flat(h_p), flat(pp["og"]), flat(x_prompt), cnt0=jnp.zeros((N_EXPERTS, 1), _F32), tm=PROJ_TILE)
    xm_s, xp_s, idx_s, gate_s, rank_s, cnt = mix(
        flat(da_s), flat(h_s), flat(ps["og"]), flat(x_sample), cnt0=cnt_p, tm=t_s)

    rows = t_all * TOP_K
    n_blocks = -(-rows // MOE_BLOCK) + N_EXPERTS
    counts = cnt[:, 0].astype(jnp.int32)
    padded = (counts + MOE_BLOCK - 1) // MOE_BLOCK * MOE_BLOCK
    pend = jnp.cumsum(padded)
    pstart = pend - counts
    experts = jnp.arange(N_EXPERTS, dtype=jnp.int32)[:, None, None]
    first_row = lambda idx: jnp.sum(jnp.where(idx[None] == experts, pstart[:, None, None], 0), axis=0)
    dest_p, dest_s = first_row(idx_p) + rank_p, first_row(idx_s) + rank_s
    n_used = (pend[-1] // MOE_BLOCK).astype(jnp.int32)[None]
    blk_lo = jnp.arange(n_blocks, dtype=jnp.int32) * MOE_BLOCK
    blk_e = jnp.minimum(jnp.sum((pend[None, :] <= blk_lo[:, None]).astype(jnp.int32), axis=1), N_EXPERTS - 1)
    row_lo = jnp.sum(jnp.where(blk_e[:, None] == experts[:, 0].T, pstart[None, :], 0), axis=1)
    blk_rows = jnp.where(blk_lo < pend[-1], jnp.clip(blk_lo + MOE_BLOCK - row_lo, 0, MOE_BLOCK), 0).astype(jnp.int32)

    def chunked(dest):
        chunk = min(DISPATCH_CHUNK, dest.shape[1] // SC_WORKERS)
        return dest.reshape(TOP_K, dest.shape[1] // chunk, chunk).transpose(1, 0, 2)

    assert rows % (SC_WORKERS * COMBINE_CHUNK) == 0
    xb = _sc_dispatch([(xp_p, chunked(dest_p)), (xp_s, chunked(dest_s))], n_blocks * MOE_BLOCK)
    dest = jnp.concatenate([dest_p, dest_s], axis=1)
    yb = _moe(blk_e, n_used, blk_rows, xb, w_gate_up[l], b_gate_up[l][:, None, :], w_down[l], b_down[l][:, None, :])
    yg = _sc_gather(yb, dest.reshape(rows // COMBINE_CHUNK, COMBINE_CHUNK)).reshape(TOP_K, t_all, PACK_W)
    nf = norm_f[None]
    y_p = _final(xm_p, yg, gate_p, nf, tm=FINAL_TILE, first_block=0).reshape(nb, s, D_MODEL)
    y_s = _final(xm_s, yg, gate_s, nf, tm=t_s, first_block=t_p // t_s).reshape(db, ls, D_MODEL)

    st = lambda a: a[None]
    return (y_p, y_s,
            st(kf_p.reshape(nb, s, DA_HEADS, 2, DA_QK)), st(vf_p.reshape(nb, s, DA_HEADS, DA_V)),
            st(c_p), st(n_p), st(mm_p),
            st(kf_s.reshape(db, ls, DA_HEADS, 2, DA_QK)), st(vf_s.reshape(db, ls, DA_HEADS, DA_V)),
            st(c_s.astype(state_C.dtype)), st(n_s.astype(state_n.dtype)), st(mm_s.astype(state_m.dtype)))
```

```python
import functools
import math

import jax
import jax.numpy as jnp
from jax import lax
from jax.experimental import pallas as pl
from jax.experimental.pallas import tpu as pltpu
from jax.experimental.pallas import tpu_sc as plsc

D_MODEL = 1024
CHUNK = 64
DA_HEADS = 4
DA_QK = 64
DA_V = 128
ML_HEADS = 4
ML_DK = 128
ML_DV = 128
HEAD_W = 128
SUBLANES = 8
N_GATES = 2 * ML_HEADS
CAST_ROWS = 128
GROUP_W = 512
ROPE_THETA = 10000.0
N_EXPERTS = 32
TOP_K = 4
D_FF = 1024
SWIGLU_LIMIT = 7.0
SWIGLU_ALPHA = 1.702
EPS = 1e-6
NEG_BIG = -1e30

LOG2E = math.log2(math.e)
VT_ROWS = 144
PROJ_TILE = 512
FINAL_TILE = 1024
ATTN_TQ = 512
MOE_BLOCK = 512
FF_CHUNK = 512
PACK_W = D_MODEL // 2
SC_CORES = 2
SC_SUBCORES = 16
SC_WORKERS = SC_CORES * SC_SUBCORES
DISPATCH_CHUNK = 64
COMBINE_CHUNK = 80
VMEM_BYTES = 64 * 1024 * 1024

_F32 = jnp.float32
_BF16 = jnp.bfloat16


def _nbytes(shape, dtype):
    return math.prod(shape) * jnp.dtype(dtype).itemsize


def _cparams(sem, blocks, resident=0):
    need = 2 * sum(blocks) + resident
    assert need <= VMEM_BYTES, need
    return pltpu.CompilerParams(dimension_semantics=sem, vmem_limit_bytes=need)


def _dot(a, b, dims=(((1,), (0,)), ((), ())), precision=None):
    return lax.dot_general(a, b, dims, precision=precision, preferred_element_type=_F32)


_NT = (((1,), (1,)), ((), ()))
_TN = (((0,), (0,)), ((), ()))


def _pack_rows(x):
    half = x.shape[1] // 2
    bits = lambda v: lax.bitcast_convert_type(v.astype(_BF16).astype(_F32), jnp.int32)
    return bits(x[:, :half]) | lax.shift_right_logical(bits(x[:, half:]), 16)


def _unpack_rows(p):
    hi = lax.bitcast_convert_type(p & jnp.int32(-65536), _F32)
    lo = lax.bitcast_convert_type(lax.shift_left(p, 16), _F32)
    return hi, lo


def _proj_kernel(x_ref, g_ref, wt_ref, bg_ref, inv_ref, *out_refs, names, tm, pos_base, pos_mod):
    out = dict(zip(names, out_refs))
    x = x_ref[0]
    xn = (x * lax.rsqrt(jnp.mean(x * x, axis=-1, keepdims=True) + EPS) * g_ref[...]).astype(_BF16)

    row = pl.program_id(1) * tm + lax.broadcasted_iota(jnp.int32, (tm, 1), 0)
    pos = (pos_base + row % pos_mod).astype(_F32)
    ang = pos * inv_ref[...]
    cos = jnp.cos(ang)
    sin = jnp.sin(ang)
    lane = lax.broadcasted_iota(jnp.int32, (1, HEAD_W), 1)
    first = (lane % DA_QK) < (DA_QK // 2)
    sin = jnp.where(first, -sin, sin)

    def rope(z):
        rot = jnp.where(first, pltpu.roll(z, HEAD_W - DA_QK // 2, 1), pltpu.roll(z, DA_QK // 2, 1))
        return z * cos + rot * sin

    def group(c, width=GROUP_W):
        return _dot(xn, wt_ref[c * GROUP_W:c * GROUP_W + width, :], _NT)

    def put_q(zq):
        for h in range(DA_HEADS):
            sl = slice(h * HEAD_W, (h + 1) * HEAD_W)
            out["q"][0, :, sl] = (rope(zq[:, sl]) * (DA_QK ** -0.5 * LOG2E)).astype(_BF16)

    def put_k(zk):
        for h in range(DA_HEADS):
            sl = slice(h * HEAD_W, (h + 1) * HEAD_W)
            rk = rope(zk[:, sl])
            out["kf"][0, :, sl] = rk
            out["kb"][0, :, sl] = rk.astype(_BF16)

    def with_ones_rows(zt):
        ones = jnp.ones((VT_ROWS - DA_V, tm), _BF16)
        return jnp.concatenate(
            [part for h in range(DA_HEADS) for part in (zt[h * DA_V:(h + 1) * DA_V], ones)], axis=0)

    def put_v(zv):
        for h in range(DA_HEADS):
            out["vf"][0, pl.ds(h, tm, stride=DA_HEADS), :] = zv[:, h * DA_V:(h + 1) * DA_V]
        if "vt" in out:
            out["vt"][0, 0] = with_ones_rows(zv.T.astype(_BF16))

    def put_mq(z):
        if "mq" in out:
            out["mq"][0] = z.astype(_BF16)
        if "mqt" in out:
            out["mqt"][0, 0] = z.T.astype(_BF16)

    def put_mk(z):
        out["mk"][0] = (z * (ML_DK ** -0.5)).astype(_BF16)

    def put_mv(z):
        if "mv" in out:
            out["mv"][0] = z.astype(_BF16)
        if "mvt" in out:
            out["mvt"][0, 0] = with_ones_rows(z.T.astype(_BF16))

    def put_mo(z):
        out["og"][0] = jax.nn.sigmoid(z).astype(_BF16)

    def put_gates(z):
        zg = z + bg_ref[...]
        lane_g = lax.broadcasted_iota(jnp.int32, zg.shape, 1)
        logsig = jnp.minimum(zg, 0.0) - jnp.log(1.0 + jnp.exp(-jnp.abs(zg)))
        gates = jnp.where(lane_g < ML_HEADS, zg, logsig)
        if "gc" in out:
            out["gc"][0] = gates[:, :N_GATES]
        out["gr"][0] = gates.T[:N_GATES, :]

    sinks = [(put_q, GROUP_W), (put_k, GROUP_W), (put_v, GROUP_W), (put_mq, GROUP_W), (put_mk, GROUP_W),
             (put_mv, GROUP_W), (put_mo, GROUP_W), (put_gates, HEAD_W)]
    nxt = group(0, sinks[0][1])
    for c, (sink, _) in enumerate(sinks):
        cur = nxt
        if c + 1 < len(sinks):
            nxt = group(c + 1, sinks[c + 1][1])
        sink(cur)


def _proj(x3, norm1, wt, b_gates, inv_full, *, names, tm, pos_base, pos_mod):
    nb, s, _ = x3.shape
    nt = s // tm
    tok = lambda w, dt: (jax.ShapeDtypeStruct((nb, s, w), dt), pl.BlockSpec((1, tm, w), lambda b, i: (b, i, 0)))
    slab = lambda rows: (jax.ShapeDtypeStruct((nb, nt, rows, tm), _BF16),
                         pl.BlockSpec((1, 1, rows, tm), lambda b, i: (b, i, 0, 0)))
    const = lambda shape: pl.BlockSpec(shape, lambda b, i: (0,) * len(shape))
    kinds = {
        "q": tok(GROUP_W, _BF16), "kf": tok(GROUP_W, _F32), "kb": tok(GROUP_W, _BF16),
        "vf": (jax.ShapeDtypeStruct((nb, s * DA_HEADS, DA_V), _F32),
               pl.BlockSpec((1, tm * DA_HEADS, DA_V), lambda b, i: (b, i, 0))),
        "vt": slab(DA_HEADS * VT_ROWS),
        "mq": tok(GROUP_W, _BF16), "mk": tok(GROUP_W, _BF16), "mv": tok(GROUP_W, _BF16), "og": tok(GROUP_W, _BF16),
        "gc": tok(N_GATES, _F32),
        "gr": (jax.ShapeDtypeStruct((nb, N_GATES, s), _F32), pl.BlockSpec((1, N_GATES, tm), lambda b, i: (b, 0, i))),
        "mqt": slab(GROUP_W), "mvt": slab(ML_HEADS * VT_ROWS),
    }
    blocks = [_nbytes((tm, D_MODEL), _F32), _nbytes(wt.shape, _BF16)]
    blocks += [_nbytes(kinds[n][1].block_shape, kinds[n][0].dtype) for n in names]
    outs = pl.pallas_call(
        functools.partial(_proj_kernel, names=names, tm=tm, pos_base=pos_base, pos_mod=pos_mod),
        grid=(nb, nt),
        in_specs=[pl.BlockSpec((1, tm, D_MODEL), lambda b, i: (b, i, 0)), const((1, D_MODEL)),
                  const(wt.shape), const((1, HEAD_W)), const((1, HEAD_W))],
        out_specs=[kinds[n][1] for n in names], out_shape=[kinds[n][0] for n in names],
        compiler_params=_cparams(("parallel", "parallel"), blocks, 8 * _nbytes((tm, GROUP_W), _F32)), name="proj",
    )(x3, norm1, wt, b_gates, inv_full)
    return dict(zip(names, outs))


def _lambda(lam_ref, lam_init):
    lv = lam_ref[...]
    s1 = jnp.sum(lv[0:1] * lv[1:2], axis=-1, keepdims=True)
    s2 = jnp.sum(lv[2:3] * lv[3:4], axis=-1, keepdims=True)
    return jnp.exp(s1) - jnp.exp(s2) + lam_init


def _split_components(q):
    lane = lax.broadcasted_iota(jnp.int32, q.shape, 1)
    zero = jnp.zeros_like(q)
    return jnp.concatenate([jnp.where(lane < DA_QK, q, zero), jnp.where(lane >= DA_QK, q, zero)], axis=0)


def _attn_kernel(lam_ref, q_ref, k_ref, vt_ref, o_ref, acc_ref, s_ref, *, tq, tk, lam_init):
    i = pl.program_id(1)
    heads = range(DA_HEADS)
    qz = [_split_components(q_ref[0, :, h * HEAD_W:(h + 1) * HEAD_W]) for h in heads]
    acc_ref[...] = jnp.zeros_like(acc_ref)

    def scores(h, j):
        k_t = k_ref[0, pl.ds(pl.multiple_of(j * tk, tk), tk), h * HEAD_W:(h + 1) * HEAD_W]
        return _dot(k_t, qz[h], _NT)

    def step(j, ms, last):
        if last:
            kpos = j * tk + lax.broadcasted_iota(jnp.int32, (tk, 1), 0)
            qpos = i * tq + lax.broadcasted_iota(jnp.int32, (1, 2 * tq), 1) % tq
            visible = kpos < (qpos // CHUNK + 1) * CHUNK
        out = []
        s = s_ref[...]
        for h in heads:
            s_next = None
            if h + 1 < DA_HEADS:
                s_next = scores(h + 1, j)
            elif not last:
                s_next = scores(0, j + 1)
            if last:
                s = jnp.where(visible, s, NEG_BIG)
            m_new = jnp.maximum(ms[h], jnp.max(s, axis=0, keepdims=True))
            alpha = jnp.exp2(ms[h] - m_new)
            p = jnp.exp2(s - m_new).astype(_BF16)
            acc_ref[h] = alpha * acc_ref[h] + _dot(vt_ref[0, j, h * VT_ROWS:(h + 1) * VT_ROWS, :], p)
            out.append(m_new)
            s = s_next
        if not last:
            s_ref[...] = s
        return tuple(out)

    n_full = (i * tq) // tk
    s_ref[...] = scores(0, 0)
    init = tuple(jnp.full((1, 2 * tq), NEG_BIG, _F32) for _ in heads)
    ms = lax.fori_loop(0, n_full, lambda j, c: step(j, c, False), init)
    step(n_full, ms, True)

    lam = _lambda(lam_ref, lam_init)
    for h in heads:
        acc = acc_ref[h]
        o = acc[:DA_V] / acc[DA_V:DA_V + 1]
        o_ref[0, :, h * HEAD_W:(h + 1) * HEAD_W] = (o[:, :tq] - lam * o[:, tq:]).T.astype(_BF16)


def _attn_prompt(lamv, q, kb, vt, *, lam_init):
    nb, s, _ = q.shape
    nk, tk = vt.shape[1], vt.shape[3]
    tq = ATTN_TQ
    assert tk % tq == 0
    return pl.pallas_call(
        functools.partial(_attn_kernel, tq=tq, tk=tk, lam_init=lam_init),
        grid=(nb, s // tq),
        in_specs=[pl.BlockSpec((4, DA_QK), lambda b, i: (0, 0)),
                  pl.BlockSpec((1, tq, GROUP_W), lambda b, i: (b, i, 0)),
                  pl.BlockSpec((1, s, GROUP_W), lambda b, i: (b, 0, 0)),
                  pl.BlockSpec((1, nk, DA_HEADS * VT_ROWS, tk), lambda b, i: (b, 0, 0, 0))],
        out_specs=pl.BlockSpec((1, tq, GROUP_W), lambda b, i: (b, i, 0)),
        out_shape=jax.ShapeDtypeStruct((nb, s, GROUP_W), _BF16),
        scratch_shapes=[pltpu.VMEM((DA_HEADS, VT_ROWS, 2 * tq), _F32), pltpu.VMEM((tk, 2 * tq), _F32)],
        compiler_params=_cparams(
            ("parallel", "arbitrary"),
            [_nbytes((tq, GROUP_W), _BF16), _nbytes((s, GROUP_W), _BF16), _nbytes(vt.shape[1:], _BF16),
             _nbytes((tq, GROUP_W), _BF16)],
            _nbytes((DA_HEADS, VT_ROWS, 2 * tq), _F32) + 5 * _nbytes((tk, 2 * tq), _F32)), name="attn",
    )(lamv, q, kb, vt)


def _attn_sample_kernel(lam_ref, q_ref, kn_ref, vn_ref, kc_ref, vc_ref, o_ref, *, lq, past, lam_init):
    lam = _lambda(lam_ref, lam_init)
    for h in range(DA_HEADS):
        sl = slice(h * HEAD_W, (h + 1) * HEAD_W)
        qz = _split_components(q_ref[0, :, sl])
        kct = kc_ref[0, sl, :].astype(_BF16)
        vc = vc_ref[0, pl.ds(h, past, stride=DA_HEADS), :].astype(_BF16)
        s_c = _dot(qz, kct)
        s_n = _dot(qz, kn_ref[0, :, sl], _NT)
        m = jnp.maximum(jnp.max(s_c, axis=-1, keepdims=True), jnp.max(s_n, axis=-1, keepdims=True))
        p_c = jnp.exp2(s_c - m)
        p_n = jnp.exp2(s_n - m)
        l = jnp.sum(p_c, axis=-1, keepdims=True) + jnp.sum(p_n, axis=-1, keepdims=True)
        o = (_dot(p_c.astype(_BF16), vc) + _dot(p_n.astype(_BF16), vn_ref[0, :, sl])) / l
        o_ref[0, :, sl] = (o[:lq] - lam * o[lq:]).astype(_BF16)


def _attn_sample(lamv, q, kn, vn, kct, vc, *, lam_init):
    nb, lq, _ = q.shape
    past = kct.shape[2]
    new = pl.BlockSpec((1, lq, GROUP_W), lambda b: (b, 0, 0))
    return pl.pallas_call(
        functools.partial(_attn_sample_kernel, lq=lq, past=past, lam_init=lam_init),
        grid=(nb,),
        in_specs=[pl.BlockSpec((4, DA_QK), lambda b: (0, 0)), new, new, new,
                  pl.BlockSpec((1, GROUP_W, past), lambda b: (b, 0, 0)),
                  pl.BlockSpec((1, past * DA_HEADS, DA_V), lambda b: (b, 0, 0))],
        out_specs=new, out_shape=jax.ShapeDtypeStruct((nb, lq, GROUP_W), _BF16),
        compiler_params=_cparams(
            ("parallel",), [_nbytes((GROUP_W, past), _F32), _nbytes((past * DA_HEADS, DA_V), _F32)]
            + 4 * [_nbytes((lq, GROUP_W), _F32)], 8 * _nbytes((2 * lq, past), _F32)), name="attn_sample",
    )(lamv, q, kn, vn, kct, vc)


def _mlstm_kernel(q_ref, k_ref, v_ref, gc_ref, gr_ref, c0_ref, n0_ref, m0_ref, h_ref, c_ref, n_ref, m_ref,
                  c_sc, n_sc, m_sc, *, chunk, nchunks):
    j = pl.program_id(1)

    @pl.when(j == 0)
    def _():
        c_sc[...] = c0_ref[0]
        n_sc[...] = n0_ref[0]
        m_sc[...] = m0_ref[0]

    L = chunk
    r_i = lax.broadcasted_iota(jnp.int32, (L, L), 0)
    c_i = lax.broadcasted_iota(jnp.int32, (L, L), 1)
    causal = r_i >= c_i
    tril = causal.astype(_F32)
    triu = (r_i <= c_i).astype(_F32)
    hi = lax.Precision.HIGHEST

    for c in range(nchunks):
        rows = slice(c * L, (c + 1) * L)
        gcol = gc_ref[0, rows, :]
        grow = gr_ref[0, :, rows]
        fc_all = _dot(tril, gcol, precision=hi)
        fr_all = _dot(grow, triu, precision=hi)
        for hh in range(ML_HEADS):
            sl = slice(hh * HEAD_W, (hh + 1) * HEAD_W)
            m = m_sc[hh][0:1, 0:1]
            fc = fc_all[:, ML_HEADS + hh:ML_HEADS + hh + 1]
            fr = fr_all[ML_HEADS + hh:ML_HEADS + hh + 1, :]
            igc = gcol[:, hh:hh + 1]
            igr = grow[hh:hh + 1, :]
            a = jnp.where(causal, fc - fr + igr, -jnp.inf)
            b = fc + m
            m_t = jnp.maximum(b, jnp.max(a, axis=-1, keepdims=True))
            w = jnp.exp(a - m_t)
            sc = jnp.exp(b - m_t)
            q = q_ref[0, rows, sl]
            k = k_ref[0, rows, sl]
            v = v_ref[0, rows, sl]
            wqk = w * _dot(q, k, _NT)
            cmat = c_sc[hh]
            nrow = n_sc[hh:hh + 1, :]
            qn = jnp.sum(q.astype(_F32) * nrow, axis=-1, keepdims=True)
            num = _dot(wqk.astype(_BF16), v) + sc * _dot(q, cmat.astype(_BF16))
            den = jnp.sum(wqk, axis=-1, keepdims=True) + sc * qn
            h_ref[0, rows, sl] = (num / jnp.maximum(jnp.abs(den), jnp.exp(-m_t))).astype(_BF16)

            fl = fc[L - 1:L, :]
            g = fl - fc + igc
            m_new = jnp.maximum(fl + m, jnp.max(g, axis=0, keepdims=True))
            decay = jnp.exp(fl + m - m_new)
            kw = k.astype(_F32) * jnp.exp(g - m_new)
            c_sc[hh] = decay * cmat + _dot(kw.astype(_BF16), v, _TN)
            n_sc[hh:hh + 1, :] = decay * nrow + jnp.sum(kw, axis=0, keepdims=True)
            m_sc[hh] = jnp.broadcast_to(m_new, m_sc.shape[1:])

    @pl.when(j == pl.num_programs(1) - 1)
    def _():
        c_ref[0] = c_sc[...]
        n_ref[0] = n_sc[...]
        m_ref[0] = m_sc[...]


def _mlstm(q, k, v, gc, gr, c0, n0, m0, *, chunk, block):
    nb, s, _ = q.shape
    tok = lambda w: pl.BlockSpec((1, block, w), lambda b, j: (b, j, 0))
    cspec = pl.BlockSpec((1, ML_HEADS, ML_DK, ML_DV), lambda b, j: (b, 0, 0, 0))
    nspec = pl.BlockSpec((1, ML_HEADS, ML_DK), lambda b, j: (b, 0, 0))
    mspec = pl.BlockSpec((1, ML_HEADS, SUBLANES, HEAD_W), lambda b, j: (b, 0, 0, 0))
    return pl.pallas_call(
        functools.partial(_mlstm_kernel, chunk=chunk, nchunks=block // chunk),
        grid=(nb, s // block),
        in_specs=[tok(GROUP_W), tok(GROUP_W), tok(GROUP_W), tok(N_GATES),
                  pl.BlockSpec((1, N_GATES, block), lambda b, j: (b, 0, j)), cspec, nspec, mspec],
        out_specs=[tok(GROUP_W), cspec, nspec, mspec],
        out_shape=[jax.ShapeDtypeStruct((nb, s, GROUP_W), _BF16), jax.ShapeDtypeStruct(c0.shape, _F32),
                   jax.ShapeDtypeStruct(n0.shape, _F32), jax.ShapeDtypeStruct(m0.shape, _F32)],
        scratch_shapes=[pltpu.VMEM((ML_HEADS, ML_DK, ML_DV), _F32), pltpu.VMEM((ML_HEADS, ML_DK), _F32),
                        pltpu.VMEM((ML_HEADS, SUBLANES, HEAD_W), _F32)],
        compiler_params=_cparams(
            ("parallel", "arbitrary"), 5 * [_nbytes((block, GROUP_W), _F32)] + 2 * [_nbytes(c0.shape[1:], _F32)],
            2 * _nbytes(c0.shape[1:], _F32)), name="mlstm",
    )(q, k, v, gc, gr, c0, n0, m0)


def _mlstm_pairs_kernel(k_ref, qt_ref, vt_ref, gr_ref, st0_ref, m0_ref, h_ref, st_ref, m_ref, st_sc, m_sc,
                        *, block):
    j = pl.program_id(1)

    @pl.when(j == 0)
    def _():
        st_sc[...] = st0_ref[0]
        m_sc[...] = m0_ref[0]

    L, W = CHUNK, 2 * CHUNK
    g8 = gr_ref[0]
    pos = lax.broadcasted_iota(jnp.int32, (1, block), 1) % L
    roll = lambda x, sh: pltpu.roll(x, sh % block, 1)
    steps = [1 << b for b in range(L.bit_length() - 1)]

    def scan(x, combine, fill, reverse=False):
        for sh in steps:
            ok = (pos < L - sh) if reverse else (pos >= sh)
            x = combine(x, jnp.where(ok, roll(x, -sh if reverse else sh), fill))
        return x

    to_head_rows = lambda x: pltpu.roll(x, ML_HEADS, 0)
    f8 = to_head_rows(scan(g8, jnp.add, 0.0))
    d8 = g8 - f8
    cm8 = scan(d8, jnp.maximum, -jnp.inf)
    big_g = to_head_rows(scan(g8, jnp.add, 0.0, reverse=True) - g8) + g8
    mg8 = jnp.maximum(scan(big_g, jnp.maximum, -jnp.inf), scan(big_g, jnp.maximum, -jnp.inf, reverse=True))
    ws8 = jnp.exp(big_g - mg8)
    d_cols = d8.T

    s_i = lax.broadcasted_iota(jnp.int32, (W, W), 0)
    t_i = lax.broadcasted_iota(jnp.int32, (W, W), 1)
    allowed = (s_i // L == t_i // L) & (s_i <= t_i)
    lane_w = lax.broadcasted_iota(jnp.int32, (1, W), 1)
    first = lane_w < L

    sts = [st_sc[hh] for hh in range(ML_HEADS)]
    ms = [m_sc[hh][0:1, 0:1] for hh in range(ML_HEADS)]

    def independent(p, hh):
        lanes = slice(p * W, (p + 1) * W)
        feat = slice(hh * HEAD_W, (hh + 1) * HEAD_W)
        kp = k_ref[0, lanes, feat]
        qt = qt_ref[0, 0, feat, lanes]
        v1t = vt_ref[0, 0, hh * VT_ROWS:(hh + 1) * VT_ROWS, lanes]
        ws = ws8[hh:hh + 1, lanes]
        v1f = v1t.astype(_F32)
        return dict(
            p=p, hh=hh, qt=qt, v1t=v1t, qk_t=_dot(kp, qt),
            upd_a=_dot((v1f * jnp.where(first, ws, 0.0)).astype(_BF16), kp),
            upd_b=_dot((v1f * jnp.where(first, 0.0, ws)).astype(_BF16), kp))

    def dependent(u):
        p, hh, qt = u["p"], u["hh"], u["qt"]
        lanes = slice(p * W, (p + 1) * W)
        row = lambda x: x[hh:hh + 1, lanes]
        cm, f_row = row(cm8), row(f8)
        w_t = jnp.exp(jnp.where(allowed, d_cols[lanes, hh:hh + 1] - cm, -jnp.inf))
        intra = _dot(u["v1t"], (w_t * u["qk_t"]).astype(_BF16))
        st_a, m_a = sts[hh], ms[hh]
        fl_a, mg_a = f_row[:, L - 1:L], row(mg8)[:, 0:1]
        fl_b, mg_b = f_row[:, W - 1:W], row(mg8)[:, L:L + 1]
        m_b = jnp.maximum(fl_a + m_a, mg_a)
        st_b = jnp.exp(fl_a + m_a - m_b) * st_a + jnp.exp(mg_a - m_b) * u["upd_a"]
        m_c = jnp.maximum(fl_b + m_b, mg_b)
        sts[hh] = jnp.exp(fl_b + m_b - m_c) * st_b + jnp.exp(mg_b - m_c) * u["upd_b"]
        ms[hh] = m_c
        inter = jnp.where(first, _dot(st_a.astype(_BF16), qt), _dot(st_b.astype(_BF16), qt))
        m_prev = jnp.where(first, m_a, m_b)
        mt = jnp.maximum(m_prev, cm)
        tot = jnp.exp(cm - mt) * intra + jnp.exp(m_prev - mt) * inter
        den = jnp.maximum(jnp.abs(tot[ML_DV:ML_DV + 1]), jnp.exp(-(f_row + mt)))
        h_ref[0, lanes, hh * HEAD_W:(hh + 1) * HEAD_W] = (tot[:ML_DV] / den).T.astype(_BF16)

    units = [(p, hh) for p in range(block // W) for hh in range(ML_HEADS)]
    nxt = independent(*units[0])
    for n in range(len(units)):
        cur = nxt
        if n + 1 < len(units):
            nxt = independent(*units[n + 1])
        dependent(cur)
    for hh in range(ML_HEADS):
        st_sc[hh] = sts[hh]
        m_sc[hh] = jnp.broadcast_to(ms[hh], m_sc.shape[1:])

    @pl.when(j == pl.num_programs(1) - 1)
    def _():
        st_ref[0] = st_sc[...]
        m_ref[0] = m_sc[...]


def _mlstm_pairs(k, qt, vt, gr, st0, m0):
    nb, s, _ = k.shape
    block = qt.shape[3]
    nt = s // block
    state = pl.BlockSpec((1, ML_HEADS, VT_ROWS, ML_DK), lambda b, j: (b, 0, 0, 0))
    mspec = pl.BlockSpec((1, ML_HEADS, SUBLANES, HEAD_W), lambda b, j: (b, 0, 0, 0))
    tok = pl.BlockSpec((1, block, GROUP_W), lambda b, j: (b, j, 0))
    return pl.pallas_call(
        functools.partial(_mlstm_pairs_kernel, block=block),
        grid=(nb, nt),
        in_specs=[tok, pl.BlockSpec((1, 1, GROUP_W, block), lambda b, j: (b, j, 0, 0)),
                  pl.BlockSpec((1, 1, ML_HEADS * VT_ROWS, block), lambda b, j: (b, j, 0, 0)),
                  pl.BlockSpec((1, N_GATES, block), lambda b, j: (b, 0, j)), state, mspec],
        out_specs=[tok, state, mspec],
        out_shape=[jax.ShapeDtypeStruct((nb, s, GROUP_W), _BF16), jax.ShapeDtypeStruct(st0.shape, _F32),
                   jax.ShapeDtypeStruct(m0.shape, _F32)],
        scratch_shapes=[pltpu.VMEM((ML_HEADS, VT_ROWS, ML_DK), _F32), pltpu.VMEM((ML_HEADS, SUBLANES, HEAD_W), _F32)],
        compiler_params=_cparams(
            ("parallel", "arbitrary"),
            3 * [_nbytes((block, GROUP_W), _BF16)] + [_nbytes((ML_HEADS * VT_ROWS, block), _BF16)]
            + 2 * [_nbytes(st0.shape[1:], _F32)], 16 * _nbytes((VT_ROWS, 2 * CHUNK), _F32) * ML_HEADS), name="mlstm_pairs",
    )(k, qt, vt, gr, st0, m0)


def _mix_kernel(da_ref, ml_ref, og_ref, x_ref, subln_ref, wo_ref, g2_ref, wr_ref, br_ref, cnt0_ref,
                xmid_ref, xp_ref, idx_ref, gate_ref, rank_ref, cnt_ref, cnt_sc, *, tm, lam_init):
    step = pl.program_id(0)

    @pl.when(step == 0)
    def _():
        cnt_sc[...] = cnt0_ref[...]

    parts = []
    for h in range(DA_HEADS):
        d = da_ref[:, h * HEAD_W:(h + 1) * HEAD_W].astype(_F32)
        d = d * lax.rsqrt(jnp.mean(d * d, axis=-1, keepdims=True) + EPS) * subln_ref[...]
        parts.append((d * (1.0 - lam_init)).astype(_BF16))
    parts.append((og_ref[...].astype(_F32) * ml_ref[...].astype(_F32)).astype(_BF16))
    xm = x_ref[...] + _dot(jnp.concatenate(parts, axis=1), wo_ref[...])
    xmid_ref[...] = xm
    xn_f = xm * lax.rsqrt(jnp.mean(xm * xm, axis=-1, keepdims=True) + EPS) * g2_ref[...]
    xp_ref[...] = _pack_rows(xn_f)
    xn = xn_f.astype(_BF16)

    logits = _dot(wr_ref[...], xn, _NT) + br_ref[...]
    e_i = lax.broadcasted_iota(jnp.int32, logits.shape, 0)
    member = jnp.zeros(logits.shape, jnp.bool_)
    vals, idxs = [], []
    for _ in range(TOP_K):
        mx = jnp.max(logits, axis=0, keepdims=True)
        sel = jnp.min(jnp.where(logits == mx, e_i, N_EXPERTS), axis=0, keepdims=True)
        hit = e_i == sel
        member = member | hit
        logits = jnp.where(hit, -jnp.inf, logits)
        vals.append(mx)
        idxs.append(sel)
    ex = [jnp.exp(v - vals[0]) for v in vals]
    tot = ex[0] + ex[1] + ex[2] + ex[3]
    idx_ref[...] = jnp.concatenate(idxs, axis=0)
    gate_ref[...] = jnp.concatenate([e / tot for e in ex], axis=0)

    upper = (lax.broadcasted_iota(jnp.int32, (tm, tm), 0) < lax.broadcasted_iota(jnp.int32, (tm, tm), 1))
    memf = member.astype(_F32)
    before = _dot(memf.astype(_BF16), upper.astype(_BF16)) + cnt_sc[...]
    rank_ref[...] = jnp.concatenate(
        [jnp.sum(jnp.where(e_i == s, before, 0.0), axis=0, keepdims=True) for s in idxs], axis=0).astype(jnp.int32)
    cnt_sc[...] += jnp.sum(memf, axis=1, keepdims=True)
    cnt_ref[...] = cnt_sc[...]


def _mix(da, ml, og, x, subln, w_out, norm2, w_rt, b_r, cnt0, *, tm, lam_init):
    t = x.shape[0]
    tok = lambda w: pl.BlockSpec((tm, w), lambda i: (i, 0))
    const = lambda shape: pl.BlockSpec(shape, lambda i: (0,) * len(shape))
    lane_tok = pl.BlockSpec((TOP_K, tm), lambda i: (0, i))
    return pl.pallas_call(
        functools.partial(_mix_kernel, tm=tm, lam_init=lam_init),
        grid=(t // tm,),
        in_specs=[tok(GROUP_W), tok(GROUP_W), tok(GROUP_W), tok(D_MODEL), const((1, DA_V)),
                  const((D_MODEL, D_MODEL)), const((1, D_MODEL)), const((N_EXPERTS, D_MODEL)),
                  const((N_EXPERTS, 1)), const((N_EXPERTS, 1))],
        out_specs=[tok(D_MODEL), tok(PACK_W), lane_tok, lane_tok, lane_tok, const((N_EXPERTS, 1))],
        out_shape=[jax.ShapeDtypeStruct((t, D_MODEL), _F32), jax.ShapeDtypeStruct((t, PACK_W), jnp.int32),
                   jax.ShapeDtypeStruct((TOP_K, t), jnp.int32), jax.ShapeDtypeStruct((TOP_K, t), _F32),
                   jax.ShapeDtypeStruct((TOP_K, t), jnp.int32), jax.ShapeDtypeStruct((N_EXPERTS, 1), _F32)],
        scratch_shapes=[pltpu.VMEM((N_EXPERTS, 1), _F32)],
        compiler_params=_cparams(
            ("arbitrary",),
            3 * [_nbytes((tm, GROUP_W), _BF16)] + 2 * [_nbytes((tm, D_MODEL), _F32)] + [_nbytes((tm, PACK_W), _F32)]
            + [_nbytes((D_MODEL, D_MODEL), _BF16)], 4 * _nbytes((tm, D_MODEL), _F32) + 2 * _nbytes((tm, tm), _F32)), name="mix",
    )(da, ml, og, x, subln, w_out, norm2, w_rt, b_r, cnt0)


def _moe_kernel(be_ref, nu_ref, nv_ref, x_ref, wgu_ref, bgu_ref, wd_ref, bd_ref, y_ref, wgu_sc, wd_sc):
    j = pl.program_id(0)
    e_prev = be_ref[jnp.maximum(j - 1, 0)]
    fresh = (j == 0) | (be_ref[j] != e_prev)

    @pl.when(fresh & (j < nu_ref[0]))
    def _():
        rows = CAST_ROWS
        def cast_gu(r, _):
            sl = pl.ds(pl.multiple_of(r * rows, rows), rows)
            wgu_sc[sl, :] = wgu_ref[0, sl, :].astype(_BF16)
            return 0
        lax.fori_loop(0, D_MODEL // rows, cast_gu, 0)
        def cast_d(r, _):
            sl = pl.ds(pl.multiple_of(r * rows, rows), rows)
            wd_sc[sl, :] = wd_ref[0, sl, :].astype(_BF16)
            return 0
        lax.fori_loop(0, D_FF // rows, cast_d, 0)

    @pl.when(j >= nu_ref[0])
    def _():
        y_ref[...] = jnp.zeros_like(y_ref)

    def ffn(row0):
        hi, lo = _unpack_rows(x_ref[row0:, :])
        x = jnp.concatenate([hi.astype(_BF16), lo.astype(_BF16)], axis=1)
        n_chunks = D_FF // FF_CHUNK

        def gate_up(c):
            lo = c * FF_CHUNK
            return (_dot(x, wgu_sc[:, lo:lo + FF_CHUNK]), _dot(x, wgu_sc[:, D_FF + lo:D_FF + lo + FF_CHUNK]))

        acc = None
        nxt = gate_up(0)
        for c in range(n_chunks):
            lo = c * FF_CHUNK
            gate, up = nxt
            if c + 1 < n_chunks:
                nxt = gate_up(c + 1)
            gate = jnp.minimum(gate + bgu_ref[0, :, lo:lo + FF_CHUNK], SWIGLU_LIMIT)
            up = jnp.clip(up + bgu_ref[0, :, D_FF + lo:D_FF + lo + FF_CHUNK], -SWIGLU_LIMIT, SWIGLU_LIMIT)
            act = (up + 1.0) * (gate * jax.nn.sigmoid(SWIGLU_ALPHA * gate))
            down = _dot(act.astype(_BF16), wd_sc[lo:lo + FF_CHUNK, :])
            acc = down if acc is None else acc + down
        y_ref[row0:, :] = _pack_rows(acc + bd_ref[0])

    half = MOE_BLOCK // 2
    used = j < nu_ref[0]

    @pl.when(used & (nv_ref[j] > half))
    def _():
        ffn(0)

    @pl.when(used & (nv_ref[j] <= half))
    def _():
        y_ref[:half, :] = jnp.zeros((half, PACK_W), jnp.int32)
        ffn(half)


def _moe(blk_e, n_used, blk_rows, xb, w_gu, b_gu, w_d, b_d):
    n_blocks = xb.shape[0] // MOE_BLOCK
    last = lambda j, nu: jnp.minimum(j, nu[0] - 1)
    row_spec = pl.BlockSpec((MOE_BLOCK, PACK_W), lambda j, be, nu, nv: (last(j, nu), 0))
    exp_spec = lambda r, c: pl.BlockSpec((1, r, c), lambda j, be, nu, nv: (be[last(j, nu)], 0, 0))
    grid_spec = pltpu.PrefetchScalarGridSpec(
        num_scalar_prefetch=3, grid=(n_blocks,),
        in_specs=[row_spec, exp_spec(D_MODEL, 2 * D_FF), exp_spec(1, 2 * D_FF), exp_spec(D_FF, D_MODEL),
                  exp_spec(1, D_MODEL)],
        out_specs=pl.BlockSpec((MOE_BLOCK, PACK_W), lambda j, be, nu, nv: (j, 0)),
        scratch_shapes=[pltpu.VMEM((D_MODEL, 2 * D_FF), _BF16), pltpu.VMEM((D_FF, D_MODEL), _BF16)])
    return pl.pallas_call(
        _moe_kernel, grid_spec=grid_spec, out_shape=jax.ShapeDtypeStruct(xb.shape, jnp.int32),
        compiler_params=_cparams(
            ("arbitrary",),
            2 * [_nbytes((MOE_BLOCK, PACK_W), _F32)] + [_nbytes((D_MODEL, 2 * D_FF), _F32), _nbytes((D_FF, D_MODEL), _F32)],
            _nbytes((D_MODEL, 3 * D_FF), _BF16) + 4 * _nbytes((MOE_BLOCK, D_MODEL), _F32)), name="moe",
    )(blk_e, n_used, blk_rows, xb, w_gu, b_gu, w_d, b_d)


def _sc_mesh():
    return plsc.VectorSubcoreMesh(core_axis_name="c", subcore_axis_name="s")


def _sc_worker():
    return lax.axis_index("s") * SC_CORES + lax.axis_index("c")


def _sc_dispatch(sources, n_rows):
    n_src = len(sources)
    shapes = []
    for _, dest3 in sources:
        n_chunks, _, chunk = dest3.shape
        per_worker = n_chunks // SC_WORKERS
        assert per_worker * SC_WORKERS == n_chunks and chunk % SUBLANES == 0
        shapes.append((per_worker, chunk))

    def body(*refs):
        x_hbms, d_hbms, o_hbm = refs[:n_src], refs[n_src:2 * n_src], refs[2 * n_src]
        scratch = refs[2 * n_src + 1:]
        load_a, load_b, scat_a, scat_b = scratch[3 * n_src:]
        worker = _sc_worker()
        for i, (per_worker, chunk) in enumerate(shapes):
            x_hbm, idx_v, rows_a, rows_b = x_hbms[i], scratch[3 * i], scratch[3 * i + 1], scratch[3 * i + 2]
            pltpu.sync_copy(d_hbms[i].at[worker], idx_v)

            def load(ci, rows, sem):
                row0 = (worker * per_worker + ci) * chunk
                return pltpu.make_async_copy(x_hbm.at[pl.ds(pl.multiple_of(row0, SUBLANES), chunk)], rows, sem)

            def scatters(ci, rows, sem):
                return [pltpu.make_async_copy(rows, o_hbm.at[idx_v.at[ci * TOP_K + k]], sem) for k in range(TOP_K)]

            def start(copies):
                for cp in copies:
                    cp.start()

            def wait(copies):
                for cp in copies:
                    cp.wait()

            load(0, rows_a, load_a).start()

            @pl.loop(0, per_worker // 2)
            def _(p):
                ca, cb = 2 * p, 2 * p + 1
                load(ca, rows_a, load_a).wait()
                load(cb, rows_b, load_b).start()
                start(scatters(ca, rows_a, scat_a))
                load(cb, rows_b, load_b).wait()
                wait(scatters(ca, rows_a, scat_a))
                load(jnp.minimum(ca + 2, per_worker - 1), rows_a, load_a).start()
                start(scatters(cb, rows_b, scat_b))
                wait(scatters(cb, rows_b, scat_b))

            last = per_worker - 1
            load(last, rows_a, load_a).wait()
            if per_worker % 2:
                start(scatters(last, rows_a, scat_a))
                wait(scatters(last, rows_a, scat_a))

    dma = pltpu.SemaphoreType.DMA
    scratch_types = []
    for per_worker, chunk in shapes:
        scratch_types += [pltpu.VMEM((per_worker * TOP_K, chunk), jnp.int32), pltpu.VMEM((chunk, PACK_W), jnp.int32),
                          pltpu.VMEM((chunk, PACK_W), jnp.int32)]
    tables = [dest3.reshape(SC_WORKERS, pw * TOP_K, chunk) for (_, dest3), (pw, chunk) in zip(sources, shapes)]
    return pl.kernel(
        body, out_type=jax.ShapeDtypeStruct((n_rows, PACK_W), jnp.int32), mesh=_sc_mesh(),
        scratch_types=scratch_types + [dma, dma, dma, dma],
        name="sc_dispatch")(*[xp for xp, _ in sources], *tables)


def _sc_gather(table, idx2):
    n_chunks, chunk = idx2.shape
    per_worker = n_chunks // SC_WORKERS
    assert per_worker * SC_WORKERS == n_chunks and chunk % SUBLANES == 0

    def body(t_hbm, i_hbm, o_hbm, idx_v, rows_a, rows_b, gath_a, gath_b, put_a, put_b):
        worker = _sc_worker()
        pltpu.sync_copy(i_hbm.at[worker], idx_v)

        def gather(ci, rows, sem):
            return pltpu.make_async_copy(t_hbm.at[idx_v.at[ci]], rows, sem)

        def put(ci, rows, sem):
            row0 = (worker * per_worker + ci) * chunk
            return pltpu.make_async_copy(rows, o_hbm.at[pl.ds(pl.multiple_of(row0, SUBLANES), chunk)], sem)

        gather(0, rows_a, gath_a).start()

        @pl.loop(0, per_worker // 2)
        def _(p):
            ca, cb = 2 * p, 2 * p + 1
            gather(ca, rows_a, gath_a).wait()
            gather(cb, rows_b, gath_b).start()
            put(ca, rows_a, put_a).start()
            gather(cb, rows_b, gath_b).wait()
            put(ca, rows_a, put_a).wait()
            gather(jnp.minimum(ca + 2, per_worker - 1), rows_a, gath_a).start()
            put(cb, rows_b, put_b).start()
            put(cb, rows_b, put_b).wait()

        last = per_worker - 1
        gather(last, rows_a, gath_a).wait()
        if per_worker % 2:
            put(last, rows_a, put_a).start()
            put(last, rows_a, put_a).wait()

    dma = pltpu.SemaphoreType.DMA
    return pl.kernel(
        body, out_type=jax.ShapeDtypeStruct((n_chunks * chunk, PACK_W), jnp.int32), mesh=_sc_mesh(),
        scratch_types=[pltpu.VMEM((per_worker, chunk), jnp.int32), pltpu.VMEM((chunk, PACK_W), jnp.int32),
                       pltpu.VMEM((chunk, PACK_W), jnp.int32), dma, dma, dma, dma],
        name="sc_gather")(table, idx2.reshape(SC_WORKERS, per_worker, chunk))


def _final_kernel(x_ref, y_ref, g_ref, nf_ref, o_ref):
    g_rows = g_ref[...]
    g = jnp.concatenate([g_rows, jnp.zeros((SUBLANES - TOP_K, g_rows.shape[1]), _F32)], axis=0).T
    hi = jnp.zeros((x_ref.shape[0], PACK_W), _F32)
    lo = jnp.zeros((x_ref.shape[0], PACK_W), _F32)
    for k in range(TOP_K):
        y_hi, y_lo = _unpack_rows(y_ref[k])
        hi = hi + g[:, k:k + 1] * y_hi
        lo = lo + g[:, k:k + 1] * y_lo
    x = x_ref[...] + jnp.concatenate([hi, lo], axis=1)
    o_ref[...] = x * lax.rsqrt(jnp.mean(x * x, axis=-1, keepdims=True) + EPS) * nf_ref[...]


def _final(xmid, yg, gates, norm_f, *, tm, first_block):
    t = xmid.shape[0]
    return pl.pallas_call(
        _final_kernel, grid=(t // tm,),
        in_specs=[pl.BlockSpec((tm, D_MODEL), lambda i: (i, 0)),
                  pl.BlockSpec((TOP_K, tm, PACK_W), lambda i: (0, i + first_block, 0)),
                  pl.BlockSpec((TOP_K, tm), lambda i: (0, i)),
                  pl.BlockSpec((1, D_MODEL), lambda i: (0, 0))],
        out_specs=pl.BlockSpec((tm, D_MODEL), lambda i: (i, 0)),
        out_shape=jax.ShapeDtypeStruct((t, D_MODEL), _F32),
        compiler_params=_cparams(
            ("parallel",), 2 * [_nbytes((tm, D_MODEL), _F32)] + [_nbytes((TOP_K, tm, PACK_W), _F32)],
            2 * _nbytes((tm, D_MODEL), _F32)), name="final",
    )(xmid, yg, gates, norm_f)


def kernel(x_prompt, x_sample, cache_k, cache_v, state_C, state_n, state_m, norm1, w_in, b_igate, b_fgate,
           lambda_q1, lambda_k1, lambda_q2, lambda_k2, subln, w_out, norm2, w_router, b_router, w_gate_up,
           b_gate_up, w_down, b_down, norm_f):
    nb, s, _ = x_prompt.shape
    db, ls, _ = x_sample.shape
    past = cache_k.shape[2]
    depth = w_in.shape[0]
    assert depth == 1 and s % PROJ_TILE == 0 and (db * ls) % SUBLANES == 0
    l = 0
    lam_init = 0.8 - 0.6 * math.exp(-0.3 * l)
    t_p, t_s = nb * s, db * ls

    w_nat = jnp.pad(w_in[l].T, ((0, HEAD_W - N_GATES), (0, 0))).astype(_BF16)
    b_gates = jnp.pad(jnp.concatenate([b_igate[l], b_fgate[l]]).astype(_F32), (0, HEAD_W - N_GATES))[None]
    inv = ROPE_THETA ** (-jnp.arange(0, DA_QK, 2, dtype=_F32) / DA_QK)
    inv_full = jnp.tile(inv, HEAD_W // (DA_QK // 2))[None]
    lamv = jnp.stack([lambda_q1[l], lambda_k1[l], lambda_q2[l], lambda_k2[l]]).astype(_F32)
    g1 = norm1[l][None]

    pp = _proj(x_prompt, g1, w_nat, b_gates, inv_full, tm=PROJ_TILE, pos_base=0, pos_mod=s,
               names=("q", "kf", "kb", "vf", "vt", "mk", "og", "gr", "mqt", "mvt"))
    ps = _proj(x_sample.reshape(1, t_s, D_MODEL), g1, w_nat, b_gates, inv_full, tm=t_s, pos_base=past, pos_mod=ls,
               names=("q", "kf", "kb", "vf", "mq", "mk", "mv", "og", "gc", "gr"))
    kf_p, vf_p, kf_s, vf_s = pp["kf"], pp["vf"], ps["kf"], ps["vf"]

    da_p = _attn_prompt(lamv, pp["q"], pp["kb"], pp["vt"], lam_init=lam_init)
    per_b = lambda a: a.reshape(db, ls, a.shape[-1])
    kct = jnp.transpose(cache_k[l], (0, 2, 3, 4, 1)).reshape(db, GROUP_W, past)
    vc = cache_v[l].reshape(db, past * DA_HEADS, DA_V)
    vn = vf_s.reshape(db, ls, GROUP_W).astype(_BF16)
    da_s = _attn_sample(lamv, per_b(ps["q"]), per_b(ps["kb"]), vn, kct, vc, lam_init=lam_init)

    assert PROJ_TILE % (2 * CHUNK) == 0
    h_p, st_p, m_p = _mlstm_pairs(pp["mk"], pp["mqt"], pp["mvt"], pp["gr"],
                                  jnp.zeros((nb, ML_HEADS, VT_ROWS, ML_DK), _F32),
                                  jnp.zeros((nb, ML_HEADS, SUBLANES, HEAD_W), _F32))
    gr_sb = ps["gr"].reshape(8, db, ls).transpose(1, 0, 2)
    m0_s = jnp.broadcast_to(state_m[l].astype(_F32)[:, :, None, None], (db, ML_HEADS, SUBLANES, HEAD_W))
    h_s, c_s, n_s, m_s = _mlstm(per_b(ps["mq"]), per_b(ps["mk"]), per_b(ps["mv"]), per_b(ps["gc"]), gr_sb,
                                state_C[l].astype(_F32), state_n[l].astype(_F32), m0_s, chunk=ls, block=ls)
    c_p, n_p, mm_p = jnp.swapaxes(st_p[:, :, :ML_DV, :], -1, -2), st_p[:, :, ML_DV, :], m_p[:, :, 0, 0]
    mm_s = m_s[:, :, 0, 0]

    wo = w_out[l].astype(_BF16)
    w_rt = w_router[l].T.astype(_BF16)
    b_r = b_router[l].astype(_F32)[:, None]
    mix = functools.partial(_mix, subln=subln[l][None], w_out=wo, norm2=norm2[l][None], w_rt=w_rt, b_r=b_r,
                            lam_init=lam_init)
    flat = lambda a: a.reshape(-1, a.shape[-1])
    t_all = t_p + t_s
    assert t_p % t_s == 0
    xm_p, xp_p, idx_p, gate_p, rank_p, cnt_p = mix(
        flat(da_p), flat(h_p), flat(pp["og"]), flat(x_prompt), cnt0=jnp.zeros((N_EXPERTS, 1), _F32), tm=PROJ_TILE)
    xm_s, xp_s, idx_s, gate_s, rank_s, cnt = mix(
        flat(da_s), flat(h_s), flat(ps["og"]), flat(x_sample), cnt0=cnt_p, tm=t_s)

    rows = t_all * TOP_K
    n_blocks = -(-rows // MOE_BLOCK) + N_EXPERTS
    counts = cnt[:, 0].astype(jnp.int32)
    padded = (counts + MOE_BLOCK - 1) // MOE_BLOCK * MOE_BLOCK
    pend = jnp.cumsum(padded)
    pstart = pend - counts
    experts = jnp.arange(N_EXPERTS, dtype=jnp.int32)[:, None, None]
    first_row = lambda idx: jnp.sum(jnp.where(idx[None] == experts, pstart[:, None, None], 0), axis=0)
    dest_p, dest_s = first_row(idx_p) + rank_p, first_row(idx_s) + rank_s
    n_used = (pend[-1] // MOE_BLOCK).astype(jnp.int32)[None]
    blk_lo = jnp.arange(n_blocks, dtype=jnp.int32) * MOE_BLOCK
    blk_e = jnp.minimum(jnp.sum((pend[None, :] <= blk_lo[:, None]).astype(jnp.int32), axis=1), N_EXPERTS - 1)
    row_lo = jnp.sum(jnp.where(blk_e[:, None] == experts[:, 0].T, pstart[None, :], 0), axis=1)
    blk_rows = jnp.where(blk_lo < pend[-1], jnp.clip(blk_lo + MOE_BLOCK - row_lo, 0, MOE_BLOCK), 0).astype(jnp.int32)

    def chunked(dest):
        chunk = min(DISPATCH_CHUNK, dest.shape[1] // SC_WORKERS)
        return dest.reshape(TOP_K, dest.shape[1] // chunk, chunk).transpose(1, 0, 2)

    assert rows % (SC_WORKERS * COMBINE_CHUNK) == 0
    xb = _sc_dispatch([(xp_p, chunked(dest_p)), (xp_s, chunked(dest_s))], n_blocks * MOE_BLOCK)
    dest = jnp.concatenate([dest_p, dest_s], axis=1)
    yb = _moe(blk_e, n_used, blk_rows, xb, w_gate_up[l], b_gate_up[l][:, None, :], w_down[l], b_down[l][:, None, :])
    yg = _sc_gather(yb, dest.reshape(rows // COMBINE_CHUNK, COMBINE_CHUNK)).reshape(TOP_K, t_all, PACK_W)
    nf = norm_f[None]
    y_p = _final(xm_p, yg, gate_p, nf, tm=FINAL_TILE, first_block=0).reshape(nb, s, D_MODEL)
    y_s = _final(xm_s, yg, gate_s, nf, tm=t_s, first_block=t_p // t_s).reshape(db, ls, D_MODEL)

    st = lambda a: a[None]
    return (y_p, y_s,
            st(kf_p.reshape(nb, s, DA_HEADS, 2, DA_QK)), st(vf_p.reshape(nb, s, DA_HEADS, DA_V)),
            st(c_p), st(n_p), st(mm_p),
            st(kf_s.reshape(db, ls, DA_HEADS, 2, DA_QK)), st(vf_s.reshape(db, ls, DA_HEADS, DA_V)),
            st(c_s.astype(state_C.dtype)), st(n_s.astype(state_n.dtype)), st(mm_s.astype(state_m.dtype)))
```

```python
import functools
import math

import jax
import jax.numpy as jnp
from jax import lax
from jax.experimental import pallas as pl
from jax.experimental.pallas import tpu as pltpu
from jax.experimental.pallas import tpu_sc as plsc

D_MODEL = 1024
CHUNK = 64
DA_HEADS = 4
DA_QK = 64
DA_V = 128
ML_HEADS = 4
ML_DK = 128
ML_DV = 128
HEAD_W = 128
SUBLANES = 8
N_GATES = 2 * ML_HEADS
CAST_ROWS = 128
GROUP_W = 512
ROPE_THETA = 10000.0
N_EXPERTS = 32
TOP_K = 4
D_FF = 1024
SWIGLU_LIMIT = 7.0
SWIGLU_ALPHA = 1.702
EPS = 1e-6
NEG_BIG = -1e30

LOG2E = math.log2(math.e)
VT_ROWS = 144
PROJ_TILE = 512
FINAL_TILE = 1024
ATTN_TQ = 512
MOE_BLOCK = 512
FF_CHUNK = 512
PACK_W = D_MODEL // 2
SC_CORES = 2
SC_SUBCORES = 16
SC_WORKERS = SC_CORES * SC_SUBCORES
DISPATCH_CHUNK = 64
COMBINE_CHUNK = 80
VMEM_BYTES = 64 * 1024 * 1024

_F32 = jnp.float32
_BF16 = jnp.bfloat16


def _nbytes(shape, dtype):
    return math.prod(shape) * jnp.dtype(dtype).itemsize


def _cparams(sem, blocks, resident=0):
    need = 2 * sum(blocks) + resident
    assert need <= VMEM_BYTES, need
    return pltpu.CompilerParams(dimension_semantics=sem, vmem_limit_bytes=need)


def _dot(a, b, dims=(((1,), (0,)), ((), ())), precision=None):
    return lax.dot_general(a, b, dims, precision=precision, preferred_element_type=_F32)


_NT = (((1,), (1,)), ((), ()))
_TN = (((0,), (0,)), ((), ()))


def _pack_rows(x):
    half = x.shape[1] // 2
    bits = lambda v: lax.bitcast_convert_type(v.astype(_BF16).astype(_F32), jnp.int32)
    return bits(x[:, :half]) | lax.shift_right_logical(bits(x[:, half:]), 16)


def _unpack_rows(p):
    hi = lax.bitcast_convert_type(p & jnp.int32(-65536), _F32)
    lo = lax.bitcast_convert_type(lax.shift_left(p, 16), _F32)
    return hi, lo


def _proj_kernel(x_ref, g_ref, wt_ref, bg_ref, inv_ref, *out_refs, names, tm, pos_base, pos_mod):
    out = dict(zip(names, out_refs))
    x = x_ref[0]
    xn = (x * lax.rsqrt(jnp.mean(x * x, axis=-1, keepdims=True) + EPS) * g_ref[...]).astype(_BF16)

    row = pl.program_id(1) * tm + lax.broadcasted_iota(jnp.int32, (tm, 1), 0)
    pos = (pos_base + row % pos_mod).astype(_F32)
    ang = pos * inv_ref[...]
    cos = jnp.cos(ang)
    sin = jnp.sin(ang)
    lane = lax.broadcasted_iota(jnp.int32, (1, HEAD_W), 1)
    first = (lane % DA_QK) < (DA_QK // 2)
    sin = jnp.where(first, -sin, sin)

    def rope(z):
        rot = jnp.where(first, pltpu.roll(z, HEAD_W - DA_QK // 2, 1), pltpu.roll(z, DA_QK // 2, 1))
        return z * cos + rot * sin

    def group(c, width=GROUP_W):
        return _dot(xn, wt_ref[c * GROUP_W:c * GROUP_W + width, :], _NT)

    def put_q(zq):
        for h in range(DA_HEADS):
            sl = slice(h * HEAD_W, (h + 1) * HEAD_W)
            out["q"][0, :, sl] = (rope(zq[:, sl]) * (DA_QK ** -0.5 * LOG2E)).astype(_BF16)

    def put_k(zk):
        for h in range(DA_HEADS):
            sl = slice(h * HEAD_W, (h + 1) * HEAD_W)
            rk = rope(zk[:, sl])
            out["kf"][0, :, sl] = rk
            out["kb"][0, :, sl] = rk.astype(_BF16)

    def with_ones_rows(zt):
        ones = jnp.ones((VT_ROWS - DA_V, tm), _BF16)
        return jnp.concatenate(
            [part for h in range(DA_HEADS) for part in (zt[h * DA_V:(h + 1) * DA_V], ones)], axis=0)

    def put_v(zv):
        for h in range(DA_HEADS):
            out["vf"][0, pl.ds(h, tm, stride=DA_HEADS), :] = zv[:, h * DA_V:(h + 1) * DA_V]
        if "vt" in out:
            out["vt"][0, 0] = with_ones_rows(zv.T.astype(_BF16))

    def put_mq(z):
        if "mq" in out:
            out["mq"][0] = z.astype(_BF16)
        if "mqt" in out:
            out["mqt"][0, 0] = z.T.astype(_BF16)

    def put_mk(z):
        out["mk"][0] = (z * (ML_DK ** -0.5)).astype(_BF16)

    def put_mv(z):
        if "mv" in out:
            out["mv"][0] = z.astype(_BF16)
        if "mvt" in out:
            out["mvt"][0, 0] = with_ones_rows(z.T.astype(_BF16))

    def put_mo(z):
        out["og"][0] = jax.nn.sigmoid(z).astype(_BF16)

    def put_gates(z):
        zg = z + bg_ref[...]
        lane_g = lax.broadcasted_iota(jnp.int32, zg.shape, 1)
        logsig = jnp.minimum(zg, 0.0) - jnp.log(1.0 + jnp.exp(-jnp.abs(zg)))
        gates = jnp.where(lane_g < ML_HEADS, zg, logsig)
        if "gc" in out:
            out["gc"][0] = gates[:, :N_GATES]
        out["gr"][0] = gates.T[:N_GATES, :]

    sinks = [(put_q, GROUP_W), (put_k, GROUP_W), (put_v, GROUP_W), (put_mq, GROUP_W), (put_mk, GROUP_W),
             (put_mv, GROUP_W), (put_mo, GROUP_W), (put_gates, HEAD_W)]
    nxt = group(0, sinks[0][1])
    for c, (sink, _) in enumerate(sinks):
        cur = nxt
        if c + 1 < len(sinks):
            nxt = group(c + 1, sinks[c + 1][1])
        sink(cur)


def _proj(x3, norm1, wt, b_gates, inv_full, *, names, tm, pos_base, pos_mod):
    nb, s, _ = x3.shape
    nt = s // tm
    tok = lambda w, dt: (jax.ShapeDtypeStruct((nb, s, w), dt), pl.BlockSpec((1, tm, w), lambda b, i: (b, i, 0)))
    slab = lambda rows: (jax.ShapeDtypeStruct((nb, nt, rows, tm), _BF16),
                         pl.BlockSpec((1, 1, rows, tm), lambda b, i: (b, i, 0, 0)))
    const = lambda shape: pl.BlockSpec(shape, lambda b, i: (0,) * len(shape))
    kinds = {
        "q": tok(GROUP_W, _BF16), "kf": tok(GROUP_W, _F32), "kb": tok(GROUP_W, _BF16),
        "vf": (jax.ShapeDtypeStruct((nb, s * DA_HEADS, DA_V), _F32),
               pl.BlockSpec((1, tm * DA_HEADS, DA_V), lambda b, i: (b, i, 0))),
        "vt": slab(DA_HEADS * VT_ROWS),
        "mq": tok(GROUP_W, _BF16), "mk": tok(GROUP_W, _BF16), "mv": tok(GROUP_W, _BF16), "og": tok(GROUP_W, _BF16),
        "gc": tok(N_GATES, _F32),
        "gr": (jax.ShapeDtypeStruct((nb, N_GATES, s), _F32), pl.BlockSpec((1, N_GATES, tm), lambda b, i: (b, 0, i))),
        "mqt": slab(GROUP_W), "mvt": slab(ML_HEADS * VT_ROWS),
    }
    blocks = [_nbytes((tm, D_MODEL), _F32), _nbytes(wt.shape, _BF16)]
    blocks += [_nbytes(kinds[n][1].block_shape, kinds[n][0].dtype) for n in names]
    outs = pl.pallas_call(
        functools.partial(_proj_kernel, names=names, tm=tm, pos_base=pos_base, pos_mod=pos_mod),
        grid=(nb, nt),
        in_specs=[pl.BlockSpec((1, tm, D_MODEL), lambda b, i: (b, i, 0)), const((1, D_MODEL)),
                  const(wt.shape), const((1, HEAD_W)), const((1, HEAD_W))],
        out_specs=[kinds[n][1] for n in names], out_shape=[kinds[n][0] for n in names],
        compiler_params=_cparams(("parallel", "parallel"), blocks, 8 * _nbytes((tm, GROUP_W), _F32)), name="proj",
    )(x3, norm1, wt, b_gates, inv_full)
    return dict(zip(names, outs))


def _lambda(lam_ref, lam_init):
    lv = lam_ref[...]
    s1 = jnp.sum(lv[0:1] * lv[1:2], axis=-1, keepdims=True)
    s2 = jnp.sum(lv[2:3] * lv[3:4], axis=-1, keepdims=True)
    return jnp.exp(s1) - jnp.exp(s2) + lam_init


def _split_components(q):
    lane = lax.broadcasted_iota(jnp.int32, q.shape, 1)
    zero = jnp.zeros_like(q)
    return jnp.concatenate([jnp.where(lane < DA_QK, q, zero), jnp.where(lane >= DA_QK, q, zero)], axis=0)


def _attn_kernel(lam_ref, q_ref, k_ref, vt_ref, o_ref, acc_ref, s_ref, *, tq, tk, lam_init):
    i = pl.program_id(1)
    heads = range(DA_HEADS)
    qz = [_split_components(q_ref[0, :, h * HEAD_W:(h + 1) * HEAD_W]) for h in heads]
    acc_ref[...] = jnp.zeros_like(acc_ref)

    def scores(h, j):
        k_t = k_ref[0, pl.ds(pl.multiple_of(j * tk, tk), tk), h * HEAD_W:(h + 1) * HEAD_W]
        return _dot(k_t, qz[h], _NT)

    def step(j, ms, last):
        if last:
            kpos = j * tk + lax.broadcasted_iota(jnp.int32, (tk, 1), 0)
            qpos = i * tq + lax.broadcasted_iota(jnp.int32, (1, 2 * tq), 1) % tq
            visible = kpos < (qpos // CHUNK + 1) * CHUNK
        out = []
        s = s_ref[...]
        for h in heads:
            s_next = None
            if h + 1 < DA_HEADS:
                s_next = scores(h + 1, j)
            elif not last:
                s_next = scores(0, j + 1)
            if last:
                s = jnp.where(visible, s, NEG_BIG)
            m_new = jnp.maximum(ms[h], jnp.max(s, axis=0, keepdims=True))
            alpha = jnp.exp2(ms[h] - m_new)
            p = jnp.exp2(s - m_new).astype(_BF16)
            acc_ref[h] = alpha * acc_ref[h] + _dot(vt_ref[0, j, h * VT_ROWS:(h + 1) * VT_ROWS, :], p)
            out.append(m_new)
            s = s_next
        if not last:
            s_ref[...] = s
        return tuple(out)

    n_full = (i * tq) // tk
    s_ref[...] = scores(0, 0)
    init = tuple(jnp.full((1, 2 * tq), NEG_BIG, _F32) for _ in heads)
    ms = lax.fori_loop(0, n_full, lambda j, c: step(j, c, False), init)
    step(n_full, ms, True)

    lam = _lambda(lam_ref, lam_init)
    for h in heads:
        acc = acc_ref[h]
        o = acc[:DA_V] / acc[DA_V:DA_V + 1]
        o_ref[0, :, h * HEAD_W:(h + 1) * HEAD_W] = (o[:, :tq] - lam * o[:, tq:]).T.astype(_BF16)


def _attn_prompt(lamv, q, kb, vt, *, lam_init):
    nb, s, _ = q.shape
    nk, tk = vt.shape[1], vt.shape[3]
    tq = ATTN_TQ
    assert tk % tq == 0
    return pl.pallas_call(
        functools.partial(_attn_kernel, tq=tq, tk=tk, lam_init=lam_init),
        grid=(nb, s // tq),
        in_specs=[pl.BlockSpec((4, DA_QK), lambda b, i: (0, 0)),
                  pl.BlockSpec((1, tq, GROUP_W), lambda b, i: (b, i, 0)),
                  pl.BlockSpec((1, s, GROUP_W), lambda b, i: (b, 0, 0)),
                  pl.BlockSpec((1, nk, DA_HEADS * VT_ROWS, tk), lambda b, i: (b, 0, 0, 0))],
        out_specs=pl.BlockSpec((1, tq, GROUP_W), lambda b, i: (b, i, 0)),
        out_shape=jax.ShapeDtypeStruct((nb, s, GROUP_W), _BF16),
        scratch_shapes=[pltpu.VMEM((DA_HEADS, VT_ROWS, 2 * tq), _F32), pltpu.VMEM((tk, 2 * tq), _F32)],
        compiler_params=_cparams(
            ("parallel", "arbitrary"),
            [_nbytes((tq, GROUP_W), _BF16), _nbytes((s, GROUP_W), _BF16), _nbytes(vt.shape[1:], _BF16),
             _nbytes((tq, GROUP_W), _BF16)],
            _nbytes((DA_HEADS, VT_ROWS, 2 * tq), _F32) + 5 * _nbytes((tk, 2 * tq), _F32)), name="attn",
    )(lamv, q, kb, vt)


def _attn_sample_kernel(lam_ref, q_ref, kn_ref, vn_ref, kc_ref, vc_ref, o_ref, *, lq, past, lam_init):
    lam = _lambda(lam_ref, lam_init)
    for h in range(DA_HEADS):
        sl = slice(h * HEAD_W, (h + 1) * HEAD_W)
        qz = _split_components(q_ref[0, :, sl])
        kct = kc_ref[0, sl, :].astype(_BF16)
        vc = vc_ref[0, pl.ds(h, past, stride=DA_HEADS), :].astype(_BF16)
        s_c = _dot(qz, kct)
        s_n = _dot(qz, kn_ref[0, :, sl], _NT)
        m = jnp.maximum(jnp.max(s_c, axis=-1, keepdims=True), jnp.max(s_n, axis=-1, keepdims=True))
        p_c = jnp.exp2(s_c - m)
        p_n = jnp.exp2(s_n - m)
        l = jnp.sum(p_c, axis=-1, keepdims=True) + jnp.sum(p_n, axis=-1, keepdims=True)
        o = (_dot(p_c.astype(_BF16), vc) + _dot(p_n.astype(_BF16), vn_ref[0, :, sl])) / l
        o_ref[0, :, sl] = (o[:lq] - lam * o[lq:]).astype(_BF16)


def _attn_sample(lamv, q, kn, vn, kct, vc, *, lam_init):
    nb, lq, _ = q.shape
    past = kct.shape[2]
    new = pl.BlockSpec((1, lq, GROUP_W), lambda b: (b, 0, 0))
    return pl.pallas_call(
        functools.partial(_attn_sample_kernel, lq=lq, past=past, lam_init=lam_init),
        grid=(nb,),
        in_specs=[pl.BlockSpec((4, DA_QK), lambda b: (0, 0)), new, new, new,
                  pl.BlockSpec((1, GROUP_W, past), lambda b: (b, 0, 0)),
                  pl.BlockSpec((1, past * DA_HEADS, DA_V), lambda b: (b, 0, 0))],
        out_specs=new, out_shape=jax.ShapeDtypeStruct((nb, lq, GROUP_W), _BF16),
        compiler_params=_cparams(
            ("parallel",), [_nbytes((GROUP_W, past), _F32), _nbytes((past * DA_HEADS, DA_V), _F32)]
            + 4 * [_nbytes((lq, GROUP_W), _F32)], 8 * _nbytes((2 * lq, past), _F32)), name="attn_sample",
    )(lamv, q, kn, vn, kct, vc)


def _mlstm_kernel(q_ref, k_ref, v_ref, gc_ref, gr_ref, c0_ref, n0_ref, m0_ref, h_ref, c_ref, n_ref, m_ref,
                  c_sc, n_sc, m_sc, *, chunk, nchunks):
    j = pl.program_id(1)

    @pl.when(j == 0)
    def _():
        c_sc[...] = c0_ref[0]
        n_sc[...] = n0_ref[0]
        m_sc[...] = m0_ref[0]

    L = chunk
    r_i = lax.broadcasted_iota(jnp.int32, (L, L), 0)
    c_i = lax.broadcasted_iota(jnp.int32, (L, L), 1)
    causal = r_i >= c_i
    tril = causal.astype(_F32)
    triu = (r_i <= c_i).astype(_F32)
    hi = lax.Precision.HIGHEST

    for c in range(nchunks):
        rows = slice(c * L, (c + 1) * L)
        gcol = gc_ref[0, rows, :]
        grow = gr_ref[0, :, rows]
        fc_all = _dot(tril, gcol, precision=hi)
        fr_all = _dot(grow, triu, precision=hi)
        for hh in range(ML_HEADS):
            sl = slice(hh * HEAD_W, (hh + 1) * HEAD_W)
            m = m_sc[hh][0:1, 0:1]
            fc = fc_all[:, ML_HEADS + hh:ML_HEADS + hh + 1]
            fr = fr_all[ML_HEADS + hh:ML_HEADS + hh + 1, :]
            igc = gcol[:, hh:hh + 1]
            igr = grow[hh:hh + 1, :]
            a = jnp.where(causal, fc - fr + igr, -jnp.inf)
            b = fc + m
            m_t = jnp.maximum(b, jnp.max(a, axis=-1, keepdims=True))
            w = jnp.exp(a - m_t)
            sc = jnp.exp(b - m_t)
            q = q_ref[0, rows, sl]
            k = k_ref[0, rows, sl]
            v = v_ref[0, rows, sl]
            wqk = w * _dot(q, k, _NT)
            cmat = c_sc[hh]
            nrow = n_sc[hh:hh + 1, :]
            qn = jnp.sum(q.astype(_F32) * nrow, axis=-1, keepdims=True)
            num = _dot(wqk.astype(_BF16), v) + sc * _dot(q, cmat.astype(_BF16))
            den = jnp.sum(wqk, axis=-1, keepdims=True) + sc * qn
            h_ref[0, rows, sl] = (num / jnp.maximum(jnp.abs(den), jnp.exp(-m_t))).astype(_BF16)

            fl = fc[L - 1:L, :]
            g = fl - fc + igc
            m_new = jnp.maximum(fl + m, jnp.max(g, axis=0, keepdims=True))
            decay = jnp.exp(fl + m - m_new)
            kw = k.astype(_F32) * jnp.exp(g - m_new)
            c_sc[hh] = decay * cmat + _dot(kw.astype(_BF16), v, _TN)
            n_sc[hh:hh + 1, :] = decay * nrow + jnp.sum(kw, axis=0, keepdims=True)
            m_sc[hh] = jnp.broadcast_to(m_new, m_sc.shape[1:])

    @pl.when(j == pl.num_programs(1) - 1)
    def _():
        c_ref[0] = c_sc[...]
        n_ref[0] = n_sc[...]
        m_ref[0] = m_sc[...]


def _mlstm(q, k, v, gc, gr, c0, n0, m0, *, chunk, block):
    nb, s, _ = q.shape
    tok = lambda w: pl.BlockSpec((1, block, w), lambda b, j: (b, j, 0))
    cspec = pl.BlockSpec((1, ML_HEADS, ML_DK, ML_DV), lambda b, j: (b, 0, 0, 0))
    nspec = pl.BlockSpec((1, ML_HEADS, ML_DK), lambda b, j: (b, 0, 0))
    mspec = pl.BlockSpec((1, ML_HEADS, SUBLANES, HEAD_W), lambda b, j: (b, 0, 0, 0))
    return pl.pallas_call(
        functools.partial(_mlstm_kernel, chunk=chunk, nchunks=block // chunk),
        grid=(nb, s // block),
        in_specs=[tok(GROUP_W), tok(GROUP_W), tok(GROUP_W), tok(N_GATES),
                  pl.BlockSpec((1, N_GATES, block), lambda b, j: (b, 0, j)), cspec, nspec, mspec],
        out_specs=[tok(GROUP_W), cspec, nspec, mspec],
        out_shape=[jax.ShapeDtypeStruct((nb, s, GROUP_W), _BF16), jax.ShapeDtypeStruct(c0.shape, _F32),
                   jax.ShapeDtypeStruct(n0.shape, _F32), jax.ShapeDtypeStruct(m0.shape, _F32)],
        scratch_shapes=[pltpu.VMEM((ML_HEADS, ML_DK, ML_DV), _F32), pltpu.VMEM((ML_HEADS, ML_DK), _F32),
                        pltpu.VMEM((ML_HEADS, SUBLANES, HEAD_W), _F32)],
        compiler_params=_cparams(
            ("parallel", "arbitrary"), 5 * [_nbytes((block, GROUP_W), _F32)] + 2 * [_nbytes(c0.shape[1:], _F32)],
            2 * _nbytes(c0.shape[1:], _F32)), name="mlstm",
    )(q, k, v, gc, gr, c0, n0, m0)


def _mlstm_pairs_kernel(k_ref, qt_ref, vt_ref, gr_ref, st0_ref, m0_ref, h_ref, st_ref, m_ref, st_sc, m_sc,
                        *, block):
    j = pl.program_id(1)

    @pl.when(j == 0)
    def _():
        st_sc[...] = st0_ref[0]
        m_sc[...] = m0_ref[0]

    L, W = CHUNK, 2 * CHUNK
    g8 = gr_ref[0]
    pos = lax.broadcasted_iota(jnp.int32, (1, block), 1) % L
    roll = lambda x, sh: pltpu.roll(x, sh % block, 1)
    steps = [1 << b for b in range(L.bit_length() - 1)]

    def scan(x, combine, fill, reverse=False):
        for sh in steps:
            ok = (pos < L - sh) if reverse else (pos >= sh)
            x = combine(x, jnp.where(ok, roll(x, -sh if reverse else sh), fill))
        return x

    to_head_rows = lambda x: pltpu.roll(x, ML_HEADS, 0)
    f8 = to_head_rows(scan(g8, jnp.add, 0.0))
    d8 = g8 - f8
    cm8 = scan(d8, jnp.maximum, -jnp.inf)
    big_g = to_head_rows(scan(g8, jnp.add, 0.0, reverse=True) - g8) + g8
    mg8 = jnp.maximum(scan(big_g, jnp.maximum, -jnp.inf), scan(big_g, jnp.maximum, -jnp.inf, reverse=True))
    ws8 = jnp.exp(big_g - mg8)
    d_cols = d8.T

    s_i = lax.broadcasted_iota(jnp.int32, (W, W), 0)
    t_i = lax.broadcasted_iota(jnp.int32, (W, W), 1)
    allowed = (s_i // L == t_i // L) & (s_i <= t_i)
    lane_w = lax.broadcasted_iota(jnp.int32, (1, W), 1)
    first = lane_w < L

    sts = [st_sc[hh] for hh in range(ML_HEADS)]
    ms = [m_sc[hh][0:1, 0:1] for hh in range(ML_HEADS)]

    def independent(p, hh):
        lanes = slice(p * W, (p + 1) * W)
        feat = slice(hh * HEAD_W, (hh + 1) * HEAD_W)
        kp = k_ref[0, lanes, feat]
        qt = qt_ref[0, 0, feat, lanes]
        v1t = vt_ref[0, 0, hh * VT_ROWS:(hh + 1) * VT_ROWS, lanes]
        ws = ws8[hh:hh + 1, lanes]
        v1f = v1t.astype(_F32)
        return dict(
            p=p, hh=hh, qt=qt, v1t=v1t, qk_t=_dot(kp, qt),
            upd_a=_dot((v1f * jnp.where(first, ws, 0.0)).astype(_BF16), kp),
            upd_b=_dot((v1f * jnp.where(first, 0.0, ws)).astype(_BF16), kp))

    def dependent(u):
        p, hh, qt = u["p"], u["hh"], u["qt"]
        lanes = slice(p * W, (p + 1) * W)
        row = lambda x: x[hh:hh + 1, lanes]
        cm, f_row = row(cm8), row(f8)
        w_t = jnp.exp(jnp.where(allowed, d_cols[lanes, hh:hh + 1] - cm, -jnp.inf))
        intra = _dot(u["v1t"], (w_t * u["qk_t"]).astype(_BF16))
        st_a, m_a = sts[hh], ms[hh]
        fl_a, mg_a = f_row[:, L - 1:L], row(mg8)[:, 0:1]
        fl_b, mg_b = f_row[:, W - 1:W], row(mg8)[:, L:L + 1]
        m_b = jnp.maximum(fl_a + m_a, mg_a)
        st_b = jnp.exp(fl_a + m_a - m_b) * st_a + jnp.exp(mg_a - m_b) * u["upd_a"]
        m_c = jnp.maximum(fl_b + m_b, mg_b)
        sts[hh] = jnp.exp(fl_b + m_b - m_c) * st_b + jnp.exp(mg_b - m_c) * u["upd_b"]
        ms[hh] = m_c
        inter = jnp.where(first, _dot(st_a.astype(_BF16), qt), _dot(st_b.astype(_BF16), qt))
        m_prev = jnp.where(first, m_a, m_b)
        mt = jnp.maximum(m_prev, cm)
        tot = jnp.exp(cm - mt) * intra + jnp.exp(m_prev - mt) * inter
        den = jnp.maximum(jnp.abs(tot[ML_DV:ML_DV + 1]), jnp.exp(-(f_row + mt)))
        h_ref[0, lanes, hh * HEAD_W:(hh + 1) * HEAD_W] = (tot[:ML_DV] / den).T.astype(_BF16)

    units = [(p, hh) for p in range(block // W) for hh in range(ML_HEADS)]
    nxt = independent(*units[0])
    for n in range(len(units)):
        cur = nxt
        if n + 1 < len(units):
            nxt = independent(*units[n + 1])
        dependent(cur)
    for hh in range(ML_HEADS):
        st_sc[hh] = sts[hh]
        m_sc[hh] = jnp.broadcast_to(ms[hh], m_sc.shape[1:])

    @pl.when(j == pl.num_programs(1) - 1)
    def _():
        st_ref[0] = st_sc[...]
        m_ref[0] = m_sc[...]


def _mlstm_pairs(k, qt, vt, gr, st0, m0):
    nb, s, _ = k.shape
    block = qt.shape[3]
    nt = s // block
    state = pl.BlockSpec((1, ML_HEADS, VT_ROWS, ML_DK), lambda b, j: (b, 0, 0, 0))
    mspec = pl.BlockSpec((1, ML_HEADS, SUBLANES, HEAD_W), lambda b, j: (b, 0, 0, 0))
    tok = pl.BlockSpec((1, block, GROUP_W), lambda b, j: (b, j, 0))
    return pl.pallas_call(
        functools.partial(_mlstm_pairs_kernel, block=block),
        grid=(nb, nt),
        in_specs=[tok, pl.BlockSpec((1, 1, GROUP_W, block), lambda b, j: (b, j, 0, 0)),
                  pl.BlockSpec((1, 1, ML_HEADS * VT_ROWS, block), lambda b, j: (b, j, 0, 0)),
                  pl.BlockSpec((1, N_GATES, block), lambda b, j: (b, 0, j)), state, mspec],
        out_specs=[tok, state, mspec],
        out_shape=[jax.ShapeDtypeStruct((nb, s, GROUP_W), _BF16), jax.ShapeDtypeStruct(st0.shape, _F32),
                   jax.ShapeDtypeStruct(m0.shape, _F32)],
        scratch_shapes=[pltpu.VMEM((ML_HEADS, VT_ROWS, ML_DK), _F32), pltpu.VMEM((ML_HEADS, SUBLANES, HEAD_W), _F32)],
        compiler_params=_cparams(
            ("parallel", "arbitrary"),
            3 * [_nbytes((block, GROUP_W), _BF16)] + [_nbytes((ML_HEADS * VT_ROWS, block), _BF16)]
            + 2 * [_nbytes(st0.shape[1:], _F32)], 16 * _nbytes((VT_ROWS, 2 * CHUNK), _F32) * ML_HEADS), name="mlstm_pairs",
    )(k, qt, vt, gr, st0, m0)


def _mix_kernel(da_ref, ml_ref, og_ref, x_ref, subln_ref, wo_ref, g2_ref, wr_ref, br_ref, cnt0_ref,
                xmid_ref, xp_ref, idx_ref, gate_ref, rank_ref, cnt_ref, cnt_sc, *, tm, lam_init):
    step = pl.program_id(0)

    @pl.when(step == 0)
    def _():
        cnt_sc[...] = cnt0_ref[...]

    parts = []
    for h in range(DA_HEADS):
        d = da_ref[:, h * HEAD_W:(h + 1) * HEAD_W].astype(_F32)
        d = d * lax.rsqrt(jnp.mean(d * d, axis=-1, keepdims=True) + EPS) * subln_ref[...]
        parts.append((d * (1.0 - lam_init)).astype(_BF16))
    parts.append((og_ref[...].astype(_F32) * ml_ref[...].astype(_F32)).astype(_BF16))
    xm = x_ref[...] + _dot(jnp.concatenate(parts, axis=1), wo_ref[...])
    xmid_ref[...] = xm
    xn_f = xm * lax.rsqrt(jnp.mean(xm * xm, axis=-1, keepdims=True) + EPS) * g2_ref[...]
    xp_ref[...] = _pack_rows(xn_f)
    xn = xn_f.astype(_BF16)

    logits = _dot(wr_ref[...], xn, _NT) + br_ref[...]
    e_i = lax.broadcasted_iota(jnp.int32, logits.shape, 0)
    member = jnp.zeros(logits.shape, jnp.bool_)
    vals, idxs = [], []
    for _ in range(TOP_K):
        mx = jnp.max(logits, axis=0, keepdims=True)
        sel = jnp.min(jnp.where(logits == mx, e_i, N_EXPERTS), axis=0, keepdims=True)
        hit = e_i == sel
        member = member | hit
        logits = jnp.where(hit, -jnp.inf, logits)
        vals.append(mx)
        idxs.append(sel)
    ex = [jnp.exp(v - vals[0]) for v in vals]
    tot = ex[0] + ex[1] + ex[2] + ex[3]
    idx_ref[...] = jnp.concatenate(idxs, axis=0)
    gate_ref[...] = jnp.concatenate([e / tot for e in ex], axis=0)

    upper = (lax.broadcasted_iota(jnp.int32, (tm, tm), 0) < lax.broadcasted_iota(jnp.int32, (tm, tm), 1))
    memf = member.astype(_F32)
    before = _dot(memf.astype(_BF16), upper.astype(_BF16)) + cnt_sc[...]
    rank_ref[...] = jnp.concatenate(
        [jnp.sum(jnp.where(e_i == s, before, 0.0), axis=0, keepdims=True) for s in idxs], axis=0).astype(jnp.int32)
    cnt_sc[...] += jnp.sum(memf, axis=1, keepdims=True)
    cnt_ref[...] = cnt_sc[...]


def _mix(da, ml, og, x, subln, w_out, norm2, w_rt, b_r, cnt0, *, tm, lam_init):
    t = x.shape[0]
    tok = lambda w: pl.BlockSpec((tm, w), lambda i: (i, 0))
    const = lambda shape: pl.BlockSpec(shape, lambda i: (0,) * len(shape))
    lane_tok = pl.BlockSpec((TOP_K, tm), lambda i: (0, i))
    return pl.pallas_call(
        functools.partial(_mix_kernel, tm=tm, lam_init=lam_init),
        grid=(t // tm,),
        in_specs=[tok(GROUP_W), tok(GROUP_W), tok(GROUP_W), tok(D_MODEL), const((1, DA_V)),
                  const((D_MODEL, D_MODEL)), const((1, D_MODEL)), const((N_EXPERTS, D_MODEL)),
                  const((N_EXPERTS, 1)), const((N_EXPERTS, 1))],
        out_specs=[tok(D_MODEL), tok(PACK_W), lane_tok, lane_tok, lane_tok, const((N_EXPERTS, 1))],
        out_shape=[jax.ShapeDtypeStruct((t, D_MODEL), _F32), jax.ShapeDtypeStruct((t, PACK_W), jnp.int32),
                   jax.ShapeDtypeStruct((TOP_K, t), jnp.int32), jax.ShapeDtypeStruct((TOP_K, t), _F32),
                   jax.ShapeDtypeStruct((TOP_K, t), jnp.int32), jax.ShapeDtypeStruct((N_EXPERTS, 1), _F32)],
        scratch_shapes=[pltpu.VMEM((N_EXPERTS, 1), _F32)],
        compiler_params=_cparams(
            ("arbitrary",),
            3 * [_nbytes((tm, GROUP_W), _BF16)] + 2 * [_nbytes((tm, D_MODEL), _F32)] + [_nbytes((tm, PACK_W), _F32)]
            + [_nbytes((D_MODEL, D_MODEL), _BF16)], 4 * _nbytes((tm, D_MODEL), _F32) + 2 * _nbytes((tm, tm), _F32)), name="mix",
    )(da, ml, og, x, subln, w_out, norm2, w_rt, b_r, cnt0)


def _moe_kernel(be_ref, nu_ref, nv_ref, slot_ref, nxt_ref, x_ref, wgu_hbm, bgu_ref, wd_hbm, bd_ref, y_ref,
                wgu_f32, wd_f32, wgu_sc, wd_sc, sems):
    j = pl.program_id(0)
    used = j < nu_ref[0]
    fresh = (j == 0) | (be_ref[j] != be_ref[jnp.maximum(j - 1, 0)])
    slot = slot_ref[j]

    def fetch(e, s):
        return (pltpu.make_async_copy(wgu_hbm.at[e], wgu_f32.at[s], sems.at[s, 0]),
                pltpu.make_async_copy(wd_hbm.at[e], wd_f32.at[s], sems.at[s, 1]))

    @pl.when(j == 0)
    def _():
        for cp in fetch(be_ref[0], slot):
            cp.start()

    @pl.when(fresh & used)
    def _():
        for cp in fetch(be_ref[j], slot):
            cp.wait()

        @pl.when(nxt_ref[j] >= 0)
        def _():
            for cp in fetch(nxt_ref[j], 1 - slot):
                cp.start()

        rows = CAST_ROWS
        def cast_gu(r, _):
            sl = pl.ds(pl.multiple_of(r * rows, rows), rows)
            wgu_sc[sl, :] = wgu_f32[slot, sl, :].astype(_BF16)
            return 0
        lax.fori_loop(0, D_MODEL // rows, cast_gu, 0)
        def cast_d(r, _):
            sl = pl.ds(pl.multiple_of(r * rows, rows), rows)
            wd_sc[sl, :] = wd_f32[slot, sl, :].astype(_BF16)
            return 0
        lax.fori_loop(0, D_FF // rows, cast_d, 0)

    @pl.when(j >= nu_ref[0])
    def _():
        y_ref[...] = jnp.zeros_like(y_ref)

    def ffn(row0):
        hi, lo = _unpack_rows(x_ref[row0:, :])
        x = jnp.concatenate([hi.astype(_BF16), lo.astype(_BF16)], axis=1)
        n_chunks = D_FF // FF_CHUNK

        def gate_up(c):
            lo = c * FF_CHUNK
            return (_dot(x, wgu_sc[:, lo:lo + FF_CHUNK]), _dot(x, wgu_sc[:, D_FF + lo:D_FF + lo + FF_CHUNK]))

        acc = None
        nxt = gate_up(0)
        for c in range(n_chunks):
            lo = c * FF_CHUNK
            gate, up = nxt
            if c + 1 < n_chunks:
                nxt = gate_up(c + 1)
            gate = jnp.minimum(gate + bgu_ref[0, :, lo:lo + FF_CHUNK], SWIGLU_LIMIT)
            up = jnp.clip(up + bgu_ref[0, :, D_FF + lo:D_FF + lo + FF_CHUNK], -SWIGLU_LIMIT, SWIGLU_LIMIT)
            act = (up + 1.0) * (gate * jax.nn.sigmoid(SWIGLU_ALPHA * gate))
            down = _dot(act.astype(_BF16), wd_sc[lo:lo + FF_CHUNK, :])
            acc = down if acc is None else acc + down
        y_ref[row0:, :] = _pack_rows(acc + bd_ref[0])

    half = MOE_BLOCK // 2
    @pl.when(used & (nv_ref[j] > half))
    def _():
        ffn(0)

    @pl.when(used & (nv_ref[j] <= half))
    def _():
        y_ref[:half, :] = jnp.zeros((half, PACK_W), jnp.int32)
        ffn(half)


def _moe(blk_e, n_used, blk_rows, blk_slot, blk_next, xb, w_gu, b_gu, w_d, b_d):
    n_blocks = xb.shape[0] // MOE_BLOCK
    last = lambda j, nu: jnp.minimum(j, nu[0] - 1)
    row_spec = pl.BlockSpec((MOE_BLOCK, PACK_W), lambda j, be, nu, nv, sl, nx: (last(j, nu), 0))
    bias_spec = lambda c: pl.BlockSpec((1, 1, c), lambda j, be, nu, nv, sl, nx: (be[last(j, nu)], 0, 0))
    hbm = pl.BlockSpec(memory_space=pl.ANY)
    grid_spec = pltpu.PrefetchScalarGridSpec(
        num_scalar_prefetch=5, grid=(n_blocks,),
        in_specs=[row_spec, hbm, bias_spec(2 * D_FF), hbm, bias_spec(D_MODEL)],
        out_specs=pl.BlockSpec((MOE_BLOCK, PACK_W), lambda j, be, nu, nv, sl, nx: (j, 0)),
        scratch_shapes=[pltpu.VMEM((2, D_MODEL, 2 * D_FF), _F32), pltpu.VMEM((2, D_FF, D_MODEL), _F32),
                        pltpu.VMEM((D_MODEL, 2 * D_FF), _BF16), pltpu.VMEM((D_FF, D_MODEL), _BF16),
                        pltpu.SemaphoreType.DMA((2, 2))])
    return pl.pallas_call(
        _moe_kernel, grid_spec=grid_spec, out_shape=jax.ShapeDtypeStruct(xb.shape, jnp.int32),
        compiler_params=_cparams(
            ("arbitrary",), 2 * [_nbytes((MOE_BLOCK, PACK_W), _F32)],
            2 * _nbytes((D_MODEL, 3 * D_FF), _F32) + _nbytes((D_MODEL, 3 * D_FF), _BF16)
            + 4 * _nbytes((MOE_BLOCK, D_MODEL), _F32)), name="moe",
    )(blk_e, n_used, blk_rows, blk_slot, blk_next, xb, w_gu, b_gu, w_d, b_d)


def _sc_mesh():
    return plsc.VectorSubcoreMesh(core_axis_name="c", subcore_axis_name="s")


def _sc_worker():
    return lax.axis_index("s") * SC_CORES + lax.axis_index("c")


def _sc_dispatch(sources, n_rows):
    n_src = len(sources)
    shapes = []
    for _, dest3 in sources:
        n_chunks, _, chunk = dest3.shape
        per_worker = n_chunks // SC_WORKERS
        assert per_worker * SC_WORKERS == n_chunks and chunk % SUBLANES == 0
        shapes.append((per_worker, chunk))

    def body(*refs):
        x_hbms, d_hbms, o_hbm = refs[:n_src], refs[n_src:2 * n_src], refs[2 * n_src]
        scratch = refs[2 * n_src + 1:]
        load_a, load_b, scat_a, scat_b = scratch[3 * n_src:]
        worker = _sc_worker()
        for i, (per_worker, chunk) in enumerate(shapes):
            x_hbm, idx_v, rows_a, rows_b = x_hbms[i], scratch[3 * i], scratch[3 * i + 1], scratch[3 * i + 2]
            pltpu.sync_copy(d_hbms[i].at[worker], idx_v)

            def load(ci, rows, sem):
                row0 = (worker * per_worker + ci) * chunk
                return pltpu.make_async_copy(x_hbm.at[pl.ds(pl.multiple_of(row0, SUBLANES), chunk)], rows, sem)

            def scatters(ci, rows, sem):
                return [pltpu.make_async_copy(rows, o_hbm.at[idx_v.at[ci * TOP_K + k]], sem) for k in range(TOP_K)]

            def start(copies):
                for cp in copies:
                    cp.start()

            def wait(copies):
                for cp in copies:
                    cp.wait()

            load(0, rows_a, load_a).start()

            @pl.loop(0, per_worker // 2)
            def _(p):
                ca, cb = 2 * p, 2 * p + 1
                load(ca, rows_a, load_a).wait()
                load(cb, rows_b, load_b).start()
                start(scatters(ca, rows_a, scat_a))
                load(cb, rows_b, load_b).wait()
                wait(scatters(ca, rows_a, scat_a))
                load(jnp.minimum(ca + 2, per_worker - 1), rows_a, load_a).start()
                start(scatters(cb, rows_b, scat_b))
                wait(scatters(cb, rows_b, scat_b))

            last = per_worker - 1
            load(last, rows_a, load_a).wait()
            if per_worker % 2:
                start(scatters(last, rows_a, scat_a))
                wait(scatters(last, rows_a, scat_a))

    dma = pltpu.SemaphoreType.DMA
    scratch_types = []
    for per_worker, chunk in shapes:
        scratch_types += [pltpu.VMEM((per_worker * TOP_K, chunk), jnp.int32), pltpu.VMEM((chunk, PACK_W), jnp.int32),
                          pltpu.VMEM((chunk, PACK_W), jnp.int32)]
    tables = [dest3.reshape(SC_WORKERS, pw * TOP_K, chunk) for (_, dest3), (pw, chunk) in zip(sources, shapes)]
    return pl.kernel(
        body, out_type=jax.ShapeDtypeStruct((n_rows, PACK_W), jnp.int32), mesh=_sc_mesh(),
        scratch_types=scratch_types + [dma, dma, dma, dma],
        name="sc_dispatch")(*[xp for xp, _ in sources], *tables)


def _sc_gather(table, idx2):
    n_chunks, chunk = idx2.shape
    per_worker = n_chunks // SC_WORKERS
    assert per_worker * SC_WORKERS == n_chunks and chunk % SUBLANES == 0

    def body(t_hbm, i_hbm, o_hbm, idx_v, rows_a, rows_b, gath_a, gath_b, put_a, put_b):
        worker = _sc_worker()
        pltpu.sync_copy(i_hbm.at[worker], idx_v)

        def gather(ci, rows, sem):
            return pltpu.make_async_copy(t_hbm.at[idx_v.at[ci]], rows, sem)

        def put(ci, rows, sem):
            row0 = (worker * per_worker + ci) * chunk
            return pltpu.make_async_copy(rows, o_hbm.at[pl.ds(pl.multiple_of(row0, SUBLANES), chunk)], sem)

        gather(0, rows_a, gath_a).start()

        @pl.loop(0, per_worker // 2)
        def _(p):
            ca, cb = 2 * p, 2 * p + 1
            gather(ca, rows_a, gath_a).wait()
            gather(cb, rows_b, gath_b).start()
            put(ca, rows_a, put_a).start()
            gather(cb, rows_b, gath_b).wait()
            put(ca, rows_a, put_a).wait()
            gather(jnp.minimum(ca + 2, per_worker - 1), rows_a, gath_a).start()
            put(cb, rows_b, put_b).start()
            put(cb, rows_b, put_b).wait()

        last = per_worker - 1
        gather(last, rows_a, gath_a).wait()
        if per_worker % 2:
            put(last, rows_a, put_a).start()
            put(last, rows_a, put_a).wait()

    dma = pltpu.SemaphoreType.DMA
    return pl.kernel(
        body, out_type=jax.ShapeDtypeStruct((n_chunks * chunk, PACK_W), jnp.int32), mesh=_sc_mesh(),
        scratch_types=[pltpu.VMEM((per_worker, chunk), jnp.int32), pltpu.VMEM((chunk, PACK_W), jnp.int32),
                       pltpu.VMEM((chunk, PACK_W), jnp.int32), dma, dma, dma, dma],
        name="sc_gather")(table, idx2.reshape(SC_WORKERS, per_worker, chunk))


def _final_kernel(x_ref, y_ref, g_ref, nf_ref, o_ref):
    g_rows = g_ref[...]
    g = jnp.concatenate([g_rows, jnp.zeros((SUBLANES - TOP_K, g_rows.shape[1]), _F32)], axis=0).T
    hi = jnp.zeros((x_ref.shape[0], PACK_W), _F32)
    lo = jnp.zeros((x_ref.shape[0], PACK_W), _F32)
    for k in range(TOP_K):
        y_hi, y_lo = _unpack_rows(y_ref[k])
        hi = hi + g[:, k:k + 1] * y_hi
        lo = lo + g[:, k:k + 1] * y_lo
    x = x_ref[...] + jnp.concatenate([hi, lo], axis=1)
    o_ref[...] = x * lax.rsqrt(jnp.mean(x * x, axis=-1, keepdims=True) + EPS) * nf_ref[...]


def _final(xmid, yg, gates, norm_f, *, tm, first_block):
    t = xmid.shape[0]
    return pl.pallas_call(
        _final_kernel, grid=(t // tm,),
        in_specs=[pl.BlockSpec((tm, D_MODEL), lambda i: (i, 0)),
                  pl.BlockSpec((TOP_K, tm, PACK_W), lambda i: (0, i + first_block, 0)),
                  pl.BlockSpec((TOP_K, tm), lambda i: (0, i)),
                  pl.BlockSpec((1, D_MODEL), lambda i: (0, 0))],
        out_specs=pl.BlockSpec((tm, D_MODEL), lambda i: (i, 0)),
        out_shape=jax.ShapeDtypeStruct((t, D_MODEL), _F32),
        compiler_params=_cparams(
            ("parallel",), 2 * [_nbytes((tm, D_MODEL), _F32)] + [_nbytes((TOP_K, tm, PACK_W), _F32)],
            2 * _nbytes((tm, D_MODEL), _F32)), name="final",
    )(xmid, yg, gates, norm_f)


def kernel(x_prompt, x_sample, cache_k, cache_v, state_C, state_n, state_m, norm1, w_in, b_igate, b_fgate,
           lambda_q1, lambda_k1, lambda_q2, lambda_k2, subln, w_out, norm2, w_router, b_router, w_gate_up,
           b_gate_up, w_down, b_down, norm_f):
    nb, s, _ = x_prompt.shape
    db, ls, _ = x_sample.shape
    past = cache_k.shape[2]
    depth = w_in.shape[0]
    assert depth == 1 and s % PROJ_TILE == 0 and (db * ls) % SUBLANES == 0
    l = 0
    lam_init = 0.8 - 0.6 * math.exp(-0.3 * l)
    t_p, t_s = nb * s, db * ls

    w_nat = jnp.pad(w_in[l].T, ((0, HEAD_W - N_GATES), (0, 0))).astype(_BF16)
    b_gates = jnp.pad(jnp.concatenate([b_igate[l], b_fgate[l]]).astype(_F32), (0, HEAD_W - N_GATES))[None]
    inv = ROPE_THETA ** (-jnp.arange(0, DA_QK, 2, dtype=_F32) / DA_QK)
    inv_full = jnp.tile(inv, HEAD_W // (DA_QK // 2))[None]
    lamv = jnp.stack([lambda_q1[l], lambda_k1[l], lambda_q2[l], lambda_k2[l]]).astype(_F32)
    g1 = norm1[l][None]

    pp = _proj(x_prompt, g1, w_nat, b_gates, inv_full, tm=PROJ_TILE, pos_base=0, pos_mod=s,
               names=("q", "kf", "kb", "vf", "vt", "mk", "og", "gr", "mqt", "mvt"))
    ps = _proj(x_sample.reshape(1, t_s, D_MODEL), g1, w_nat, b_gates, inv_full, tm=t_s, pos_base=past, pos_mod=ls,
               names=("q", "kf", "kb", "vf", "mq", "mk", "mv", "og", "gc", "gr"))
    kf_p, vf_p, kf_s, vf_s = pp["kf"], pp["vf"], ps["kf"], ps["vf"]

    da_p = _attn_prompt(lamv, pp["q"], pp["kb"], pp["vt"], lam_init=lam_init)
    per_b = lambda a: a.reshape(db, ls, a.shape[-1])
    kct = jnp.transpose(cache_k[l], (0, 2, 3, 4, 1)).reshape(db, GROUP_W, past)
    vc = cache_v[l].reshape(db, past * DA_HEADS, DA_V)
    vn = vf_s.reshape(db, ls, GROUP_W).astype(_BF16)
    da_s = _attn_sample(lamv, per_b(ps["q"]), per_b(ps["kb"]), vn, kct, vc, lam_init=lam_init)

    assert PROJ_TILE % (2 * CHUNK) == 0
    h_p, st_p, m_p = _mlstm_pairs(pp["mk"], pp["mqt"], pp["mvt"], pp["gr"],
                                  jnp.zeros((nb, ML_HEADS, VT_ROWS, ML_DK), _F32),
                                  jnp.zeros((nb, ML_HEADS, SUBLANES, HEAD_W), _F32))
    gr_sb = ps["gr"].reshape(8, db, ls).transpose(1, 0, 2)
    m0_s = jnp.broadcast_to(state_m[l].astype(_F32)[:, :, None, None], (db, ML_HEADS, SUBLANES, HEAD_W))
    h_s, c_s, n_s, m_s = _mlstm(per_b(ps["mq"]), per_b(ps["mk"]), per_b(ps["mv"]), per_b(ps["gc"]), gr_sb,
                                state_C[l].astype(_F32), state_n[l].astype(_F32), m0_s, chunk=ls, block=ls)
    c_p, n_p, mm_p = jnp.swapaxes(st_p[:, :, :ML_DV, :], -1, -2), st_p[:, :, ML_DV, :], m_p[:, :, 0, 0]
    mm_s = m_s[:, :, 0, 0]

    wo = w_out[l].astype(_BF16)
    w_rt = w_router[l].T.astype(_BF16)
    b_r = b_router[l].astype(_F32)[:, None]
    mix = functools.partial(_mix, subln=subln[l][None], w_out=wo, norm2=norm2[l][None], w_rt=w_rt, b_r=b_r,
                            lam_init=lam_init)
    flat = lambda a: a.reshape(-1, a.shape[-1])
    t_all = t_p + t_s
    assert t_p % t_s == 0
    xm_p, xp_p, idx_p, gate_p, rank_p, cnt_p = mix(
        flat(da_p), flat(h_p), flat(pp["og"]), flat(x_prompt), cnt0=jnp.zeros((N_EXPERTS, 1), _F32), tm=PROJ_TILE)
    xm_s, xp_s, idx_s, gate_s, rank_s, cnt = mix(
        flat(da_s), flat(h_s), flat(ps["og"]), flat(x_sample), cnt0=cnt_p, tm=t_s)

    rows = t_all * TOP_K
    n_blocks = -(-rows // MOE_BLOCK) + N_EXPERTS
    counts = cnt[:, 0].astype(jnp.int32)
    padded = (counts + MOE_BLOCK - 1) // MOE_BLOCK * MOE_BLOCK
    pend = jnp.cumsum(padded)
    pstart = pend - counts
    experts = jnp.arange(N_EXPERTS, dtype=jnp.int32)[:, None, None]
    first_row = lambda idx: jnp.sum(jnp.where(idx[None] == experts, pstart[:, None, None], 0), axis=0)
    dest_p, dest_s = first_row(idx_p) + rank_p, first_row(idx_s) + rank_s
    n_used = (pend[-1] // MOE_BLOCK).astype(jnp.int32)[None]
    blk_lo = jnp.arange(n_blocks, dtype=jnp.int32) * MOE_BLOCK
    blk_e = jnp.minimum(jnp.sum((pend[None, :] <= blk_lo[:, None]).astype(jnp.int32), axis=1), N_EXPERTS - 1)
    row_lo = jnp.sum(jnp.where(blk_e[:, None] == experts[:, 0].T, pstart[None, :], 0), axis=1)
    blk_rows = jnp.where(blk_lo < pend[-1], jnp.clip(blk_lo + MOE_BLOCK - row_lo, 0, MOE_BLOCK), 0).astype(jnp.int32)

    def chunked(dest):
        chunk = min(DISPATCH_CHUNK, dest.shape[1] // SC_WORKERS)
        return dest.reshape(TOP_K, dest.shape[1] // chunk, chunk).transpose(1, 0, 2)

    assert rows % (SC_WORKERS * COMBINE_CHUNK) == 0
    xb = _sc_dispatch([(xp_p, chunked(dest_p)), (xp_s, chunked(dest_s))], n_blocks * MOE_BLOCK)
    dest = jnp.concatenate([dest_p, dest_s], axis=1)
    first_blk = jnp.concatenate([jnp.ones((1,), jnp.bool_), blk_e[1:] != blk_e[:-1]])
    blk_slot = ((jnp.cumsum(first_blk.astype(jnp.int32)) - 1) % 2).astype(jnp.int32)
    after = jnp.sum(jnp.where(blk_e[:, None] == experts[:, 0].T, pend[None, :], 0), axis=1) // MOE_BLOCK
    blk_next = jnp.where(after < n_used[0], blk_e[jnp.minimum(after, n_blocks - 1)], -1).astype(jnp.int32)
    yb = _moe(blk_e, n_used, blk_rows, blk_slot, blk_next, xb, w_gate_up[l], b_gate_up[l][:, None, :], w_down[l], b_down[l][:, None, :])
    yg = _sc_gather(yb, dest.reshape(rows // COMBINE_CHUNK, COMBINE_CHUNK)).reshape(TOP_K, t_all, PACK_W)
    nf = norm_f[None]
    y_p = _final(xm_p, yg, gate_p, nf, tm=FINAL_TILE, first_block=0).reshape(nb, s, D_MODEL)
    y_s = _final(xm_s, yg, gate_s, nf, tm=t_s, first_block=t_p // t_s).reshape(db, ls, D_MODEL)

    st = lambda a: a[None]
    return (y_p, y_s,
            st(kf_p.reshape(nb, s, DA_HEADS, 2, DA_QK)), st(vf_p.reshape(nb, s, DA_HEADS, DA_V)),
            st(c_p), st(n_p), st(mm_p),
            st(kf_s.reshape(db, ls, DA_HEADS, 2, DA_QK)), st(vf_s.reshape(db, ls, DA_HEADS, DA_V)),
            st(c_s.astype(state_C.dtype)), st(n_s.astype(state_n.dtype)), st(mm_s.astype(state_m.dtype)))
```

```python
import functools
import math

import jax
import jax.numpy as jnp
from jax import lax
from jax.experimental import pallas as pl
from jax.experimental.pallas import tpu as pltpu
from jax.experimental.pallas import tpu_sc as plsc

D_MODEL = 1024
CHUNK = 64
DA_HEADS = 4
DA_QK = 64
DA_V = 128
ML_HEADS = 4
ML_DK = 128
ML_DV = 128
HEAD_W = 128
SUBLANES = 8
N_GATES = 2 * ML_HEADS
CAST_ROWS = 128
GROUP_W = 512
ROPE_THETA = 10000.0
N_EXPERTS = 32
TOP_K = 4
D_FF = 1024
SWIGLU_LIMIT = 7.0
SWIGLU_ALPHA = 1.702
EPS = 1e-6
NEG_BIG = -1e30

LOG2E = math.log2(math.e)
VT_ROWS = 144
PROJ_TILE = 512
FINAL_TILE = 1024
ATTN_TQ = 512
MOE_BLOCK = 512
FF_CHUNK = 512
PACK_W = D_MODEL // 2
SC_CORES = 2
SC_SUBCORES = 16
SC_WORKERS = SC_CORES * SC_SUBCORES
DISPATCH_CHUNK = 64
COMBINE_CHUNK = 80
VMEM_BYTES = 64 * 1024 * 1024

_F32 = jnp.float32
_BF16 = jnp.bfloat16


def _nbytes(shape, dtype):
    return math.prod(shape) * jnp.dtype(dtype).itemsize


def _cparams(sem, blocks, resident=0):
    need = 2 * sum(blocks) + resident
    assert need <= VMEM_BYTES, need
    return pltpu.CompilerParams(dimension_semantics=sem, vmem_limit_bytes=need)


def _dot(a, b, dims=(((1,), (0,)), ((), ())), precision=None):
    return lax.dot_general(a, b, dims, precision=precision, preferred_element_type=_F32)


_NT = (((1,), (1,)), ((), ()))
_TN = (((0,), (0,)), ((), ()))


def _pack_rows(x):
    half = x.shape[1] // 2
    bits = lambda v: lax.bitcast_convert_type(v.astype(_BF16).astype(_F32), jnp.int32)
    return bits(x[:, :half]) | lax.shift_right_logical(bits(x[:, half:]), 16)


def _unpack_rows(p):
    hi = lax.bitcast_convert_type(p & jnp.int32(-65536), _F32)
    lo = lax.bitcast_convert_type(lax.shift_left(p, 16), _F32)
    return hi, lo


def _proj_kernel(x_ref, g_ref, wt_ref, bg_ref, inv_ref, *out_refs, names, tm, pos_base, pos_mod):
    out = dict(zip(names, out_refs))
    x = x_ref[0]
    xn = (x * lax.rsqrt(jnp.mean(x * x, axis=-1, keepdims=True) + EPS) * g_ref[...]).astype(_BF16)

    row = pl.program_id(1) * tm + lax.broadcasted_iota(jnp.int32, (tm, 1), 0)
    pos = (pos_base + row % pos_mod).astype(_F32)
    ang = pos * inv_ref[...]
    cos = jnp.cos(ang)
    sin = jnp.sin(ang)
    lane = lax.broadcasted_iota(jnp.int32, (1, HEAD_W), 1)
    first = (lane % DA_QK) < (DA_QK // 2)
    sin = jnp.where(first, -sin, sin)

    def rope(z):
        rot = jnp.where(first, pltpu.roll(z, HEAD_W - DA_QK // 2, 1), pltpu.roll(z, DA_QK // 2, 1))
        return z * cos + rot * sin

    def group(c, width=GROUP_W):
        return _dot(xn, wt_ref[c * GROUP_W:c * GROUP_W + width, :], _NT)

    def put_q(zq):
        for h in range(DA_HEADS):
            sl = slice(h * HEAD_W, (h + 1) * HEAD_W)
            out["q"][0, :, sl] = (rope(zq[:, sl]) * (DA_QK ** -0.5 * LOG2E)).astype(_BF16)

    def put_k(zk):
        for h in range(DA_HEADS):
            sl = slice(h * HEAD_W, (h + 1) * HEAD_W)
            rk = rope(zk[:, sl])
            out["kf"][0, :, sl] = rk
            out["kb"][0, :, sl] = rk.astype(_BF16)

    def with_ones_rows(zt):
        ones = jnp.ones((VT_ROWS - DA_V, tm), _BF16)
        return jnp.concatenate(
            [part for h in range(DA_HEADS) for part in (zt[h * DA_V:(h + 1) * DA_V], ones)], axis=0)

    def put_v(zv):
        for h in range(DA_HEADS):
            out["vf"][0, pl.ds(h, tm, stride=DA_HEADS), :] = zv[:, h * DA_V:(h + 1) * DA_V]
        if "vt" in out:
            out["vt"][0, 0] = with_ones_rows(zv.T.astype(_BF16))

    def put_mq(z):
        if "mq" in out:
            out["mq"][0] = z.astype(_BF16)
        if "mqt" in out:
            out["mqt"][0, 0] = z.T.astype(_BF16)

    def put_mk(z):
        out["mk"][0] = (z * (ML_DK ** -0.5)).astype(_BF16)

    def put_mv(z):
        if "mv" in out:
            out["mv"][0] = z.astype(_BF16)
        if "mvt" in out:
            out["mvt"][0, 0] = with_ones_rows(z.T.astype(_BF16))

    def put_mo(z):
        out["og"][0] = jax.nn.sigmoid(z).astype(_BF16)

    def put_gates(z):
        zg = z + bg_ref[...]
        lane_g = lax.broadcasted_iota(jnp.int32, zg.shape, 1)
        logsig = jnp.minimum(zg, 0.0) - jnp.log(1.0 + jnp.exp(-jnp.abs(zg)))
        gates = jnp.where(lane_g < ML_HEADS, zg, logsig)
        if "gc" in out:
            out["gc"][0] = gates[:, :N_GATES]
        out["gr"][0] = gates.T[:N_GATES, :]

    sinks = [(put_q, GROUP_W), (put_k, GROUP_W), (put_v, GROUP_W), (put_mq, GROUP_W), (put_mk, GROUP_W),
             (put_mv, GROUP_W), (put_mo, GROUP_W), (put_gates, HEAD_W)]
    pairs = [sinks[c:c + 2] for c in range(0, len(sinks), 2)]
    width = lambda pair: sum(w for _, w in pair)
    nxt = group(0, width(pairs[0]))
    for c, pair in enumerate(pairs):
        cur = nxt
        if c + 1 < len(pairs):
            nxt = group(2 * (c + 1), width(pairs[c + 1]))
        pair[0][0](cur[:, :GROUP_W])
        pair[1][0](cur[:, GROUP_W:])


def _proj(x3, norm1, wt, b_gates, inv_full, *, names, tm, pos_base, pos_mod):
    nb, s, _ = x3.shape
    nt = s // tm
    tok = lambda w, dt: (jax.ShapeDtypeStruct((nb, s, w), dt), pl.BlockSpec((1, tm, w), lambda b, i: (b, i, 0)))
    slab = lambda rows: (jax.ShapeDtypeStruct((nb, nt, rows, tm), _BF16),
                         pl.BlockSpec((1, 1, rows, tm), lambda b, i: (b, i, 0, 0)))
    const = lambda shape: pl.BlockSpec(shape, lambda b, i: (0,) * len(shape))
    kinds = {
        "q": tok(GROUP_W, _BF16), "kf": tok(GROUP_W, _F32), "kb": tok(GROUP_W, _BF16),
        "vf": (jax.ShapeDtypeStruct((nb, s * DA_HEADS, DA_V), _F32),
               pl.BlockSpec((1, tm * DA_HEADS, DA_V), lambda b, i: (b, i, 0))),
        "vt": slab(DA_HEADS * VT_ROWS),
        "mq": tok(GROUP_W, _BF16), "mk": tok(GROUP_W, _BF16), "mv": tok(GROUP_W, _BF16), "og": tok(GROUP_W, _BF16),
        "gc": tok(N_GATES, _F32),
        "gr": (jax.ShapeDtypeStruct((nb, N_GATES, s), _F32), pl.BlockSpec((1, N_GATES, tm), lambda b, i: (b, 0, i))),
        "mqt": slab(GROUP_W), "mvt": slab(ML_HEADS * VT_ROWS),
    }
    blocks = [_nbytes((tm, D_MODEL), _F32), _nbytes(wt.shape, _BF16)]
    blocks += [_nbytes(kinds[n][1].block_shape, kinds[n][0].dtype) for n in names]
    outs = pl.pallas_call(
        functools.partial(_proj_kernel, names=names, tm=tm, pos_base=pos_base, pos_mod=pos_mod),
        grid=(nb, nt),
        in_specs=[pl.BlockSpec((1, tm, D_MODEL), lambda b, i: (b, i, 0)), const((1, D_MODEL)),
                  const(wt.shape), const((1, HEAD_W)), const((1, HEAD_W))],
        out_specs=[kinds[n][1] for n in names], out_shape=[kinds[n][0] for n in names],
        compiler_params=_cparams(("parallel", "parallel"), blocks, 8 * _nbytes((tm, GROUP_W), _F32)), name="proj",
    )(x3, norm1, wt, b_gates, inv_full)
    return dict(zip(names, outs))


def _lambda(lam_ref, lam_init):
    lv = lam_ref[...]
    s1 = jnp.sum(lv[0:1] * lv[1:2], axis=-1, keepdims=True)
    s2 = jnp.sum(lv[2:3] * lv[3:4], axis=-1, keepdims=True)
    return jnp.exp(s1) - jnp.exp(s2) + lam_init


def _split_components(q):
    lane = lax.broadcasted_iota(jnp.int32, q.shape, 1)
    zero = jnp.zeros_like(q)
    return jnp.concatenate([jnp.where(lane < DA_QK, q, zero), jnp.where(lane >= DA_QK, q, zero)], axis=0)


def _attn_kernel(lam_ref, q_ref, k_ref, vt_ref, o_ref, acc_ref, s_ref, *, tq, tk, lam_init):
    i = pl.program_id(1)
    heads = range(DA_HEADS)
    qz = [_split_components(q_ref[0, :, h * HEAD_W:(h + 1) * HEAD_W]) for h in heads]
    acc_ref[...] = jnp.zeros_like(acc_ref)

    def scores(h, j):
        k_t = k_ref[0, pl.ds(pl.multiple_of(j * tk, tk), tk), h * HEAD_W:(h + 1) * HEAD_W]
        return _dot(k_t, qz[h], _NT)

    def step(j, ms, last):
        if last:
            kpos = j * tk + lax.broadcasted_iota(jnp.int32, (tk, 1), 0)
            qpos = i * tq + lax.broadcasted_iota(jnp.int32, (1, 2 * tq), 1) % tq
            visible = kpos < (qpos // CHUNK + 1) * CHUNK
        out = []
        s = s_ref[...]
        for h in heads:
            s_next = None
            if h + 1 < DA_HEADS:
                s_next = scores(h + 1, j)
            elif not last:
                s_next = scores(0, j + 1)
            if last:
                s = jnp.where(visible, s, NEG_BIG)
            m_new = jnp.maximum(ms[h], jnp.max(s, axis=0, keepdims=True))
            alpha = jnp.exp2(ms[h] - m_new)
            p = jnp.exp2(s - m_new).astype(_BF16)
            acc_ref[h] = alpha * acc_ref[h] + _dot(vt_ref[0, j, h * VT_ROWS:(h + 1) * VT_ROWS, :], p)
            out.append(m_new)
            s = s_next
        if not last:
            s_ref[...] = s
        return tuple(out)

    n_full = (i * tq) // tk
    s_ref[...] = scores(0, 0)
    init = tuple(jnp.full((1, 2 * tq), NEG_BIG, _F32) for _ in heads)
    ms = lax.fori_loop(0, n_full, lambda j, c: step(j, c, False), init)
    step(n_full, ms, True)

    lam = _lambda(lam_ref, lam_init)
    for h in heads:
        acc = acc_ref[h]
        o = acc[:DA_V] / acc[DA_V:DA_V + 1]
        o_ref[0, :, h * HEAD_W:(h + 1) * HEAD_W] = (o[:, :tq] - lam * o[:, tq:]).T.astype(_BF16)


def _attn_prompt(lamv, q, kb, vt, *, lam_init):
    nb, s, _ = q.shape
    nk, tk = vt.shape[1], vt.shape[3]
    tq = ATTN_TQ
    assert tk % tq == 0
    return pl.pallas_call(
        functools.partial(_attn_kernel, tq=tq, tk=tk, lam_init=lam_init),
        grid=(nb, s // tq),
        in_specs=[pl.BlockSpec((4, DA_QK), lambda b, i: (0, 0)),
                  pl.BlockSpec((1, tq, GROUP_W), lambda b, i: (b, i, 0)),
                  pl.BlockSpec((1, s, GROUP_W), lambda b, i: (b, 0, 0)),
                  pl.BlockSpec((1, nk, DA_HEADS * VT_ROWS, tk), lambda b, i: (b, 0, 0, 0))],
        out_specs=pl.BlockSpec((1, tq, GROUP_W), lambda b, i: (b, i, 0)),
        out_shape=jax.ShapeDtypeStruct((nb, s, GROUP_W), _BF16),
        scratch_shapes=[pltpu.VMEM((DA_HEADS, VT_ROWS, 2 * tq), _F32), pltpu.VMEM((tk, 2 * tq), _F32)],
        compiler_params=_cparams(
            ("parallel", "arbitrary"),
            [_nbytes((tq, GROUP_W), _BF16), _nbytes((s, GROUP_W), _BF16), _nbytes(vt.shape[1:], _BF16),
             _nbytes((tq, GROUP_W), _BF16)],
            _nbytes((DA_HEADS, VT_ROWS, 2 * tq), _F32) + 5 * _nbytes((tk, 2 * tq), _F32)), name="attn",
    )(lamv, q, kb, vt)


def _attn_sample_kernel(lam_ref, q_ref, kn_ref, vn_ref, kc_ref, vc_ref, o_ref, *, lq, past, lam_init):
    lam = _lambda(lam_ref, lam_init)
    for h in range(DA_HEADS):
        sl = slice(h * HEAD_W, (h + 1) * HEAD_W)
        qz = _split_components(q_ref[0, :, sl])
        kct = kc_ref[0, sl, :].astype(_BF16)
        vc = vc_ref[0, pl.ds(h, past, stride=DA_HEADS), :].astype(_BF16)
        s_c = _dot(qz, kct)
        s_n = _dot(qz, kn_ref[0, :, sl], _NT)
        m = jnp.maximum(jnp.max(s_c, axis=-1, keepdims=True), jnp.max(s_n, axis=-1, keepdims=True))
        p_c = jnp.exp2(s_c - m)
        p_n = jnp.exp2(s_n - m)
        l = jnp.sum(p_c, axis=-1, keepdims=True) + jnp.sum(p_n, axis=-1, keepdims=True)
        o = (_dot(p_c.astype(_BF16), vc) + _dot(p_n.astype(_BF16), vn_ref[0, :, sl])) / l
        o_ref[0, :, sl] = (o[:lq] - lam * o[lq:]).astype(_BF16)


def _attn_sample(lamv, q, kn, vn, kct, vc, *, lam_init):
    nb, lq, _ = q.shape
    past = kct.shape[2]
    new = pl.BlockSpec((1, lq, GROUP_W), lambda b: (b, 0, 0))
    return pl.pallas_call(
        functools.partial(_attn_sample_kernel, lq=lq, past=past, lam_init=lam_init),
        grid=(nb,),
        in_specs=[pl.BlockSpec((4, DA_QK), lambda b: (0, 0)), new, new, new,
                  pl.BlockSpec((1, GROUP_W, past), lambda b: (b, 0, 0)),
                  pl.BlockSpec((1, past * DA_HEADS, DA_V), lambda b: (b, 0, 0))],
        out_specs=new, out_shape=jax.ShapeDtypeStruct((nb, lq, GROUP_W), _BF16),
        compiler_params=_cparams(
            ("parallel",), [_nbytes((GROUP_W, past), _F32), _nbytes((past * DA_HEADS, DA_V), _F32)]
            + 4 * [_nbytes((lq, GROUP_W), _F32)], 8 * _nbytes((2 * lq, past), _F32)), name="attn_sample",
    )(lamv, q, kn, vn, kct, vc)


def _mlstm_kernel(q_ref, k_ref, v_ref, gc_ref, gr_ref, c0_ref, n0_ref, m0_ref, h_ref, c_ref, n_ref, m_ref,
                  c_sc, n_sc, m_sc, *, chunk, nchunks):
    j = pl.program_id(1)

    @pl.when(j == 0)
    def _():
        c_sc[...] = c0_ref[0]
        n_sc[...] = n0_ref[0]
        m_sc[...] = m0_ref[0]

    L = chunk
    r_i = lax.broadcasted_iota(jnp.int32, (L, L), 0)
    c_i = lax.broadcasted_iota(jnp.int32, (L, L), 1)
    causal = r_i >= c_i
    tril = causal.astype(_F32)
    triu = (r_i <= c_i).astype(_F32)
    hi = lax.Precision.HIGHEST

    for c in range(nchunks):
        rows = slice(c * L, (c + 1) * L)
        gcol = gc_ref[0, rows, :]
        grow = gr_ref[0, :, rows]
        fc_all = _dot(tril, gcol, precision=hi)
        fr_all = _dot(grow, triu, precision=hi)
        for hh in range(ML_HEADS):
            sl = slice(hh * HEAD_W, (hh + 1) * HEAD_W)
            m = m_sc[hh][0:1, 0:1]
            fc = fc_all[:, ML_HEADS + hh:ML_HEADS + hh + 1]
            fr = fr_all[ML_HEADS + hh:ML_HEADS + hh + 1, :]
            igc = gcol[:, hh:hh + 1]
            igr = grow[hh:hh + 1, :]
            a = jnp.where(causal, fc - fr + igr, -jnp.inf)
            b = fc + m
            m_t = jnp.maximum(b, jnp.max(a, axis=-1, keepdims=True))
            w = jnp.exp(a - m_t)
            sc = jnp.exp(b - m_t)
            q = q_ref[0, rows, sl]
            k = k_ref[0, rows, sl]
            v = v_ref[0, rows, sl]
            wqk = w * _dot(q, k, _NT)
            cmat = c_sc[hh]
            nrow = n_sc[hh:hh + 1, :]
            qn = jnp.sum(q.astype(_F32) * nrow, axis=-1, keepdims=True)
            num = _dot(wqk.astype(_BF16), v) + sc * _dot(q, cmat.astype(_BF16))
            den = jnp.sum(wqk, axis=-1, keepdims=True) + sc * qn
            h_ref[0, rows, sl] = (num / jnp.maximum(jnp.abs(den), jnp.exp(-m_t))).astype(_BF16)

            fl = fc[L - 1:L, :]
            g = fl - fc + igc
            m_new = jnp.maximum(fl + m, jnp.max(g, axis=0, keepdims=True))
            decay = jnp.exp(fl + m - m_new)
            kw = k.astype(_F32) * jnp.exp(g - m_new)
            c_sc[hh] = decay * cmat + _dot(kw.astype(_BF16), v, _TN)
            n_sc[hh:hh + 1, :] = decay * nrow + jnp.sum(kw, axis=0, keepdims=True)
            m_sc[hh] = jnp.broadcast_to(m_new, m_sc.shape[1:])

    @pl.when(j == pl.num_programs(1) - 1)
    def _():
        c_ref[0] = c_sc[...]
        n_ref[0] = n_sc[...]
        m_ref[0] = m_sc[...]


def _mlstm(q, k, v, gc, gr, c0, n0, m0, *, chunk, block):
    nb, s, _ = q.shape
    tok = lambda w: pl.BlockSpec((1, block, w), lambda b, j: (b, j, 0))
    cspec = pl.BlockSpec((1, ML_HEADS, ML_DK, ML_DV), lambda b, j: (b, 0, 0, 0))
    nspec = pl.BlockSpec((1, ML_HEADS, ML_DK), lambda b, j: (b, 0, 0))
    mspec = pl.BlockSpec((1, ML_HEADS, SUBLANES, HEAD_W), lambda b, j: (b, 0, 0, 0))
    return pl.pallas_call(
        functools.partial(_mlstm_kernel, chunk=chunk, nchunks=block // chunk),
        grid=(nb, s // block),
        in_specs=[tok(GROUP_W), tok(GROUP_W), tok(GROUP_W), tok(N_GATES),
                  pl.BlockSpec((1, N_GATES, block), lambda b, j: (b, 0, j)), cspec, nspec, mspec],
        out_specs=[tok(GROUP_W), cspec, nspec, mspec],
        out_shape=[jax.ShapeDtypeStruct((nb, s, GROUP_W), _BF16), jax.ShapeDtypeStruct(c0.shape, _F32),
                   jax.ShapeDtypeStruct(n0.shape, _F32), jax.ShapeDtypeStruct(m0.shape, _F32)],
        scratch_shapes=[pltpu.VMEM((ML_HEADS, ML_DK, ML_DV), _F32), pltpu.VMEM((ML_HEADS, ML_DK), _F32),
                        pltpu.VMEM((ML_HEADS, SUBLANES, HEAD_W), _F32)],
        compiler_params=_cparams(
            ("parallel", "arbitrary"), 5 * [_nbytes((block, GROUP_W), _F32)] + 2 * [_nbytes(c0.shape[1:], _F32)],
            2 * _nbytes(c0.shape[1:], _F32)), name="mlstm",
    )(q, k, v, gc, gr, c0, n0, m0)


def _mlstm_pairs_kernel(k_ref, qt_ref, vt_ref, gr_ref, st0_ref, m0_ref, h_ref, st_ref, m_ref, st_sc, m_sc,
                        *, block):
    j = pl.program_id(1)

    @pl.when(j == 0)
    def _():
        st_sc[...] = st0_ref[0]
        m_sc[...] = m0_ref[0]

    L, W = CHUNK, 2 * CHUNK
    g8 = gr_ref[0]
    pos = lax.broadcasted_iota(jnp.int32, (1, block), 1) % L
    roll = lambda x, sh: pltpu.roll(x, sh % block, 1)
    steps = [1 << b for b in range(L.bit_length() - 1)]

    def scan(x, combine, fill, reverse=False):
        for sh in steps:
            ok = (pos < L - sh) if reverse else (pos >= sh)
            x = combine(x, jnp.where(ok, roll(x, -sh if reverse else sh), fill))
        return x

    to_head_rows = lambda x: pltpu.roll(x, ML_HEADS, 0)
    f8 = to_head_rows(scan(g8, jnp.add, 0.0))
    d8 = g8 - f8
    cm8 = scan(d8, jnp.maximum, -jnp.inf)
    big_g = to_head_rows(scan(g8, jnp.add, 0.0, reverse=True) - g8) + g8
    mg8 = jnp.maximum(scan(big_g, jnp.maximum, -jnp.inf), scan(big_g, jnp.maximum, -jnp.inf, reverse=True))
    ws8 = jnp.exp(big_g - mg8)
    d_cols = d8.T

    s_i = lax.broadcasted_iota(jnp.int32, (W, W), 0)
    t_i = lax.broadcasted_iota(jnp.int32, (W, W), 1)
    allowed = (s_i // L == t_i // L) & (s_i <= t_i)
    lane_w = lax.broadcasted_iota(jnp.int32, (1, W), 1)
    first = lane_w < L

    sts = [st_sc[hh] for hh in range(ML_HEADS)]
    ms = [m_sc[hh][0:1, 0:1] for hh in range(ML_HEADS)]

    def independent(p, hh):
        lanes = slice(p * W, (p + 1) * W)
        feat = slice(hh * HEAD_W, (hh + 1) * HEAD_W)
        kp = k_ref[0, lanes, feat]
        qt = qt_ref[0, 0, feat, lanes]
        v1t = vt_ref[0, 0, hh * VT_ROWS:(hh + 1) * VT_ROWS, lanes]
        ws = ws8[hh:hh + 1, lanes]
        v1f = v1t.astype(_F32)
        return dict(
            p=p, hh=hh, qt=qt, v1t=v1t, qk_t=_dot(kp, qt),
            upd_a=_dot((v1f * jnp.where(first, ws, 0.0)).astype(_BF16), kp),
            upd_b=_dot((v1f * jnp.where(first, 0.0, ws)).astype(_BF16), kp))

    def dependent(u):
        p, hh, qt = u["p"], u["hh"], u["qt"]
        lanes = slice(p * W, (p + 1) * W)
        row = lambda x: x[hh:hh + 1, lanes]
        cm, f_row = row(cm8), row(f8)
        w_t = jnp.exp(jnp.where(allowed, d_cols[lanes, hh:hh + 1] - cm, -jnp.inf))
        intra = _dot(u["v1t"], (w_t * u["qk_t"]).astype(_BF16))
        st_a, m_a = sts[hh], ms[hh]
        fl_a, mg_a = f_row[:, L - 1:L], row(mg8)[:, 0:1]
        fl_b, mg_b = f_row[:, W - 1:W], row(mg8)[:, L:L + 1]
        m_b = jnp.maximum(fl_a + m_a, mg_a)
        st_b = jnp.exp(fl_a + m_a - m_b) * st_a + jnp.exp(mg_a - m_b) * u["upd_a"]
        m_c = jnp.maximum(fl_b + m_b, mg_b)
        sts[hh] = jnp.exp(fl_b + m_b - m_c) * st_b + jnp.exp(mg_b - m_c) * u["upd_b"]
        ms[hh] = m_c
        inter = jnp.where(first, _dot(st_a.astype(_BF16), qt), _dot(st_b.astype(_BF16), qt))
        m_prev = jnp.where(first, m_a, m_b)
        mt = jnp.maximum(m_prev, cm)
        tot = jnp.exp(cm - mt) * intra + jnp.exp(m_prev - mt) * inter
        den = jnp.maximum(jnp.abs(tot[ML_DV:ML_DV + 1]), jnp.exp(-(f_row + mt)))
        h_ref[0, lanes, hh * HEAD_W:(hh + 1) * HEAD_W] = (tot[:ML_DV] / den).T.astype(_BF16)

    units = [(p, hh) for p in range(block // W) for hh in range(ML_HEADS)]
    nxt = independent(*units[0])
    for n in range(len(units)):
        cur = nxt
        if n + 1 < len(units):
            nxt = independent(*units[n + 1])
        dependent(cur)
    for hh in range(ML_HEADS):
        st_sc[hh] = sts[hh]
        m_sc[hh] = jnp.broadcast_to(ms[hh], m_sc.shape[1:])

    @pl.when(j == pl.num_programs(1) - 1)
    def _():
        st_ref[0] = st_sc[...]
        m_ref[0] = m_sc[...]


def _mlstm_pairs(k, qt, vt, gr, st0, m0):
    nb, s, _ = k.shape
    block = qt.shape[3]
    nt = s // block
    state = pl.BlockSpec((1, ML_HEADS, VT_ROWS, ML_DK), lambda b, j: (b, 0, 0, 0))
    mspec = pl.BlockSpec((1, ML_HEADS, SUBLANES, HEAD_W), lambda b, j: (b, 0, 0, 0))
    tok = pl.BlockSpec((1, block, GROUP_W), lambda b, j: (b, j, 0))
    return pl.pallas_call(
        functools.partial(_mlstm_pairs_kernel, block=block),
        grid=(nb, nt),
        in_specs=[tok, pl.BlockSpec((1, 1, GROUP_W, block), lambda b, j: (b, j, 0, 0)),
                  pl.BlockSpec((1, 1, ML_HEADS * VT_ROWS, block), lambda b, j: (b, j, 0, 0)),
                  pl.BlockSpec((1, N_GATES, block), lambda b, j: (b, 0, j)), state, mspec],
        out_specs=[tok, state, mspec],
        out_shape=[jax.ShapeDtypeStruct((nb, s, GROUP_W), _BF16), jax.ShapeDtypeStruct(st0.shape, _F32),
                   jax.ShapeDtypeStruct(m0.shape, _F32)],
        scratch_shapes=[pltpu.VMEM((ML_HEADS, VT_ROWS, ML_DK), _F32), pltpu.VMEM((ML_HEADS, SUBLANES, HEAD_W), _F32)],
        compiler_params=_cparams(
            ("parallel", "arbitrary"),
            3 * [_nbytes((block, GROUP_W), _BF16)] + [_nbytes((ML_HEADS * VT_ROWS, block), _BF16)]
            + 2 * [_nbytes(st0.shape[1:], _F32)], 16 * _nbytes((VT_ROWS, 2 * CHUNK), _F32) * ML_HEADS), name="mlstm_pairs",
    )(k, qt, vt, gr, st0, m0)


def _mix_kernel(da_ref, ml_ref, og_ref, x_ref, subln_ref, wo_ref, g2_ref, wr_ref, br_ref, cnt0_ref,
                xmid_ref, xp_ref, idx_ref, gate_ref, rank_ref, cnt_ref, cnt_sc, upper_sc, *, tm, lam_init):
    step = pl.program_id(0)

    @pl.when(step == 0)
    def _():
        cnt_sc[...] = cnt0_ref[...]
        earlier = lax.broadcasted_iota(jnp.int32, (tm, tm), 0) < lax.broadcasted_iota(jnp.int32, (tm, tm), 1)
        upper_sc[...] = earlier.astype(_BF16)

    parts = []
    for h in range(DA_HEADS):
        d = da_ref[:, h * HEAD_W:(h + 1) * HEAD_W].astype(_F32)
        d = d * lax.rsqrt(jnp.mean(d * d, axis=-1, keepdims=True) + EPS) * subln_ref[...]
        parts.append((d * (1.0 - lam_init)).astype(_BF16))
    parts.append((og_ref[...].astype(_F32) * ml_ref[...].astype(_F32)).astype(_BF16))
    xm = x_ref[...] + _dot(jnp.concatenate(parts, axis=1), wo_ref[...])
    xmid_ref[...] = xm
    xn_f = xm * lax.rsqrt(jnp.mean(xm * xm, axis=-1, keepdims=True) + EPS) * g2_ref[...]
    xp_ref[...] = _pack_rows(xn_f)
    xn = xn_f.astype(_BF16)

    logits = _dot(wr_ref[...], xn, _NT) + br_ref[...]
    e_i = lax.broadcasted_iota(jnp.int32, logits.shape, 0)
    member = jnp.zeros(logits.shape, jnp.bool_)
    vals, idxs = [], []
    for _ in range(TOP_K):
        mx = jnp.max(logits, axis=0, keepdims=True)
        sel = jnp.min(jnp.where(logits == mx, e_i, N_EXPERTS), axis=0, keepdims=True)
        hit = e_i == sel
        member = member | hit
        logits = jnp.where(hit, -jnp.inf, logits)
        vals.append(mx)
        idxs.append(sel)
    ex = [jnp.exp(v - vals[0]) for v in vals]
    tot = ex[0] + ex[1] + ex[2] + ex[3]
    idx_ref[...] = jnp.concatenate(idxs, axis=0)
    gate_ref[...] = jnp.concatenate([e / tot for e in ex], axis=0)

    memf = member.astype(_F32)
    before = _dot(memf.astype(_BF16), upper_sc[...]) + cnt_sc[...]
    rank_ref[...] = jnp.concatenate(
        [jnp.sum(jnp.where(e_i == s, before, 0.0), axis=0, keepdims=True) for s in idxs], axis=0).astype(jnp.int32)
    cnt_sc[...] += jnp.sum(memf, axis=1, keepdims=True)
    cnt_ref[...] = cnt_sc[...]


def _mix(da, ml, og, x, subln, w_out, norm2, w_rt, b_r, cnt0, *, tm, lam_init):
    t = x.shape[0]
    tok = lambda w: pl.BlockSpec((tm, w), lambda i: (i, 0))
    const = lambda shape: pl.BlockSpec(shape, lambda i: (0,) * len(shape))
    lane_tok = pl.BlockSpec((TOP_K, tm), lambda i: (0, i))
    return pl.pallas_call(
        functools.partial(_mix_kernel, tm=tm, lam_init=lam_init),
        grid=(t // tm,),
        in_specs=[tok(GROUP_W), tok(GROUP_W), tok(GROUP_W), tok(D_MODEL), const((1, DA_V)),
                  const((D_MODEL, D_MODEL)), const((1, D_MODEL)), const((N_EXPERTS, D_MODEL)),
                  const((N_EXPERTS, 1)), const((N_EXPERTS, 1))],
        out_specs=[tok(D_MODEL), tok(PACK_W), lane_tok, lane_tok, lane_tok, const((N_EXPERTS, 1))],
        out_shape=[jax.ShapeDtypeStruct((t, D_MODEL), _F32), jax.ShapeDtypeStruct((t, PACK_W), jnp.int32),
                   jax.ShapeDtypeStruct((TOP_K, t), jnp.int32), jax.ShapeDtypeStruct((TOP_K, t), _F32),
                   jax.ShapeDtypeStruct((TOP_K, t), jnp.int32), jax.ShapeDtypeStruct((N_EXPERTS, 1), _F32)],
        scratch_shapes=[pltpu.VMEM((N_EXPERTS, 1), _F32), pltpu.VMEM((tm, tm), _BF16)],
        compiler_params=_cparams(
            ("arbitrary",),
            3 * [_nbytes((tm, GROUP_W), _BF16)] + 2 * [_nbytes((tm, D_MODEL), _F32)] + [_nbytes((tm, PACK_W), _F32)]
            + [_nbytes((D_MODEL, D_MODEL), _BF16)], 4 * _nbytes((tm, D_MODEL), _F32) + 2 * _nbytes((tm, tm), _F32)), name="mix",
    )(da, ml, og, x, subln, w_out, norm2, w_rt, b_r, cnt0)


def _moe_kernel(be_ref, nu_ref, nv_ref, slot_ref, nxt_ref, x_ref, wgu_hbm, bgu_ref, wd_hbm, bd_ref, y_ref,
                wgu_f32, wd_f32, wgu_sc, wd_sc, sems):
    j = pl.program_id(0)
    used = j < nu_ref[0]
    fresh = (j == 0) | (be_ref[j] != be_ref[jnp.maximum(j - 1, 0)])
    slot = slot_ref[j]

    def fetch(e, s):
        return (pltpu.make_async_copy(wgu_hbm.at[e], wgu_f32.at[s], sems.at[s, 0]),
                pltpu.make_async_copy(wd_hbm.at[e], wd_f32.at[s], sems.at[s, 1]))

    @pl.when(j == 0)
    def _():
        for cp in fetch(be_ref[0], slot):
            cp.start()

    @pl.when(fresh & used)
    def _():
        for cp in fetch(be_ref[j], slot):
            cp.wait()

        @pl.when(nxt_ref[j] >= 0)
        def _():
            for cp in fetch(nxt_ref[j], 1 - slot):
                cp.start()

        rows = CAST_ROWS
        def cast_gu(r, _):
            sl = pl.ds(pl.multiple_of(r * rows, rows), rows)
            wgu_sc[sl, :] = wgu_f32[slot, sl, :].astype(_BF16)
            return 0
        lax.fori_loop(0, D_MODEL // rows, cast_gu, 0)
        def cast_d(r, _):
            sl = pl.ds(pl.multiple_of(r * rows, rows), rows)
            wd_sc[sl, :] = wd_f32[slot, sl, :].astype(_BF16)
            return 0
        lax.fori_loop(0, D_FF // rows, cast_d, 0)

    @pl.when(j >= nu_ref[0])
    def _():
        y_ref[...] = jnp.zeros_like(y_ref)

    def ffn(row0):
        hi, lo = _unpack_rows(x_ref[row0:, :])
        x = jnp.concatenate([hi.astype(_BF16), lo.astype(_BF16)], axis=1)
        n_chunks = D_FF // FF_CHUNK

        def gate_up(c):
            lo = c * FF_CHUNK
            return (_dot(x, wgu_sc[:, lo:lo + FF_CHUNK]), _dot(x, wgu_sc[:, D_FF + lo:D_FF + lo + FF_CHUNK]))

        acc = None
        nxt = gate_up(0)
        for c in range(n_chunks):
            lo = c * FF_CHUNK
            gate, up = nxt
            if c + 1 < n_chunks:
                nxt = gate_up(c + 1)
            gate = jnp.minimum(gate + bgu_ref[0, :, lo:lo + FF_CHUNK], SWIGLU_LIMIT)
            up = jnp.clip(up + bgu_ref[0, :, D_FF + lo:D_FF + lo + FF_CHUNK], -SWIGLU_LIMIT, SWIGLU_LIMIT)
            act = (up + 1.0) * (gate * jax.nn.sigmoid(SWIGLU_ALPHA * gate))
            down = _dot(act.astype(_BF16), wd_sc[lo:lo + FF_CHUNK, :])
            acc = down if acc is None else acc + down
        y_ref[row0:, :] = _pack_rows(acc + bd_ref[0])

    half = MOE_BLOCK // 2
    @pl.when(used & (nv_ref[j] > half))
    def _():
        ffn(0)

    @pl.when(used & (nv_ref[j] <= half))
    def _():
        y_ref[:half, :] = jnp.zeros((half, PACK_W), jnp.int32)
        ffn(half)


def _moe(blk_e, n_used, blk_rows, blk_slot, blk_next, xb, w_gu, b_gu, w_d, b_d):
    n_blocks = xb.shape[0] // MOE_BLOCK
    last = lambda j, nu: jnp.minimum(j, nu[0] - 1)
    row_spec = pl.BlockSpec((MOE_BLOCK, PACK_W), lambda j, be, nu, nv, sl, nx: (last(j, nu), 0))
    bias_spec = lambda c: pl.BlockSpec((1, 1, c), lambda j, be, nu, nv, sl, nx: (be[last(j, nu)], 0, 0))
    hbm = pl.BlockSpec(memory_space=pl.ANY)
    grid_spec = pltpu.PrefetchScalarGridSpec(
        num_scalar_prefetch=5, grid=(n_blocks,),
        in_specs=[row_spec, hbm, bias_spec(2 * D_FF), hbm, bias_spec(D_MODEL)],
        out_specs=pl.BlockSpec((MOE_BLOCK, PACK_W), lambda j, be, nu, nv, sl, nx: (j, 0)),
        scratch_shapes=[pltpu.VMEM((2, D_MODEL, 2 * D_FF), _F32), pltpu.VMEM((2, D_FF, D_MODEL), _F32),
                        pltpu.VMEM((D_MODEL, 2 * D_FF), _BF16), pltpu.VMEM((D_FF, D_MODEL), _BF16),
                        pltpu.SemaphoreType.DMA((2, 2))])
    return pl.pallas_call(
        _moe_kernel, grid_spec=grid_spec, out_shape=jax.ShapeDtypeStruct(xb.shape, jnp.int32),
        compiler_params=_cparams(
            ("arbitrary",), 2 * [_nbytes((MOE_BLOCK, PACK_W), _F32)],
            2 * _nbytes((D_MODEL, 3 * D_FF), _F32) + _nbytes((D_MODEL, 3 * D_FF), _BF16)
            + 4 * _nbytes((MOE_BLOCK, D_MODEL), _F32)), name="moe",
    )(blk_e, n_used, blk_rows, blk_slot, blk_next, xb, w_gu, b_gu, w_d, b_d)


def _sc_mesh():
    return plsc.VectorSubcoreMesh(core_axis_name="c", subcore_axis_name="s")


def _sc_worker():
    return lax.axis_index("s") * SC_CORES + lax.axis_index("c")


def _sc_dispatch(sources, n_rows):
    n_src = len(sources)
    shapes = []
    for _, dest3 in sources:
        n_chunks, _, chunk = dest3.shape
        per_worker = n_chunks // SC_WORKERS
        assert per_worker * SC_WORKERS == n_chunks and chunk % SUBLANES == 0
        shapes.append((per_worker, chunk))

    def body(*refs):
        x_hbms, d_hbms, o_hbm = refs[:n_src], refs[n_src:2 * n_src], refs[2 * n_src]
        scratch = refs[2 * n_src + 1:]
        load_a, load_b, scat_a, scat_b = scratch[3 * n_src:]
        worker = _sc_worker()
        for i, (per_worker, chunk) in enumerate(shapes):
            x_hbm, idx_v, rows_a, rows_b = x_hbms[i], scratch[3 * i], scratch[3 * i + 1], scratch[3 * i + 2]
            pltpu.sync_copy(d_hbms[i].at[worker], idx_v)

            def load(ci, rows, sem):
                row0 = (worker * per_worker + ci) * chunk
                return pltpu.make_async_copy(x_hbm.at[pl.ds(pl.multiple_of(row0, SUBLANES), chunk)], rows, sem)

            def scatters(ci, rows, sem):
                return [pltpu.make_async_copy(rows, o_hbm.at[idx_v.at[ci * TOP_K + k]], sem) for k in range(TOP_K)]

            def start(copies):
                for cp in copies:
                    cp.start()

            def wait(copies):
                for cp in copies:
                    cp.wait()

            load(0, rows_a, load_a).start()

            @pl.loop(0, per_worker // 2)
            def _(p):
                ca, cb = 2 * p, 2 * p + 1
                load(ca, rows_a, load_a).wait()
                load(cb, rows_b, load_b).start()
                start(scatters(ca, rows_a, scat_a))
                load(cb, rows_b, load_b).wait()
                wait(scatters(ca, rows_a, scat_a))
                load(jnp.minimum(ca + 2, per_worker - 1), rows_a, load_a).start()
                start(scatters(cb, rows_b, scat_b))
                wait(scatters(cb, rows_b, scat_b))

            last = per_worker - 1
            load(last, rows_a, load_a).wait()
            if per_worker % 2:
                start(scatters(last, rows_a, scat_a))
                wait(scatters(last, rows_a, scat_a))

    dma = pltpu.SemaphoreType.DMA
    scratch_types = []
    for per_worker, chunk in shapes:
        scratch_types += [pltpu.VMEM((per_worker * TOP_K, chunk), jnp.int32), pltpu.VMEM((chunk, PACK_W), jnp.int32),
                          pltpu.VMEM((chunk, PACK_W), jnp.int32)]
    tables = [dest3.reshape(SC_WORKERS, pw * TOP_K, chunk) for (_, dest3), (pw, chunk) in zip(sources, shapes)]
    return pl.kernel(
        body, out_type=jax.ShapeDtypeStruct((n_rows, PACK_W), jnp.int32), mesh=_sc_mesh(),
        scratch_types=scratch_types + [dma, dma, dma, dma],
        name="sc_dispatch")(*[xp for xp, _ in sources], *tables)


def _sc_gather(table, idx2):
    n_chunks, chunk = idx2.shape
    per_worker = n_chunks // SC_WORKERS
    assert per_worker * SC_WORKERS == n_chunks and chunk % SUBLANES == 0

    def body(t_hbm, i_hbm, o_hbm, idx_v, rows_a, rows_b, gath_a, gath_b, put_a, put_b):
        worker = _sc_worker()
        pltpu.sync_copy(i_hbm.at[worker], idx_v)

        def gather(ci, rows, sem):
            return pltpu.make_async_copy(t_hbm.at[idx_v.at[ci]], rows, sem)

        def put(ci, rows, sem):
            row0 = (worker * per_worker + ci) * chunk
            return pltpu.make_async_copy(rows, o_hbm.at[pl.ds(pl.multiple_of(row0, SUBLANES), chunk)], sem)

        gather(0, rows_a, gath_a).start()

        @pl.loop(0, per_worker // 2)
        def _(p):
            ca, cb = 2 * p, 2 * p + 1
            gather(ca, rows_a, gath_a).wait()
            gather(cb, rows_b, gath_b).start()
            put(ca, rows_a, put_a).start()
            gather(cb, rows_b, gath_b).wait()
            put(ca, rows_a, put_a).wait()
            gather(jnp.minimum(ca + 2, per_worker - 1), rows_a, gath_a).start()
            put(cb, rows_b, put_b).start()
            put(cb, rows_b, put_b).wait()

        last = per_worker - 1
        gather(last, rows_a, gath_a).wait()
        if per_worker % 2:
            put(last, rows_a, put_a).start()
            put(last, rows_a, put_a).wait()

    dma = pltpu.SemaphoreType.DMA
    return pl.kernel(
        body, out_type=jax.ShapeDtypeStruct((n_chunks * chunk, PACK_W), jnp.int32), mesh=_sc_mesh(),
        scratch_types=[pltpu.VMEM((per_worker, chunk), jnp.int32), pltpu.VMEM((chunk, PACK_W), jnp.int32),
                       pltpu.VMEM((chunk, PACK_W), jnp.int32), dma, dma, dma, dma],
        name="sc_gather")(table, idx2.reshape(SC_WORKERS, per_worker, chunk))


def _final_kernel(x_ref, y_ref, g_ref, nf_ref, o_ref):
    g_rows = g_ref[...]
    g = jnp.concatenate([g_rows, jnp.zeros((SUBLANES - TOP_K, g_rows.shape[1]), _F32)], axis=0).T
    hi = jnp.zeros((x_ref.shape[0], PACK_W), _F32)
    lo = jnp.zeros((x_ref.shape[0], PACK_W), _F32)
    for k in range(TOP_K):
        y_hi, y_lo = _unpack_rows(y_ref[k])
        hi = hi + g[:, k:k + 1] * y_hi
        lo = lo + g[:, k:k + 1] * y_lo
    x = x_ref[...] + jnp.concatenate([hi, lo], axis=1)
    o_ref[...] = x * lax.rsqrt(jnp.mean(x * x, axis=-1, keepdims=True) + EPS) * nf_ref[...]


def _final(xmid, yg, gates, norm_f, *, tm, first_block):
    t = xmid.shape[0]
    return pl.pallas_call(
        _final_kernel, grid=(t // tm,),
        in_specs=[pl.BlockSpec((tm, D_MODEL), lambda i: (i, 0)),
                  pl.BlockSpec((TOP_K, tm, PACK_W), lambda i: (0, i + first_block, 0)),
                  pl.BlockSpec((TOP_K, tm), lambda i: (0, i)),
                  pl.BlockSpec((1, D_MODEL), lambda i: (0, 0))],
        out_specs=pl.BlockSpec((tm, D_MODEL), lambda i: (i, 0)),
        out_shape=jax.ShapeDtypeStruct((t, D_MODEL), _F32),
        compiler_params=_cparams(
            ("parallel",), 2 * [_nbytes((tm, D_MODEL), _F32)] + [_nbytes((TOP_K, tm, PACK_W), _F32)],
            2 * _nbytes((tm, D_MODEL), _F32)), name="final",
    )(xmid, yg, gates, norm_f)


def kernel(x_prompt, x_sample, cache_k, cache_v, state_C, state_n, state_m, norm1, w_in, b_igate, b_fgate,
           lambda_q1, lambda_k1, lambda_q2, lambda_k2, subln, w_out, norm2, w_router, b_router, w_gate_up,
           b_gate_up, w_down, b_down, norm_f):
    nb, s, _ = x_prompt.shape
    db, ls, _ = x_sample.shape
    past = cache_k.shape[2]
    depth = w_in.shape[0]
    assert depth == 1 and s % PROJ_TILE == 0 and (db * ls) % SUBLANES == 0
    l = 0
    lam_init = 0.8 - 0.6 * math.exp(-0.3 * l)
    t_p, t_s = nb * s, db * ls

    w_nat = jnp.pad(w_in[l].T, ((0, HEAD_W - N_GATES), (0, 0))).astype(_BF16)
    b_gates = jnp.pad(jnp.concatenate([b_igate[l], b_fgate[l]]).astype(_F32), (0, HEAD_W - N_GATES))[None]
    inv = ROPE_THETA ** (-jnp.arange(0, DA_QK, 2, dtype=_F32) / DA_QK)
    inv_full = jnp.tile(inv, HEAD_W // (DA_QK // 2))[None]
    lamv = jnp.stack([lambda_q1[l], lambda_k1[l], lambda_q2[l], lambda_k2[l]]).astype(_F32)
    g1 = norm1[l][None]

    pp = _proj(x_prompt, g1, w_nat, b_gates, inv_full, tm=PROJ_TILE, pos_base=0, pos_mod=s,
               names=("q", "kf", "kb", "vf", "vt", "mk", "og", "gr", "mqt", "mvt"))
    ps = _proj(x_sample.reshape(1, t_s, D_MODEL), g1, w_nat, b_gates, inv_full, tm=t_s, pos_base=past, pos_mod=ls,
               names=("q", "kf", "kb", "vf", "mq", "mk", "mv", "og", "gc", "gr"))
    kf_p, vf_p, kf_s, vf_s = pp["kf"], pp["vf"], ps["kf"], ps["vf"]

    da_p = _attn_prompt(lamv, pp["q"], pp["kb"], pp["vt"], lam_init=lam_init)
    per_b = lambda a: a.reshape(db, ls, a.shape[-1])
    kct = jnp.transpose(cache_k[l], (0, 2, 3, 4, 1)).reshape(db, GROUP_W, past)
    vc = cache_v[l].reshape(db, past * DA_HEADS, DA_V)
    vn = vf_s.reshape(db, ls, GROUP_W).astype(_BF16)
    da_s = _attn_sample(lamv, per_b(ps["q"]), per_b(ps["kb"]), vn, kct, vc, lam_init=lam_init)

    assert PROJ_TILE % (2 * CHUNK) == 0
    h_p, st_p, m_p = _mlstm_pairs(pp["mk"], pp["mqt"], pp["mvt"], pp["gr"],
                                  jnp.zeros((nb, ML_HEADS, VT_ROWS, ML_DK), _F32),
                                  jnp.zeros((nb, ML_HEADS, SUBLANES, HEAD_W), _F32))
    gr_sb = ps["gr"].reshape(8, db, ls).transpose(1, 0, 2)
    m0_s = jnp.broadcast_to(state_m[l].astype(_F32)[:, :, None, None], (db, ML_HEADS, SUBLANES, HEAD_W))
    h_s, c_s, n_s, m_s = _mlstm(per_b(ps["mq"]), per_b(ps["mk"]), per_b(ps["mv"]), per_b(ps["gc"]), gr_sb,
                                state_C[l].astype(_F32), state_n[l].astype(_F32), m0_s, chunk=ls, block=ls)
    c_p, n_p, mm_p = jnp.swapaxes(st_p[:, :, :ML_DV, :], -1, -2), st_p[:, :, ML_DV, :], m_p[:, :, 0, 0]
    mm_s = m_s[:, :, 0, 0]

    wo = w_out[l].astype(_BF16)
    w_rt = w_router[l].T.astype(_BF16)
    b_r = b_router[l].astype(_F32)[:, None]
    mix = functools.partial(_mix, subln=subln[l][None], w_out=wo, norm2=norm2[l][None], w_rt=w_rt, b_r=b_r,
                            lam_init=lam_init)
    flat = lambda a: a.reshape(-1, a.shape[-1])
    t_all = t_p + t_s
    assert t_p % t_s == 0
    xm_p, xp_p, idx_p, gate_p, rank_p, cnt_p = mix(
        flat(da_p), flat(h_p), flat(pp["og"]), flat(x_prompt), cnt0=jnp.zeros((N_EXPERTS, 1), _F32), tm=PROJ_TILE)
    xm_s, xp_s, idx_s, gate_s, rank_s, cnt = mix(
        flat(da_s), flat(h_s), flat(ps["og"]), flat(x_sample), cnt0=cnt_p, tm=t_s)

    rows = t_all * TOP_K
    n_blocks = -(-rows // MOE_BLOCK) + N_EXPERTS
    counts = cnt[:, 0].astype(jnp.int32)
    padded = (counts + MOE_BLOCK - 1) // MOE_BLOCK * MOE_BLOCK
    pend = jnp.cumsum(padded)
    pstart = pend - counts
    experts = jnp.arange(N_EXPERTS, dtype=jnp.int32)[:, None, None]
    first_row = lambda idx: jnp.sum(jnp.where(idx[None] == experts, pstart[:, None, None], 0), axis=0)
    dest_p, dest_s = first_row(idx_p) + rank_p, first_row(idx_s) + rank_s
    n_used = (pend[-1] // MOE_BLOCK).astype(jnp.int32)[None]
    blk_lo = jnp.arange(n_blocks, dtype=jnp.int32) * MOE_BLOCK
    blk_e = jnp.minimum(jnp.sum((pend[None, :] <= blk_lo[:, None]).astype(jnp.int32), axis=1), N_EXPERTS - 1)
    row_lo = jnp.sum(jnp.where(blk_e[:, None] == experts[:, 0].T, pstart[None, :], 0), axis=1)
    blk_rows = jnp.where(blk_lo < pend[-1], jnp.clip(blk_lo + MOE_BLOCK - row_lo, 0, MOE_BLOCK), 0).astype(jnp.int32)

    def chunked(dest):
        chunk = min(DISPATCH_CHUNK, dest.shape[1] // SC_WORKERS)
        return dest.reshape(TOP_K, dest.shape[1] // chunk, chunk).transpose(1, 0, 2)

    assert rows % (SC_WORKERS * COMBINE_CHUNK) == 0
    xb = _sc_dispatch([(xp_p, chunked(dest_p)), (xp_s, chunked(dest_s))], n_blocks * MOE_BLOCK)
    dest = jnp.concatenate([dest_p, dest_s], axis=1)
    first_blk = jnp.concatenate([jnp.ones((1,), jnp.bool_), blk_e[1:] != blk_e[:-1]])
    blk_slot = ((jnp.cumsum(first_blk.astype(jnp.int32)) - 1) % 2).astype(jnp.int32)
    after = jnp.sum(jnp.where(blk_e[:, None] == experts[:, 0].T, pend[None, :], 0), axis=1) // MOE_BLOCK
    blk_next = jnp.where(after < n_used[0], blk_e[jnp.minimum(after, n_blocks - 1)], -1).astype(jnp.int32)
    yb = _moe(blk_e, n_used, blk_rows, blk_slot, blk_next, xb, w_gate_up[l], b_gate_up[l][:, None, :], w_down[l], b_down[l][:, None, :])
    yg = _sc_gather(yb, dest.reshape(rows // COMBINE_CHUNK, COMBINE_CHUNK)).reshape(TOP_K, t_all, PACK_W)
    nf = norm_f[None]
    y_p = _final(xm_p, yg, gate_p, nf, tm=FINAL_TILE, first_block=0).reshape(nb, s, D_MODEL)
    y_s = _final(xm_s, yg, gate_s, nf, tm=t_s, first_block=t_p // t_s).reshape(db, ls, D_MODEL)

    st = lambda a: a[None]
    return (y_p, y_s,
            st(kf_p.reshape(nb, s, DA_HEADS, 2, DA_QK)), st(vf_p.reshape(nb, s, DA_HEADS, DA_V)),
            st(c_p), st(n_p), st(mm_p),
            st(kf_s.reshape(db, ls, DA_HEADS, 2, DA_QK)), st(vf_s.reshape(db, ls, DA_HEADS, DA_V)),
            st(c_s.astype(state_C.dtype)), st(n_s.astype(state_n.dtype)), st(mm_s.astype(state_m.dtype)))
```

```python
import functools
import math

import jax
import jax.numpy as jnp
from jax import lax
from jax.experimental import pallas as pl
from jax.experimental.pallas import tpu as pltpu
from jax.experimental.pallas import tpu_sc as plsc

D_MODEL = 1024
CHUNK = 64
DA_HEADS = 4
DA_QK = 64
DA_V = 128
ML_HEADS = 4
ML_DK = 128
ML_DV = 128
HEAD_W = 128
SUBLANES = 8
N_GATES = 2 * ML_HEADS
CAST_ROWS = 128
GROUP_W = 512
ROPE_THETA = 10000.0
N_EXPERTS = 32
TOP_K = 4
D_FF = 1024
SWIGLU_LIMIT = 7.0
SWIGLU_ALPHA = 1.702
EPS = 1e-6
NEG_BIG = -1e30

LOG2E = math.log2(math.e)
VT_ROWS = 144
PROJ_TILE = 512
FINAL_TILE = 1024
ATTN_TQ = 512
MOE_BLOCK = 512
FF_CHUNK = 512
PACK_W = D_MODEL // 2
SC_CORES = 2
SC_SUBCORES = 16
SC_WORKERS = SC_CORES * SC_SUBCORES
DISPATCH_CHUNK = 64
COMBINE_CHUNK = 80
VMEM_BYTES = 64 * 1024 * 1024

_F32 = jnp.float32
_BF16 = jnp.bfloat16


def _nbytes(shape, dtype):
    return math.prod(shape) * jnp.dtype(dtype).itemsize


def _cparams(sem, blocks, resident=0):
    need = 2 * sum(blocks) + resident
    assert need <= VMEM_BYTES, need
    return pltpu.CompilerParams(dimension_semantics=sem, vmem_limit_bytes=need)


def _dot(a, b, dims=(((1,), (0,)), ((), ())), precision=None):
    return lax.dot_general(a, b, dims, precision=precision, preferred_element_type=_F32)


_NT = (((1,), (1,)), ((), ()))
_TN = (((0,), (0,)), ((), ()))


def _pack_rows(x):
    half = x.shape[1] // 2
    bits = lambda v: lax.bitcast_convert_type(v.astype(_BF16).astype(_F32), jnp.int32)
    return bits(x[:, :half]) | lax.shift_right_logical(bits(x[:, half:]), 16)


def _unpack_rows(p):
    hi = lax.bitcast_convert_type(p & jnp.int32(-65536), _F32)
    lo = lax.bitcast_convert_type(lax.shift_left(p, 16), _F32)
    return hi, lo


def _chunk_gate_rows(g8, block):
    L = CHUNK
    pos = lax.broadcasted_iota(jnp.int32, (1, block), 1) % L
    roll = lambda x, sh: pltpu.roll(x, sh % block, 1)
    steps = [1 << b for b in range(L.bit_length() - 1)]

    def scan(x, combine, fill, reverse=False):
        for sh in steps:
            ok = (pos < L - sh) if reverse else (pos >= sh)
            x = combine(x, jnp.where(ok, roll(x, -sh if reverse else sh), fill))
        return x

    to_head_rows = lambda x: pltpu.roll(x, ML_HEADS, 0)
    f8 = to_head_rows(scan(g8, jnp.add, 0.0))
    d8 = g8 - f8
    cm8 = scan(d8, jnp.maximum, -jnp.inf)
    big_g = to_head_rows(scan(g8, jnp.add, 0.0, reverse=True) - g8) + g8
    mg8 = jnp.maximum(scan(big_g, jnp.maximum, -jnp.inf), scan(big_g, jnp.maximum, -jnp.inf, reverse=True))
    return jnp.concatenate([f8, d8, cm8, mg8, jnp.exp(big_g - mg8)], axis=0)


def _proj_kernel(x_ref, g_ref, wt_ref, bg_ref, inv_ref, *out_refs, names, tm, pos_base, pos_mod):
    out = dict(zip(names, out_refs))
    x = x_ref[0]
    xn = (x * lax.rsqrt(jnp.mean(x * x, axis=-1, keepdims=True) + EPS) * g_ref[...]).astype(_BF16)

    row = pl.program_id(1) * tm + lax.broadcasted_iota(jnp.int32, (tm, 1), 0)
    pos = (pos_base + row % pos_mod).astype(_F32)
    ang = pos * inv_ref[...]
    cos = jnp.cos(ang)
    sin = jnp.sin(ang)
    lane = lax.broadcasted_iota(jnp.int32, (1, HEAD_W), 1)
    first = (lane % DA_QK) < (DA_QK // 2)
    sin = jnp.where(first, -sin, sin)

    def rope(z):
        rot = jnp.where(first, pltpu.roll(z, HEAD_W - DA_QK // 2, 1), pltpu.roll(z, DA_QK // 2, 1))
        return z * cos + rot * sin

    def group(c, width=GROUP_W):
        return _dot(xn, wt_ref[c * GROUP_W:c * GROUP_W + width, :], _NT)

    def put_q(zq):
        for h in range(DA_HEADS):
            sl = slice(h * HEAD_W, (h + 1) * HEAD_W)
            out["q"][0, :, sl] = (rope(zq[:, sl]) * (DA_QK ** -0.5 * LOG2E)).astype(_BF16)

    def put_k(zk):
        for h in range(DA_HEADS):
            sl = slice(h * HEAD_W, (h + 1) * HEAD_W)
            rk = rope(zk[:, sl])
            out["kf"][0, :, sl] = rk
            out["kb"][0, :, sl] = rk.astype(_BF16)

    def with_ones_rows(zt):
        ones = jnp.ones((VT_ROWS - DA_V, tm), _BF16)
        return jnp.concatenate(
            [part for h in range(DA_HEADS) for part in (zt[h * DA_V:(h + 1) * DA_V], ones)], axis=0)

    def put_v(zv):
        for h in range(DA_HEADS):
            out["vf"][0, pl.ds(h, tm, stride=DA_HEADS), :] = zv[:, h * DA_V:(h + 1) * DA_V]
        if "vt" in out:
            out["vt"][0, 0] = with_ones_rows(zv.T.astype(_BF16))

    def put_mq(z):
        if "mq" in out:
            out["mq"][0] = z.astype(_BF16)
        if "mqt" in out:
            out["mqt"][0, 0] = z.T.astype(_BF16)

    def put_mk(z):
        out["mk"][0] = (z * (ML_DK ** -0.5)).astype(_BF16)

    def put_mv(z):
        if "mv" in out:
            out["mv"][0] = z.astype(_BF16)
        if "mvt" in out:
            out["mvt"][0, 0] = with_ones_rows(z.T.astype(_BF16))

    def put_mo(z):
        out["og"][0] = jax.nn.sigmoid(z).astype(_BF16)

    def put_gates(z):
        zg = z + bg_ref[...]
        lane_g = lax.broadcasted_iota(jnp.int32, zg.shape, 1)
        logsig = jnp.minimum(zg, 0.0) - jnp.log(1.0 + jnp.exp(-jnp.abs(zg)))
        gates = jnp.where(lane_g < ML_HEADS, zg, logsig)
        if "gc" in out:
            out["gc"][0] = gates[:, :N_GATES]
        g8 = gates.T[:N_GATES, :]
        if "gr" in out:
            out["gr"][0] = g8
        if "gs" in out:
            out["gs"][0] = _chunk_gate_rows(g8, tm)

    sinks = [(put_q, GROUP_W), (put_k, GROUP_W), (put_v, GROUP_W), (put_mq, GROUP_W), (put_mk, GROUP_W),
             (put_mv, GROUP_W), (put_mo, GROUP_W), (put_gates, HEAD_W)]
    pairs = [(c, sinks[c:c + 2]) for c in range(0, len(sinks), 2)]
    pairs = pairs[-1:] + pairs[:-1]
    width = lambda pair: sum(w for _, w in pair)
    nxt = group(pairs[0][0], width(pairs[0][1]))
    for n, (_, pair) in enumerate(pairs):
        cur = nxt
        if n + 1 < len(pairs):
            nxt = group(pairs[n + 1][0], width(pairs[n + 1][1]))
        pair[0][0](cur[:, :GROUP_W])
        pair[1][0](cur[:, GROUP_W:])


def _proj(x3, norm1, wt, b_gates, inv_full, *, names, tm, pos_base, pos_mod):
    nb, s, _ = x3.shape
    nt = s // tm
    tok = lambda w, dt: (jax.ShapeDtypeStruct((nb, s, w), dt), pl.BlockSpec((1, tm, w), lambda b, i: (b, i, 0)))
    slab = lambda rows: (jax.ShapeDtypeStruct((nb, nt, rows, tm), _BF16),
                         pl.BlockSpec((1, 1, rows, tm), lambda b, i: (b, i, 0, 0)))
    const = lambda shape: pl.BlockSpec(shape, lambda b, i: (0,) * len(shape))
    kinds = {
        "q": tok(GROUP_W, _BF16), "kf": tok(GROUP_W, _F32), "kb": tok(GROUP_W, _BF16),
        "vf": (jax.ShapeDtypeStruct((nb, s * DA_HEADS, DA_V), _F32),
               pl.BlockSpec((1, tm * DA_HEADS, DA_V), lambda b, i: (b, i, 0))),
        "vt": slab(DA_HEADS * VT_ROWS),
        "mq": tok(GROUP_W, _BF16), "mk": tok(GROUP_W, _BF16), "mv": tok(GROUP_W, _BF16), "og": tok(GROUP_W, _BF16),
        "gc": tok(N_GATES, _F32),
        "gr": (jax.ShapeDtypeStruct((nb, N_GATES, s), _F32), pl.BlockSpec((1, N_GATES, tm), lambda b, i: (b, 0, i))),
        "gs": (jax.ShapeDtypeStruct((nb, 5 * N_GATES, s), _F32),
               pl.BlockSpec((1, 5 * N_GATES, tm), lambda b, i: (b, 0, i))),
        "mqt": slab(GROUP_W), "mvt": slab(ML_HEADS * VT_ROWS),
    }
    blocks = [_nbytes((tm, D_MODEL), _F32), _nbytes(wt.shape, _BF16)]
    blocks += [_nbytes(kinds[n][1].block_shape, kinds[n][0].dtype) for n in names]
    outs = pl.pallas_call(
        functools.partial(_proj_kernel, names=names, tm=tm, pos_base=pos_base, pos_mod=pos_mod),
        grid=(nb, nt),
        in_specs=[pl.BlockSpec((1, tm, D_MODEL), lambda b, i: (b, i, 0)), const((1, D_MODEL)),
                  const(wt.shape), const((1, HEAD_W)), const((1, HEAD_W))],
        out_specs=[kinds[n][1] for n in names], out_shape=[kinds[n][0] for n in names],
        compiler_params=_cparams(("parallel", "parallel"), blocks, 8 * _nbytes((tm, GROUP_W), _F32)), name="proj",
    )(x3, norm1, wt, b_gates, inv_full)
    return dict(zip(names, outs))


def _lambda(lam_ref, lam_init):
    lv = lam_ref[...]
    s1 = jnp.sum(lv[0:1] * lv[1:2], axis=-1, keepdims=True)
    s2 = jnp.sum(lv[2:3] * lv[3:4], axis=-1, keepdims=True)
    return jnp.exp(s1) - jnp.exp(s2) + lam_init


def _split_components(q):
    lane = lax.broadcasted_iota(jnp.int32, q.shape, 1)
    zero = jnp.zeros_like(q)
    return jnp.concatenate([jnp.where(lane < DA_QK, q, zero), jnp.where(lane >= DA_QK, q, zero)], axis=0)


def _attn_kernel(lam_ref, q_ref, k_ref, vt_ref, o_ref, acc_ref, s_ref, *, tq, tk, lam_init):
    i = pl.program_id(1)
    heads = range(DA_HEADS)
    qz = [_split_components(q_ref[0, :, h * HEAD_W:(h + 1) * HEAD_W]) for h in heads]
    acc_ref[...] = jnp.zeros_like(acc_ref)

    def scores(h, j):
        k_t = k_ref[0, pl.ds(pl.multiple_of(j * tk, tk), tk), h * HEAD_W:(h + 1) * HEAD_W]
        return _dot(k_t, qz[h], _NT)

    def step(j, ms, last):
        if last:
            kpos = j * tk + lax.broadcasted_iota(jnp.int32, (tk, 1), 0)
            qpos = i * tq + lax.broadcasted_iota(jnp.int32, (1, 2 * tq), 1) % tq
            visible = kpos < (qpos // CHUNK + 1) * CHUNK
        out = []
        s = s_ref[...]
        for h in heads:
            s_next = None
            if h + 1 < DA_HEADS:
                s_next = scores(h + 1, j)
            elif not last:
                s_next = scores(0, j + 1)
            if last:
                s = jnp.where(visible, s, NEG_BIG)
            m_new = jnp.maximum(ms[h], jnp.max(s, axis=0, keepdims=True))
            alpha = jnp.exp2(ms[h] - m_new)
            p = jnp.exp2(s - m_new).astype(_BF16)
            acc_ref[h] = alpha * acc_ref[h] + _dot(vt_ref[0, j, h * VT_ROWS:(h + 1) * VT_ROWS, :], p)
            out.append(m_new)
            s = s_next
        if not last:
            s_ref[...] = s
        return tuple(out)

    n_full = (i * tq) // tk
    s_ref[...] = scores(0, 0)
    init = tuple(jnp.full((1, 2 * tq), NEG_BIG, _F32) for _ in heads)
    ms = lax.fori_loop(0, n_full, lambda j, c: step(j, c, False), init)
    step(n_full, ms, True)

    lam = _lambda(lam_ref, lam_init)
    for h in heads:
        acc = acc_ref[h]
        o = acc[:DA_V] / acc[DA_V:DA_V + 1]
        o_ref[0, :, h * HEAD_W:(h + 1) * HEAD_W] = (o[:, :tq] - lam * o[:, tq:]).T.astype(_BF16)


def _attn_prompt(lamv, q, kb, vt, *, lam_init):
    nb, s, _ = q.shape
    nk, tk = vt.shape[1], vt.shape[3]
    tq = ATTN_TQ
    assert tk % tq == 0
    return pl.pallas_call(
        functools.partial(_attn_kernel, tq=tq, tk=tk, lam_init=lam_init),
        grid=(nb, s // tq),
        in_specs=[pl.BlockSpec((4, DA_QK), lambda b, i: (0, 0)),
                  pl.BlockSpec((1, tq, GROUP_W), lambda b, i: (b, i, 0)),
                  pl.BlockSpec((1, s, GROUP_W), lambda b, i: (b, 0, 0)),
                  pl.BlockSpec((1, nk, DA_HEADS * VT_ROWS, tk), lambda b, i: (b, 0, 0, 0))],
        out_specs=pl.BlockSpec((1, tq, GROUP_W), lambda b, i: (b, i, 0)),
        out_shape=jax.ShapeDtypeStruct((nb, s, GROUP_W), _BF16),
        scratch_shapes=[pltpu.VMEM((DA_HEADS, VT_ROWS, 2 * tq), _F32), pltpu.VMEM((tk, 2 * tq), _F32)],
        compiler_params=_cparams(
            ("parallel", "arbitrary"),
            [_nbytes((tq, GROUP_W), _BF16), _nbytes((s, GROUP_W), _BF16), _nbytes(vt.shape[1:], _BF16),
             _nbytes((tq, GROUP_W), _BF16)],
            _nbytes((DA_HEADS, VT_ROWS, 2 * tq), _F32) + 5 * _nbytes((tk, 2 * tq), _F32)), name="attn",
    )(lamv, q, kb, vt)


def _attn_sample_kernel(lam_ref, q_ref, kn_ref, vn_ref, kc_ref, vc_ref, o_ref, *, lq, past, lam_init):
    lam = _lambda(lam_ref, lam_init)
    for h in range(DA_HEADS):
        sl = slice(h * HEAD_W, (h + 1) * HEAD_W)
        qz = _split_components(q_ref[0, :, sl])
        kct = kc_ref[0, sl, :].astype(_BF16)
        vc = vc_ref[0, pl.ds(h, past, stride=DA_HEADS), :].astype(_BF16)
        s_c = _dot(qz, kct)
        s_n = _dot(qz, kn_ref[0, :, sl], _NT)
        m = jnp.maximum(jnp.max(s_c, axis=-1, keepdims=True), jnp.max(s_n, axis=-1, keepdims=True))
        p_c = jnp.exp2(s_c - m)
        p_n = jnp.exp2(s_n - m)
        l = jnp.sum(p_c, axis=-1, keepdims=True) + jnp.sum(p_n, axis=-1, keepdims=True)
        o = (_dot(p_c.astype(_BF16), vc) + _dot(p_n.astype(_BF16), vn_ref[0, :, sl])) / l
        o_ref[0, :, sl] = (o[:lq] - lam * o[lq:]).astype(_BF16)


def _attn_sample(lamv, q, kn, vn, kct, vc, *, lam_init):
    nb, lq, _ = q.shape
    past = kct.shape[2]
    new = pl.BlockSpec((1, lq, GROUP_W), lambda b: (b, 0, 0))
    return pl.pallas_call(
        functools.partial(_attn_sample_kernel, lq=lq, past=past, lam_init=lam_init),
        grid=(nb,),
        in_specs=[pl.BlockSpec((4, DA_QK), lambda b: (0, 0)), new, new, new,
                  pl.BlockSpec((1, GROUP_W, past), lambda b: (b, 0, 0)),
                  pl.BlockSpec((1, past * DA_HEADS, DA_V), lambda b: (b, 0, 0))],
        out_specs=new, out_shape=jax.ShapeDtypeStruct((nb, lq, GROUP_W), _BF16),
        compiler_params=_cparams(
            ("parallel",), [_nbytes((GROUP_W, past), _F32), _nbytes((past * DA_HEADS, DA_V), _F32)]
            + 4 * [_nbytes((lq, GROUP_W), _F32)], 8 * _nbytes((2 * lq, past), _F32)), name="attn_sample",
    )(lamv, q, kn, vn, kct, vc)


def _mlstm_kernel(q_ref, k_ref, v_ref, gc_ref, gr_ref, c0_ref, n0_ref, m0_ref, h_ref, c_ref, n_ref, m_ref,
                  c_sc, n_sc, m_sc, *, chunk, nchunks):
    j = pl.program_id(1)

    @pl.when(j == 0)
    def _():
        c_sc[...] = c0_ref[0]
        n_sc[...] = n0_ref[0]
        m_sc[...] = m0_ref[0]

    L = chunk
    r_i = lax.broadcasted_iota(jnp.int32, (L, L), 0)
    c_i = lax.broadcasted_iota(jnp.int32, (L, L), 1)
    causal = r_i >= c_i
    tril = causal.astype(_F32)
    triu = (r_i <= c_i).astype(_F32)
    hi = lax.Precision.HIGHEST

    for c in range(nchunks):
        rows = slice(c * L, (c + 1) * L)
        gcol = gc_ref[0, rows, :]
        grow = gr_ref[0, :, rows]
        fc_all = _dot(tril, gcol, precision=hi)
        fr_all = _dot(grow, triu, precision=hi)
        for hh in range(ML_HEADS):
            sl = slice(hh * HEAD_W, (hh + 1) * HEAD_W)
            m = m_sc[hh][0:1, 0:1]
            fc = fc_all[:, ML_HEADS + hh:ML_HEADS + hh + 1]
            fr = fr_all[ML_HEADS + hh:ML_HEADS + hh + 1, :]
            igc = gcol[:, hh:hh + 1]
            igr = grow[hh:hh + 1, :]
            a = jnp.where(causal, fc - fr + igr, -jnp.inf)
            b = fc + m
            m_t = jnp.maximum(b, jnp.max(a, axis=-1, keepdims=True))
            w = jnp.exp(a - m_t)
            sc = jnp.exp(b - m_t)
            q = q_ref[0, rows, sl]
            k = k_ref[0, rows, sl]
            v = v_ref[0, rows, sl]
            wqk = w * _dot(q, k, _NT)
            cmat = c_sc[hh]
            nrow = n_sc[hh:hh + 1, :]
            qn = jnp.sum(q.astype(_F32) * nrow, axis=-1, keepdims=True)
            num = _dot(wqk.astype(_BF16), v) + sc * _dot(q, cmat.astype(_BF16))
            den = jnp.sum(wqk, axis=-1, keepdims=True) + sc * qn
            h_ref[0, rows, sl] = (num / jnp.maximum(jnp.abs(den), jnp.exp(-m_t))).astype(_BF16)

            fl = fc[L - 1:L, :]
            g = fl - fc + igc
            m_new = jnp.maximum(fl + m, jnp.max(g, axis=0, keepdims=True))
            decay = jnp.exp(fl + m - m_new)
            kw = k.astype(_F32) * jnp.exp(g - m_new)
            c_sc[hh] = decay * cmat + _dot(kw.astype(_BF16), v, _TN)
            n_sc[hh:hh + 1, :] = decay * nrow + jnp.sum(kw, axis=0, keepdims=True)
            m_sc[hh] = jnp.broadcast_to(m_new, m_sc.shape[1:])

    @pl.when(j == pl.num_programs(1) - 1)
    def _():
        c_ref[0] = c_sc[...]
        n_ref[0] = n_sc[...]
        m_ref[0] = m_sc[...]


def _mlstm(q, k, v, gc, gr, c0, n0, m0, *, chunk, block):
    nb, s, _ = q.shape
    tok = lambda w: pl.BlockSpec((1, block, w), lambda b, j: (b, j, 0))
    cspec = pl.BlockSpec((1, ML_HEADS, ML_DK, ML_DV), lambda b, j: (b, 0, 0, 0))
    nspec = pl.BlockSpec((1, ML_HEADS, ML_DK), lambda b, j: (b, 0, 0))
    mspec = pl.BlockSpec((1, ML_HEADS, SUBLANES, HEAD_W), lambda b, j: (b, 0, 0, 0))
    return pl.pallas_call(
        functools.partial(_mlstm_kernel, chunk=chunk, nchunks=block // chunk),
        grid=(nb, s // block),
        in_specs=[tok(GROUP_W), tok(GROUP_W), tok(GROUP_W), tok(N_GATES),
                  pl.BlockSpec((1, N_GATES, block), lambda b, j: (b, 0, j)), cspec, nspec, mspec],
        out_specs=[tok(GROUP_W), cspec, nspec, mspec],
        out_shape=[jax.ShapeDtypeStruct((nb, s, GROUP_W), _BF16), jax.ShapeDtypeStruct(c0.shape, _F32),
                   jax.ShapeDtypeStruct(n0.shape, _F32), jax.ShapeDtypeStruct(m0.shape, _F32)],
        scratch_shapes=[pltpu.VMEM((ML_HEADS, ML_DK, ML_DV), _F32), pltpu.VMEM((ML_HEADS, ML_DK), _F32),
                        pltpu.VMEM((ML_HEADS, SUBLANES, HEAD_W), _F32)],
        compiler_params=_cparams(
            ("parallel", "arbitrary"), 5 * [_nbytes((block, GROUP_W), _F32)] + 2 * [_nbytes(c0.shape[1:], _F32)],
            2 * _nbytes(c0.shape[1:], _F32)), name="mlstm",
    )(q, k, v, gc, gr, c0, n0, m0)


def _mlstm_pairs_kernel(k_ref, qt_ref, vt_ref, gs_ref, st0_ref, m0_ref, h_ref, st_ref, m_ref, st_sc, m_sc,
                        *, block):
    j = pl.program_id(1)

    @pl.when(j == 0)
    def _():
        st_sc[...] = st0_ref[0]
        m_sc[...] = m0_ref[0]

    L, W = CHUNK, 2 * CHUNK
    rows8 = lambda n: gs_ref[0, n * N_GATES:(n + 1) * N_GATES, :]
    f8, d8, cm8, mg8, ws8 = (rows8(n) for n in range(5))
    d_cols = d8.T

    s_i = lax.broadcasted_iota(jnp.int32, (W, W), 0)
    t_i = lax.broadcasted_iota(jnp.int32, (W, W), 1)
    allowed = (s_i // L == t_i // L) & (s_i <= t_i)
    lane_w = lax.broadcasted_iota(jnp.int32, (1, W), 1)
    first = lane_w < L

    sts = [st_sc[hh] for hh in range(ML_HEADS)]
    ms = [m_sc[hh][0:1, 0:1] for hh in range(ML_HEADS)]

    def independent(p, hh):
        lanes = slice(p * W, (p + 1) * W)
        feat = slice(hh * HEAD_W, (hh + 1) * HEAD_W)
        kp = k_ref[0, lanes, feat]
        qt = qt_ref[0, 0, feat, lanes]
        v1t = vt_ref[0, 0, hh * VT_ROWS:(hh + 1) * VT_ROWS, lanes]
        ws = ws8[hh:hh + 1, lanes]
        v1f = v1t.astype(_F32)
        return dict(
            p=p, hh=hh, qt=qt, v1t=v1t, qk_t=_dot(kp, qt),
            upd_a=_dot((v1f * jnp.where(first, ws, 0.0)).astype(_BF16), kp),
            upd_b=_dot((v1f * jnp.where(first, 0.0, ws)).astype(_BF16), kp))

    def dependent(u):
        p, hh, qt = u["p"], u["hh"], u["qt"]
        lanes = slice(p * W, (p + 1) * W)
        row = lambda x: x[hh:hh + 1, lanes]
        cm, f_row = row(cm8), row(f8)
        w_t = jnp.exp(jnp.where(allowed, d_cols[lanes, hh:hh + 1] - cm, -jnp.inf))
        intra = _dot(u["v1t"], (w_t * u["qk_t"]).astype(_BF16))
        st_a, m_a = sts[hh], ms[hh]
        fl_a, mg_a = f_row[:, L - 1:L], row(mg8)[:, 0:1]
        fl_b, mg_b = f_row[:, W - 1:W], row(mg8)[:, L:L + 1]
        m_b = jnp.maximum(fl_a + m_a, mg_a)
        st_b = jnp.exp(fl_a + m_a - m_b) * st_a + jnp.exp(mg_a - m_b) * u["upd_a"]
        m_c = jnp.maximum(fl_b + m_b, mg_b)
        sts[hh] = jnp.exp(fl_b + m_b - m_c) * st_b + jnp.exp(mg_b - m_c) * u["upd_b"]
        ms[hh] = m_c
        inter = jnp.where(first, _dot(st_a.astype(_BF16), qt), _dot(st_b.astype(_BF16), qt))
        m_prev = jnp.where(first, m_a, m_b)
        mt = jnp.maximum(m_prev, cm)
        tot = jnp.exp(cm - mt) * intra + jnp.exp(m_prev - mt) * inter
        den = jnp.maximum(jnp.abs(tot[ML_DV:ML_DV + 1]), jnp.exp(-(f_row + mt)))
        h_ref[0, lanes, hh * HEAD_W:(hh + 1) * HEAD_W] = (tot[:ML_DV] / den).T.astype(_BF16)

    units = [(p, hh) for p in range(block // W) for hh in range(ML_HEADS)]
    nxt = independent(*units[0])
    for n in range(len(units)):
        cur = nxt
        if n + 1 < len(units):
            nxt = independent(*units[n + 1])
        dependent(cur)
    for hh in range(ML_HEADS):
        st_sc[hh] = sts[hh]
        m_sc[hh] = jnp.broadcast_to(ms[hh], m_sc.shape[1:])

    @pl.when(j == pl.num_programs(1) - 1)
    def _():
        st_ref[0] = st_sc[...]
        m_ref[0] = m_sc[...]


def _mlstm_pairs(k, qt, vt, gs, st0, m0):
    nb, s, _ = k.shape
    block = qt.shape[3]
    nt = s // block
    state = pl.BlockSpec((1, ML_HEADS, VT_ROWS, ML_DK), lambda b, j: (b, 0, 0, 0))
    mspec = pl.BlockSpec((1, ML_HEADS, SUBLANES, HEAD_W), lambda b, j: (b, 0, 0, 0))
    tok = pl.BlockSpec((1, block, GROUP_W), lambda b, j: (b, j, 0))
    return pl.pallas_call(
        functools.partial(_mlstm_pairs_kernel, block=block),
        grid=(nb, nt),
        in_specs=[tok, pl.BlockSpec((1, 1, GROUP_W, block), lambda b, j: (b, j, 0, 0)),
                  pl.BlockSpec((1, 1, ML_HEADS * VT_ROWS, block), lambda b, j: (b, j, 0, 0)),
                  pl.BlockSpec((1, 5 * N_GATES, block), lambda b, j: (b, 0, j)), state, mspec],
        out_specs=[tok, state, mspec],
        out_shape=[jax.ShapeDtypeStruct((nb, s, GROUP_W), _BF16), jax.ShapeDtypeStruct(st0.shape, _F32),
                   jax.ShapeDtypeStruct(m0.shape, _F32)],
        scratch_shapes=[pltpu.VMEM((ML_HEADS, VT_ROWS, ML_DK), _F32), pltpu.VMEM((ML_HEADS, SUBLANES, HEAD_W), _F32)],
        compiler_params=_cparams(
            ("parallel", "arbitrary"),
            3 * [_nbytes((block, GROUP_W), _BF16)] + [_nbytes((ML_HEADS * VT_ROWS, block), _BF16)]
            + 2 * [_nbytes(st0.shape[1:], _F32)], 16 * _nbytes((VT_ROWS, 2 * CHUNK), _F32) * ML_HEADS), name="mlstm_pairs",
    )(k, qt, vt, gs, st0, m0)


def _mix_kernel(da_ref, ml_ref, og_ref, x_ref, subln_ref, wo_ref, g2_ref, wr_ref, br_ref, cnt0_ref,
                xmid_ref, xp_ref, idx_ref, gate_ref, rank_ref, cnt_ref, cnt_sc, upper_sc, *, tm, lam_init):
    step = pl.program_id(0)

    @pl.when(step == 0)
    def _():
        cnt_sc[...] = cnt0_ref[...]
        earlier = lax.broadcasted_iota(jnp.int32, (tm, tm), 0) < lax.broadcasted_iota(jnp.int32, (tm, tm), 1)
        upper_sc[...] = earlier.astype(_BF16)

    parts = []
    for h in range(DA_HEADS):
        d = da_ref[:, h * HEAD_W:(h + 1) * HEAD_W].astype(_F32)
        d = d * lax.rsqrt(jnp.mean(d * d, axis=-1, keepdims=True) + EPS) * subln_ref[...]
        parts.append((d * (1.0 - lam_init)).astype(_BF16))
    parts.append((og_ref[...].astype(_F32) * ml_ref[...].astype(_F32)).astype(_BF16))
    xm = x_ref[...] + _dot(jnp.concatenate(parts, axis=1), wo_ref[...])
    xmid_ref[...] = xm
    xn_f = xm * lax.rsqrt(jnp.mean(xm * xm, axis=-1, keepdims=True) + EPS) * g2_ref[...]
    xp_ref[...] = _pack_rows(xn_f)
    xn = xn_f.astype(_BF16)

    logits = _dot(wr_ref[...], xn, _NT) + br_ref[...]
    e_i = lax.broadcasted_iota(jnp.int32, logits.shape, 0)
    member = jnp.zeros(logits.shape, jnp.bool_)
    vals, idxs = [], []
    for _ in range(TOP_K):
        mx = jnp.max(logits, axis=0, keepdims=True)
        sel = jnp.min(jnp.where(logits == mx, e_i, N_EXPERTS), axis=0, keepdims=True)
        hit = e_i == sel
        member = member | hit
        logits = jnp.where(hit, -jnp.inf, logits)
        vals.append(mx)
        idxs.append(sel)
    ex = [jnp.exp(v - vals[0]) for v in vals]
    tot = ex[0] + ex[1] + ex[2] + ex[3]
    idx_ref[...] = jnp.concatenate(idxs, axis=0)
    gate_ref[...] = jnp.concatenate([e / tot for e in ex], axis=0)

    memf = member.astype(_F32)
    before = _dot(memf.astype(_BF16), upper_sc[...]) + cnt_sc[...]
    rank_ref[...] = jnp.concatenate(
        [jnp.sum(jnp.where(e_i == s, before, 0.0), axis=0, keepdims=True) for s in idxs], axis=0).astype(jnp.int32)
    cnt_sc[...] += jnp.sum(memf, axis=1, keepdims=True)
    cnt_ref[...] = cnt_sc[...]


def _mix(da, ml, og, x, subln, w_out, norm2, w_rt, b_r, cnt0, *, tm, lam_init):
    t = x.shape[0]
    tok = lambda w: pl.BlockSpec((tm, w), lambda i: (i, 0))
    const = lambda shape: pl.BlockSpec(shape, lambda i: (0,) * len(shape))
    lane_tok = pl.BlockSpec((TOP_K, tm), lambda i: (0, i))
    return pl.pallas_call(
        functools.partial(_mix_kernel, tm=tm, lam_init=lam_init),
        grid=(t // tm,),
        in_specs=[tok(GROUP_W), tok(GROUP_W), tok(GROUP_W), tok(D_MODEL), const((1, DA_V)),
                  const((D_MODEL, D_MODEL)), const((1, D_MODEL)), const((N_EXPERTS, D_MODEL)),
                  const((N_EXPERTS, 1)), const((N_EXPERTS, 1))],
        out_specs=[tok(D_MODEL), tok(PACK_W), lane_tok, lane_tok, lane_tok, const((N_EXPERTS, 1))],
        out_shape=[jax.ShapeDtypeStruct((t, D_MODEL), _F32), jax.ShapeDtypeStruct((t, PACK_W), jnp.int32),
                   jax.ShapeDtypeStruct((TOP_K, t), jnp.int32), jax.ShapeDtypeStruct((TOP_K, t), _F32),
                   jax.ShapeDtypeStruct((TOP_K, t), jnp.int32), jax.ShapeDtypeStruct((N_EXPERTS, 1), _F32)],
        scratch_shapes=[pltpu.VMEM((N_EXPERTS, 1), _F32), pltpu.VMEM((tm, tm), _BF16)],
        compiler_params=_cparams(
            ("arbitrary",),
            3 * [_nbytes((tm, GROUP_W), _BF16)] + 2 * [_nbytes((tm, D_MODEL), _F32)] + [_nbytes((tm, PACK_W), _F32)]
            + [_nbytes((D_MODEL, D_MODEL), _BF16)], 4 * _nbytes((tm, D_MODEL), _F32) + 2 * _nbytes((tm, tm), _F32)), name="mix",
    )(da, ml, og, x, subln, w_out, norm2, w_rt, b_r, cnt0)


def _moe_kernel(be_ref, nu_ref, nv_ref, slot_ref, nxt_ref, x_ref, wgu_hbm, bgu_ref, wd_hbm, bd_ref, y_ref,
                wgu_f32, wd_f32, wgu_sc, wd_sc, sems):
    j = pl.program_id(0)
    used = j < nu_ref[0]
    fresh = (j == 0) | (be_ref[j] != be_ref[jnp.maximum(j - 1, 0)])
    slot = slot_ref[j]

    def fetch(e, s):
        return (pltpu.make_async_copy(wgu_hbm.at[e], wgu_f32.at[s], sems.at[s, 0]),
                pltpu.make_async_copy(wd_hbm.at[e], wd_f32.at[s], sems.at[s, 1]))

    @pl.when(j == 0)
    def _():
        for cp in fetch(be_ref[0], slot):
            cp.start()

    @pl.when(fresh & used)
    def _():
        for cp in fetch(be_ref[j], slot):
            cp.wait()

        @pl.when(nxt_ref[j] >= 0)
        def _():
            for cp in fetch(nxt_ref[j], 1 - slot):
                cp.start()

        rows = CAST_ROWS
        def cast_gu(r, _):
            sl = pl.ds(pl.multiple_of(r * rows, rows), rows)
            wgu_sc[sl, :] = wgu_f32[slot, sl, :].astype(_BF16)
            return 0
        lax.fori_loop(0, D_MODEL // rows, cast_gu, 0)
        def cast_d(r, _):
            sl = pl.ds(pl.multiple_of(r * rows, rows), rows)
            wd_sc[sl, :] = wd_f32[slot, sl, :].astype(_BF16)
            return 0
        lax.fori_loop(0, D_FF // rows, cast_d, 0)

    @pl.when(j >= nu_ref[0])
    def _():
        y_ref[...] = jnp.zeros_like(y_ref)

    def ffn(row0):
        hi, lo = _unpack_rows(x_ref[row0:, :])
        x = jnp.concatenate([hi.astype(_BF16), lo.astype(_BF16)], axis=1)
        n_chunks = D_FF // FF_CHUNK

        def gate_up(c):
            lo = c * FF_CHUNK
            return (_dot(x, wgu_sc[:, lo:lo + FF_CHUNK]), _dot(x, wgu_sc[:, D_FF + lo:D_FF + lo + FF_CHUNK]))

        acc = None
        nxt = gate_up(0)
        for c in range(n_chunks):
            lo = c * FF_CHUNK
            gate, up = nxt
            if c + 1 < n_chunks:
                nxt = gate_up(c + 1)
            gate = jnp.minimum(gate + bgu_ref[0, :, lo:lo + FF_CHUNK], SWIGLU_LIMIT)
            up = jnp.clip(up + bgu_ref[0, :, D_FF + lo:D_FF + lo + FF_CHUNK], -SWIGLU_LIMIT, SWIGLU_LIMIT)
            act = (up + 1.0) * (gate * jax.nn.sigmoid(SWIGLU_ALPHA * gate))
            down = _dot(act.astype(_BF16), wd_sc[lo:lo + FF_CHUNK, :])
            acc = down if acc is None else acc + down
        y_ref[row0:, :] = _pack_rows(acc + bd_ref[0])

    half = MOE_BLOCK // 2
    @pl.when(used & (nv_ref[j] > half))
    def _():
        ffn(0)

    @pl.when(used & (nv_ref[j] <= half))
    def _():
        y_ref[:half, :] = jnp.zeros((half, PACK_W), jnp.int32)
        ffn(half)


def _moe(blk_e, n_used, blk_rows, blk_slot, blk_next, xb, w_gu, b_gu, w_d, b_d):
    n_blocks = xb.shape[0] // MOE_BLOCK
    last = lambda j, nu: jnp.minimum(j, nu[0] - 1)
    row_spec = pl.BlockSpec((MOE_BLOCK, PACK_W), lambda j, be, nu, nv, sl, nx: (last(j, nu), 0))
    bias_spec = lambda c: pl.BlockSpec((1, 1, c), lambda j, be, nu, nv, sl, nx: (be[last(j, nu)], 0, 0))
    hbm = pl.BlockSpec(memory_space=pl.ANY)
    grid_spec = pltpu.PrefetchScalarGridSpec(
        num_scalar_prefetch=5, grid=(n_blocks,),
        in_specs=[row_spec, hbm, bias_spec(2 * D_FF), hbm, bias_spec(D_MODEL)],
        out_specs=pl.BlockSpec((MOE_BLOCK, PACK_W), lambda j, be, nu, nv, sl, nx: (j, 0)),
        scratch_shapes=[pltpu.VMEM((2, D_MODEL, 2 * D_FF), _F32), pltpu.VMEM((2, D_FF, D_MODEL), _F32),
                        pltpu.VMEM((D_MODEL, 2 * D_FF), _BF16), pltpu.VMEM((D_FF, D_MODEL), _BF16),
                        pltpu.SemaphoreType.DMA((2, 2))])
    return pl.pallas_call(
        _moe_kernel, grid_spec=grid_spec, out_shape=jax.ShapeDtypeStruct(xb.shape, jnp.int32),
        compiler_params=_cparams(
            ("arbitrary",), 2 * [_nbytes((MOE_BLOCK, PACK_W), _F32)],
            2 * _nbytes((D_MODEL, 3 * D_FF), _F32) + _nbytes((D_MODEL, 3 * D_FF), _BF16)
            + 4 * _nbytes((MOE_BLOCK, D_MODEL), _F32)), name="moe",
    )(blk_e, n_used, blk_rows, blk_slot, blk_next, xb, w_gu, b_gu, w_d, b_d)


def _sc_mesh():
    return plsc.VectorSubcoreMesh(core_axis_name="c", subcore_axis_name="s")


def _sc_worker():
    return lax.axis_index("s") * SC_CORES + lax.axis_index("c")


def _sc_dispatch(sources, n_rows):
    n_src = len(sources)
    shapes = []
    for _, dest3 in sources:
        n_chunks, _, chunk = dest3.shape
        per_worker = n_chunks // SC_WORKERS
        assert per_worker * SC_WORKERS == n_chunks and chunk % SUBLANES == 0
        shapes.append((per_worker, chunk))

    def body(*refs):
        x_hbms, d_hbms, o_hbm = refs[:n_src], refs[n_src:2 * n_src], refs[2 * n_src]
        scratch = refs[2 * n_src + 1:]
        load_a, load_b, scat_a, scat_b = scratch[3 * n_src:]
        worker = _sc_worker()
        for i, (per_worker, chunk) in enumerate(shapes):
            x_hbm, idx_v, rows_a, rows_b = x_hbms[i], scratch[3 * i], scratch[3 * i + 1], scratch[3 * i + 2]
            pltpu.sync_copy(d_hbms[i].at[worker], idx_v)

            def load(ci, rows, sem):
                row0 = (worker * per_worker + ci) * chunk
                return pltpu.make_async_copy(x_hbm.at[pl.ds(pl.multiple_of(row0, SUBLANES), chunk)], rows, sem)

            def scatters(ci, rows, sem):
                return [pltpu.make_async_copy(rows, o_hbm.at[idx_v.at[ci * TOP_K + k]], sem) for k in range(TOP_K)]

            def start(copies):
                for cp in copies:
                    cp.start()

            def wait(copies):
                for cp in copies:
                    cp.wait()

            load(0, rows_a, load_a).start()

            @pl.loop(0, per_worker // 2)
            def _(p):
                ca, cb = 2 * p, 2 * p + 1
                load(ca, rows_a, load_a).wait()
                load(cb, rows_b, load_b).start()
                start(scatters(ca, rows_a, scat_a))
                load(cb, rows_b, load_b).wait()
                wait(scatters(ca, rows_a, scat_a))
                load(jnp.minimum(ca + 2, per_worker - 1), rows_a, load_a).start()
                start(scatters(cb, rows_b, scat_b))
                wait(scatters(cb, rows_b, scat_b))

            last = per_worker - 1
            load(last, rows_a, load_a).wait()
            if per_worker % 2:
                start(scatters(last, rows_a, scat_a))
                wait(scatters(last, rows_a, scat_a))

    dma = pltpu.SemaphoreType.DMA
    scratch_types = []
    for per_worker, chunk in shapes:
        scratch_types += [pltpu.VMEM((per_worker * TOP_K, chunk), jnp.int32), pltpu.VMEM((chunk, PACK_W), jnp.int32),
                          pltpu.VMEM((chunk, PACK_W), jnp.int32)]
    tables = [dest3.reshape(SC_WORKERS, pw * TOP_K, chunk) for (_, dest3), (pw, chunk) in zip(sources, shapes)]
    return pl.kernel(
        body, out_type=jax.ShapeDtypeStruct((n_rows, PACK_W), jnp.int32), mesh=_sc_mesh(),
        scratch_types=scratch_types + [dma, dma, dma, dma],
        name="sc_dispatch")(*[xp for xp, _ in sources], *tables)


def _sc_gather(table, idx2):
    n_chunks, chunk = idx2.shape
    per_worker = n_chunks // SC_WORKERS
    assert per_worker * SC_WORKERS == n_chunks and chunk % SUBLANES == 0

    def body(t_hbm, i_hbm, o_hbm, idx_v, rows_a, rows_b, gath_a, gath_b, put_a, put_b):
        worker = _sc_worker()
        pltpu.sync_copy(i_hbm.at[worker], idx_v)

        def gather(ci, rows, sem):
            return pltpu.make_async_copy(t_hbm.at[idx_v.at[ci]], rows, sem)

        def put(ci, rows, sem):
            row0 = (worker * per_worker + ci) * chunk
            return pltpu.make_async_copy(rows, o_hbm.at[pl.ds(pl.multiple_of(row0, SUBLANES), chunk)], sem)

        gather(0, rows_a, gath_a).start()

        @pl.loop(0, per_worker // 2)
        def _(p):
            ca, cb = 2 * p, 2 * p + 1
            gather(ca, rows_a, gath_a).wait()
            gather(cb, rows_b, gath_b).start()
            put(ca, rows_a, put_a).start()
            gather(cb, rows_b, gath_b).wait()
            put(ca, rows_a, put_a).wait()
            gather(jnp.minimum(ca + 2, per_worker - 1), rows_a, gath_a).start()
            put(cb, rows_b, put_b).start()
            put(cb, rows_b, put_b).wait()

        last = per_worker - 1
        gather(last, rows_a, gath_a).wait()
        if per_worker % 2:
            put(last, rows_a, put_a).start()
            put(last, rows_a, put_a).wait()

    dma = pltpu.SemaphoreType.DMA
    return pl.kernel(
        body, out_type=jax.ShapeDtypeStruct((n_chunks * chunk, PACK_W), jnp.int32), mesh=_sc_mesh(),
        scratch_types=[pltpu.VMEM((per_worker, chunk), jnp.int32), pltpu.VMEM((chunk, PACK_W), jnp.int32),
                       pltpu.VMEM((chunk, PACK_W), jnp.int32), dma, dma, dma, dma],
        name="sc_gather")(table, idx2.reshape(SC_WORKERS, per_worker, chunk))


def _final_kernel(x_ref, y_ref, g_ref, nf_ref, o_ref):
    g_rows = g_ref[...]
    g = jnp.concatenate([g_rows, jnp.zeros((SUBLANES - TOP_K, g_rows.shape[1]), _F32)], axis=0).T
    hi = jnp.zeros((x_ref.shape[0], PACK_W), _F32)
    lo = jnp.zeros((x_ref.shape[0], PACK_W), _F32)
    for k in range(TOP_K):
        y_hi, y_lo = _unpack_rows(y_ref[k])
        hi = hi + g[:, k:k + 1] * y_hi
        lo = lo + g[:, k:k + 1] * y_lo
    x = x_ref[...] + jnp.concatenate([hi, lo], axis=1)
    o_ref[...] = x * lax.rsqrt(jnp.mean(x * x, axis=-1, keepdims=True) + EPS) * nf_ref[...]


def _final(xmid, yg, gates, norm_f, *, tm, first_block):
    t = xmid.shape[0]
    return pl.pallas_call(
        _final_kernel, grid=(t // tm,),
        in_specs=[pl.BlockSpec((tm, D_MODEL), lambda i: (i, 0)),
                  pl.BlockSpec((TOP_K, tm, PACK_W), lambda i: (0, i + first_block, 0)),
                  pl.BlockSpec((TOP_K, tm), lambda i: (0, i)),
                  pl.BlockSpec((1, D_MODEL), lambda i: (0, 0))],
        out_specs=pl.BlockSpec((tm, D_MODEL), lambda i: (i, 0)),
        out_shape=jax.ShapeDtypeStruct((t, D_MODEL), _F32),
        compiler_params=_cparams(
            ("parallel",), 2 * [_nbytes((tm, D_MODEL), _F32)] + [_nbytes((TOP_K, tm, PACK_W), _F32)],
            2 * _nbytes((tm, D_MODEL), _F32)), name="final",
    )(xmid, yg, gates, norm_f)


def kernel(x_prompt, x_sample, cache_k, cache_v, state_C, state_n, state_m, norm1, w_in, b_igate, b_fgate,
           lambda_q1, lambda_k1, lambda_q2, lambda_k2, subln, w_out, norm2, w_router, b_router, w_gate_up,
           b_gate_up, w_down, b_down, norm_f):
    nb, s, _ = x_prompt.shape
    db, ls, _ = x_sample.shape
    past = cache_k.shape[2]
    depth = w_in.shape[0]
    assert depth == 1 and s % PROJ_TILE == 0 and (db * ls) % SUBLANES == 0
    l = 0
    lam_init = 0.8 - 0.6 * math.exp(-0.3 * l)
    t_p, t_s = nb * s, db * ls

    w_nat = jnp.pad(w_in[l].T, ((0, HEAD_W - N_GATES), (0, 0))).astype(_BF16)
    b_gates = jnp.pad(jnp.concatenate([b_igate[l], b_fgate[l]]).astype(_F32), (0, HEAD_W - N_GATES))[None]
    inv = ROPE_THETA ** (-jnp.arange(0, DA_QK, 2, dtype=_F32) / DA_QK)
    inv_full = jnp.tile(inv, HEAD_W // (DA_QK // 2))[None]
    lamv = jnp.stack([lambda_q1[l], lambda_k1[l], lambda_q2[l], lambda_k2[l]]).astype(_F32)
    g1 = norm1[l][None]

    pp = _proj(x_prompt, g1, w_nat, b_gates, inv_full, tm=PROJ_TILE, pos_base=0, pos_mod=s,
               names=("q", "kf", "kb", "vf", "vt", "mk", "og", "gs", "mqt", "mvt"))
    ps = _proj(x_sample.reshape(1, t_s, D_MODEL), g1, w_nat, b_gates, inv_full, tm=t_s, pos_base=past, pos_mod=ls,
               names=("q", "kf", "kb", "vf", "mq", "mk", "mv", "og", "gc", "gr"))
    kf_p, vf_p, kf_s, vf_s = pp["kf"], pp["vf"], ps["kf"], ps["vf"]

    da_p = _attn_prompt(lamv, pp["q"], pp["kb"], pp["vt"], lam_init=lam_init)
    per_b = lambda a: a.reshape(db, ls, a.shape[-1])
    kct = jnp.transpose(cache_k[l], (0, 2, 3, 4, 1)).reshape(db, GROUP_W, past)
    vc = cache_v[l].reshape(db, past * DA_HEADS, DA_V)
    vn = vf_s.reshape(db, ls, GROUP_W).astype(_BF16)
    da_s = _attn_sample(lamv, per_b(ps["q"]), per_b(ps["kb"]), vn, kct, vc, lam_init=lam_init)

    assert PROJ_TILE % (2 * CHUNK) == 0
    h_p, st_p, m_p = _mlstm_pairs(pp["mk"], pp["mqt"], pp["mvt"], pp["gs"],
                                  jnp.zeros((nb, ML_HEADS, VT_ROWS, ML_DK), _F32),
                                  jnp.zeros((nb, ML_HEADS, SUBLANES, HEAD_W), _F32))
    gr_sb = ps["gr"].reshape(8, db, ls).transpose(1, 0, 2)
    m0_s = jnp.broadcast_to(state_m[l].astype(_F32)[:, :, None, None], (db, ML_HEADS, SUBLANES, HEAD_W))
    h_s, c_s, n_s, m_s = _mlstm(per_b(ps["mq"]), per_b(ps["mk"]), per_b(ps["mv"]), per_b(ps["gc"]), gr_sb,
                                state_C[l].astype(_F32), state_n[l].astype(_F32), m0_s, chunk=ls, block=ls)
    c_p, n_p, mm_p = jnp.swapaxes(st_p[:, :, :ML_DV, :], -1, -2), st_p[:, :, ML_DV, :], m_p[:, :, 0, 0]
    mm_s = m_s[:, :, 0, 0]

    wo = w_out[l].astype(_BF16)
    w_rt = w_router[l].T.astype(_BF16)
    b_r = b_router[l].astype(_F32)[:, None]
    mix = functools.partial(_mix, subln=subln[l][None], w_out=wo, norm2=norm2[l][None], w_rt=w_rt, b_r=b_r,
                            lam_init=lam_init)
    flat = lambda a: a.reshape(-1, a.shape[-1])
    t_all = t_p + t_s
    assert t_p % t_s == 0
    xm_p, xp_p, idx_p, gate_p, rank_p, cnt_p = mix(
        flat(da_p), flat(h_p), flat(pp["og"]), flat(x_prompt), cnt0=jnp.zeros((N_EXPERTS, 1), _F32), tm=PROJ_TILE)
    xm_s, xp_s, idx_s, gate_s, rank_s, cnt = mix(
        flat(da_s), flat(h_s), flat(ps["og"]), flat(x_sample), cnt0=cnt_p, tm=t_s)

    rows = t_all * TOP_K
    n_blocks = -(-rows // MOE_BLOCK) + N_EXPERTS
    counts = cnt[:, 0].astype(jnp.int32)
    padded = (counts + MOE_BLOCK - 1) // MOE_BLOCK * MOE_BLOCK
    pend = jnp.cumsum(padded)
    pstart = pend - counts
    experts = jnp.arange(N_EXPERTS, dtype=jnp.int32)[:, None, None]
    first_row = lambda idx: jnp.sum(jnp.where(idx[None] == experts, pstart[:, None, None], 0), axis=0)
    dest_p, dest_s = first_row(idx_p) + rank_p, first_row(idx_s) + rank_s
    n_used = (pend[-1] // MOE_BLOCK).astype(jnp.int32)[None]
    blk_lo = jnp.arange(n_blocks, dtype=jnp.int32) * MOE_BLOCK
    blk_e = jnp.minimum(jnp.sum((pend[None, :] <= blk_lo[:, None]).astype(jnp.int32), axis=1), N_EXPERTS - 1)
    row_lo = jnp.sum(jnp.where(blk_e[:, None] == experts[:, 0].T, pstart[None, :], 0), axis=1)
    blk_rows = jnp.where(blk_lo < pend[-1], jnp.clip(blk_lo + MOE_BLOCK - row_lo, 0, MOE_BLOCK), 0).astype(jnp.int32)

    def chunked(dest):
        chunk = min(DISPATCH_CHUNK, dest.shape[1] // SC_WORKERS)
        return dest.reshape(TOP_K, dest.shape[1] // chunk, chunk).transpose(1, 0, 2)

    assert rows % (SC_WORKERS * COMBINE_CHUNK) == 0
    xb = _sc_dispatch([(xp_p, chunked(dest_p)), (xp_s, chunked(dest_s))], n_blocks * MOE_BLOCK)
    dest = jnp.concatenate([dest_p, dest_s], axis=1)
    first_blk = jnp.concatenate([jnp.ones((1,), jnp.bool_), blk_e[1:] != blk_e[:-1]])
    blk_slot = ((jnp.cumsum(first_blk.astype(jnp.int32)) - 1) % 2).astype(jnp.int32)
    after = jnp.sum(jnp.where(blk_e[:, None] == experts[:, 0].T, pend[None, :], 0), axis=1) // MOE_BLOCK
    blk_next = jnp.where(after < n_used[0], blk_e[jnp.minimum(after, n_blocks - 1)], -1).astype(jnp.int32)
    yb = _moe(blk_e, n_used, blk_rows, blk_slot, blk_next, xb, w_gate_up[l], b_gate_up[l][:, None, :], w_down[l], b_down[l][:, None, :])
    yg = _sc_gather(yb, dest.reshape(rows // COMBINE_CHUNK, COMBINE_CHUNK)).reshape(TOP_K, t_all, PACK_W)
    nf = norm_f[None]
    y_p = _final(xm_p, yg, gate_p, nf, tm=FINAL_TILE, first_block=0).reshape(nb, s, D_MODEL)
    y_s = _final(xm_s, yg, gate_s, nf, tm=t_s, first_block=t_p // t_s).reshape(db, ls, D_MODEL)

    st = lambda a: a[None]
    return (y_p, y_s,
            st(kf_p.reshape(nb, s, DA_HEADS, 2, DA_QK)), st(vf_p.reshape(nb, s, DA_HEADS, DA_V)),
            st(c_p), st(n_p), st(mm_p),
            st(kf_s.reshape(db, ls, DA_HEADS, 2, DA_QK)), st(vf_s.reshape(db, ls, DA_HEADS, DA_V)),
            st(c_s.astype(state_C.dtype)), st(n_s.astype(state_n.dtype)), st(mm_s.astype(state_m.dtype)))
```

```python
import functools
import math

import jax
import jax.numpy as jnp
from jax import lax
from jax.experimental import pallas as pl
from jax.experimental.pallas import tpu as pltpu
from jax.experimental.pallas import tpu_sc as plsc

D_MODEL = 1024
CHUNK = 64
DA_HEADS = 4
DA_QK = 64
DA_V = 128
ML_HEADS = 4
ML_DK = 128
ML_DV = 128
HEAD_W = 128
SUBLANES = 8
N_GATES = 2 * ML_HEADS
CAST_ROWS = 128
GROUP_W = 512
ROPE_THETA = 10000.0
N_EXPERTS = 32
TOP_K = 4
D_FF = 1024
SWIGLU_LIMIT = 7.0
SWIGLU_ALPHA = 1.702
EPS = 1e-6
NEG_BIG = -1e30

LOG2E = math.log2(math.e)
VT_ROWS = 144
PROJ_TILE = 512
FINAL_TILE = 1024
ATTN_TQ = 512
MOE_BLOCK = 512
FF_CHUNK = 512
PACK_W = D_MODEL // 2
SC_CORES = 2
SC_SUBCORES = 16
SC_WORKERS = SC_CORES * SC_SUBCORES
DISPATCH_CHUNK = 64
COMBINE_CHUNK = 80
VMEM_BYTES = 64 * 1024 * 1024

_F32 = jnp.float32
_BF16 = jnp.bfloat16


def _nbytes(shape, dtype):
    return math.prod(shape) * jnp.dtype(dtype).itemsize


def _cparams(sem, blocks, resident=0):
    need = 2 * sum(blocks) + resident
    assert need <= VMEM_BYTES, need
    return pltpu.CompilerParams(dimension_semantics=sem, vmem_limit_bytes=need)


def _dot(a, b, dims=(((1,), (0,)), ((), ())), precision=None):
    return lax.dot_general(a, b, dims, precision=precision, preferred_element_type=_F32)


_NT = (((1,), (1,)), ((), ()))
_TN = (((0,), (0,)), ((), ()))


def _pack_rows(x):
    half = x.shape[1] // 2
    bits = lambda v: lax.bitcast_convert_type(v.astype(_BF16).astype(_F32), jnp.int32)
    return bits(x[:, :half]) | lax.shift_right_logical(bits(x[:, half:]), 16)


def _unpack_rows(p):
    hi = lax.bitcast_convert_type(p & jnp.int32(-65536), _F32)
    lo = lax.bitcast_convert_type(lax.shift_left(p, 16), _F32)
    return hi, lo


def _chunk_gate_rows(g8, block):
    L = CHUNK
    pos = lax.broadcasted_iota(jnp.int32, (1, block), 1) % L
    roll = lambda x, sh: pltpu.roll(x, sh % block, 1)
    steps = [1 << b for b in range(L.bit_length() - 1)]

    def scan(x, combine, fill, reverse=False):
        for sh in steps:
            ok = (pos < L - sh) if reverse else (pos >= sh)
            x = combine(x, jnp.where(ok, roll(x, -sh if reverse else sh), fill))
        return x

    to_head_rows = lambda x: pltpu.roll(x, ML_HEADS, 0)
    f8 = to_head_rows(scan(g8, jnp.add, 0.0))
    d8 = g8 - f8
    cm8 = scan(d8, jnp.maximum, -jnp.inf)
    big_g = to_head_rows(scan(g8, jnp.add, 0.0, reverse=True) - g8) + g8
    mg8 = jnp.maximum(scan(big_g, jnp.maximum, -jnp.inf), scan(big_g, jnp.maximum, -jnp.inf, reverse=True))
    return jnp.concatenate([f8, d8, cm8, mg8, jnp.exp(big_g - mg8)], axis=0)


def _proj_kernel(x_ref, g_ref, wt_ref, bg_ref, inv_ref, *refs, names, tm, pos_base, pos_mod):
    out = dict(zip(names, refs))
    cos_sc, sin_sc = refs[len(names):]
    x = x_ref[0]
    xn = (x * lax.rsqrt(jnp.mean(x * x, axis=-1, keepdims=True) + EPS) * g_ref[...]).astype(_BF16)

    row = lax.broadcasted_iota(jnp.int32, (tm, 1), 0)
    if pos_mod % tm == 0:
        tile_pos, row_pos = pos_base + (pl.program_id(1) * tm) % pos_mod, row
    else:
        assert tm % pos_mod == 0
        tile_pos, row_pos = pos_base, row % pos_mod

    @pl.when((pl.program_id(0) == 0) & (pl.program_id(1) == 0))
    def _():
        ang_row = row_pos.astype(_F32) * inv_ref[...]
        cos_sc[...] = jnp.cos(ang_row)
        sin_sc[...] = jnp.sin(ang_row)

    ang_tile = jnp.full((1, 1), tile_pos, jnp.int32).astype(_F32) * inv_ref[...]
    cos_t, sin_t = jnp.cos(ang_tile), jnp.sin(ang_tile)
    cos = cos_t * cos_sc[...] - sin_t * sin_sc[...]
    sin = sin_t * cos_sc[...] + cos_t * sin_sc[...]
    lane = lax.broadcasted_iota(jnp.int32, (1, HEAD_W), 1)
    first = (lane % DA_QK) < (DA_QK // 2)
    sin = jnp.where(first, -sin, sin)

    def rope(z):
        rot = jnp.where(first, pltpu.roll(z, HEAD_W - DA_QK // 2, 1), pltpu.roll(z, DA_QK // 2, 1))
        return z * cos + rot * sin

    def group(c, width=GROUP_W):
        return _dot(xn, wt_ref[c * GROUP_W:c * GROUP_W + width, :], _NT)

    def put_q(zq):
        for h in range(DA_HEADS):
            sl = slice(h * HEAD_W, (h + 1) * HEAD_W)
            out["q"][0, :, sl] = (rope(zq[:, sl]) * (DA_QK ** -0.5 * LOG2E)).astype(_BF16)

    def put_k(zk):
        for h in range(DA_HEADS):
            sl = slice(h * HEAD_W, (h + 1) * HEAD_W)
            rk = rope(zk[:, sl])
            out["kf"][0, :, sl] = rk
            out["kb"][0, :, sl] = rk.astype(_BF16)

    def with_ones_rows(zt):
        ones = jnp.ones((VT_ROWS - DA_V, tm), _BF16)
        return jnp.concatenate(
            [part for h in range(DA_HEADS) for part in (zt[h * DA_V:(h + 1) * DA_V], ones)], axis=0)

    def put_v(zv):
        for h in range(DA_HEADS):
            out["vf"][0, pl.ds(h, tm, stride=DA_HEADS), :] = zv[:, h * DA_V:(h + 1) * DA_V]
        if "vt" in out:
            out["vt"][0, 0] = with_ones_rows(zv.T.astype(_BF16))

    def put_mq(z):
        if "mq" in out:
            out["mq"][0] = z.astype(_BF16)
        if "mqt" in out:
            out["mqt"][0, 0] = z.T.astype(_BF16)

    def put_mk(z):
        out["mk"][0] = (z * (ML_DK ** -0.5)).astype(_BF16)

    def put_mv(z):
        if "mv" in out:
            out["mv"][0] = z.astype(_BF16)
        if "mvt" in out:
            out["mvt"][0, 0] = with_ones_rows(z.T.astype(_BF16))

    def put_mo(z):
        out["og"][0] = jax.nn.sigmoid(z).astype(_BF16)

    def put_gates(z):
        zg = z + bg_ref[...]
        lane_g = lax.broadcasted_iota(jnp.int32, zg.shape, 1)
        logsig = jnp.minimum(zg, 0.0) - jnp.log(1.0 + jnp.exp(-jnp.abs(zg)))
        gates = jnp.where(lane_g < ML_HEADS, zg, logsig)
        if "gc" in out:
            out["gc"][0] = gates[:, :N_GATES]
        g8 = gates.T[:N_GATES, :]
        if "gr" in out:
            out["gr"][0] = g8
        if "gs" in out:
            out["gs"][0] = _chunk_gate_rows(g8, tm)

    sinks = [(put_q, GROUP_W), (put_k, GROUP_W), (put_v, GROUP_W), (put_mq, GROUP_W), (put_mk, GROUP_W),
             (put_mv, GROUP_W), (put_mo, GROUP_W), (put_gates, HEAD_W)]
    pairs = [(c, sinks[c:c + 2]) for c in range(0, len(sinks), 2)]
    pairs = pairs[-1:] + pairs[:-1]
    width = lambda pair: sum(w for _, w in pair)
    nxt = group(pairs[0][0], width(pairs[0][1]))
    for n, (_, pair) in enumerate(pairs):
        cur = nxt
        if n + 1 < len(pairs):
            nxt = group(pairs[n + 1][0], width(pairs[n + 1][1]))
        pair[0][0](cur[:, :GROUP_W])
        pair[1][0](cur[:, GROUP_W:])


def _proj(x3, norm1, wt, b_gates, inv_full, *, names, tm, pos_base, pos_mod):
    nb, s, _ = x3.shape
    nt = s // tm
    tok = lambda w, dt: (jax.ShapeDtypeStruct((nb, s, w), dt), pl.BlockSpec((1, tm, w), lambda b, i: (b, i, 0)))
    slab = lambda rows: (jax.ShapeDtypeStruct((nb, nt, rows, tm), _BF16),
                         pl.BlockSpec((1, 1, rows, tm), lambda b, i: (b, i, 0, 0)))
    const = lambda shape: pl.BlockSpec(shape, lambda b, i: (0,) * len(shape))
    kinds = {
        "q": tok(GROUP_W, _BF16), "kf": tok(GROUP_W, _F32), "kb": tok(GROUP_W, _BF16),
        "vf": (jax.ShapeDtypeStruct((nb, s * DA_HEADS, DA_V), _F32),
               pl.BlockSpec((1, tm * DA_HEADS, DA_V), lambda b, i: (b, i, 0))),
        "vt": slab(DA_HEADS * VT_ROWS),
        "mq": tok(GROUP_W, _BF16), "mk": tok(GROUP_W, _BF16), "mv": tok(GROUP_W, _BF16), "og": tok(GROUP_W, _BF16),
        "gc": tok(N_GATES, _F32),
        "gr": (jax.ShapeDtypeStruct((nb, N_GATES, s), _F32), pl.BlockSpec((1, N_GATES, tm), lambda b, i: (b, 0, i))),
        "gs": (jax.ShapeDtypeStruct((nb, 5 * N_GATES, s), _F32),
               pl.BlockSpec((1, 5 * N_GATES, tm), lambda b, i: (b, 0, i))),
        "mqt": slab(GROUP_W), "mvt": slab(ML_HEADS * VT_ROWS),
    }
    blocks = [_nbytes((tm, D_MODEL), _F32), _nbytes(wt.shape, _BF16)]
    blocks += [_nbytes(kinds[n][1].block_shape, kinds[n][0].dtype) for n in names]
    outs = pl.pallas_call(
        functools.partial(_proj_kernel, names=names, tm=tm, pos_base=pos_base, pos_mod=pos_mod),
        grid=(nb, nt),
        in_specs=[pl.BlockSpec((1, tm, D_MODEL), lambda b, i: (b, i, 0)), const((1, D_MODEL)),
                  const(wt.shape), const((1, HEAD_W)), const((1, HEAD_W))],
        out_specs=[kinds[n][1] for n in names], out_shape=[kinds[n][0] for n in names],
        scratch_shapes=[pltpu.VMEM((tm, HEAD_W), _F32), pltpu.VMEM((tm, HEAD_W), _F32)],
        compiler_params=_cparams(("arbitrary", "arbitrary"), blocks, 8 * _nbytes((tm, GROUP_W), _F32)), name="proj",
    )(x3, norm1, wt, b_gates, inv_full)
    return dict(zip(names, outs))


def _lambda(lam_ref, lam_init):
    lv = lam_ref[...]
    s1 = jnp.sum(lv[0:1] * lv[1:2], axis=-1, keepdims=True)
    s2 = jnp.sum(lv[2:3] * lv[3:4], axis=-1, keepdims=True)
    return jnp.exp(s1) - jnp.exp(s2) + lam_init


def _split_components(q):
    lane = lax.broadcasted_iota(jnp.int32, q.shape, 1)
    zero = jnp.zeros_like(q)
    return jnp.concatenate([jnp.where(lane < DA_QK, q, zero), jnp.where(lane >= DA_QK, q, zero)], axis=0)


def _attn_kernel(lam_ref, q_ref, k_ref, vt_ref, o_ref, acc_ref, s_ref, *, tq, tk, lam_init):
    i = pl.program_id(1)
    heads = range(DA_HEADS)
    qz = [_split_components(q_ref[0, :, h * HEAD_W:(h + 1) * HEAD_W]) for h in heads]
    acc_ref[...] = jnp.zeros_like(acc_ref)

    def scores(h, j):
        k_t = k_ref[0, pl.ds(pl.multiple_of(j * tk, tk), tk), h * HEAD_W:(h + 1) * HEAD_W]
        return _dot(k_t, qz[h], _NT)

    def step(j, ms, last):
        if last:
            kpos = j * tk + lax.broadcasted_iota(jnp.int32, (tk, 1), 0)
            qpos = i * tq + lax.broadcasted_iota(jnp.int32, (1, 2 * tq), 1) % tq
            visible = kpos < (qpos // CHUNK + 1) * CHUNK
        out = []
        s = s_ref[...]
        for h in heads:
            s_next = None
            if h + 1 < DA_HEADS:
                s_next = scores(h + 1, j)
            elif not last:
                s_next = scores(0, j + 1)
            if last:
                s = jnp.where(visible, s, NEG_BIG)
            m_new = jnp.maximum(ms[h], jnp.max(s, axis=0, keepdims=True))
            alpha = jnp.exp2(ms[h] - m_new)
            p = jnp.exp2(s - m_new).astype(_BF16)
            acc_ref[h] = alpha * acc_ref[h] + _dot(vt_ref[0, j, h * VT_ROWS:(h + 1) * VT_ROWS, :], p)
            out.append(m_new)
            s = s_next
        if not last:
            s_ref[...] = s
        return tuple(out)

    n_full = (i * tq) // tk
    s_ref[...] = scores(0, 0)
    init = tuple(jnp.full((1, 2 * tq), NEG_BIG, _F32) for _ in heads)
    ms = lax.fori_loop(0, n_full, lambda j, c: step(j, c, False), init)
    step(n_full, ms, True)

    lam = _lambda(lam_ref, lam_init)
    for h in heads:
        acc = acc_ref[h]
        o = acc[:DA_V] / acc[DA_V:DA_V + 1]
        o_ref[0, :, h * HEAD_W:(h + 1) * HEAD_W] = (o[:, :tq] - lam * o[:, tq:]).T.astype(_BF16)


def _attn_prompt(lamv, q, kb, vt, *, lam_init):
    nb, s, _ = q.shape
    nk, tk = vt.shape[1], vt.shape[3]
    tq = ATTN_TQ
    assert tk % tq == 0
    return pl.pallas_call(
        functools.partial(_attn_kernel, tq=tq, tk=tk, lam_init=lam_init),
        grid=(nb, s // tq),
        in_specs=[pl.BlockSpec((4, DA_QK), lambda b, i: (0, 0)),
                  pl.BlockSpec((1, tq, GROUP_W), lambda b, i: (b, i, 0)),
                  pl.BlockSpec((1, s, GROUP_W), lambda b, i: (b, 0, 0)),
                  pl.BlockSpec((1, nk, DA_HEADS * VT_ROWS, tk), lambda b, i: (b, 0, 0, 0))],
        out_specs=pl.BlockSpec((1, tq, GROUP_W), lambda b, i: (b, i, 0)),
        out_shape=jax.ShapeDtypeStruct((nb, s, GROUP_W), _BF16),
        scratch_shapes=[pltpu.VMEM((DA_HEADS, VT_ROWS, 2 * tq), _F32), pltpu.VMEM((tk, 2 * tq), _F32)],
        compiler_params=_cparams(
            ("parallel", "arbitrary"),
            [_nbytes((tq, GROUP_W), _BF16), _nbytes((s, GROUP_W), _BF16), _nbytes(vt.shape[1:], _BF16),
             _nbytes((tq, GROUP_W), _BF16)],
            _nbytes((DA_HEADS, VT_ROWS, 2 * tq), _F32) + 5 * _nbytes((tk, 2 * tq), _F32)), name="attn",
    )(lamv, q, kb, vt)


def _attn_sample_kernel(lam_ref, q_ref, kn_ref, vn_ref, kc_ref, vc_ref, o_ref, *, lq, past, lam_init):
    lam = _lambda(lam_ref, lam_init)
    for h in range(DA_HEADS):
        sl = slice(h * HEAD_W, (h + 1) * HEAD_W)
        qz = _split_components(q_ref[0, :, sl])
        kct = kc_ref[0, sl, :].astype(_BF16)
        vc = vc_ref[0, pl.ds(h, past, stride=DA_HEADS), :].astype(_BF16)
        s_c = _dot(qz, kct)
        s_n = _dot(qz, kn_ref[0, :, sl], _NT)
        m = jnp.maximum(jnp.max(s_c, axis=-1, keepdims=True), jnp.max(s_n, axis=-1, keepdims=True))
        p_c = jnp.exp2(s_c - m)
        p_n = jnp.exp2(s_n - m)
        l = jnp.sum(p_c, axis=-1, keepdims=True) + jnp.sum(p_n, axis=-1, keepdims=True)
        o = (_dot(p_c.astype(_BF16), vc) + _dot(p_n.astype(_BF16), vn_ref[0, :, sl])) / l
        o_ref[0, :, sl] = (o[:lq] - lam * o[lq:]).astype(_BF16)


def _attn_sample(lamv, q, kn, vn, kct, vc, *, lam_init):
    nb, lq, _ = q.shape
    past = kct.shape[2]
    new = pl.BlockSpec((1, lq, GROUP_W), lambda b: (b, 0, 0))
    return pl.pallas_call(
        functools.partial(_attn_sample_kernel, lq=lq, past=past, lam_init=lam_init),
        grid=(nb,),
        in_specs=[pl.BlockSpec((4, DA_QK), lambda b: (0, 0)), new, new, new,
                  pl.BlockSpec((1, GROUP_W, past), lambda b: (b, 0, 0)),
                  pl.BlockSpec((1, past * DA_HEADS, DA_V), lambda b: (b, 0, 0))],
        out_specs=new, out_shape=jax.ShapeDtypeStruct((nb, lq, GROUP_W), _BF16),
        compiler_params=_cparams(
            ("parallel",), [_nbytes((GROUP_W, past), _F32), _nbytes((past * DA_HEADS, DA_V), _F32)]
            + 4 * [_nbytes((lq, GROUP_W), _F32)], 8 * _nbytes((2 * lq, past), _F32)), name="attn_sample",
    )(lamv, q, kn, vn, kct, vc)


def _mlstm_kernel(q_ref, k_ref, v_ref, gc_ref, gr_ref, c0_ref, n0_ref, m0_ref, h_ref, c_ref, n_ref, m_ref,
                  c_sc, n_sc, m_sc, *, chunk, nchunks):
    j = pl.program_id(1)

    @pl.when(j == 0)
    def _():
        c_sc[...] = c0_ref[0]
        n_sc[...] = n0_ref[0]
        m_sc[...] = m0_ref[0]

    L = chunk
    r_i = lax.broadcasted_iota(jnp.int32, (L, L), 0)
    c_i = lax.broadcasted_iota(jnp.int32, (L, L), 1)
    causal = r_i >= c_i
    tril = causal.astype(_F32)
    triu = (r_i <= c_i).astype(_F32)
    hi = lax.Precision.HIGHEST

    for c in range(nchunks):
        rows = slice(c * L, (c + 1) * L)
        gcol = gc_ref[0, rows, :]
        grow = gr_ref[0, :, rows]
        fc_all = _dot(tril, gcol, precision=hi)
        fr_all = _dot(grow, triu, precision=hi)
        for hh in range(ML_HEADS):
            sl = slice(hh * HEAD_W, (hh + 1) * HEAD_W)
            m = m_sc[hh][0:1, 0:1]
            fc = fc_all[:, ML_HEADS + hh:ML_HEADS + hh + 1]
            fr = fr_all[ML_HEADS + hh:ML_HEADS + hh + 1, :]
            igc = gcol[:, hh:hh + 1]
            igr = grow[hh:hh + 1, :]
            a = jnp.where(causal, fc - fr + igr, -jnp.inf)
            b = fc + m
            m_t = jnp.maximum(b, jnp.max(a, axis=-1, keepdims=True))
            w = jnp.exp(a - m_t)
            sc = jnp.exp(b - m_t)
            q = q_ref[0, rows, sl]
            k = k_ref[0, rows, sl]
            v = v_ref[0, rows, sl]
            wqk = w * _dot(q, k, _NT)
            cmat = c_sc[hh]
            nrow = n_sc[hh:hh + 1, :]
            qn = jnp.sum(q.astype(_F32) * nrow, axis=-1, keepdims=True)
            num = _dot(wqk.astype(_BF16), v) + sc * _dot(q, cmat.astype(_BF16))
            den = jnp.sum(wqk, axis=-1, keepdims=True) + sc * qn
            h_ref[0, rows, sl] = (num / jnp.maximum(jnp.abs(den), jnp.exp(-m_t))).astype(_BF16)

            fl = fc[L - 1:L, :]
            g = fl - fc + igc
            m_new = jnp.maximum(fl + m, jnp.max(g, axis=0, keepdims=True))
            decay = jnp.exp(fl + m - m_new)
            kw = k.astype(_F32) * jnp.exp(g - m_new)
            c_sc[hh] = decay * cmat + _dot(kw.astype(_BF16), v, _TN)
            n_sc[hh:hh + 1, :] = decay * nrow + jnp.sum(kw, axis=0, keepdims=True)
            m_sc[hh] = jnp.broadcast_to(m_new, m_sc.shape[1:])

    @pl.when(j == pl.num_programs(1) - 1)
    def _():
        c_ref[0] = c_sc[...]
        n_ref[0] = n_sc[...]
        m_ref[0] = m_sc[...]


def _mlstm(q, k, v, gc, gr, c0, n0, m0, *, chunk, block):
    nb, s, _ = q.shape
    tok = lambda w: pl.BlockSpec((1, block, w), lambda b, j: (b, j, 0))
    cspec = pl.BlockSpec((1, ML_HEADS, ML_DK, ML_DV), lambda b, j: (b, 0, 0, 0))
    nspec = pl.BlockSpec((1, ML_HEADS, ML_DK), lambda b, j: (b, 0, 0))
    mspec = pl.BlockSpec((1, ML_HEADS, SUBLANES, HEAD_W), lambda b, j: (b, 0, 0, 0))
    return pl.pallas_call(
        functools.partial(_mlstm_kernel, chunk=chunk, nchunks=block // chunk),
        grid=(nb, s // block),
        in_specs=[tok(GROUP_W), tok(GROUP_W), tok(GROUP_W), tok(N_GATES),
                  pl.BlockSpec((1, N_GATES, block), lambda b, j: (b, 0, j)), cspec, nspec, mspec],
        out_specs=[tok(GROUP_W), cspec, nspec, mspec],
        out_shape=[jax.ShapeDtypeStruct((nb, s, GROUP_W), _BF16), jax.ShapeDtypeStruct(c0.shape, _F32),
                   jax.ShapeDtypeStruct(n0.shape, _F32), jax.ShapeDtypeStruct(m0.shape, _F32)],
        scratch_shapes=[pltpu.VMEM((ML_HEADS, ML_DK, ML_DV), _F32), pltpu.VMEM((ML_HEADS, ML_DK), _F32),
                        pltpu.VMEM((ML_HEADS, SUBLANES, HEAD_W), _F32)],
        compiler_params=_cparams(
            ("parallel", "arbitrary"), 5 * [_nbytes((block, GROUP_W), _F32)] + 2 * [_nbytes(c0.shape[1:], _F32)],
            2 * _nbytes(c0.shape[1:], _F32)), name="mlstm",
    )(q, k, v, gc, gr, c0, n0, m0)


def _mlstm_pairs_kernel(k_ref, qt_ref, vt_ref, gs_ref, st0_ref, m0_ref, h_ref, st_ref, m_ref, st_sc, m_sc,
                        *, block):
    j = pl.program_id(1)

    @pl.when(j == 0)
    def _():
        st_sc[...] = st0_ref[0]
        m_sc[...] = m0_ref[0]

    L, W = CHUNK, 2 * CHUNK
    rows8 = lambda n: gs_ref[0, n * N_GATES:(n + 1) * N_GATES, :]
    f8, d8, cm8, mg8, ws8 = (rows8(n) for n in range(5))
    d_cols = d8.T

    s_i = lax.broadcasted_iota(jnp.int32, (W, W), 0)
    t_i = lax.broadcasted_iota(jnp.int32, (W, W), 1)
    allowed = (s_i // L == t_i // L) & (s_i <= t_i)
    lane_w = lax.broadcasted_iota(jnp.int32, (1, W), 1)
    first = lane_w < L

    sts = [st_sc[hh] for hh in range(ML_HEADS)]
    ms = [m_sc[hh][0:1, 0:1] for hh in range(ML_HEADS)]

    def independent(p, hh):
        lanes = slice(p * W, (p + 1) * W)
        feat = slice(hh * HEAD_W, (hh + 1) * HEAD_W)
        kp = k_ref[0, lanes, feat]
        qt = qt_ref[0, 0, feat, lanes]
        v1t = vt_ref[0, 0, hh * VT_ROWS:(hh + 1) * VT_ROWS, lanes]
        ws = ws8[hh:hh + 1, lanes]
        v1f = v1t.astype(_F32)
        return dict(
            p=p, hh=hh, qt=qt, v1t=v1t, qk_t=_dot(kp, qt),
            upd_a=_dot((v1f * jnp.where(first, ws, 0.0)).astype(_BF16), kp),
            upd_b=_dot((v1f * jnp.where(first, 0.0, ws)).astype(_BF16), kp))

    def dependent(u):
        p, hh, qt = u["p"], u["hh"], u["qt"]
        lanes = slice(p * W, (p + 1) * W)
        row = lambda x: x[hh:hh + 1, lanes]
        cm, f_row = row(cm8), row(f8)
        w_t = jnp.exp(jnp.where(allowed, d_cols[lanes, hh:hh + 1] - cm, -jnp.inf))
        intra = _dot(u["v1t"], (w_t * u["qk_t"]).astype(_BF16))
        st_a, m_a = sts[hh], ms[hh]
        fl_a, mg_a = f_row[:, L - 1:L], row(mg8)[:, 0:1]
        fl_b, mg_b = f_row[:, W - 1:W], row(mg8)[:, L:L + 1]
        m_b = jnp.maximum(fl_a + m_a, mg_a)
        st_b = jnp.exp(fl_a + m_a - m_b) * st_a + jnp.exp(mg_a - m_b) * u["upd_a"]
        m_c = jnp.maximum(fl_b + m_b, mg_b)
        sts[hh] = jnp.exp(fl_b + m_b - m_c) * st_b + jnp.exp(mg_b - m_c) * u["upd_b"]
        ms[hh] = m_c
        inter = jnp.where(first, _dot(st_a.astype(_BF16), qt), _dot(st_b.astype(_BF16), qt))
        m_prev = jnp.where(first, m_a, m_b)
        mt = jnp.maximum(m_prev, cm)
        tot = jnp.exp(cm - mt) * intra + jnp.exp(m_prev - mt) * inter
        den = jnp.maximum(jnp.abs(tot[ML_DV:ML_DV + 1]), jnp.exp(-(f_row + mt)))
        h_ref[0, lanes, hh * HEAD_W:(hh + 1) * HEAD_W] = (tot[:ML_DV] / den).T.astype(_BF16)

    units = [(p, hh) for p in range(block // W) for hh in range(ML_HEADS)]
    nxt = independent(*units[0])
    for n in range(len(units)):
        cur = nxt
        if n + 1 < len(units):
            nxt = independent(*units[n + 1])
        dependent(cur)
    for hh in range(ML_HEADS):
        st_sc[hh] = sts[hh]
        m_sc[hh] = jnp.broadcast_to(ms[hh], m_sc.shape[1:])

    @pl.when(j == pl.num_programs(1) - 1)
    def _():
        st_ref[0] = st_sc[...]
        m_ref[0] = m_sc[...]


def _mlstm_pairs(k, qt, vt, gs, st0, m0):
    nb, s, _ = k.shape
    block = qt.shape[3]
    nt = s // block
    state = pl.BlockSpec((1, ML_HEADS, VT_ROWS, ML_DK), lambda b, j: (b, 0, 0, 0))
    mspec = pl.BlockSpec((1, ML_HEADS, SUBLANES, HEAD_W), lambda b, j: (b, 0, 0, 0))
    tok = pl.BlockSpec((1, block, GROUP_W), lambda b, j: (b, j, 0))
    return pl.pallas_call(
        functools.partial(_mlstm_pairs_kernel, block=block),
        grid=(nb, nt),
        in_specs=[tok, pl.BlockSpec((1, 1, GROUP_W, block), lambda b, j: (b, j, 0, 0)),
                  pl.BlockSpec((1, 1, ML_HEADS * VT_ROWS, block), lambda b, j: (b, j, 0, 0)),
                  pl.BlockSpec((1, 5 * N_GATES, block), lambda b, j: (b, 0, j)), state, mspec],
        out_specs=[tok, state, mspec],
        out_shape=[jax.ShapeDtypeStruct((nb, s, GROUP_W), _BF16), jax.ShapeDtypeStruct(st0.shape, _F32),
                   jax.ShapeDtypeStruct(m0.shape, _F32)],
        scratch_shapes=[pltpu.VMEM((ML_HEADS, VT_ROWS, ML_DK), _F32), pltpu.VMEM((ML_HEADS, SUBLANES, HEAD_W), _F32)],
        compiler_params=_cparams(
            ("parallel", "arbitrary"),
            3 * [_nbytes((block, GROUP_W), _BF16)] + [_nbytes((ML_HEADS * VT_ROWS, block), _BF16)]
            + 2 * [_nbytes(st0.shape[1:], _F32)], 16 * _nbytes((VT_ROWS, 2 * CHUNK), _F32) * ML_HEADS), name="mlstm_pairs",
    )(k, qt, vt, gs, st0, m0)


def _mix_kernel(da_ref, ml_ref, og_ref, x_ref, subln_ref, wo_ref, g2_ref, wr_ref, br_ref, cnt0_ref,
                xmid_ref, xp_ref, idx_ref, gate_ref, rank_ref, cnt_ref, cnt_sc, upper_sc, *, tm, lam_init):
    step = pl.program_id(0)

    @pl.when(step == 0)
    def _():
        cnt_sc[...] = cnt0_ref[...]
        earlier = lax.broadcasted_iota(jnp.int32, (tm, tm), 0) < lax.broadcasted_iota(jnp.int32, (tm, tm), 1)
        upper_sc[...] = earlier.astype(_BF16)

    parts = []
    for h in range(DA_HEADS):
        d = da_ref[:, h * HEAD_W:(h + 1) * HEAD_W].astype(_F32)
        d = d * lax.rsqrt(jnp.mean(d * d, axis=-1, keepdims=True) + EPS) * subln_ref[...]
        parts.append((d * (1.0 - lam_init)).astype(_BF16))
    parts.append((og_ref[...].astype(_F32) * ml_ref[...].astype(_F32)).astype(_BF16))
    xm = x_ref[...] + _dot(jnp.concatenate(parts, axis=1), wo_ref[...])
    xmid_ref[...] = xm
    xn_f = xm * lax.rsqrt(jnp.mean(xm * xm, axis=-1, keepdims=True) + EPS) * g2_ref[...]
    xp_ref[...] = _pack_rows(xn_f)
    xn = xn_f.astype(_BF16)

    logits = _dot(wr_ref[...], xn, _NT) + br_ref[...]
    e_i = lax.broadcasted_iota(jnp.int32, logits.shape, 0)
    member = jnp.zeros(logits.shape, jnp.bool_)
    vals, idxs = [], []
    for _ in range(TOP_K):
        mx = jnp.max(logits, axis=0, keepdims=True)
        sel = jnp.min(jnp.where(logits == mx, e_i, N_EXPERTS), axis=0, keepdims=True)
        hit = e_i == sel
        member = member | hit
        logits = jnp.where(hit, -jnp.inf, logits)
        vals.append(mx)
        idxs.append(sel)
    ex = [jnp.exp(v - vals[0]) for v in vals]
    tot = ex[0] + ex[1] + ex[2] + ex[3]
    idx_ref[...] = jnp.concatenate(idxs, axis=0)
    gate_ref[...] = jnp.concatenate([e / tot for e in ex], axis=0)

    memf = member.astype(_F32)
    before = _dot(memf.astype(_BF16), upper_sc[...]) + cnt_sc[...]
    rank_ref[...] = jnp.concatenate(
        [jnp.sum(jnp.where(e_i == s, before, 0.0), axis=0, keepdims=True) for s in idxs], axis=0).astype(jnp.int32)
    cnt_sc[...] += jnp.sum(memf, axis=1, keepdims=True)
    cnt_ref[...] = cnt_sc[...]


def _mix(da, ml, og, x, subln, w_out, norm2, w_rt, b_r, cnt0, *, tm, lam_init):
    t = x.shape[0]
    tok = lambda w: pl.BlockSpec((tm, w), lambda i: (i, 0))
    const = lambda shape: pl.BlockSpec(shape, lambda i: (0,) * len(shape))
    lane_tok = pl.BlockSpec((TOP_K, tm), lambda i: (0, i))
    return pl.pallas_call(
        functools.partial(_mix_kernel, tm=tm, lam_init=lam_init),
        grid=(t // tm,),
        in_specs=[tok(GROUP_W), tok(GROUP_W), tok(GROUP_W), tok(D_MODEL), const((1, DA_V)),
                  const((D_MODEL, D_MODEL)), const((1, D_MODEL)), const((N_EXPERTS, D_MODEL)),
                  const((N_EXPERTS, 1)), const((N_EXPERTS, 1))],
        out_specs=[tok(D_MODEL), tok(PACK_W), lane_tok, lane_tok, lane_tok, const((N_EXPERTS, 1))],
        out_shape=[jax.ShapeDtypeStruct((t, D_MODEL), _F32), jax.ShapeDtypeStruct((t, PACK_W), jnp.int32),
                   jax.ShapeDtypeStruct((TOP_K, t), jnp.int32), jax.ShapeDtypeStruct((TOP_K, t), _F32),
                   jax.ShapeDtypeStruct((TOP_K, t), jnp.int32), jax.ShapeDtypeStruct((N_EXPERTS, 1), _F32)],
        scratch_shapes=[pltpu.VMEM((N_EXPERTS, 1), _F32), pltpu.VMEM((tm, tm), _BF16)],
        compiler_params=_cparams(
            ("arbitrary",),
            3 * [_nbytes((tm, GROUP_W), _BF16)] + 2 * [_nbytes((tm, D_MODEL), _F32)] + [_nbytes((tm, PACK_W), _F32)]
            + [_nbytes((D_MODEL, D_MODEL), _BF16)], 4 * _nbytes((tm, D_MODEL), _F32) + 2 * _nbytes((tm, tm), _F32)), name="mix",
    )(da, ml, og, x, subln, w_out, norm2, w_rt, b_r, cnt0)


def _moe_kernel(be_ref, nu_ref, nv_ref, slot_ref, nxt_ref, x_ref, wgu_hbm, bgu_ref, wd_hbm, bd_ref, y_ref,
                wgu_f32, wd_f32, wgu_sc, wd_sc, sems):
    j = pl.program_id(0)
    used = j < nu_ref[0]
    fresh = (j == 0) | (be_ref[j] != be_ref[jnp.maximum(j - 1, 0)])
    slot = slot_ref[j]

    def fetch(e, s):
        return (pltpu.make_async_copy(wgu_hbm.at[e], wgu_f32.at[s], sems.at[s, 0]),
                pltpu.make_async_copy(wd_hbm.at[e], wd_f32.at[s], sems.at[s, 1]))

    @pl.when(j == 0)
    def _():
        for cp in fetch(be_ref[0], slot):
            cp.start()

    @pl.when(fresh & used)
    def _():
        for cp in fetch(be_ref[j], slot):
            cp.wait()

        @pl.when(nxt_ref[j] >= 0)
        def _():
            for cp in fetch(nxt_ref[j], 1 - slot):
                cp.start()

        rows = CAST_ROWS
        def cast_gu(r, _):
            sl = pl.ds(pl.multiple_of(r * rows, rows), rows)
            wgu_sc[sl, :] = wgu_f32[slot, sl, :].astype(_BF16)
            return 0
        lax.fori_loop(0, D_MODEL // rows, cast_gu, 0)
        def cast_d(r, _):
            sl = pl.ds(pl.multiple_of(r * rows, rows), rows)
            wd_sc[sl, :] = wd_f32[slot, sl, :].astype(_BF16)
            return 0
        lax.fori_loop(0, D_FF // rows, cast_d, 0)

    @pl.when(j >= nu_ref[0])
    def _():
        y_ref[...] = jnp.zeros_like(y_ref)

    def ffn(row0):
        hi, lo = _unpack_rows(x_ref[row0:, :])
        x = jnp.concatenate([hi.astype(_BF16), lo.astype(_BF16)], axis=1)
        n_chunks = D_FF // FF_CHUNK

        def gate_up(c):
            lo = c * FF_CHUNK
            return (_dot(x, wgu_sc[:, lo:lo + FF_CHUNK]), _dot(x, wgu_sc[:, D_FF + lo:D_FF + lo + FF_CHUNK]))

        acc = None
        nxt = gate_up(0)
        for c in range(n_chunks):
            lo = c * FF_CHUNK
            gate, up = nxt
            if c + 1 < n_chunks:
                nxt = gate_up(c + 1)
            gate = jnp.minimum(gate + bgu_ref[0, :, lo:lo + FF_CHUNK], SWIGLU_LIMIT)
            up = jnp.clip(up + bgu_ref[0, :, D_FF + lo:D_FF + lo + FF_CHUNK], -SWIGLU_LIMIT, SWIGLU_LIMIT)
            act = (up + 1.0) * (gate * jax.nn.sigmoid(SWIGLU_ALPHA * gate))
            down = _dot(act.astype(_BF16), wd_sc[lo:lo + FF_CHUNK, :])
            acc = down if acc is None else acc + down
        y_ref[row0:, :] = _pack_rows(acc + bd_ref[0])

    half = MOE_BLOCK // 2
    @pl.when(used & (nv_ref[j] > half))
    def _():
        ffn(0)

    @pl.when(used & (nv_ref[j] <= half))
    def _():
        y_ref[:half, :] = jnp.zeros((half, PACK_W), jnp.int32)
        ffn(half)


def _moe(blk_e, n_used, blk_rows, blk_slot, blk_next, xb, w_gu, b_gu, w_d, b_d):
    n_blocks = xb.shape[0] // MOE_BLOCK
    last = lambda j, nu: jnp.minimum(j, nu[0] - 1)
    row_spec = pl.BlockSpec((MOE_BLOCK, PACK_W), lambda j, be, nu, nv, sl, nx: (last(j, nu), 0))
    bias_spec = lambda c: pl.BlockSpec((1, 1, c), lambda j, be, nu, nv, sl, nx: (be[last(j, nu)], 0, 0))
    hbm = pl.BlockSpec(memory_space=pl.ANY)
    grid_spec = pltpu.PrefetchScalarGridSpec(
        num_scalar_prefetch=5, grid=(n_blocks,),
        in_specs=[row_spec, hbm, bias_spec(2 * D_FF), hbm, bias_spec(D_MODEL)],
        out_specs=pl.BlockSpec((MOE_BLOCK, PACK_W), lambda j, be, nu, nv, sl, nx: (j, 0)),
        scratch_shapes=[pltpu.VMEM((2, D_MODEL, 2 * D_FF), _F32), pltpu.VMEM((2, D_FF, D_MODEL), _F32),
                        pltpu.VMEM((D_MODEL, 2 * D_FF), _BF16), pltpu.VMEM((D_FF, D_MODEL), _BF16),
                        pltpu.SemaphoreType.DMA((2, 2))])
    return pl.pallas_call(
        _moe_kernel, grid_spec=grid_spec, out_shape=jax.ShapeDtypeStruct(xb.shape, jnp.int32),
        compiler_params=_cparams(
            ("arbitrary",), 2 * [_nbytes((MOE_BLOCK, PACK_W), _F32)],
            2 * _nbytes((D_MODEL, 3 * D_FF), _F32) + _nbytes((D_MODEL, 3 * D_FF), _BF16)
            + 4 * _nbytes((MOE_BLOCK, D_MODEL), _F32)), name="moe",
    )(blk_e, n_used, blk_rows, blk_slot, blk_next, xb, w_gu, b_gu, w_d, b_d)


def _sc_mesh():
    return plsc.VectorSubcoreMesh(core_axis_name="c", subcore_axis_name="s")


def _sc_worker():
    return lax.axis_index("s") * SC_CORES + lax.axis_index("c")


def _sc_dispatch(sources, n_rows):
    n_src = len(sources)
    shapes = []
    for _, dest3 in sources:
        n_chunks, _, chunk = dest3.shape
        per_worker = n_chunks // SC_WORKERS
        assert per_worker * SC_WORKERS == n_chunks and chunk % SUBLANES == 0
        shapes.append((per_worker, chunk))

    def body(*refs):
        x_hbms, d_hbms, o_hbm = refs[:n_src], refs[n_src:2 * n_src], refs[2 * n_src]
        scratch = refs[2 * n_src + 1:]
        load_a, load_b, scat_a, scat_b = scratch[3 * n_src:]
        worker = _sc_worker()
        for i, (per_worker, chunk) in enumerate(shapes):
            x_hbm, idx_v, rows_a, rows_b = x_hbms[i], scratch[3 * i], scratch[3 * i + 1], scratch[3 * i + 2]
            pltpu.sync_copy(d_hbms[i].at[worker], idx_v)

            def load(ci, rows, sem):
                row0 = (worker * per_worker + ci) * chunk
                return pltpu.make_async_copy(x_hbm.at[pl.ds(pl.multiple_of(row0, SUBLANES), chunk)], rows, sem)

            def scatters(ci, rows, sem):
                return [pltpu.make_async_copy(rows, o_hbm.at[idx_v.at[ci * TOP_K + k]], sem) for k in range(TOP_K)]

            def start(copies):
                for cp in copies:
                    cp.start()

            def wait(copies):
                for cp in copies:
                    cp.wait()

            load(0, rows_a, load_a).start()

            @pl.loop(0, per_worker // 2)
            def _(p):
                ca, cb = 2 * p, 2 * p + 1
                load(ca, rows_a, load_a).wait()
                load(cb, rows_b, load_b).start()
                start(scatters(ca, rows_a, scat_a))
                load(cb, rows_b, load_b).wait()
                wait(scatters(ca, rows_a, scat_a))
                load(jnp.minimum(ca + 2, per_worker - 1), rows_a, load_a).start()
                start(scatters(cb, rows_b, scat_b))
                wait(scatters(cb, rows_b, scat_b))

            last = per_worker - 1
            load(last, rows_a, load_a).wait()
            if per_worker % 2:
                start(scatters(last, rows_a, scat_a))
                wait(scatters(last, rows_a, scat_a))

    dma = pltpu.SemaphoreType.DMA
    scratch_types = []
    for per_worker, chunk in shapes:
        scratch_types += [pltpu.VMEM((per_worker * TOP_K, chunk), jnp.int32), pltpu.VMEM((chunk, PACK_W), jnp.int32),
                          pltpu.VMEM((chunk, PACK_W), jnp.int32)]
    tables = [dest3.reshape(SC_WORKERS, pw * TOP_K, chunk) for (_, dest3), (pw, chunk) in zip(sources, shapes)]
    return pl.kernel(
        body, out_type=jax.ShapeDtypeStruct((n_rows, PACK_W), jnp.int32), mesh=_sc_mesh(),
        scratch_types=scratch_types + [dma, dma, dma, dma],
        name="sc_dispatch")(*[xp for xp, _ in sources], *tables)


def _sc_gather(table, idx2):
    n_chunks, chunk = idx2.shape
    per_worker = n_chunks // SC_WORKERS
    assert per_worker * SC_WORKERS == n_chunks and chunk % SUBLANES == 0

    def body(t_hbm, i_hbm, o_hbm, idx_v, rows_a, rows_b, gath_a, gath_b, put_a, put_b):
        worker = _sc_worker()
        pltpu.sync_copy(i_hbm.at[worker], idx_v)

        def gather(ci, rows, sem):
            return pltpu.make_async_copy(t_hbm.at[idx_v.at[ci]], rows, sem)

        def put(ci, rows, sem):
            row0 = (worker * per_worker + ci) * chunk
            return pltpu.make_async_copy(rows, o_hbm.at[pl.ds(pl.multiple_of(row0, SUBLANES), chunk)], sem)

        gather(0, rows_a, gath_a).start()

        @pl.loop(0, per_worker // 2)
        def _(p):
            ca, cb = 2 * p, 2 * p + 1
            gather(ca, rows_a, gath_a).wait()
            gather(cb, rows_b, gath_b).start()
            put(ca, rows_a, put_a).start()
            gather(cb, rows_b, gath_b).wait()
            put(ca, rows_a, put_a).wait()
            gather(jnp.minimum(ca + 2, per_worker - 1), rows_a, gath_a).start()
            put(cb, rows_b, put_b).start()
            put(cb, rows_b, put_b).wait()

        last = per_worker - 1
        gather(last, rows_a, gath_a).wait()
        if per_worker % 2:
            put(last, rows_a, put_a).start()
            put(last, rows_a, put_a).wait()

    dma = pltpu.SemaphoreType.DMA
    return pl.kernel(
        body, out_type=jax.ShapeDtypeStruct((n_chunks * chunk, PACK_W), jnp.int32), mesh=_sc_mesh(),
        scratch_types=[pltpu.VMEM((per_worker, chunk), jnp.int32), pltpu.VMEM((chunk, PACK_W), jnp.int32),
                       pltpu.VMEM((chunk, PACK_W), jnp.int32), dma, dma, dma, dma],
        name="sc_gather")(table, idx2.reshape(SC_WORKERS, per_worker, chunk))


def _final_kernel(x_ref, y_ref, g_ref, nf_ref, o_ref):
    g_rows = g_ref[...]
    g = jnp.concatenate([g_rows, jnp.zeros((SUBLANES - TOP_K, g_rows.shape[1]), _F32)], axis=0).T
    hi = jnp.zeros((x_ref.shape[0], PACK_W), _F32)
    lo = jnp.zeros((x_ref.shape[0], PACK_W), _F32)
    for k in range(TOP_K):
        y_hi, y_lo = _unpack_rows(y_ref[k])
        hi = hi + g[:, k:k + 1] * y_hi
        lo = lo + g[:, k:k + 1] * y_lo
    x = x_ref[...] + jnp.concatenate([hi, lo], axis=1)
    o_ref[...] = x * lax.rsqrt(jnp.mean(x * x, axis=-1, keepdims=True) + EPS) * nf_ref[...]


def _final(xmid, yg, gates, norm_f, *, tm, first_block):
    t = xmid.shape[0]
    return pl.pallas_call(
        _final_kernel, grid=(t // tm,),
        in_specs=[pl.BlockSpec((tm, D_MODEL), lambda i: (i, 0)),
                  pl.BlockSpec((TOP_K, tm, PACK_W), lambda i: (0, i + first_block, 0)),
                  pl.BlockSpec((TOP_K, tm), lambda i: (0, i)),
                  pl.BlockSpec((1, D_MODEL), lambda i: (0, 0))],
        out_specs=pl.BlockSpec((tm, D_MODEL), lambda i: (i, 0)),
        out_shape=jax.ShapeDtypeStruct((t, D_MODEL), _F32),
        compiler_params=_cparams(
            ("parallel",), 2 * [_nbytes((tm, D_MODEL), _F32)] + [_nbytes((TOP_K, tm, PACK_W), _F32)],
            2 * _nbytes((tm, D_MODEL), _F32)), name="final",
    )(xmid, yg, gates, norm_f)


def kernel(x_prompt, x_sample, cache_k, cache_v, state_C, state_n, state_m, norm1, w_in, b_igate, b_fgate,
           lambda_q1, lambda_k1, lambda_q2, lambda_k2, subln, w_out, norm2, w_router, b_router, w_gate_up,
           b_gate_up, w_down, b_down, norm_f):
    nb, s, _ = x_prompt.shape
    db, ls, _ = x_sample.shape
    past = cache_k.shape[2]
    depth = w_in.shape[0]
    assert depth == 1 and s % PROJ_TILE == 0 and (db * ls) % SUBLANES == 0
    l = 0
    lam_init = 0.8 - 0.6 * math.exp(-0.3 * l)
    t_p, t_s = nb * s, db * ls

    w_nat = jnp.pad(w_in[l].T, ((0, HEAD_W - N_GATES), (0, 0))).astype(_BF16)
    b_gates = jnp.pad(jnp.concatenate([b_igate[l], b_fgate[l]]).astype(_F32), (0, HEAD_W - N_GATES))[None]
    inv = ROPE_THETA ** (-jnp.arange(0, DA_QK, 2, dtype=_F32) / DA_QK)
    inv_full = jnp.tile(inv, HEAD_W // (DA_QK // 2))[None]
    lamv = jnp.stack([lambda_q1[l], lambda_k1[l], lambda_q2[l], lambda_k2[l]]).astype(_F32)
    g1 = norm1[l][None]

    pp = _proj(x_prompt, g1, w_nat, b_gates, inv_full, tm=PROJ_TILE, pos_base=0, pos_mod=s,
               names=("q", "kf", "kb", "vf", "vt", "mk", "og", "gs", "mqt", "mvt"))
    ps = _proj(x_sample.reshape(1, t_s, D_MODEL), g1, w_nat, b_gates, inv_full, tm=t_s, pos_base=past, pos_mod=ls,
               names=("q", "kf", "kb", "vf", "mq", "mk", "mv", "og", "gc", "gr"))
    kf_p, vf_p, kf_s, vf_s = pp["kf"], pp["vf"], ps["kf"], ps["vf"]

    da_p = _attn_prompt(lamv, pp["q"], pp["kb"], pp["vt"], lam_init=lam_init)
    per_b = lambda a: a.reshape(db, ls, a.shape[-1])
    kct = jnp.transpose(cache_k[l], (0, 2, 3, 4, 1)).reshape(db, GROUP_W, past)
    vc = cache_v[l].reshape(db, past * DA_HEADS, DA_V)
    vn = vf_s.reshape(db, ls, GROUP_W).astype(_BF16)
    da_s = _attn_sample(lamv, per_b(ps["q"]), per_b(ps["kb"]), vn, kct, vc, lam_init=lam_init)

    assert PROJ_TILE % (2 * CHUNK) == 0
    h_p, st_p, m_p = _mlstm_pairs(pp["mk"], pp["mqt"], pp["mvt"], pp["gs"],
                                  jnp.zeros((nb, ML_HEADS, VT_ROWS, ML_DK), _F32),
                                  jnp.zeros((nb, ML_HEADS, SUBLANES, HEAD_W), _F32))
    gr_sb = ps["gr"].reshape(8, db, ls).transpose(1, 0, 2)
    m0_s = jnp.broadcast_to(state_m[l].astype(_F32)[:, :, None, None], (db, ML_HEADS, SUBLANES, HEAD_W))
    h_s, c_s, n_s, m_s = _mlstm(per_b(ps["mq"]), per_b(ps["mk"]), per_b(ps["mv"]), per_b(ps["gc"]), gr_sb,
                                state_C[l].astype(_F32), state_n[l].astype(_F32), m0_s, chunk=ls, block=ls)
    c_p, n_p, mm_p = jnp.swapaxes(st_p[:, :, :ML_DV, :], -1, -2), st_p[:, :, ML_DV, :], m_p[:, :, 0, 0]
    mm_s = m_s[:, :, 0, 0]

    wo = w_out[l].astype(_BF16)
    w_rt = w_router[l].T.astype(_BF16)
    b_r = b_router[l].astype(_F32)[:, None]
    mix = functools.partial(_mix, subln=subln[l][None], w_out=wo, norm2=norm2[l][None], w_rt=w_rt, b_r=b_r,
                            lam_init=lam_init)
    flat = lambda a: a.reshape(-1, a.shape[-1])
    t_all = t_p + t_s
    assert t_p % t_s == 0
    xm_p, xp_p, idx_p, gate_p, rank_p, cnt_p = mix(
        flat(da_p), flat(h_p), flat(pp["og"]), flat(x_prompt), cnt0=jnp.zeros((N_EXPERTS, 1), _F32), tm=PROJ_TILE)
    xm_s, xp_s, idx_s, gate_s, rank_s, cnt = mix(
        flat(da_s), flat(h_s), flat(ps["og"]), flat(x_sample), cnt0=cnt_p, tm=t_s)

    rows = t_all * TOP_K
    n_blocks = -(-rows // MOE_BLOCK) + N_EXPERTS
    counts = cnt[:, 0].astype(jnp.int32)
    padded = (counts + MOE_BLOCK - 1) // MOE_BLOCK * MOE_BLOCK
    pend = jnp.cumsum(padded)
    pstart = pend - counts
    experts = jnp.arange(N_EXPERTS, dtype=jnp.int32)[:, None, None]
    first_row = lambda idx: jnp.sum(jnp.where(idx[None] == experts, pstart[:, None, None], 0), axis=0)
    dest_p, dest_s = first_row(idx_p) + rank_p, first_row(idx_s) + rank_s
    n_used = (pend[-1] // MOE_BLOCK).astype(jnp.int32)[None]
    blk_lo = jnp.arange(n_blocks, dtype=jnp.int32) * MOE_BLOCK
    blk_e = jnp.minimum(jnp.sum((pend[None, :] <= blk_lo[:, None]).astype(jnp.int32), axis=1), N_EXPERTS - 1)
    row_lo = jnp.sum(jnp.where(blk_e[:, None] == experts[:, 0].T, pstart[None, :], 0), axis=1)
    blk_rows = jnp.where(blk_lo < pend[-1], jnp.clip(blk_lo + MOE_BLOCK - row_lo, 0, MOE_BLOCK), 0).astype(jnp.int32)

    def chunked(dest):
        chunk = min(DISPATCH_CHUNK, dest.shape[1] // SC_WORKERS)
        return dest.reshape(TOP_K, dest.shape[1] // chunk, chunk).transpose(1, 0, 2)

    assert rows % (SC_WORKERS * COMBINE_CHUNK) == 0
    xb = _sc_dispatch([(xp_p, chunked(dest_p)), (xp_s, chunked(dest_s))], n_blocks * MOE_BLOCK)
    dest = jnp.concatenate([dest_p, dest_s], axis=1)
    first_blk = jnp.concatenate([jnp.ones((1,), jnp.bool_), blk_e[1:] != blk_e[:-1]])
    blk_slot = ((jnp.cumsum(first_blk.astype(jnp.int32)) - 1) % 2).astype(jnp.int32)
    after = jnp.sum(jnp.where(blk_e[:, None] == experts[:, 0].T, pend[None, :], 0), axis=1) // MOE_BLOCK
    blk_next = jnp.where(after < n_used[0], blk_e[jnp.minimum(after, n_blocks - 1)], -1).astype(jnp.int32)
    yb = _moe(blk_e, n_used, blk_rows, blk_slot, blk_next, xb, w_gate_up[l], b_gate_up[l][:, None, :], w_down[l], b_down[l][:, None, :])
    yg = _sc_gather(yb, dest.reshape(rows // COMBINE_CHUNK, COMBINE_CHUNK)).reshape(TOP_K, t_all, PACK_W)
    nf = norm_f[None]
    y_p = _final(xm_p, yg, gate_p, nf, tm=FINAL_TILE, first_block=0).reshape(nb, s, D_MODEL)
    y_s = _final(xm_s, yg, gate_s, nf, tm=t_s, first_block=t_p // t_s).reshape(db, ls, D_MODEL)

    st = lambda a: a[None]
    return (y_p, y_s,
            st(kf_p.reshape(nb, s, DA_HEADS, 2, DA_QK)), st(vf_p.reshape(nb, s, DA_HEADS, DA_V)),
            st(c_p), st(n_p), st(mm_p),
            st(kf_s.reshape(db, ls, DA_HEADS, 2, DA_QK)), st(vf_s.reshape(db, ls, DA_HEADS, DA_V)),
            st(c_s.astype(state_C.dtype)), st(n_s.astype(state_n.dtype)), st(mm_s.astype(state_m.dtype)))
```

```python
import functools
import math

import jax
import jax.numpy as jnp
from jax import lax
from jax.experimental import pallas as pl
from jax.experimental.pallas import tpu as pltpu
from jax.experimental.pallas import tpu_sc as plsc

D_MODEL = 1024
CHUNK = 64
DA_HEADS = 4
DA_QK = 64
DA_V = 128
ML_HEADS = 4
ML_DK = 128
ML_DV = 128
HEAD_W = 128
SUBLANES = 8
N_GATES = 2 * ML_HEADS
CAST_ROWS = 128
GROUP_W = 512
ROPE_THETA = 10000.0
N_EXPERTS = 32
TOP_K = 4
D_FF = 1024
SWIGLU_LIMIT = 7.0
SWIGLU_ALPHA = 1.702
EPS = 1e-6
NEG_BIG = -1e30

LOG2E = math.log2(math.e)
VT_ROWS = 144
PROJ_TILE = 512
FINAL_TILE = 1024
ATTN_TQ = 512
MOE_BLOCK = 512
FF_CHUNK = 512
PACK_W = D_MODEL // 2
SC_CORES = 2
SC_SUBCORES = 16
SC_WORKERS = SC_CORES * SC_SUBCORES
DISPATCH_CHUNK = 64
COMBINE_CHUNK = 104
VMEM_BYTES = 64 * 1024 * 1024

_F32 = jnp.float32
_BF16 = jnp.bfloat16


def _nbytes(shape, dtype):
    return math.prod(shape) * jnp.dtype(dtype).itemsize


def _cparams(sem, blocks, resident=0):
    need = 2 * sum(blocks) + resident
    assert need <= VMEM_BYTES, need
    return pltpu.CompilerParams(dimension_semantics=sem, vmem_limit_bytes=need)


def _dot(a, b, dims=(((1,), (0,)), ((), ())), precision=None):
    return lax.dot_general(a, b, dims, precision=precision, preferred_element_type=_F32)


_NT = (((1,), (1,)), ((), ()))
_TN = (((0,), (0,)), ((), ()))


def _pack_rows(x):
    half = x.shape[1] // 2
    bits = lambda v: lax.bitcast_convert_type(v.astype(_BF16).astype(_F32), jnp.int32)
    return bits(x[:, :half]) | lax.shift_right_logical(bits(x[:, half:]), 16)


def _unpack_rows(p):
    hi = lax.bitcast_convert_type(p & jnp.int32(-65536), _F32)
    lo = lax.bitcast_convert_type(lax.shift_left(p, 16), _F32)
    return hi, lo


def _chunk_gate_rows(g8, block):
    L = CHUNK
    pos = lax.broadcasted_iota(jnp.int32, (1, block), 1) % L
    roll = lambda x, sh: pltpu.roll(x, sh % block, 1)
    steps = [1 << b for b in range(L.bit_length() - 1)]

    def scan(x, combine, fill, reverse=False):
        for sh in steps:
            ok = (pos < L - sh) if reverse else (pos >= sh)
            x = combine(x, jnp.where(ok, roll(x, -sh if reverse else sh), fill))
        return x

    to_head_rows = lambda x: pltpu.roll(x, ML_HEADS, 0)
    f8 = to_head_rows(scan(g8, jnp.add, 0.0))
    d8 = g8 - f8
    cm8 = scan(d8, jnp.maximum, -jnp.inf)
    big_g = to_head_rows(scan(g8, jnp.add, 0.0, reverse=True) - g8) + g8
    mg8 = jnp.maximum(scan(big_g, jnp.maximum, -jnp.inf), scan(big_g, jnp.maximum, -jnp.inf, reverse=True))
    return jnp.concatenate([f8, d8, cm8, mg8, jnp.exp(big_g - mg8)], axis=0)


def _proj_kernel(x_ref, g_ref, wt_ref, bg_ref, inv_ref, *refs, names, tm, pos_base, pos_mod):
    out = dict(zip(names, refs))
    cos_sc, sin_sc = refs[len(names):]
    x = x_ref[0]
    xn = (x * lax.rsqrt(jnp.mean(x * x, axis=-1, keepdims=True) + EPS) * g_ref[...]).astype(_BF16)

    row = lax.broadcasted_iota(jnp.int32, (tm, 1), 0)
    if pos_mod % tm == 0:
        tile_pos, row_pos = pos_base + (pl.program_id(1) * tm) % pos_mod, row
    else:
        assert tm % pos_mod == 0
        tile_pos, row_pos = pos_base, row % pos_mod

    @pl.when((pl.program_id(0) == 0) & (pl.program_id(1) == 0))
    def _():
        ang_row = row_pos.astype(_F32) * inv_ref[...]
        cos_sc[...] = jnp.cos(ang_row)
        sin_sc[...] = jnp.sin(ang_row)

    ang_tile = jnp.full((1, 1), tile_pos, jnp.int32).astype(_F32) * inv_ref[...]
    cos_t, sin_t = jnp.cos(ang_tile), jnp.sin(ang_tile)
    cos = cos_t * cos_sc[...] - sin_t * sin_sc[...]
    sin = sin_t * cos_sc[...] + cos_t * sin_sc[...]
    lane = lax.broadcasted_iota(jnp.int32, (1, HEAD_W), 1)
    first = (lane % DA_QK) < (DA_QK // 2)
    sin = jnp.where(first, -sin, sin)

    def rope(z):
        rot = jnp.where(first, pltpu.roll(z, HEAD_W - DA_QK // 2, 1), pltpu.roll(z, DA_QK // 2, 1))
        return z * cos + rot * sin

    def group(c, width=GROUP_W):
        return _dot(xn, wt_ref[c * GROUP_W:c * GROUP_W + width, :], _NT)

    def put_q(zq):
        for h in range(DA_HEADS):
            sl = slice(h * HEAD_W, (h + 1) * HEAD_W)
            out["q"][0, :, sl] = (rope(zq[:, sl]) * (DA_QK ** -0.5 * LOG2E)).astype(_BF16)

    def put_k(zk):
        for h in range(DA_HEADS):
            sl = slice(h * HEAD_W, (h + 1) * HEAD_W)
            rk = rope(zk[:, sl])
            out["kf"][0, :, sl] = rk
            out["kb"][0, :, sl] = rk.astype(_BF16)

    def with_ones_rows(zt):
        ones = jnp.ones((VT_ROWS - DA_V, tm), _BF16)
        return jnp.concatenate(
            [part for h in range(DA_HEADS) for part in (zt[h * DA_V:(h + 1) * DA_V], ones)], axis=0)

    def put_v(zv):
        for h in range(DA_HEADS):
            out["vf"][0, pl.ds(h, tm, stride=DA_HEADS), :] = zv[:, h * DA_V:(h + 1) * DA_V]
        if "vt" in out:
            out["vt"][0, 0] = with_ones_rows(zv.T.astype(_BF16))

    def put_mq(z):
        if "mq" in out:
            out["mq"][0] = z.astype(_BF16)
        if "mqt" in out:
            out["mqt"][0, 0] = z.T.astype(_BF16)

    def put_mk(z):
        out["mk"][0] = (z * (ML_DK ** -0.5)).astype(_BF16)

    def put_mv(z):
        if "mv" in out:
            out["mv"][0] = z.astype(_BF16)
        if "mvt" in out:
            out["mvt"][0, 0] = with_ones_rows(z.T.astype(_BF16))

    def put_mo(z):
        out["og"][0] = jax.nn.sigmoid(z).astype(_BF16)

    def put_gates(z):
        zg = z + bg_ref[...]
        lane_g = lax.broadcasted_iota(jnp.int32, zg.shape, 1)
        logsig = jnp.minimum(zg, 0.0) - jnp.log(1.0 + jnp.exp(-jnp.abs(zg)))
        gates = jnp.where(lane_g < ML_HEADS, zg, logsig)
        if "gc" in out:
            out["gc"][0] = gates[:, :N_GATES]
        g8 = gates.T[:N_GATES, :]
        if "gr" in out:
            out["gr"][0] = g8
        if "gs" in out:
            out["gs"][0] = _chunk_gate_rows(g8, tm)

    sinks = [(put_q, GROUP_W), (put_k, GROUP_W), (put_v, GROUP_W), (put_mq, GROUP_W), (put_mk, GROUP_W),
             (put_mv, GROUP_W), (put_mo, GROUP_W), (put_gates, HEAD_W)]
    pairs = [(c, sinks[c:c + 2]) for c in range(0, len(sinks), 2)]
    pairs = pairs[-1:] + pairs[:-1]
    width = lambda pair: sum(w for _, w in pair)
    nxt = group(pairs[0][0], width(pairs[0][1]))
    for n, (_, pair) in enumerate(pairs):
        cur = nxt
        if n + 1 < len(pairs):
            nxt = group(pairs[n + 1][0], width(pairs[n + 1][1]))
        pair[0][0](cur[:, :GROUP_W])
        pair[1][0](cur[:, GROUP_W:])


def _proj(x3, norm1, wt, b_gates, inv_full, *, names, tm, pos_base, pos_mod):
    nb, s, _ = x3.shape
    nt = s // tm
    tok = lambda w, dt: (jax.ShapeDtypeStruct((nb, s, w), dt), pl.BlockSpec((1, tm, w), lambda b, i: (b, i, 0)))
    slab = lambda rows: (jax.ShapeDtypeStruct((nb, nt, rows, tm), _BF16),
                         pl.BlockSpec((1, 1, rows, tm), lambda b, i: (b, i, 0, 0)))
    const = lambda shape: pl.BlockSpec(shape, lambda b, i: (0,) * len(shape))
    kinds = {
        "q": tok(GROUP_W, _BF16), "kf": tok(GROUP_W, _F32), "kb": tok(GROUP_W, _BF16),
        "vf": (jax.ShapeDtypeStruct((nb, s * DA_HEADS, DA_V), _F32),
               pl.BlockSpec((1, tm * DA_HEADS, DA_V), lambda b, i: (b, i, 0))),
        "vt": slab(DA_HEADS * VT_ROWS),
        "mq": tok(GROUP_W, _BF16), "mk": tok(GROUP_W, _BF16), "mv": tok(GROUP_W, _BF16), "og": tok(GROUP_W, _BF16),
        "gc": tok(N_GATES, _F32),
        "gr": (jax.ShapeDtypeStruct((nb, N_GATES, s), _F32), pl.BlockSpec((1, N_GATES, tm), lambda b, i: (b, 0, i))),
        "gs": (jax.ShapeDtypeStruct((nb, 5 * N_GATES, s), _F32),
               pl.BlockSpec((1, 5 * N_GATES, tm), lambda b, i: (b, 0, i))),
        "mqt": slab(GROUP_W), "mvt": slab(ML_HEADS * VT_ROWS),
    }
    blocks = [_nbytes((tm, D_MODEL), _F32), _nbytes(wt.shape, _BF16)]
    blocks += [_nbytes(kinds[n][1].block_shape, kinds[n][0].dtype) for n in names]
    outs = pl.pallas_call(
        functools.partial(_proj_kernel, names=names, tm=tm, pos_base=pos_base, pos_mod=pos_mod),
        grid=(nb, nt),
        in_specs=[pl.BlockSpec((1, tm, D_MODEL), lambda b, i: (b, i, 0)), const((1, D_MODEL)),
                  const(wt.shape), const((1, HEAD_W)), const((1, HEAD_W))],
        out_specs=[kinds[n][1] for n in names], out_shape=[kinds[n][0] for n in names],
        scratch_shapes=[pltpu.VMEM((tm, HEAD_W), _F32), pltpu.VMEM((tm, HEAD_W), _F32)],
        compiler_params=_cparams(("arbitrary", "arbitrary"), blocks, 8 * _nbytes((tm, GROUP_W), _F32)), name="proj",
    )(x3, norm1, wt, b_gates, inv_full)
    return dict(zip(names, outs))


def _lambda(lam_ref, lam_init):
    lv = lam_ref[...]
    s1 = jnp.sum(lv[0:1] * lv[1:2], axis=-1, keepdims=True)
    s2 = jnp.sum(lv[2:3] * lv[3:4], axis=-1, keepdims=True)
    return jnp.exp(s1) - jnp.exp(s2) + lam_init


def _split_components(q):
    lane = lax.broadcasted_iota(jnp.int32, q.shape, 1)
    zero = jnp.zeros_like(q)
    return jnp.concatenate([jnp.where(lane < DA_QK, q, zero), jnp.where(lane >= DA_QK, q, zero)], axis=0)


def _attn_kernel(lam_ref, q_ref, k_ref, vt_ref, o_ref, acc_ref, s_ref, *, tq, tk, lam_init):
    i = pl.program_id(1)
    heads = range(DA_HEADS)
    qz = [_split_components(q_ref[0, :, h * HEAD_W:(h + 1) * HEAD_W]) for h in heads]
    acc_ref[...] = jnp.zeros_like(acc_ref)

    def scores(h, j):
        k_t = k_ref[0, pl.ds(pl.multiple_of(j * tk, tk), tk), h * HEAD_W:(h + 1) * HEAD_W]
        return _dot(k_t, qz[h], _NT)

    def step(j, ms, last):
        if last:
            kpos = j * tk + lax.broadcasted_iota(jnp.int32, (tk, 1), 0)
            qpos = i * tq + lax.broadcasted_iota(jnp.int32, (1, 2 * tq), 1) % tq
            visible = kpos < (qpos // CHUNK + 1) * CHUNK
        out = []
        s = s_ref[...]
        for h in heads:
            s_next = None
            if h + 1 < DA_HEADS:
                s_next = scores(h + 1, j)
            elif not last:
                s_next = scores(0, j + 1)
            if last:
                s = jnp.where(visible, s, NEG_BIG)
            m_new = jnp.maximum(ms[h], jnp.max(s, axis=0, keepdims=True))
            alpha = jnp.exp2(ms[h] - m_new)
            p = jnp.exp2(s - m_new).astype(_BF16)
            acc_ref[h] = alpha * acc_ref[h] + _dot(vt_ref[0, j, h * VT_ROWS:(h + 1) * VT_ROWS, :], p)
            out.append(m_new)
            s = s_next
        if not last:
            s_ref[...] = s
        return tuple(out)

    n_full = (i * tq) // tk
    s_ref[...] = scores(0, 0)
    init = tuple(jnp.full((1, 2 * tq), NEG_BIG, _F32) for _ in heads)
    ms = lax.fori_loop(0, n_full, lambda j, c: step(j, c, False), init)
    step(n_full, ms, True)

    lam = _lambda(lam_ref, lam_init)
    for h in heads:
        acc = acc_ref[h]
        o = acc[:DA_V] / acc[DA_V:DA_V + 1]
        o_ref[0, :, h * HEAD_W:(h + 1) * HEAD_W] = (o[:, :tq] - lam * o[:, tq:]).T.astype(_BF16)


def _attn_prompt(lamv, q, kb, vt, *, lam_init):
    nb, s, _ = q.shape
    nk, tk = vt.shape[1], vt.shape[3]
    tq = ATTN_TQ
    assert tk % tq == 0
    return pl.pallas_call(
        functools.partial(_attn_kernel, tq=tq, tk=tk, lam_init=lam_init),
        grid=(nb, s // tq),
        in_specs=[pl.BlockSpec((4, DA_QK), lambda b, i: (0, 0)),
                  pl.BlockSpec((1, tq, GROUP_W), lambda b, i: (b, i, 0)),
                  pl.BlockSpec((1, s, GROUP_W), lambda b, i: (b, 0, 0)),
                  pl.BlockSpec((1, nk, DA_HEADS * VT_ROWS, tk), lambda b, i: (b, 0, 0, 0))],
        out_specs=pl.BlockSpec((1, tq, GROUP_W), lambda b, i: (b, i, 0)),
        out_shape=jax.ShapeDtypeStruct((nb, s, GROUP_W), _BF16),
        scratch_shapes=[pltpu.VMEM((DA_HEADS, VT_ROWS, 2 * tq), _F32), pltpu.VMEM((tk, 2 * tq), _F32)],
        compiler_params=_cparams(
            ("parallel", "arbitrary"),
            [_nbytes((tq, GROUP_W), _BF16), _nbytes((s, GROUP_W), _BF16), _nbytes(vt.shape[1:], _BF16),
             _nbytes((tq, GROUP_W), _BF16)],
            _nbytes((DA_HEADS, VT_ROWS, 2 * tq), _F32) + 5 * _nbytes((tk, 2 * tq), _F32)), name="attn",
    )(lamv, q, kb, vt)


def _attn_sample_kernel(lam_ref, q_ref, kn_ref, vn_ref, kc_ref, vc_ref, o_ref, *, lq, past, lam_init):
    lam = _lambda(lam_ref, lam_init)
    for h in range(DA_HEADS):
        sl = slice(h * HEAD_W, (h + 1) * HEAD_W)
        qz = _split_components(q_ref[0, :, sl])
        kct = kc_ref[0, sl, :].astype(_BF16)
        vc = vc_ref[0, pl.ds(h, past, stride=DA_HEADS), :].astype(_BF16)
        s_c = _dot(qz, kct)
        s_n = _dot(qz, kn_ref[0, :, sl], _NT)
        m = jnp.maximum(jnp.max(s_c, axis=-1, keepdims=True), jnp.max(s_n, axis=-1, keepdims=True))
        p_c = jnp.exp2(s_c - m)
        p_n = jnp.exp2(s_n - m)
        l = jnp.sum(p_c, axis=-1, keepdims=True) + jnp.sum(p_n, axis=-1, keepdims=True)
        o = (_dot(p_c.astype(_BF16), vc) + _dot(p_n.astype(_BF16), vn_ref[0, :, sl])) / l
        o_ref[0, :, sl] = (o[:lq] - lam * o[lq:]).astype(_BF16)


def _attn_sample(lamv, q, kn, vn, kct, vc, *, lam_init):
    nb, lq, _ = q.shape
    past = kct.shape[2]
    new = pl.BlockSpec((1, lq, GROUP_W), lambda b: (b, 0, 0))
    return pl.pallas_call(
        functools.partial(_attn_sample_kernel, lq=lq, past=past, lam_init=lam_init),
        grid=(nb,),
        in_specs=[pl.BlockSpec((4, DA_QK), lambda b: (0, 0)), new, new, new,
                  pl.BlockSpec((1, GROUP_W, past), lambda b: (b, 0, 0)),
                  pl.BlockSpec((1, past * DA_HEADS, DA_V), lambda b: (b, 0, 0))],
        out_specs=new, out_shape=jax.ShapeDtypeStruct((nb, lq, GROUP_W), _BF16),
        compiler_params=_cparams(
            ("parallel",), [_nbytes((GROUP_W, past), _F32), _nbytes((past * DA_HEADS, DA_V), _F32)]
            + 4 * [_nbytes((lq, GROUP_W), _F32)], 8 * _nbytes((2 * lq, past), _F32)), name="attn_sample",
    )(lamv, q, kn, vn, kct, vc)


def _mlstm_kernel(q_ref, k_ref, v_ref, gc_ref, gr_ref, c0_ref, n0_ref, m0_ref, h_ref, c_ref, n_ref, m_ref,
                  c_sc, n_sc, m_sc, *, chunk, nchunks):
    j = pl.program_id(1)

    @pl.when(j == 0)
    def _():
        c_sc[...] = c0_ref[0]
        n_sc[...] = n0_ref[0]
        m_sc[...] = m0_ref[0]

    L = chunk
    r_i = lax.broadcasted_iota(jnp.int32, (L, L), 0)
    c_i = lax.broadcasted_iota(jnp.int32, (L, L), 1)
    causal = r_i >= c_i
    tril = causal.astype(_F32)
    triu = (r_i <= c_i).astype(_F32)
    hi = lax.Precision.HIGHEST

    for c in range(nchunks):
        rows = slice(c * L, (c + 1) * L)
        gcol = gc_ref[0, rows, :]
        grow = gr_ref[0, :, rows]
        fc_all = _dot(tril, gcol, precision=hi)
        fr_all = _dot(grow, triu, precision=hi)
        for hh in range(ML_HEADS):
            sl = slice(hh * HEAD_W, (hh + 1) * HEAD_W)
            m = m_sc[hh][0:1, 0:1]
            fc = fc_all[:, ML_HEADS + hh:ML_HEADS + hh + 1]
            fr = fr_all[ML_HEADS + hh:ML_HEADS + hh + 1, :]
            igc = gcol[:, hh:hh + 1]
            igr = grow[hh:hh + 1, :]
            a = jnp.where(causal, fc - fr + igr, -jnp.inf)
            b = fc + m
            m_t = jnp.maximum(b, jnp.max(a, axis=-1, keepdims=True))
            w = jnp.exp(a - m_t)
            sc = jnp.exp(b - m_t)
            q = q_ref[0, rows, sl]
            k = k_ref[0, rows, sl]
            v = v_ref[0, rows, sl]
            wqk = w * _dot(q, k, _NT)
            cmat = c_sc[hh]
            nrow = n_sc[hh:hh + 1, :]
            qn = jnp.sum(q.astype(_F32) * nrow, axis=-1, keepdims=True)
            num = _dot(wqk.astype(_BF16), v) + sc * _dot(q, cmat.astype(_BF16))
            den = jnp.sum(wqk, axis=-1, keepdims=True) + sc * qn
            h_ref[0, rows, sl] = (num / jnp.maximum(jnp.abs(den), jnp.exp(-m_t))).astype(_BF16)

            fl = fc[L - 1:L, :]
            g = fl - fc + igc
            m_new = jnp.maximum(fl + m, jnp.max(g, axis=0, keepdims=True))
            decay = jnp.exp(fl + m - m_new)
            kw = k.astype(_F32) * jnp.exp(g - m_new)
            c_sc[hh] = decay * cmat + _dot(kw.astype(_BF16), v, _TN)
            n_sc[hh:hh + 1, :] = decay * nrow + jnp.sum(kw, axis=0, keepdims=True)
            m_sc[hh] = jnp.broadcast_to(m_new, m_sc.shape[1:])

    @pl.when(j == pl.num_programs(1) - 1)
    def _():
        c_ref[0] = c_sc[...]
        n_ref[0] = n_sc[...]
        m_ref[0] = m_sc[...]


def _mlstm(q, k, v, gc, gr, c0, n0, m0, *, chunk, block):
    nb, s, _ = q.shape
    tok = lambda w: pl.BlockSpec((1, block, w), lambda b, j: (b, j, 0))
    cspec = pl.BlockSpec((1, ML_HEADS, ML_DK, ML_DV), lambda b, j: (b, 0, 0, 0))
    nspec = pl.BlockSpec((1, ML_HEADS, ML_DK), lambda b, j: (b, 0, 0))
    mspec = pl.BlockSpec((1, ML_HEADS, SUBLANES, HEAD_W), lambda b, j: (b, 0, 0, 0))
    return pl.pallas_call(
        functools.partial(_mlstm_kernel, chunk=chunk, nchunks=block // chunk),
        grid=(nb, s // block),
        in_specs=[tok(GROUP_W), tok(GROUP_W), tok(GROUP_W), tok(N_GATES),
                  pl.BlockSpec((1, N_GATES, block), lambda b, j: (b, 0, j)), cspec, nspec, mspec],
        out_specs=[tok(GROUP_W), cspec, nspec, mspec],
        out_shape=[jax.ShapeDtypeStruct((nb, s, GROUP_W), _BF16), jax.ShapeDtypeStruct(c0.shape, _F32),
                   jax.ShapeDtypeStruct(n0.shape, _F32), jax.ShapeDtypeStruct(m0.shape, _F32)],
        scratch_shapes=[pltpu.VMEM((ML_HEADS, ML_DK, ML_DV), _F32), pltpu.VMEM((ML_HEADS, ML_DK), _F32),
                        pltpu.VMEM((ML_HEADS, SUBLANES, HEAD_W), _F32)],
        compiler_params=_cparams(
            ("parallel", "arbitrary"), 5 * [_nbytes((block, GROUP_W), _F32)] + 2 * [_nbytes(c0.shape[1:], _F32)],
            2 * _nbytes(c0.shape[1:], _F32)), name="mlstm",
    )(q, k, v, gc, gr, c0, n0, m0)


def _mlstm_pairs_kernel(k_ref, qt_ref, vt_ref, gs_ref, st0_ref, m0_ref, h_ref, st_ref, m_ref, st_sc, m_sc,
                        *, block):
    j = pl.program_id(1)

    @pl.when(j == 0)
    def _():
        st_sc[...] = st0_ref[0]
        m_sc[...] = m0_ref[0]

    L, W = CHUNK, 2 * CHUNK
    rows8 = lambda n: gs_ref[0, n * N_GATES:(n + 1) * N_GATES, :]
    f8, d8, cm8, mg8, ws8 = (rows8(n) for n in range(5))
    d_cols = d8.T

    s_i = lax.broadcasted_iota(jnp.int32, (W, W), 0)
    t_i = lax.broadcasted_iota(jnp.int32, (W, W), 1)
    allowed = (s_i // L == t_i // L) & (s_i <= t_i)
    lane_w = lax.broadcasted_iota(jnp.int32, (1, W), 1)
    first = lane_w < L

    sts = [st_sc[hh] for hh in range(ML_HEADS)]
    ms = [m_sc[hh][0:1, 0:1] for hh in range(ML_HEADS)]

    def independent(p, hh):
        lanes = slice(p * W, (p + 1) * W)
        feat = slice(hh * HEAD_W, (hh + 1) * HEAD_W)
        kp = k_ref[0, lanes, feat]
        qt = qt_ref[0, 0, feat, lanes]
        v1t = vt_ref[0, 0, hh * VT_ROWS:(hh + 1) * VT_ROWS, lanes]
        ws = ws8[hh:hh + 1, lanes]
        v1f = v1t.astype(_F32)
        return dict(
            p=p, hh=hh, qt=qt, v1t=v1t, qk_t=_dot(kp, qt),
            upd_a=_dot((v1f * jnp.where(first, ws, 0.0)).astype(_BF16), kp),
            upd_b=_dot((v1f * jnp.where(first, 0.0, ws)).astype(_BF16), kp))

    def dependent(u):
        p, hh, qt = u["p"], u["hh"], u["qt"]
        lanes = slice(p * W, (p + 1) * W)
        row = lambda x: x[hh:hh + 1, lanes]
        cm, f_row = row(cm8), row(f8)
        w_t = jnp.exp(jnp.where(allowed, d_cols[lanes, hh:hh + 1] - cm, -jnp.inf))
        intra = _dot(u["v1t"], (w_t * u["qk_t"]).astype(_BF16))
        st_a, m_a = sts[hh], ms[hh]
        fl_a, mg_a = f_row[:, L - 1:L], row(mg8)[:, 0:1]
        fl_b, mg_b = f_row[:, W - 1:W], row(mg8)[:, L:L + 1]
        m_b = jnp.maximum(fl_a + m_a, mg_a)
        st_b = jnp.exp(fl_a + m_a - m_b) * st_a + jnp.exp(mg_a - m_b) * u["upd_a"]
        m_c = jnp.maximum(fl_b + m_b, mg_b)
        sts[hh] = jnp.exp(fl_b + m_b - m_c) * st_b + jnp.exp(mg_b - m_c) * u["upd_b"]
        ms[hh] = m_c
        inter = jnp.where(first, _dot(st_a.astype(_BF16), qt), _dot(st_b.astype(_BF16), qt))
        m_prev = jnp.where(first, m_a, m_b)
        mt = jnp.maximum(m_prev, cm)
        tot = jnp.exp(cm - mt) * intra + jnp.exp(m_prev - mt) * inter
        den = jnp.maximum(jnp.abs(tot[ML_DV:ML_DV + 1]), jnp.exp(-(f_row + mt)))
        h_ref[0, lanes, hh * HEAD_W:(hh + 1) * HEAD_W] = (tot[:ML_DV] / den).T.astype(_BF16)

    units = [(p, hh) for p in range(block // W) for hh in range(ML_HEADS)]
    nxt = independent(*units[0])
    for n in range(len(units)):
        cur = nxt
        if n + 1 < len(units):
            nxt = independent(*units[n + 1])
        dependent(cur)
    for hh in range(ML_HEADS):
        st_sc[hh] = sts[hh]
        m_sc[hh] = jnp.broadcast_to(ms[hh], m_sc.shape[1:])

    @pl.when(j == pl.num_programs(1) - 1)
    def _():
        st_ref[0] = st_sc[...]
        m_ref[0] = m_sc[...]


def _mlstm_pairs(k, qt, vt, gs, st0, m0):
    nb, s, _ = k.shape
    block = qt.shape[3]
    nt = s // block
    state = pl.BlockSpec((1, ML_HEADS, VT_ROWS, ML_DK), lambda b, j: (b, 0, 0, 0))
    mspec = pl.BlockSpec((1, ML_HEADS, SUBLANES, HEAD_W), lambda b, j: (b, 0, 0, 0))
    tok = pl.BlockSpec((1, block, GROUP_W), lambda b, j: (b, j, 0))
    return pl.pallas_call(
        functools.partial(_mlstm_pairs_kernel, block=block),
        grid=(nb, nt),
        in_specs=[tok, pl.BlockSpec((1, 1, GROUP_W, block), lambda b, j: (b, j, 0, 0)),
                  pl.BlockSpec((1, 1, ML_HEADS * VT_ROWS, block), lambda b, j: (b, j, 0, 0)),
                  pl.BlockSpec((1, 5 * N_GATES, block), lambda b, j: (b, 0, j)), state, mspec],
        out_specs=[tok, state, mspec],
        out_shape=[jax.ShapeDtypeStruct((nb, s, GROUP_W), _BF16), jax.ShapeDtypeStruct(st0.shape, _F32),
                   jax.ShapeDtypeStruct(m0.shape, _F32)],
        scratch_shapes=[pltpu.VMEM((ML_HEADS, VT_ROWS, ML_DK), _F32), pltpu.VMEM((ML_HEADS, SUBLANES, HEAD_W), _F32)],
        compiler_params=_cparams(
            ("parallel", "arbitrary"),
            3 * [_nbytes((block, GROUP_W), _BF16)] + [_nbytes((ML_HEADS * VT_ROWS, block), _BF16)]
            + 2 * [_nbytes(st0.shape[1:], _F32)], 16 * _nbytes((VT_ROWS, 2 * CHUNK), _F32) * ML_HEADS), name="mlstm_pairs",
    )(k, qt, vt, gs, st0, m0)


def _mix_kernel(da_ref, ml_ref, og_ref, x_ref, subln_ref, wo_ref, g2_ref, wr_ref, br_ref, cnt0_ref,
                xmid_ref, xp_ref, idx_ref, gate_ref, rank_ref, cnt_ref, cnt_sc, upper_sc, *, tm, lam_init):
    step = pl.program_id(0)

    @pl.when(step == 0)
    def _():
        cnt_sc[...] = cnt0_ref[...]
        earlier = lax.broadcasted_iota(jnp.int32, (tm, tm), 0) < lax.broadcasted_iota(jnp.int32, (tm, tm), 1)
        upper_sc[...] = earlier.astype(_BF16)

    parts = []
    for h in range(DA_HEADS):
        d = da_ref[:, h * HEAD_W:(h + 1) * HEAD_W].astype(_F32)
        d = d * lax.rsqrt(jnp.mean(d * d, axis=-1, keepdims=True) + EPS) * subln_ref[...]
        parts.append((d * (1.0 - lam_init)).astype(_BF16))
    parts.append((og_ref[...].astype(_F32) * ml_ref[...].astype(_F32)).astype(_BF16))
    xm = x_ref[...] + _dot(jnp.concatenate(parts, axis=1), wo_ref[...])
    xmid_ref[...] = xm
    xn_f = xm * lax.rsqrt(jnp.mean(xm * xm, axis=-1, keepdims=True) + EPS) * g2_ref[...]
    xp_ref[...] = _pack_rows(xn_f)
    xn = xn_f.astype(_BF16)

    logits = _dot(wr_ref[...], xn, _NT) + br_ref[...]
    e_i = lax.broadcasted_iota(jnp.int32, logits.shape, 0)
    member = jnp.zeros(logits.shape, jnp.bool_)
    vals, idxs = [], []
    for _ in range(TOP_K):
        mx = jnp.max(logits, axis=0, keepdims=True)
        sel = jnp.min(jnp.where(logits == mx, e_i, N_EXPERTS), axis=0, keepdims=True)
        hit = e_i == sel
        member = member | hit
        logits = jnp.where(hit, -jnp.inf, logits)
        vals.append(mx)
        idxs.append(sel)
    ex = [jnp.exp(v - vals[0]) for v in vals]
    tot = ex[0] + ex[1] + ex[2] + ex[3]
    idx_ref[...] = jnp.concatenate(idxs, axis=0)
    gate_ref[...] = jnp.concatenate([e / tot for e in ex], axis=0)

    memf = member.astype(_F32)
    before = _dot(memf.astype(_BF16), upper_sc[...]) + cnt_sc[...]
    rank_ref[...] = jnp.concatenate(
        [jnp.sum(jnp.where(e_i == s, before, 0.0), axis=0, keepdims=True) for s in idxs], axis=0).astype(jnp.int32)
    cnt_sc[...] += jnp.sum(memf, axis=1, keepdims=True)
    cnt_ref[...] = cnt_sc[...]


def _mix(da, ml, og, x, subln, w_out, norm2, w_rt, b_r, cnt0, *, tm, lam_init):
    t = x.shape[0]
    tok = lambda w: pl.BlockSpec((tm, w), lambda i: (i, 0))
    const = lambda shape: pl.BlockSpec(shape, lambda i: (0,) * len(shape))
    lane_tok = pl.BlockSpec((TOP_K, tm), lambda i: (0, i))
    return pl.pallas_call(
        functools.partial(_mix_kernel, tm=tm, lam_init=lam_init),
        grid=(t // tm,),
        in_specs=[tok(GROUP_W), tok(GROUP_W), tok(GROUP_W), tok(D_MODEL), const((1, DA_V)),
                  const((D_MODEL, D_MODEL)), const((1, D_MODEL)), const((N_EXPERTS, D_MODEL)),
                  const((N_EXPERTS, 1)), const((N_EXPERTS, 1))],
        out_specs=[tok(D_MODEL), tok(PACK_W), lane_tok, lane_tok, lane_tok, const((N_EXPERTS, 1))],
        out_shape=[jax.ShapeDtypeStruct((t, D_MODEL), _F32), jax.ShapeDtypeStruct((t, PACK_W), jnp.int32),
                   jax.ShapeDtypeStruct((TOP_K, t), jnp.int32), jax.ShapeDtypeStruct((TOP_K, t), _F32),
                   jax.ShapeDtypeStruct((TOP_K, t), jnp.int32), jax.ShapeDtypeStruct((N_EXPERTS, 1), _F32)],
        scratch_shapes=[pltpu.VMEM((N_EXPERTS, 1), _F32), pltpu.VMEM((tm, tm), _BF16)],
        compiler_params=_cparams(
            ("arbitrary",),
            3 * [_nbytes((tm, GROUP_W), _BF16)] + 2 * [_nbytes((tm, D_MODEL), _F32)] + [_nbytes((tm, PACK_W), _F32)]
            + [_nbytes((D_MODEL, D_MODEL), _BF16)], 4 * _nbytes((tm, D_MODEL), _F32) + 2 * _nbytes((tm, tm), _F32)), name="mix",
    )(da, ml, og, x, subln, w_out, norm2, w_rt, b_r, cnt0)


def _moe_kernel(be_ref, nu_ref, nv_ref, slot_ref, nxt_ref, x_ref, wgu_hbm, bgu_ref, wd_hbm, bd_ref, y_ref,
                wgu_f32, wd_f32, wgu_sc, wd_sc, sems):
    j = pl.program_id(0)
    used = j < nu_ref[0]
    fresh = (j == 0) | (be_ref[j] != be_ref[jnp.maximum(j - 1, 0)])
    slot = slot_ref[j]

    def fetch(e, s):
        return (pltpu.make_async_copy(wgu_hbm.at[e], wgu_f32.at[s], sems.at[s, 0]),
                pltpu.make_async_copy(wd_hbm.at[e], wd_f32.at[s], sems.at[s, 1]))

    @pl.when(j == 0)
    def _():
        for cp in fetch(be_ref[0], slot):
            cp.start()

    @pl.when(fresh & used)
    def _():
        for cp in fetch(be_ref[j], slot):
            cp.wait()

        @pl.when(nxt_ref[j] >= 0)
        def _():
            for cp in fetch(nxt_ref[j], 1 - slot):
                cp.start()

        rows = CAST_ROWS
        def cast_gu(r, _):
            sl = pl.ds(pl.multiple_of(r * rows, rows), rows)
            wgu_sc[sl, :] = wgu_f32[slot, sl, :].astype(_BF16)
            return 0
        lax.fori_loop(0, D_MODEL // rows, cast_gu, 0)
        def cast_d(r, _):
            sl = pl.ds(pl.multiple_of(r * rows, rows), rows)
            wd_sc[sl, :] = wd_f32[slot, sl, :].astype(_BF16)
            return 0
        lax.fori_loop(0, D_FF // rows, cast_d, 0)

    @pl.when(j >= nu_ref[0])
    def _():
        y_ref[...] = jnp.zeros_like(y_ref)

    def ffn(row0):
        hi, lo = _unpack_rows(x_ref[row0:, :])
        x = jnp.concatenate([hi.astype(_BF16), lo.astype(_BF16)], axis=1)
        n_chunks = D_FF // FF_CHUNK

        def gate_up(c):
            lo = c * FF_CHUNK
            return (_dot(x, wgu_sc[:, lo:lo + FF_CHUNK]), _dot(x, wgu_sc[:, D_FF + lo:D_FF + lo + FF_CHUNK]))

        acc = None
        nxt = gate_up(0)
        for c in range(n_chunks):
            lo = c * FF_CHUNK
            gate, up = nxt
            if c + 1 < n_chunks:
                nxt = gate_up(c + 1)
            gate = jnp.minimum(gate + bgu_ref[0, :, lo:lo + FF_CHUNK], SWIGLU_LIMIT)
            up = jnp.clip(up + bgu_ref[0, :, D_FF + lo:D_FF + lo + FF_CHUNK], -SWIGLU_LIMIT, SWIGLU_LIMIT)
            act = (up + 1.0) * (gate * jax.nn.sigmoid(SWIGLU_ALPHA * gate))
            down = _dot(act.astype(_BF16), wd_sc[lo:lo + FF_CHUNK, :])
            acc = down if acc is None else acc + down
        y_ref[row0:, :] = _pack_rows(acc + bd_ref[0])

    half = MOE_BLOCK // 2
    @pl.when(used & (nv_ref[j] > half))
    def _():
        ffn(0)

    @pl.when(used & (nv_ref[j] <= half))
    def _():
        y_ref[:half, :] = jnp.zeros((half, PACK_W), jnp.int32)
        ffn(half)


def _moe(blk_e, n_used, blk_rows, blk_slot, blk_next, xb, w_gu, b_gu, w_d, b_d):
    n_blocks = xb.shape[0] // MOE_BLOCK
    last = lambda j, nu: jnp.minimum(j, nu[0] - 1)
    row_spec = pl.BlockSpec((MOE_BLOCK, PACK_W), lambda j, be, nu, nv, sl, nx: (last(j, nu), 0))
    bias_spec = lambda c: pl.BlockSpec((1, 1, c), lambda j, be, nu, nv, sl, nx: (be[last(j, nu)], 0, 0))
    hbm = pl.BlockSpec(memory_space=pl.ANY)
    grid_spec = pltpu.PrefetchScalarGridSpec(
        num_scalar_prefetch=5, grid=(n_blocks,),
        in_specs=[row_spec, hbm, bias_spec(2 * D_FF), hbm, bias_spec(D_MODEL)],
        out_specs=pl.BlockSpec((MOE_BLOCK, PACK_W), lambda j, be, nu, nv, sl, nx: (j, 0)),
        scratch_shapes=[pltpu.VMEM((2, D_MODEL, 2 * D_FF), _F32), pltpu.VMEM((2, D_FF, D_MODEL), _F32),
                        pltpu.VMEM((D_MODEL, 2 * D_FF), _BF16), pltpu.VMEM((D_FF, D_MODEL), _BF16),
                        pltpu.SemaphoreType.DMA((2, 2))])
    return pl.pallas_call(
        _moe_kernel, grid_spec=grid_spec, out_shape=jax.ShapeDtypeStruct(xb.shape, jnp.int32),
        compiler_params=_cparams(
            ("arbitrary",), 2 * [_nbytes((MOE_BLOCK, PACK_W), _F32)],
            2 * _nbytes((D_MODEL, 3 * D_FF), _F32) + _nbytes((D_MODEL, 3 * D_FF), _BF16)
            + 4 * _nbytes((MOE_BLOCK, D_MODEL), _F32)), name="moe",
    )(blk_e, n_used, blk_rows, blk_slot, blk_next, xb, w_gu, b_gu, w_d, b_d)


def _sc_mesh():
    return plsc.VectorSubcoreMesh(core_axis_name="c", subcore_axis_name="s")


def _sc_worker():
    return lax.axis_index("s") * SC_CORES + lax.axis_index("c")


def _sc_dispatch(sources, n_rows):
    n_src = len(sources)
    shapes = []
    for _, dest3 in sources:
        n_chunks, _, chunk = dest3.shape
        per_worker = n_chunks // SC_WORKERS
        assert per_worker * SC_WORKERS == n_chunks and chunk % SUBLANES == 0
        shapes.append((per_worker, chunk))

    def body(*refs):
        x_hbms, d_hbms, o_hbm = refs[:n_src], refs[n_src:2 * n_src], refs[2 * n_src]
        scratch = refs[2 * n_src + 1:]
        load_a, load_b, scat_a, scat_b = scratch[3 * n_src:]
        worker = _sc_worker()
        for i, (per_worker, chunk) in enumerate(shapes):
            x_hbm, idx_v, rows_a, rows_b = x_hbms[i], scratch[3 * i], scratch[3 * i + 1], scratch[3 * i + 2]
            pltpu.sync_copy(d_hbms[i].at[worker], idx_v)

            def load(ci, rows, sem):
                row0 = (worker * per_worker + ci) * chunk
                return pltpu.make_async_copy(x_hbm.at[pl.ds(pl.multiple_of(row0, SUBLANES), chunk)], rows, sem)

            def scatters(ci, rows, sem):
                return [pltpu.make_async_copy(rows, o_hbm.at[idx_v.at[ci * TOP_K + k]], sem) for k in range(TOP_K)]

            def start(copies):
                for cp in copies:
                    cp.start()

            def wait(copies):
                for cp in copies:
                    cp.wait()

            load(0, rows_a, load_a).start()

            @pl.loop(0, per_worker // 2)
            def _(p):
                ca, cb = 2 * p, 2 * p + 1
                load(ca, rows_a, load_a).wait()
                load(cb, rows_b, load_b).start()
                start(scatters(ca, rows_a, scat_a))
                load(cb, rows_b, load_b).wait()
                wait(scatters(ca, rows_a, scat_a))
                load(jnp.minimum(ca + 2, per_worker - 1), rows_a, load_a).start()
                start(scatters(cb, rows_b, scat_b))
                wait(scatters(cb, rows_b, scat_b))

            last = per_worker - 1
            load(last, rows_a, load_a).wait()
            if per_worker % 2:
                start(scatters(last, rows_a, scat_a))
                wait(scatters(last, rows_a, scat_a))

    dma = pltpu.SemaphoreType.DMA
    scratch_types = []
    for per_worker, chunk in shapes:
        scratch_types += [pltpu.VMEM((per_worker * TOP_K, chunk), jnp.int32), pltpu.VMEM((chunk, PACK_W), jnp.int32),
                          pltpu.VMEM((chunk, PACK_W), jnp.int32)]
    tables = [dest3.reshape(SC_WORKERS, pw * TOP_K, chunk) for (_, dest3), (pw, chunk) in zip(sources, shapes)]
    return pl.kernel(
        body, out_type=jax.ShapeDtypeStruct((n_rows, PACK_W), jnp.int32), mesh=_sc_mesh(),
        scratch_types=scratch_types + [dma, dma, dma, dma],
        name="sc_dispatch")(*[xp for xp, _ in sources], *tables)


def _sc_gather(table, idx2):
    n_chunks, chunk = idx2.shape
    per_worker = n_chunks // SC_WORKERS
    assert per_worker * SC_WORKERS == n_chunks and chunk % SUBLANES == 0

    def body(t_hbm, i_hbm, o_hbm, idx_v, rows_a, rows_b, gath_a, gath_b, put_a, put_b):
        worker = _sc_worker()
        pltpu.sync_copy(i_hbm.at[worker], idx_v)

        def gather(ci, rows, sem):
            return pltpu.make_async_copy(t_hbm.at[idx_v.at[ci]], rows, sem)

        def put(ci, rows, sem):
            row0 = (worker * per_worker + ci) * chunk
            return pltpu.make_async_copy(rows, o_hbm.at[pl.ds(pl.multiple_of(row0, SUBLANES), chunk)], sem)

        gather(0, rows_a, gath_a).start()

        @pl.loop(0, per_worker // 2)
        def _(p):
            ca, cb = 2 * p, 2 * p + 1
            gather(ca, rows_a, gath_a).wait()
            gather(cb, rows_b, gath_b).start()
            put(ca, rows_a, put_a).start()
            gather(cb, rows_b, gath_b).wait()
            put(ca, rows_a, put_a).wait()
            gather(jnp.minimum(ca + 2, per_worker - 1), rows_a, gath_a).start()
            put(cb, rows_b, put_b).start()
            put(cb, rows_b, put_b).wait()

        last = per_worker - 1
        gather(last, rows_a, gath_a).wait()
        if per_worker % 2:
            put(last, rows_a, put_a).start()
            put(last, rows_a, put_a).wait()

    dma = pltpu.SemaphoreType.DMA
    return pl.kernel(
        body, out_type=jax.ShapeDtypeStruct((n_chunks * chunk, PACK_W), jnp.int32), mesh=_sc_mesh(),
        scratch_types=[pltpu.VMEM((per_worker, chunk), jnp.int32), pltpu.VMEM((chunk, PACK_W), jnp.int32),
                       pltpu.VMEM((chunk, PACK_W), jnp.int32), dma, dma, dma, dma],
        name="sc_gather")(table, idx2.reshape(SC_WORKERS, per_worker, chunk))


def _final_kernel(x_ref, y_ref, g_ref, nf_ref, o_ref):
    g_rows = g_ref[...]
    g = jnp.concatenate([g_rows, jnp.zeros((SUBLANES - TOP_K, g_rows.shape[1]), _F32)], axis=0).T
    hi = jnp.zeros((x_ref.shape[0], PACK_W), _F32)
    lo = jnp.zeros((x_ref.shape[0], PACK_W), _F32)
    for k in range(TOP_K):
        y_hi, y_lo = _unpack_rows(y_ref[k])
        hi = hi + g[:, k:k + 1] * y_hi
        lo = lo + g[:, k:k + 1] * y_lo
    x = x_ref[...] + jnp.concatenate([hi, lo], axis=1)
    o_ref[...] = x * lax.rsqrt(jnp.mean(x * x, axis=-1, keepdims=True) + EPS) * nf_ref[...]


def _final(xmid, yg, gates, norm_f, *, tm, first_block):
    t = xmid.shape[0]
    return pl.pallas_call(
        _final_kernel, grid=(t // tm,),
        in_specs=[pl.BlockSpec((tm, D_MODEL), lambda i: (i, 0)),
                  pl.BlockSpec((TOP_K, tm, PACK_W), lambda i: (0, i + first_block, 0)),
                  pl.BlockSpec((TOP_K, tm), lambda i: (0, i)),
                  pl.BlockSpec((1, D_MODEL), lambda i: (0, 0))],
        out_specs=pl.BlockSpec((tm, D_MODEL), lambda i: (i, 0)),
        out_shape=jax.ShapeDtypeStruct((t, D_MODEL), _F32),
        compiler_params=_cparams(
            ("parallel",), 2 * [_nbytes((tm, D_MODEL), _F32)] + [_nbytes((TOP_K, tm, PACK_W), _F32)],
            2 * _nbytes((tm, D_MODEL), _F32)), name="final",
    )(xmid, yg, gates, norm_f)


def kernel(x_prompt, x_sample, cache_k, cache_v, state_C, state_n, state_m, norm1, w_in, b_igate, b_fgate,
           lambda_q1, lambda_k1, lambda_q2, lambda_k2, subln, w_out, norm2, w_router, b_router, w_gate_up,
           b_gate_up, w_down, b_down, norm_f):
    nb, s, _ = x_prompt.shape
    db, ls, _ = x_sample.shape
    past = cache_k.shape[2]
    depth = w_in.shape[0]
    assert depth == 1 and s % PROJ_TILE == 0 and (db * ls) % SUBLANES == 0
    l = 0
    lam_init = 0.8 - 0.6 * math.exp(-0.3 * l)
    t_p, t_s = nb * s, db * ls

    w_nat = jnp.pad(w_in[l].T, ((0, HEAD_W - N_GATES), (0, 0))).astype(_BF16)
    b_gates = jnp.pad(jnp.concatenate([b_igate[l], b_fgate[l]]).astype(_F32), (0, HEAD_W - N_GATES))[None]
    inv = ROPE_THETA ** (-jnp.arange(0, DA_QK, 2, dtype=_F32) / DA_QK)
    inv_full = jnp.tile(inv, HEAD_W // (DA_QK // 2))[None]
    lamv = jnp.stack([lambda_q1[l], lambda_k1[l], lambda_q2[l], lambda_k2[l]]).astype(_F32)
    g1 = norm1[l][None]

    pp = _proj(x_prompt, g1, w_nat, b_gates, inv_full, tm=PROJ_TILE, pos_base=0, pos_mod=s,
               names=("q", "kf", "kb", "vf", "vt", "mk", "og", "gs", "mqt", "mvt"))
    ps = _proj(x_sample.reshape(1, t_s, D_MODEL), g1, w_nat, b_gates, inv_full, tm=t_s, pos_base=past, pos_mod=ls,
               names=("q", "kf", "kb", "vf", "mq", "mk", "mv", "og", "gc", "gr"))
    kf_p, vf_p, kf_s, vf_s = pp["kf"], pp["vf"], ps["kf"], ps["vf"]

    da_p = _attn_prompt(lamv, pp["q"], pp["kb"], pp["vt"], lam_init=lam_init)
    per_b = lambda a: a.reshape(db, ls, a.shape[-1])
    kct = jnp.transpose(cache_k[l], (0, 2, 3, 4, 1)).reshape(db, GROUP_W, past)
    vc = cache_v[l].reshape(db, past * DA_HEADS, DA_V)
    vn = vf_s.reshape(db, ls, GROUP_W).astype(_BF16)
    da_s = _attn_sample(lamv, per_b(ps["q"]), per_b(ps["kb"]), vn, kct, vc, lam_init=lam_init)

    assert PROJ_TILE % (2 * CHUNK) == 0
    h_p, st_p, m_p = _mlstm_pairs(pp["mk"], pp["mqt"], pp["mvt"], pp["gs"],
                                  jnp.zeros((nb, ML_HEADS, VT_ROWS, ML_DK), _F32),
                                  jnp.zeros((nb, ML_HEADS, SUBLANES, HEAD_W), _F32))
    gr_sb = ps["gr"].reshape(8, db, ls).transpose(1, 0, 2)
    m0_s = jnp.broadcast_to(state_m[l].astype(_F32)[:, :, None, None], (db, ML_HEADS, SUBLANES, HEAD_W))
    h_s, c_s, n_s, m_s = _mlstm(per_b(ps["mq"]), per_b(ps["mk"]), per_b(ps["mv"]), per_b(ps["gc"]), gr_sb,
                                state_C[l].astype(_F32), state_n[l].astype(_F32), m0_s, chunk=ls, block=ls)
    c_p, n_p, mm_p = jnp.swapaxes(st_p[:, :, :ML_DV, :], -1, -2), st_p[:, :, ML_DV, :], m_p[:, :, 0, 0]
    mm_s = m_s[:, :, 0, 0]

    wo = w_out[l].astype(_BF16)
    w_rt = w_router[l].T.astype(_BF16)
    b_r = b_router[l].astype(_F32)[:, None]
    mix = functools.partial(_mix, subln=subln[l][None], w_out=wo, norm2=norm2[l][None], w_rt=w_rt, b_r=b_r,
                            lam_init=lam_init)
    flat = lambda a: a.reshape(-1, a.shape[-1])
    t_all = t_p + t_s
    assert t_p % t_s == 0
    xm_p, xp_p, idx_p, gate_p, rank_p, cnt_p = mix(
        flat(da_p), flat(h_p), flat(pp["og"]), flat(x_prompt), cnt0=jnp.zeros((N_EXPERTS, 1), _F32), tm=PROJ_TILE)
    xm_s, xp_s, idx_s, gate_s, rank_s, cnt = mix(
        flat(da_s), flat(h_s), flat(ps["og"]), flat(x_sample), cnt0=cnt_p, tm=t_s)

    rows = t_all * TOP_K
    n_blocks = -(-rows // MOE_BLOCK) + N_EXPERTS
    counts = cnt[:, 0].astype(jnp.int32)
    padded = (counts + MOE_BLOCK - 1) // MOE_BLOCK * MOE_BLOCK
    pend = jnp.cumsum(padded)
    pstart = pend - counts
    experts = jnp.arange(N_EXPERTS, dtype=jnp.int32)[:, None, None]
    first_row = lambda idx: jnp.sum(jnp.where(idx[None] == experts, pstart[:, None, None], 0), axis=0)
    dest_p, dest_s = first_row(idx_p) + rank_p, first_row(idx_s) + rank_s
    n_used = (pend[-1] // MOE_BLOCK).astype(jnp.int32)[None]
    blk_lo = jnp.arange(n_blocks, dtype=jnp.int32) * MOE_BLOCK
    blk_e = jnp.minimum(jnp.sum((pend[None, :] <= blk_lo[:, None]).astype(jnp.int32), axis=1), N_EXPERTS - 1)
    row_lo = jnp.sum(jnp.where(blk_e[:, None] == experts[:, 0].T, pstart[None, :], 0), axis=1)
    blk_rows = jnp.where(blk_lo < pend[-1], jnp.clip(blk_lo + MOE_BLOCK - row_lo, 0, MOE_BLOCK), 0).astype(jnp.int32)

    def chunked(dest):
        chunk = min(DISPATCH_CHUNK, dest.shape[1] // SC_WORKERS)
        return dest.reshape(TOP_K, dest.shape[1] // chunk, chunk).transpose(1, 0, 2)

    assert rows % (SC_WORKERS * COMBINE_CHUNK) == 0
    xb = _sc_dispatch([(xp_p, chunked(dest_p)), (xp_s, chunked(dest_s))], n_blocks * MOE_BLOCK)
    dest = jnp.concatenate([dest_p, dest_s], axis=1)
    first_blk = jnp.concatenate([jnp.ones((1,), jnp.bool_), blk_e[1:] != blk_e[:-1]])
    blk_slot = ((jnp.cumsum(first_blk.astype(jnp.int32)) - 1) % 2).astype(jnp.int32)
    after = jnp.sum(jnp.where(blk_e[:, None] == experts[:, 0].T, pend[None, :], 0), axis=1) // MOE_BLOCK
    blk_next = jnp.where(after < n_used[0], blk_e[jnp.minimum(after, n_blocks - 1)], -1).astype(jnp.int32)
    yb = _moe(blk_e, n_used, blk_rows, blk_slot, blk_next, xb, w_gate_up[l], b_gate_up[l][:, None, :], w_down[l], b_down[l][:, None, :])
    yg = _sc_gather(yb, dest.reshape(rows // COMBINE_CHUNK, COMBINE_CHUNK)).reshape(TOP_K, t_all, PACK_W)
    nf = norm_f[None]
    y_p = _final(xm_p, yg, gate_p, nf, tm=FINAL_TILE, first_block=0).reshape(nb, s, D_MODEL)
    y_s = _final(xm_s, yg, gate_s, nf, tm=t_s, first_block=t_p // t_s).reshape(db, ls, D_MODEL)

    st = lambda a: a[None]
    return (y_p, y_s,
            st(kf_p.reshape(nb, s, DA_HEADS, 2, DA_QK)), st(vf_p.reshape(nb, s, DA_HEADS, DA_V)),
            st(c_p), st(n_p), st(mm_p),
            st(kf_s.reshape(db, ls, DA_HEADS, 2, DA_QK)), st(vf_s.reshape(db, ls, DA_HEADS, DA_V)),
            st(c_s.astype(state_C.dtype)), st(n_s.astype(state_n.dtype)), st(mm_s.astype(state_m.dtype)))
```

```python
import functools
import math

import jax
import jax.numpy as jnp
from jax import lax
from jax.experimental import pallas as pl
from jax.experimental.pallas import tpu as pltpu
from jax.experimental.pallas import tpu_sc as plsc

D_MODEL = 1024
CHUNK = 64
DA_HEADS = 4
DA_QK = 64
DA_V = 128
ML_HEADS = 4
ML_DK = 128
ML_DV = 128
HEAD_W = 128
SUBLANES = 8
N_GATES = 2 * ML_HEADS
CAST_ROWS = 128
MLSTM_SAMPLE_ROWS = 4
GROUP_W = 512
ROPE_THETA = 10000.0
N_EXPERTS = 32
TOP_K = 4
D_FF = 1024
SWIGLU_LIMIT = 7.0
SWIGLU_ALPHA = 1.702
EPS = 1e-6
NEG_BIG = -1e30

LOG2E = math.log2(math.e)
VT_ROWS = 144
PROJ_TILE = 512
FINAL_TILE = 1024
ATTN_TQ = 512
MOE_BLOCK = 512
FF_CHUNK = 512
PACK_W = D_MODEL // 2
SC_CORES = 2
SC_SUBCORES = 16
SC_WORKERS = SC_CORES * SC_SUBCORES
DISPATCH_CHUNK = 64
COMBINE_CHUNK = 104
VMEM_BYTES = 64 * 1024 * 1024

_F32 = jnp.float32
_BF16 = jnp.bfloat16


def _nbytes(shape, dtype):
    return math.prod(shape) * jnp.dtype(dtype).itemsize


def _cparams(sem, blocks, resident=0):
    need = 2 * sum(blocks) + resident
    assert need <= VMEM_BYTES, need
    return pltpu.CompilerParams(dimension_semantics=sem, vmem_limit_bytes=need)


def _dot(a, b, dims=(((1,), (0,)), ((), ())), precision=None):
    return lax.dot_general(a, b, dims, precision=precision, preferred_element_type=_F32)


_NT = (((1,), (1,)), ((), ()))
_TN = (((0,), (0,)), ((), ()))


def _pack_rows(x):
    half = x.shape[1] // 2
    bits = lambda v: lax.bitcast_convert_type(v.astype(_BF16).astype(_F32), jnp.int32)
    return bits(x[:, :half]) | lax.shift_right_logical(bits(x[:, half:]), 16)


def _unpack_rows(p):
    hi = lax.bitcast_convert_type(p & jnp.int32(-65536), _F32)
    lo = lax.bitcast_convert_type(lax.shift_left(p, 16), _F32)
    return hi, lo


def _chunk_gate_rows(g8, block):
    L = CHUNK
    pos = lax.broadcasted_iota(jnp.int32, (1, block), 1) % L
    roll = lambda x, sh: pltpu.roll(x, sh % block, 1)
    steps = [1 << b for b in range(L.bit_length() - 1)]

    def scan(x, combine, fill, reverse=False):
        for sh in steps:
            ok = (pos < L - sh) if reverse else (pos >= sh)
            x = combine(x, jnp.where(ok, roll(x, -sh if reverse else sh), fill))
        return x

    to_head_rows = lambda x: pltpu.roll(x, ML_HEADS, 0)
    f8 = to_head_rows(scan(g8, jnp.add, 0.0))
    d8 = g8 - f8
    cm8 = scan(d8, jnp.maximum, -jnp.inf)
    big_g = to_head_rows(scan(g8, jnp.add, 0.0, reverse=True) - g8) + g8
    mg8 = jnp.maximum(scan(big_g, jnp.maximum, -jnp.inf), scan(big_g, jnp.maximum, -jnp.inf, reverse=True))
    return jnp.concatenate([f8, d8, cm8, mg8, jnp.exp(big_g - mg8)], axis=0)


def _proj_kernel(x_ref, g_ref, wt_ref, bg_ref, inv_ref, *refs, names, tm, pos_base, pos_mod):
    out = dict(zip(names, refs))
    cos_sc, sin_sc = refs[len(names):]
    x = x_ref[0]
    xn = (x * lax.rsqrt(jnp.mean(x * x, axis=-1, keepdims=True) + EPS) * g_ref[...]).astype(_BF16)

    row = lax.broadcasted_iota(jnp.int32, (tm, 1), 0)
    if pos_mod % tm == 0:
        tile_pos, row_pos = pos_base + (pl.program_id(1) * tm) % pos_mod, row
    else:
        assert tm % pos_mod == 0
        tile_pos, row_pos = pos_base, row % pos_mod

    @pl.when((pl.program_id(0) == 0) & (pl.program_id(1) == 0))
    def _():
        ang_row = row_pos.astype(_F32) * inv_ref[...]
        cos_sc[...] = jnp.cos(ang_row)
        sin_sc[...] = jnp.sin(ang_row)

    ang_tile = jnp.full((1, 1), tile_pos, jnp.int32).astype(_F32) * inv_ref[...]
    cos_t, sin_t = jnp.cos(ang_tile), jnp.sin(ang_tile)
    cos = cos_t * cos_sc[...] - sin_t * sin_sc[...]
    sin = sin_t * cos_sc[...] + cos_t * sin_sc[...]
    lane = lax.broadcasted_iota(jnp.int32, (1, HEAD_W), 1)
    first = (lane % DA_QK) < (DA_QK // 2)
    sin = jnp.where(first, -sin, sin)

    def rope(z):
        rot = jnp.where(first, pltpu.roll(z, HEAD_W - DA_QK // 2, 1), pltpu.roll(z, DA_QK // 2, 1))
        return z * cos + rot * sin

    def group(c, width=GROUP_W):
        return _dot(xn, wt_ref[c * GROUP_W:c * GROUP_W + width, :], _NT)

    def put_q(zq):
        for h in range(DA_HEADS):
            sl = slice(h * HEAD_W, (h + 1) * HEAD_W)
            out["q"][0, :, sl] = (rope(zq[:, sl]) * (DA_QK ** -0.5 * LOG2E)).astype(_BF16)

    def put_k(zk):
        for h in range(DA_HEADS):
            sl = slice(h * HEAD_W, (h + 1) * HEAD_W)
            rk = rope(zk[:, sl])
            out["kf"][0, :, sl] = rk
            out["kb"][0, :, sl] = rk.astype(_BF16)

    def with_ones_rows(zt):
        ones = jnp.ones((VT_ROWS - DA_V, tm), _BF16)
        return jnp.concatenate(
            [part for h in range(DA_HEADS) for part in (zt[h * DA_V:(h + 1) * DA_V], ones)], axis=0)

    def put_v(zv):
        for h in range(DA_HEADS):
            out["vf"][0, pl.ds(h, tm, stride=DA_HEADS), :] = zv[:, h * DA_V:(h + 1) * DA_V]
        if "vt" in out:
            out["vt"][0, 0] = with_ones_rows(zv.T.astype(_BF16))

    def put_mq(z):
        if "mq" in out:
            out["mq"][0] = z.astype(_BF16)
        if "mqt" in out:
            out["mqt"][0, 0] = z.T.astype(_BF16)

    def put_mk(z):
        out["mk"][0] = (z * (ML_DK ** -0.5)).astype(_BF16)

    def put_mv(z):
        if "mv" in out:
            out["mv"][0] = z.astype(_BF16)
        if "mvt" in out:
            out["mvt"][0, 0] = with_ones_rows(z.T.astype(_BF16))

    def put_mo(z):
        out["og"][0] = jax.nn.sigmoid(z).astype(_BF16)

    def put_gates(z):
        zg = z + bg_ref[...]
        lane_g = lax.broadcasted_iota(jnp.int32, zg.shape, 1)
        logsig = jnp.minimum(zg, 0.0) - jnp.log(1.0 + jnp.exp(-jnp.abs(zg)))
        gates = jnp.where(lane_g < ML_HEADS, zg, logsig)
        if "gc" in out:
            out["gc"][0] = gates[:, :N_GATES]
        g8 = gates.T[:N_GATES, :]
        if "gr" in out:
            out["gr"][0] = g8
        if "gs" in out:
            out["gs"][0] = _chunk_gate_rows(g8, tm)

    sinks = [(put_q, GROUP_W), (put_k, GROUP_W), (put_v, GROUP_W), (put_mq, GROUP_W), (put_mk, GROUP_W),
             (put_mv, GROUP_W), (put_mo, GROUP_W), (put_gates, HEAD_W)]
    pairs = [(c, sinks[c:c + 2]) for c in range(0, len(sinks), 2)]
    pairs = pairs[-1:] + pairs[:-1]
    width = lambda pair: sum(w for _, w in pair)
    nxt = group(pairs[0][0], width(pairs[0][1]))
    for n, (_, pair) in enumerate(pairs):
        cur = nxt
        if n + 1 < len(pairs):
            nxt = group(pairs[n + 1][0], width(pairs[n + 1][1]))
        pair[0][0](cur[:, :GROUP_W])
        pair[1][0](cur[:, GROUP_W:])


def _proj(x3, norm1, wt, b_gates, inv_full, *, names, tm, pos_base, pos_mod):
    nb, s, _ = x3.shape
    nt = s // tm
    tok = lambda w, dt: (jax.ShapeDtypeStruct((nb, s, w), dt), pl.BlockSpec((1, tm, w), lambda b, i: (b, i, 0)))
    slab = lambda rows: (jax.ShapeDtypeStruct((nb, nt, rows, tm), _BF16),
                         pl.BlockSpec((1, 1, rows, tm), lambda b, i: (b, i, 0, 0)))
    const = lambda shape: pl.BlockSpec(shape, lambda b, i: (0,) * len(shape))
    kinds = {
        "q": tok(GROUP_W, _BF16), "kf": tok(GROUP_W, _F32), "kb": tok(GROUP_W, _BF16),
        "vf": (jax.ShapeDtypeStruct((nb, s * DA_HEADS, DA_V), _F32),
               pl.BlockSpec((1, tm * DA_HEADS, DA_V), lambda b, i: (b, i, 0))),
        "vt": slab(DA_HEADS * VT_ROWS),
        "mq": tok(GROUP_W, _BF16), "mk": tok(GROUP_W, _BF16), "mv": tok(GROUP_W, _BF16), "og": tok(GROUP_W, _BF16),
        "gc": tok(N_GATES, _F32),
        "gr": (jax.ShapeDtypeStruct((nb, N_GATES, s), _F32), pl.BlockSpec((1, N_GATES, tm), lambda b, i: (b, 0, i))),
        "gs": (jax.ShapeDtypeStruct((nb, 5 * N_GATES, s), _F32),
               pl.BlockSpec((1, 5 * N_GATES, tm), lambda b, i: (b, 0, i))),
        "mqt": slab(GROUP_W), "mvt": slab(ML_HEADS * VT_ROWS),
    }
    blocks = [_nbytes((tm, D_MODEL), _F32), _nbytes(wt.shape, _BF16)]
    blocks += [_nbytes(kinds[n][1].block_shape, kinds[n][0].dtype) for n in names]
    outs = pl.pallas_call(
        functools.partial(_proj_kernel, names=names, tm=tm, pos_base=pos_base, pos_mod=pos_mod),
        grid=(nb, nt),
        in_specs=[pl.BlockSpec((1, tm, D_MODEL), lambda b, i: (b, i, 0)), const((1, D_MODEL)),
                  const(wt.shape), const((1, HEAD_W)), const((1, HEAD_W))],
        out_specs=[kinds[n][1] for n in names], out_shape=[kinds[n][0] for n in names],
        scratch_shapes=[pltpu.VMEM((tm, HEAD_W), _F32), pltpu.VMEM((tm, HEAD_W), _F32)],
        compiler_params=_cparams(("arbitrary", "arbitrary"), blocks, 8 * _nbytes((tm, GROUP_W), _F32)), name="proj",
    )(x3, norm1, wt, b_gates, inv_full)
    return dict(zip(names, outs))


def _lambda(lam_ref, lam_init):
    lv = lam_ref[...]
    s1 = jnp.sum(lv[0:1] * lv[1:2], axis=-1, keepdims=True)
    s2 = jnp.sum(lv[2:3] * lv[3:4], axis=-1, keepdims=True)
    return jnp.exp(s1) - jnp.exp(s2) + lam_init


def _split_components(q):
    lane = lax.broadcasted_iota(jnp.int32, q.shape, 1)
    zero = jnp.zeros_like(q)
    return jnp.concatenate([jnp.where(lane < DA_QK, q, zero), jnp.where(lane >= DA_QK, q, zero)], axis=0)


def _attn_kernel(lam_ref, q_ref, k_ref, vt_ref, o_ref, acc_ref, s_ref, *, tq, tk, lam_init):
    i = pl.program_id(1)
    heads = range(DA_HEADS)
    qz = [_split_components(q_ref[0, :, h * HEAD_W:(h + 1) * HEAD_W]) for h in heads]
    acc_ref[...] = jnp.zeros_like(acc_ref)

    def scores(h, j):
        k_t = k_ref[0, pl.ds(pl.multiple_of(j * tk, tk), tk), h * HEAD_W:(h + 1) * HEAD_W]
        return _dot(k_t, qz[h], _NT)

    def step(j, ms, last):
        if last:
            kpos = j * tk + lax.broadcasted_iota(jnp.int32, (tk, 1), 0)
            qpos = i * tq + lax.broadcasted_iota(jnp.int32, (1, 2 * tq), 1) % tq
            visible = kpos < (qpos // CHUNK + 1) * CHUNK
        out = []
        s = s_ref[...]
        for h in heads:
            s_next = None
            if h + 1 < DA_HEADS:
                s_next = scores(h + 1, j)
            elif not last:
                s_next = scores(0, j + 1)
            if last:
                s = jnp.where(visible, s, NEG_BIG)
            m_new = jnp.maximum(ms[h], jnp.max(s, axis=0, keepdims=True))
            alpha = jnp.exp2(ms[h] - m_new)
            p = jnp.exp2(s - m_new).astype(_BF16)
            acc_ref[h] = alpha * acc_ref[h] + _dot(vt_ref[0, j, h * VT_ROWS:(h + 1) * VT_ROWS, :], p)
            out.append(m_new)
            s = s_next
        if not last:
            s_ref[...] = s
        return tuple(out)

    n_full = (i * tq) // tk
    s_ref[...] = scores(0, 0)
    init = tuple(jnp.full((1, 2 * tq), NEG_BIG, _F32) for _ in heads)
    ms = lax.fori_loop(0, n_full, lambda j, c: step(j, c, False), init)
    step(n_full, ms, True)

    lam = _lambda(lam_ref, lam_init)
    for h in heads:
        acc = acc_ref[h]
        o = acc[:DA_V] / acc[DA_V:DA_V + 1]
        o_ref[0, :, h * HEAD_W:(h + 1) * HEAD_W] = (o[:, :tq] - lam * o[:, tq:]).T.astype(_BF16)


def _attn_prompt(lamv, q, kb, vt, *, lam_init):
    nb, s, _ = q.shape
    nk, tk = vt.shape[1], vt.shape[3]
    tq = ATTN_TQ
    assert tk % tq == 0
    return pl.pallas_call(
        functools.partial(_attn_kernel, tq=tq, tk=tk, lam_init=lam_init),
        grid=(nb, s // tq),
        in_specs=[pl.BlockSpec((4, DA_QK), lambda b, i: (0, 0)),
                  pl.BlockSpec((1, tq, GROUP_W), lambda b, i: (b, i, 0)),
                  pl.BlockSpec((1, s, GROUP_W), lambda b, i: (b, 0, 0)),
                  pl.BlockSpec((1, nk, DA_HEADS * VT_ROWS, tk), lambda b, i: (b, 0, 0, 0))],
        out_specs=pl.BlockSpec((1, tq, GROUP_W), lambda b, i: (b, i, 0)),
        out_shape=jax.ShapeDtypeStruct((nb, s, GROUP_W), _BF16),
        scratch_shapes=[pltpu.VMEM((DA_HEADS, VT_ROWS, 2 * tq), _F32), pltpu.VMEM((tk, 2 * tq), _F32)],
        compiler_params=_cparams(
            ("parallel", "arbitrary"),
            [_nbytes((tq, GROUP_W), _BF16), _nbytes((s, GROUP_W), _BF16), _nbytes(vt.shape[1:], _BF16),
             _nbytes((tq, GROUP_W), _BF16)],
            _nbytes((DA_HEADS, VT_ROWS, 2 * tq), _F32) + 5 * _nbytes((tk, 2 * tq), _F32)), name="attn",
    )(lamv, q, kb, vt)


def _attn_sample_kernel(lam_ref, q_ref, kn_ref, vn_ref, kc_ref, vc_ref, o_ref, *, lq, past, lam_init):
    lam = _lambda(lam_ref, lam_init)
    for h in range(DA_HEADS):
        sl = slice(h * HEAD_W, (h + 1) * HEAD_W)
        qz = _split_components(q_ref[0, :, sl])
        kct = kc_ref[0, sl, :].astype(_BF16)
        vc = vc_ref[0, pl.ds(h, past, stride=DA_HEADS), :].astype(_BF16)
        s_c = _dot(qz, kct)
        s_n = _dot(qz, kn_ref[0, :, sl], _NT)
        m = jnp.maximum(jnp.max(s_c, axis=-1, keepdims=True), jnp.max(s_n, axis=-1, keepdims=True))
        p_c = jnp.exp2(s_c - m)
        p_n = jnp.exp2(s_n - m)
        l = jnp.sum(p_c, axis=-1, keepdims=True) + jnp.sum(p_n, axis=-1, keepdims=True)
        o = (_dot(p_c.astype(_BF16), vc) + _dot(p_n.astype(_BF16), vn_ref[0, :, sl])) / l
        o_ref[0, :, sl] = (o[:lq] - lam * o[lq:]).astype(_BF16)


def _attn_sample(lamv, q, kn, vn, kct, vc, *, lam_init):
    nb, lq, _ = q.shape
    past = kct.shape[2]
    new = pl.BlockSpec((1, lq, GROUP_W), lambda b: (b, 0, 0))
    return pl.pallas_call(
        functools.partial(_attn_sample_kernel, lq=lq, past=past, lam_init=lam_init),
        grid=(nb,),
        in_specs=[pl.BlockSpec((4, DA_QK), lambda b: (0, 0)), new, new, new,
                  pl.BlockSpec((1, GROUP_W, past), lambda b: (b, 0, 0)),
                  pl.BlockSpec((1, past * DA_HEADS, DA_V), lambda b: (b, 0, 0))],
        out_specs=new, out_shape=jax.ShapeDtypeStruct((nb, lq, GROUP_W), _BF16),
        compiler_params=_cparams(
            ("parallel",), [_nbytes((GROUP_W, past), _F32), _nbytes((past * DA_HEADS, DA_V), _F32)]
            + 4 * [_nbytes((lq, GROUP_W), _F32)], 8 * _nbytes((2 * lq, past), _F32)), name="attn_sample",
    )(lamv, q, kn, vn, kct, vc)


def _mlstm_kernel(q_ref, k_ref, v_ref, gc_ref, gr_ref, c0_ref, n0_ref, m0_ref, h_ref, c_ref, n_ref, m_ref,
                  c_sc, n_sc, m_sc, *, chunk, nchunks, bb):
    j = pl.program_id(1)

    @pl.when(j == 0)
    def _():
        c_sc[...] = c0_ref[...]
        n_sc[...] = n0_ref[...]
        m_sc[...] = m0_ref[...]

    L = chunk
    r_i = lax.broadcasted_iota(jnp.int32, (L, L), 0)
    c_i = lax.broadcasted_iota(jnp.int32, (L, L), 1)
    causal = r_i >= c_i
    tril = causal.astype(_F32)
    triu = (r_i <= c_i).astype(_F32)
    hi = lax.Precision.HIGHEST

    for c, bi in [(c, bi) for c in range(nchunks) for bi in range(bb)]:
        rows = slice(c * L, (c + 1) * L)
        gcol = gc_ref[bi, rows, :]
        grow = gr_ref[bi, :, rows]
        fc_all = _dot(tril, gcol, precision=hi)
        fr_all = _dot(grow, triu, precision=hi)
        for hh in range(ML_HEADS):
            sl = slice(hh * HEAD_W, (hh + 1) * HEAD_W)
            m = m_sc[bi, hh][0:1, 0:1]
            fc = fc_all[:, ML_HEADS + hh:ML_HEADS + hh + 1]
            fr = fr_all[ML_HEADS + hh:ML_HEADS + hh + 1, :]
            igc = gcol[:, hh:hh + 1]
            igr = grow[hh:hh + 1, :]
            a = jnp.where(causal, fc - fr + igr, -jnp.inf)
            b = fc + m
            m_t = jnp.maximum(b, jnp.max(a, axis=-1, keepdims=True))
            w = jnp.exp(a - m_t)
            sc = jnp.exp(b - m_t)
            q = q_ref[bi, rows, sl]
            k = k_ref[bi, rows, sl]
            v = v_ref[bi, rows, sl]
            wqk = w * _dot(q, k, _NT)
            cmat = c_sc[bi, hh]
            nrow = n_sc[bi, hh:hh + 1, :]
            qn = jnp.sum(q.astype(_F32) * nrow, axis=-1, keepdims=True)
            num = _dot(wqk.astype(_BF16), v) + sc * _dot(q, cmat.astype(_BF16))
            den = jnp.sum(wqk, axis=-1, keepdims=True) + sc * qn
            h_ref[bi, rows, sl] = (num / jnp.maximum(jnp.abs(den), jnp.exp(-m_t))).astype(_BF16)

            fl = fc[L - 1:L, :]
            g = fl - fc + igc
            m_new = jnp.maximum(fl + m, jnp.max(g, axis=0, keepdims=True))
            decay = jnp.exp(fl + m - m_new)
            kw = k.astype(_F32) * jnp.exp(g - m_new)
            c_sc[bi, hh] = decay * cmat + _dot(kw.astype(_BF16), v, _TN)
            n_sc[bi, hh:hh + 1, :] = decay * nrow + jnp.sum(kw, axis=0, keepdims=True)
            m_sc[bi, hh] = jnp.broadcast_to(m_new, m_sc.shape[2:])

    @pl.when(j == pl.num_programs(1) - 1)
    def _():
        c_ref[...] = c_sc[...]
        n_ref[...] = n_sc[...]
        m_ref[...] = m_sc[...]


def _mlstm(q, k, v, gc, gr, c0, n0, m0, *, chunk, block, bb):
    nb, s, _ = q.shape
    assert nb % bb == 0
    tok = lambda w: pl.BlockSpec((bb, block, w), lambda b, j: (b, j, 0))
    cspec = pl.BlockSpec((bb, ML_HEADS, ML_DK, ML_DV), lambda b, j: (b, 0, 0, 0))
    nspec = pl.BlockSpec((bb, ML_HEADS, ML_DK), lambda b, j: (b, 0, 0))
    mspec = pl.BlockSpec((bb, ML_HEADS, SUBLANES, HEAD_W), lambda b, j: (b, 0, 0, 0))
    return pl.pallas_call(
        functools.partial(_mlstm_kernel, chunk=chunk, nchunks=block // chunk, bb=bb),
        grid=(nb // bb, s // block),
        in_specs=[tok(GROUP_W), tok(GROUP_W), tok(GROUP_W), tok(N_GATES),
                  pl.BlockSpec((bb, N_GATES, block), lambda b, j: (b, 0, j)), cspec, nspec, mspec],
        out_specs=[tok(GROUP_W), cspec, nspec, mspec],
        out_shape=[jax.ShapeDtypeStruct((nb, s, GROUP_W), _BF16), jax.ShapeDtypeStruct(c0.shape, _F32),
                   jax.ShapeDtypeStruct(n0.shape, _F32), jax.ShapeDtypeStruct(m0.shape, _F32)],
        scratch_shapes=[pltpu.VMEM((bb, ML_HEADS, ML_DK, ML_DV), _F32), pltpu.VMEM((bb, ML_HEADS, ML_DK), _F32),
                        pltpu.VMEM((bb, ML_HEADS, SUBLANES, HEAD_W), _F32)],
        compiler_params=_cparams(
            ("parallel", "arbitrary"), 5 * [_nbytes((bb, block, GROUP_W), _F32)] + 2 * [bb * _nbytes(c0.shape[1:], _F32)],
            2 * bb * _nbytes(c0.shape[1:], _F32)), name="mlstm",
    )(q, k, v, gc, gr, c0, n0, m0)


def _mlstm_pairs_kernel(k_ref, qt_ref, vt_ref, gs_ref, st0_ref, m0_ref, h_ref, st_ref, m_ref, st_sc, m_sc,
                        *, block):
    j = pl.program_id(1)

    @pl.when(j == 0)
    def _():
        st_sc[...] = st0_ref[0]
        m_sc[...] = m0_ref[0]

    L, W = CHUNK, 2 * CHUNK
    rows8 = lambda n: gs_ref[0, n * N_GATES:(n + 1) * N_GATES, :]
    f8, d8, cm8, mg8, ws8 = (rows8(n) for n in range(5))
    d_cols = d8.T

    s_i = lax.broadcasted_iota(jnp.int32, (W, W), 0)
    t_i = lax.broadcasted_iota(jnp.int32, (W, W), 1)
    allowed = (s_i // L == t_i // L) & (s_i <= t_i)
    lane_w = lax.broadcasted_iota(jnp.int32, (1, W), 1)
    first = lane_w < L

    sts = [st_sc[hh] for hh in range(ML_HEADS)]
    ms = [m_sc[hh][0:1, 0:1] for hh in range(ML_HEADS)]

    def independent(p, hh):
        lanes = slice(p * W, (p + 1) * W)
        feat = slice(hh * HEAD_W, (hh + 1) * HEAD_W)
        kp = k_ref[0, lanes, feat]
        qt = qt_ref[0, 0, feat, lanes]
        v1t = vt_ref[0, 0, hh * VT_ROWS:(hh + 1) * VT_ROWS, lanes]
        ws = ws8[hh:hh + 1, lanes]
        v1f = v1t.astype(_F32)
        return dict(
            p=p, hh=hh, qt=qt, v1t=v1t, qk_t=_dot(kp, qt),
            upd_a=_dot((v1f * jnp.where(first, ws, 0.0)).astype(_BF16), kp),
            upd_b=_dot((v1f * jnp.where(first, 0.0, ws)).astype(_BF16), kp))

    def dependent(u):
        p, hh, qt = u["p"], u["hh"], u["qt"]
        lanes = slice(p * W, (p + 1) * W)
        row = lambda x: x[hh:hh + 1, lanes]
        cm, f_row = row(cm8), row(f8)
        w_t = jnp.exp(jnp.where(allowed, d_cols[lanes, hh:hh + 1] - cm, -jnp.inf))
        intra = _dot(u["v1t"], (w_t * u["qk_t"]).astype(_BF16))
        st_a, m_a = sts[hh], ms[hh]
        fl_a, mg_a = f_row[:, L - 1:L], row(mg8)[:, 0:1]
        fl_b, mg_b = f_row[:, W - 1:W], row(mg8)[:, L:L + 1]
        m_b = jnp.maximum(fl_a + m_a, mg_a)
        st_b = jnp.exp(fl_a + m_a - m_b) * st_a + jnp.exp(mg_a - m_b) * u["upd_a"]
        m_c = jnp.maximum(fl_b + m_b, mg_b)
        sts[hh] = jnp.exp(fl_b + m_b - m_c) * st_b + jnp.exp(mg_b - m_c) * u["upd_b"]
        ms[hh] = m_c
        inter = jnp.where(first, _dot(st_a.astype(_BF16), qt), _dot(st_b.astype(_BF16), qt))
        m_prev = jnp.where(first, m_a, m_b)
        mt = jnp.maximum(m_prev, cm)
        tot = jnp.exp(cm - mt) * intra + jnp.exp(m_prev - mt) * inter
        den = jnp.maximum(jnp.abs(tot[ML_DV:ML_DV + 1]), jnp.exp(-(f_row + mt)))
        h_ref[0, lanes, hh * HEAD_W:(hh + 1) * HEAD_W] = (tot[:ML_DV] / den).T.astype(_BF16)

    units = [(p, hh) for p in range(block // W) for hh in range(ML_HEADS)]
    nxt = independent(*units[0])
    for n in range(len(units)):
        cur = nxt
        if n + 1 < len(units):
            nxt = independent(*units[n + 1])
        dependent(cur)
    for hh in range(ML_HEADS):
        st_sc[hh] = sts[hh]
        m_sc[hh] = jnp.broadcast_to(ms[hh], m_sc.shape[1:])

    @pl.when(j == pl.num_programs(1) - 1)
    def _():
        st_ref[0] = st_sc[...]
        m_ref[0] = m_sc[...]


def _mlstm_pairs(k, qt, vt, gs, st0, m0):
    nb, s, _ = k.shape
    block = qt.shape[3]
    nt = s // block
    state = pl.BlockSpec((1, ML_HEADS, VT_ROWS, ML_DK), lambda b, j: (b, 0, 0, 0))
    mspec = pl.BlockSpec((1, ML_HEADS, SUBLANES, HEAD_W), lambda b, j: (b, 0, 0, 0))
    tok = pl.BlockSpec((1, block, GROUP_W), lambda b, j: (b, j, 0))
    return pl.pallas_call(
        functools.partial(_mlstm_pairs_kernel, block=block),
        grid=(nb, nt),
        in_specs=[tok, pl.BlockSpec((1, 1, GROUP_W, block), lambda b, j: (b, j, 0, 0)),
                  pl.BlockSpec((1, 1, ML_HEADS * VT_ROWS, block), lambda b, j: (b, j, 0, 0)),
                  pl.BlockSpec((1, 5 * N_GATES, block), lambda b, j: (b, 0, j)), state, mspec],
        out_specs=[tok, state, mspec],
        out_shape=[jax.ShapeDtypeStruct((nb, s, GROUP_W), _BF16), jax.ShapeDtypeStruct(st0.shape, _F32),
                   jax.ShapeDtypeStruct(m0.shape, _F32)],
        scratch_shapes=[pltpu.VMEM((ML_HEADS, VT_ROWS, ML_DK), _F32), pltpu.VMEM((ML_HEADS, SUBLANES, HEAD_W), _F32)],
        compiler_params=_cparams(
            ("parallel", "arbitrary"),
            3 * [_nbytes((block, GROUP_W), _BF16)] + [_nbytes((ML_HEADS * VT_ROWS, block), _BF16)]
            + 2 * [_nbytes(st0.shape[1:], _F32)], 16 * _nbytes((VT_ROWS, 2 * CHUNK), _F32) * ML_HEADS), name="mlstm_pairs",
    )(k, qt, vt, gs, st0, m0)


def _mix_kernel(da_ref, ml_ref, og_ref, x_ref, subln_ref, wo_ref, g2_ref, wr_ref, br_ref, cnt0_ref,
                xmid_ref, xp_ref, idx_ref, gate_ref, rank_ref, cnt_ref, cnt_sc, upper_sc, *, tm, lam_init):
    step = pl.program_id(0)

    @pl.when(step == 0)
    def _():
        cnt_sc[...] = cnt0_ref[...]
        earlier = lax.broadcasted_iota(jnp.int32, (tm, tm), 0) < lax.broadcasted_iota(jnp.int32, (tm, tm), 1)
        upper_sc[...] = earlier.astype(_BF16)

    parts = []
    for h in range(DA_HEADS):
        d = da_ref[:, h * HEAD_W:(h + 1) * HEAD_W].astype(_F32)
        d = d * lax.rsqrt(jnp.mean(d * d, axis=-1, keepdims=True) + EPS) * subln_ref[...]
        parts.append((d * (1.0 - lam_init)).astype(_BF16))
    parts.append((og_ref[...].astype(_F32) * ml_ref[...].astype(_F32)).astype(_BF16))
    xm = x_ref[...] + _dot(jnp.concatenate(parts, axis=1), wo_ref[...])
    xmid_ref[...] = xm
    xn_f = xm * lax.rsqrt(jnp.mean(xm * xm, axis=-1, keepdims=True) + EPS) * g2_ref[...]
    xp_ref[...] = _pack_rows(xn_f)
    xn = xn_f.astype(_BF16)

    logits = _dot(wr_ref[...], xn, _NT) + br_ref[...]
    e_i = lax.broadcasted_iota(jnp.int32, logits.shape, 0)
    member = jnp.zeros(logits.shape, jnp.bool_)
    vals, idxs = [], []
    for _ in range(TOP_K):
        mx = jnp.max(logits, axis=0, keepdims=True)
        sel = jnp.min(jnp.where(logits == mx, e_i, N_EXPERTS), axis=0, keepdims=True)
        hit = e_i == sel
        member = member | hit
        logits = jnp.where(hit, -jnp.inf, logits)
        vals.append(mx)
        idxs.append(sel)
    ex = [jnp.exp(v - vals[0]) for v in vals]
    tot = ex[0] + ex[1] + ex[2] + ex[3]
    idx_ref[...] = jnp.concatenate(idxs, axis=0)
    gate_ref[...] = jnp.concatenate([e / tot for e in ex], axis=0)

    memf = member.astype(_F32)
    before = _dot(memf.astype(_BF16), upper_sc[...]) + cnt_sc[...]
    rank_ref[...] = jnp.concatenate(
        [jnp.sum(jnp.where(e_i == s, before, 0.0), axis=0, keepdims=True) for s in idxs], axis=0).astype(jnp.int32)
    cnt_sc[...] += jnp.sum(memf, axis=1, keepdims=True)
    cnt_ref[...] = cnt_sc[...]


def _mix(da, ml, og, x, subln, w_out, norm2, w_rt, b_r, cnt0, *, tm, lam_init):
    t = x.shape[0]
    tok = lambda w: pl.BlockSpec((tm, w), lambda i: (i, 0))
    const = lambda shape: pl.BlockSpec(shape, lambda i: (0,) * len(shape))
    lane_tok = pl.BlockSpec((TOP_K, tm), lambda i: (0, i))
    return pl.pallas_call(
        functools.partial(_mix_kernel, tm=tm, lam_init=lam_init),
        grid=(t // tm,),
        in_specs=[tok(GROUP_W), tok(GROUP_W), tok(GROUP_W), tok(D_MODEL), const((1, DA_V)),
                  const((D_MODEL, D_MODEL)), const((1, D_MODEL)), const((N_EXPERTS, D_MODEL)),
                  const((N_EXPERTS, 1)), const((N_EXPERTS, 1))],
        out_specs=[tok(D_MODEL), tok(PACK_W), lane_tok, lane_tok, lane_tok, const((N_EXPERTS, 1))],
        out_shape=[jax.ShapeDtypeStruct((t, D_MODEL), _F32), jax.ShapeDtypeStruct((t, PACK_W), jnp.int32),
                   jax.ShapeDtypeStruct((TOP_K, t), jnp.int32), jax.ShapeDtypeStruct((TOP_K, t), _F32),
                   jax.ShapeDtypeStruct((TOP_K, t), jnp.int32), jax.ShapeDtypeStruct((N_EXPERTS, 1), _F32)],
        scratch_shapes=[pltpu.VMEM((N_EXPERTS, 1), _F32), pltpu.VMEM((tm, tm), _BF16)],
        compiler_params=_cparams(
            ("arbitrary",),
            3 * [_nbytes((tm, GROUP_W), _BF16)] + 2 * [_nbytes((tm, D_MODEL), _F32)] + [_nbytes((tm, PACK_W), _F32)]
            + [_nbytes((D_MODEL, D_MODEL), _BF16)], 4 * _nbytes((tm, D_MODEL), _F32) + 2 * _nbytes((tm, tm), _F32)), name="mix",
    )(da, ml, og, x, subln, w_out, norm2, w_rt, b_r, cnt0)


def _moe_kernel(be_ref, nu_ref, nv_ref, slot_ref, nxt_ref, x_ref, wgu_hbm, bgu_ref, wd_hbm, bd_ref, y_ref,
                wgu_f32, wd_f32, wgu_sc, wd_sc, sems):
    j = pl.program_id(0)
    used = j < nu_ref[0]
    fresh = (j == 0) | (be_ref[j] != be_ref[jnp.maximum(j - 1, 0)])
    slot = slot_ref[j]

    def fetch(e, s):
        return (pltpu.make_async_copy(wgu_hbm.at[e], wgu_f32.at[s], sems.at[s, 0]),
                pltpu.make_async_copy(wd_hbm.at[e], wd_f32.at[s], sems.at[s, 1]))

    @pl.when(j == 0)
    def _():
        for cp in fetch(be_ref[0], slot):
            cp.start()

    @pl.when(fresh & used)
    def _():
        for cp in fetch(be_ref[j], slot):
            cp.wait()

        @pl.when(nxt_ref[j] >= 0)
        def _():
            for cp in fetch(nxt_ref[j], 1 - slot):
                cp.start()

        rows = CAST_ROWS
        def cast_gu(r, _):
            sl = pl.ds(pl.multiple_of(r * rows, rows), rows)
            wgu_sc[sl, :] = wgu_f32[slot, sl, :].astype(_BF16)
            return 0
        lax.fori_loop(0, D_MODEL // rows, cast_gu, 0)
        def cast_d(r, _):
            sl = pl.ds(pl.multiple_of(r * rows, rows), rows)
            wd_sc[sl, :] = wd_f32[slot, sl, :].astype(_BF16)
            return 0
        lax.fori_loop(0, D_FF // rows, cast_d, 0)

    @pl.when(j >= nu_ref[0])
    def _():
        y_ref[...] = jnp.zeros_like(y_ref)

    def ffn(row0):
        hi, lo = _unpack_rows(x_ref[row0:, :])
        x = jnp.concatenate([hi.astype(_BF16), lo.astype(_BF16)], axis=1)
        n_chunks = D_FF // FF_CHUNK

        def gate_up(c):
            lo = c * FF_CHUNK
            return (_dot(x, wgu_sc[:, lo:lo + FF_CHUNK]), _dot(x, wgu_sc[:, D_FF + lo:D_FF + lo + FF_CHUNK]))

        acc = None
        nxt = gate_up(0)
        for c in range(n_chunks):
            lo = c * FF_CHUNK
            gate, up = nxt
            if c + 1 < n_chunks:
                nxt = gate_up(c + 1)
            gate = jnp.minimum(gate + bgu_ref[0, :, lo:lo + FF_CHUNK], SWIGLU_LIMIT)
            up = jnp.clip(up + bgu_ref[0, :, D_FF + lo:D_FF + lo + FF_CHUNK], -SWIGLU_LIMIT, SWIGLU_LIMIT)
            act = (up + 1.0) * (gate * jax.nn.sigmoid(SWIGLU_ALPHA * gate))
            down = _dot(act.astype(_BF16), wd_sc[lo:lo + FF_CHUNK, :])
            acc = down if acc is None else acc + down
        y_ref[row0:, :] = _pack_rows(acc + bd_ref[0])

    half = MOE_BLOCK // 2
    @pl.when(used & (nv_ref[j] > half))
    def _():
        ffn(0)

    @pl.when(used & (nv_ref[j] <= half))
    def _():
        y_ref[:half, :] = jnp.zeros((half, PACK_W), jnp.int32)
        ffn(half)


def _moe(blk_e, n_used, blk_rows, blk_slot, blk_next, xb, w_gu, b_gu, w_d, b_d):
    n_blocks = xb.shape[0] // MOE_BLOCK
    last = lambda j, nu: jnp.minimum(j, nu[0] - 1)
    row_spec = pl.BlockSpec((MOE_BLOCK, PACK_W), lambda j, be, nu, nv, sl, nx: (last(j, nu), 0))
    bias_spec = lambda c: pl.BlockSpec((1, 1, c), lambda j, be, nu, nv, sl, nx: (be[last(j, nu)], 0, 0))
    hbm = pl.BlockSpec(memory_space=pl.ANY)
    grid_spec = pltpu.PrefetchScalarGridSpec(
        num_scalar_prefetch=5, grid=(n_blocks,),
        in_specs=[row_spec, hbm, bias_spec(2 * D_FF), hbm, bias_spec(D_MODEL)],
        out_specs=pl.BlockSpec((MOE_BLOCK, PACK_W), lambda j, be, nu, nv, sl, nx: (j, 0)),
        scratch_shapes=[pltpu.VMEM((2, D_MODEL, 2 * D_FF), _F32), pltpu.VMEM((2, D_FF, D_MODEL), _F32),
                        pltpu.VMEM((D_MODEL, 2 * D_FF), _BF16), pltpu.VMEM((D_FF, D_MODEL), _BF16),
                        pltpu.SemaphoreType.DMA((2, 2))])
    return pl.pallas_call(
        _moe_kernel, grid_spec=grid_spec, out_shape=jax.ShapeDtypeStruct(xb.shape, jnp.int32),
        compiler_params=_cparams(
            ("arbitrary",), 2 * [_nbytes((MOE_BLOCK, PACK_W), _F32)],
            2 * _nbytes((D_MODEL, 3 * D_FF), _F32) + _nbytes((D_MODEL, 3 * D_FF), _BF16)
            + 4 * _nbytes((MOE_BLOCK, D_MODEL), _F32)), name="moe",
    )(blk_e, n_used, blk_rows, blk_slot, blk_next, xb, w_gu, b_gu, w_d, b_d)


def _sc_mesh():
    return plsc.VectorSubcoreMesh(core_axis_name="c", subcore_axis_name="s")


def _sc_worker():
    return lax.axis_index("s") * SC_CORES + lax.axis_index("c")


def _sc_dispatch(sources, n_rows):
    n_src = len(sources)
    shapes = []
    for _, dest3 in sources:
        n_chunks, _, chunk = dest3.shape
        per_worker = n_chunks // SC_WORKERS
        assert per_worker * SC_WORKERS == n_chunks and chunk % SUBLANES == 0
        shapes.append((per_worker, chunk))

    def body(*refs):
        x_hbms, d_hbms, o_hbm = refs[:n_src], refs[n_src:2 * n_src], refs[2 * n_src]
        scratch = refs[2 * n_src + 1:]
        load_a, load_b, scat_a, scat_b = scratch[3 * n_src:]
        worker = _sc_worker()
        for i, (per_worker, chunk) in enumerate(shapes):
            x_hbm, idx_v, rows_a, rows_b = x_hbms[i], scratch[3 * i], scratch[3 * i + 1], scratch[3 * i + 2]
            pltpu.sync_copy(d_hbms[i].at[worker], idx_v)

            def load(ci, rows, sem):
                row0 = (worker * per_worker + ci) * chunk
                return pltpu.make_async_copy(x_hbm.at[pl.ds(pl.multiple_of(row0, SUBLANES), chunk)], rows, sem)

            def scatters(ci, rows, sem):
                return [pltpu.make_async_copy(rows, o_hbm.at[idx_v.at[ci * TOP_K + k]], sem) for k in range(TOP_K)]

            def start(copies):
                for cp in copies:
                    cp.start()

            def wait(copies):
                for cp in copies:
                    cp.wait()

            load(0, rows_a, load_a).start()

            @pl.loop(0, per_worker // 2)
            def _(p):
                ca, cb = 2 * p, 2 * p + 1
                load(ca, rows_a, load_a).wait()
                load(cb, rows_b, load_b).start()
                start(scatters(ca, rows_a, scat_a))
                load(cb, rows_b, load_b).wait()
                wait(scatters(ca, rows_a, scat_a))
                load(jnp.minimum(ca + 2, per_worker - 1), rows_a, load_a).start()
                start(scatters(cb, rows_b, scat_b))
                wait(scatters(cb, rows_b, scat_b))

            last = per_worker - 1
            load(last, rows_a, load_a).wait()
            if per_worker % 2:
                start(scatters(last, rows_a, scat_a))
                wait(scatters(last, rows_a, scat_a))

    dma = pltpu.SemaphoreType.DMA
    scratch_types = []
    for per_worker, chunk in shapes:
        scratch_types += [pltpu.VMEM((per_worker * TOP_K, chunk), jnp.int32), pltpu.VMEM((chunk, PACK_W), jnp.int32),
                          pltpu.VMEM((chunk, PACK_W), jnp.int32)]
    tables = [dest3.reshape(SC_WORKERS, pw * TOP_K, chunk) for (_, dest3), (pw, chunk) in zip(sources, shapes)]
    return pl.kernel(
        body, out_type=jax.ShapeDtypeStruct((n_rows, PACK_W), jnp.int32), mesh=_sc_mesh(),
        scratch_types=scratch_types + [dma, dma, dma, dma],
        name="sc_dispatch")(*[xp for xp, _ in sources], *tables)


def _sc_gather(table, idx2):
    n_chunks, chunk = idx2.shape
    per_worker = n_chunks // SC_WORKERS
    assert per_worker * SC_WORKERS == n_chunks and chunk % SUBLANES == 0

    def body(t_hbm, i_hbm, o_hbm, idx_v, rows_a, rows_b, gath_a, gath_b, put_a, put_b):
        worker = _sc_worker()
        pltpu.sync_copy(i_hbm.at[worker], idx_v)

        def gather(ci, rows, sem):
            return pltpu.make_async_copy(t_hbm.at[idx_v.at[ci]], rows, sem)

        def put(ci, rows, sem):
            row0 = (worker * per_worker + ci) * chunk
            return pltpu.make_async_copy(rows, o_hbm.at[pl.ds(pl.multiple_of(row0, SUBLANES), chunk)], sem)

        gather(0, rows_a, gath_a).start()

        @pl.loop(0, per_worker // 2)
        def _(p):
            ca, cb = 2 * p, 2 * p + 1
            gather(ca, rows_a, gath_a).wait()
            gather(cb, rows_b, gath_b).start()
            put(ca, rows_a, put_a).start()
            gather(cb, rows_b, gath_b).wait()
            put(ca, rows_a, put_a).wait()
            gather(jnp.minimum(ca + 2, per_worker - 1), rows_a, gath_a).start()
            put(cb, rows_b, put_b).start()
            put(cb, rows_b, put_b).wait()

        last = per_worker - 1
        gather(last, rows_a, gath_a).wait()
        if per_worker % 2:
            put(last, rows_a, put_a).start()
            put(last, rows_a, put_a).wait()

    dma = pltpu.SemaphoreType.DMA
    return pl.kernel(
        body, out_type=jax.ShapeDtypeStruct((n_chunks * chunk, PACK_W), jnp.int32), mesh=_sc_mesh(),
        scratch_types=[pltpu.VMEM((per_worker, chunk), jnp.int32), pltpu.VMEM((chunk, PACK_W), jnp.int32),
                       pltpu.VMEM((chunk, PACK_W), jnp.int32), dma, dma, dma, dma],
        name="sc_gather")(table, idx2.reshape(SC_WORKERS, per_worker, chunk))


def _final_kernel(x_ref, y_ref, g_ref, nf_ref, o_ref):
    g_rows = g_ref[...]
    g = jnp.concatenate([g_rows, jnp.zeros((SUBLANES - TOP_K, g_rows.shape[1]), _F32)], axis=0).T
    hi = jnp.zeros((x_ref.shape[0], PACK_W), _F32)
    lo = jnp.zeros((x_ref.shape[0], PACK_W), _F32)
    for k in range(TOP_K):
        y_hi, y_lo = _unpack_rows(y_ref[k])
        hi = hi + g[:, k:k + 1] * y_hi
        lo = lo + g[:, k:k + 1] * y_lo
    x = x_ref[...] + jnp.concatenate([hi, lo], axis=1)
    o_ref[...] = x * lax.rsqrt(jnp.mean(x * x, axis=-1, keepdims=True) + EPS) * nf_ref[...]


def _final(xmid, yg, gates, norm_f, *, tm, first_block):
    t = xmid.shape[0]
    return pl.pallas_call(
        _final_kernel, grid=(t // tm,),
        in_specs=[pl.BlockSpec((tm, D_MODEL), lambda i: (i, 0)),
                  pl.BlockSpec((TOP_K, tm, PACK_W), lambda i: (0, i + first_block, 0)),
                  pl.BlockSpec((TOP_K, tm), lambda i: (0, i)),
                  pl.BlockSpec((1, D_MODEL), lambda i: (0, 0))],
        out_specs=pl.BlockSpec((tm, D_MODEL), lambda i: (i, 0)),
        out_shape=jax.ShapeDtypeStruct((t, D_MODEL), _F32),
        compiler_params=_cparams(
            ("parallel",), 2 * [_nbytes((tm, D_MODEL), _F32)] + [_nbytes((TOP_K, tm, PACK_W), _F32)],
            2 * _nbytes((tm, D_MODEL), _F32)), name="final",
    )(xmid, yg, gates, norm_f)


def kernel(x_prompt, x_sample, cache_k, cache_v, state_C, state_n, state_m, norm1, w_in, b_igate, b_fgate,
           lambda_q1, lambda_k1, lambda_q2, lambda_k2, subln, w_out, norm2, w_router, b_router, w_gate_up,
           b_gate_up, w_down, b_down, norm_f):
    nb, s, _ = x_prompt.shape
    db, ls, _ = x_sample.shape
    past = cache_k.shape[2]
    depth = w_in.shape[0]
    assert depth == 1 and s % PROJ_TILE == 0 and (db * ls) % SUBLANES == 0
    l = 0
    lam_init = 0.8 - 0.6 * math.exp(-0.3 * l)
    t_p, t_s = nb * s, db * ls

    w_nat = jnp.pad(w_in[l].T, ((0, HEAD_W - N_GATES), (0, 0))).astype(_BF16)
    b_gates = jnp.pad(jnp.concatenate([b_igate[l], b_fgate[l]]).astype(_F32), (0, HEAD_W - N_GATES))[None]
    inv = ROPE_THETA ** (-jnp.arange(0, DA_QK, 2, dtype=_F32) / DA_QK)
    inv_full = jnp.tile(inv, HEAD_W // (DA_QK // 2))[None]
    lamv = jnp.stack([lambda_q1[l], lambda_k1[l], lambda_q2[l], lambda_k2[l]]).astype(_F32)
    g1 = norm1[l][None]

    pp = _proj(x_prompt, g1, w_nat, b_gates, inv_full, tm=PROJ_TILE, pos_base=0, pos_mod=s,
               names=("q", "kf", "kb", "vf", "vt", "mk", "og", "gs", "mqt", "mvt"))
    ps = _proj(x_sample.reshape(1, t_s, D_MODEL), g1, w_nat, b_gates, inv_full, tm=t_s, pos_base=past, pos_mod=ls,
               names=("q", "kf", "kb", "vf", "mq", "mk", "mv", "og", "gc", "gr"))
    kf_p, vf_p, kf_s, vf_s = pp["kf"], pp["vf"], ps["kf"], ps["vf"]

    da_p = _attn_prompt(lamv, pp["q"], pp["kb"], pp["vt"], lam_init=lam_init)
    per_b = lambda a: a.reshape(db, ls, a.shape[-1])
    kct = jnp.transpose(cache_k[l], (0, 2, 3, 4, 1)).reshape(db, GROUP_W, past)
    vc = cache_v[l].reshape(db, past * DA_HEADS, DA_V)
    vn = vf_s.reshape(db, ls, GROUP_W).astype(_BF16)
    da_s = _attn_sample(lamv, per_b(ps["q"]), per_b(ps["kb"]), vn, kct, vc, lam_init=lam_init)

    assert PROJ_TILE % (2 * CHUNK) == 0
    h_p, st_p, m_p = _mlstm_pairs(pp["mk"], pp["mqt"], pp["mvt"], pp["gs"],
                                  jnp.zeros((nb, ML_HEADS, VT_ROWS, ML_DK), _F32),
                                  jnp.zeros((nb, ML_HEADS, SUBLANES, HEAD_W), _F32))
    gr_sb = ps["gr"].reshape(8, db, ls).transpose(1, 0, 2)
    m0_s = jnp.broadcast_to(state_m[l].astype(_F32)[:, :, None, None], (db, ML_HEADS, SUBLANES, HEAD_W))
    h_s, c_s, n_s, m_s = _mlstm(per_b(ps["mq"]), per_b(ps["mk"]), per_b(ps["mv"]), per_b(ps["gc"]), gr_sb,
                                state_C[l].astype(_F32), state_n[l].astype(_F32), m0_s, chunk=ls, block=ls,
                                bb=MLSTM_SAMPLE_ROWS)
    c_p, n_p, mm_p = jnp.swapaxes(st_p[:, :, :ML_DV, :], -1, -2), st_p[:, :, ML_DV, :], m_p[:, :, 0, 0]
    mm_s = m_s[:, :, 0, 0]

    wo = w_out[l].astype(_BF16)
    w_rt = w_router[l].T.astype(_BF16)
    b_r = b_router[l].astype(_F32)[:, None]
    mix = functools.partial(_mix, subln=subln[l][None], w_out=wo, norm2=norm2[l][None], w_rt=w_rt, b_r=b_r,
                            lam_init=lam_init)
    flat = lambda a: a.reshape(-1, a.shape[-1])
    t_all = t_p + t_s
    assert t_p % t_s == 0
    xm_p, xp_p, idx_p, gate_p, rank_p, cnt_p = mix(
        flat(da_p), flat(h_p), flat(pp["og"]), flat(x_prompt), cnt0=jnp.zeros((N_EXPERTS, 1), _F32), tm=PROJ_TILE)
    xm_s, xp_s, idx_s, gate_s, rank_s, cnt = mix(
        flat(da_s), flat(h_s), flat(ps["og"]), flat(x_sample), cnt0=cnt_p, tm=t_s)

    rows = t_all * TOP_K
    n_blocks = -(-rows // MOE_BLOCK) + N_EXPERTS
    counts = cnt[:, 0].astype(jnp.int32)
    padded = (counts + MOE_BLOCK - 1) // MOE_BLOCK * MOE_BLOCK
    pend = jnp.cumsum(padded)
    pstart = pend - counts
    experts = jnp.arange(N_EXPERTS, dtype=jnp.int32)[:, None, None]
    first_row = lambda idx: jnp.sum(jnp.where(idx[None] == experts, pstart[:, None, None], 0), axis=0)
    dest_p, dest_s = first_row(idx_p) + rank_p, first_row(idx_s) + rank_s
    n_used = (pend[-1] // MOE_BLOCK).astype(jnp.int32)[None]
    blk_lo = jnp.arange(n_blocks, dtype=jnp.int32) * MOE_BLOCK
    blk_e = jnp.minimum(jnp.sum((pend[None, :] <= blk_lo[:, None]).astype(jnp.int32), axis=1), N_EXPERTS - 1)
    row_lo = jnp.sum(jnp.where(blk_e[:, None] == experts[:, 0].T, pstart[None, :], 0), axis=1)
    blk_rows = jnp.where(blk_lo < pend[-1], jnp.clip(blk_lo + MOE_BLOCK - row_lo, 0, MOE_BLOCK), 0).astype(jnp.int32)

    def chunked(dest):
        chunk = min(DISPATCH_CHUNK, dest.shape[1] // SC_WORKERS)
        return dest.reshape(TOP_K, dest.shape[1] // chunk, chunk).transpose(1, 0, 2)

    assert rows % (SC_WORKERS * COMBINE_CHUNK) == 0
    xb = _sc_dispatch([(xp_p, chunked(dest_p)), (xp_s, chunked(dest_s))], n_blocks * MOE_BLOCK)
    dest = jnp.concatenate([dest_p, dest_s], axis=1)
    first_blk = jnp.concatenate([jnp.ones((1,), jnp.bool_), blk_e[1:] != blk_e[:-1]])
    blk_slot = ((jnp.cumsum(first_blk.astype(jnp.int32)) - 1) % 2).astype(jnp.int32)
    after = jnp.sum(jnp.where(blk_e[:, None] == experts[:, 0].T, pend[None, :], 0), axis=1) // MOE_BLOCK
    blk_next = jnp.where(after < n_used[0], blk_e[jnp.minimum(after, n_blocks - 1)], -1).astype(jnp.int32)
    yb = _moe(blk_e, n_used, blk_rows, blk_slot, blk_next, xb, w_gate_up[l], b_gate_up[l][:, None, :], w_down[l], b_down[l][:, None, :])
    yg = _sc_gather(yb, dest.reshape(rows // COMBINE_CHUNK, COMBINE_CHUNK)).reshape(TOP_K, t_all, PACK_W)
    nf = norm_f[None]
    y_p = _final(xm_p, yg, gate_p, nf, tm=FINAL_TILE, first_block=0).reshape(nb, s, D_MODEL)
    y_s = _final(xm_s, yg, gate_s, nf, tm=t_s, first_block=t_p // t_s).reshape(db, ls, D_MODEL)

    st = lambda a: a[None]
    return (y_p, y_s,
            st(kf_p.reshape(nb, s, DA_HEADS, 2, DA_QK)), st(vf_p.reshape(nb, s, DA_HEADS, DA_V)),
            st(c_p), st(n_p), st(mm_p),
            st(kf_s.reshape(db, ls, DA_HEADS, 2, DA_QK)), st(vf_s.reshape(db, ls, DA_HEADS, DA_V)),
            st(c_s.astype(state_C.dtype)), st(n_s.astype(state_n.dtype)), st(mm_s.astype(state_m.dtype)))
```

```python
import functools
import math

import jax
import jax.numpy as jnp
from jax import lax
from jax.experimental import pallas as pl
from jax.experimental.pallas import tpu as pltpu
from jax.experimental.pallas import tpu_sc as plsc

D_MODEL = 1024
CHUNK = 64
DA_HEADS = 4
DA_QK = 64
DA_V = 128
ML_HEADS = 4
ML_DK = 128
ML_DV = 128
HEAD_W = 128
SUBLANES = 8
N_GATES = 2 * ML_HEADS
CAST_ROWS = 128
GROUP_W = 512
ROPE_THETA = 10000.0
N_EXPERTS = 32
TOP_K = 4
D_FF = 1024
SWIGLU_LIMIT = 7.0
SWIGLU_ALPHA = 1.702
EPS = 1e-6
NEG_BIG = -1e30

LOG2E = math.log2(math.e)
VT_ROWS = 144
PROJ_TILE = 512
FINAL_TILE = 1024
ATTN_TQ = 512
MOE_BLOCK = 512
FF_CHUNK = 512
PACK_W = D_MODEL // 2
SC_CORES = 2
SC_SUBCORES = 16
SC_WORKERS = SC_CORES * SC_SUBCORES
DISPATCH_CHUNK = 64
COMBINE_CHUNK = 104
VMEM_BYTES = 64 * 1024 * 1024

_F32 = jnp.float32
_BF16 = jnp.bfloat16


def _nbytes(shape, dtype):
    return math.prod(shape) * jnp.dtype(dtype).itemsize


def _cparams(sem, blocks, resident=0):
    need = 2 * sum(blocks) + resident
    assert need <= VMEM_BYTES, need
    return pltpu.CompilerParams(dimension_semantics=sem, vmem_limit_bytes=need)


def _dot(a, b, dims=(((1,), (0,)), ((), ())), precision=None):
    return lax.dot_general(a, b, dims, precision=precision, preferred_element_type=_F32)


_NT = (((1,), (1,)), ((), ()))
_TN = (((0,), (0,)), ((), ()))


def _pack_rows(x):
    half = x.shape[1] // 2
    bits = lambda v: lax.bitcast_convert_type(v.astype(_BF16).astype(_F32), jnp.int32)
    return bits(x[:, :half]) | lax.shift_right_logical(bits(x[:, half:]), 16)


def _unpack_rows(p):
    hi = lax.bitcast_convert_type(p & jnp.int32(-65536), _F32)
    lo = lax.bitcast_convert_type(lax.shift_left(p, 16), _F32)
    return hi, lo


def _chunk_gate_rows(g8, block):
    L = CHUNK
    pos = lax.broadcasted_iota(jnp.int32, (1, block), 1) % L
    roll = lambda x, sh: pltpu.roll(x, sh % block, 1)
    steps = [1 << b for b in range(L.bit_length() - 1)]

    def scan(x, combine, fill, reverse=False):
        for sh in steps:
            ok = (pos < L - sh) if reverse else (pos >= sh)
            x = combine(x, jnp.where(ok, roll(x, -sh if reverse else sh), fill))
        return x

    to_head_rows = lambda x: pltpu.roll(x, ML_HEADS, 0)
    f8 = to_head_rows(scan(g8, jnp.add, 0.0))
    d8 = g8 - f8
    cm8 = scan(d8, jnp.maximum, -jnp.inf)
    big_g = to_head_rows(scan(g8, jnp.add, 0.0, reverse=True) - g8) + g8
    mg8 = jnp.maximum(scan(big_g, jnp.maximum, -jnp.inf), scan(big_g, jnp.maximum, -jnp.inf, reverse=True))
    return jnp.concatenate([f8, d8, cm8, mg8, jnp.exp(big_g - mg8)], axis=0)


def _proj_kernel(x_ref, g_ref, wt_ref, bg_ref, inv_ref, *refs, names, tm, pos_base, pos_mod):
    out = dict(zip(names, refs))
    cos_sc, sin_sc = refs[len(names):]
    x = x_ref[0]
    xn = (x * lax.rsqrt(jnp.mean(x * x, axis=-1, keepdims=True) + EPS) * g_ref[...]).astype(_BF16)

    row = lax.broadcasted_iota(jnp.int32, (tm, 1), 0)
    if pos_mod % tm == 0:
        tile_pos, row_pos = pos_base + (pl.program_id(1) * tm) % pos_mod, row
    else:
        assert tm % pos_mod == 0
        tile_pos, row_pos = pos_base, row % pos_mod

    @pl.when((pl.program_id(0) == 0) & (pl.program_id(1) == 0))
    def _():
        ang_row = row_pos.astype(_F32) * inv_ref[...]
        cos_sc[...] = jnp.cos(ang_row)
        sin_sc[...] = jnp.sin(ang_row)

    ang_tile = jnp.full((1, 1), tile_pos, jnp.int32).astype(_F32) * inv_ref[...]
    cos_t, sin_t = jnp.cos(ang_tile), jnp.sin(ang_tile)
    cos = cos_t * cos_sc[...] - sin_t * sin_sc[...]
    sin = sin_t * cos_sc[...] + cos_t * sin_sc[...]
    lane = lax.broadcasted_iota(jnp.int32, (1, HEAD_W), 1)
    first = (lane % DA_QK) < (DA_QK // 2)
    sin = jnp.where(first, -sin, sin)

    def rope(z):
        rot = jnp.where(first, pltpu.roll(z, HEAD_W - DA_QK // 2, 1), pltpu.roll(z, DA_QK // 2, 1))
        return z * cos + rot * sin

    def group(c, width=GROUP_W):
        return _dot(xn, wt_ref[c * GROUP_W:c * GROUP_W + width, :], _NT)

    def put_q(zq):
        for h in range(DA_HEADS):
            sl = slice(h * HEAD_W, (h + 1) * HEAD_W)
            out["q"][0, :, sl] = (rope(zq[:, sl]) * (DA_QK ** -0.5 * LOG2E)).astype(_BF16)

    def put_k(zk):
        for h in range(DA_HEADS):
            sl = slice(h * HEAD_W, (h + 1) * HEAD_W)
            rk = rope(zk[:, sl])
            out["kf"][0, :, sl] = rk
            out["kb"][0, :, sl] = rk.astype(_BF16)

    def with_ones_rows(zt):
        ones = jnp.ones((VT_ROWS - DA_V, tm), _BF16)
        return jnp.concatenate(
            [part for h in range(DA_HEADS) for part in (zt[h * DA_V:(h + 1) * DA_V], ones)], axis=0)

    def put_v(zv):
        for h in range(DA_HEADS):
            out["vf"][0, pl.ds(h, tm, stride=DA_HEADS), :] = zv[:, h * DA_V:(h + 1) * DA_V]
        if "vt" in out:
            out["vt"][0, 0] = with_ones_rows(zv.T.astype(_BF16))

    def put_mq(z):
        if "mq" in out:
            out["mq"][0] = z.astype(_BF16)
        if "mqt" in out:
            out["mqt"][0, 0] = z.T.astype(_BF16)

    def put_mk(z):
        out["mk"][0] = (z * (ML_DK ** -0.5)).astype(_BF16)

    def put_mv(z):
        if "mv" in out:
            out["mv"][0] = z.astype(_BF16)
        if "mvt" in out:
            out["mvt"][0, 0] = with_ones_rows(z.T.astype(_BF16))

    def put_mo(z):
        out["og"][0] = jax.nn.sigmoid(z).astype(_BF16)

    def put_gates(z):
        zg = z + bg_ref[...]
        lane_g = lax.broadcasted_iota(jnp.int32, zg.shape, 1)
        logsig = jnp.minimum(zg, 0.0) - jnp.log(1.0 + jnp.exp(-jnp.abs(zg)))
        gates = jnp.where(lane_g < ML_HEADS, zg, logsig)
        if "gc" in out:
            out["gc"][0] = gates[:, :N_GATES]
        g8 = gates.T[:N_GATES, :]
        if "gr" in out:
            out["gr"][0] = g8
        if "gs" in out:
            out["gs"][0] = _chunk_gate_rows(g8, tm)

    sinks = [(put_q, GROUP_W), (put_k, GROUP_W), (put_v, GROUP_W), (put_mq, GROUP_W), (put_mk, GROUP_W),
             (put_mv, GROUP_W), (put_mo, GROUP_W), (put_gates, HEAD_W)]
    pairs = [(c, sinks[c:c + 2]) for c in range(0, len(sinks), 2)]
    pairs = pairs[-1:] + pairs[:-1]
    width = lambda pair: sum(w for _, w in pair)
    nxt = group(pairs[0][0], width(pairs[0][1]))
    for n, (_, pair) in enumerate(pairs):
        cur = nxt
        if n + 1 < len(pairs):
            nxt = group(pairs[n + 1][0], width(pairs[n + 1][1]))
        pair[0][0](cur[:, :GROUP_W])
        pair[1][0](cur[:, GROUP_W:])


def _proj(x3, norm1, wt, b_gates, inv_full, *, names, tm, pos_base, pos_mod):
    nb, s, _ = x3.shape
    nt = s // tm
    tok = lambda w, dt: (jax.ShapeDtypeStruct((nb, s, w), dt), pl.BlockSpec((1, tm, w), lambda b, i: (b, i, 0)))
    slab = lambda rows: (jax.ShapeDtypeStruct((nb, nt, rows, tm), _BF16),
                         pl.BlockSpec((1, 1, rows, tm), lambda b, i: (b, i, 0, 0)))
    const = lambda shape: pl.BlockSpec(shape, lambda b, i: (0,) * len(shape))
    kinds = {
        "q": tok(GROUP_W, _BF16), "kf": tok(GROUP_W, _F32), "kb": tok(GROUP_W, _BF16),
        "vf": (jax.ShapeDtypeStruct((nb, s * DA_HEADS, DA_V), _F32),
               pl.BlockSpec((1, tm * DA_HEADS, DA_V), lambda b, i: (b, i, 0))),
        "vt": slab(DA_HEADS * VT_ROWS),
        "mq": tok(GROUP_W, _BF16), "mk": tok(GROUP_W, _BF16), "mv": tok(GROUP_W, _BF16), "og": tok(GROUP_W, _BF16),
        "gc": tok(N_GATES, _F32),
        "gr": (jax.ShapeDtypeStruct((nb, N_GATES, s), _F32), pl.BlockSpec((1, N_GATES, tm), lambda b, i: (b, 0, i))),
        "gs": (jax.ShapeDtypeStruct((nb, 5 * N_GATES, s), _F32),
               pl.BlockSpec((1, 5 * N_GATES, tm), lambda b, i: (b, 0, i))),
        "mqt": slab(GROUP_W), "mvt": slab(ML_HEADS * VT_ROWS),
    }
    blocks = [_nbytes((tm, D_MODEL), _F32), _nbytes(wt.shape, _BF16)]
    blocks += [_nbytes(kinds[n][1].block_shape, kinds[n][0].dtype) for n in names]
    outs = pl.pallas_call(
        functools.partial(_proj_kernel, names=names, tm=tm, pos_base=pos_base, pos_mod=pos_mod),
        grid=(nb, nt),
        in_specs=[pl.BlockSpec((1, tm, D_MODEL), lambda b, i: (b, i, 0)), const((1, D_MODEL)),
                  const(wt.shape), const((1, HEAD_W)), const((1, HEAD_W))],
        out_specs=[kinds[n][1] for n in names], out_shape=[kinds[n][0] for n in names],
        scratch_shapes=[pltpu.VMEM((tm, HEAD_W), _F32), pltpu.VMEM((tm, HEAD_W), _F32)],
        compiler_params=_cparams(("arbitrary", "arbitrary"), blocks, 8 * _nbytes((tm, GROUP_W), _F32)), name="proj",
    )(x3, norm1, wt, b_gates, inv_full)
    return dict(zip(names, outs))


def _lambda(lam_ref, lam_init):
    lv = lam_ref[...]
    s1 = jnp.sum(lv[0:1] * lv[1:2], axis=-1, keepdims=True)
    s2 = jnp.sum(lv[2:3] * lv[3:4], axis=-1, keepdims=True)
    return jnp.exp(s1) - jnp.exp(s2) + lam_init


def _split_components(q):
    lane = lax.broadcasted_iota(jnp.int32, q.shape, 1)
    zero = jnp.zeros_like(q)
    return jnp.concatenate([jnp.where(lane < DA_QK, q, zero), jnp.where(lane >= DA_QK, q, zero)], axis=0)


def _attn_kernel(lam_ref, q_ref, k_ref, vt_ref, o_ref, acc_ref, s_ref, *, tq, tk, lam_init):
    i = pl.program_id(1)
    heads = range(DA_HEADS)
    qz = [_split_components(q_ref[0, :, h * HEAD_W:(h + 1) * HEAD_W]) for h in heads]
    acc_ref[...] = jnp.zeros_like(acc_ref)

    def scores(h, j):
        k_t = k_ref[0, pl.ds(pl.multiple_of(j * tk, tk), tk), h * HEAD_W:(h + 1) * HEAD_W]
        return _dot(k_t, qz[h], _NT)

    def step(j, ms, last):
        if last:
            kpos = j * tk + lax.broadcasted_iota(jnp.int32, (tk, 1), 0)
            qpos = i * tq + lax.broadcasted_iota(jnp.int32, (1, 2 * tq), 1) % tq
            visible = kpos < (qpos // CHUNK + 1) * CHUNK
        out = []
        s = s_ref[...]
        for h in heads:
            s_next = None
            if h + 1 < DA_HEADS:
                s_next = scores(h + 1, j)
            elif not last:
                s_next = scores(0, j + 1)
            if last:
                s = jnp.where(visible, s, NEG_BIG)
            m_new = jnp.maximum(ms[h], jnp.max(s, axis=0, keepdims=True))
            alpha = jnp.exp2(ms[h] - m_new)
            p = jnp.exp2(s - m_new).astype(_BF16)
            acc_ref[h] = alpha * acc_ref[h] + _dot(vt_ref[0, j, h * VT_ROWS:(h + 1) * VT_ROWS, :], p)
            out.append(m_new)
            s = s_next
        if not last:
            s_ref[...] = s
        return tuple(out)

    n_full = (i * tq) // tk
    s_ref[...] = scores(0, 0)
    init = tuple(jnp.full((1, 2 * tq), NEG_BIG, _F32) for _ in heads)
    ms = lax.fori_loop(0, n_full, lambda j, c: step(j, c, False), init)
    step(n_full, ms, True)

    lam = _lambda(lam_ref, lam_init)
    for h in heads:
        acc = acc_ref[h]
        o = acc[:DA_V] / acc[DA_V:DA_V + 1]
        o_ref[0, :, h * HEAD_W:(h + 1) * HEAD_W] = (o[:, :tq] - lam * o[:, tq:]).T.astype(_BF16)


def _attn_prompt(lamv, q, kb, vt, *, lam_init):
    nb, s, _ = q.shape
    nk, tk = vt.shape[1], vt.shape[3]
    tq = ATTN_TQ
    assert tk % tq == 0
    return pl.pallas_call(
        functools.partial(_attn_kernel, tq=tq, tk=tk, lam_init=lam_init),
        grid=(nb, s // tq),
        in_specs=[pl.BlockSpec((4, DA_QK), lambda b, i: (0, 0)),
                  pl.BlockSpec((1, tq, GROUP_W), lambda b, i: (b, i, 0)),
                  pl.BlockSpec((1, s, GROUP_W), lambda b, i: (b, 0, 0), pipeline_mode=pl.Buffered(1)),
                  pl.BlockSpec((1, nk, DA_HEADS * VT_ROWS, tk), lambda b, i: (b, 0, 0, 0),
                               pipeline_mode=pl.Buffered(1))],
        out_specs=pl.BlockSpec((1, tq, GROUP_W), lambda b, i: (b, i, 0)),
        out_shape=jax.ShapeDtypeStruct((nb, s, GROUP_W), _BF16),
        scratch_shapes=[pltpu.VMEM((DA_HEADS, VT_ROWS, 2 * tq), _F32), pltpu.VMEM((tk, 2 * tq), _F32)],
        compiler_params=_cparams(
            ("parallel", "arbitrary"),
            [_nbytes((tq, GROUP_W), _BF16), _nbytes((tq, GROUP_W), _BF16)],
            _nbytes((s, GROUP_W), _BF16) + _nbytes(vt.shape[1:], _BF16)
            + _nbytes((DA_HEADS, VT_ROWS, 2 * tq), _F32) + 5 * _nbytes((tk, 2 * tq), _F32)), name="attn",
    )(lamv, q, kb, vt)


def _attn_sample_kernel(lam_ref, q_ref, kn_ref, vn_ref, kc_ref, vc_ref, o_ref, *, lq, past, lam_init):
    lam = _lambda(lam_ref, lam_init)
    for h in range(DA_HEADS):
        sl = slice(h * HEAD_W, (h + 1) * HEAD_W)
        qz = _split_components(q_ref[0, :, sl])
        kct = kc_ref[0, sl, :].astype(_BF16)
        vc = vc_ref[0, pl.ds(h, past, stride=DA_HEADS), :].astype(_BF16)
        s_c = _dot(qz, kct)
        s_n = _dot(qz, kn_ref[0, :, sl], _NT)
        m = jnp.maximum(jnp.max(s_c, axis=-1, keepdims=True), jnp.max(s_n, axis=-1, keepdims=True))
        p_c = jnp.exp2(s_c - m)
        p_n = jnp.exp2(s_n - m)
        l = jnp.sum(p_c, axis=-1, keepdims=True) + jnp.sum(p_n, axis=-1, keepdims=True)
        o = (_dot(p_c.astype(_BF16), vc) + _dot(p_n.astype(_BF16), vn_ref[0, :, sl])) / l
        o_ref[0, :, sl] = (o[:lq] - lam * o[lq:]).astype(_BF16)


def _attn_sample(lamv, q, kn, vn, kct, vc, *, lam_init):
    nb, lq, _ = q.shape
    past = kct.shape[2]
    new = pl.BlockSpec((1, lq, GROUP_W), lambda b: (b, 0, 0))
    return pl.pallas_call(
        functools.partial(_attn_sample_kernel, lq=lq, past=past, lam_init=lam_init),
        grid=(nb,),
        in_specs=[pl.BlockSpec((4, DA_QK), lambda b: (0, 0)), new, new, new,
                  pl.BlockSpec((1, GROUP_W, past), lambda b: (b, 0, 0)),
                  pl.BlockSpec((1, past * DA_HEADS, DA_V), lambda b: (b, 0, 0))],
        out_specs=new, out_shape=jax.ShapeDtypeStruct((nb, lq, GROUP_W), _BF16),
        compiler_params=_cparams(
            ("parallel",), [_nbytes((GROUP_W, past), _F32), _nbytes((past * DA_HEADS, DA_V), _F32)]
            + 4 * [_nbytes((lq, GROUP_W), _F32)], 8 * _nbytes((2 * lq, past), _F32)), name="attn_sample",
    )(lamv, q, kn, vn, kct, vc)


def _mlstm_kernel(q_ref, k_ref, v_ref, gc_ref, gr_ref, c0_ref, n0_ref, m0_ref, h_ref, c_ref, n_ref, m_ref,
                  c_sc, n_sc, m_sc, *, chunk, nchunks):
    j = pl.program_id(1)

    @pl.when(j == 0)
    def _():
        c_sc[...] = c0_ref[0]
        n_sc[...] = n0_ref[0]
        m_sc[...] = m0_ref[0]

    L = chunk
    r_i = lax.broadcasted_iota(jnp.int32, (L, L), 0)
    c_i = lax.broadcasted_iota(jnp.int32, (L, L), 1)
    causal = r_i >= c_i
    tril = causal.astype(_F32)
    triu = (r_i <= c_i).astype(_F32)
    hi = lax.Precision.HIGHEST

    for c in range(nchunks):
        rows = slice(c * L, (c + 1) * L)
        gcol = gc_ref[0, rows, :]
        grow = gr_ref[0, :, rows]
        fc_all = _dot(tril, gcol, precision=hi)
        fr_all = _dot(grow, triu, precision=hi)
        for hh in range(ML_HEADS):
            sl = slice(hh * HEAD_W, (hh + 1) * HEAD_W)
            m = m_sc[hh][0:1, 0:1]
            fc = fc_all[:, ML_HEADS + hh:ML_HEADS + hh + 1]
            fr = fr_all[ML_HEADS + hh:ML_HEADS + hh + 1, :]
            igc = gcol[:, hh:hh + 1]
            igr = grow[hh:hh + 1, :]
            a = jnp.where(causal, fc - fr + igr, -jnp.inf)
            b = fc + m
            m_t = jnp.maximum(b, jnp.max(a, axis=-1, keepdims=True))
            w = jnp.exp(a - m_t)
            sc = jnp.exp(b - m_t)
            q = q_ref[0, rows, sl]
            k = k_ref[0, rows, sl]
            v = v_ref[0, rows, sl]
            wqk = w * _dot(q, k, _NT)
            cmat = c_sc[hh]
            nrow = n_sc[hh:hh + 1, :]
            qn = jnp.sum(q.astype(_F32) * nrow, axis=-1, keepdims=True)
            num = _dot(wqk.astype(_BF16), v) + sc * _dot(q, cmat.astype(_BF16))
            den = jnp.sum(wqk, axis=-1, keepdims=True) + sc * qn
            h_ref[0, rows, sl] = (num / jnp.maximum(jnp.abs(den), jnp.exp(-m_t))).astype(_BF16)

            fl = fc[L - 1:L, :]
            g = fl - fc + igc
            m_new = jnp.maximum(fl + m, jnp.max(g, axis=0, keepdims=True))
            decay = jnp.exp(fl + m - m_new)
            kw = k.astype(_F32) * jnp.exp(g - m_new)
            c_sc[hh] = decay * cmat + _dot(kw.astype(_BF16), v, _TN)
            n_sc[hh:hh + 1, :] = decay * nrow + jnp.sum(kw, axis=0, keepdims=True)
            m_sc[hh] = jnp.broadcast_to(m_new, m_sc.shape[1:])

    @pl.when(j == pl.num_programs(1) - 1)
    def _():
        c_ref[0] = c_sc[...]
        n_ref[0] = n_sc[...]
        m_ref[0] = m_sc[...]


def _mlstm(q, k, v, gc, gr, c0, n0, m0, *, chunk, block):
    nb, s, _ = q.shape
    tok = lambda w: pl.BlockSpec((1, block, w), lambda b, j: (b, j, 0))
    cspec = pl.BlockSpec((1, ML_HEADS, ML_DK, ML_DV), lambda b, j: (b, 0, 0, 0))
    nspec = pl.BlockSpec((1, ML_HEADS, ML_DK), lambda b, j: (b, 0, 0))
    mspec = pl.BlockSpec((1, ML_HEADS, SUBLANES, HEAD_W), lambda b, j: (b, 0, 0, 0))
    return pl.pallas_call(
        functools.partial(_mlstm_kernel, chunk=chunk, nchunks=block // chunk),
        grid=(nb, s // block),
        in_specs=[tok(GROUP_W), tok(GROUP_W), tok(GROUP_W), tok(N_GATES),
                  pl.BlockSpec((1, N_GATES, block), lambda b, j: (b, 0, j)), cspec, nspec, mspec],
        out_specs=[tok(GROUP_W), cspec, nspec, mspec],
        out_shape=[jax.ShapeDtypeStruct((nb, s, GROUP_W), _BF16), jax.ShapeDtypeStruct(c0.shape, _F32),
                   jax.ShapeDtypeStruct(n0.shape, _F32), jax.ShapeDtypeStruct(m0.shape, _F32)],
        scratch_shapes=[pltpu.VMEM((ML_HEADS, ML_DK, ML_DV), _F32), pltpu.VMEM((ML_HEADS, ML_DK), _F32),
                        pltpu.VMEM((ML_HEADS, SUBLANES, HEAD_W), _F32)],
        compiler_params=_cparams(
            ("parallel", "arbitrary"), 5 * [_nbytes((block, GROUP_W), _F32)] + 2 * [_nbytes(c0.shape[1:], _F32)],
            2 * _nbytes(c0.shape[1:], _F32)), name="mlstm",
    )(q, k, v, gc, gr, c0, n0, m0)


def _mlstm_pairs_kernel(k_ref, qt_ref, vt_ref, gs_ref, st0_ref, m0_ref, h_ref, st_ref, m_ref, st_sc, m_sc,
                        *, block):
    j = pl.program_id(1)

    @pl.when(j == 0)
    def _():
        st_sc[...] = st0_ref[0]
        m_sc[...] = m0_ref[0]

    L, W = CHUNK, 2 * CHUNK
    rows8 = lambda n: gs_ref[0, n * N_GATES:(n + 1) * N_GATES, :]
    f8, d8, cm8, mg8, ws8 = (rows8(n) for n in range(5))
    d_cols = d8.T

    s_i = lax.broadcasted_iota(jnp.int32, (W, W), 0)
    t_i = lax.broadcasted_iota(jnp.int32, (W, W), 1)
    allowed = (s_i // L == t_i // L) & (s_i <= t_i)
    lane_w = lax.broadcasted_iota(jnp.int32, (1, W), 1)
    first = lane_w < L

    sts = [st_sc[hh] for hh in range(ML_HEADS)]
    ms = [m_sc[hh][0:1, 0:1] for hh in range(ML_HEADS)]

    def independent(p, hh):
        lanes = slice(p * W, (p + 1) * W)
        feat = slice(hh * HEAD_W, (hh + 1) * HEAD_W)
        kp = k_ref[0, lanes, feat]
        qt = qt_ref[0, 0, feat, lanes]
        v1t = vt_ref[0, 0, hh * VT_ROWS:(hh + 1) * VT_ROWS, lanes]
        ws = ws8[hh:hh + 1, lanes]
        v1f = v1t.astype(_F32)
        return dict(
            p=p, hh=hh, qt=qt, v1t=v1t, qk_t=_dot(kp, qt),
            upd_a=_dot((v1f * jnp.where(first, ws, 0.0)).astype(_BF16), kp),
            upd_b=_dot((v1f * jnp.where(first, 0.0, ws)).astype(_BF16), kp))

    def dependent(u):
        p, hh, qt = u["p"], u["hh"], u["qt"]
        lanes = slice(p * W, (p + 1) * W)
        row = lambda x: x[hh:hh + 1, lanes]
        cm, f_row = row(cm8), row(f8)
        w_t = jnp.exp(jnp.where(allowed, d_cols[lanes, hh:hh + 1] - cm, -jnp.inf))
        intra = _dot(u["v1t"], (w_t * u["qk_t"]).astype(_BF16))
        st_a, m_a = sts[hh], ms[hh]
        fl_a, mg_a = f_row[:, L - 1:L], row(mg8)[:, 0:1]
        fl_b, mg_b = f_row[:, W - 1:W], row(mg8)[:, L:L + 1]
        m_b = jnp.maximum(fl_a + m_a, mg_a)
        st_b = jnp.exp(fl_a + m_a - m_b) * st_a + jnp.exp(mg_a - m_b) * u["upd_a"]
        m_c = jnp.maximum(fl_b + m_b, mg_b)
        sts[hh] = jnp.exp(fl_b + m_b - m_c) * st_b + jnp.exp(mg_b - m_c) * u["upd_b"]
        ms[hh] = m_c
        inter = jnp.where(first, _dot(st_a.astype(_BF16), qt), _dot(st_b.astype(_BF16), qt))
        m_prev = jnp.where(first, m_a, m_b)
        mt = jnp.maximum(m_prev, cm)
        tot = jnp.exp(cm - mt) * intra + jnp.exp(m_prev - mt) * inter
        den = jnp.maximum(jnp.abs(tot[ML_DV:ML_DV + 1]), jnp.exp(-(f_row + mt)))
        h_ref[0, lanes, hh * HEAD_W:(hh + 1) * HEAD_W] = (tot[:ML_DV] / den).T.astype(_BF16)

    units = [(p, hh) for p in range(block // W) for hh in range(ML_HEADS)]
    nxt = independent(*units[0])
    for n in range(len(units)):
        cur = nxt
        if n + 1 < len(units):
            nxt = independent(*units[n + 1])
        dependent(cur)
    for hh in range(ML_HEADS):
        st_sc[hh] = sts[hh]
        m_sc[hh] = jnp.broadcast_to(ms[hh], m_sc.shape[1:])

    @pl.when(j == pl.num_programs(1) - 1)
    def _():
        st_ref[0] = st_sc[...]
        m_ref[0] = m_sc[...]


def _mlstm_pairs(k, qt, vt, gs, st0, m0):
    nb, s, _ = k.shape
    block = qt.shape[3]
    nt = s // block
    state = pl.BlockSpec((1, ML_HEADS, VT_ROWS, ML_DK), lambda b, j: (b, 0, 0, 0))
    mspec = pl.BlockSpec((1, ML_HEADS, SUBLANES, HEAD_W), lambda b, j: (b, 0, 0, 0))
    tok = pl.BlockSpec((1, block, GROUP_W), lambda b, j: (b, j, 0))
    return pl.pallas_call(
        functools.partial(_mlstm_pairs_kernel, block=block),
        grid=(nb, nt),
        in_specs=[tok, pl.BlockSpec((1, 1, GROUP_W, block), lambda b, j: (b, j, 0, 0)),
                  pl.BlockSpec((1, 1, ML_HEADS * VT_ROWS, block), lambda b, j: (b, j, 0, 0)),
                  pl.BlockSpec((1, 5 * N_GATES, block), lambda b, j: (b, 0, j)), state, mspec],
        out_specs=[tok, state, mspec],
        out_shape=[jax.ShapeDtypeStruct((nb, s, GROUP_W), _BF16), jax.ShapeDtypeStruct(st0.shape, _F32),
                   jax.ShapeDtypeStruct(m0.shape, _F32)],
        scratch_shapes=[pltpu.VMEM((ML_HEADS, VT_ROWS, ML_DK), _F32), pltpu.VMEM((ML_HEADS, SUBLANES, HEAD_W), _F32)],
        compiler_params=_cparams(
            ("parallel", "arbitrary"),
            3 * [_nbytes((block, GROUP_W), _BF16)] + [_nbytes((ML_HEADS * VT_ROWS, block), _BF16)]
            + 2 * [_nbytes(st0.shape[1:], _F32)], 16 * _nbytes((VT_ROWS, 2 * CHUNK), _F32) * ML_HEADS), name="mlstm_pairs",
    )(k, qt, vt, gs, st0, m0)


def _mix_kernel(da_ref, ml_ref, og_ref, x_ref, subln_ref, wo_ref, g2_ref, wr_ref, br_ref, cnt0_ref,
                xmid_ref, xp_ref, idx_ref, gate_ref, rank_ref, cnt_ref, cnt_sc, upper_sc, *, tm, lam_init):
    step = pl.program_id(0)

    @pl.when(step == 0)
    def _():
        cnt_sc[...] = cnt0_ref[...]
        earlier = lax.broadcasted_iota(jnp.int32, (tm, tm), 0) < lax.broadcasted_iota(jnp.int32, (tm, tm), 1)
        upper_sc[...] = earlier.astype(_BF16)

    parts = []
    for h in range(DA_HEADS):
        d = da_ref[:, h * HEAD_W:(h + 1) * HEAD_W].astype(_F32)
        d = d * lax.rsqrt(jnp.mean(d * d, axis=-1, keepdims=True) + EPS) * subln_ref[...]
        parts.append((d * (1.0 - lam_init)).astype(_BF16))
    parts.append((og_ref[...].astype(_F32) * ml_ref[...].astype(_F32)).astype(_BF16))
    xm = x_ref[...] + _dot(jnp.concatenate(parts, axis=1), wo_ref[...])
    xmid_ref[...] = xm
    xn_f = xm * lax.rsqrt(jnp.mean(xm * xm, axis=-1, keepdims=True) + EPS) * g2_ref[...]
    xp_ref[...] = _pack_rows(xn_f)
    xn = xn_f.astype(_BF16)

    logits = _dot(wr_ref[...], xn, _NT) + br_ref[...]
    e_i = lax.broadcasted_iota(jnp.int32, logits.shape, 0)
    member = jnp.zeros(logits.shape, jnp.bool_)
    vals, idxs = [], []
    for _ in range(TOP_K):
        mx = jnp.max(logits, axis=0, keepdims=True)
        sel = jnp.min(jnp.where(logits == mx, e_i, N_EXPERTS), axis=0, keepdims=True)
        hit = e_i == sel
        member = member | hit
        logits = jnp.where(hit, -jnp.inf, logits)
        vals.append(mx)
        idxs.append(sel)
    ex = [jnp.exp(v - vals[0]) for v in vals]
    tot = ex[0] + ex[1] + ex[2] + ex[3]
    idx_ref[...] = jnp.concatenate(idxs, axis=0)
    gate_ref[...] = jnp.concatenate([e / tot for e in ex], axis=0)

    memf = member.astype(_F32)
    before = _dot(memf.astype(_BF16), upper_sc[...]) + cnt_sc[...]
    rank_ref[...] = jnp.concatenate(
        [jnp.sum(jnp.where(e_i == s, before, 0.0), axis=0, keepdims=True) for s in idxs], axis=0).astype(jnp.int32)
    cnt_sc[...] += jnp.sum(memf, axis=1, keepdims=True)
    cnt_ref[...] = cnt_sc[...]


def _mix(da, ml, og, x, subln, w_out, norm2, w_rt, b_r, cnt0, *, tm, lam_init):
    t = x.shape[0]
    tok = lambda w: pl.BlockSpec((tm, w), lambda i: (i, 0))
    const = lambda shape: pl.BlockSpec(shape, lambda i: (0,) * len(shape))
    lane_tok = pl.BlockSpec((TOP_K, tm), lambda i: (0, i))
    return pl.pallas_call(
        functools.partial(_mix_kernel, tm=tm, lam_init=lam_init),
        grid=(t // tm,),
        in_specs=[tok(GROUP_W), tok(GROUP_W), tok(GROUP_W), tok(D_MODEL), const((1, DA_V)),
                  const((D_MODEL, D_MODEL)), const((1, D_MODEL)), const((N_EXPERTS, D_MODEL)),
                  const((N_EXPERTS, 1)), const((N_EXPERTS, 1))],
        out_specs=[tok(D_MODEL), tok(PACK_W), lane_tok, lane_tok, lane_tok, const((N_EXPERTS, 1))],
        out_shape=[jax.ShapeDtypeStruct((t, D_MODEL), _F32), jax.ShapeDtypeStruct((t, PACK_W), jnp.int32),
                   jax.ShapeDtypeStruct((TOP_K, t), jnp.int32), jax.ShapeDtypeStruct((TOP_K, t), _F32),
                   jax.ShapeDtypeStruct((TOP_K, t), jnp.int32), jax.ShapeDtypeStruct((N_EXPERTS, 1), _F32)],
        scratch_shapes=[pltpu.VMEM((N_EXPERTS, 1), _F32), pltpu.VMEM((tm, tm), _BF16)],
        compiler_params=_cparams(
            ("arbitrary",),
            3 * [_nbytes((tm, GROUP_W), _BF16)] + 2 * [_nbytes((tm, D_MODEL), _F32)] + [_nbytes((tm, PACK_W), _F32)]
            + [_nbytes((D_MODEL, D_MODEL), _BF16)], 4 * _nbytes((tm, D_MODEL), _F32) + 2 * _nbytes((tm, tm), _F32)), name="mix",
    )(da, ml, og, x, subln, w_out, norm2, w_rt, b_r, cnt0)


def _moe_kernel(be_ref, nu_ref, nv_ref, slot_ref, nxt_ref, x_ref, wgu_hbm, bgu_ref, wd_hbm, bd_ref, y_ref,
                wgu_f32, wd_f32, wgu_sc, wd_sc, sems):
    j = pl.program_id(0)
    used = j < nu_ref[0]
    fresh = (j == 0) | (be_ref[j] != be_ref[jnp.maximum(j - 1, 0)])
    slot = slot_ref[j]

    def fetch(e, s):
        return (pltpu.make_async_copy(wgu_hbm.at[e], wgu_f32.at[s], sems.at[s, 0]),
                pltpu.make_async_copy(wd_hbm.at[e], wd_f32.at[s], sems.at[s, 1]))

    @pl.when(j == 0)
    def _():
        for cp in fetch(be_ref[0], slot):
            cp.start()

    @pl.when(fresh & used)
    def _():
        for cp in fetch(be_ref[j], slot):
            cp.wait()

        @pl.when(nxt_ref[j] >= 0)
        def _():
            for cp in fetch(nxt_ref[j], 1 - slot):
                cp.start()

        rows = CAST_ROWS
        def cast_gu(r, _):
            sl = pl.ds(pl.multiple_of(r * rows, rows), rows)
            wgu_sc[sl, :] = wgu_f32[slot, sl, :].astype(_BF16)
            return 0
        lax.fori_loop(0, D_MODEL // rows, cast_gu, 0)
        def cast_d(r, _):
            sl = pl.ds(pl.multiple_of(r * rows, rows), rows)
            wd_sc[sl, :] = wd_f32[slot, sl, :].astype(_BF16)
            return 0
        lax.fori_loop(0, D_FF // rows, cast_d, 0)

    @pl.when(j >= nu_ref[0])
    def _():
        y_ref[...] = jnp.zeros_like(y_ref)

    def ffn(row0):
        hi, lo = _unpack_rows(x_ref[row0:, :])
        x = jnp.concatenate([hi.astype(_BF16), lo.astype(_BF16)], axis=1)
        n_chunks = D_FF // FF_CHUNK

        def gate_up(c):
            lo = c * FF_CHUNK
            return (_dot(x, wgu_sc[:, lo:lo + FF_CHUNK]), _dot(x, wgu_sc[:, D_FF + lo:D_FF + lo + FF_CHUNK]))

        acc = None
        nxt = gate_up(0)
        for c in range(n_chunks):
            lo = c * FF_CHUNK
            gate, up = nxt
            if c + 1 < n_chunks:
                nxt = gate_up(c + 1)
            gate = jnp.minimum(gate + bgu_ref[0, :, lo:lo + FF_CHUNK], SWIGLU_LIMIT)
            up = jnp.clip(up + bgu_ref[0, :, D_FF + lo:D_FF + lo + FF_CHUNK], -SWIGLU_LIMIT, SWIGLU_LIMIT)
            act = (up + 1.0) * (gate * jax.nn.sigmoid(SWIGLU_ALPHA * gate))
            down = _dot(act.astype(_BF16), wd_sc[lo:lo + FF_CHUNK, :])
            acc = down if acc is None else acc + down
        y_ref[row0:, :] = _pack_rows(acc + bd_ref[0])

    half = MOE_BLOCK // 2
    @pl.when(used & (nv_ref[j] > half))
    def _():
        ffn(0)

    @pl.when(used & (nv_ref[j] <= half))
    def _():
        y_ref[:half, :] = jnp.zeros((half, PACK_W), jnp.int32)
        ffn(half)


def _moe(blk_e, n_used, blk_rows, blk_slot, blk_next, xb, w_gu, b_gu, w_d, b_d):
    n_blocks = xb.shape[0] // MOE_BLOCK
    last = lambda j, nu: jnp.minimum(j, nu[0] - 1)
    row_spec = pl.BlockSpec((MOE_BLOCK, PACK_W), lambda j, be, nu, nv, sl, nx: (last(j, nu), 0))
    bias_spec = lambda c: pl.BlockSpec((1, 1, c), lambda j, be, nu, nv, sl, nx: (be[last(j, nu)], 0, 0))
    hbm = pl.BlockSpec(memory_space=pl.ANY)
    grid_spec = pltpu.PrefetchScalarGridSpec(
        num_scalar_prefetch=5, grid=(n_blocks,),
        in_specs=[row_spec, hbm, bias_spec(2 * D_FF), hbm, bias_spec(D_MODEL)],
        out_specs=pl.BlockSpec((MOE_BLOCK, PACK_W), lambda j, be, nu, nv, sl, nx: (j, 0)),
        scratch_shapes=[pltpu.VMEM((2, D_MODEL, 2 * D_FF), _F32), pltpu.VMEM((2, D_FF, D_MODEL), _F32),
                        pltpu.VMEM((D_MODEL, 2 * D_FF), _BF16), pltpu.VMEM((D_FF, D_MODEL), _BF16),
                        pltpu.SemaphoreType.DMA((2, 2))])
    return pl.pallas_call(
        _moe_kernel, grid_spec=grid_spec, out_shape=jax.ShapeDtypeStruct(xb.shape, jnp.int32),
        compiler_params=_cparams(
            ("arbitrary",), 2 * [_nbytes((MOE_BLOCK, PACK_W), _F32)],
            2 * _nbytes((D_MODEL, 3 * D_FF), _F32) + _nbytes((D_MODEL, 3 * D_FF), _BF16)
            + 4 * _nbytes((MOE_BLOCK, D_MODEL), _F32)), name="moe",
    )(blk_e, n_used, blk_rows, blk_slot, blk_next, xb, w_gu, b_gu, w_d, b_d)


def _sc_mesh():
    return plsc.VectorSubcoreMesh(core_axis_name="c", subcore_axis_name="s")


def _sc_worker():
    return lax.axis_index("s") * SC_CORES + lax.axis_index("c")


def _sc_dispatch(sources, n_rows):
    n_src = len(sources)
    shapes = []
    for _, dest3 in sources:
        n_chunks, _, chunk = dest3.shape
        per_worker = n_chunks // SC_WORKERS
        assert per_worker * SC_WORKERS == n_chunks and chunk % SUBLANES == 0
        shapes.append((per_worker, chunk))

    def body(*refs):
        x_hbms, d_hbms, o_hbm = refs[:n_src], refs[n_src:2 * n_src], refs[2 * n_src]
        scratch = refs[2 * n_src + 1:]
        load_a, load_b, scat_a, scat_b = scratch[3 * n_src:]
        worker = _sc_worker()
        for i, (per_worker, chunk) in enumerate(shapes):
            x_hbm, idx_v, rows_a, rows_b = x_hbms[i], scratch[3 * i], scratch[3 * i + 1], scratch[3 * i + 2]
            pltpu.sync_copy(d_hbms[i].at[worker], idx_v)

            def load(ci, rows, sem):
                row0 = (worker * per_worker + ci) * chunk
                return pltpu.make_async_copy(x_hbm.at[pl.ds(pl.multiple_of(row0, SUBLANES), chunk)], rows, sem)

            def scatters(ci, rows, sem):
                return [pltpu.make_async_copy(rows, o_hbm.at[idx_v.at[ci * TOP_K + k]], sem) for k in range(TOP_K)]

            def start(copies):
                for cp in copies:
                    cp.start()

            def wait(copies):
                for cp in copies:
                    cp.wait()

            load(0, rows_a, load_a).start()

            @pl.loop(0, per_worker // 2)
            def _(p):
                ca, cb = 2 * p, 2 * p + 1
                load(ca, rows_a, load_a).wait()
                load(cb, rows_b, load_b).start()
                start(scatters(ca, rows_a, scat_a))
                load(cb, rows_b, load_b).wait()
                wait(scatters(ca, rows_a, scat_a))
                load(jnp.minimum(ca + 2, per_worker - 1), rows_a, load_a).start()
                start(scatters(cb, rows_b, scat_b))
                wait(scatters(cb, rows_b, scat_b))

            last = per_worker - 1
            load(last, rows_a, load_a).wait()
            if per_worker % 2:
                start(scatters(last, rows_a, scat_a))
                wait(scatters(last, rows_a, scat_a))

    dma = pltpu.SemaphoreType.DMA
    scratch_types = []
    for per_worker, chunk in shapes:
        scratch_types += [pltpu.VMEM((per_worker * TOP_K, chunk), jnp.int32), pltpu.VMEM((chunk, PACK_W), jnp.int32),
                          pltpu.VMEM((chunk, PACK_W), jnp.int32)]
    tables = [dest3.reshape(SC_WORKERS, pw * TOP_K, chunk) for (_, dest3), (pw, chunk) in zip(sources, shapes)]
    return pl.kernel(
        body, out_type=jax.ShapeDtypeStruct((n_rows, PACK_W), jnp.int32), mesh=_sc_mesh(),
        scratch_types=scratch_types + [dma, dma, dma, dma],
        name="sc_dispatch")(*[xp for xp, _ in sources], *tables)


def _sc_gather(table, idx2):
    n_chunks, chunk = idx2.shape
    per_worker = n_chunks // SC_WORKERS
    assert per_worker * SC_WORKERS == n_chunks and chunk % SUBLANES == 0

    def body(t_hbm, i_hbm, o_hbm, idx_v, rows_a, rows_b, gath_a, gath_b, put_a, put_b):
        worker = _sc_worker()
        pltpu.sync_copy(i_hbm.at[worker], idx_v)

        def gather(ci, rows, sem):
            return pltpu.make_async_copy(t_hbm.at[idx_v.at[ci]], rows, sem)

        def put(ci, rows, sem):
            row0 = (worker * per_worker + ci) * chunk
            return pltpu.make_async_copy(rows, o_hbm.at[pl.ds(pl.multiple_of(row0, SUBLANES), chunk)], sem)

        gather(0, rows_a, gath_a).start()

        @pl.loop(0, per_worker // 2)
        def _(p):
            ca, cb = 2 * p, 2 * p + 1
            gather(ca, rows_a, gath_a).wait()
            gather(cb, rows_b, gath_b).start()
            put(ca, rows_a, put_a).start()
            gather(cb, rows_b, gath_b).wait()
            put(ca, rows_a, put_a).wait()
            gather(jnp.minimum(ca + 2, per_worker - 1), rows_a, gath_a).start()
            put(cb, rows_b, put_b).start()
            put(cb, rows_b, put_b).wait()

        last = per_worker - 1
        gather(last, rows_a, gath_a).wait()
        if per_worker % 2:
            put(last, rows_a, put_a).start()
            put(last, rows_a, put_a).wait()

    dma = pltpu.SemaphoreType.DMA
    return pl.kernel(
        body, out_type=jax.ShapeDtypeStruct((n_chunks * chunk, PACK_W), jnp.int32), mesh=_sc_mesh(),
        scratch_types=[pltpu.VMEM((per_worker, chunk), jnp.int32), pltpu.VMEM((chunk, PACK_W), jnp.int32),
                       pltpu.VMEM((chunk, PACK_W), jnp.int32), dma, dma, dma, dma],
        name="sc_gather")(table, idx2.reshape(SC_WORKERS, per_worker, chunk))


def _final_kernel(x_ref, y_ref, g_ref, nf_ref, o_ref):
    g_rows = g_ref[...]
    g = jnp.concatenate([g_rows, jnp.zeros((SUBLANES - TOP_K, g_rows.shape[1]), _F32)], axis=0).T
    hi = jnp.zeros((x_ref.shape[0], PACK_W), _F32)
    lo = jnp.zeros((x_ref.shape[0], PACK_W), _F32)
    for k in range(TOP_K):
        y_hi, y_lo = _unpack_rows(y_ref[k])
        hi = hi + g[:, k:k + 1] * y_hi
        lo = lo + g[:, k:k + 1] * y_lo
    x = x_ref[...] + jnp.concatenate([hi, lo], axis=1)
    o_ref[...] = x * lax.rsqrt(jnp.mean(x * x, axis=-1, keepdims=True) + EPS) * nf_ref[...]


def _final(xmid, yg, gates, norm_f, *, tm, first_block):
    t = xmid.shape[0]
    return pl.pallas_call(
        _final_kernel, grid=(t // tm,),
        in_specs=[pl.BlockSpec((tm, D_MODEL), lambda i: (i, 0)),
                  pl.BlockSpec((TOP_K, tm, PACK_W), lambda i: (0, i + first_block, 0)),
                  pl.BlockSpec((TOP_K, tm), lambda i: (0, i)),
                  pl.BlockSpec((1, D_MODEL), lambda i: (0, 0))],
        out_specs=pl.BlockSpec((tm, D_MODEL), lambda i: (i, 0)),
        out_shape=jax.ShapeDtypeStruct((t, D_MODEL), _F32),
        compiler_params=_cparams(
            ("parallel",), 2 * [_nbytes((tm, D_MODEL), _F32)] + [_nbytes((TOP_K, tm, PACK_W), _F32)],
            2 * _nbytes((tm, D_MODEL), _F32)), name="final",
    )(xmid, yg, gates, norm_f)


def kernel(x_prompt, x_sample, cache_k, cache_v, state_C, state_n, state_m, norm1, w_in, b_igate, b_fgate,
           lambda_q1, lambda_k1, lambda_q2, lambda_k2, subln, w_out, norm2, w_router, b_router, w_gate_up,
           b_gate_up, w_down, b_down, norm_f):
    nb, s, _ = x_prompt.shape
    db, ls, _ = x_sample.shape
    past = cache_k.shape[2]
    depth = w_in.shape[0]
    assert depth == 1 and s % PROJ_TILE == 0 and (db * ls) % SUBLANES == 0
    l = 0
    lam_init = 0.8 - 0.6 * math.exp(-0.3 * l)
    t_p, t_s = nb * s, db * ls

    w_nat = jnp.pad(w_in[l].T, ((0, HEAD_W - N_GATES), (0, 0))).astype(_BF16)
    b_gates = jnp.pad(jnp.concatenate([b_igate[l], b_fgate[l]]).astype(_F32), (0, HEAD_W - N_GATES))[None]
    inv = ROPE_THETA ** (-jnp.arange(0, DA_QK, 2, dtype=_F32) / DA_QK)
    inv_full = jnp.tile(inv, HEAD_W // (DA_QK // 2))[None]
    lamv = jnp.stack([lambda_q1[l], lambda_k1[l], lambda_q2[l], lambda_k2[l]]).astype(_F32)
    g1 = norm1[l][None]

    pp = _proj(x_prompt, g1, w_nat, b_gates, inv_full, tm=PROJ_TILE, pos_base=0, pos_mod=s,
               names=("q", "kf", "kb", "vf", "vt", "mk", "og", "gs", "mqt", "mvt"))
    ps = _proj(x_sample.reshape(1, t_s, D_MODEL), g1, w_nat, b_gates, inv_full, tm=t_s, pos_base=past, pos_mod=ls,
               names=("q", "kf", "kb", "vf", "mq", "mk", "mv", "og", "gc", "gr"))
    kf_p, vf_p, kf_s, vf_s = pp["kf"], pp["vf"], ps["kf"], ps["vf"]

    da_p = _attn_prompt(lamv, pp["q"], pp["kb"], pp["vt"], lam_init=lam_init)
    per_b = lambda a: a.reshape(db, ls, a.shape[-1])
    kct = jnp.transpose(cache_k[l], (0, 2, 3, 4, 1)).reshape(db, GROUP_W, past)
    vc = cache_v[l].reshape(db, past * DA_HEADS, DA_V)
    vn = vf_s.reshape(db, ls, GROUP_W).astype(_BF16)
    da_s = _attn_sample(lamv, per_b(ps["q"]), per_b(ps["kb"]), vn, kct, vc, lam_init=lam_init)

    assert PROJ_TILE % (2 * CHUNK) == 0
    h_p, st_p, m_p = _mlstm_pairs(pp["mk"], pp["mqt"], pp["mvt"], pp["gs"],
                                  jnp.zeros((nb, ML_HEADS, VT_ROWS, ML_DK), _F32),
                                  jnp.zeros((nb, ML_HEADS, SUBLANES, HEAD_W), _F32))
    gr_sb = ps["gr"].reshape(8, db, ls).transpose(1, 0, 2)
    m0_s = jnp.broadcast_to(state_m[l].astype(_F32)[:, :, None, None], (db, ML_HEADS, SUBLANES, HEAD_W))
    h_s, c_s, n_s, m_s = _mlstm(per_b(ps["mq"]), per_b(ps["mk"]), per_b(ps["mv"]), per_b(ps["gc"]), gr_sb,
                                state_C[l].astype(_F32), state_n[l].astype(_F32), m0_s, chunk=ls, block=ls)
    c_p, n_p, mm_p = jnp.swapaxes(st_p[:, :, :ML_DV, :], -1, -2), st_p[:, :, ML_DV, :], m_p[:, :, 0, 0]
    mm_s = m_s[:, :, 0, 0]

    wo = w_out[l].astype(_BF16)
    w_rt = w_router[l].T.astype(_BF16)
    b_r = b_router[l].astype(_F32)[:, None]
    mix = functools.partial(_mix, subln=subln[l][None], w_out=wo, norm2=norm2[l][None], w_rt=w_rt, b_r=b_r,
                            lam_init=lam_init)
    flat = lambda a: a.reshape(-1, a.shape[-1])
    t_all = t_p + t_s
    assert t_p % t_s == 0
    xm_p, xp_p, idx_p, gate_p, rank_p, cnt_p = mix(
        flat(da_p), flat(h_p), flat(pp["og"]), flat(x_prompt), cnt0=jnp.zeros((N_EXPERTS, 1), _F32), tm=PROJ_TILE)
    xm_s, xp_s, idx_s, gate_s, rank_s, cnt = mix(
        flat(da_s), flat(h_s), flat(ps["og"]), flat(x_sample), cnt0=cnt_p, tm=t_s)

    rows = t_all * TOP_K
    n_blocks = -(-rows // MOE_BLOCK) + N_EXPERTS
    counts = cnt[:, 0].astype(jnp.int32)
    padded = (counts + MOE_BLOCK - 1) // MOE_BLOCK * MOE_BLOCK
    pend = jnp.cumsum(padded)
    pstart = pend - counts
    experts = jnp.arange(N_EXPERTS, dtype=jnp.int32)[:, None, None]
    first_row = lambda idx: jnp.sum(jnp.where(idx[None] == experts, pstart[:, None, None], 0), axis=0)
    dest_p, dest_s = first_row(idx_p) + rank_p, first_row(idx_s) + rank_s
    n_used = (pend[-1] // MOE_BLOCK).astype(jnp.int32)[None]
    blk_lo = jnp.arange(n_blocks, dtype=jnp.int32) * MOE_BLOCK
    blk_e = jnp.minimum(jnp.sum((pend[None, :] <= blk_lo[:, None]).astype(jnp.int32), axis=1), N_EXPERTS - 1)
    row_lo = jnp.sum(jnp.where(blk_e[:, None] == experts[:, 0].T, pstart[None, :], 0), axis=1)
    blk_rows = jnp.where(blk_lo < pend[-1], jnp.clip(blk_lo + MOE_BLOCK - row_lo, 0, MOE_BLOCK), 0).astype(jnp.int32)

    def chunked(dest):
        chunk = min(DISPATCH_CHUNK, dest.shape[1] // SC_WORKERS)
        return dest.reshape(TOP_K, dest.shape[1] // chunk, chunk).transpose(1, 0, 2)

    assert rows % (SC_WORKERS * COMBINE_CHUNK) == 0
    xb = _sc_dispatch([(xp_p, chunked(dest_p)), (xp_s, chunked(dest_s))], n_blocks * MOE_BLOCK)
    dest = jnp.concatenate([dest_p, dest_s], axis=1)
    first_blk = jnp.concatenate([jnp.ones((1,), jnp.bool_), blk_e[1:] != blk_e[:-1]])
    blk_slot = ((jnp.cumsum(first_blk.astype(jnp.int32)) - 1) % 2).astype(jnp.int32)
    after = jnp.sum(jnp.where(blk_e[:, None] == experts[:, 0].T, pend[None, :], 0), axis=1) // MOE_BLOCK
    blk_next = jnp.where(after < n_used[0], blk_e[jnp.minimum(after, n_blocks - 1)], -1).astype(jnp.int32)
    yb = _moe(blk_e, n_used, blk_rows, blk_slot, blk_next, xb, w_gate_up[l], b_gate_up[l][:, None, :], w_down[l], b_down[l][:, None, :])
    yg = _sc_gather(yb, dest.reshape(rows // COMBINE_CHUNK, COMBINE_CHUNK)).reshape(TOP_K, t_all, PACK_W)
    nf = norm_f[None]
    y_p = _final(xm_p, yg, gate_p, nf, tm=FINAL_TILE, first_block=0).reshape(nb, s, D_MODEL)
    y_s = _final(xm_s, yg, gate_s, nf, tm=t_s, first_block=t_p // t_s).reshape(db, ls, D_MODEL)

    st = lambda a: a[None]
    return (y_p, y_s,
            st(kf_p.reshape(nb, s, DA_HEADS, 2, DA_QK)), st(vf_p.reshape(nb, s, DA_HEADS, DA_V)),
            st(c_p), st(n_p), st(mm_p),
            st(kf_s.reshape(db, ls, DA_HEADS, 2, DA_QK)), st(vf_s.reshape(db, ls, DA_HEADS, DA_V)),
            st(c_s.astype(state_C.dtype)), st(n_s.astype(state_n.dtype)), st(mm_s.astype(state_m.dtype)))
```

```python
import functools
import math

import jax
import jax.numpy as jnp
from jax import lax
from jax.experimental import pallas as pl
from jax.experimental.pallas import tpu as pltpu
from jax.experimental.pallas import tpu_sc as plsc

D_MODEL = 1024
CHUNK = 64
DA_HEADS = 4
DA_QK = 64
DA_V = 128
ML_HEADS = 4
ML_DK = 128
ML_DV = 128
HEAD_W = 128
SUBLANES = 8
N_GATES = 2 * ML_HEADS
CAST_ROWS = 128
MLSTM_SAMPLE_ROWS = 8
GROUP_W = 512
ROPE_THETA = 10000.0
N_EXPERTS = 32
TOP_K = 4
D_FF = 1024
SWIGLU_LIMIT = 7.0
SWIGLU_ALPHA = 1.702
EPS = 1e-6
NEG_BIG = -1e30

LOG2E = math.log2(math.e)
VT_ROWS = 144
PROJ_TILE = 512
FINAL_TILE = 1024
ATTN_TQ = 512
MOE_BLOCK = 512
FF_CHUNK = 512
PACK_W = D_MODEL // 2
SC_CORES = 2
SC_SUBCORES = 16
SC_WORKERS = SC_CORES * SC_SUBCORES
DISPATCH_CHUNK = 64
COMBINE_CHUNK = 104
VMEM_BYTES = 64 * 1024 * 1024

_F32 = jnp.float32
_BF16 = jnp.bfloat16


def _nbytes(shape, dtype):
    return math.prod(shape) * jnp.dtype(dtype).itemsize


def _cparams(sem, blocks, resident=0):
    need = 2 * sum(blocks) + resident
    assert need <= VMEM_BYTES, need
    return pltpu.CompilerParams(dimension_semantics=sem, vmem_limit_bytes=need)


def _dot(a, b, dims=(((1,), (0,)), ((), ())), precision=None):
    return lax.dot_general(a, b, dims, precision=precision, preferred_element_type=_F32)


_NT = (((1,), (1,)), ((), ()))
_TN = (((0,), (0,)), ((), ()))


def _pack_rows(x):
    half = x.shape[1] // 2
    bits = lambda v: lax.bitcast_convert_type(v.astype(_BF16).astype(_F32), jnp.int32)
    return bits(x[:, :half]) | lax.shift_right_logical(bits(x[:, half:]), 16)


def _unpack_rows(p):
    hi = lax.bitcast_convert_type(p & jnp.int32(-65536), _F32)
    lo = lax.bitcast_convert_type(lax.shift_left(p, 16), _F32)
    return hi, lo


def _chunk_gate_rows(g8, block):
    L = CHUNK
    pos = lax.broadcasted_iota(jnp.int32, (1, block), 1) % L
    roll = lambda x, sh: pltpu.roll(x, sh % block, 1)
    steps = [1 << b for b in range(L.bit_length() - 1)]

    def scan(x, combine, fill, reverse=False):
        for sh in steps:
            ok = (pos < L - sh) if reverse else (pos >= sh)
            x = combine(x, jnp.where(ok, roll(x, -sh if reverse else sh), fill))
        return x

    to_head_rows = lambda x: pltpu.roll(x, ML_HEADS, 0)
    f8 = to_head_rows(scan(g8, jnp.add, 0.0))
    d8 = g8 - f8
    cm8 = scan(d8, jnp.maximum, -jnp.inf)
    big_g = to_head_rows(scan(g8, jnp.add, 0.0, reverse=True) - g8) + g8
    mg8 = jnp.maximum(scan(big_g, jnp.maximum, -jnp.inf), scan(big_g, jnp.maximum, -jnp.inf, reverse=True))
    return jnp.concatenate([f8, d8, cm8, mg8, jnp.exp(big_g - mg8)], axis=0)


def _proj_kernel(x_ref, g_ref, wt_ref, bg_ref, inv_ref, *refs, names, tm, pos_base, pos_mod):
    out = dict(zip(names, refs))
    cos_sc, sin_sc = refs[len(names):]
    x = x_ref[0]
    xn = (x * lax.rsqrt(jnp.mean(x * x, axis=-1, keepdims=True) + EPS) * g_ref[...]).astype(_BF16)

    row = lax.broadcasted_iota(jnp.int32, (tm, 1), 0)
    if pos_mod % tm == 0:
        tile_pos, row_pos = pos_base + (pl.program_id(1) * tm) % pos_mod, row
    else:
        assert tm % pos_mod == 0
        tile_pos, row_pos = pos_base, row % pos_mod

    @pl.when((pl.program_id(0) == 0) & (pl.program_id(1) == 0))
    def _():
        ang_row = row_pos.astype(_F32) * inv_ref[...]
        cos_sc[...] = jnp.cos(ang_row)
        sin_sc[...] = jnp.sin(ang_row)

    ang_tile = jnp.full((1, 1), tile_pos, jnp.int32).astype(_F32) * inv_ref[...]
    cos_t, sin_t = jnp.cos(ang_tile), jnp.sin(ang_tile)
    cos = cos_t * cos_sc[...] - sin_t * sin_sc[...]
    sin = sin_t * cos_sc[...] + cos_t * sin_sc[...]
    lane = lax.broadcasted_iota(jnp.int32, (1, HEAD_W), 1)
    first = (lane % DA_QK) < (DA_QK // 2)
    sin = jnp.where(first, -sin, sin)

    def rope(z):
        rot = jnp.where(first, pltpu.roll(z, HEAD_W - DA_QK // 2, 1), pltpu.roll(z, DA_QK // 2, 1))
        return z * cos + rot * sin

    def group(c, width=GROUP_W):
        return _dot(xn, wt_ref[c * GROUP_W:c * GROUP_W + width, :], _NT)

    def put_q(zq):
        for h in range(DA_HEADS):
            sl = slice(h * HEAD_W, (h + 1) * HEAD_W)
            out["q"][0, :, sl] = (rope(zq[:, sl]) * (DA_QK ** -0.5 * LOG2E)).astype(_BF16)

    def put_k(zk):
        for h in range(DA_HEADS):
            sl = slice(h * HEAD_W, (h + 1) * HEAD_W)
            rk = rope(zk[:, sl])
            out["kf"][0, :, sl] = rk
            out["kb"][0, :, sl] = rk.astype(_BF16)

    def with_ones_rows(zt):
        ones = jnp.ones((VT_ROWS - DA_V, tm), _BF16)
        return jnp.concatenate(
            [part for h in range(DA_HEADS) for part in (zt[h * DA_V:(h + 1) * DA_V], ones)], axis=0)

    def put_v(zv):
        for h in range(DA_HEADS):
            out["vf"][0, pl.ds(h, tm, stride=DA_HEADS), :] = zv[:, h * DA_V:(h + 1) * DA_V]
        if "vt" in out:
            out["vt"][0, 0] = with_ones_rows(zv.T.astype(_BF16))

    def put_mq(z):
        if "mq" in out:
            out["mq"][0] = z.astype(_BF16)
        if "mqt" in out:
            out["mqt"][0, 0] = z.T.astype(_BF16)

    def put_mk(z):
        out["mk"][0] = (z * (ML_DK ** -0.5)).astype(_BF16)

    def put_mv(z):
        if "mv" in out:
            out["mv"][0] = z.astype(_BF16)
        if "mvt" in out:
            out["mvt"][0, 0] = with_ones_rows(z.T.astype(_BF16))

    def put_mo(z):
        out["og"][0] = jax.nn.sigmoid(z).astype(_BF16)

    def put_gates(z):
        zg = z + bg_ref[...]
        lane_g = lax.broadcasted_iota(jnp.int32, zg.shape, 1)
        logsig = jnp.minimum(zg, 0.0) - jnp.log(1.0 + jnp.exp(-jnp.abs(zg)))
        gates = jnp.where(lane_g < ML_HEADS, zg, logsig)
        if "gc" in out:
            out["gc"][0] = gates[:, :N_GATES]
        g8 = gates.T[:N_GATES, :]
        if "gr" in out:
            out["gr"][0] = g8
        if "gs" in out:
            out["gs"][0] = _chunk_gate_rows(g8, tm)

    sinks = [(put_q, GROUP_W), (put_k, GROUP_W), (put_v, GROUP_W), (put_mq, GROUP_W), (put_mk, GROUP_W),
             (put_mv, GROUP_W), (put_mo, GROUP_W), (put_gates, HEAD_W)]
    pairs = [(c, sinks[c:c + 2]) for c in range(0, len(sinks), 2)]
    pairs = pairs[-1:] + pairs[:-1]
    width = lambda pair: sum(w for _, w in pair)
    nxt = group(pairs[0][0], width(pairs[0][1]))
    for n, (_, pair) in enumerate(pairs):
        cur = nxt
        if n + 1 < len(pairs):
            nxt = group(pairs[n + 1][0], width(pairs[n + 1][1]))
        pair[0][0](cur[:, :GROUP_W])
        pair[1][0](cur[:, GROUP_W:])


def _proj(x3, norm1, wt, b_gates, inv_full, *, names, tm, pos_base, pos_mod):
    nb, s, _ = x3.shape
    nt = s // tm
    tok = lambda w, dt: (jax.ShapeDtypeStruct((nb, s, w), dt), pl.BlockSpec((1, tm, w), lambda b, i: (b, i, 0)))
    slab = lambda rows: (jax.ShapeDtypeStruct((nb, nt, rows, tm), _BF16),
                         pl.BlockSpec((1, 1, rows, tm), lambda b, i: (b, i, 0, 0)))
    const = lambda shape: pl.BlockSpec(shape, lambda b, i: (0,) * len(shape))
    kinds = {
        "q": tok(GROUP_W, _BF16), "kf": tok(GROUP_W, _F32), "kb": tok(GROUP_W, _BF16),
        "vf": (jax.ShapeDtypeStruct((nb, s * DA_HEADS, DA_V), _F32),
               pl.BlockSpec((1, tm * DA_HEADS, DA_V), lambda b, i: (b, i, 0))),
        "vt": slab(DA_HEADS * VT_ROWS),
        "mq": tok(GROUP_W, _BF16), "mk": tok(GROUP_W, _BF16), "mv": tok(GROUP_W, _BF16), "og": tok(GROUP_W, _BF16),
        "gc": tok(N_GATES, _F32),
        "gr": (jax.ShapeDtypeStruct((nb, N_GATES, s), _F32), pl.BlockSpec((1, N_GATES, tm), lambda b, i: (b, 0, i))),
        "gs": (jax.ShapeDtypeStruct((nb, 5 * N_GATES, s), _F32),
               pl.BlockSpec((1, 5 * N_GATES, tm), lambda b, i: (b, 0, i))),
        "mqt": slab(GROUP_W), "mvt": slab(ML_HEADS * VT_ROWS),
    }
    blocks = [_nbytes((tm, D_MODEL), _F32), _nbytes(wt.shape, _BF16)]
    blocks += [_nbytes(kinds[n][1].block_shape, kinds[n][0].dtype) for n in names]
    outs = pl.pallas_call(
        functools.partial(_proj_kernel, names=names, tm=tm, pos_base=pos_base, pos_mod=pos_mod),
        grid=(nb, nt),
        in_specs=[pl.BlockSpec((1, tm, D_MODEL), lambda b, i: (b, i, 0)), const((1, D_MODEL)),
                  const(wt.shape), const((1, HEAD_W)), const((1, HEAD_W))],
        out_specs=[kinds[n][1] for n in names], out_shape=[kinds[n][0] for n in names],
        scratch_shapes=[pltpu.VMEM((tm, HEAD_W), _F32), pltpu.VMEM((tm, HEAD_W), _F32)],
        compiler_params=_cparams(("arbitrary", "arbitrary"), blocks, 8 * _nbytes((tm, GROUP_W), _F32)), name="proj",
    )(x3, norm1, wt, b_gates, inv_full)
    return dict(zip(names, outs))


def _lambda(lam_ref, lam_init):
    lv = lam_ref[...]
    s1 = jnp.sum(lv[0:1] * lv[1:2], axis=-1, keepdims=True)
    s2 = jnp.sum(lv[2:3] * lv[3:4], axis=-1, keepdims=True)
    return jnp.exp(s1) - jnp.exp(s2) + lam_init


def _split_components(q):
    lane = lax.broadcasted_iota(jnp.int32, q.shape, 1)
    zero = jnp.zeros_like(q)
    return jnp.concatenate([jnp.where(lane < DA_QK, q, zero), jnp.where(lane >= DA_QK, q, zero)], axis=0)


def _attn_kernel(lam_ref, q_ref, k_ref, vt_ref, o_ref, acc_ref, s_ref, *, tq, tk, lam_init):
    i = pl.program_id(1)
    heads = range(DA_HEADS)
    qz = [_split_components(q_ref[0, :, h * HEAD_W:(h + 1) * HEAD_W]) for h in heads]
    acc_ref[...] = jnp.zeros_like(acc_ref)

    def scores(h, j):
        k_t = k_ref[0, pl.ds(pl.multiple_of(j * tk, tk), tk), h * HEAD_W:(h + 1) * HEAD_W]
        return _dot(k_t, qz[h], _NT)

    def step(j, ms, last):
        if last:
            kpos = j * tk + lax.broadcasted_iota(jnp.int32, (tk, 1), 0)
            qpos = i * tq + lax.broadcasted_iota(jnp.int32, (1, 2 * tq), 1) % tq
            visible = kpos < (qpos // CHUNK + 1) * CHUNK
        out = []
        s = s_ref[...]
        for h in heads:
            s_next = None
            if h + 1 < DA_HEADS:
                s_next = scores(h + 1, j)
            elif not last:
                s_next = scores(0, j + 1)
            if last:
                s = jnp.where(visible, s, NEG_BIG)
            m_new = jnp.maximum(ms[h], jnp.max(s, axis=0, keepdims=True))
            alpha = jnp.exp2(ms[h] - m_new)
            p = jnp.exp2(s - m_new).astype(_BF16)
            acc_ref[h] = alpha * acc_ref[h] + _dot(vt_ref[0, j, h * VT_ROWS:(h + 1) * VT_ROWS, :], p)
            out.append(m_new)
            s = s_next
        if not last:
            s_ref[...] = s
        return tuple(out)

    n_full = (i * tq) // tk
    s_ref[...] = scores(0, 0)
    init = tuple(jnp.full((1, 2 * tq), NEG_BIG, _F32) for _ in heads)
    ms = lax.fori_loop(0, n_full, lambda j, c: step(j, c, False), init)
    step(n_full, ms, True)

    lam = _lambda(lam_ref, lam_init)
    for h in heads:
        acc = acc_ref[h]
        o = acc[:DA_V] / acc[DA_V:DA_V + 1]
        o_ref[0, :, h * HEAD_W:(h + 1) * HEAD_W] = (o[:, :tq] - lam * o[:, tq:]).T.astype(_BF16)


def _attn_prompt(lamv, q, kb, vt, *, lam_init):
    nb, s, _ = q.shape
    nk, tk = vt.shape[1], vt.shape[3]
    tq = ATTN_TQ
    assert tk % tq == 0
    return pl.pallas_call(
        functools.partial(_attn_kernel, tq=tq, tk=tk, lam_init=lam_init),
        grid=(nb, s // tq),
        in_specs=[pl.BlockSpec((4, DA_QK), lambda b, i: (0, 0)),
                  pl.BlockSpec((1, tq, GROUP_W), lambda b, i: (b, i, 0)),
                  pl.BlockSpec((1, s, GROUP_W), lambda b, i: (b, 0, 0)),
                  pl.BlockSpec((1, nk, DA_HEADS * VT_ROWS, tk), lambda b, i: (b, 0, 0, 0))],
        out_specs=pl.BlockSpec((1, tq, GROUP_W), lambda b, i: (b, i, 0)),
        out_shape=jax.ShapeDtypeStruct((nb, s, GROUP_W), _BF16),
        scratch_shapes=[pltpu.VMEM((DA_HEADS, VT_ROWS, 2 * tq), _F32), pltpu.VMEM((tk, 2 * tq), _F32)],
        compiler_params=_cparams(
            ("parallel", "arbitrary"),
            [_nbytes((tq, GROUP_W), _BF16), _nbytes((s, GROUP_W), _BF16), _nbytes(vt.shape[1:], _BF16),
             _nbytes((tq, GROUP_W), _BF16)],
            _nbytes((DA_HEADS, VT_ROWS, 2 * tq), _F32) + 5 * _nbytes((tk, 2 * tq), _F32)), name="attn",
    )(lamv, q, kb, vt)


def _attn_sample_kernel(lam_ref, q_ref, kn_ref, vn_ref, kc_ref, vc_ref, o_ref, *, lq, past, lam_init):
    lam = _lambda(lam_ref, lam_init)
    for h in range(DA_HEADS):
        sl = slice(h * HEAD_W, (h + 1) * HEAD_W)
        qz = _split_components(q_ref[0, :, sl])
        kct = kc_ref[0, sl, :].astype(_BF16)
        vc = vc_ref[0, pl.ds(h, past, stride=DA_HEADS), :].astype(_BF16)
        s_c = _dot(qz, kct)
        s_n = _dot(qz, kn_ref[0, :, sl], _NT)
        m = jnp.maximum(jnp.max(s_c, axis=-1, keepdims=True), jnp.max(s_n, axis=-1, keepdims=True))
        p_c = jnp.exp2(s_c - m)
        p_n = jnp.exp2(s_n - m)
        l = jnp.sum(p_c, axis=-1, keepdims=True) + jnp.sum(p_n, axis=-1, keepdims=True)
        o = (_dot(p_c.astype(_BF16), vc) + _dot(p_n.astype(_BF16), vn_ref[0, :, sl])) / l
        o_ref[0, :, sl] = (o[:lq] - lam * o[lq:]).astype(_BF16)


def _attn_sample(lamv, q, kn, vn, kct, vc, *, lam_init):
    nb, lq, _ = q.shape
    past = kct.shape[2]
    new = pl.BlockSpec((1, lq, GROUP_W), lambda b: (b, 0, 0))
    return pl.pallas_call(
        functools.partial(_attn_sample_kernel, lq=lq, past=past, lam_init=lam_init),
        grid=(nb,),
        in_specs=[pl.BlockSpec((4, DA_QK), lambda b: (0, 0)), new, new, new,
                  pl.BlockSpec((1, GROUP_W, past), lambda b: (b, 0, 0)),
                  pl.BlockSpec((1, past * DA_HEADS, DA_V), lambda b: (b, 0, 0))],
        out_specs=new, out_shape=jax.ShapeDtypeStruct((nb, lq, GROUP_W), _BF16),
        compiler_params=_cparams(
            ("parallel",), [_nbytes((GROUP_W, past), _F32), _nbytes((past * DA_HEADS, DA_V), _F32)]
            + 4 * [_nbytes((lq, GROUP_W), _F32)], 8 * _nbytes((2 * lq, past), _F32)), name="attn_sample",
    )(lamv, q, kn, vn, kct, vc)


def _mlstm_kernel(q_ref, k_ref, v_ref, gc_ref, gr_ref, c0_ref, n0_ref, m0_ref, h_ref, c_ref, n_ref, m_ref,
                  c_sc, n_sc, m_sc, *, chunk, nchunks, bb):
    j = pl.program_id(1)

    @pl.when(j == 0)
    def _():
        c_sc[...] = c0_ref[...]
        n_sc[...] = n0_ref[...]
        m_sc[...] = m0_ref[...]

    L = chunk
    r_i = lax.broadcasted_iota(jnp.int32, (L, L), 0)
    c_i = lax.broadcasted_iota(jnp.int32, (L, L), 1)
    causal = r_i >= c_i
    tril = causal.astype(_F32)
    triu = (r_i <= c_i).astype(_F32)
    hi = lax.Precision.HIGHEST

    for c, bi in [(c, bi) for c in range(nchunks) for bi in range(bb)]:
        rows = slice(c * L, (c + 1) * L)
        gcol = gc_ref[bi, rows, :]
        grow = gr_ref[bi, :, rows]
        fc_all = _dot(tril, gcol, precision=hi)
        fr_all = _dot(grow, triu, precision=hi)
        for hh in range(ML_HEADS):
            sl = slice(hh * HEAD_W, (hh + 1) * HEAD_W)
            m = m_sc[bi, hh][0:1, 0:1]
            fc = fc_all[:, ML_HEADS + hh:ML_HEADS + hh + 1]
            fr = fr_all[ML_HEADS + hh:ML_HEADS + hh + 1, :]
            igc = gcol[:, hh:hh + 1]
            igr = grow[hh:hh + 1, :]
            a = jnp.where(causal, fc - fr + igr, -jnp.inf)
            b = fc + m
            m_t = jnp.maximum(b, jnp.max(a, axis=-1, keepdims=True))
            w = jnp.exp(a - m_t)
            sc = jnp.exp(b - m_t)
            q = q_ref[bi, rows, sl]
            k = k_ref[bi, rows, sl]
            v = v_ref[bi, rows, sl]
            wqk = w * _dot(q, k, _NT)
            cmat = c_sc[bi, hh]
            nrow = n_sc[bi, hh:hh + 1, :]
            qn = jnp.sum(q.astype(_F32) * nrow, axis=-1, keepdims=True)
            num = _dot(wqk.astype(_BF16), v) + sc * _dot(q, cmat.astype(_BF16))
            den = jnp.sum(wqk, axis=-1, keepdims=True) + sc * qn
            h_ref[bi, rows, sl] = (num / jnp.maximum(jnp.abs(den), jnp.exp(-m_t))).astype(_BF16)

            fl = fc[L - 1:L, :]
            g = fl - fc + igc
            m_new = jnp.maximum(fl + m, jnp.max(g, axis=0, keepdims=True))
            decay = jnp.exp(fl + m - m_new)
            kw = k.astype(_F32) * jnp.exp(g - m_new)
            c_sc[bi, hh] = decay * cmat + _dot(kw.astype(_BF16), v, _TN)
            n_sc[bi, hh:hh + 1, :] = decay * nrow + jnp.sum(kw, axis=0, keepdims=True)
            m_sc[bi, hh] = jnp.broadcast_to(m_new, m_sc.shape[2:])

    @pl.when(j == pl.num_programs(1) - 1)
    def _():
        c_ref[...] = c_sc[...]
        n_ref[...] = n_sc[...]
        m_ref[...] = m_sc[...]


def _mlstm(q, k, v, gc, gr, c0, n0, m0, *, chunk, block, bb):
    nb, s, _ = q.shape
    assert nb % bb == 0
    tok = lambda w: pl.BlockSpec((bb, block, w), lambda b, j: (b, j, 0))
    cspec = pl.BlockSpec((bb, ML_HEADS, ML_DK, ML_DV), lambda b, j: (b, 0, 0, 0))
    nspec = pl.BlockSpec((bb, ML_HEADS, ML_DK), lambda b, j: (b, 0, 0))
    mspec = pl.BlockSpec((bb, ML_HEADS, SUBLANES, HEAD_W), lambda b, j: (b, 0, 0, 0))
    return pl.pallas_call(
        functools.partial(_mlstm_kernel, chunk=chunk, nchunks=block // chunk, bb=bb),
        grid=(nb // bb, s // block),
        in_specs=[tok(GROUP_W), tok(GROUP_W), tok(GROUP_W), tok(N_GATES),
                  pl.BlockSpec((bb, N_GATES, block), lambda b, j: (b, 0, j)), cspec, nspec, mspec],
        out_specs=[tok(GROUP_W), cspec, nspec, mspec],
        out_shape=[jax.ShapeDtypeStruct((nb, s, GROUP_W), _BF16), jax.ShapeDtypeStruct(c0.shape, _F32),
                   jax.ShapeDtypeStruct(n0.shape, _F32), jax.ShapeDtypeStruct(m0.shape, _F32)],
        scratch_shapes=[pltpu.VMEM((bb, ML_HEADS, ML_DK, ML_DV), _F32), pltpu.VMEM((bb, ML_HEADS, ML_DK), _F32),
                        pltpu.VMEM((bb, ML_HEADS, SUBLANES, HEAD_W), _F32)],
        compiler_params=_cparams(
            ("parallel", "arbitrary"), 5 * [_nbytes((bb, block, GROUP_W), _F32)] + 2 * [bb * _nbytes(c0.shape[1:], _F32)],
            2 * bb * _nbytes(c0.shape[1:], _F32)), name="mlstm",
    )(q, k, v, gc, gr, c0, n0, m0)


def _mlstm_pairs_kernel(k_ref, qt_ref, vt_ref, gs_ref, st0_ref, m0_ref, h_ref, st_ref, m_ref, st_sc, m_sc,
                        *, block):
    j = pl.program_id(1)

    @pl.when(j == 0)
    def _():
        st_sc[...] = st0_ref[0]
        m_sc[...] = m0_ref[0]

    L, W = CHUNK, 2 * CHUNK
    rows8 = lambda n: gs_ref[0, n * N_GATES:(n + 1) * N_GATES, :]
    f8, d8, cm8, mg8, ws8 = (rows8(n) for n in range(5))
    d_cols = d8.T

    s_i = lax.broadcasted_iota(jnp.int32, (W, W), 0)
    t_i = lax.broadcasted_iota(jnp.int32, (W, W), 1)
    allowed = (s_i // L == t_i // L) & (s_i <= t_i)
    lane_w = lax.broadcasted_iota(jnp.int32, (1, W), 1)
    first = lane_w < L

    sts = [st_sc[hh] for hh in range(ML_HEADS)]
    ms = [m_sc[hh][0:1, 0:1] for hh in range(ML_HEADS)]

    def independent(p, hh):
        lanes = slice(p * W, (p + 1) * W)
        feat = slice(hh * HEAD_W, (hh + 1) * HEAD_W)
        kp = k_ref[0, lanes, feat]
        qt = qt_ref[0, 0, feat, lanes]
        v1t = vt_ref[0, 0, hh * VT_ROWS:(hh + 1) * VT_ROWS, lanes]
        ws = ws8[hh:hh + 1, lanes]
        v1f = v1t.astype(_F32)
        return dict(
            p=p, hh=hh, qt=qt, v1t=v1t, qk_t=_dot(kp, qt),
            upd_a=_dot((v1f * jnp.where(first, ws, 0.0)).astype(_BF16), kp),
            upd_b=_dot((v1f * jnp.where(first, 0.0, ws)).astype(_BF16), kp))

    def dependent(u):
        p, hh, qt = u["p"], u["hh"], u["qt"]
        lanes = slice(p * W, (p + 1) * W)
        row = lambda x: x[hh:hh + 1, lanes]
        cm, f_row = row(cm8), row(f8)
        w_t = jnp.exp(jnp.where(allowed, d_cols[lanes, hh:hh + 1] - cm, -jnp.inf))
        intra = _dot(u["v1t"], (w_t * u["qk_t"]).astype(_BF16))
        st_a, m_a = sts[hh], ms[hh]
        fl_a, mg_a = f_row[:, L - 1:L], row(mg8)[:, 0:1]
        fl_b, mg_b = f_row[:, W - 1:W], row(mg8)[:, L:L + 1]
        m_b = jnp.maximum(fl_a + m_a, mg_a)
        st_b = jnp.exp(fl_a + m_a - m_b) * st_a + jnp.exp(mg_a - m_b) * u["upd_a"]
        m_c = jnp.maximum(fl_b + m_b, mg_b)
        sts[hh] = jnp.exp(fl_b + m_b - m_c) * st_b + jnp.exp(mg_b - m_c) * u["upd_b"]
        ms[hh] = m_c
        inter = jnp.where(first, _dot(st_a.astype(_BF16), qt), _dot(st_b.astype(_BF16), qt))
        m_prev = jnp.where(first, m_a, m_b)
        mt = jnp.maximum(m_prev, cm)
        tot = jnp.exp(cm - mt) * intra + jnp.exp(m_prev - mt) * inter
        den = jnp.maximum(jnp.abs(tot[ML_DV:ML_DV + 1]), jnp.exp(-(f_row + mt)))
        h_ref[0, lanes, hh * HEAD_W:(hh + 1) * HEAD_W] = (tot[:ML_DV] / den).T.astype(_BF16)

    units = [(p, hh) for p in range(block // W) for hh in range(ML_HEADS)]
    nxt = independent(*units[0])
    for n in range(len(units)):
        cur = nxt
        if n + 1 < len(units):
            nxt = independent(*units[n + 1])
        dependent(cur)
    for hh in range(ML_HEADS):
        st_sc[hh] = sts[hh]
        m_sc[hh] = jnp.broadcast_to(ms[hh], m_sc.shape[1:])

    @pl.when(j == pl.num_programs(1) - 1)
    def _():
        st_ref[0] = st_sc[...]
        m_ref[0] = m_sc[...]


def _mlstm_pairs(k, qt, vt, gs, st0, m0):
    nb, s, _ = k.shape
    block = qt.shape[3]
    nt = s // block
    state = pl.BlockSpec((1, ML_HEADS, VT_ROWS, ML_DK), lambda b, j: (b, 0, 0, 0))
    mspec = pl.BlockSpec((1, ML_HEADS, SUBLANES, HEAD_W), lambda b, j: (b, 0, 0, 0))
    tok = pl.BlockSpec((1, block, GROUP_W), lambda b, j: (b, j, 0))
    return pl.pallas_call(
        functools.partial(_mlstm_pairs_kernel, block=block),
        grid=(nb, nt),
        in_specs=[tok, pl.BlockSpec((1, 1, GROUP_W, block), lambda b, j: (b, j, 0, 0)),
                  pl.BlockSpec((1, 1, ML_HEADS * VT_ROWS, block), lambda b, j: (b, j, 0, 0)),
                  pl.BlockSpec((1, 5 * N_GATES, block), lambda b, j: (b, 0, j)), state, mspec],
        out_specs=[tok, state, mspec],
        out_shape=[jax.ShapeDtypeStruct((nb, s, GROUP_W), _BF16), jax.ShapeDtypeStruct(st0.shape, _F32),
                   jax.ShapeDtypeStruct(m0.shape, _F32)],
        scratch_shapes=[pltpu.VMEM((ML_HEADS, VT_ROWS, ML_DK), _F32), pltpu.VMEM((ML_HEADS, SUBLANES, HEAD_W), _F32)],
        compiler_params=_cparams(
            ("parallel", "arbitrary"),
            3 * [_nbytes((block, GROUP_W), _BF16)] + [_nbytes((ML_HEADS * VT_ROWS, block), _BF16)]
            + 2 * [_nbytes(st0.shape[1:], _F32)], 16 * _nbytes((VT_ROWS, 2 * CHUNK), _F32) * ML_HEADS), name="mlstm_pairs",
    )(k, qt, vt, gs, st0, m0)


def _mix_kernel(da_ref, ml_ref, og_ref, x_ref, subln_ref, wo_ref, g2_ref, wr_ref, br_ref, cnt0_ref,
                xmid_ref, xp_ref, idx_ref, gate_ref, rank_ref, cnt_ref, cnt_sc, upper_sc, *, tm, lam_init):
    step = pl.program_id(0)

    @pl.when(step == 0)
    def _():
        cnt_sc[...] = cnt0_ref[...]
        earlier = lax.broadcasted_iota(jnp.int32, (tm, tm), 0) < lax.broadcasted_iota(jnp.int32, (tm, tm), 1)
        upper_sc[...] = earlier.astype(_BF16)

    parts = []
    for h in range(DA_HEADS):
        d = da_ref[:, h * HEAD_W:(h + 1) * HEAD_W].astype(_F32)
        d = d * lax.rsqrt(jnp.mean(d * d, axis=-1, keepdims=True) + EPS) * subln_ref[...]
        parts.append((d * (1.0 - lam_init)).astype(_BF16))
    parts.append((og_ref[...].astype(_F32) * ml_ref[...].astype(_F32)).astype(_BF16))
    xm = x_ref[...] + _dot(jnp.concatenate(parts, axis=1), wo_ref[...])
    xmid_ref[...] = xm
    xn_f = xm * lax.rsqrt(jnp.mean(xm * xm, axis=-1, keepdims=True) + EPS) * g2_ref[...]
    xp_ref[...] = _pack_rows(xn_f)
    xn = xn_f.astype(_BF16)

    logits = _dot(wr_ref[...], xn, _NT) + br_ref[...]
    e_i = lax.broadcasted_iota(jnp.int32, logits.shape, 0)
    member = jnp.zeros(logits.shape, jnp.bool_)
    vals, idxs = [], []
    for _ in range(TOP_K):
        mx = jnp.max(logits, axis=0, keepdims=True)
        sel = jnp.min(jnp.where(logits == mx, e_i, N_EXPERTS), axis=0, keepdims=True)
        hit = e_i == sel
        member = member | hit
        logits = jnp.where(hit, -jnp.inf, logits)
        vals.append(mx)
        idxs.append(sel)
    ex = [jnp.exp(v - vals[0]) for v in vals]
    tot = ex[0] + ex[1] + ex[2] + ex[3]
    idx_ref[...] = jnp.concatenate(idxs, axis=0)
    gate_ref[...] = jnp.concatenate([e / tot for e in ex], axis=0)

    memf = member.astype(_F32)
    before = _dot(memf.astype(_BF16), upper_sc[...]) + cnt_sc[...]
    rank_ref[...] = jnp.concatenate(
        [jnp.sum(jnp.where(e_i == s, before, 0.0), axis=0, keepdims=True) for s in idxs], axis=0).astype(jnp.int32)
    cnt_sc[...] += jnp.sum(memf, axis=1, keepdims=True)
    cnt_ref[...] = cnt_sc[...]


def _mix(da, ml, og, x, subln, w_out, norm2, w_rt, b_r, cnt0, *, tm, lam_init):
    t = x.shape[0]
    tok = lambda w: pl.BlockSpec((tm, w), lambda i: (i, 0))
    const = lambda shape: pl.BlockSpec(shape, lambda i: (0,) * len(shape))
    lane_tok = pl.BlockSpec((TOP_K, tm), lambda i: (0, i))
    return pl.pallas_call(
        functools.partial(_mix_kernel, tm=tm, lam_init=lam_init),
        grid=(t // tm,),
        in_specs=[tok(GROUP_W), tok(GROUP_W), tok(GROUP_W), tok(D_MODEL), const((1, DA_V)),
                  const((D_MODEL, D_MODEL)), const((1, D_MODEL)), const((N_EXPERTS, D_MODEL)),
                  const((N_EXPERTS, 1)), const((N_EXPERTS, 1))],
        out_specs=[tok(D_MODEL), tok(PACK_W), lane_tok, lane_tok, lane_tok, const((N_EXPERTS, 1))],
        out_shape=[jax.ShapeDtypeStruct((t, D_MODEL), _F32), jax.ShapeDtypeStruct((t, PACK_W), jnp.int32),
                   jax.ShapeDtypeStruct((TOP_K, t), jnp.int32), jax.ShapeDtypeStruct((TOP_K, t), _F32),
                   jax.ShapeDtypeStruct((TOP_K, t), jnp.int32), jax.ShapeDtypeStruct((N_EXPERTS, 1), _F32)],
        scratch_shapes=[pltpu.VMEM((N_EXPERTS, 1), _F32), pltpu.VMEM((tm, tm), _BF16)],
        compiler_params=_cparams(
            ("arbitrary",),
            3 * [_nbytes((tm, GROUP_W), _BF16)] + 2 * [_nbytes((tm, D_MODEL), _F32)] + [_nbytes((tm, PACK_W), _F32)]
            + [_nbytes((D_MODEL, D_MODEL), _BF16)], 4 * _nbytes((tm, D_MODEL), _F32) + 2 * _nbytes((tm, tm), _F32)), name="mix",
    )(da, ml, og, x, subln, w_out, norm2, w_rt, b_r, cnt0)


def _moe_kernel(be_ref, nu_ref, nv_ref, slot_ref, nxt_ref, x_ref, wgu_hbm, bgu_ref, wd_hbm, bd_ref, y_ref,
                wgu_f32, wd_f32, wgu_sc, wd_sc, sems):
    j = pl.program_id(0)
    used = j < nu_ref[0]
    fresh = (j == 0) | (be_ref[j] != be_ref[jnp.maximum(j - 1, 0)])
    slot = slot_ref[j]

    def fetch(e, s):
        return (pltpu.make_async_copy(wgu_hbm.at[e], wgu_f32.at[s], sems.at[s, 0]),
                pltpu.make_async_copy(wd_hbm.at[e], wd_f32.at[s], sems.at[s, 1]))

    @pl.when(j == 0)
    def _():
        for cp in fetch(be_ref[0], slot):
            cp.start()

    @pl.when(fresh & used)
    def _():
        for cp in fetch(be_ref[j], slot):
            cp.wait()

        @pl.when(nxt_ref[j] >= 0)
        def _():
            for cp in fetch(nxt_ref[j], 1 - slot):
                cp.start()

        rows = CAST_ROWS
        def cast_gu(r, _):
            sl = pl.ds(pl.multiple_of(r * rows, rows), rows)
            wgu_sc[sl, :] = wgu_f32[slot, sl, :].astype(_BF16)
            return 0
        lax.fori_loop(0, D_MODEL // rows, cast_gu, 0)
        def cast_d(r, _):
            sl = pl.ds(pl.multiple_of(r * rows, rows), rows)
            wd_sc[sl, :] = wd_f32[slot, sl, :].astype(_BF16)
            return 0
        lax.fori_loop(0, D_FF // rows, cast_d, 0)

    @pl.when(j >= nu_ref[0])
    def _():
        y_ref[...] = jnp.zeros_like(y_ref)

    def ffn(row0):
        hi, lo = _unpack_rows(x_ref[row0:, :])
        x = jnp.concatenate([hi.astype(_BF16), lo.astype(_BF16)], axis=1)
        n_chunks = D_FF // FF_CHUNK

        def gate_up(c):
            lo = c * FF_CHUNK
            return (_dot(x, wgu_sc[:, lo:lo + FF_CHUNK]), _dot(x, wgu_sc[:, D_FF + lo:D_FF + lo + FF_CHUNK]))

        acc = None
        nxt = gate_up(0)
        for c in range(n_chunks):
            lo = c * FF_CHUNK
            gate, up = nxt
            if c + 1 < n_chunks:
                nxt = gate_up(c + 1)
            gate = jnp.minimum(gate + bgu_ref[0, :, lo:lo + FF_CHUNK], SWIGLU_LIMIT)
            up = jnp.clip(up + bgu_ref[0, :, D_FF + lo:D_FF + lo + FF_CHUNK], -SWIGLU_LIMIT, SWIGLU_LIMIT)
            act = (up + 1.0) * (gate * jax.nn.sigmoid(SWIGLU_ALPHA * gate))
            down = _dot(act.astype(_BF16), wd_sc[lo:lo + FF_CHUNK, :])
            acc = down if acc is None else acc + down
        y_ref[row0:, :] = _pack_rows(acc + bd_ref[0])

    half = MOE_BLOCK // 2
    @pl.when(used & (nv_ref[j] > half))
    def _():
        ffn(0)

    @pl.when(used & (nv_ref[j] <= half))
    def _():
        y_ref[:half, :] = jnp.zeros((half, PACK_W), jnp.int32)
        ffn(half)


def _moe(blk_e, n_used, blk_rows, blk_slot, blk_next, xb, w_gu, b_gu, w_d, b_d):
    n_blocks = xb.shape[0] // MOE_BLOCK
    last = lambda j, nu: jnp.minimum(j, nu[0] - 1)
    row_spec = pl.BlockSpec((MOE_BLOCK, PACK_W), lambda j, be, nu, nv, sl, nx: (last(j, nu), 0))
    bias_spec = lambda c: pl.BlockSpec((1, 1, c), lambda j, be, nu, nv, sl, nx: (be[last(j, nu)], 0, 0))
    hbm = pl.BlockSpec(memory_space=pl.ANY)
    grid_spec = pltpu.PrefetchScalarGridSpec(
        num_scalar_prefetch=5, grid=(n_blocks,),
        in_specs=[row_spec, hbm, bias_spec(2 * D_FF), hbm, bias_spec(D_MODEL)],
        out_specs=pl.BlockSpec((MOE_BLOCK, PACK_W), lambda j, be, nu, nv, sl, nx: (j, 0)),
        scratch_shapes=[pltpu.VMEM((2, D_MODEL, 2 * D_FF), _F32), pltpu.VMEM((2, D_FF, D_MODEL), _F32),
                        pltpu.VMEM((D_MODEL, 2 * D_FF), _BF16), pltpu.VMEM((D_FF, D_MODEL), _BF16),
                        pltpu.SemaphoreType.DMA((2, 2))])
    return pl.pallas_call(
        _moe_kernel, grid_spec=grid_spec, out_shape=jax.ShapeDtypeStruct(xb.shape, jnp.int32),
        compiler_params=_cparams(
            ("arbitrary",), 2 * [_nbytes((MOE_BLOCK, PACK_W), _F32)],
            2 * _nbytes((D_MODEL, 3 * D_FF), _F32) + _nbytes((D_MODEL, 3 * D_FF), _BF16)
            + 4 * _nbytes((MOE_BLOCK, D_MODEL), _F32)), name="moe",
    )(blk_e, n_used, blk_rows, blk_slot, blk_next, xb, w_gu, b_gu, w_d, b_d)


def _sc_mesh():
    return plsc.VectorSubcoreMesh(core_axis_name="c", subcore_axis_name="s")


def _sc_worker():
    return lax.axis_index("s") * SC_CORES + lax.axis_index("c")


def _sc_dispatch(sources, n_rows):
    n_src = len(sources)
    shapes = []
    for _, dest3 in sources:
        n_chunks, _, chunk = dest3.shape
        per_worker = n_chunks // SC_WORKERS
        assert per_worker * SC_WORKERS == n_chunks and chunk % SUBLANES == 0
        shapes.append((per_worker, chunk))

    def body(*refs):
        x_hbms, d_hbms, o_hbm = refs[:n_src], refs[n_src:2 * n_src], refs[2 * n_src]
        scratch = refs[2 * n_src + 1:]
        load_a, load_b, scat_a, scat_b = scratch[3 * n_src:]
        worker = _sc_worker()
        for i, (per_worker, chunk) in enumerate(shapes):
            x_hbm, idx_v, rows_a, rows_b = x_hbms[i], scratch[3 * i], scratch[3 * i + 1], scratch[3 * i + 2]
            pltpu.sync_copy(d_hbms[i].at[worker], idx_v)

            def load(ci, rows, sem):
                row0 = (worker * per_worker + ci) * chunk
                return pltpu.make_async_copy(x_hbm.at[pl.ds(pl.multiple_of(row0, SUBLANES), chunk)], rows, sem)

            def scatters(ci, rows, sem):
                return [pltpu.make_async_copy(rows, o_hbm.at[idx_v.at[ci * TOP_K + k]], sem) for k in range(TOP_K)]

            def start(copies):
                for cp in copies:
                    cp.start()

            def wait(copies):
                for cp in copies:
                    cp.wait()

            load(0, rows_a, load_a).start()

            @pl.loop(0, per_worker // 2)
            def _(p):
                ca, cb = 2 * p, 2 * p + 1
                load(ca, rows_a, load_a).wait()
                load(cb, rows_b, load_b).start()
                start(scatters(ca, rows_a, scat_a))
                load(cb, rows_b, load_b).wait()
                wait(scatters(ca, rows_a, scat_a))
                load(jnp.minimum(ca + 2, per_worker - 1), rows_a, load_a).start()
                start(scatters(cb, rows_b, scat_b))
                wait(scatters(cb, rows_b, scat_b))

            last = per_worker - 1
            load(last, rows_a, load_a).wait()
            if per_worker % 2:
                start(scatters(last, rows_a, scat_a))
                wait(scatters(last, rows_a, scat_a))

    dma = pltpu.SemaphoreType.DMA
    scratch_types = []
    for per_worker, chunk in shapes:
        scratch_types += [pltpu.VMEM((per_worker * TOP_K, chunk), jnp.int32), pltpu.VMEM((chunk, PACK_W), jnp.int32),
                          pltpu.VMEM((chunk, PACK_W), jnp.int32)]
    tables = [dest3.reshape(SC_WORKERS, pw * TOP_K, chunk) for (_, dest3), (pw, chunk) in zip(sources, shapes)]
    return pl.kernel(
        body, out_type=jax.ShapeDtypeStruct((n_rows, PACK_W), jnp.int32), mesh=_sc_mesh(),
        scratch_types=scratch_types + [dma, dma, dma, dma],
        name="sc_dispatch")(*[xp for xp, _ in sources], *tables)


def _sc_gather(table, idx2):
    n_chunks, chunk = idx2.shape
    per_worker = n_chunks // SC_WORKERS
    assert per_worker * SC_WORKERS == n_chunks and chunk % SUBLANES == 0

    def body(t_hbm, i_hbm, o_hbm, idx_v, rows_a, rows_b, gath_a, gath_b, put_a, put_b):
        worker = _sc_worker()
        pltpu.sync_copy(i_hbm.at[worker], idx_v)

        def gather(ci, rows, sem):
            return pltpu.make_async_copy(t_hbm.at[idx_v.at[ci]], rows, sem)

        def put(ci, rows, sem):
            row0 = (worker * per_worker + ci) * chunk
            return pltpu.make_async_copy(rows, o_hbm.at[pl.ds(pl.multiple_of(row0, SUBLANES), chunk)], sem)

        gather(0, rows_a, gath_a).start()

        @pl.loop(0, per_worker // 2)
        def _(p):
            ca, cb = 2 * p, 2 * p + 1
            gather(ca, rows_a, gath_a).wait()
            gather(cb, rows_b, gath_b).start()
            put(ca, rows_a, put_a).start()
            gather(cb, rows_b, gath_b).wait()
            put(ca, rows_a, put_a).wait()
            gather(jnp.minimum(ca + 2, per_worker - 1), rows_a, gath_a).start()
            put(cb, rows_b, put_b).start()
            put(cb, rows_b, put_b).wait()

        last = per_worker - 1
        gather(last, rows_a, gath_a).wait()
        if per_worker % 2:
            put(last, rows_a, put_a).start()
            put(last, rows_a, put_a).wait()

    dma = pltpu.SemaphoreType.DMA
    return pl.kernel(
        body, out_type=jax.ShapeDtypeStruct((n_chunks * chunk, PACK_W), jnp.int32), mesh=_sc_mesh(),
        scratch_types=[pltpu.VMEM((per_worker, chunk), jnp.int32), pltpu.VMEM((chunk, PACK_W), jnp.int32),
                       pltpu.VMEM((chunk, PACK_W), jnp.int32), dma, dma, dma, dma],
        name="sc_gather")(table, idx2.reshape(SC_WORKERS, per_worker, chunk))


def _final_kernel(x_ref, y_ref, g_ref, nf_ref, o_ref):
    g_rows = g_ref[...]
    g = jnp.concatenate([g_rows, jnp.zeros((SUBLANES - TOP_K, g_rows.shape[1]), _F32)], axis=0).T
    hi = jnp.zeros((x_ref.shape[0], PACK_W), _F32)
    lo = jnp.zeros((x_ref.shape[0], PACK_W), _F32)
    for k in range(TOP_K):
        y_hi, y_lo = _unpack_rows(y_ref[k])
        hi = hi + g[:, k:k + 1] * y_hi
        lo = lo + g[:, k:k + 1] * y_lo
    x = x_ref[...] + jnp.concatenate([hi, lo], axis=1)
    o_ref[...] = x * lax.rsqrt(jnp.mean(x * x, axis=-1, keepdims=True) + EPS) * nf_ref[...]


def _final(xmid, yg, gates, norm_f, *, tm, first_block):
    t = xmid.shape[0]
    return pl.pallas_call(
        _final_kernel, grid=(t // tm,),
        in_specs=[pl.BlockSpec((tm, D_MODEL), lambda i: (i, 0)),
                  pl.BlockSpec((TOP_K, tm, PACK_W), lambda i: (0, i + first_block, 0)),
                  pl.BlockSpec((TOP_K, tm), lambda i: (0, i)),
                  pl.BlockSpec((1, D_MODEL), lambda i: (0, 0))],
        out_specs=pl.BlockSpec((tm, D_MODEL), lambda i: (i, 0)),
        out_shape=jax.ShapeDtypeStruct((t, D_MODEL), _F32),
        compiler_params=_cparams(
            ("parallel",), 2 * [_nbytes((tm, D_MODEL), _F32)] + [_nbytes((TOP_K, tm, PACK_W), _F32)],
            2 * _nbytes((tm, D_MODEL), _F32)), name="final",
    )(xmid, yg, gates, norm_f)


def kernel(x_prompt, x_sample, cache_k, cache_v, state_C, state_n, state_m, norm1, w_in, b_igate, b_fgate,
           lambda_q1, lambda_k1, lambda_q2, lambda_k2, subln, w_out, norm2, w_router, b_router, w_gate_up,
           b_gate_up, w_down, b_down, norm_f):
    nb, s, _ = x_prompt.shape
    db, ls, _ = x_sample.shape
    past = cache_k.shape[2]
    depth = w_in.shape[0]
    assert depth == 1 and s % PROJ_TILE == 0 and (db * ls) % SUBLANES == 0
    l = 0
    lam_init = 0.8 - 0.6 * math.exp(-0.3 * l)
    t_p, t_s = nb * s, db * ls

    w_nat = jnp.pad(w_in[l].T, ((0, HEAD_W - N_GATES), (0, 0))).astype(_BF16)
    b_gates = jnp.pad(jnp.concatenate([b_igate[l], b_fgate[l]]).astype(_F32), (0, HEAD_W - N_GATES))[None]
    inv = ROPE_THETA ** (-jnp.arange(0, DA_QK, 2, dtype=_F32) / DA_QK)
    inv_full = jnp.tile(inv, HEAD_W // (DA_QK // 2))[None]
    lamv = jnp.stack([lambda_q1[l], lambda_k1[l], lambda_q2[l], lambda_k2[l]]).astype(_F32)
    g1 = norm1[l][None]

    pp = _proj(x_prompt, g1, w_nat, b_gates, inv_full, tm=PROJ_TILE, pos_base=0, pos_mod=s,
               names=("q", "kf", "kb", "vf", "vt", "mk", "og", "gs", "mqt", "mvt"))
    ps = _proj(x_sample.reshape(1, t_s, D_MODEL), g1, w_nat, b_gates, inv_full, tm=t_s, pos_base=past, pos_mod=ls,
               names=("q", "kf", "kb", "vf", "mq", "mk", "mv", "og", "gc", "gr"))
    kf_p, vf_p, kf_s, vf_s = pp["kf"], pp["vf"], ps["kf"], ps["vf"]

    da_p = _attn_prompt(lamv, pp["q"], pp["kb"], pp["vt"], lam_init=lam_init)
    per_b = lambda a: a.reshape(db, ls, a.shape[-1])
    kct = jnp.transpose(cache_k[l], (0, 2, 3, 4, 1)).reshape(db, GROUP_W, past)
    vc = cache_v[l].reshape(db, past * DA_HEADS, DA_V)
    vn = vf_s.reshape(db, ls, GROUP_W).astype(_BF16)
    da_s = _attn_sample(lamv, per_b(ps["q"]), per_b(ps["kb"]), vn, kct, vc, lam_init=lam_init)

    assert PROJ_TILE % (2 * CHUNK) == 0
    h_p, st_p, m_p = _mlstm_pairs(pp["mk"], pp["mqt"], pp["mvt"], pp["gs"],
                                  jnp.zeros((nb, ML_HEADS, VT_ROWS, ML_DK), _F32),
                                  jnp.zeros((nb, ML_HEADS, SUBLANES, HEAD_W), _F32))
    gr_sb = ps["gr"].reshape(8, db, ls).transpose(1, 0, 2)
    m0_s = jnp.broadcast_to(state_m[l].astype(_F32)[:, :, None, None], (db, ML_HEADS, SUBLANES, HEAD_W))
    h_s, c_s, n_s, m_s = _mlstm(per_b(ps["mq"]), per_b(ps["mk"]), per_b(ps["mv"]), per_b(ps["gc"]), gr_sb,
                                state_C[l].astype(_F32), state_n[l].astype(_F32), m0_s, chunk=ls, block=ls,
                                bb=MLSTM_SAMPLE_ROWS)
    c_p, n_p, mm_p = jnp.swapaxes(st_p[:, :, :ML_DV, :], -1, -2), st_p[:, :, ML_DV, :], m_p[:, :, 0, 0]
    mm_s = m_s[:, :, 0, 0]

    wo = w_out[l].astype(_BF16)
    w_rt = w_router[l].T.astype(_BF16)
    b_r = b_router[l].astype(_F32)[:, None]
    mix = functools.partial(_mix, subln=subln[l][None], w_out=wo, norm2=norm2[l][None], w_rt=w_rt, b_r=b_r,
                            lam_init=lam_init)
    flat = lambda a: a.reshape(-1, a.shape[-1])
    t_all = t_p + t_s
    assert t_p % t_s == 0
    xm_p, xp_p, idx_p, gate_p, rank_p, cnt_p = mix(
        flat(da_p), flat(h_p), flat(pp["og"]), flat(x_prompt), cnt0=jnp.zeros((N_EXPERTS, 1), _F32), tm=PROJ_TILE)
    xm_s, xp_s, idx_s, gate_s, rank_s, cnt = mix(
        flat(da_s), flat(h_s), flat(ps["og"]), flat(x_sample), cnt0=cnt_p, tm=t_s)

    rows = t_all * TOP_K
    n_blocks = -(-rows // MOE_BLOCK) + N_EXPERTS
    counts = cnt[:, 0].astype(jnp.int32)
    padded = (counts + MOE_BLOCK - 1) // MOE_BLOCK * MOE_BLOCK
    pend = jnp.cumsum(padded)
    pstart = pend - counts
    experts = jnp.arange(N_EXPERTS, dtype=jnp.int32)[:, None, None]
    first_row = lambda idx: jnp.sum(jnp.where(idx[None] == experts, pstart[:, None, None], 0), axis=0)
    dest_p, dest_s = first_row(idx_p) + rank_p, first_row(idx_s) + rank_s
    n_used = (pend[-1] // MOE_BLOCK).astype(jnp.int32)[None]
    blk_lo = jnp.arange(n_blocks, dtype=jnp.int32) * MOE_BLOCK
    blk_e = jnp.minimum(jnp.sum((pend[None, :] <= blk_lo[:, None]).astype(jnp.int32), axis=1), N_EXPERTS - 1)
    row_lo = jnp.sum(jnp.where(blk_e[:, None] == experts[:, 0].T, pstart[None, :], 0), axis=1)
    blk_rows = jnp.where(blk_lo < pend[-1], jnp.clip(blk_lo + MOE_BLOCK - row_lo, 0, MOE_BLOCK), 0).astype(jnp.int32)

    def chunked(dest):
        chunk = min(DISPATCH_CHUNK, dest.shape[1] // SC_WORKERS)
        return dest.reshape(TOP_K, dest.shape[1] // chunk, chunk).transpose(1, 0, 2)

    assert rows % (SC_WORKERS * COMBINE_CHUNK) == 0
    xb = _sc_dispatch([(xp_p, chunked(dest_p)), (xp_s, chunked(dest_s))], n_blocks * MOE_BLOCK)
    dest = jnp.concatenate([dest_p, dest_s], axis=1)
    first_blk = jnp.concatenate([jnp.ones((1,), jnp.bool_), blk_e[1:] != blk_e[:-1]])
    blk_slot = ((jnp.cumsum(first_blk.astype(jnp.int32)) - 1) % 2).astype(jnp.int32)
    after = jnp.sum(jnp.where(blk_e[:, None] == experts[:, 0].T, pend[None, :], 0), axis=1) // MOE_BLOCK
    blk_next = jnp.where(after < n_used[0], blk_e[jnp.minimum(after, n_blocks - 1)], -1).astype(jnp.int32)
    yb = _moe(blk_e, n_used, blk_rows, blk_slot, blk_next, xb, w_gate_up[l], b_gate_up[l][:, None, :], w_down[l], b_down[l][:, None, :])
    yg = _sc_gather(yb, dest.reshape(rows // COMBINE_CHUNK, COMBINE_CHUNK)).reshape(TOP_K, t_all, PACK_W)
    nf = norm_f[None]
    y_p = _final(xm_p, yg, gate_p, nf, tm=FINAL_TILE, first_block=0).reshape(nb, s, D_MODEL)
    y_s = _final(xm_s, yg, gate_s, nf, tm=t_s, first_block=t_p // t_s).reshape(db, ls, D_MODEL)

    st = lambda a: a[None]
    return (y_p, y_s,
            st(kf_p.reshape(nb, s, DA_HEADS, 2, DA_QK)), st(vf_p.reshape(nb, s, DA_HEADS, DA_V)),
            st(c_p), st(n_p), st(mm_p),
            st(kf_s.reshape(db, ls, DA_HEADS, 2, DA_QK)), st(vf_s.reshape(db, ls, DA_HEADS, DA_V)),
            st(c_s.astype(state_C.dtype)), st(n_s.astype(state_n.dtype)), st(mm_s.astype(state_m.dtype)))
```
